```python
import jax, jax.numpy as jnp
from jax import lax
import numpy as np

D_MODEL = 1024
BATCH = 8
SEQ = 8192
DEPTH = 2

CHUNK = 64
CONV_WIDTH = D_MODEL
CONV_GROUPS = 16
CONV_K = 3
LRU_WIDTH = D_MODEL
LRU_HEADS = 16
LRU_HEAD_DIM = LRU_WIDTH // LRU_HEADS
LRU_CONV_K = 4
LRU_C = 8.0
D_FF = ((-(-8 * D_MODEL // 3)) + 255) // 256 * 256
RMS_EPS = 1e-6
IN_WIDTHS = (CONV_WIDTH, CONV_WIDTH, CONV_WIDTH, LRU_WIDTH, LRU_WIDTH, D_MODEL, D_MODEL)
IN_TOTAL = sum(IN_WIDTHS)
SPLIT_POINTS = tuple(int(v) for v in np.cumsum(IN_WIDTHS)[:-1])

kernel_name = "hybrid_shortconv_rglru_gated_merge"


def rmsnorm(x, g):
    xf = x.astype(jnp.float32)
    var = jnp.mean(xf * xf, axis=-1, keepdims=True)
    return (xf * lax.rsqrt(var + RMS_EPS) * g.astype(jnp.float32)).astype(x.dtype)


def causal_depthwise_conv(x, w, b=None):
    K = w.shape[0]
    S = x.shape[1]
    xp = jnp.pad(x, ((0, 0), (K - 1, 0), (0, 0)))
    y = xp[:, 0:S] * w[0]
    for k in range(1, K):
        y = y + xp[:, k:k + S] * w[k]
    if b is not None:
        y = y + b
    return y


def rg_lru(x, w_a, b_a, w_x, b_x, lam):
    Bsz, S, W = x.shape
    f32 = jnp.float32
    xf = x.astype(f32)
    xh = xf.reshape(Bsz, S, LRU_HEADS, LRU_HEAD_DIM)
    r = jax.nn.sigmoid(jnp.einsum('bshd,hde->bshe', xh, w_a.astype(f32)).reshape(Bsz, S, W) + b_a.astype(f32))
    i = jax.nn.sigmoid(jnp.einsum('bshd,hde->bshe', xh, w_x.astype(f32)).reshape(Bsz, S, W) + b_x.astype(f32))
    log_a = -LRU_C * r * jax.nn.softplus(-lam.astype(f32))
    a = jnp.exp(log_a)
    b = jnp.sqrt(-jnp.expm1(2.0 * log_a)) * (i * xf)
    n_chunks = S // CHUNK
    a_c = a.reshape(Bsz, n_chunks, CHUNK, W)
    b_c = b.reshape(Bsz, n_chunks, CHUNK, W)

    def combine(left, right):
        a_l, b_l = left
        a_r, b_r = right
        return a_l * a_r, a_r * b_l + b_r

    a_cum, h_loc = lax.associative_scan(combine, (a_c, b_c), axis=2)

    def step(h_prev, inp):
        a_cum_k, h_loc_k = inp
        h = h_loc_k + a_cum_k * h_prev[:, None, :]
        return h[:, -1], h

    h0 = jnp.zeros((Bsz, W), f32)
    _, hs = lax.scan(step, h0, (jnp.moveaxis(a_cum, 1, 0), jnp.moveaxis(h_loc, 1, 0)))
    return jnp.moveaxis(hs, 0, 1).reshape(Bsz, S, W).astype(x.dtype)


def _fwd_setup_inputs(seed: int = 0) -> dict:
    key = jax.random.key(seed)
    ks = jax.random.split(key, 24)
    f32 = jnp.float32
    nrm = lambda k, shape, fan_in: jax.random.normal(k, shape, f32) * (fan_in ** -0.5)
    x = jax.random.normal(ks[0], (BATCH, SEQ, D_MODEL), f32)
    ln1_g = 1.0 + 0.02 * jax.random.normal(ks[1], (DEPTH, D_MODEL), f32)
    w_in = nrm(ks[2], (DEPTH, D_MODEL, IN_TOTAL), D_MODEL)
    conv_a_w = nrm(ks[3], (DEPTH, CONV_K, CONV_WIDTH), CONV_K)
    conv_b_w = nrm(ks[4], (DEPTH, LRU_CONV_K, LRU_WIDTH), LRU_CONV_K)
    conv_b_b = 0.02 * jax.random.normal(ks[5], (DEPTH, LRU_WIDTH), f32)
    lru_wa = nrm(ks[6], (DEPTH, LRU_HEADS, LRU_HEAD_DIM, LRU_HEAD_DIM), LRU_HEAD_DIM)
    lru_ba = 0.02 * jax.random.normal(ks[7], (DEPTH, LRU_WIDTH), f32)
    lru_wx = nrm(ks[8], (DEPTH, LRU_HEADS, LRU_HEAD_DIM, LRU_HEAD_DIM), LRU_HEAD_DIM)
    lru_bx = 0.02 * jax.random.normal(ks[9], (DEPTH, LRU_WIDTH), f32)
    u = jax.random.uniform(ks[10], (DEPTH, LRU_WIDTH), f32, 0.9, 0.999)
    s = u ** (1.0 / LRU_C)
    lru_lambda = jnp.log(s) - jnp.log1p(-s)
    w_out_a = nrm(ks[11], (DEPTH, CONV_WIDTH, D_MODEL), CONV_WIDTH)
    w_out_b = nrm(ks[12], (DEPTH, LRU_WIDTH, D_MODEL), LRU_WIDTH)
    gate_bias = 0.02 * jax.random.normal(ks[13], (DEPTH, 2, D_MODEL), f32)
    w_o = nrm(ks[14], (DEPTH, D_MODEL, D_MODEL), D_MODEL)
    ln2_g = 1.0 + 0.02 * jax.random.normal(ks[15], (DEPTH, D_MODEL), f32)
    w_ffn_gate = nrm(ks[16], (DEPTH, D_MODEL, D_FF), D_MODEL)
    w_ffn_up = nrm(ks[17], (DEPTH, D_MODEL, D_FF), D_MODEL)
    w_ffn_down = nrm(ks[18], (DEPTH, D_FF, D_MODEL), D_FF)
    final_g = 1.0 + 0.02 * jax.random.normal(ks[19], (D_MODEL,), f32)
    return {"x": x, "ln1_g": ln1_g, "w_in": w_in, "conv_a_w": conv_a_w,
            "conv_b_w": conv_b_w, "conv_b_b": conv_b_b, "lru_wa": lru_wa,
            "lru_ba": lru_ba, "lru_wx": lru_wx, "lru_bx": lru_bx,
            "lru_lambda": lru_lambda, "w_out_a": w_out_a, "w_out_b": w_out_b,
            "gate_bias": gate_bias, "w_o": w_o, "ln2_g": ln2_g,
            "w_ffn_gate": w_ffn_gate, "w_ffn_up": w_ffn_up,
            "w_ffn_down": w_ffn_down, "final_g": final_g}


def _fwd_reference(x, ln1_g, w_in, conv_a_w, conv_b_w, conv_b_b, lru_wa, lru_ba,
              lru_wx, lru_bx, lru_lambda, w_out_a, w_out_b, gate_bias, w_o,
              ln2_g, w_ffn_gate, w_ffn_up, w_ffn_down, final_g):
    for l in range(DEPTH):
        h = rmsnorm(x, ln1_g[l])
        proj = h @ w_in[l]
        b_a, c_a, x_a, x_b, g_b, gate_a_logit, gate_b_logit = jnp.split(proj, SPLIT_POINTS, axis=-1)
        y_a = b_a * causal_depthwise_conv(c_a * x_a, conv_a_w[l])
        u_b = causal_depthwise_conv(x_b, conv_b_w[l], conv_b_b[l])
        y_b = rg_lru(u_b, lru_wa[l], lru_ba[l], lru_wx[l], lru_bx[l], lru_lambda[l])
        y_b = y_b * jax.nn.gelu(g_b)
        merged = (jax.nn.sigmoid(gate_a_logit + gate_bias[l, 0]) * (y_a @ w_out_a[l])
                  + jax.nn.sigmoid(gate_b_logit + gate_bias[l, 1]) * (y_b @ w_out_b[l]))
        x = x + merged @ w_o[l]
        h = rmsnorm(x, ln2_g[l])
        x = x + (jax.nn.silu(h @ w_ffn_gate[l]) * (h @ w_ffn_up[l])) @ w_ffn_down[l]
    return rmsnorm(x, final_g)


import jax as _jax
import jax.numpy as _jnp

TWIN_FORMAT = 'train_step'
FWD_PARAMS = ['x', 'ln1_g', 'w_in', 'conv_a_w', 'conv_b_w', 'conv_b_b', 'lru_wa', 'lru_ba', 'lru_wx', 'lru_bx', 'lru_lambda', 'w_out_a', 'w_out_b', 'gate_bias', 'w_o', 'ln2_g', 'w_ffn_gate', 'w_ffn_up', 'w_ffn_down', 'final_g']
TWIN_WEIGHTS = ['ln1_g', 'w_in', 'conv_a_w', 'conv_b_w', 'conv_b_b', 'lru_wa', 'lru_ba', 'lru_wx', 'lru_bx', 'lru_lambda', 'w_out_a', 'w_out_b', 'gate_bias', 'w_o', 'ln2_g', 'w_ffn_gate', 'w_ffn_up', 'w_ffn_down', 'final_g']
TWIN_DIFF_INPUT = 'x'
TWIN_INPUTS = ['x', 'ln1_g', 'w_in', 'conv_a_w', 'conv_b_w', 'conv_b_b', 'lru_wa', 'lru_ba', 'lru_wx', 'lru_bx', 'lru_lambda', 'w_out_a', 'w_out_b', 'gate_bias', 'w_o', 'ln2_g', 'w_ffn_gate', 'w_ffn_up', 'w_ffn_down', 'final_g', 'loss_target', 'm_ln1_g', 'm_w_in', 'm_conv_a_w', 'm_conv_b_w', 'm_conv_b_b', 'm_lru_wa', 'm_lru_ba', 'm_lru_wx', 'm_lru_bx', 'm_lru_lambda', 'm_w_out_a', 'm_w_out_b', 'm_gate_bias', 'm_w_o', 'm_ln2_g', 'm_w_ffn_gate', 'm_w_ffn_up', 'm_w_ffn_down', 'm_final_g', 'v_ln1_g', 'v_w_in', 'v_conv_a_w', 'v_conv_b_w', 'v_conv_b_b', 'v_lru_wa', 'v_lru_ba', 'v_lru_wx', 'v_lru_bx', 'v_lru_lambda', 'v_w_out_a', 'v_w_out_b', 'v_gate_bias', 'v_w_o', 'v_ln2_g', 'v_w_ffn_gate', 'v_w_ffn_up', 'v_w_ffn_down', 'v_final_g']
TWIN_OUTPUTS = ['loss', 'grad_x', 'grad_ln1_g', 'grad_w_in', 'grad_conv_a_w', 'grad_conv_b_w', 'grad_conv_b_b', 'grad_lru_wa', 'grad_lru_ba', 'grad_lru_wx', 'grad_lru_bx', 'grad_lru_lambda', 'grad_w_out_a', 'grad_w_out_b', 'grad_gate_bias', 'grad_w_o', 'grad_ln2_g', 'grad_w_ffn_gate', 'grad_w_ffn_up', 'grad_w_ffn_down', 'grad_final_g', 'delta_ln1_g', 'delta_w_in', 'delta_conv_a_w', 'delta_conv_b_w', 'delta_conv_b_b', 'delta_lru_wa', 'delta_lru_ba', 'delta_lru_wx', 'delta_lru_bx', 'delta_lru_lambda', 'delta_w_out_a', 'delta_w_out_b', 'delta_gate_bias', 'delta_w_o', 'delta_ln2_g', 'delta_w_ffn_gate', 'delta_w_ffn_up', 'delta_w_ffn_down', 'delta_final_g', 'new_m_ln1_g', 'new_m_w_in', 'new_m_conv_a_w', 'new_m_conv_b_w', 'new_m_conv_b_b', 'new_m_lru_wa', 'new_m_lru_ba', 'new_m_lru_wx', 'new_m_lru_bx', 'new_m_lru_lambda', 'new_m_w_out_a', 'new_m_w_out_b', 'new_m_gate_bias', 'new_m_w_o', 'new_m_ln2_g', 'new_m_w_ffn_gate', 'new_m_w_ffn_up', 'new_m_w_ffn_down', 'new_m_final_g', 'new_v_ln1_g', 'new_v_w_in', 'new_v_conv_a_w', 'new_v_conv_b_w', 'new_v_conv_b_b', 'new_v_lru_wa', 'new_v_lru_ba', 'new_v_lru_wx', 'new_v_lru_bx', 'new_v_lru_lambda', 'new_v_w_out_a', 'new_v_w_out_b', 'new_v_gate_bias', 'new_v_w_o', 'new_v_ln2_g', 'new_v_w_ffn_gate', 'new_v_w_ffn_up', 'new_v_w_ffn_down', 'new_v_final_g']
TWIN_LEAF_KINDS = {'loss': 'loss', 'grad_x': 'grad_x', 'grad_ln1_g': 'grad_w', 'grad_w_in': 'grad_w', 'grad_conv_a_w': 'grad_w', 'grad_conv_b_w': 'grad_w', 'grad_conv_b_b': 'grad_w', 'grad_lru_wa': 'grad_w', 'grad_lru_ba': 'grad_w', 'grad_lru_wx': 'grad_w', 'grad_lru_bx': 'grad_w', 'grad_lru_lambda': 'grad_w', 'grad_w_out_a': 'grad_w', 'grad_w_out_b': 'grad_w', 'grad_gate_bias': 'grad_w', 'grad_w_o': 'grad_w', 'grad_ln2_g': 'grad_w', 'grad_w_ffn_gate': 'grad_w', 'grad_w_ffn_up': 'grad_w', 'grad_w_ffn_down': 'grad_w', 'grad_final_g': 'grad_w', 'delta_ln1_g': 'delta_w', 'delta_w_in': 'delta_w', 'delta_conv_a_w': 'delta_w', 'delta_conv_b_w': 'delta_w', 'delta_conv_b_b': 'delta_w', 'delta_lru_wa': 'delta_w', 'delta_lru_ba': 'delta_w', 'delta_lru_wx': 'delta_w', 'delta_lru_bx': 'delta_w', 'delta_lru_lambda': 'delta_w', 'delta_w_out_a': 'delta_w', 'delta_w_out_b': 'delta_w', 'delta_gate_bias': 'delta_w', 'delta_w_o': 'delta_w', 'delta_ln2_g': 'delta_w', 'delta_w_ffn_gate': 'delta_w', 'delta_w_ffn_up': 'delta_w', 'delta_w_ffn_down': 'delta_w', 'delta_final_g': 'delta_w', 'new_m_ln1_g': 'new_m', 'new_m_w_in': 'new_m', 'new_m_conv_a_w': 'new_m', 'new_m_conv_b_w': 'new_m', 'new_m_conv_b_b': 'new_m', 'new_m_lru_wa': 'new_m', 'new_m_lru_ba': 'new_m', 'new_m_lru_wx': 'new_m', 'new_m_lru_bx': 'new_m', 'new_m_lru_lambda': 'new_m', 'new_m_w_out_a': 'new_m', 'new_m_w_out_b': 'new_m', 'new_m_gate_bias': 'new_m', 'new_m_w_o': 'new_m', 'new_m_ln2_g': 'new_m', 'new_m_w_ffn_gate': 'new_m', 'new_m_w_ffn_up': 'new_m', 'new_m_w_ffn_down': 'new_m', 'new_m_final_g': 'new_m', 'new_v_ln1_g': 'new_v', 'new_v_w_in': 'new_v', 'new_v_conv_a_w': 'new_v', 'new_v_conv_b_w': 'new_v', 'new_v_conv_b_b': 'new_v', 'new_v_lru_wa': 'new_v', 'new_v_lru_ba': 'new_v', 'new_v_lru_wx': 'new_v', 'new_v_lru_bx': 'new_v', 'new_v_lru_lambda': 'new_v', 'new_v_w_out_a': 'new_v', 'new_v_w_out_b': 'new_v', 'new_v_gate_bias': 'new_v', 'new_v_w_o': 'new_v', 'new_v_ln2_g': 'new_v', 'new_v_w_ffn_gate': 'new_v', 'new_v_w_ffn_up': 'new_v', 'new_v_w_ffn_down': 'new_v', 'new_v_final_g': 'new_v'}


def _forward(args):
    return _fwd_reference(*[args[k] for k in FWD_PARAMS])


def _output_shape():
    def fwd():
        inp = _fwd_setup_inputs(0)
        return _fwd_reference(*[inp[k] for k in FWD_PARAMS])
    out = _jax.eval_shape(fwd)
    return out.shape, out.dtype

N_MICROBATCH = 1
ADAM_LR = 0.001
ADAM_B1 = 0.9
ADAM_B2 = 0.999
ADAM_EPS = 1e-08
ADAM_WD = 0.01
ADAM_STEP = 10
PER_EXAMPLE_BATCH_AXIS = {'x': 0, 'loss_target': 0}
SHARED_INPUTS = []
_WEIGHT_DTYPES = {'ln1_g': _jnp.float32, 'w_in': _jnp.float32, 'conv_a_w': _jnp.float32, 'conv_b_w': _jnp.float32, 'conv_b_b': _jnp.float32, 'lru_wa': _jnp.float32, 'lru_ba': _jnp.float32, 'lru_wx': _jnp.float32, 'lru_bx': _jnp.float32, 'lru_lambda': _jnp.float32, 'w_out_a': _jnp.float32, 'w_out_b': _jnp.float32, 'gate_bias': _jnp.float32, 'w_o': _jnp.float32, 'ln2_g': _jnp.float32, 'w_ffn_gate': _jnp.float32, 'w_ffn_up': _jnp.float32, 'w_ffn_down': _jnp.float32, 'final_g': _jnp.float32}
MOMENT_SCALE = {'ln1_g': 2.549475e-01, 'w_in': 9.639506e-02, 'conv_a_w': 1.369388e-01, 'conv_b_w': 7.113414e-02, 'conv_b_b': 7.898533e-01, 'lru_wa': 2.778536e-02, 'lru_ba': 2.037360e-02, 'lru_wx': 5.155412e-02, 'lru_bx': 2.372156e-02, 'lru_lambda': 3.818334e-02, 'w_out_a': 1.348016e-01, 'w_out_b': 7.575621e-02, 'gate_bias': 4.056191e-02, 'w_o': 1.511711e-01, 'ln2_g': 1.836078e-01, 'w_ffn_gate': 7.654400e-02, 'w_ffn_up': 7.409062e-02, 'w_ffn_down': 1.227967e-01, 'final_g': 6.382920e+01}


def _to_microbatches(a, axis):
    t = _jnp.moveaxis(a, axis, 0)
    t = t.reshape((N_MICROBATCH, t.shape[0] // N_MICROBATCH) + t.shape[1:])
    return _jnp.moveaxis(t, 1, axis + 1)


def setup_inputs(seed: int = 0) -> dict:
    inp = _fwd_setup_inputs(seed)
    key = _jax.random.fold_in(_jax.random.key(seed), 7919)
    shape, _ = _output_shape()
    out = dict(inp)
    out["loss_target"] = _jax.random.normal(_jax.random.fold_in(key, 0), shape, _jnp.float32)
    for i, name in enumerate(TWIN_WEIGHTS):
        w = inp[name].astype(_jnp.float32)
        if MOMENT_SCALE is None:
            s = _jnp.sqrt(_jnp.mean(_jnp.square(w)) + 1e-30)
        else:
            s = MOMENT_SCALE[name]
        km, kv = _jax.random.split(_jax.random.fold_in(key, i + 1))
        out[name] = w
        out["m_" + name] = s * _jax.random.normal(km, w.shape, _jnp.float32)
        out["v_" + name] = (s * s) * _jax.random.uniform(kv, w.shape, _jnp.float32, 0.5, 1.5)
    if N_MICROBATCH > 1:
        for name, axis in PER_EXAMPLE_BATCH_AXIS.items():
            out[name] = _to_microbatches(out[name], axis)
    return {'x': out['x'], 'ln1_g': out['ln1_g'], 'w_in': out['w_in'], 'conv_a_w': out['conv_a_w'], 'conv_b_w': out['conv_b_w'], 'conv_b_b': out['conv_b_b'], 'lru_wa': out['lru_wa'], 'lru_ba': out['lru_ba'], 'lru_wx': out['lru_wx'], 'lru_bx': out['lru_bx'], 'lru_lambda': out['lru_lambda'], 'w_out_a': out['w_out_a'], 'w_out_b': out['w_out_b'], 'gate_bias': out['gate_bias'], 'w_o': out['w_o'], 'ln2_g': out['ln2_g'], 'w_ffn_gate': out['w_ffn_gate'], 'w_ffn_up': out['w_ffn_up'], 'w_ffn_down': out['w_ffn_down'], 'final_g': out['final_g'], 'loss_target': out['loss_target'], 'm_ln1_g': out['m_ln1_g'], 'm_w_in': out['m_w_in'], 'm_conv_a_w': out['m_conv_a_w'], 'm_conv_b_w': out['m_conv_b_w'], 'm_conv_b_b': out['m_conv_b_b'], 'm_lru_wa': out['m_lru_wa'], 'm_lru_ba': out['m_lru_ba'], 'm_lru_wx': out['m_lru_wx'], 'm_lru_bx': out['m_lru_bx'], 'm_lru_lambda': out['m_lru_lambda'], 'm_w_out_a': out['m_w_out_a'], 'm_w_out_b': out['m_w_out_b'], 'm_gate_bias': out['m_gate_bias'], 'm_w_o': out['m_w_o'], 'm_ln2_g': out['m_ln2_g'], 'm_w_ffn_gate': out['m_w_ffn_gate'], 'm_w_ffn_up': out['m_w_ffn_up'], 'm_w_ffn_down': out['m_w_ffn_down'], 'm_final_g': out['m_final_g'], 'v_ln1_g': out['v_ln1_g'], 'v_w_in': out['v_w_in'], 'v_conv_a_w': out['v_conv_a_w'], 'v_conv_b_w': out['v_conv_b_w'], 'v_conv_b_b': out['v_conv_b_b'], 'v_lru_wa': out['v_lru_wa'], 'v_lru_ba': out['v_lru_ba'], 'v_lru_wx': out['v_lru_wx'], 'v_lru_bx': out['v_lru_bx'], 'v_lru_lambda': out['v_lru_lambda'], 'v_w_out_a': out['v_w_out_a'], 'v_w_out_b': out['v_w_out_b'], 'v_gate_bias': out['v_gate_bias'], 'v_w_o': out['v_w_o'], 'v_ln2_g': out['v_ln2_g'], 'v_w_ffn_gate': out['v_w_ffn_gate'], 'v_w_ffn_up': out['v_w_ffn_up'], 'v_w_ffn_down': out['v_w_ffn_down'], 'v_final_g': out['v_final_g']}


def _loss(weights, diff, rest, loss_target):
    with _jax.named_scope("forward"):
        args = {**rest, TWIN_DIFF_INPUT: diff, **{k: w.astype(_WEIGHT_DTYPES[k]) for k, w in weights.items()}}
        y = _forward(args)
    with _jax.named_scope("loss_head"):
        err = _jnp.square(y.astype(_jnp.float32) - loss_target)
        return 0.5 * _jnp.sum(_jnp.mean(err, axis=-1)) if err.ndim else 0.5 * err


def _adamw(w, g, m, v):
    m = ADAM_B1 * m + (1.0 - ADAM_B1) * g
    v = ADAM_B2 * v + (1.0 - ADAM_B2) * _jnp.square(g)
    m_hat = m / (1.0 - ADAM_B1 ** ADAM_STEP)
    v_hat = v / (1.0 - ADAM_B2 ** ADAM_STEP)
    delta = -ADAM_LR * (m_hat / (_jnp.sqrt(v_hat) + ADAM_EPS) + ADAM_WD * w)
    return delta, m, v


def reference(x, ln1_g, w_in, conv_a_w, conv_b_w, conv_b_b, lru_wa, lru_ba, lru_wx, lru_bx, lru_lambda, w_out_a, w_out_b, gate_bias, w_o, ln2_g, w_ffn_gate, w_ffn_up, w_ffn_down, final_g, loss_target, m_ln1_g, m_w_in, m_conv_a_w, m_conv_b_w, m_conv_b_b, m_lru_wa, m_lru_ba, m_lru_wx, m_lru_bx, m_lru_lambda, m_w_out_a, m_w_out_b, m_gate_bias, m_w_o, m_ln2_g, m_w_ffn_gate, m_w_ffn_up, m_w_ffn_down, m_final_g, v_ln1_g, v_w_in, v_conv_a_w, v_conv_b_w, v_conv_b_b, v_lru_wa, v_lru_ba, v_lru_wx, v_lru_bx, v_lru_lambda, v_w_out_a, v_w_out_b, v_gate_bias, v_w_o, v_ln2_g, v_w_ffn_gate, v_w_ffn_up, v_w_ffn_down, v_final_g):
    given = dict(x=x, ln1_g=ln1_g, w_in=w_in, conv_a_w=conv_a_w, conv_b_w=conv_b_w, conv_b_b=conv_b_b, lru_wa=lru_wa, lru_ba=lru_ba, lru_wx=lru_wx, lru_bx=lru_bx, lru_lambda=lru_lambda, w_out_a=w_out_a, w_out_b=w_out_b, gate_bias=gate_bias, w_o=w_o, ln2_g=ln2_g, w_ffn_gate=w_ffn_gate, w_ffn_up=w_ffn_up, w_ffn_down=w_ffn_down, final_g=final_g, loss_target=loss_target, m_ln1_g=m_ln1_g, m_w_in=m_w_in, m_conv_a_w=m_conv_a_w, m_conv_b_w=m_conv_b_w, m_conv_b_b=m_conv_b_b, m_lru_wa=m_lru_wa, m_lru_ba=m_lru_ba, m_lru_wx=m_lru_wx, m_lru_bx=m_lru_bx, m_lru_lambda=m_lru_lambda, m_w_out_a=m_w_out_a, m_w_out_b=m_w_out_b, m_gate_bias=m_gate_bias, m_w_o=m_w_o, m_ln2_g=m_ln2_g, m_w_ffn_gate=m_w_ffn_gate, m_w_ffn_up=m_w_ffn_up, m_w_ffn_down=m_w_ffn_down, m_final_g=m_final_g, v_ln1_g=v_ln1_g, v_w_in=v_w_in, v_conv_a_w=v_conv_a_w, v_conv_b_w=v_conv_b_w, v_conv_b_b=v_conv_b_b, v_lru_wa=v_lru_wa, v_lru_ba=v_lru_ba, v_lru_wx=v_lru_wx, v_lru_bx=v_lru_bx, v_lru_lambda=v_lru_lambda, v_w_out_a=v_w_out_a, v_w_out_b=v_w_out_b, v_gate_bias=v_gate_bias, v_w_o=v_w_o, v_ln2_g=v_ln2_g, v_w_ffn_gate=v_w_ffn_gate, v_w_ffn_up=v_w_ffn_up, v_w_ffn_down=v_w_ffn_down, v_final_g=v_final_g)
    weights = {n: given[n] for n in TWIN_WEIGHTS}
    shared = {n: given[n] for n in SHARED_INPUTS}
    per_example = {n: given[n] for n in ['x']}
    grad_fn = _jax.value_and_grad(_loss, argnums=(0, 1))

    def one_microbatch(ex, loss_target):
        ex = dict(ex)
        diff = ex.pop(TWIN_DIFF_INPUT)
        return grad_fn(weights, diff, {**shared, **ex}, loss_target)

    if N_MICROBATCH == 1:
        loss, (grad_w, grad_x) = one_microbatch(per_example, given["loss_target"])
    else:
        def body(carry, xs):
            loss_sum, grad_sum = carry
            l_k, (gw_k, gx_k) = one_microbatch(xs[0], xs[1])
            with _jax.named_scope("update"):
                return (loss_sum + l_k, _jax.tree.map(_jnp.add, grad_sum, gw_k)), gx_k

        init = (_jnp.zeros((), _jnp.float32), _jax.tree.map(_jnp.zeros_like, weights))
        (loss, grad_w), grad_x = _jax.lax.scan(body, init, (per_example, given["loss_target"]))
    with _jax.named_scope("update"):
        delta_w, new_m, new_v = {}, {}, {}
        for n in TWIN_WEIGHTS:
            delta_w[n], new_m[n], new_v[n] = _adamw(weights[n], grad_w[n], given["m_" + n], given["v_" + n])
    return (loss, grad_x, *[grad_w[n] for n in TWIN_WEIGHTS], *[delta_w[n] for n in TWIN_WEIGHTS],
            *[new_m[n] for n in TWIN_WEIGHTS], *[new_v[n] for n in TWIN_WEIGHTS])
```

```python
import functools

import jax
import jax.numpy as jnp
from jax import lax
from jax.experimental import pallas as pl
from jax.experimental.pallas import tpu as pltpu

F32 = jnp.float32
BF16 = jnp.bfloat16
MESH = pl.DeviceIdType.MESH

N_CHIP = 4
RMS_EPS = 1e-6
LRU_C = 8.0
LRU_HEAD_DIM = 64
LRU_BLOCK = 256
CONV_A_K = 3
CONV_B_K = 4
ADAM_LR = 0.001
ADAM_B1 = 0.9
ADAM_B2 = 0.999
ADAM_EPS = 1e-08
ADAM_WD = 0.01
ADAM_STEP = 10
SUBLANES = 8
VMEM_LIMIT = 56 * 1024 * 1024


def _params(*sem):
    return pltpu.CompilerParams(dimension_semantics=sem, vmem_limit_bytes=VMEM_LIMIT)


def _sigmoid(v):
    return 1.0 / (1.0 + jnp.exp(-v))


def _neg_expm1(y):
    poly = y * (1.0 + y * (1.0 / 2 + y * (1.0 / 6 + y * (1.0 / 24 + y * (1.0 / 120 + y * (1.0 / 720 + y * (1.0 / 5040)))))))
    return jnp.where(y > -0.25, -poly, 1.0 - jnp.exp(y))


def _gelu_parts(v):
    k = 0.7978845608028654
    v2 = v * v
    t = jnp.tanh(k * (v + 0.044715 * v * v2))
    gelu = 0.5 * v * (1.0 + t)
    dgelu = 0.5 * (1.0 + t) + 0.5 * v * (1.0 - t * t) * k * (1.0 + 3 * 0.044715 * v2)
    return gelu, dgelu


def _shift_down(v, k, prev8):
    rolled = pltpu.roll(v, k, 0)
    r8 = lax.broadcasted_iota(jnp.int32, prev8.shape, 0)
    head = jnp.where(r8 < k, pltpu.roll(prev8, k, 0), rolled[0:SUBLANES])
    return jnp.concatenate([head, rolled[SUBLANES:]], axis=0)


def _shift_up(v, k, next8):
    tm = v.shape[0]
    rolled = pltpu.roll(v, tm - k, 0)
    r8 = lax.broadcasted_iota(jnp.int32, next8.shape, 0)
    tail = jnp.where(r8 >= SUBLANES - k, pltpu.roll(next8, SUBLANES - k, 0), rolled[tm - SUBLANES:])
    return jnp.concatenate([rolled[:tm - SUBLANES], tail], axis=0)


def _colsum8(v):
    tm, c = v.shape
    return jnp.sum(v.reshape(tm // SUBLANES, SUBLANES, c), axis=0)


def _rms_stats(xv):
    var = jnp.mean(xv * xv, axis=-1, keepdims=True)
    return lax.rsqrt(var + RMS_EPS)


def _rms_bwd(dh, xv, g):
    rstd = _rms_stats(xv)
    xhat = xv * rstd
    dxhat = dh * g
    dx = rstd * (dxhat - xhat * jnp.mean(dxhat * xhat, axis=-1, keepdims=True))
    return dx, dh * xhat


def _rms_inproj(x, g_row, win, layer, tm):
    T, D = x.shape
    ns = win.shape[-1]

    def body(x_ref, g_ref, w_ref, p_ref, h_ref):
        @pl.when(pl.program_id(1) == 0)
        def _():
            xv = x_ref[...]
            h_ref[...] = (xv * _rms_stats(xv) * g_ref[...]).astype(BF16)
        p_ref[...] = jnp.dot(h_ref[...], w_ref[...], preferred_element_type=F32).astype(BF16)

    return pl.pallas_call(
        body, name=f"rms_inproj_{layer}", grid=(T // tm, N_CHIP),
        in_specs=[pl.BlockSpec((tm, D), lambda i, j: (i, 0)),
                  pl.BlockSpec((1, D), lambda i, j: (0, 0)),
                  pl.BlockSpec((None, None, D, ns), lambda i, j: (layer, j, 0, 0))],
        out_specs=[pl.BlockSpec((tm, ns), lambda i, j: (i, j)),
                   pl.BlockSpec((tm, D), lambda i, j: (i, 0))],
        out_shape=[jax.ShapeDtypeStruct((T, N_CHIP * ns), BF16), jax.ShapeDtypeStruct((T, D), BF16)],
        compiler_params=_params("parallel", "arbitrary"),
    )(x, g_row, win)


def _mixer_recompute(ca, xa, xb, zprev8, xbprev8, cw_ref, wab_ref, sp):
    row = lambda k: cw_ref[pl.ds(k, 1), :]
    z = ca * xa
    z1 = _shift_down(z, 1, zprev8)
    z2 = _shift_down(z, 2, zprev8)
    cz = row(2) * z + row(1) * z1 + row(0) * z2
    x1 = _shift_down(xb, 1, xbprev8)
    x2 = _shift_down(xb, 2, xbprev8)
    x3 = _shift_down(xb, 3, xbprev8)
    u = row(6) * xb + row(5) * x1 + row(4) * x2 + row(3) * x3 + row(7)
    ub = u.astype(BF16)
    nb = wab_ref.shape[0]
    ras, ixs = [], []
    for b in range(nb):
        ri = jnp.dot(ub[:, b * LRU_BLOCK:(b + 1) * LRU_BLOCK], wab_ref[b], preferred_element_type=F32)
        ras.append(ri[:, :LRU_BLOCK])
        ixs.append(ri[:, LRU_BLOCK:])
    r = _sigmoid(jnp.concatenate(ras, axis=1) + row(8))
    gi = _sigmoid(jnp.concatenate(ixs, axis=1) + row(9))
    la = (-LRU_C) * r * sp
    a = jnp.exp(la)
    m = jnp.sqrt(_neg_expm1(2.0 * la))
    return dict(z=z, z1=z1, z2=z2, cz=cz, x1=x1, x2=x2, x3=x3, u=u, ub=ub, r=r, gi=gi, a=a, m=m)


def _softplus_neg(lam):
    v = -lam
    return jnp.maximum(v, 0.0) + jnp.log1p(jnp.exp(-jnp.abs(v)))


def _mixer_fwd(p, cw, wab, lam_row, layer, tm):
    T = p.shape[0]
    D = p.shape[1] // 7
    ngroups = tm // SUBLANES

    def body(ba_ref, ca_ref, xa_ref, xb_ref, gb_ref, cw_ref, wab_ref, lam_ref, ya_ref, yb_ref, h_ref,
             zprev, xbprev, hcarry, a_s, h_s):
        @pl.when(pl.program_id(0) == 0)
        def _():
            zprev[...] = jnp.zeros_like(zprev)
            xbprev[...] = jnp.zeros_like(xbprev)
            hcarry[...] = jnp.zeros_like(hcarry)

        ca = ca_ref[...].astype(F32)
        xa = xa_ref[...].astype(F32)
        xb = xb_ref[...].astype(F32)
        sp = _softplus_neg(lam_ref[...])
        c = _mixer_recompute(ca, xa, xb, zprev[...], xbprev[...], cw_ref, wab_ref, sp)
        zprev[...] = c["z"][tm - SUBLANES:]
        xbprev[...] = xb[tm - SUBLANES:]
        ya_ref[...] = (ba_ref[...].astype(F32) * c["cz"]).astype(BF16)

        A = c["a"]
        B = c["m"] * c["gi"] * c["u"]
        q = lax.broadcasted_iota(jnp.int32, A.shape, 0) & (SUBLANES - 1)
        for s in (1, 2, 4):
            msk = q >= s
            B = jnp.where(msk, A * pltpu.roll(B, s, 0) + B, B)
            A = jnp.where(msk, A * pltpu.roll(A, s, 0), A)
        a_s[...] = A
        h_s[...] = B

        def step(g, carry):
            off = pl.multiple_of(g * SUBLANES, SUBLANES)
            hg = h_s[pl.ds(off, SUBLANES), :] + a_s[pl.ds(off, SUBLANES), :] * carry
            h_s[pl.ds(off, SUBLANES), :] = hg
            return jnp.broadcast_to(hg[SUBLANES - 1:SUBLANES, :], hg.shape)

        hcarry[...] = lax.fori_loop(0, ngroups, step, hcarry[...], unroll=4)
        h = h_s[...]
        h_ref[...] = h
        gelu, _ = _gelu_parts(gb_ref[...].astype(F32))
        yb_ref[...] = (h * gelu).astype(BF16)

    col = lambda k: pl.BlockSpec((tm, D), lambda i: (i, k))
    full = lambda a: pl.BlockSpec(a.shape, lambda i: (0,) * a.ndim)
    tok = pl.BlockSpec((tm, D), lambda i: (i, 0))
    return pl.pallas_call(
        body, name=f"mixer_fwd_{layer}", grid=(T // tm,),
        in_specs=[col(0), col(1), col(2), col(3), col(4), full(cw), full(wab), full(lam_row)],
        out_specs=[tok, tok, tok],
        out_shape=[jax.ShapeDtypeStruct((T, D), BF16), jax.ShapeDtypeStruct((T, D), BF16), jax.ShapeDtypeStruct((T, D), F32)],
        scratch_shapes=[pltpu.VMEM((SUBLANES, D), F32), pltpu.VMEM((SUBLANES, D), F32), pltpu.VMEM((SUBLANES, D), F32),
                        pltpu.VMEM((tm, D), F32), pltpu.VMEM((tm, D), F32)],
        compiler_params=_params("arbitrary"),
    )(p, p, p, p, p, cw, wab, lam_row)


def _merge_fwd(x, p, ya, yb, woa, wob, wo, gbias, layer, tm):
    T, D = x.shape

    def body(x_ref, ga_ref, gb_ref, ya_ref, yb_ref, woa_ref, wob_ref, wo_ref, bias_ref, oa_ref, ob_ref, mg_ref, x1_ref):
        oa = jnp.dot(ya_ref[...], woa_ref[...], preferred_element_type=F32)
        ob = jnp.dot(yb_ref[...], wob_ref[...], preferred_element_type=F32)
        sa = _sigmoid(ga_ref[...].astype(F32) + bias_ref[pl.ds(0, 1), :])
        sb = _sigmoid(gb_ref[...].astype(F32) + bias_ref[pl.ds(1, 1), :])
        mg = (sa * oa + sb * ob).astype(BF16)
        oa_ref[...] = oa.astype(BF16)
        ob_ref[...] = ob.astype(BF16)
        mg_ref[...] = mg
        x1_ref[...] = x_ref[...] + jnp.dot(mg, wo_ref[...], preferred_element_type=F32)

    tok = pl.BlockSpec((tm, D), lambda i: (i, 0))
    wsp = pl.BlockSpec((None, D, D), lambda i: (layer, 0, 0))
    bf = jax.ShapeDtypeStruct((T, D), BF16)
    return pl.pallas_call(
        body, name=f"merge_fwd_{layer}", grid=(T // tm,),
        in_specs=[tok, pl.BlockSpec((tm, D), lambda i: (i, 5)), pl.BlockSpec((tm, D), lambda i: (i, 6)), tok, tok,
                  wsp, wsp, wsp, pl.BlockSpec(gbias.shape, lambda i: (0, 0))],
        out_specs=[tok, tok, tok, tok],
        out_shape=[bf, bf, bf, jax.ShapeDtypeStruct((T, D), F32)],
        compiler_params=_params("parallel"),
    )(x, p, p, ya, yb, woa, wob, wo, gbias)


def _ffn_fwd(x1, g_row, wg, wu, wd, layer, tm):
    T, D = x1.shape
    fs = wg.shape[-1]

    def body(x_ref, g_ref, wg_ref, wu_ref, wd_ref, h_ref, gg_ref, uu_ref, x2_ref, acc):
        j = pl.program_id(1)

        @pl.when(j == 0)
        def _():
            xv = x_ref[...]
            h_ref[...] = (xv * _rms_stats(xv) * g_ref[...]).astype(BF16)
            acc[...] = xv

        h = h_ref[...]
        gg = jnp.dot(h, wg_ref[...], preferred_element_type=F32)
        uu = jnp.dot(h, wu_ref[...], preferred_element_type=F32)
        gg_ref[...] = gg.astype(BF16)
        uu_ref[...] = uu.astype(BF16)
        act = (gg * _sigmoid(gg) * uu).astype(BF16)
        acc[...] += jnp.dot(act, wd_ref[...], preferred_element_type=F32)

        @pl.when(j == N_CHIP - 1)
        def _():
            x2_ref[...] = acc[...]

    tok = pl.BlockSpec((tm, D), lambda i, j: (i, 0))
    cm = pl.BlockSpec((None, tm, fs), lambda i, j: (j, i, 0))
    return pl.pallas_call(
        body, name=f"ffn_fwd_{layer}", grid=(T // tm, N_CHIP),
        in_specs=[tok, pl.BlockSpec((1, D), lambda i, j: (0, 0)),
                  pl.BlockSpec((None, None, D, fs), lambda i, j: (layer, j, 0, 0)),
                  pl.BlockSpec((None, None, D, fs), lambda i, j: (layer, j, 0, 0)),
                  pl.BlockSpec((None, None, fs, D), lambda i, j: (layer, j, 0, 0))],
        out_specs=[tok, cm, cm, tok],
        out_shape=[jax.ShapeDtypeStruct((T, D), BF16), jax.ShapeDtypeStruct((N_CHIP, T, fs), BF16),
                   jax.ShapeDtypeStruct((N_CHIP, T, fs), BF16), jax.ShapeDtypeStruct((T, D), F32)],
        scratch_shapes=[pltpu.VMEM((tm, D), F32)],
        compiler_params=_params("parallel", "arbitrary"),
    )(x1, g_row, wg, wu, wd)


def _final_loss(x, g_row, target, tm):
    T, D = x.shape
    n = T // tm

    def body(x_ref, g_ref, t_ref, dx_ref, red_ref, acc):
        i = pl.program_id(0)

        @pl.when(i == 0)
        def _():
            acc[...] = jnp.zeros_like(acc)

        xv = x_ref[...]
        g = g_ref[...]
        rstd = _rms_stats(xv)
        xhat = xv * rstd
        err = xhat * g - t_ref[...]
        dy = err * (1.0 / D)
        dxhat = dy * g
        dx_ref[...] = rstd * (dxhat - xhat * jnp.mean(dxhat * xhat, axis=-1, keepdims=True))
        acc[0] += _colsum8(err * err)
        acc[1] += _colsum8(dy * xhat)

        @pl.when(i == n - 1)
        def _():
            red_ref[pl.ds(0, 1), :] = jnp.sum(acc[0], axis=0, keepdims=True) * (0.5 / D)
            red_ref[pl.ds(1, 1), :] = jnp.sum(acc[1], axis=0, keepdims=True)

    tok = pl.BlockSpec((tm, D), lambda i: (i, 0))
    return pl.pallas_call(
        body, name="final_loss", grid=(n,),
        in_specs=[tok, pl.BlockSpec((1, D), lambda i: (0, 0)), tok],
        out_specs=[tok, pl.BlockSpec((2, D), lambda i: (0, 0))],
        out_shape=[jax.ShapeDtypeStruct((T, D), F32), jax.ShapeDtypeStruct((2, D), F32)],
        scratch_shapes=[pltpu.VMEM((2, SUBLANES, D), F32)],
        compiler_params=_params("arbitrary"),
    )(x, g_row, target)


def _ffn_bwd(dx2, x1, g_row, gg, uu, wg, wu, wd, layer, tm):
    T, D = dx2.shape
    fs = wg.shape[-1]
    n = T // tm
    nt = (((1,), (1,)), ((), ()))

    def body(dx_ref, x_ref, g_ref, gg_ref, uu_ref, wg_ref, wu_ref, wd_ref, dg_ref, du_ref, act_ref, dx1_ref, dxb_ref, red_ref,
             acc, racc):
        i = pl.program_id(0)
        j = pl.program_id(1)

        @pl.when((i == 0) & (j == 0))
        def _():
            racc[...] = jnp.zeros_like(racc)

        @pl.when(j == 0)
        def _():
            dxb_ref[...] = dx_ref[...].astype(BF16)
            acc[...] = jnp.zeros_like(acc)

        dact = lax.dot_general(dxb_ref[...], wd_ref[...], nt, preferred_element_type=F32)
        g = gg_ref[...].astype(F32)
        u = uu_ref[...].astype(F32)
        s = _sigmoid(g)
        silu = g * s
        dg = (dact * u * (s * (1.0 + g * (1.0 - s)))).astype(BF16)
        du = (dact * silu).astype(BF16)
        dg_ref[...] = dg
        du_ref[...] = du
        act_ref[...] = (silu * u).astype(BF16)
        acc[...] += (lax.dot_general(dg, wg_ref[...], nt, preferred_element_type=F32)
                     + lax.dot_general(du, wu_ref[...], nt, preferred_element_type=F32))

        @pl.when(j == N_CHIP - 1)
        def _():
            dx, dgain = _rms_bwd(acc[...], x_ref[...], g_ref[...])
            dx1_ref[...] = dx_ref[...] + dx
            racc[...] += _colsum8(dgain)

        @pl.when((i == n - 1) & (j == N_CHIP - 1))
        def _():
            red_ref[...] = jnp.sum(racc[...], axis=0, keepdims=True)

    tok = pl.BlockSpec((tm, D), lambda i, j: (i, 0))
    cm = pl.BlockSpec((None, tm, fs), lambda i, j: (j, i, 0))
    cms = jax.ShapeDtypeStruct((N_CHIP, T, fs), BF16)
    return pl.pallas_call(
        body, name=f"ffn_bwd_{layer}", grid=(n, N_CHIP),
        in_specs=[tok, tok, pl.BlockSpec((1, D), lambda i, j: (0, 0)), cm, cm,
                  pl.BlockSpec((None, None, D, fs), lambda i, j: (layer, j, 0, 0)),
                  pl.BlockSpec((None, None, D, fs), lambda i, j: (layer, j, 0, 0)),
                  pl.BlockSpec((None, None, fs, D), lambda i, j: (layer, j, 0, 0))],
        out_specs=[cm, cm, cm, tok, tok, pl.BlockSpec((1, D), lambda i, j: (0, 0))],
        out_shape=[cms, cms, cms, jax.ShapeDtypeStruct((T, D), F32), jax.ShapeDtypeStruct((T, D), BF16),
                   jax.ShapeDtypeStruct((1, D), F32)],
        scratch_shapes=[pltpu.VMEM((tm, D), F32), pltpu.VMEM((SUBLANES, D), F32)],
        compiler_params=_params("arbitrary", "arbitrary"),
    )(dx2, x1, g_row, gg, uu, wg, wu, wd)


def _merge_bwd(dx1, p, oa, ob, woa, wob, wo, gbias, layer, tm):
    T, D = dx1.shape
    n = T // tm
    nt = (((1,), (1,)), ((), ()))

    def body(dx_ref, ga_ref, gb_ref, oa_ref, ob_ref, woa_ref, wob_ref, wo_ref, bias_ref,
             dya_ref, dyb_ref, doa_ref, dob_ref, dgl_ref, dxb_ref, red_ref, racc):
        i = pl.program_id(0)

        @pl.when(i == 0)
        def _():
            racc[...] = jnp.zeros_like(racc)

        dxb = dx_ref[...].astype(BF16)
        dxb_ref[...] = dxb
        dm = lax.dot_general(dxb, wo_ref[...], nt, preferred_element_type=F32)
        sa = _sigmoid(ga_ref[...].astype(F32) + bias_ref[pl.ds(0, 1), :])
        sb = _sigmoid(gb_ref[...].astype(F32) + bias_ref[pl.ds(1, 1), :])
        doa = (dm * sa).astype(BF16)
        dob = (dm * sb).astype(BF16)
        dga = dm * oa_ref[...].astype(F32) * (sa * (1.0 - sa))
        dgb = dm * ob_ref[...].astype(F32) * (sb * (1.0 - sb))
        doa_ref[...] = doa
        dob_ref[...] = dob
        dgl_ref[:, 0:D] = dga.astype(BF16)
        dgl_ref[:, D:2 * D] = dgb.astype(BF16)
        racc[0] += _colsum8(dga)
        racc[1] += _colsum8(dgb)
        dya_ref[...] = lax.dot_general(doa, woa_ref[...], nt, preferred_element_type=F32).astype(BF16)
        dyb_ref[...] = lax.dot_general(dob, wob_ref[...], nt, preferred_element_type=F32).astype(BF16)

        @pl.when(i == n - 1)
        def _():
            red_ref[pl.ds(0, 1), :] = jnp.sum(racc[0], axis=0, keepdims=True)
            red_ref[pl.ds(1, 1), :] = jnp.sum(racc[1], axis=0, keepdims=True)

    tok = pl.BlockSpec((tm, D), lambda i: (i, 0))
    wsp = pl.BlockSpec((None, D, D), lambda i: (layer, 0, 0))
    bf = jax.ShapeDtypeStruct((T, D), BF16)
    return pl.pallas_call(
        body, name=f"merge_bwd_{layer}", grid=(n,),
        in_specs=[tok, pl.BlockSpec((tm, D), lambda i: (i, 5)), pl.BlockSpec((tm, D), lambda i: (i, 6)), tok, tok,
                  wsp, wsp, wsp, pl.BlockSpec(gbias.shape, lambda i: (0, 0))],
        out_specs=[tok, tok, tok, tok, pl.BlockSpec((tm, 2 * D), lambda i: (i, 0)), tok, pl.BlockSpec((2, D), lambda i: (0, 0))],
        out_shape=[bf, bf, bf, bf, jax.ShapeDtypeStruct((T, 2 * D), BF16), bf, jax.ShapeDtypeStruct((2, D), F32)],
        scratch_shapes=[pltpu.VMEM((2, SUBLANES, D), F32)],
        compiler_params=_params("arbitrary"),
    )(dx1, p, p, oa, ob, woa, wob, wo, gbias)


N_MIXER_RED = 16


def _mixer_bwd(p, hseq, dya, dyb, dgl, cw, wab, wabt, lam_row, layer, tm):
    T = p.shape[0]
    D = p.shape[1] // 7
    n = T // tm
    ngroups = tm // SUBLANES
    nb = wab.shape[0]
    hb = 16
    tn = (((0,), (0,)), ((), ()))

    def body(ba_ref, ca_ref, xa_ref, xb_ref, gb_ref, h_ref, dya_ref, dyb_ref, dgl_ref,
             cap_ref, xap_ref, xbp_ref, hp_ref, ban_ref, dyan_ref,
             cw_ref, wab_ref, wabt_ref, lam_ref,
             dp_ref, red_ref, dwab_ref,
             racc, wacc, anext, gnext, dunext, c_s, g_s):
        i = pl.program_id(0)
        first_tile = i == n - 1
        last_tile = i == 0

        @pl.when(i == 0)
        def _():
            racc[...] = jnp.zeros_like(racc)
            wacc[...] = jnp.zeros_like(wacc)
            anext[...] = jnp.zeros_like(anext)
            gnext[...] = jnp.zeros_like(gnext)
            dunext[...] = jnp.zeros_like(dunext)

        keep_prev = jnp.where(first_tile, 0.0, 1.0)
        keep_next = jnp.where(last_tile, 0.0, 1.0)
        ba = ba_ref[...].astype(F32)
        ca = ca_ref[...].astype(F32)
        xa = xa_ref[...].astype(F32)
        xb = xb_ref[...].astype(F32)
        h = h_ref[...]
        dya = dya_ref[...].astype(F32)
        dyb = dyb_ref[...].astype(F32)
        zprev8 = (cap_ref[...].astype(F32) * xap_ref[...].astype(F32))[hb - SUBLANES:] * keep_prev
        xbprev8 = xbp_ref[...].astype(F32)[hb - SUBLANES:] * keep_prev
        hprev8 = hp_ref[...] * keep_prev
        dcznext8 = (dyan_ref[...].astype(F32) * ban_ref[...].astype(F32))[:SUBLANES] * keep_next

        lam = lam_ref[...]
        sp = _softplus_neg(lam)
        c = _mixer_recompute(ca, xa, xb, zprev8, xbprev8, cw_ref, wab_ref, sp)
        row = lambda k: cw_ref[pl.ds(k, 1), :]
        a, m, r, gi, u = c["a"], c["m"], c["r"], c["gi"], c["u"]

        gelu, dgelu = _gelu_parts(gb_ref[...].astype(F32))
        dgb = dyb * h * dgelu
        Dv = dyb * gelu
        Cv = _shift_up(a, 1, anext[...])
        q = lax.broadcasted_iota(jnp.int32, Cv.shape, 0) & (SUBLANES - 1)
        for s in (1, 2, 4):
            msk = q < SUBLANES - s
            Dv = jnp.where(msk, Dv + Cv * pltpu.roll(Dv, tm - s, 0), Dv)
            Cv = jnp.where(msk, Cv * pltpu.roll(Cv, tm - s, 0), Cv)
        c_s[...] = Cv
        g_s[...] = Dv

        def step(k, carry):
            off = pl.multiple_of((ngroups - 1 - k) * SUBLANES, SUBLANES)
            gg = g_s[pl.ds(off, SUBLANES), :] + c_s[pl.ds(off, SUBLANES), :] * carry
            g_s[pl.ds(off, SUBLANES), :] = gg
            return jnp.broadcast_to(gg[0:1, :], gg.shape)

        gnext[...] = lax.fori_loop(0, ngroups, step, gnext[...], unroll=4)
        anext[...] = a[0:SUBLANES]
        g = g_s[...]

        hprev = _shift_down(h, 1, hprev8)
        da = g * hprev
        gm = g * m
        dgi = gm * u
        du = gm * gi
        dmv = g * gi * u
        dla = a * (da - dmv * a / m)
        dra = dla * ((-LRU_C) * sp) * (r * (1.0 - r))
        dix = dgi * (gi * (1.0 - gi))
        racc[10] += _colsum8(dla * r)
        racc[8] += _colsum8(dra)
        racc[9] += _colsum8(dix)
        drab = dra.astype(BF16)
        dixb = dix.astype(BF16)
        ub = c["ub"]
        dus = []
        for b in range(nb):
            sl = slice(b * LRU_BLOCK, (b + 1) * LRU_BLOCK)
            dri = jnp.concatenate([drab[:, sl], dixb[:, sl]], axis=1)
            dus.append(jnp.dot(dri, wabt_ref[b], preferred_element_type=F32))
            wacc[b] += lax.dot_general(ub[:, sl], dri, tn, preferred_element_type=F32)
        du = du + jnp.concatenate(dus, axis=1)

        dun = dunext[...]
        du1 = _shift_up(du, 1, dun)
        du2 = _shift_up(du, 2, dun)
        du3 = _shift_up(du, 3, dun)
        dxb = row(6) * du + row(5) * du1 + row(4) * du2 + row(3) * du3
        dunext[...] = du[0:SUBLANES]
        racc[6] += _colsum8(du * xb)
        racc[5] += _colsum8(du * c["x1"])
        racc[4] += _colsum8(du * c["x2"])
        racc[3] += _colsum8(du * c["x3"])
        racc[7] += _colsum8(du)

        dba = dya * c["cz"]
        dcz = dya * ba
        dcz1 = _shift_up(dcz, 1, dcznext8)
        dcz2 = _shift_up(dcz, 2, dcznext8)
        dz = row(2) * dcz + row(1) * dcz1 + row(0) * dcz2
        racc[2] += _colsum8(dcz * c["z"])
        racc[1] += _colsum8(dcz * c["z1"])
        racc[0] += _colsum8(dcz * c["z2"])

        dp_ref[:, 0:D] = dba.astype(BF16)
        dp_ref[:, D:2 * D] = (dz * xa).astype(BF16)
        dp_ref[:, 2 * D:3 * D] = (dz * ca).astype(BF16)
        dp_ref[:, 3 * D:4 * D] = dxb.astype(BF16)
        dp_ref[:, 4 * D:5 * D] = dgb.astype(BF16)
        dp_ref[:, 5 * D:7 * D] = dgl_ref[...]

        @pl.when(i == n - 1)
        def _():
            dlam_scale = LRU_C * _sigmoid(-lam)
            for k in range(N_MIXER_RED):
                tot = jnp.sum(racc[k], axis=0, keepdims=True)
                red_ref[pl.ds(k, 1), :] = tot * dlam_scale if k == 10 else tot
            dwab_ref[...] = wacc[...]

    rt = lambda i: n - 1 - i
    col = lambda k: pl.BlockSpec((tm, D), lambda i: (rt(i), k))
    tok = pl.BlockSpec((tm, D), lambda i: (rt(i), 0))
    full = lambda a: pl.BlockSpec(a.shape, lambda i: (0,) * a.ndim)
    prev16 = lambda k: pl.BlockSpec((hb, D), lambda i: (jnp.maximum(rt(i) * (tm // hb) - 1, 0), k))
    next16 = lambda k: pl.BlockSpec((hb, D), lambda i: (jnp.minimum((rt(i) + 1) * (tm // hb), T // hb - 1), k))
    hprev = pl.BlockSpec((SUBLANES, D), lambda i: (jnp.maximum(rt(i) * ngroups - 1, 0), 0))
    return pl.pallas_call(
        body, name=f"mixer_bwd_{layer}", grid=(n,),
        in_specs=[col(0), col(1), col(2), col(3), col(4), tok, tok, tok, pl.BlockSpec((tm, 2 * D), lambda i: (rt(i), 0)),
                  prev16(1), prev16(2), prev16(3), hprev, next16(0), next16(0),
                  full(cw), full(wab), full(wabt), full(lam_row)],
        out_specs=[pl.BlockSpec((tm, 7 * D), lambda i: (rt(i), 0)),
                   pl.BlockSpec((N_MIXER_RED, D), lambda i: (0, 0)),
                   pl.BlockSpec((nb, LRU_BLOCK, 2 * LRU_BLOCK), lambda i: (0, 0, 0))],
        out_shape=[jax.ShapeDtypeStruct((T, 7 * D), BF16), jax.ShapeDtypeStruct((N_MIXER_RED, D), F32),
                   jax.ShapeDtypeStruct((nb, LRU_BLOCK, 2 * LRU_BLOCK), F32)],
        scratch_shapes=[pltpu.VMEM((N_MIXER_RED, SUBLANES, D), F32), pltpu.VMEM((nb, LRU_BLOCK, 2 * LRU_BLOCK), F32),
                        pltpu.VMEM((SUBLANES, D), F32), pltpu.VMEM((SUBLANES, D), F32), pltpu.VMEM((SUBLANES, D), F32),
                        pltpu.VMEM((tm, D), F32), pltpu.VMEM((tm, D), F32)],
        compiler_params=_params("arbitrary"),
    )(p, p, p, p, p, hseq, dya, dyb, dgl, p, p, p, hseq, p, dya, cw, wab, wabt, lam_row)


def _inproj_bwd(dp, dx1, x, g_row, win, layer, tm):
    T, D = x.shape
    ns = win.shape[-1]
    n = T // tm
    nt = (((1,), (1,)), ((), ()))

    def body(dp_ref, dx_ref, x_ref, g_ref, w_ref, dx0_ref, red_ref, acc, racc):
        i = pl.program_id(0)
        j = pl.program_id(1)

        @pl.when((i == 0) & (j == 0))
        def _():
            racc[...] = jnp.zeros_like(racc)

        @pl.when(j == 0)
        def _():
            acc[...] = jnp.zeros_like(acc)

        acc[...] += lax.dot_general(dp_ref[...], w_ref[...], nt, preferred_element_type=F32)

        @pl.when(j == N_CHIP - 1)
        def _():
            dx, dgain = _rms_bwd(acc[...], x_ref[...], g_ref[...])
            dx0_ref[...] = dx_ref[...] + dx
            racc[...] += _colsum8(dgain)

        @pl.when((i == n - 1) & (j == N_CHIP - 1))
        def _():
            red_ref[...] = jnp.sum(racc[...], axis=0, keepdims=True)

    tok = pl.BlockSpec((tm, D), lambda i, j: (i, 0))
    return pl.pallas_call(
        body, name=f"inproj_bwd_{layer}", grid=(n, N_CHIP),
        in_specs=[pl.BlockSpec((tm, ns), lambda i, j: (i, j)), tok, tok, pl.BlockSpec((1, D), lambda i, j: (0, 0)),
                  pl.BlockSpec((None, None, D, ns), lambda i, j: (layer, j, 0, 0))],
        out_specs=[tok, pl.BlockSpec((1, D), lambda i, j: (0, 0))],
        out_shape=[jax.ShapeDtypeStruct((T, D), F32), jax.ShapeDtypeStruct((1, D), F32)],
        scratch_shapes=[pltpu.VMEM((tm, D), F32), pltpu.VMEM((SUBLANES, D), F32)],
        compiler_params=_params("arbitrary", "arbitrary"),
    )(dp, dx1, x, g_row, win)


def _wgrad(a, b, name, tk, layer, n_layers, prev=None, a_kind="whole", b_kind="whole", nj=1):
    T = a.shape[-2]
    width = lambda v, kind: v.shape[-1] // nj if kind == "cols" else v.shape[-1]
    ka, kb = width(a, a_kind), width(b, b_kind)
    nt = T // tk
    tn = (((0,), (0,)), ((), ()))

    def spec(k, kind):
        if kind == "cm":
            return pl.BlockSpec((None, tk, k), lambda j, t: (j, t, 0))
        if kind == "cols":
            return pl.BlockSpec((tk, k), lambda j, t: (t, j))
        return pl.BlockSpec((tk, k), lambda j, t: (t, 0))

    def body(a_ref, b_ref, *rest):
        o_ref, ob_ref = rest[-2:]
        t = pl.program_id(1)

        @pl.when(t == 0)
        def _():
            o_ref[...] = jnp.zeros_like(o_ref)

        o_ref[...] += lax.dot_general(a_ref[...], b_ref[...], tn, preferred_element_type=F32)

        @pl.when(t == nt - 1)
        def _():
            ob_ref[...] = o_ref[...].astype(BF16)

    o_spec = pl.BlockSpec((None, None, ka, kb), lambda j, t: (layer, j, 0, 0))
    in_specs = [spec(ka, a_kind), spec(kb, b_kind)]
    operands = [a, b]
    aliases = {}
    if prev is not None:
        in_specs += [pl.BlockSpec(memory_space=pl.ANY)] * 2
        operands += list(prev)
        aliases = {2: 0, 3: 1}
    return pl.pallas_call(
        body, name=f"{name}_{layer}", grid=(nj, nt),
        in_specs=in_specs, out_specs=[o_spec, o_spec],
        out_shape=[jax.ShapeDtypeStruct((n_layers, nj, ka, kb), F32), jax.ShapeDtypeStruct((n_layers, nj, ka, kb), BF16)],
        input_output_aliases=aliases,
        compiler_params=_params("parallel", "arbitrary"),
    )(*operands)


def _block_diag(w):
    hb = LRU_BLOCK // LRU_HEAD_DIM
    nb = w.shape[0] // hb
    w4 = w.reshape(nb, hb, LRU_HEAD_DIM, LRU_HEAD_DIM)
    eye = jnp.eye(hb, dtype=w.dtype)
    return jnp.einsum("bide,ij->bidje", w4, eye).reshape(nb, LRU_BLOCK, LRU_BLOCK)


def _diag_heads(m):
    hb = LRU_BLOCK // LRU_HEAD_DIM
    nb = m.shape[0]
    m5 = m.reshape(nb, hb, LRU_HEAD_DIM, hb, LRU_HEAD_DIM)
    eye = jnp.eye(hb, dtype=m.dtype)
    return jnp.einsum("bidje,ij->bide", m5, eye).reshape(nb * hb, LRU_HEAD_DIM, LRU_HEAD_DIM)


def _local_step(x, target, big, small, tm, tm_mix, tk):
    L = small["ln1_g"].shape[0]
    D = x.shape[1]
    saved = []
    h = x
    for l in range(L):
        cw = jnp.concatenate([small["conv_a_w"][l], small["conv_b_w"][l], small["conv_b_b"][l][None],
                              small["lru_ba"][l][None], small["lru_bx"][l][None]], axis=0)
        wab = jnp.concatenate([_block_diag(small["lru_wa"][l]), _block_diag(small["lru_wx"][l])], axis=2).astype(BF16)
        wabt = jnp.swapaxes(wab, 1, 2)
        lam_row = small["lru_lambda"][l][None]
        ln1_row = small["ln1_g"][l][None]
        ln2_row = small["ln2_g"][l][None]
        p, h1 = _rms_inproj(h, ln1_row, big["win"], l, tm)
        ya, yb, hseq = _mixer_fwd(p, cw, wab, lam_row, l, tm_mix)
        oa, ob, mg, x1 = _merge_fwd(h, p, ya, yb, big["woa"], big["wob"], big["wo"], small["gate_bias"][l], l, tm)
        h2, gg, uu, x2 = _ffn_fwd(x1, ln2_row, big["wg"], big["wu"], big["wd"], l, tm)
        saved.append(dict(x0=h, p=p, h1=h1, ya=ya, yb=yb, hseq=hseq, oa=oa, ob=ob, mg=mg, x1=x1, h2=h2, gg=gg, uu=uu,
                          cw=cw, wab=wab, wabt=wabt, lam_row=lam_row, ln1_row=ln1_row, ln2_row=ln2_row))
        h = x2

    dx, red = _final_loss(h, small["final_g"][None], target, tm)
    loss_row, d_final_g = red[0], red[1]

    gbig = dict(win=None, woa=None, wob=None, wo=None, wg=None, wu=None, wd=None)
    gsmall = {k: [None] * L for k in ("ln1_g", "ln2_g", "conv_a_w", "conv_b_w", "conv_b_b", "lru_wa", "lru_ba", "lru_wx",
                                      "lru_bx", "lru_lambda", "gate_bias")}
    for l in reversed(range(L)):
        s = saved[l]
        dgg, duu, act, dx1, dx2b, dln2 = _ffn_bwd(dx, s["x1"], s["ln2_row"], s["gg"], s["uu"], big["wg"], big["wu"], big["wd"], l, tm)
        gbig["wg"] = _wgrad(s["h2"], dgg, "wgrad_ffn_gate", tk, l, L, gbig["wg"], "whole", "cm", N_CHIP)
        gbig["wu"] = _wgrad(s["h2"], duu, "wgrad_ffn_up", tk, l, L, gbig["wu"], "whole", "cm", N_CHIP)
        gbig["wd"] = _wgrad(act, dx2b, "wgrad_ffn_down", tk, l, L, gbig["wd"], "cm", "whole", N_CHIP)
        dya, dyb, doa, dob, dgl, dx1b, dgbias = _merge_bwd(dx1, s["p"], s["oa"], s["ob"], big["woa"], big["wob"], big["wo"],
                                                         small["gate_bias"][l], l, tm)
        gbig["wo"] = _wgrad(s["mg"], dx1b, "wgrad_w_o", tk, l, L, gbig["wo"])
        gbig["woa"] = _wgrad(s["ya"], doa, "wgrad_w_out_a", tk, l, L, gbig["woa"])
        gbig["wob"] = _wgrad(s["yb"], dob, "wgrad_w_out_b", tk, l, L, gbig["wob"])
        dp, mred, dwab = _mixer_bwd(s["p"], s["hseq"], dya, dyb, dgl, s["cw"], s["wab"], s["wabt"], s["lam_row"], l, tm_mix)
        dx, dln1 = _inproj_bwd(dp, dx1, s["x0"], s["ln1_row"], big["win"], l, tm)
        gbig["win"] = _wgrad(s["h1"], dp, "wgrad_w_in", tk, l, L, gbig["win"], "whole", "cols", N_CHIP)
        gsmall["ln1_g"][l] = dln1[0]
        gsmall["ln2_g"][l] = dln2[0]
        gsmall["conv_a_w"][l] = mred[0:CONV_A_K]
        gsmall["conv_b_w"][l] = mred[CONV_A_K:CONV_A_K + CONV_B_K]
        gsmall["conv_b_b"][l] = mred[7]
        gsmall["lru_ba"][l] = mred[8]
        gsmall["lru_bx"][l] = mred[9]
        gsmall["lru_lambda"][l] = mred[10]
        gsmall["lru_wa"][l] = _diag_heads(dwab[:, :, :LRU_BLOCK])
        gsmall["lru_wx"][l] = _diag_heads(dwab[:, :, LRU_BLOCK:])
        gsmall["gate_bias"][l] = dgbias
    gsmall = {k: jnp.stack(v) for k, v in gsmall.items()}
    gsmall["final_g"] = d_final_g
    return loss_row, dx, gbig, gsmall


ANY = pl.BlockSpec(memory_space=pl.ANY)


def _place():
    x, y, c = lax.axis_index("x"), lax.axis_index("y"), lax.axis_index("c")
    other_chips = [(1 - x, y), (x, 1 - y), (1 - x, 1 - y)]
    return x, y, c, other_chips


def _chip_id(x, y):
    return 2 * x + y


def _allgather_weights(shards, small_shard):
    nt = len(shards)
    L = shards[0].shape[0]
    assert L == 2

    def body(*refs):
        ins, sm_in = refs[:nt], refs[nt]
        outs, sm_out = refs[nt + 1:2 * nt + 1], refs[2 * nt + 1]
        lsem, ssem, rsem, fssem, frsem, sm_ssem, sm_rsem = refs[2 * nt + 2:]
        x, y, c, chips = _place()
        me = _chip_id(x, y)
        sib = (x, y, 1 - c)

        local = [pltpu.make_async_copy(ins[t].at[l], outs[t].at[l, me], lsem.at[t * L + l]) for t in range(nt) for l in range(L)]
        local.append(pltpu.make_async_copy(sm_in, sm_out.at[me], lsem.at[nt * L]))
        for cp in local:
            cp.start()

        def ici(t, j, src_chip, to):
            return pltpu.make_async_remote_copy(src_ref=ins[t].at[c], dst_ref=outs[t].at[c, src_chip],
                                                send_sem=ssem.at[t * 3 + j], recv_sem=rsem.at[t * 3 + j],
                                                device_id=to, device_id_type=MESH)

        def d2d(t, j, layer, src_chip):
            blk = outs[t].at[layer, src_chip]
            return pltpu.make_async_remote_copy(src_ref=blk, dst_ref=blk, send_sem=fssem.at[t * 3 + j],
                                                recv_sem=frsem.at[t * 3 + j], device_id=sib, device_id_type=MESH)

        def sm(j, src_chip, to):
            return pltpu.make_async_remote_copy(src_ref=sm_in, dst_ref=sm_out.at[src_chip], send_sem=sm_ssem.at[j],
                                                recv_sem=sm_rsem.at[j], device_id=to, device_id_type=MESH)

        sends = [ici(t, j, me, (cx, cy, c)) for t in range(nt) for j, (cx, cy) in enumerate(chips)]
        sends += [sm(j, me, (cx, cy, c)) for j, (cx, cy) in enumerate(chips)]
        for cp in sends:
            cp.start()
        passed = []
        for t in range(nt):
            for j, (cx, cy) in enumerate(chips):
                ici(t, j, _chip_id(cx, cy), sib).wait_recv()
                f = d2d(t, j, c, _chip_id(cx, cy))
                f.start()
                passed.append(f)
        for t in range(nt):
            for j, (cx, cy) in enumerate(chips):
                d2d(t, j, 1 - c, _chip_id(cx, cy)).wait_recv()
        for j, (cx, cy) in enumerate(chips):
            sm(j, _chip_id(cx, cy), sib).wait_recv()
        for cp in sends + passed:
            cp.wait_send()
        for cp in local:
            cp.wait()

    out_shape = [jax.ShapeDtypeStruct((L, N_CHIP) + s.shape[1:], s.dtype) for s in shards]
    out_shape.append(jax.ShapeDtypeStruct((N_CHIP,) + small_shard.shape, small_shard.dtype))
    dma = pltpu.SemaphoreType.DMA
    return pl.pallas_call(
        body, name="allgather_weights", out_shape=out_shape,
        in_specs=[ANY] * (nt + 1), out_specs=[ANY] * (nt + 1),
        scratch_shapes=[dma((nt * L + 1,)), dma((nt * 3,)), dma((nt * 3,)), dma((nt * 3,)), dma((nt * 3,)), dma((3,)), dma((3,))],
    )(*shards, small_shard)


def _exchange_halves(gbf):
    nt = len(gbf)

    def body(*refs):
        ins, outs = refs[:nt], refs[nt:2 * nt]
        ssem, rsem = refs[2 * nt:]
        x, y, c, _ = _place()
        cps = [pltpu.make_async_remote_copy(src_ref=ins[t].at[1 - c], dst_ref=outs[t], send_sem=ssem.at[t], recv_sem=rsem.at[t],
                                            device_id=(x, y, 1 - c), device_id_type=MESH) for t in range(nt)]
        for cp in cps:
            cp.start()
        for cp in cps:
            cp.wait()

    dma = pltpu.SemaphoreType.DMA
    return pl.pallas_call(
        body, name="reduce_exchange_halves", out_shape=[jax.ShapeDtypeStruct(g.shape[1:], g.dtype) for g in gbf],
        in_specs=[ANY] * nt, out_specs=[ANY] * nt, scratch_shapes=[dma((nt,)), dma((nt,))],
    )(*gbf)


def _scatter_chip_sums(pcb):
    nt = len(pcb)

    def body(*refs):
        ins, outs = refs[:nt], refs[nt:2 * nt]
        ssem, rsem = refs[2 * nt:]
        x, y, c, chips = _place()
        cps = [pltpu.make_async_remote_copy(src_ref=ins[t].at[_chip_id(cx, cy)], dst_ref=outs[t].at[j],
                                            send_sem=ssem.at[t * 3 + j], recv_sem=rsem.at[t * 3 + j],
                                            device_id=(cx, cy, c), device_id_type=MESH)
               for t in range(nt) for j, (cx, cy) in enumerate(chips)]
        for cp in cps:
            cp.start()
        for cp in cps:
            cp.wait()

    dma = pltpu.SemaphoreType.DMA
    return pl.pallas_call(
        body, name="reduce_scatter_chips", out_shape=[jax.ShapeDtypeStruct((3,) + g.shape[1:], g.dtype) for g in pcb],
        in_specs=[ANY] * nt, out_specs=[ANY] * nt, scratch_shapes=[dma((nt * 3,)), dma((nt * 3,))],
    )(*pcb)


def _share_reduced(red):
    nt = len(red)
    L = 2

    def body(*refs):
        ins, outs = refs[:nt], refs[nt:2 * nt]
        lsem, ssem, rsem = refs[2 * nt:]
        x, y, c, _ = _place()
        local = [pltpu.make_async_copy(ins[t], outs[t].at[c], lsem.at[t]) for t in range(nt)]
        cps = [pltpu.make_async_remote_copy(src_ref=ins[t], dst_ref=outs[t].at[c], send_sem=ssem.at[t], recv_sem=rsem.at[t],
                                            device_id=(x, y, 1 - c), device_id_type=MESH) for t in range(nt)]
        for cp in local + cps:
            cp.start()
        for t in range(nt):
            pltpu.make_async_remote_copy(src_ref=ins[t], dst_ref=outs[t].at[1 - c], send_sem=ssem.at[t], recv_sem=rsem.at[t],
                                         device_id=(x, y, 1 - c), device_id_type=MESH).wait_recv()
        for cp in cps:
            cp.wait_send()
        for cp in local:
            cp.wait()

    dma = pltpu.SemaphoreType.DMA
    return pl.pallas_call(
        body, name="reduce_share", out_shape=[jax.ShapeDtypeStruct((L,) + g.shape, g.dtype) for g in red],
        in_specs=[ANY] * nt, out_specs=[ANY] * nt, scratch_shapes=[dma((nt,)), dma((nt,)), dma((nt,))],
    )(*red)


def _small_allreduce(buf):
    R, C = buf.shape
    n_dev = 8
    rp = R // n_dev
    rel = [(k >> 2 & 1, k >> 1 & 1, k & 1) for k in range(1, n_dev)]

    def body(in_ref, out_ref, recv, s1, r1, s2, r2):
        x, y, c, _ = _place()
        flip = lambda v, bit: 1 - v if bit else v
        peers = [(flip(x, kx), flip(y, ky), flip(c, kc)) for kx, ky, kc in rel]
        dev = lambda p: 4 * p[0] + 2 * p[1] + p[2]
        part = lambda ref, d: ref.at[pl.ds(pl.multiple_of(d * rp, SUBLANES), rp), :]
        me = dev((x, y, c))

        def scatter(k, src_dev, to):
            return pltpu.make_async_remote_copy(src_ref=part(in_ref, dev(to)), dst_ref=recv.at[src_dev], send_sem=s1.at[k],
                                                recv_sem=r1.at[k], device_id=to, device_id_type=MESH)

        def gather(k, src_dev, to):
            return pltpu.make_async_remote_copy(src_ref=part(out_ref, src_dev), dst_ref=part(out_ref, src_dev), send_sem=s2.at[k],
                                                recv_sem=r2.at[k], device_id=to, device_id_type=MESH)

        first = [scatter(k, me, p) for k, p in enumerate(peers)]
        for cp in first:
            cp.start()
        recv[me] = part(in_ref, me)[...]
        for k, p in enumerate(peers):
            scatter(k, dev(p), (x, y, c)).wait_recv()
        total = recv[0]
        for d in range(1, n_dev):
            total = total + recv[d]
        part(out_ref, me)[...] = total
        second = [gather(k, me, p) for k, p in enumerate(peers)]
        for cp in second:
            cp.start()
        for k, p in enumerate(peers):
            gather(k, dev(p), (x, y, c)).wait_recv()
        for cp in first + second:
            cp.wait_send()

    dma = pltpu.SemaphoreType.DMA
    vm = pl.BlockSpec(memory_space=pltpu.VMEM)
    return pl.pallas_call(
        body, name="small_allreduce", out_shape=jax.ShapeDtypeStruct((R, C), buf.dtype),
        in_specs=[vm], out_specs=vm,
        scratch_shapes=[pltpu.VMEM((n_dev, rp, C), buf.dtype), dma((n_dev - 1,)), dma((n_dev - 1,)), dma((n_dev - 1,)), dma((n_dev - 1,))],
    )(buf)


def _row_block(k):
    b = 256
    while k % b:
        b //= 2
    return b


def _add_halves(g, recv, c_arr, name):
    _, nj, K, N = g.shape
    bk = _row_block(K)

    def body(c_ref, g_ref, r_ref, o_ref, ob_ref):
        s = g_ref[...] + r_ref[...].astype(F32)
        o_ref[...] = s
        ob_ref[...] = s.astype(BF16)

    blk = pl.BlockSpec((None, bk, N), lambda j, i, c_ref: (j, i, 0))
    grid_spec = pltpu.PrefetchScalarGridSpec(
        num_scalar_prefetch=1, grid=(nj, K // bk),
        in_specs=[pl.BlockSpec((None, None, bk, N), lambda j, i, c_ref: (c_ref[0], j, i, 0)), blk],
        out_specs=[blk, blk])
    return pl.pallas_call(
        body, name=name, grid_spec=grid_spec,
        out_shape=[jax.ShapeDtypeStruct((nj, K, N), F32), jax.ShapeDtypeStruct((nj, K, N), BF16)],
        compiler_params=_params("parallel", "parallel"),
    )(c_arr, g, recv)


def _add_chips(pc, recv, chip_arr, name):
    _, K, N = pc.shape
    bk = _row_block(K)

    def body(k_ref, p_ref, r0_ref, r1_ref, r2_ref, o_ref):
        o_ref[...] = ((p_ref[...] + r0_ref[...].astype(F32)) + r1_ref[...].astype(F32)) + r2_ref[...].astype(F32)

    rspec = lambda j: pl.BlockSpec((None, bk, N), lambda i, k_ref: (j, i, 0))
    grid_spec = pltpu.PrefetchScalarGridSpec(
        num_scalar_prefetch=1, grid=(K // bk,),
        in_specs=[pl.BlockSpec((None, bk, N), lambda i, k_ref: (k_ref[0], i, 0)), rspec(0), rspec(1), rspec(2)],
        out_specs=pl.BlockSpec((bk, N), lambda i, k_ref: (i, 0)))
    return pl.pallas_call(
        body, name=name, grid_spec=grid_spec, out_shape=jax.ShapeDtypeStruct((K, N), F32),
        compiler_params=_params("parallel"),
    )(chip_arr, pc, recv, recv, recv)


def _adamw_math(w, g, m, v):
    m = ADAM_B1 * m + (1.0 - ADAM_B1) * g
    v = ADAM_B2 * v + (1.0 - ADAM_B2) * (g * g)
    m_hat = m / (1.0 - ADAM_B1 ** ADAM_STEP)
    v_hat = v / (1.0 - ADAM_B2 ** ADAM_STEP)
    delta = -ADAM_LR * (m_hat / (jnp.sqrt(v_hat) + ADAM_EPS) + ADAM_WD * w)
    return delta, m, v


def _adamw(w, g, m, v, name):
    L, K, N = w.shape
    bk = _row_block(K)

    def body(w_ref, g_ref, m_ref, v_ref, d_ref, nm_ref, nv_ref):
        d_ref[...], nm_ref[...], nv_ref[...] = _adamw_math(w_ref[...], g_ref[...], m_ref[...], v_ref[...])

    blk = pl.BlockSpec((None, bk, N), lambda l, i: (l, i, 0))
    sds = jax.ShapeDtypeStruct((L, K, N), F32)
    return pl.pallas_call(
        body, name=name, grid=(L, K // bk), in_specs=[blk] * 4, out_specs=[blk] * 3, out_shape=[sds] * 3,
        compiler_params=_params("parallel", "parallel"),
    )(w, g, m, v)


def _adamw_rows(w, g, m, v):
    def body(w_ref, g_ref, m_ref, v_ref, d_ref, nm_ref, nv_ref):
        d_ref[...], nm_ref[...], nv_ref[...] = _adamw_math(w_ref[...], g_ref[...], m_ref[...], v_ref[...])

    sds = jax.ShapeDtypeStruct(w.shape, F32)
    return pl.pallas_call(body, name="adamw_small", out_shape=[sds] * 3)(w, g, m, v)


def _cast_bf16(w, name):
    L, K, N = w.shape
    bk = _row_block(K)

    def body(w_ref, o_ref):
        o_ref[...] = w_ref[...].astype(BF16)

    blk = pl.BlockSpec((None, bk, N), lambda l, i: (l, i, 0))
    return pl.pallas_call(
        body, name=name, grid=(L, K // bk), in_specs=[blk], out_specs=blk, out_shape=jax.ShapeDtypeStruct((L, K, N), BF16),
        compiler_params=_params("parallel", "parallel"),
    )(w)


BIG = ("w_in", "w_out_a", "w_out_b", "w_o", "w_ffn_gate", "w_ffn_up", "w_ffn_down")
BIG_KEY = dict(w_in="win", w_out_a="woa", w_out_b="wob", w_o="wo", w_ffn_gate="wg", w_ffn_up="wu", w_ffn_down="wd")
SHARDED_SMALL = ("conv_a_w", "conv_b_w", "gate_bias")
REPLICATED = ("ln1_g", "conv_b_b", "lru_wa", "lru_ba", "lru_wx", "lru_bx", "lru_lambda", "ln2_g", "final_g")
WEIGHTS = ("ln1_g", "w_in", "conv_a_w", "conv_b_w", "conv_b_b", "lru_wa", "lru_ba", "lru_wx", "lru_bx", "lru_lambda",
           "w_out_a", "w_out_b", "gate_bias", "w_o", "ln2_g", "w_ffn_gate", "w_ffn_up", "w_ffn_down", "final_g")
LANES = 1024


def _pack_rows(arrays, row_multiple):
    flat = jnp.concatenate([a.reshape(-1) for a in arrays])
    rows = -(-flat.shape[0] // LANES)
    rows = -(-rows // row_multiple) * row_multiple
    flat = jnp.pad(flat, (0, rows * LANES - flat.shape[0]))
    return flat.reshape(rows, LANES)


def _unpack_rows(buf, shapes):
    flat = buf.reshape(-1)
    out, off = [], 0
    for s in shapes:
        n = 1
        for d in s:
            n *= d
        out.append(flat[off:off + n].reshape(s))
        off += n
    return out


def _step(w, m, v, x, target, tm, tm_mix, tk):
    xi, yi, ci = lax.axis_index("x"), lax.axis_index("y"), lax.axis_index("c")
    chip = _chip_id(xi, yi)
    c_arr = jnp.reshape(ci, (1,)).astype(jnp.int32)
    chip_arr = jnp.reshape(chip, (1,)).astype(jnp.int32)
    L = w["ln1_g"].shape[0]
    D = x.shape[1]
    dc = D // N_CHIP

    shards = [_cast_bf16(w[n], f"cast_{n}") for n in BIG]
    small_shard = jnp.concatenate([w[n] for n in SHARDED_SMALL], axis=1)
    *gathered, small_g = _allgather_weights(shards, small_shard)
    big = {}
    for n, g in zip(BIG, gathered):
        big[BIG_KEY[n]] = g.reshape(L, D, D) if n in ("w_out_a", "w_out_b", "w_o") else g
    small_full = jnp.transpose(small_g, (1, 2, 0, 3)).reshape(L, small_shard.shape[1], D)
    small = {n: w[n] for n in REPLICATED}
    off = 0
    for n in SHARDED_SMALL:
        k = w[n].shape[1]
        small[n] = small_full[:, off:off + k]
        off += k

    loss_row, grad_x, gbig, gsmall = _local_step(x, target, big, small, tm, tm_mix, tk)

    shape4 = lambda n: (L, N_CHIP) + w[n].shape[1:]
    g32 = [gbig[BIG_KEY[n]][0].reshape(shape4(n)) for n in BIG]
    g16 = [gbig[BIG_KEY[n]][1].reshape(shape4(n)) for n in BIG]
    from_sibling = _exchange_halves(g16)
    chip_sums = [_add_halves(g, r, c_arr, f"add_halves_{n}") for n, g, r in zip(BIG, g32, from_sibling)]
    from_chips = _scatter_chip_sums([s[1] for s in chip_sums])
    reduced = [_add_chips(s[0], r, chip_arr, f"add_chips_{n}") for n, s, r in zip(BIG, chip_sums, from_chips)]
    grads = dict(zip(BIG, _share_reduced(reduced)))

    order = [n for n in WEIGHTS if n not in BIG]
    packed = _pack_rows([gsmall[n] for n in order] + [loss_row], 8 * SUBLANES)
    summed = _small_allreduce(packed)
    parts = _unpack_rows(summed, [gsmall[n].shape for n in order] + [loss_row.shape])
    loss = jnp.sum(parts[-1])
    for n, g in zip(order, parts[:-1]):
        grads[n] = lax.dynamic_slice_in_dim(g, chip * dc, dc, axis=2) if n in SHARDED_SMALL else g

    delta, new_m, new_v = {}, {}, {}
    for n in BIG:
        delta[n], new_m[n], new_v[n] = _adamw(w[n], grads[n], m[n], v[n], f"adamw_{n}")
    pw, pg, pm, pv = (_pack_rows([d[n] for n in order], SUBLANES) for d in (w, grads, m, v))
    outs = _adamw_rows(pw, pg, pm, pv)
    shapes = [w[n].shape for n in order]
    for d, buf in zip((delta, new_m, new_v), outs):
        d.update(zip(order, _unpack_rows(buf, shapes)))
    return loss, grad_x, grads, delta, new_m, new_v


def kernel(x, ln1_g, w_in, conv_a_w, conv_b_w, conv_b_b, lru_wa, lru_ba, lru_wx, lru_bx, lru_lambda, w_out_a, w_out_b, gate_bias, w_o, ln2_g, w_ffn_gate, w_ffn_up, w_ffn_down, final_g, loss_target, m_ln1_g, m_w_in, m_conv_a_w, m_conv_b_w, m_conv_b_b, m_lru_wa, m_lru_ba, m_lru_wx, m_lru_bx, m_lru_lambda, m_w_out_a, m_w_out_b, m_gate_bias, m_w_o, m_ln2_g, m_w_ffn_gate, m_w_ffn_up, m_w_ffn_down, m_final_g, v_ln1_g, v_w_in, v_conv_a_w, v_conv_b_w, v_conv_b_b, v_lru_wa, v_lru_ba, v_lru_wx, v_lru_bx, v_lru_lambda, v_w_out_a, v_w_out_b, v_gate_bias, v_w_o, v_ln2_g, v_w_ffn_gate, v_w_ffn_up, v_w_ffn_down, v_final_g):
    w = dict(ln1_g=ln1_g, w_in=w_in, conv_a_w=conv_a_w, conv_b_w=conv_b_w, conv_b_b=conv_b_b, lru_wa=lru_wa, lru_ba=lru_ba,
             lru_wx=lru_wx, lru_bx=lru_bx, lru_lambda=lru_lambda, w_out_a=w_out_a, w_out_b=w_out_b, gate_bias=gate_bias, w_o=w_o,
             ln2_g=ln2_g, w_ffn_gate=w_ffn_gate, w_ffn_up=w_ffn_up, w_ffn_down=w_ffn_down, final_g=final_g)
    m = dict(ln1_g=m_ln1_g, w_in=m_w_in, conv_a_w=m_conv_a_w, conv_b_w=m_conv_b_w, conv_b_b=m_conv_b_b, lru_wa=m_lru_wa,
             lru_ba=m_lru_ba, lru_wx=m_lru_wx, lru_bx=m_lru_bx, lru_lambda=m_lru_lambda, w_out_a=m_w_out_a, w_out_b=m_w_out_b,
             gate_bias=m_gate_bias, w_o=m_w_o, ln2_g=m_ln2_g, w_ffn_gate=m_w_ffn_gate, w_ffn_up=m_w_ffn_up,
             w_ffn_down=m_w_ffn_down, final_g=m_final_g)
    v = dict(ln1_g=v_ln1_g, w_in=v_w_in, conv_a_w=v_conv_a_w, conv_b_w=v_conv_b_w, conv_b_b=v_conv_b_b, lru_wa=v_lru_wa,
             lru_ba=v_lru_ba, lru_wx=v_lru_wx, lru_bx=v_lru_bx, lru_lambda=v_lru_lambda, w_out_a=v_w_out_a, w_out_b=v_w_out_b,
             gate_bias=v_gate_bias, w_o=v_w_o, ln2_g=v_ln2_g, w_ffn_gate=v_w_ffn_gate, w_ffn_up=v_w_ffn_up,
             w_ffn_down=v_w_ffn_down, final_g=v_final_g)
    T = x.shape[1]
    tm = min(512, T)
    tm_mix = min(256, T)
    loss, grad_x, grads, delta, new_m, new_v = _step(w, m, v, x[0], loss_target[0], tm, tm_mix, tm)
    return (loss, grad_x[None], *[grads[n] for n in WEIGHTS], *[delta[n] for n in WEIGHTS],
            *[new_m[n] for n in WEIGHTS], *[new_v[n] for n in WEIGHTS])
```

```python
import functools

import jax
import jax.numpy as jnp
from jax import lax
from jax.experimental import pallas as pl
from jax.experimental.pallas import tpu as pltpu

F32 = jnp.float32
BF16 = jnp.bfloat16
MESH = pl.DeviceIdType.MESH

N_CHIP = 4
RMS_EPS = 1e-6
LRU_C = 8.0
LRU_HEAD_DIM = 64
LRU_BLOCK = 256
CONV_A_K = 3
CONV_B_K = 4
ADAM_LR = 0.001
ADAM_B1 = 0.9
ADAM_B2 = 0.999
ADAM_EPS = 1e-08
ADAM_WD = 0.01
ADAM_STEP = 10
SUBLANES = 8
VMEM_LIMIT = 56 * 1024 * 1024


def _params(*sem):
    return pltpu.CompilerParams(dimension_semantics=sem, vmem_limit_bytes=VMEM_LIMIT)


def _sigmoid(v):
    return 1.0 / (1.0 + jnp.exp(-v))


def _one_minus_sq(la, a):
    return jnp.tanh(-la) * (1.0 + a * a)


def _gelu_parts(v):
    k = 0.7978845608028654
    v2 = v * v
    t = jnp.tanh(k * (v + 0.044715 * v * v2))
    gelu = 0.5 * v * (1.0 + t)
    dgelu = 0.5 * (1.0 + t) + 0.5 * v * (1.0 - t * t) * k * (1.0 + 3 * 0.044715 * v2)
    return gelu, dgelu


def _shift_down(v, k, prev8):
    rolled = pltpu.roll(v, k, 0)
    r8 = lax.broadcasted_iota(jnp.int32, prev8.shape, 0)
    head = jnp.where(r8 < k, pltpu.roll(prev8, k, 0), rolled[0:SUBLANES])
    return jnp.concatenate([head, rolled[SUBLANES:]], axis=0)


def _shift_up(v, k, next8):
    tm = v.shape[0]
    rolled = pltpu.roll(v, tm - k, 0)
    r8 = lax.broadcasted_iota(jnp.int32, next8.shape, 0)
    tail = jnp.where(r8 >= SUBLANES - k, pltpu.roll(next8, SUBLANES - k, 0), rolled[tm - SUBLANES:])
    return jnp.concatenate([rolled[:tm - SUBLANES], tail], axis=0)


def _colsum8(v):
    tm, c = v.shape
    return jnp.sum(v.reshape(tm // SUBLANES, SUBLANES, c), axis=0)


def _rms_stats(xv):
    var = jnp.mean(xv * xv, axis=-1, keepdims=True)
    return lax.rsqrt(var + RMS_EPS)


def _rms_bwd(dh, xv, g):
    rstd = _rms_stats(xv)
    xhat = xv * rstd
    dxhat = dh * g
    dx = rstd * (dxhat - xhat * jnp.mean(dxhat * xhat, axis=-1, keepdims=True))
    return dx, dh * xhat


def _rms_inproj(x, g_row, win, layer, tm):
    T, D = x.shape
    ns = win.shape[-1]

    def body(x_ref, g_ref, w_ref, p_ref, h_ref):
        @pl.when(pl.program_id(1) == 0)
        def _():
            xv = x_ref[...]
            h_ref[...] = (xv * _rms_stats(xv) * g_ref[...]).astype(BF16)
        p_ref[...] = jnp.dot(h_ref[...], w_ref[...], preferred_element_type=F32).astype(BF16)

    return pl.pallas_call(
        body, name=f"rms_inproj_{layer}", grid=(T // tm, N_CHIP),
        in_specs=[pl.BlockSpec((tm, D), lambda i, j: (i, 0)),
                  pl.BlockSpec((1, D), lambda i, j: (0, 0)),
                  pl.BlockSpec((None, None, D, ns), lambda i, j: (layer, j, 0, 0))],
        out_specs=[pl.BlockSpec((tm, ns), lambda i, j: (i, j)),
                   pl.BlockSpec((tm, D), lambda i, j: (i, 0))],
        out_shape=[jax.ShapeDtypeStruct((T, N_CHIP * ns), BF16), jax.ShapeDtypeStruct((T, D), BF16)],
        compiler_params=_params("parallel", "arbitrary"),
    )(x, g_row, win)


def _mixer_recompute(ca, xa, xb, zprev8, xbprev8, cw_ref, wab_ref, sp):
    row = lambda k: cw_ref[pl.ds(k, 1), :]
    z = ca * xa
    z1 = _shift_down(z, 1, zprev8)
    z2 = _shift_down(z, 2, zprev8)
    cz = row(2) * z + row(1) * z1 + row(0) * z2
    x1 = _shift_down(xb, 1, xbprev8)
    x2 = _shift_down(xb, 2, xbprev8)
    x3 = _shift_down(xb, 3, xbprev8)
    u = row(6) * xb + row(5) * x1 + row(4) * x2 + row(3) * x3 + row(7)
    ub = u.astype(BF16)
    nb = wab_ref.shape[0]
    ras, ixs = [], []
    for b in range(nb):
        ri = jnp.dot(ub[:, b * LRU_BLOCK:(b + 1) * LRU_BLOCK], wab_ref[b], preferred_element_type=F32)
        ras.append(ri[:, :LRU_BLOCK])
        ixs.append(ri[:, LRU_BLOCK:])
    r = _sigmoid(jnp.concatenate(ras, axis=1) + row(8))
    gi = _sigmoid(jnp.concatenate(ixs, axis=1) + row(9))
    la = (-LRU_C) * r * sp
    a = jnp.exp(la)
    m = jnp.sqrt(_one_minus_sq(la, a))
    return dict(z=z, z1=z1, z2=z2, cz=cz, x1=x1, x2=x2, x3=x3, u=u, ub=ub, r=r, gi=gi, a=a, m=m)


def _softplus_neg(lam):
    v = -lam
    return jnp.maximum(v, 0.0) + jnp.log1p(jnp.exp(-jnp.abs(v)))


def _mixer_fwd(p, cw, wab, lam_row, layer, tm):
    T = p.shape[0]
    D = p.shape[1] // 7
    ngroups = tm // SUBLANES

    def body(ba_ref, ca_ref, xa_ref, xb_ref, gb_ref, cw_ref, wab_ref, lam_ref, ya_ref, yb_ref, h_ref,
             zprev, xbprev, hcarry, a_s, h_s):
        @pl.when(pl.program_id(0) == 0)
        def _():
            zprev[...] = jnp.zeros_like(zprev)
            xbprev[...] = jnp.zeros_like(xbprev)
            hcarry[...] = jnp.zeros_like(hcarry)

        ca = ca_ref[...].astype(F32)
        xa = xa_ref[...].astype(F32)
        xb = xb_ref[...].astype(F32)
        sp = _softplus_neg(lam_ref[...])
        c = _mixer_recompute(ca, xa, xb, zprev[...], xbprev[...], cw_ref, wab_ref, sp)
        zprev[...] = c["z"][tm - SUBLANES:]
        xbprev[...] = xb[tm - SUBLANES:]
        ya_ref[...] = (ba_ref[...].astype(F32) * c["cz"]).astype(BF16)

        A = c["a"]
        B = c["m"] * c["gi"] * c["u"]
        q = lax.broadcasted_iota(jnp.int32, A.shape, 0) & (SUBLANES - 1)
        for s in (1, 2, 4):
            msk = q >= s
            B = jnp.where(msk, A * pltpu.roll(B, s, 0) + B, B)
            A = jnp.where(msk, A * pltpu.roll(A, s, 0), A)
        a_s[...] = A
        h_s[...] = B

        def step(g, carry):
            off = pl.multiple_of(g * SUBLANES, SUBLANES)
            hg = h_s[pl.ds(off, SUBLANES), :] + a_s[pl.ds(off, SUBLANES), :] * carry
            h_s[pl.ds(off, SUBLANES), :] = hg
            return jnp.broadcast_to(hg[SUBLANES - 1:SUBLANES, :], hg.shape)

        hcarry[...] = lax.fori_loop(0, ngroups, step, hcarry[...], unroll=4)
        h = h_s[...]
        h_ref[...] = h
        gelu, _ = _gelu_parts(gb_ref[...].astype(F32))
        yb_ref[...] = (h * gelu).astype(BF16)

    col = lambda k: pl.BlockSpec((tm, D), lambda i: (i, k))
    full = lambda a: pl.BlockSpec(a.shape, lambda i: (0,) * a.ndim)
    tok = pl.BlockSpec((tm, D), lambda i: (i, 0))
    return pl.pallas_call(
        body, name=f"mixer_fwd_{layer}", grid=(T // tm,),
        in_specs=[col(0), col(1), col(2), col(3), col(4), full(cw), full(wab), full(lam_row)],
        out_specs=[tok, tok, tok],
        out_shape=[jax.ShapeDtypeStruct((T, D), BF16), jax.ShapeDtypeStruct((T, D), BF16), jax.ShapeDtypeStruct((T, D), F32)],
        scratch_shapes=[pltpu.VMEM((SUBLANES, D), F32), pltpu.VMEM((SUBLANES, D), F32), pltpu.VMEM((SUBLANES, D), F32),
                        pltpu.VMEM((tm, D), F32), pltpu.VMEM((tm, D), F32)],
        compiler_params=_params("arbitrary"),
    )(p, p, p, p, p, cw, wab, lam_row)


def _merge_fwd(x, p, ya, yb, woa, wob, wo, gbias, layer, tm):
    T, D = x.shape

    def body(x_ref, ga_ref, gb_ref, ya_ref, yb_ref, woa_ref, wob_ref, wo_ref, bias_ref, oa_ref, ob_ref, mg_ref, x1_ref):
        oa = jnp.dot(ya_ref[...], woa_ref[...], preferred_element_type=F32)
        ob = jnp.dot(yb_ref[...], wob_ref[...], preferred_element_type=F32)
        sa = _sigmoid(ga_ref[...].astype(F32) + bias_ref[pl.ds(0, 1), :])
        sb = _sigmoid(gb_ref[...].astype(F32) + bias_ref[pl.ds(1, 1), :])
        mg = (sa * oa + sb * ob).astype(BF16)
        oa_ref[...] = oa.astype(BF16)
        ob_ref[...] = ob.astype(BF16)
        mg_ref[...] = mg
        x1_ref[...] = x_ref[...] + jnp.dot(mg, wo_ref[...], preferred_element_type=F32)

    tok = pl.BlockSpec((tm, D), lambda i: (i, 0))
    wsp = pl.BlockSpec((None, D, D), lambda i: (layer, 0, 0))
    bf = jax.ShapeDtypeStruct((T, D), BF16)
    return pl.pallas_call(
        body, name=f"merge_fwd_{layer}", grid=(T // tm,),
        in_specs=[tok, pl.BlockSpec((tm, D), lambda i: (i, 5)), pl.BlockSpec((tm, D), lambda i: (i, 6)), tok, tok,
                  wsp, wsp, wsp, pl.BlockSpec(gbias.shape, lambda i: (0, 0))],
        out_specs=[tok, tok, tok, tok],
        out_shape=[bf, bf, bf, jax.ShapeDtypeStruct((T, D), F32)],
        compiler_params=_params("parallel"),
    )(x, p, p, ya, yb, woa, wob, wo, gbias)


def _ffn_fwd(x1, g_row, wg, wu, wd, layer, tm):
    T, D = x1.shape
    fs = wg.shape[-1]

    def body(x_ref, g_ref, wg_ref, wu_ref, wd_ref, h_ref, gg_ref, uu_ref, x2_ref, acc):
        j = pl.program_id(1)

        @pl.when(j == 0)
        def _():
            xv = x_ref[...]
            h_ref[...] = (xv * _rms_stats(xv) * g_ref[...]).astype(BF16)
            acc[...] = xv

        h = h_ref[...]
        gg = jnp.dot(h, wg_ref[...], preferred_element_type=F32)
        uu = jnp.dot(h, wu_ref[...], preferred_element_type=F32)
        gg_ref[...] = gg.astype(BF16)
        uu_ref[...] = uu.astype(BF16)
        act = (gg * _sigmoid(gg) * uu).astype(BF16)
        acc[...] += jnp.dot(act, wd_ref[...], preferred_element_type=F32)

        @pl.when(j == N_CHIP - 1)
        def _():
            x2_ref[...] = acc[...]

    tok = pl.BlockSpec((tm, D), lambda i, j: (i, 0))
    cm = pl.BlockSpec((None, tm, fs), lambda i, j: (j, i, 0))
    return pl.pallas_call(
        body, name=f"ffn_fwd_{layer}", grid=(T // tm, N_CHIP),
        in_specs=[tok, pl.BlockSpec((1, D), lambda i, j: (0, 0)),
                  pl.BlockSpec((None, None, D, fs), lambda i, j: (layer, j, 0, 0)),
                  pl.BlockSpec((None, None, D, fs), lambda i, j: (layer, j, 0, 0)),
                  pl.BlockSpec((None, None, fs, D), lambda i, j: (layer, j, 0, 0))],
        out_specs=[tok, cm, cm, tok],
        out_shape=[jax.ShapeDtypeStruct((T, D), BF16), jax.ShapeDtypeStruct((N_CHIP, T, fs), BF16),
                   jax.ShapeDtypeStruct((N_CHIP, T, fs), BF16), jax.ShapeDtypeStruct((T, D), F32)],
        scratch_shapes=[pltpu.VMEM((tm, D), F32)],
        compiler_params=_params("parallel", "arbitrary"),
    )(x1, g_row, wg, wu, wd)


def _final_loss(x, g_row, target, tm):
    T, D = x.shape
    n = T // tm

    def body(x_ref, g_ref, t_ref, dx_ref, red_ref, acc):
        i = pl.program_id(0)

        @pl.when(i == 0)
        def _():
            acc[...] = jnp.zeros_like(acc)

        xv = x_ref[...]
        g = g_ref[...]
        rstd = _rms_stats(xv)
        xhat = xv * rstd
        err = xhat * g - t_ref[...]
        dy = err * (1.0 / D)
        dxhat = dy * g
        dx_ref[...] = rstd * (dxhat - xhat * jnp.mean(dxhat * xhat, axis=-1, keepdims=True))
        acc[0] += _colsum8(err * err)
        acc[1] += _colsum8(dy * xhat)

        @pl.when(i == n - 1)
        def _():
            red_ref[pl.ds(0, 1), :] = jnp.sum(acc[0], axis=0, keepdims=True) * (0.5 / D)
            red_ref[pl.ds(1, 1), :] = jnp.sum(acc[1], axis=0, keepdims=True)

    tok = pl.BlockSpec((tm, D), lambda i: (i, 0))
    return pl.pallas_call(
        body, name="final_loss", grid=(n,),
        in_specs=[tok, pl.BlockSpec((1, D), lambda i: (0, 0)), tok],
        out_specs=[tok, pl.BlockSpec((2, D), lambda i: (0, 0))],
        out_shape=[jax.ShapeDtypeStruct((T, D), F32), jax.ShapeDtypeStruct((2, D), F32)],
        scratch_shapes=[pltpu.VMEM((2, SUBLANES, D), F32)],
        compiler_params=_params("arbitrary"),
    )(x, g_row, target)


def _ffn_bwd(dx2, x1, g_row, gg, uu, wg, wu, wd, layer, tm):
    T, D = dx2.shape
    fs = wg.shape[-1]
    n = T // tm
    nt = (((1,), (1,)), ((), ()))

    def body(dx_ref, x_ref, g_ref, gg_ref, uu_ref, wg_ref, wu_ref, wd_ref, dg_ref, du_ref, act_ref, dx1_ref, dxb_ref, red_ref,
             acc, racc):
        i = pl.program_id(0)
        j = pl.program_id(1)

        @pl.when((i == 0) & (j == 0))
        def _():
            racc[...] = jnp.zeros_like(racc)

        @pl.when(j == 0)
        def _():
            dxb_ref[...] = dx_ref[...].astype(BF16)
            acc[...] = jnp.zeros_like(acc)

        dact = lax.dot_general(dxb_ref[...], wd_ref[...], nt, preferred_element_type=F32)
        g = gg_ref[...].astype(F32)
        u = uu_ref[...].astype(F32)
        s = _sigmoid(g)
        silu = g * s
        dg = (dact * u * (s * (1.0 + g * (1.0 - s)))).astype(BF16)
        du = (dact * silu).astype(BF16)
        dg_ref[...] = dg
        du_ref[...] = du
        act_ref[...] = (silu * u).astype(BF16)
        acc[...] += (lax.dot_general(dg, wg_ref[...], nt, preferred_element_type=F32)
                     + lax.dot_general(du, wu_ref[...], nt, preferred_element_type=F32))

        @pl.when(j == N_CHIP - 1)
        def _():
            dx, dgain = _rms_bwd(acc[...], x_ref[...], g_ref[...])
            dx1_ref[...] = dx_ref[...] + dx
            racc[...] += _colsum8(dgain)

        @pl.when((i == n - 1) & (j == N_CHIP - 1))
        def _():
            red_ref[...] = jnp.sum(racc[...], axis=0, keepdims=True)

    tok = pl.BlockSpec((tm, D), lambda i, j: (i, 0))
    cm = pl.BlockSpec((None, tm, fs), lambda i, j: (j, i, 0))
    cms = jax.ShapeDtypeStruct((N_CHIP, T, fs), BF16)
    return pl.pallas_call(
        body, name=f"ffn_bwd_{layer}", grid=(n, N_CHIP),
        in_specs=[tok, tok, pl.BlockSpec((1, D), lambda i, j: (0, 0)), cm, cm,
                  pl.BlockSpec((None, None, D, fs), lambda i, j: (layer, j, 0, 0)),
                  pl.BlockSpec((None, None, D, fs), lambda i, j: (layer, j, 0, 0)),
                  pl.BlockSpec((None, None, fs, D), lambda i, j: (layer, j, 0, 0))],
        out_specs=[cm, cm, cm, tok, tok, pl.BlockSpec((1, D), lambda i, j: (0, 0))],
        out_shape=[cms, cms, cms, jax.ShapeDtypeStruct((T, D), F32), jax.ShapeDtypeStruct((T, D), BF16),
                   jax.ShapeDtypeStruct((1, D), F32)],
        scratch_shapes=[pltpu.VMEM((tm, D), F32), pltpu.VMEM((SUBLANES, D), F32)],
        compiler_params=_params("arbitrary", "arbitrary"),
    )(dx2, x1, g_row, gg, uu, wg, wu, wd)


def _merge_bwd(dx1, p, oa, ob, woa, wob, wo, gbias, layer, tm):
    T, D = dx1.shape
    n = T // tm
    nt = (((1,), (1,)), ((), ()))

    def body(dx_ref, ga_ref, gb_ref, oa_ref, ob_ref, woa_ref, wob_ref, wo_ref, bias_ref,
             dya_ref, dyb_ref, doa_ref, dob_ref, dgl_ref, dxb_ref, red_ref, racc):
        i = pl.program_id(0)

        @pl.when(i == 0)
        def _():
            racc[...] = jnp.zeros_like(racc)

        dxb = dx_ref[...].astype(BF16)
        dxb_ref[...] = dxb
        dm = lax.dot_general(dxb, wo_ref[...], nt, preferred_element_type=F32)
        sa = _sigmoid(ga_ref[...].astype(F32) + bias_ref[pl.ds(0, 1), :])
        sb = _sigmoid(gb_ref[...].astype(F32) + bias_ref[pl.ds(1, 1), :])
        doa = (dm * sa).astype(BF16)
        dob = (dm * sb).astype(BF16)
        dga = dm * oa_ref[...].astype(F32) * (sa * (1.0 - sa))
        dgb = dm * ob_ref[...].astype(F32) * (sb * (1.0 - sb))
        doa_ref[...] = doa
        dob_ref[...] = dob
        dgl_ref[:, 0:D] = dga.astype(BF16)
        dgl_ref[:, D:2 * D] = dgb.astype(BF16)
        racc[0] += _colsum8(dga)
        racc[1] += _colsum8(dgb)
        dya_ref[...] = lax.dot_general(doa, woa_ref[...], nt, preferred_element_type=F32).astype(BF16)
        dyb_ref[...] = lax.dot_general(dob, wob_ref[...], nt, preferred_element_type=F32).astype(BF16)

        @pl.when(i == n - 1)
        def _():
            red_ref[pl.ds(0, 1), :] = jnp.sum(racc[0], axis=0, keepdims=True)
            red_ref[pl.ds(1, 1), :] = jnp.sum(racc[1], axis=0, keepdims=True)

    tok = pl.BlockSpec((tm, D), lambda i: (i, 0))
    wsp = pl.BlockSpec((None, D, D), lambda i: (layer, 0, 0))
    bf = jax.ShapeDtypeStruct((T, D), BF16)
    return pl.pallas_call(
        body, name=f"merge_bwd_{layer}", grid=(n,),
        in_specs=[tok, pl.BlockSpec((tm, D), lambda i: (i, 5)), pl.BlockSpec((tm, D), lambda i: (i, 6)), tok, tok,
                  wsp, wsp, wsp, pl.BlockSpec(gbias.shape, lambda i: (0, 0))],
        out_specs=[tok, tok, tok, tok, pl.BlockSpec((tm, 2 * D), lambda i: (i, 0)), tok, pl.BlockSpec((2, D), lambda i: (0, 0))],
        out_shape=[bf, bf, bf, bf, jax.ShapeDtypeStruct((T, 2 * D), BF16), bf, jax.ShapeDtypeStruct((2, D), F32)],
        scratch_shapes=[pltpu.VMEM((2, SUBLANES, D), F32)],
        compiler_params=_params("arbitrary"),
    )(dx1, p, p, oa, ob, woa, wob, wo, gbias)


N_MIXER_RED = 16


def _mixer_bwd(p, hseq, dya, dyb, dgl, cw, wab, wabt, lam_row, layer, tm):
    T = p.shape[0]
    D = p.shape[1] // 7
    n = T // tm
    ngroups = tm // SUBLANES
    nb = wab.shape[0]
    hb = 16
    tn = (((0,), (0,)), ((), ()))

    def body(ba_ref, ca_ref, xa_ref, xb_ref, gb_ref, h_ref, dya_ref, dyb_ref, dgl_ref,
             cap_ref, xap_ref, xbp_ref, hp_ref, ban_ref, dyan_ref,
             cw_ref, wab_ref, wabt_ref, lam_ref,
             dp_ref, red_ref, dwab_ref,
             racc, wacc, anext, gnext, dunext, c_s, g_s):
        i = pl.program_id(0)
        first_tile = i == n - 1
        last_tile = i == 0

        @pl.when(i == 0)
        def _():
            racc[...] = jnp.zeros_like(racc)
            wacc[...] = jnp.zeros_like(wacc)
            anext[...] = jnp.zeros_like(anext)
            gnext[...] = jnp.zeros_like(gnext)
            dunext[...] = jnp.zeros_like(dunext)

        keep_prev = jnp.where(first_tile, 0.0, 1.0)
        keep_next = jnp.where(last_tile, 0.0, 1.0)
        ba = ba_ref[...].astype(F32)
        ca = ca_ref[...].astype(F32)
        xa = xa_ref[...].astype(F32)
        xb = xb_ref[...].astype(F32)
        h = h_ref[...]
        dya = dya_ref[...].astype(F32)
        dyb = dyb_ref[...].astype(F32)
        zprev8 = (cap_ref[...].astype(F32) * xap_ref[...].astype(F32))[hb - SUBLANES:] * keep_prev
        xbprev8 = xbp_ref[...].astype(F32)[hb - SUBLANES:] * keep_prev
        hprev8 = hp_ref[...] * keep_prev
        dcznext8 = (dyan_ref[...].astype(F32) * ban_ref[...].astype(F32))[:SUBLANES] * keep_next

        lam = lam_ref[...]
        sp = _softplus_neg(lam)
        c = _mixer_recompute(ca, xa, xb, zprev8, xbprev8, cw_ref, wab_ref, sp)
        row = lambda k: cw_ref[pl.ds(k, 1), :]
        a, m, r, gi, u = c["a"], c["m"], c["r"], c["gi"], c["u"]

        gelu, dgelu = _gelu_parts(gb_ref[...].astype(F32))
        dgb = dyb * h * dgelu
        Dv = dyb * gelu
        Cv = _shift_up(a, 1, anext[...])
        q = lax.broadcasted_iota(jnp.int32, Cv.shape, 0) & (SUBLANES - 1)
        for s in (1, 2, 4):
            msk = q < SUBLANES - s
            Dv = jnp.where(msk, Dv + Cv * pltpu.roll(Dv, tm - s, 0), Dv)
            Cv = jnp.where(msk, Cv * pltpu.roll(Cv, tm - s, 0), Cv)
        c_s[...] = Cv
        g_s[...] = Dv

        def step(k, carry):
            off = pl.multiple_of((ngroups - 1 - k) * SUBLANES, SUBLANES)
            gg = g_s[pl.ds(off, SUBLANES), :] + c_s[pl.ds(off, SUBLANES), :] * carry
            g_s[pl.ds(off, SUBLANES), :] = gg
            return jnp.broadcast_to(gg[0:1, :], gg.shape)

        gnext[...] = lax.fori_loop(0, ngroups, step, gnext[...], unroll=4)
        anext[...] = a[0:SUBLANES]
        g = g_s[...]

        hprev = _shift_down(h, 1, hprev8)
        da = g * hprev
        gm = g * m
        dgi = gm * u
        du = gm * gi
        dmv = g * gi * u
        dla = a * (da - dmv * a / m)
        dra = dla * ((-LRU_C) * sp) * (r * (1.0 - r))
        dix = dgi * (gi * (1.0 - gi))
        racc[10] += _colsum8(dla * r)
        racc[8] += _colsum8(dra)
        racc[9] += _colsum8(dix)
        drab = dra.astype(BF16)
        dixb = dix.astype(BF16)
        ub = c["ub"]
        dus = []
        for b in range(nb):
            sl = slice(b * LRU_BLOCK, (b + 1) * LRU_BLOCK)
            dri = jnp.concatenate([drab[:, sl], dixb[:, sl]], axis=1)
            dus.append(jnp.dot(dri, wabt_ref[b], preferred_element_type=F32))
            wacc[b] += lax.dot_general(ub[:, sl], dri, tn, preferred_element_type=F32)
        du = du + jnp.concatenate(dus, axis=1)

        dun = dunext[...]
        du1 = _shift_up(du, 1, dun)
        du2 = _shift_up(du, 2, dun)
        du3 = _shift_up(du, 3, dun)
        dxb = row(6) * du + row(5) * du1 + row(4) * du2 + row(3) * du3
        dunext[...] = du[0:SUBLANES]
        racc[6] += _colsum8(du * xb)
        racc[5] += _colsum8(du * c["x1"])
        racc[4] += _colsum8(du * c["x2"])
        racc[3] += _colsum8(du * c["x3"])
        racc[7] += _colsum8(du)

        dba = dya * c["cz"]
        dcz = dya * ba
        dcz1 = _shift_up(dcz, 1, dcznext8)
        dcz2 = _shift_up(dcz, 2, dcznext8)
        dz = row(2) * dcz + row(1) * dcz1 + row(0) * dcz2
        racc[2] += _colsum8(dcz * c["z"])
        racc[1] += _colsum8(dcz * c["z1"])
        racc[0] += _colsum8(dcz * c["z2"])

        dp_ref[:, 0:D] = dba.astype(BF16)
        dp_ref[:, D:2 * D] = (dz * xa).astype(BF16)
        dp_ref[:, 2 * D:3 * D] = (dz * ca).astype(BF16)
        dp_ref[:, 3 * D:4 * D] = dxb.astype(BF16)
        dp_ref[:, 4 * D:5 * D] = dgb.astype(BF16)
        dp_ref[:, 5 * D:7 * D] = dgl_ref[...]

        @pl.when(i == n - 1)
        def _():
            dlam_scale = LRU_C * _sigmoid(-lam)
            for k in range(N_MIXER_RED):
                tot = jnp.sum(racc[k], axis=0, keepdims=True)
                red_ref[pl.ds(k, 1), :] = tot * dlam_scale if k == 10 else tot
            dwab_ref[...] = wacc[...]

    rt = lambda i: n - 1 - i
    col = lambda k: pl.BlockSpec((tm, D), lambda i: (rt(i), k))
    tok = pl.BlockSpec((tm, D), lambda i: (rt(i), 0))
    full = lambda a: pl.BlockSpec(a.shape, lambda i: (0,) * a.ndim)
    prev16 = lambda k: pl.BlockSpec((hb, D), lambda i: (jnp.maximum(rt(i) * (tm // hb) - 1, 0), k))
    next16 = lambda k: pl.BlockSpec((hb, D), lambda i: (jnp.minimum((rt(i) + 1) * (tm // hb), T // hb - 1), k))
    hprev = pl.BlockSpec((SUBLANES, D), lambda i: (jnp.maximum(rt(i) * ngroups - 1, 0), 0))
    return pl.pallas_call(
        body, name=f"mixer_bwd_{layer}", grid=(n,),
        in_specs=[col(0), col(1), col(2), col(3), col(4), tok, tok, tok, pl.BlockSpec((tm, 2 * D), lambda i: (rt(i), 0)),
                  prev16(1), prev16(2), prev16(3), hprev, next16(0), next16(0),
                  full(cw), full(wab), full(wabt), full(lam_row)],
        out_specs=[pl.BlockSpec((tm, 7 * D), lambda i: (rt(i), 0)),
                   pl.BlockSpec((N_MIXER_RED, D), lambda i: (0, 0)),
                   pl.BlockSpec((nb, LRU_BLOCK, 2 * LRU_BLOCK), lambda i: (0, 0, 0))],
        out_shape=[jax.ShapeDtypeStruct((T, 7 * D), BF16), jax.ShapeDtypeStruct((N_MIXER_RED, D), F32),
                   jax.ShapeDtypeStruct((nb, LRU_BLOCK, 2 * LRU_BLOCK), F32)],
        scratch_shapes=[pltpu.VMEM((N_MIXER_RED, SUBLANES, D), F32), pltpu.VMEM((nb, LRU_BLOCK, 2 * LRU_BLOCK), F32),
                        pltpu.VMEM((SUBLANES, D), F32), pltpu.VMEM((SUBLANES, D), F32), pltpu.VMEM((SUBLANES, D), F32),
                        pltpu.VMEM((tm, D), F32), pltpu.VMEM((tm, D), F32)],
        compiler_params=_params("arbitrary"),
    )(p, p, p, p, p, hseq, dya, dyb, dgl, p, p, p, hseq, p, dya, cw, wab, wabt, lam_row)


def _inproj_bwd(dp, dx1, x, g_row, win, layer, tm):
    T, D = x.shape
    ns = win.shape[-1]
    n = T // tm
    nt = (((1,), (1,)), ((), ()))

    def body(dp_ref, dx_ref, x_ref, g_ref, w_ref, dx0_ref, red_ref, acc, racc):
        i = pl.program_id(0)
        j = pl.program_id(1)

        @pl.when((i == 0) & (j == 0))
        def _():
            racc[...] = jnp.zeros_like(racc)

        @pl.when(j == 0)
        def _():
            acc[...] = jnp.zeros_like(acc)

        acc[...] += lax.dot_general(dp_ref[...], w_ref[...], nt, preferred_element_type=F32)

        @pl.when(j == N_CHIP - 1)
        def _():
            dx, dgain = _rms_bwd(acc[...], x_ref[...], g_ref[...])
            dx0_ref[...] = dx_ref[...] + dx
            racc[...] += _colsum8(dgain)

        @pl.when((i == n - 1) & (j == N_CHIP - 1))
        def _():
            red_ref[...] = jnp.sum(racc[...], axis=0, keepdims=True)

    tok = pl.BlockSpec((tm, D), lambda i, j: (i, 0))
    return pl.pallas_call(
        body, name=f"inproj_bwd_{layer}", grid=(n, N_CHIP),
        in_specs=[pl.BlockSpec((tm, ns), lambda i, j: (i, j)), tok, tok, pl.BlockSpec((1, D), lambda i, j: (0, 0)),
                  pl.BlockSpec((None, None, D, ns), lambda i, j: (layer, j, 0, 0))],
        out_specs=[tok, pl.BlockSpec((1, D), lambda i, j: (0, 0))],
        out_shape=[jax.ShapeDtypeStruct((T, D), F32), jax.ShapeDtypeStruct((1, D), F32)],
        scratch_shapes=[pltpu.VMEM((tm, D), F32), pltpu.VMEM((SUBLANES, D), F32)],
        compiler_params=_params("arbitrary", "arbitrary"),
    )(dp, dx1, x, g_row, win)


def _wgrad(a, b, name, tk, layer, n_layers, prev=None, a_kind="whole", b_kind="whole", nj=1):
    T = a.shape[-2]
    width = lambda v, kind: v.shape[-1] // nj if kind == "cols" else v.shape[-1]
    ka, kb = width(a, a_kind), width(b, b_kind)
    nt = T // tk
    tn = (((0,), (0,)), ((), ()))

    def spec(k, kind):
        if kind == "cm":
            return pl.BlockSpec((None, tk, k), lambda j, t: (j, t, 0))
        if kind == "cols":
            return pl.BlockSpec((tk, k), lambda j, t: (t, j))
        return pl.BlockSpec((tk, k), lambda j, t: (t, 0))

    def body(a_ref, b_ref, *rest):
        o_ref, ob_ref = rest[-2:]
        t = pl.program_id(1)

        @pl.when(t == 0)
        def _():
            o_ref[...] = jnp.zeros_like(o_ref)

        o_ref[...] += lax.dot_general(a_ref[...], b_ref[...], tn, preferred_element_type=F32)

        @pl.when(t == nt - 1)
        def _():
            ob_ref[...] = o_ref[...].astype(BF16)

    o_spec = pl.BlockSpec((None, None, ka, kb), lambda j, t: (layer, j, 0, 0))
    in_specs = [spec(ka, a_kind), spec(kb, b_kind)]
    operands = [a, b]
    aliases = {}
    if prev is not None:
        in_specs += [pl.BlockSpec(memory_space=pl.ANY)] * 2
        operands += list(prev)
        aliases = {2: 0, 3: 1}
    return pl.pallas_call(
        body, name=f"{name}_{layer}", grid=(nj, nt),
        in_specs=in_specs, out_specs=[o_spec, o_spec],
        out_shape=[jax.ShapeDtypeStruct((n_layers, nj, ka, kb), F32), jax.ShapeDtypeStruct((n_layers, nj, ka, kb), BF16)],
        input_output_aliases=aliases,
        compiler_params=_params("parallel", "arbitrary"),
    )(*operands)


def _wgrad_pair(a, b1, b2, name, tk, layer, n_layers, prev=None):
    T, ka = a.shape
    nj, _, kb = b1.shape
    nt = T // tk
    tn = (((0,), (0,)), ((), ()))

    def body(a_ref, b1_ref, b2_ref, *rest):
        o1_ref, o1b_ref, o2_ref, o2b_ref = rest[-4:]
        t = pl.program_id(1)

        @pl.when(t == 0)
        def _():
            o1_ref[...] = jnp.zeros_like(o1_ref)
            o2_ref[...] = jnp.zeros_like(o2_ref)

        av = a_ref[...]
        o1_ref[...] += lax.dot_general(av, b1_ref[...], tn, preferred_element_type=F32)
        o2_ref[...] += lax.dot_general(av, b2_ref[...], tn, preferred_element_type=F32)

        @pl.when(t == nt - 1)
        def _():
            o1b_ref[...] = o1_ref[...].astype(BF16)
            o2b_ref[...] = o2_ref[...].astype(BF16)

    b_spec = pl.BlockSpec((None, tk, kb), lambda j, t: (j, t, 0))
    o_spec = pl.BlockSpec((None, None, ka, kb), lambda j, t: (layer, j, 0, 0))
    in_specs = [pl.BlockSpec((tk, ka), lambda j, t: (t, 0)), b_spec, b_spec]
    operands = [a, b1, b2]
    aliases = {}
    if prev is not None:
        in_specs += [pl.BlockSpec(memory_space=pl.ANY)] * 4
        operands += list(prev)
        aliases = {3: 0, 4: 1, 5: 2, 6: 3}
    f32 = jax.ShapeDtypeStruct((n_layers, nj, ka, kb), F32)
    b16 = jax.ShapeDtypeStruct((n_layers, nj, ka, kb), BF16)
    return pl.pallas_call(
        body, name=f"{name}_{layer}", grid=(nj, nt),
        in_specs=in_specs, out_specs=[o_spec] * 4, out_shape=[f32, b16, f32, b16],
        input_output_aliases=aliases,
        compiler_params=_params("parallel", "arbitrary"),
    )(*operands)


def _block_diag(w):
    hb = LRU_BLOCK // LRU_HEAD_DIM
    nb = w.shape[0] // hb
    w4 = w.reshape(nb, hb, LRU_HEAD_DIM, LRU_HEAD_DIM)
    eye = jnp.eye(hb, dtype=w.dtype)
    return jnp.einsum("bide,ij->bidje", w4, eye).reshape(nb, LRU_BLOCK, LRU_BLOCK)


def _diag_heads(m):
    hb = LRU_BLOCK // LRU_HEAD_DIM
    nb = m.shape[0]
    m5 = m.reshape(nb, hb, LRU_HEAD_DIM, hb, LRU_HEAD_DIM)
    eye = jnp.eye(hb, dtype=m.dtype)
    return jnp.einsum("bidje,ij->bide", m5, eye).reshape(nb * hb, LRU_HEAD_DIM, LRU_HEAD_DIM)


def _tiles(T):
    cap = lambda n: min(n, T)
    return dict(inproj=cap(1024), mixer=cap(256), merge=cap(512), ffn=cap(1024), ffn_bwd=cap(512), loss=cap(512), inproj_bwd=cap(1024),
                wgrad_in=cap(1024), wgrad=cap(2048))


def _local_step(x, target, big, small, tiles):
    L = small["ln1_g"].shape[0]
    D = x.shape[1]
    saved = []
    h = x
    for l in range(L):
        cw = jnp.concatenate([small["conv_a_w"][l], small["conv_b_w"][l], small["conv_b_b"][l][None],
                              small["lru_ba"][l][None], small["lru_bx"][l][None]], axis=0)
        wab = jnp.concatenate([_block_diag(small["lru_wa"][l]), _block_diag(small["lru_wx"][l])], axis=2).astype(BF16)
        wabt = jnp.swapaxes(wab, 1, 2)
        lam_row = small["lru_lambda"][l][None]
        ln1_row = small["ln1_g"][l][None]
        ln2_row = small["ln2_g"][l][None]
        p, h1 = _rms_inproj(h, ln1_row, big["win"], l, tiles["inproj"])
        ya, yb, hseq = _mixer_fwd(p, cw, wab, lam_row, l, tiles["mixer"])
        oa, ob, mg, x1 = _merge_fwd(h, p, ya, yb, big["woa"], big["wob"], big["wo"], small["gate_bias"][l], l, tiles["merge"])
        h2, gg, uu, x2 = _ffn_fwd(x1, ln2_row, big["wg"], big["wu"], big["wd"], l, tiles["ffn"])
        saved.append(dict(x0=h, p=p, h1=h1, ya=ya, yb=yb, hseq=hseq, oa=oa, ob=ob, mg=mg, x1=x1, h2=h2, gg=gg, uu=uu,
                          cw=cw, wab=wab, wabt=wabt, lam_row=lam_row, ln1_row=ln1_row, ln2_row=ln2_row))
        h = x2

    dx, red = _final_loss(h, small["final_g"][None], target, tiles["loss"])
    loss_row, d_final_g = red[0], red[1]

    gbig = dict(win=None, woa=None, wob=None, wo=None, wg=None, wu=None, wd=None)
    gate_up = None
    gsmall = {k: [None] * L for k in ("ln1_g", "ln2_g", "conv_a_w", "conv_b_w", "conv_b_b", "lru_wa", "lru_ba", "lru_wx",
                                      "lru_bx", "lru_lambda", "gate_bias")}
    tk = tiles["wgrad"]
    for l in reversed(range(L)):
        s = saved[l]
        dgg, duu, act, dx1, dx2b, dln2 = _ffn_bwd(dx, s["x1"], s["ln2_row"], s["gg"], s["uu"], big["wg"], big["wu"], big["wd"],
                                                 l, tiles["ffn_bwd"])
        gate_up = _wgrad_pair(s["h2"], dgg, duu, "wgrad_ffn_gate_up", tk, l, L, gate_up)
        gbig["wg"], gbig["wu"] = gate_up[0:2], gate_up[2:4]
        gbig["wd"] = _wgrad(act, dx2b, "wgrad_ffn_down", tk, l, L, gbig["wd"], "cm", "whole", N_CHIP)
        dya, dyb, doa, dob, dgl, dx1b, dgbias = _merge_bwd(dx1, s["p"], s["oa"], s["ob"], big["woa"], big["wob"], big["wo"],
                                                         small["gate_bias"][l], l, tiles["merge"])
        gbig["wo"] = _wgrad(s["mg"], dx1b, "wgrad_w_o", tk, l, L, gbig["wo"])
        gbig["woa"] = _wgrad(s["ya"], doa, "wgrad_w_out_a", tk, l, L, gbig["woa"])
        gbig["wob"] = _wgrad(s["yb"], dob, "wgrad_w_out_b", tk, l, L, gbig["wob"])
        dp, mred, dwab = _mixer_bwd(s["p"], s["hseq"], dya, dyb, dgl, s["cw"], s["wab"], s["wabt"], s["lam_row"], l,
                                    tiles["mixer"])
        dx, dln1 = _inproj_bwd(dp, dx1, s["x0"], s["ln1_row"], big["win"], l, tiles["inproj_bwd"])
        gbig["win"] = _wgrad(s["h1"], dp, "wgrad_w_in", tiles["wgrad_in"], l, L, gbig["win"], "whole", "cols", N_CHIP)
        gsmall["ln1_g"][l] = dln1[0]
        gsmall["ln2_g"][l] = dln2[0]
        gsmall["conv_a_w"][l] = mred[0:CONV_A_K]
        gsmall["conv_b_w"][l] = mred[CONV_A_K:CONV_A_K + CONV_B_K]
        gsmall["conv_b_b"][l] = mred[7]
        gsmall["lru_ba"][l] = mred[8]
        gsmall["lru_bx"][l] = mred[9]
        gsmall["lru_lambda"][l] = mred[10]
        gsmall["lru_wa"][l] = _diag_heads(dwab[:, :, :LRU_BLOCK])
        gsmall["lru_wx"][l] = _diag_heads(dwab[:, :, LRU_BLOCK:])
        gsmall["gate_bias"][l] = dgbias
    gsmall = {k: jnp.stack(v) for k, v in gsmall.items()}
    gsmall["final_g"] = d_final_g
    return loss_row, dx, gbig, gsmall


ANY = pl.BlockSpec(memory_space=pl.ANY)


def _place():
    x, y, c = lax.axis_index("x"), lax.axis_index("y"), lax.axis_index("c")
    other_chips = [(1 - x, y), (x, 1 - y), (1 - x, 1 - y)]
    return x, y, c, other_chips


def _chip_id(x, y):
    return 2 * x + y


def _allgather_weights(slots, small_shard):
    nt = len(slots)
    assert slots[0].shape[0] == 2

    def body(*refs):
        sm_in = refs[nt]
        outs, sm_out = refs[nt + 1:2 * nt + 1], refs[2 * nt + 1]
        lsem, ssem, rsem, fssem, frsem, sm_ssem, sm_rsem = refs[2 * nt + 2:]
        x, y, c, chips = _place()
        me = _chip_id(x, y)
        sib = (x, y, 1 - c)

        local = [pltpu.make_async_copy(sm_in, sm_out.at[me], lsem)]
        for cp in local:
            cp.start()

        def ici(t, j, src_chip, to):
            blk = outs[t].at[c, src_chip]
            return pltpu.make_async_remote_copy(src_ref=blk, dst_ref=blk, send_sem=ssem.at[t * 3 + j],
                                                recv_sem=rsem.at[t * 3 + j], device_id=to, device_id_type=MESH)

        def d2d(t, j, layer, src_chip):
            blk = outs[t].at[layer, src_chip]
            return pltpu.make_async_remote_copy(src_ref=blk, dst_ref=blk, send_sem=fssem.at[t * 3 + j],
                                                recv_sem=frsem.at[t * 3 + j], device_id=sib, device_id_type=MESH)

        def sm(j, src_chip, to):
            return pltpu.make_async_remote_copy(src_ref=sm_in, dst_ref=sm_out.at[src_chip], send_sem=sm_ssem.at[j],
                                                recv_sem=sm_rsem.at[j], device_id=to, device_id_type=MESH)

        sends = [ici(t, j, me, (cx, cy, c)) for t in range(nt) for j, (cx, cy) in enumerate(chips)]
        sends += [sm(j, me, (cx, cy, c)) for j, (cx, cy) in enumerate(chips)]
        for cp in sends:
            cp.start()
        passed = []
        for t in range(nt):
            for j, (cx, cy) in enumerate(chips):
                ici(t, j, _chip_id(cx, cy), sib).wait_recv()
                f = d2d(t, j, c, _chip_id(cx, cy))
                f.start()
                passed.append(f)
        for t in range(nt):
            for j, (cx, cy) in enumerate(chips):
                d2d(t, j, 1 - c, _chip_id(cx, cy)).wait_recv()
        for j, (cx, cy) in enumerate(chips):
            sm(j, _chip_id(cx, cy), sib).wait_recv()
        for cp in sends + passed:
            cp.wait_send()
        for cp in local:
            cp.wait()

    out_shape = [jax.ShapeDtypeStruct(s.shape, s.dtype) for s in slots]
    out_shape.append(jax.ShapeDtypeStruct((N_CHIP,) + small_shard.shape, small_shard.dtype))
    dma = pltpu.SemaphoreType.DMA
    return pl.pallas_call(
        body, name="allgather_weights", out_shape=out_shape,
        in_specs=[ANY] * (nt + 1), out_specs=[ANY] * (nt + 1),
        input_output_aliases={t: t for t in range(nt)},
        scratch_shapes=[dma, dma((nt * 3,)), dma((nt * 3,)), dma((nt * 3,)), dma((nt * 3,)), dma((3,)), dma((3,))],
    )(*slots, small_shard)


def _exchange_halves(gbf):
    nt = len(gbf)

    def body(*refs):
        ins, outs = refs[:nt], refs[nt:2 * nt]
        ssem, rsem = refs[2 * nt:]
        x, y, c, _ = _place()
        cps = [pltpu.make_async_remote_copy(src_ref=ins[t].at[1 - c], dst_ref=outs[t], send_sem=ssem.at[t], recv_sem=rsem.at[t],
                                            device_id=(x, y, 1 - c), device_id_type=MESH) for t in range(nt)]
        for cp in cps:
            cp.start()
        for cp in cps:
            cp.wait()

    dma = pltpu.SemaphoreType.DMA
    return pl.pallas_call(
        body, name="reduce_exchange_halves", out_shape=[jax.ShapeDtypeStruct(g.shape[1:], g.dtype) for g in gbf],
        in_specs=[ANY] * nt, out_specs=[ANY] * nt, scratch_shapes=[dma((nt,)), dma((nt,))],
    )(*gbf)


def _scatter_chip_sums(pcb):
    nt = len(pcb)

    def body(*refs):
        ins, outs = refs[:nt], refs[nt:2 * nt]
        ssem, rsem = refs[2 * nt:]
        x, y, c, chips = _place()
        cps = [pltpu.make_async_remote_copy(src_ref=ins[t].at[_chip_id(cx, cy)], dst_ref=outs[t].at[j],
                                            send_sem=ssem.at[t * 3 + j], recv_sem=rsem.at[t * 3 + j],
                                            device_id=(cx, cy, c), device_id_type=MESH)
               for t in range(nt) for j, (cx, cy) in enumerate(chips)]
        for cp in cps:
            cp.start()
        for cp in cps:
            cp.wait()

    dma = pltpu.SemaphoreType.DMA
    return pl.pallas_call(
        body, name="reduce_scatter_chips", out_shape=[jax.ShapeDtypeStruct((3,) + g.shape[1:], g.dtype) for g in pcb],
        in_specs=[ANY] * nt, out_specs=[ANY] * nt, scratch_shapes=[dma((nt * 3,)), dma((nt * 3,))],
    )(*pcb)


def _share_reduced(red):
    nt = len(red)

    def body(*refs):
        outs = refs[nt:2 * nt]
        ssem, rsem = refs[2 * nt:]
        x, y, c, _ = _place()

        def share(t, layer):
            blk = outs[t].at[layer]
            return pltpu.make_async_remote_copy(src_ref=blk, dst_ref=blk, send_sem=ssem.at[t], recv_sem=rsem.at[t],
                                                device_id=(x, y, 1 - c), device_id_type=MESH)

        cps = [share(t, c) for t in range(nt)]
        for cp in cps:
            cp.start()
        for t in range(nt):
            share(t, 1 - c).wait_recv()
        for cp in cps:
            cp.wait_send()

    dma = pltpu.SemaphoreType.DMA
    return pl.pallas_call(
        body, name="reduce_share", out_shape=[jax.ShapeDtypeStruct(g.shape, g.dtype) for g in red],
        in_specs=[ANY] * nt, out_specs=[ANY] * nt, input_output_aliases={t: t for t in range(nt)},
        scratch_shapes=[dma((nt,)), dma((nt,))],
    )(*red)


def _small_allreduce(buf):
    R, C = buf.shape
    n_dev = 8
    rp = R // n_dev
    rel = [(k >> 2 & 1, k >> 1 & 1, k & 1) for k in range(1, n_dev)]

    def body(in_ref, out_ref, recv, s1, r1, s2, r2):
        x, y, c, _ = _place()
        flip = lambda v, bit: 1 - v if bit else v
        peers = [(flip(x, kx), flip(y, ky), flip(c, kc)) for kx, ky, kc in rel]
        dev = lambda p: 4 * p[0] + 2 * p[1] + p[2]
        part = lambda ref, d: ref.at[pl.ds(pl.multiple_of(d * rp, SUBLANES), rp), :]
        me = dev((x, y, c))

        def scatter(k, src_dev, to):
            return pltpu.make_async_remote_copy(src_ref=part(in_ref, dev(to)), dst_ref=recv.at[src_dev], send_sem=s1.at[k],
                                                recv_sem=r1.at[k], device_id=to, device_id_type=MESH)

        def gather(k, src_dev, to):
            return pltpu.make_async_remote_copy(src_ref=part(out_ref, src_dev), dst_ref=part(out_ref, src_dev), send_sem=s2.at[k],
                                                recv_sem=r2.at[k], device_id=to, device_id_type=MESH)

        first = [scatter(k, me, p) for k, p in enumerate(peers)]
        for cp in first:
            cp.start()
        recv[me] = part(in_ref, me)[...]
        for k, p in enumerate(peers):
            scatter(k, dev(p), (x, y, c)).wait_recv()
        total = recv[0]
        for d in range(1, n_dev):
            total = total + recv[d]
        part(out_ref, me)[...] = total
        second = [gather(k, me, p) for k, p in enumerate(peers)]
        for cp in second:
            cp.start()
        for k, p in enumerate(peers):
            gather(k, dev(p), (x, y, c)).wait_recv()
        for cp in first + second:
            cp.wait_send()

    dma = pltpu.SemaphoreType.DMA
    vm = pl.BlockSpec(memory_space=pltpu.VMEM)
    return pl.pallas_call(
        body, name="small_allreduce", out_shape=jax.ShapeDtypeStruct((R, C), buf.dtype),
        in_specs=[vm], out_specs=vm,
        scratch_shapes=[pltpu.VMEM((n_dev, rp, C), buf.dtype), dma((n_dev - 1,)), dma((n_dev - 1,)), dma((n_dev - 1,)), dma((n_dev - 1,))],
    )(buf)


def _row_block(k):
    b = 256
    while k % b:
        b //= 2
    return b


def _add_halves(g, recv, c_arr, name):
    _, nj, K, N = g.shape
    bk = _row_block(K)

    def body(c_ref, g_ref, r_ref, o_ref, ob_ref):
        s = g_ref[...] + r_ref[...].astype(F32)
        o_ref[...] = s
        ob_ref[...] = s.astype(BF16)

    blk = pl.BlockSpec((None, bk, N), lambda j, i, c_ref: (j, i, 0))
    grid_spec = pltpu.PrefetchScalarGridSpec(
        num_scalar_prefetch=1, grid=(nj, K // bk),
        in_specs=[pl.BlockSpec((None, None, bk, N), lambda j, i, c_ref: (c_ref[0], j, i, 0)), blk],
        out_specs=[blk, blk])
    return pl.pallas_call(
        body, name=name, grid_spec=grid_spec,
        out_shape=[jax.ShapeDtypeStruct((nj, K, N), F32), jax.ShapeDtypeStruct((nj, K, N), BF16)],
        compiler_params=_params("parallel", "parallel"),
    )(c_arr, g, recv)


def _add_chips(pc, recv, place_arr, n_layers, name):
    _, K, N = pc.shape
    bk = _row_block(K)

    def body(k_ref, p_ref, r0_ref, r1_ref, r2_ref, o_ref):
        o_ref[...] = ((p_ref[...] + r0_ref[...].astype(F32)) + r1_ref[...].astype(F32)) + r2_ref[...].astype(F32)

    rspec = lambda j: pl.BlockSpec((None, bk, N), lambda i, k_ref: (j, i, 0))
    grid_spec = pltpu.PrefetchScalarGridSpec(
        num_scalar_prefetch=1, grid=(K // bk,),
        in_specs=[pl.BlockSpec((None, bk, N), lambda i, k_ref: (k_ref[0], i, 0)), rspec(0), rspec(1), rspec(2)],
        out_specs=pl.BlockSpec((None, bk, N), lambda i, k_ref: (k_ref[1], i, 0)))
    return pl.pallas_call(
        body, name=name, grid_spec=grid_spec, out_shape=jax.ShapeDtypeStruct((n_layers, K, N), F32),
        compiler_params=_params("parallel"),
    )(place_arr, pc, recv, recv, recv)


def _adamw_math(w, g, m, v):
    m = ADAM_B1 * m + (1.0 - ADAM_B1) * g
    v = ADAM_B2 * v + (1.0 - ADAM_B2) * (g * g)
    m_hat = m / (1.0 - ADAM_B1 ** ADAM_STEP)
    v_hat = v / (1.0 - ADAM_B2 ** ADAM_STEP)
    delta = -ADAM_LR * (m_hat / (jnp.sqrt(v_hat) + ADAM_EPS) + ADAM_WD * w)
    return delta, m, v


def _adamw(w, g, m, v, name):
    L, K, N = w.shape
    bk = _row_block(K)

    def body(w_ref, g_ref, m_ref, v_ref, d_ref, nm_ref, nv_ref):
        d_ref[...], nm_ref[...], nv_ref[...] = _adamw_math(w_ref[...], g_ref[...], m_ref[...], v_ref[...])

    blk = pl.BlockSpec((None, bk, N), lambda l, i: (l, i, 0))
    sds = jax.ShapeDtypeStruct((L, K, N), F32)
    return pl.pallas_call(
        body, name=name, grid=(L, K // bk), in_specs=[blk] * 4, out_specs=[blk] * 3, out_shape=[sds] * 3,
        compiler_params=_params("parallel", "parallel"),
    )(w, g, m, v)


def _adamw_rows(w, g, m, v):
    def body(w_ref, g_ref, m_ref, v_ref, d_ref, nm_ref, nv_ref):
        d_ref[...], nm_ref[...], nv_ref[...] = _adamw_math(w_ref[...], g_ref[...], m_ref[...], v_ref[...])

    sds = jax.ShapeDtypeStruct(w.shape, F32)
    return pl.pallas_call(body, name="adamw_small", out_shape=[sds] * 3)(w, g, m, v)


def _cast_bf16(w, place_arr, name):
    L, K, N = w.shape
    bk = _row_block(K)

    def body(k_ref, w_ref, o_ref):
        o_ref[...] = w_ref[...].astype(BF16)

    grid_spec = pltpu.PrefetchScalarGridSpec(
        num_scalar_prefetch=1, grid=(L, K // bk),
        in_specs=[pl.BlockSpec((None, bk, N), lambda l, i, k_ref: (l, i, 0))],
        out_specs=pl.BlockSpec((None, None, bk, N), lambda l, i, k_ref: (l, k_ref[0], i, 0)))
    return pl.pallas_call(
        body, name=name, grid_spec=grid_spec, out_shape=jax.ShapeDtypeStruct((L, N_CHIP, K, N), BF16),
        compiler_params=_params("parallel", "parallel"),
    )(place_arr, w)


BIG = ("w_in", "w_out_a", "w_out_b", "w_o", "w_ffn_gate", "w_ffn_up", "w_ffn_down")
BIG_KEY = dict(w_in="win", w_out_a="woa", w_out_b="wob", w_o="wo", w_ffn_gate="wg", w_ffn_up="wu", w_ffn_down="wd")
SHARDED_SMALL = ("conv_a_w", "conv_b_w", "gate_bias")
REPLICATED = ("ln1_g", "conv_b_b", "lru_wa", "lru_ba", "lru_wx", "lru_bx", "lru_lambda", "ln2_g", "final_g")
WEIGHTS = ("ln1_g", "w_in", "conv_a_w", "conv_b_w", "conv_b_b", "lru_wa", "lru_ba", "lru_wx", "lru_bx", "lru_lambda",
           "w_out_a", "w_out_b", "gate_bias", "w_o", "ln2_g", "w_ffn_gate", "w_ffn_up", "w_ffn_down", "final_g")
LANES = 1024


def _pack_rows(arrays, row_multiple):
    flat = jnp.concatenate([a.reshape(-1) for a in arrays])
    rows = -(-flat.shape[0] // LANES)
    rows = -(-rows // row_multiple) * row_multiple
    flat = jnp.pad(flat, (0, rows * LANES - flat.shape[0]))
    return flat.reshape(rows, LANES)


def _unpack_rows(buf, shapes):
    flat = buf.reshape(-1)
    out, off = [], 0
    for s in shapes:
        n = 1
        for d in s:
            n *= d
        out.append(flat[off:off + n].reshape(s))
        off += n
    return out


def _step(w, m, v, x, target):
    xi, yi, ci = lax.axis_index("x"), lax.axis_index("y"), lax.axis_index("c")
    chip = _chip_id(xi, yi)
    c_arr = jnp.reshape(ci, (1,)).astype(jnp.int32)
    place_arr = jnp.stack([chip, ci]).astype(jnp.int32)
    L = w["ln1_g"].shape[0]
    D = x.shape[1]
    dc = D // N_CHIP

    shards = [_cast_bf16(w[n], place_arr, f"cast_{n}") for n in BIG]
    small_shard = jnp.concatenate([w[n] for n in SHARDED_SMALL], axis=1)
    *gathered, small_g = _allgather_weights(shards, small_shard)
    big = {}
    for n, g in zip(BIG, gathered):
        big[BIG_KEY[n]] = g.reshape(L, D, D) if n in ("w_out_a", "w_out_b", "w_o") else g
    small_full = jnp.transpose(small_g, (1, 2, 0, 3)).reshape(L, small_shard.shape[1], D)
    small = {n: w[n] for n in REPLICATED}
    off = 0
    for n in SHARDED_SMALL:
        k = w[n].shape[1]
        small[n] = small_full[:, off:off + k]
        off += k

    loss_row, grad_x, gbig, gsmall = _local_step(x, target, big, small, _tiles(x.shape[0]))

    shape4 = lambda n: (L, N_CHIP) + w[n].shape[1:]
    g32 = [gbig[BIG_KEY[n]][0].reshape(shape4(n)) for n in BIG]
    g16 = [gbig[BIG_KEY[n]][1].reshape(shape4(n)) for n in BIG]
    from_sibling = _exchange_halves(g16)
    chip_sums = [_add_halves(g, r, c_arr, f"add_halves_{n}") for n, g, r in zip(BIG, g32, from_sibling)]
    from_chips = _scatter_chip_sums([s[1] for s in chip_sums])
    reduced = [_add_chips(s[0], r, place_arr, L, f"add_chips_{n}") for n, s, r in zip(BIG, chip_sums, from_chips)]
    grads = dict(zip(BIG, _share_reduced(reduced)))

    order = [n for n in WEIGHTS if n not in BIG]
    packed = _pack_rows([gsmall[n] for n in order] + [loss_row], 8 * SUBLANES)
    summed = _small_allreduce(packed)
    parts = _unpack_rows(summed, [gsmall[n].shape for n in order] + [loss_row.shape])
    loss = jnp.sum(parts[-1])
    for n, g in zip(order, parts[:-1]):
        grads[n] = lax.dynamic_slice_in_dim(g, chip * dc, dc, axis=2) if n in SHARDED_SMALL else g

    delta, new_m, new_v = {}, {}, {}
    for n in BIG:
        delta[n], new_m[n], new_v[n] = _adamw(w[n], grads[n], m[n], v[n], f"adamw_{n}")
    pw, pg, pm, pv = (_pack_rows([d[n] for n in order], SUBLANES) for d in (w, grads, m, v))
    outs = _adamw_rows(pw, pg, pm, pv)
    shapes = [w[n].shape for n in order]
    for d, buf in zip((delta, new_m, new_v), outs):
        d.update(zip(order, _unpack_rows(buf, shapes)))
    return loss, grad_x, grads, delta, new_m, new_v


def kernel(x, ln1_g, w_in, conv_a_w, conv_b_w, conv_b_b, lru_wa, lru_ba, lru_wx, lru_bx, lru_lambda, w_out_a, w_out_b, gate_bias, w_o, ln2_g, w_ffn_gate, w_ffn_up, w_ffn_down, final_g, loss_target, m_ln1_g, m_w_in, m_conv_a_w, m_conv_b_w, m_conv_b_b, m_lru_wa, m_lru_ba, m_lru_wx, m_lru_bx, m_lru_lambda, m_w_out_a, m_w_out_b, m_gate_bias, m_w_o, m_ln2_g, m_w_ffn_gate, m_w_ffn_up, m_w_ffn_down, m_final_g, v_ln1_g, v_w_in, v_conv_a_w, v_conv_b_w, v_conv_b_b, v_lru_wa, v_lru_ba, v_lru_wx, v_lru_bx, v_lru_lambda, v_w_out_a, v_w_out_b, v_gate_bias, v_w_o, v_ln2_g, v_w_ffn_gate, v_w_ffn_up, v_w_ffn_down, v_final_g):
    w = dict(ln1_g=ln1_g, w_in=w_in, conv_a_w=conv_a_w, conv_b_w=conv_b_w, conv_b_b=conv_b_b, lru_wa=lru_wa, lru_ba=lru_ba,
             lru_wx=lru_wx, lru_bx=lru_bx, lru_lambda=lru_lambda, w_out_a=w_out_a, w_out_b=w_out_b, gate_bias=gate_bias, w_o=w_o,
             ln2_g=ln2_g, w_ffn_gate=w_ffn_gate, w_ffn_up=w_ffn_up, w_ffn_down=w_ffn_down, final_g=final_g)
    m = dict(ln1_g=m_ln1_g, w_in=m_w_in, conv_a_w=m_conv_a_w, conv_b_w=m_conv_b_w, conv_b_b=m_conv_b_b, lru_wa=m_lru_wa,
             lru_ba=m_lru_ba, lru_wx=m_lru_wx, lru_bx=m_lru_bx, lru_lambda=m_lru_lambda, w_out_a=m_w_out_a, w_out_b=m_w_out_b,
             gate_bias=m_gate_bias, w_o=m_w_o, ln2_g=m_ln2_g, w_ffn_gate=m_w_ffn_gate, w_ffn_up=m_w_ffn_up,
             w_ffn_down=m_w_ffn_down, final_g=m_final_g)
    v = dict(ln1_g=v_ln1_g, w_in=v_w_in, conv_a_w=v_conv_a_w, conv_b_w=v_conv_b_w, conv_b_b=v_conv_b_b, lru_wa=v_lru_wa,
             lru_ba=v_lru_ba, lru_wx=v_lru_wx, lru_bx=v_lru_bx, lru_lambda=v_lru_lambda, w_out_a=v_w_out_a, w_out_b=v_w_out_b,
             gate_bias=v_gate_bias, w_o=v_w_o, ln2_g=v_ln2_g, w_ffn_gate=v_w_ffn_gate, w_ffn_up=v_w_ffn_up,
             w_ffn_down=v_w_ffn_down, final_g=v_final_g)
    loss, grad_x, grads, delta, new_m, new_v = _step(w, m, v, x[0], loss_target[0])
    return (loss, grad_x[None], *[grads[n] for n in WEIGHTS], *[delta[n] for n in WEIGHTS],
            *[new_m[n] for n in WEIGHTS], *[new_v[n] for n in WEIGHTS])
```

```python
import functools

import jax
import jax.numpy as jnp
from jax import lax
from jax.experimental import pallas as pl
from jax.experimental.pallas import tpu as pltpu

F32 = jnp.float32
BF16 = jnp.bfloat16
MESH = pl.DeviceIdType.MESH

N_CHIP = 4
RMS_EPS = 1e-6
LRU_C = 8.0
LRU_HEAD_DIM = 64
LRU_BLOCK = 256
CONV_A_K = 3
CONV_B_K = 4
ADAM_LR = 0.001
ADAM_B1 = 0.9
ADAM_B2 = 0.999
ADAM_EPS = 1e-08
ADAM_WD = 0.01
ADAM_STEP = 10
SUBLANES = 8
VMEM_LIMIT = 56 * 1024 * 1024


def _params(*sem):
    return pltpu.CompilerParams(dimension_semantics=sem, vmem_limit_bytes=VMEM_LIMIT)


def _sigmoid(v):
    return 1.0 / (1.0 + jnp.exp(-v))


def _one_minus_sq(la, a):
    return jnp.tanh(-la) * (1.0 + a * a)


def _gelu_parts(v):
    k = 0.7978845608028654
    v2 = v * v
    t = jnp.tanh(k * (v + 0.044715 * v * v2))
    gelu = 0.5 * v * (1.0 + t)
    dgelu = 0.5 * (1.0 + t) + 0.5 * v * (1.0 - t * t) * k * (1.0 + 3 * 0.044715 * v2)
    return gelu, dgelu


def _shift_down(v, k, prev8):
    rolled = pltpu.roll(v, k, 0)
    r8 = lax.broadcasted_iota(jnp.int32, prev8.shape, 0)
    head = jnp.where(r8 < k, pltpu.roll(prev8, k, 0), rolled[0:SUBLANES])
    return jnp.concatenate([head, rolled[SUBLANES:]], axis=0)


def _shift_up(v, k, next8):
    tm = v.shape[0]
    rolled = pltpu.roll(v, tm - k, 0)
    r8 = lax.broadcasted_iota(jnp.int32, next8.shape, 0)
    tail = jnp.where(r8 >= SUBLANES - k, pltpu.roll(next8, SUBLANES - k, 0), rolled[tm - SUBLANES:])
    return jnp.concatenate([rolled[:tm - SUBLANES], tail], axis=0)


def _colsum8(v):
    tm, c = v.shape
    return jnp.sum(v.reshape(tm // SUBLANES, SUBLANES, c), axis=0)


def _rms_stats(xv):
    var = jnp.mean(xv * xv, axis=-1, keepdims=True)
    return lax.rsqrt(var + RMS_EPS)


def _rms_bwd(dh, xv, g):
    rstd = _rms_stats(xv)
    xhat = xv * rstd
    dxhat = dh * g
    dx = rstd * (dxhat - xhat * jnp.mean(dxhat * xhat, axis=-1, keepdims=True))
    return dx, dh * xhat


ANY = pl.BlockSpec(memory_space=pl.ANY)


def _place():
    x, y, c = lax.axis_index("x"), lax.axis_index("y"), lax.axis_index("c")
    other_chips = [(1 - x, y), (x, 1 - y), (1 - x, 1 - y)]
    return x, y, c, other_chips


def _chip_id(x, y):
    return 2 * x + y


def _half(c, hk):
    return pl.ds(pl.multiple_of(c * hk, 16), hk)


def _remote(src, dst, to, sems):
    return pltpu.make_async_remote_copy(src_ref=src, dst_ref=dst, device_id=to, device_id_type=MESH, **sems)


class _Carried:
    def __init__(self):
        self.ro, self.io, self.nw, self.parts, self.n = [], [], [], [], 0

    def add(self, maker, n, ro=(), io=(), nw=()):
        def index(items, item, same):
            for k, other in enumerate(items):
                if same(other, item):
                    return k
            items.append(item)
            return len(items) - 1

        r = [index(self.ro, a, lambda p, q: p is q) for a in ro]
        i = [index(self.io, a, lambda p, q: p[0] is q[0] and p[1] == q[1]) for a in io]
        w = [index(self.nw, a, lambda p, q: False) for a in nw]
        self.parts.append((maker, r, i, w, self.n))
        self.n += n
        return self

    def pairs(self, ro, io, nw, ssem, rsem):
        out = []
        for maker, r, i, w, base in self.parts:
            sems = lambda k, base=base: dict(send_sem=ssem.at[base + k], recv_sem=rsem.at[base + k])
            out += maker([ro[k] for k in r], [io[k] for k in i], [nw[k] for k in w], sems)
        return out

    def start(self, *refs):
        for send, _ in self.pairs(*refs):
            send.start()

    def finish(self, *refs):
        pairs = self.pairs(*refs)
        for _, recv in pairs:
            recv.wait_recv()
        for send, _ in pairs:
            send.wait_send()

    def operands(self):
        return list(self.ro) + [store[key] for store, key in self.io]

    def out_shapes(self):
        return [jax.ShapeDtypeStruct(store[key].shape, store[key].dtype) for store, key in self.io] + [s for _, _, s in self.nw]

    def keep(self, results):
        for (store, key), arr in zip(self.io, results[:len(self.io)]):
            store[key] = arr
        for (store, key, _), arr in zip(self.nw, results[len(self.io):]):
            store[key] = arr


def _call(body, comm, *, name, grid, in_specs, out_specs, out_shape, compiler_params, scratch_shapes=(), aliases=None):
    aliases = dict(aliases or {})
    if comm is None or not comm.parts:
        return pl.pallas_call(body, name=name, grid=grid, in_specs=in_specs, out_specs=out_specs, out_shape=out_shape,
                              scratch_shapes=list(scratch_shapes), input_output_aliases=aliases, compiler_params=compiler_params)
    n_in, n_out, n_scr = len(in_specs), len(out_shape), len(scratch_shapes)
    n_ro, n_io, n_nw = len(comm.ro), len(comm.io), len(comm.nw)

    def carried(*refs):
        base_in = refs[:n_in]
        ro = refs[n_in:n_in + n_ro]
        pos = n_in + n_ro + n_io
        base_out = refs[pos:pos + n_out]
        io = refs[pos + n_out:pos + n_out + n_io]
        nw = refs[pos + n_out + n_io:pos + n_out + n_io + n_nw]
        pos += n_out + n_io + n_nw
        scr = refs[pos:pos + n_scr]
        ssem, rsem = refs[pos + n_scr], refs[pos + n_scr + 1]
        first = pl.program_id(0) == 0
        last = pl.program_id(0) == grid[0] - 1
        for axis in range(1, len(grid)):
            first = first & (pl.program_id(axis) == 0)
            last = last & (pl.program_id(axis) == grid[axis] - 1)

        @pl.when(first)
        def _():
            comm.start(ro, io, nw, ssem, rsem)

        body(*base_in, *base_out, *scr)

        @pl.when(last)
        def _():
            comm.finish(ro, io, nw, ssem, rsem)

    aliases.update({n_in + n_ro + k: n_out + k for k in range(n_io)})
    dma = pltpu.SemaphoreType.DMA
    call = pl.pallas_call(
        carried, name=name, grid=grid,
        in_specs=list(in_specs) + [ANY] * (n_ro + n_io), out_specs=list(out_specs) + [ANY] * (n_io + n_nw),
        out_shape=list(out_shape) + comm.out_shapes(), input_output_aliases=aliases,
        scratch_shapes=list(scratch_shapes) + [dma((comm.n,)), dma((comm.n,))], compiler_params=compiler_params)

    def run(*operands):
        res = call(*operands, *comm.operands())
        comm.keep(res[n_out:])
        return res[:n_out]

    return run


def _run_comm(rounds, name):
    ro, io, nw, uses = [], [], [], []
    for r in rounds:
        def index(items, item, same):
            for k, other in enumerate(items):
                if same(other, item):
                    return k
            items.append(item)
            return len(items) - 1
        uses.append(([index(ro, a, lambda p, q: p is q) for a in r.ro],
                     [index(io, a, lambda p, q: p[0] is q[0] and p[1] == q[1]) for a in r.io],
                     [index(nw, a, lambda p, q: False) for a in r.nw]))
    n_ro, n_io, n_nw = len(ro), len(io), len(nw)

    def body(*refs):
        ro_refs = refs[:n_ro]
        io_refs = refs[n_ro + n_io:n_ro + 2 * n_io]
        nw_refs = refs[n_ro + 2 * n_io:n_ro + 2 * n_io + n_nw]
        sems = refs[n_ro + 2 * n_io + n_nw:]
        for k, (r, (a, b, c)) in enumerate(zip(rounds, uses)):
            args = ([ro_refs[i] for i in a], [io_refs[i] for i in b], [nw_refs[i] for i in c], sems[2 * k], sems[2 * k + 1])
            r.start(*args)
            r.finish(*args)

    operands = ro + [store[key] for store, key in io]
    out_shape = [jax.ShapeDtypeStruct(store[key].shape, store[key].dtype) for store, key in io] + [s for _, _, s in nw]
    dma = pltpu.SemaphoreType.DMA
    res = pl.pallas_call(
        body, name=name, out_shape=out_shape,
        in_specs=[ANY] * (n_ro + n_io), out_specs=[ANY] * (n_io + n_nw),
        input_output_aliases={n_ro + k: k for k in range(n_io)},
        scratch_shapes=[dma((r.n,)) for r in rounds for _ in range(2)],
    )(*operands)
    for (store, key), arr in zip(io, res[:n_io]):
        store[key] = arr
    for (store, key, _), arr in zip(nw, res[n_io:]):
        store[key] = arr


def _gather_ici(ro, io, nw, sems):
    s = io[0]
    x, y, c, chips = _place()
    hk = s.shape[1] // 2
    mine = s.at[_chip_id(x, y), _half(c, hk)]
    pairs = []
    for j, (cx, cy) in enumerate(chips):
        theirs = s.at[_chip_id(cx, cy), _half(c, hk)]
        pairs.append((_remote(mine, mine, (cx, cy, c), sems(j)), _remote(theirs, theirs, (cx, cy, c), sems(j))))
    return pairs


def _gather_d2d(ro, io, nw, sems):
    s = io[0]
    x, y, c, chips = _place()
    hk = s.shape[1] // 2
    sib = (x, y, 1 - c)
    pairs = []
    for j, (cx, cy) in enumerate(chips):
        here = s.at[_chip_id(cx, cy), _half(c, hk)]
        there = s.at[_chip_id(cx, cy), _half(1 - c, hk)]
        pairs.append((_remote(here, here, sib, sems(j)), _remote(there, there, sib, sems(j))))
    return pairs


def _gather_small(ro, io, nw, sems):
    x, y, c, chips = _place()
    return [(_remote(ro[0], nw[0].at[j], (cx, cy, c), sems(j)),) * 2 for j, (cx, cy) in enumerate(chips)]


def _reduce_halves(ro, io, nw, sems):
    x, y, c, _ = _place()
    g = ro[0]
    hk = g.shape[1] // 2
    sib = (x, y, 1 - c)
    return [(_remote(g.at[:, _half(1 - c, hk)], nw[0], sib, sems(0)), _remote(g.at[:, _half(c, hk)], nw[0], sib, sems(0)))]


def _reduce_chips(ro, io, nw, sems):
    x, y, c, chips = _place()
    return [(_remote(ro[0].at[_chip_id(cx, cy)], nw[0].at[j], (cx, cy, c), sems(j)),) * 2 for j, (cx, cy) in enumerate(chips)]


def _reduce_share(layer):
    def maker(ro, io, nw, sems):
        g = io[0]
        x, y, c, _ = _place()
        hk = g.shape[1] // 2
        sib = (x, y, 1 - c)
        mine, theirs = g.at[layer, _half(c, hk)], g.at[layer, _half(1 - c, hk)]
        return [(_remote(mine, mine, sib, sems(0)), _remote(theirs, theirs, sib, sems(0)))]
    return maker


def _rms_inproj(x, g_row, win, layer, tm, comm=None):
    T, D = x.shape
    ns = win.shape[-1]

    def body(x_ref, g_ref, w_ref, p_ref, h_ref):
        @pl.when(pl.program_id(1) == 0)
        def _():
            xv = x_ref[...]
            h_ref[...] = (xv * _rms_stats(xv) * g_ref[...]).astype(BF16)
        p_ref[...] = jnp.dot(h_ref[...], w_ref[...], preferred_element_type=F32).astype(BF16)

    return _call(
        body, comm, name=f"rms_inproj_{layer}", grid=(T // tm, N_CHIP),
        in_specs=[pl.BlockSpec((tm, D), lambda i, j: (i, 0)),
                  pl.BlockSpec((1, D), lambda i, j: (0, 0)),
                  pl.BlockSpec((None, D, ns), lambda i, j: (j, 0, 0))],
        out_specs=[pl.BlockSpec((tm, ns), lambda i, j: (i, j)),
                   pl.BlockSpec((tm, D), lambda i, j: (i, 0))],
        out_shape=[jax.ShapeDtypeStruct((T, N_CHIP * ns), BF16), jax.ShapeDtypeStruct((T, D), BF16)],
        compiler_params=_params("parallel", "arbitrary"),
    )(x, g_row, win)


def _mixer_recompute(ca, xa, xb, zprev8, xbprev8, cw_ref, wab_ref, sp):
    row = lambda k: cw_ref[pl.ds(k, 1), :]
    z = ca * xa
    z1 = _shift_down(z, 1, zprev8)
    z2 = _shift_down(z, 2, zprev8)
    cz = row(2) * z + row(1) * z1 + row(0) * z2
    x1 = _shift_down(xb, 1, xbprev8)
    x2 = _shift_down(xb, 2, xbprev8)
    x3 = _shift_down(xb, 3, xbprev8)
    u = row(6) * xb + row(5) * x1 + row(4) * x2 + row(3) * x3 + row(7)
    ub = u.astype(BF16)
    nb = wab_ref.shape[0]
    ras, ixs = [], []
    for b in range(nb):
        ri = jnp.dot(ub[:, b * LRU_BLOCK:(b + 1) * LRU_BLOCK], wab_ref[b], preferred_element_type=F32)
        ras.append(ri[:, :LRU_BLOCK])
        ixs.append(ri[:, LRU_BLOCK:])
    r = _sigmoid(jnp.concatenate(ras, axis=1) + row(8))
    gi = _sigmoid(jnp.concatenate(ixs, axis=1) + row(9))
    la = (-LRU_C) * r * sp
    a = jnp.exp(la)
    m = jnp.sqrt(_one_minus_sq(la, a))
    return dict(z=z, z1=z1, z2=z2, cz=cz, x1=x1, x2=x2, x3=x3, u=u, ub=ub, r=r, gi=gi, a=a, m=m)


def _softplus_neg(lam):
    v = -lam
    return jnp.maximum(v, 0.0) + jnp.log1p(jnp.exp(-jnp.abs(v)))


def _mixer_fwd(p, cw, wab, lam_row, layer, tm, comm=None):
    T = p.shape[0]
    D = p.shape[1] // 7
    ngroups = tm // SUBLANES

    def body(ba_ref, ca_ref, xa_ref, xb_ref, gb_ref, cw_ref, wab_ref, lam_ref, ya_ref, yb_ref, h_ref,
             zprev, xbprev, hcarry, a_s, h_s):
        @pl.when(pl.program_id(0) == 0)
        def _():
            zprev[...] = jnp.zeros_like(zprev)
            xbprev[...] = jnp.zeros_like(xbprev)
            hcarry[...] = jnp.zeros_like(hcarry)

        ca = ca_ref[...].astype(F32)
        xa = xa_ref[...].astype(F32)
        xb = xb_ref[...].astype(F32)
        sp = _softplus_neg(lam_ref[...])
        c = _mixer_recompute(ca, xa, xb, zprev[...], xbprev[...], cw_ref, wab_ref, sp)
        zprev[...] = c["z"][tm - SUBLANES:]
        xbprev[...] = xb[tm - SUBLANES:]
        ya_ref[...] = (ba_ref[...].astype(F32) * c["cz"]).astype(BF16)

        A = c["a"]
        B = c["m"] * c["gi"] * c["u"]
        q = lax.broadcasted_iota(jnp.int32, A.shape, 0) & (SUBLANES - 1)
        for s in (1, 2, 4):
            msk = q >= s
            B = jnp.where(msk, A * pltpu.roll(B, s, 0) + B, B)
            A = jnp.where(msk, A * pltpu.roll(A, s, 0), A)
        a_s[...] = A
        h_s[...] = B

        def step(g, carry):
            off = pl.multiple_of(g * SUBLANES, SUBLANES)
            hg = h_s[pl.ds(off, SUBLANES), :] + a_s[pl.ds(off, SUBLANES), :] * carry
            h_s[pl.ds(off, SUBLANES), :] = hg
            return jnp.broadcast_to(hg[SUBLANES - 1:SUBLANES, :], hg.shape)

        hcarry[...] = lax.fori_loop(0, ngroups, step, hcarry[...], unroll=4)
        h = h_s[...]
        h_ref[...] = h
        gelu, _ = _gelu_parts(gb_ref[...].astype(F32))
        yb_ref[...] = (h * gelu).astype(BF16)

    col = lambda k: pl.BlockSpec((tm, D), lambda i: (i, k))
    full = lambda a: pl.BlockSpec(a.shape, lambda i: (0,) * a.ndim)
    tok = pl.BlockSpec((tm, D), lambda i: (i, 0))
    return _call(
        body, comm, name=f"mixer_fwd_{layer}", grid=(T // tm,),
        in_specs=[col(0), col(1), col(2), col(3), col(4), full(cw), full(wab), full(lam_row)],
        out_specs=[tok, tok, tok],
        out_shape=[jax.ShapeDtypeStruct((T, D), BF16), jax.ShapeDtypeStruct((T, D), BF16), jax.ShapeDtypeStruct((T, D), F32)],
        scratch_shapes=[pltpu.VMEM((SUBLANES, D), F32), pltpu.VMEM((SUBLANES, D), F32), pltpu.VMEM((SUBLANES, D), F32),
                        pltpu.VMEM((tm, D), F32), pltpu.VMEM((tm, D), F32)],
        compiler_params=_params("arbitrary"),
    )(p, p, p, p, p, cw, wab, lam_row)


def _merge_fwd(x, p, ya, yb, woa, wob, wo, gbias, layer, tm, comm=None):
    T, D = x.shape

    def body(x_ref, ga_ref, gb_ref, ya_ref, yb_ref, woa_ref, wob_ref, wo_ref, bias_ref, oa_ref, ob_ref, mg_ref, x1_ref):
        oa = jnp.dot(ya_ref[...], woa_ref[...], preferred_element_type=F32)
        ob = jnp.dot(yb_ref[...], wob_ref[...], preferred_element_type=F32)
        sa = _sigmoid(ga_ref[...].astype(F32) + bias_ref[pl.ds(0, 1), :])
        sb = _sigmoid(gb_ref[...].astype(F32) + bias_ref[pl.ds(1, 1), :])
        mg = (sa * oa + sb * ob).astype(BF16)
        oa_ref[...] = oa.astype(BF16)
        ob_ref[...] = ob.astype(BF16)
        mg_ref[...] = mg
        x1_ref[...] = x_ref[...] + jnp.dot(mg, wo_ref[...], preferred_element_type=F32)

    tok = pl.BlockSpec((tm, D), lambda i: (i, 0))
    wsp = pl.BlockSpec((D, D), lambda i: (0, 0))
    bf = jax.ShapeDtypeStruct((T, D), BF16)
    return _call(
        body, comm, name=f"merge_fwd_{layer}", grid=(T // tm,),
        in_specs=[tok, pl.BlockSpec((tm, D), lambda i: (i, 5)), pl.BlockSpec((tm, D), lambda i: (i, 6)), tok, tok,
                  wsp, wsp, wsp, pl.BlockSpec(gbias.shape, lambda i: (0, 0))],
        out_specs=[tok, tok, tok, tok],
        out_shape=[bf, bf, bf, jax.ShapeDtypeStruct((T, D), F32)],
        compiler_params=_params("parallel"),
    )(x, p, p, ya, yb, woa, wob, wo, gbias)


def _ffn_fwd(x1, g_row, wg, wu, wd, layer, tm, comm=None):
    T, D = x1.shape
    fs = wg.shape[-1]

    def body(x_ref, g_ref, wg_ref, wu_ref, wd_ref, h_ref, gg_ref, uu_ref, x2_ref, acc):
        j = pl.program_id(1)

        @pl.when(j == 0)
        def _():
            xv = x_ref[...]
            h_ref[...] = (xv * _rms_stats(xv) * g_ref[...]).astype(BF16)
            acc[...] = xv

        h = h_ref[...]
        gg = jnp.dot(h, wg_ref[...], preferred_element_type=F32)
        uu = jnp.dot(h, wu_ref[...], preferred_element_type=F32)
        gg_ref[...] = gg.astype(BF16)
        uu_ref[...] = uu.astype(BF16)
        act = (gg * _sigmoid(gg) * uu).astype(BF16)
        acc[...] += jnp.dot(act, wd_ref[...], preferred_element_type=F32)

        @pl.when(j == N_CHIP - 1)
        def _():
            x2_ref[...] = acc[...]

    tok = pl.BlockSpec((tm, D), lambda i, j: (i, 0))
    cm = pl.BlockSpec((None, tm, fs), lambda i, j: (j, i, 0))
    return _call(
        body, comm, name=f"ffn_fwd_{layer}", grid=(T // tm, N_CHIP),
        in_specs=[tok, pl.BlockSpec((1, D), lambda i, j: (0, 0)),
                  pl.BlockSpec((None, D, fs), lambda i, j: (j, 0, 0)),
                  pl.BlockSpec((None, D, fs), lambda i, j: (j, 0, 0)),
                  pl.BlockSpec((None, fs, D), lambda i, j: (j, 0, 0))],
        out_specs=[tok, cm, cm, tok],
        out_shape=[jax.ShapeDtypeStruct((T, D), BF16), jax.ShapeDtypeStruct((N_CHIP, T, fs), BF16),
                   jax.ShapeDtypeStruct((N_CHIP, T, fs), BF16), jax.ShapeDtypeStruct((T, D), F32)],
        scratch_shapes=[pltpu.VMEM((tm, D), F32)],
        compiler_params=_params("parallel", "arbitrary"),
    )(x1, g_row, wg, wu, wd)


def _final_loss(x, g_row, target, tm):
    T, D = x.shape
    n = T // tm

    def body(x_ref, g_ref, t_ref, dx_ref, red_ref, acc):
        i = pl.program_id(0)

        @pl.when(i == 0)
        def _():
            acc[...] = jnp.zeros_like(acc)

        xv = x_ref[...]
        g = g_ref[...]
        rstd = _rms_stats(xv)
        xhat = xv * rstd
        err = xhat * g - t_ref[...]
        dy = err * (1.0 / D)
        dxhat = dy * g
        dx_ref[...] = rstd * (dxhat - xhat * jnp.mean(dxhat * xhat, axis=-1, keepdims=True))
        acc[0] += _colsum8(err * err)
        acc[1] += _colsum8(dy * xhat)

        @pl.when(i == n - 1)
        def _():
            red_ref[pl.ds(0, 1), :] = jnp.sum(acc[0], axis=0, keepdims=True) * (0.5 / D)
            red_ref[pl.ds(1, 1), :] = jnp.sum(acc[1], axis=0, keepdims=True)

    tok = pl.BlockSpec((tm, D), lambda i: (i, 0))
    return pl.pallas_call(
        body, name="final_loss", grid=(n,),
        in_specs=[tok, pl.BlockSpec((1, D), lambda i: (0, 0)), tok],
        out_specs=[tok, pl.BlockSpec((2, D), lambda i: (0, 0))],
        out_shape=[jax.ShapeDtypeStruct((T, D), F32), jax.ShapeDtypeStruct((2, D), F32)],
        scratch_shapes=[pltpu.VMEM((2, SUBLANES, D), F32)],
        compiler_params=_params("arbitrary"),
    )(x, g_row, target)


def _ffn_bwd(dx2, x1, g_row, gg, uu, wg, wu, wd, layer, tm, comm=None):
    T, D = dx2.shape
    fs = wg.shape[-1]
    n = T // tm
    nt = (((1,), (1,)), ((), ()))

    def body(dx_ref, x_ref, g_ref, gg_ref, uu_ref, wg_ref, wu_ref, wd_ref, dg_ref, du_ref, act_ref, dx1_ref, dxb_ref, red_ref,
             acc, racc):
        i = pl.program_id(0)
        j = pl.program_id(1)

        @pl.when((i == 0) & (j == 0))
        def _():
            racc[...] = jnp.zeros_like(racc)

        @pl.when(j == 0)
        def _():
            dxb_ref[...] = dx_ref[...].astype(BF16)
            acc[...] = jnp.zeros_like(acc)

        dact = lax.dot_general(dxb_ref[...], wd_ref[...], nt, preferred_element_type=F32)
        g = gg_ref[...].astype(F32)
        u = uu_ref[...].astype(F32)
        s = _sigmoid(g)
        silu = g * s
        dg = (dact * u * (s * (1.0 + g * (1.0 - s)))).astype(BF16)
        du = (dact * silu).astype(BF16)
        dg_ref[...] = dg
        du_ref[...] = du
        act_ref[...] = (silu * u).astype(BF16)
        acc[...] += (lax.dot_general(dg, wg_ref[...], nt, preferred_element_type=F32)
                     + lax.dot_general(du, wu_ref[...], nt, preferred_element_type=F32))

        @pl.when(j == N_CHIP - 1)
        def _():
            dx, dgain = _rms_bwd(acc[...], x_ref[...], g_ref[...])
            dx1_ref[...] = dx_ref[...] + dx
            racc[...] += _colsum8(dgain)

        @pl.when((i == n - 1) & (j == N_CHIP - 1))
        def _():
            red_ref[...] = jnp.sum(racc[...], axis=0, keepdims=True)

    tok = pl.BlockSpec((tm, D), lambda i, j: (i, 0))
    cm = pl.BlockSpec((None, tm, fs), lambda i, j: (j, i, 0))
    cms = jax.ShapeDtypeStruct((N_CHIP, T, fs), BF16)
    return _call(
        body, comm, name=f"ffn_bwd_{layer}", grid=(n, N_CHIP),
        in_specs=[tok, tok, pl.BlockSpec((1, D), lambda i, j: (0, 0)), cm, cm,
                  pl.BlockSpec((None, D, fs), lambda i, j: (j, 0, 0)),
                  pl.BlockSpec((None, D, fs), lambda i, j: (j, 0, 0)),
                  pl.BlockSpec((None, fs, D), lambda i, j: (j, 0, 0))],
        out_specs=[cm, cm, cm, tok, tok, pl.BlockSpec((1, D), lambda i, j: (0, 0))],
        out_shape=[cms, cms, cms, jax.ShapeDtypeStruct((T, D), F32), jax.ShapeDtypeStruct((T, D), BF16),
                   jax.ShapeDtypeStruct((1, D), F32)],
        scratch_shapes=[pltpu.VMEM((tm, D), F32), pltpu.VMEM((SUBLANES, D), F32)],
        compiler_params=_params("arbitrary", "arbitrary"),
    )(dx2, x1, g_row, gg, uu, wg, wu, wd)


def _merge_bwd(dx1, p, oa, ob, woa, wob, wo, gbias, layer, tm, comm=None):
    T, D = dx1.shape
    n = T // tm
    nt = (((1,), (1,)), ((), ()))

    def body(dx_ref, ga_ref, gb_ref, oa_ref, ob_ref, woa_ref, wob_ref, wo_ref, bias_ref,
             dya_ref, dyb_ref, doa_ref, dob_ref, dgl_ref, dxb_ref, red_ref, racc):
        i = pl.program_id(0)

        @pl.when(i == 0)
        def _():
            racc[...] = jnp.zeros_like(racc)

        dxb = dx_ref[...].astype(BF16)
        dxb_ref[...] = dxb
        dm = lax.dot_general(dxb, wo_ref[...], nt, preferred_element_type=F32)
        sa = _sigmoid(ga_ref[...].astype(F32) + bias_ref[pl.ds(0, 1), :])
        sb = _sigmoid(gb_ref[...].astype(F32) + bias_ref[pl.ds(1, 1), :])
        doa = (dm * sa).astype(BF16)
        dob = (dm * sb).astype(BF16)
        dga = dm * oa_ref[...].astype(F32) * (sa * (1.0 - sa))
        dgb = dm * ob_ref[...].astype(F32) * (sb * (1.0 - sb))
        doa_ref[...] = doa
        dob_ref[...] = dob
        dgl_ref[:, 0:D] = dga.astype(BF16)
        dgl_ref[:, D:2 * D] = dgb.astype(BF16)
        racc[0] += _colsum8(dga)
        racc[1] += _colsum8(dgb)
        dya_ref[...] = lax.dot_general(doa, woa_ref[...], nt, preferred_element_type=F32).astype(BF16)
        dyb_ref[...] = lax.dot_general(dob, wob_ref[...], nt, preferred_element_type=F32).astype(BF16)

        @pl.when(i == n - 1)
        def _():
            red_ref[pl.ds(0, 1), :] = jnp.sum(racc[0], axis=0, keepdims=True)
            red_ref[pl.ds(1, 1), :] = jnp.sum(racc[1], axis=0, keepdims=True)

    tok = pl.BlockSpec((tm, D), lambda i: (i, 0))
    wsp = pl.BlockSpec((D, D), lambda i: (0, 0))
    bf = jax.ShapeDtypeStruct((T, D), BF16)
    return _call(
        body, comm, name=f"merge_bwd_{layer}", grid=(n,),
        in_specs=[tok, pl.BlockSpec((tm, D), lambda i: (i, 5)), pl.BlockSpec((tm, D), lambda i: (i, 6)), tok, tok,
                  wsp, wsp, wsp, pl.BlockSpec(gbias.shape, lambda i: (0, 0))],
        out_specs=[tok, tok, tok, tok, pl.BlockSpec((tm, 2 * D), lambda i: (i, 0)), tok, pl.BlockSpec((2, D), lambda i: (0, 0))],
        out_shape=[bf, bf, bf, bf, jax.ShapeDtypeStruct((T, 2 * D), BF16), bf, jax.ShapeDtypeStruct((2, D), F32)],
        scratch_shapes=[pltpu.VMEM((2, SUBLANES, D), F32)],
        compiler_params=_params("arbitrary"),
    )(dx1, p, p, oa, ob, woa, wob, wo, gbias)


N_MIXER_RED = 16


def _mixer_bwd(p, hseq, dya, dyb, dgl, cw, wab, wabt, lam_row, layer, tm, comm=None):
    T = p.shape[0]
    D = p.shape[1] // 7
    n = T // tm
    ngroups = tm // SUBLANES
    nb = wab.shape[0]
    hb = 16
    tn = (((0,), (0,)), ((), ()))

    def body(ba_ref, ca_ref, xa_ref, xb_ref, gb_ref, h_ref, dya_ref, dyb_ref, dgl_ref,
             cap_ref, xap_ref, xbp_ref, hp_ref, ban_ref, dyan_ref,
             cw_ref, wab_ref, wabt_ref, lam_ref,
             dp_ref, red_ref, dwab_ref,
             racc, wacc, anext, gnext, dunext, c_s, g_s):
        i = pl.program_id(0)
        first_tile = i == n - 1
        last_tile = i == 0

        @pl.when(i == 0)
        def _():
            racc[...] = jnp.zeros_like(racc)
            wacc[...] = jnp.zeros_like(wacc)
            anext[...] = jnp.zeros_like(anext)
            gnext[...] = jnp.zeros_like(gnext)
            dunext[...] = jnp.zeros_like(dunext)

        keep_prev = jnp.where(first_tile, 0.0, 1.0)
        keep_next = jnp.where(last_tile, 0.0, 1.0)
        ba = ba_ref[...].astype(F32)
        ca = ca_ref[...].astype(F32)
        xa = xa_ref[...].astype(F32)
        xb = xb_ref[...].astype(F32)
        h = h_ref[...]
        dya = dya_ref[...].astype(F32)
        dyb = dyb_ref[...].astype(F32)
        zprev8 = (cap_ref[...].astype(F32) * xap_ref[...].astype(F32))[hb - SUBLANES:] * keep_prev
        xbprev8 = xbp_ref[...].astype(F32)[hb - SUBLANES:] * keep_prev
        hprev8 = hp_ref[...] * keep_prev
        dcznext8 = (dyan_ref[...].astype(F32) * ban_ref[...].astype(F32))[:SUBLANES] * keep_next

        lam = lam_ref[...]
        sp = _softplus_neg(lam)
        c = _mixer_recompute(ca, xa, xb, zprev8, xbprev8, cw_ref, wab_ref, sp)
        row = lambda k: cw_ref[pl.ds(k, 1), :]
        a, m, r, gi, u = c["a"], c["m"], c["r"], c["gi"], c["u"]

        gelu, dgelu = _gelu_parts(gb_ref[...].astype(F32))
        dgb = dyb * h * dgelu
        Dv = dyb * gelu
        Cv = _shift_up(a, 1, anext[...])
        q = lax.broadcasted_iota(jnp.int32, Cv.shape, 0) & (SUBLANES - 1)
        for s in (1, 2, 4):
            msk = q < SUBLANES - s
            Dv = jnp.where(msk, Dv + Cv * pltpu.roll(Dv, tm - s, 0), Dv)
            Cv = jnp.where(msk, Cv * pltpu.roll(Cv, tm - s, 0), Cv)
        c_s[...] = Cv
        g_s[...] = Dv

        def step(k, carry):
            off = pl.multiple_of((ngroups - 1 - k) * SUBLANES, SUBLANES)
            gg = g_s[pl.ds(off, SUBLANES), :] + c_s[pl.ds(off, SUBLANES), :] * carry
            g_s[pl.ds(off, SUBLANES), :] = gg
            return jnp.broadcast_to(gg[0:1, :], gg.shape)

        gnext[...] = lax.fori_loop(0, ngroups, step, gnext[...], unroll=4)
        anext[...] = a[0:SUBLANES]
        g = g_s[...]

        hprev = _shift_down(h, 1, hprev8)
        da = g * hprev
        gm = g * m
        dgi = gm * u
        du = gm * gi
        dmv = g * gi * u
        dla = a * (da - dmv * a / m)
        dra = dla * ((-LRU_C) * sp) * (r * (1.0 - r))
        dix = dgi * (gi * (1.0 - gi))
        racc[10] += _colsum8(dla * r)
        racc[8] += _colsum8(dra)
        racc[9] += _colsum8(dix)
        drab = dra.astype(BF16)
        dixb = dix.astype(BF16)
        ub = c["ub"]
        dus = []
        for b in range(nb):
            sl = slice(b * LRU_BLOCK, (b + 1) * LRU_BLOCK)
            dri = jnp.concatenate([drab[:, sl], dixb[:, sl]], axis=1)
            dus.append(jnp.dot(dri, wabt_ref[b], preferred_element_type=F32))
            wacc[b] += lax.dot_general(ub[:, sl], dri, tn, preferred_element_type=F32)
        du = du + jnp.concatenate(dus, axis=1)

        dun = dunext[...]
        du1 = _shift_up(du, 1, dun)
        du2 = _shift_up(du, 2, dun)
        du3 = _shift_up(du, 3, dun)
        dxb = row(6) * du + row(5) * du1 + row(4) * du2 + row(3) * du3
        dunext[...] = du[0:SUBLANES]
        racc[6] += _colsum8(du * xb)
        racc[5] += _colsum8(du * c["x1"])
        racc[4] += _colsum8(du * c["x2"])
        racc[3] += _colsum8(du * c["x3"])
        racc[7] += _colsum8(du)

        dba = dya * c["cz"]
        dcz = dya * ba
        dcz1 = _shift_up(dcz, 1, dcznext8)
        dcz2 = _shift_up(dcz, 2, dcznext8)
        dz = row(2) * dcz + row(1) * dcz1 + row(0) * dcz2
        racc[2] += _colsum8(dcz * c["z"])
        racc[1] += _colsum8(dcz * c["z1"])
        racc[0] += _colsum8(dcz * c["z2"])

        dp_ref[:, 0:D] = dba.astype(BF16)
        dp_ref[:, D:2 * D] = (dz * xa).astype(BF16)
        dp_ref[:, 2 * D:3 * D] = (dz * ca).astype(BF16)
        dp_ref[:, 3 * D:4 * D] = dxb.astype(BF16)
        dp_ref[:, 4 * D:5 * D] = dgb.astype(BF16)
        dp_ref[:, 5 * D:7 * D] = dgl_ref[...]

        @pl.when(i == n - 1)
        def _():
            dlam_scale = LRU_C * _sigmoid(-lam)
            for k in range(N_MIXER_RED):
                tot = jnp.sum(racc[k], axis=0, keepdims=True)
                red_ref[pl.ds(k, 1), :] = tot * dlam_scale if k == 10 else tot
            dwab_ref[...] = wacc[...]

    rt = lambda i: n - 1 - i
    col = lambda k: pl.BlockSpec((tm, D), lambda i: (rt(i), k))
    tok = pl.BlockSpec((tm, D), lambda i: (rt(i), 0))
    full = lambda a: pl.BlockSpec(a.shape, lambda i: (0,) * a.ndim)
    prev16 = lambda k: pl.BlockSpec((hb, D), lambda i: (jnp.maximum(rt(i) * (tm // hb) - 1, 0), k))
    next16 = lambda k: pl.BlockSpec((hb, D), lambda i: (jnp.minimum((rt(i) + 1) * (tm // hb), T // hb - 1), k))
    hprev = pl.BlockSpec((SUBLANES, D), lambda i: (jnp.maximum(rt(i) * ngroups - 1, 0), 0))
    return _call(
        body, comm, name=f"mixer_bwd_{layer}", grid=(n,),
        in_specs=[col(0), col(1), col(2), col(3), col(4), tok, tok, tok, pl.BlockSpec((tm, 2 * D), lambda i: (rt(i), 0)),
                  prev16(1), prev16(2), prev16(3), hprev, next16(0), next16(0),
                  full(cw), full(wab), full(wabt), full(lam_row)],
        out_specs=[pl.BlockSpec((tm, 7 * D), lambda i: (rt(i), 0)),
                   pl.BlockSpec((N_MIXER_RED, D), lambda i: (0, 0)),
                   pl.BlockSpec((nb, LRU_BLOCK, 2 * LRU_BLOCK), lambda i: (0, 0, 0))],
        out_shape=[jax.ShapeDtypeStruct((T, 7 * D), BF16), jax.ShapeDtypeStruct((N_MIXER_RED, D), F32),
                   jax.ShapeDtypeStruct((nb, LRU_BLOCK, 2 * LRU_BLOCK), F32)],
        scratch_shapes=[pltpu.VMEM((N_MIXER_RED, SUBLANES, D), F32), pltpu.VMEM((nb, LRU_BLOCK, 2 * LRU_BLOCK), F32),
                        pltpu.VMEM((SUBLANES, D), F32), pltpu.VMEM((SUBLANES, D), F32), pltpu.VMEM((SUBLANES, D), F32),
                        pltpu.VMEM((tm, D), F32), pltpu.VMEM((tm, D), F32)],
        compiler_params=_params("arbitrary"),
    )(p, p, p, p, p, hseq, dya, dyb, dgl, p, p, p, hseq, p, dya, cw, wab, wabt, lam_row)


def _inproj_bwd(dp, dx1, x, g_row, win, layer, tm, comm=None):
    T, D = x.shape
    ns = win.shape[-1]
    n = T // tm
    nt = (((1,), (1,)), ((), ()))

    def body(dp_ref, dx_ref, x_ref, g_ref, w_ref, dx0_ref, red_ref, acc, racc):
        i = pl.program_id(0)
        j = pl.program_id(1)

        @pl.when((i == 0) & (j == 0))
        def _():
            racc[...] = jnp.zeros_like(racc)

        @pl.when(j == 0)
        def _():
            acc[...] = jnp.zeros_like(acc)

        acc[...] += lax.dot_general(dp_ref[...], w_ref[...], nt, preferred_element_type=F32)

        @pl.when(j == N_CHIP - 1)
        def _():
            dx, dgain = _rms_bwd(acc[...], x_ref[...], g_ref[...])
            dx0_ref[...] = dx_ref[...] + dx
            racc[...] += _colsum8(dgain)

        @pl.when((i == n - 1) & (j == N_CHIP - 1))
        def _():
            red_ref[...] = jnp.sum(racc[...], axis=0, keepdims=True)

    tok = pl.BlockSpec((tm, D), lambda i, j: (i, 0))
    return _call(
        body, comm, name=f"inproj_bwd_{layer}", grid=(n, N_CHIP),
        in_specs=[pl.BlockSpec((tm, ns), lambda i, j: (i, j)), tok, tok, pl.BlockSpec((1, D), lambda i, j: (0, 0)),
                  pl.BlockSpec((None, D, ns), lambda i, j: (j, 0, 0))],
        out_specs=[tok, pl.BlockSpec((1, D), lambda i, j: (0, 0))],
        out_shape=[jax.ShapeDtypeStruct((T, D), F32), jax.ShapeDtypeStruct((1, D), F32)],
        scratch_shapes=[pltpu.VMEM((tm, D), F32), pltpu.VMEM((SUBLANES, D), F32)],
        compiler_params=_params("arbitrary", "arbitrary"),
    )(dp, dx1, x, g_row, win)


def _wgrad(a, b, name, tk, a_kind="whole", b_kind="whole", nj=1, comm=None):
    T = a.shape[-2]
    width = lambda v, kind: v.shape[-1] // nj if kind == "cols" else v.shape[-1]
    ka, kb = width(a, a_kind), width(b, b_kind)
    nt = T // tk
    tn = (((0,), (0,)), ((), ()))

    def spec(k, kind):
        if kind == "cm":
            return pl.BlockSpec((None, tk, k), lambda j, t: (j, t, 0))
        if kind == "cols":
            return pl.BlockSpec((tk, k), lambda j, t: (t, j))
        return pl.BlockSpec((tk, k), lambda j, t: (t, 0))

    def body(a_ref, b_ref, o_ref, ob_ref):
        t = pl.program_id(1)

        @pl.when(t == 0)
        def _():
            o_ref[...] = jnp.zeros_like(o_ref)

        o_ref[...] += lax.dot_general(a_ref[...], b_ref[...], tn, preferred_element_type=F32)

        @pl.when(t == nt - 1)
        def _():
            ob_ref[...] = o_ref[...].astype(BF16)

    o_spec = pl.BlockSpec((None, ka, kb), lambda j, t: (j, 0, 0))
    return _call(
        body, comm, name=name, grid=(nj, nt),
        in_specs=[spec(ka, a_kind), spec(kb, b_kind)], out_specs=[o_spec, o_spec],
        out_shape=[jax.ShapeDtypeStruct((nj, ka, kb), F32), jax.ShapeDtypeStruct((nj, ka, kb), BF16)],
        compiler_params=_params("parallel", "arbitrary"),
    )(a, b)


def _wgrad_pair(a, b1, b2, name, tk, comm=None):
    T, ka = a.shape
    nj, _, kb = b1.shape
    nt = T // tk
    tn = (((0,), (0,)), ((), ()))

    def body(a_ref, b1_ref, b2_ref, o1_ref, o1b_ref, o2_ref, o2b_ref):
        t = pl.program_id(1)

        @pl.when(t == 0)
        def _():
            o1_ref[...] = jnp.zeros_like(o1_ref)
            o2_ref[...] = jnp.zeros_like(o2_ref)

        av = a_ref[...]
        o1_ref[...] += lax.dot_general(av, b1_ref[...], tn, preferred_element_type=F32)
        o2_ref[...] += lax.dot_general(av, b2_ref[...], tn, preferred_element_type=F32)

        @pl.when(t == nt - 1)
        def _():
            o1b_ref[...] = o1_ref[...].astype(BF16)
            o2b_ref[...] = o2_ref[...].astype(BF16)

    b_spec = pl.BlockSpec((None, tk, kb), lambda j, t: (j, t, 0))
    o_spec = pl.BlockSpec((None, ka, kb), lambda j, t: (j, 0, 0))
    f32 = jax.ShapeDtypeStruct((nj, ka, kb), F32)
    b16 = jax.ShapeDtypeStruct((nj, ka, kb), BF16)
    return _call(
        body, comm, name=name, grid=(nj, nt),
        in_specs=[pl.BlockSpec((tk, ka), lambda j, t: (t, 0)), b_spec, b_spec], out_specs=[o_spec] * 4,
        out_shape=[f32, b16, f32, b16], compiler_params=_params("parallel", "arbitrary"),
    )(a, b1, b2)


def _block_diag(w):
    hb = LRU_BLOCK // LRU_HEAD_DIM
    nb = w.shape[0] // hb
    w4 = w.reshape(nb, hb, LRU_HEAD_DIM, LRU_HEAD_DIM)
    eye = jnp.eye(hb, dtype=w.dtype)
    return jnp.einsum("bide,ij->bidje", w4, eye).reshape(nb, LRU_BLOCK, LRU_BLOCK)


def _diag_heads(m):
    hb = LRU_BLOCK // LRU_HEAD_DIM
    nb = m.shape[0]
    m5 = m.reshape(nb, hb, LRU_HEAD_DIM, hb, LRU_HEAD_DIM)
    eye = jnp.eye(hb, dtype=m.dtype)
    return jnp.einsum("bidje,ij->bide", m5, eye).reshape(nb * hb, LRU_HEAD_DIM, LRU_HEAD_DIM)


def _tiles(T):
    cap = lambda n: min(n, T)
    return dict(inproj=cap(1024), mixer=cap(256), merge=cap(512), ffn=cap(1024), ffn_bwd=cap(512), loss=cap(512), inproj_bwd=cap(1024),
                wgrad_in=cap(1024), wgrad=cap(2048))


class _NoSchedule:
    def carry(self, name):
        return None

    def after(self, name):
        pass

    def grad(self, key, layer, f32, b16):
        pass


def _local_step(x, target, W, small, tiles, sched):
    L = small["ln1_g"].shape[0]
    D = x.shape[1]
    square = lambda a: a.reshape(D, D)
    saved = []
    h = x
    for l in range(L):
        cw = jnp.concatenate([small["conv_a_w"][l], small["conv_b_w"][l], small["conv_b_b"][l][None],
                              small["lru_ba"][l][None], small["lru_bx"][l][None]], axis=0)
        wab = jnp.concatenate([_block_diag(small["lru_wa"][l]), _block_diag(small["lru_wx"][l])], axis=2).astype(BF16)
        wabt = jnp.swapaxes(wab, 1, 2)
        lam_row = small["lru_lambda"][l][None]
        ln1_row = small["ln1_g"][l][None]
        ln2_row = small["ln2_g"][l][None]
        p, h1 = _rms_inproj(h, ln1_row, W["win", l], l, tiles["inproj"], sched.carry(f"rms_inproj_{l}"))
        ya, yb, hseq = _mixer_fwd(p, cw, wab, lam_row, l, tiles["mixer"], sched.carry(f"mixer_fwd_{l}"))
        oa, ob, mg, x1 = _merge_fwd(h, p, ya, yb, square(W["woa", l]), square(W["wob", l]), square(W["wo", l]),
                                    small["gate_bias"][l], l, tiles["merge"], sched.carry(f"merge_fwd_{l}"))
        h2, gg, uu, x2 = _ffn_fwd(x1, ln2_row, W["wg", l], W["wu", l], W["wd", l], l, tiles["ffn"], sched.carry(f"ffn_fwd_{l}"))
        saved.append(dict(x0=h, p=p, h1=h1, ya=ya, yb=yb, hseq=hseq, oa=oa, ob=ob, mg=mg, x1=x1, h2=h2, gg=gg, uu=uu,
                          cw=cw, wab=wab, wabt=wabt, lam_row=lam_row, ln1_row=ln1_row, ln2_row=ln2_row))
        h = x2

    dx, red = _final_loss(h, small["final_g"][None], target, tiles["loss"])
    loss_row, d_final_g = red[0], red[1]

    gsmall = {k: [None] * L for k in ("ln1_g", "ln2_g", "conv_a_w", "conv_b_w", "conv_b_b", "lru_wa", "lru_ba", "lru_wx",
                                      "lru_bx", "lru_lambda", "gate_bias")}
    tk = tiles["wgrad"]
    for l in reversed(range(L)):
        s = saved[l]
        dgg, duu, act, dx1, dx2b, dln2 = _ffn_bwd(dx, s["x1"], s["ln2_row"], s["gg"], s["uu"], W["wg", l], W["wu", l], W["wd", l],
                                                 l, tiles["ffn_bwd"], sched.carry(f"ffn_bwd_{l}"))
        sched.after(f"ffn_bwd_{l}")
        gate_up = _wgrad_pair(s["h2"], dgg, duu, f"wgrad_ffn_gate_up_{l}", tk, sched.carry(f"wgrad_ffn_gate_up_{l}"))
        sched.grad("wg", l, *gate_up[0:2])
        sched.grad("wu", l, *gate_up[2:4])
        sched.after(f"wgrad_ffn_gate_up_{l}")
        sched.grad("wd", l, *_wgrad(act, dx2b, f"wgrad_ffn_down_{l}", tk, "cm", "whole", N_CHIP, sched.carry(f"wgrad_ffn_down_{l}")))
        dya, dyb, doa, dob, dgl, dx1b, dgbias = _merge_bwd(dx1, s["p"], s["oa"], s["ob"], square(W["woa", l]), square(W["wob", l]),
                                                         square(W["wo", l]), small["gate_bias"][l], l, tiles["merge"],
                                                         sched.carry(f"merge_bwd_{l}"))
        sched.after(f"merge_bwd_{l}")
        sched.grad("wo", l, *_wgrad(s["mg"], dx1b, f"wgrad_w_o_{l}", tk))
        sched.grad("woa", l, *_wgrad(s["ya"], doa, f"wgrad_w_out_a_{l}", tk))
        sched.grad("wob", l, *_wgrad(s["yb"], dob, f"wgrad_w_out_b_{l}", tk))
        dp, mred, dwab = _mixer_bwd(s["p"], s["hseq"], dya, dyb, dgl, s["cw"], s["wab"], s["wabt"], s["lam_row"], l,
                                    tiles["mixer"], sched.carry(f"mixer_bwd_{l}"))
        sched.after(f"mixer_bwd_{l}")
        dx, dln1 = _inproj_bwd(dp, dx1, s["x0"], s["ln1_row"], W["win", l], l, tiles["inproj_bwd"], sched.carry(f"inproj_bwd_{l}"))
        sched.after(f"inproj_bwd_{l}")
        sched.grad("win", l, *_wgrad(s["h1"], dp, f"wgrad_w_in_{l}", tiles["wgrad_in"], "whole", "cols", N_CHIP))
        gsmall["ln1_g"][l] = dln1[0]
        gsmall["ln2_g"][l] = dln2[0]
        gsmall["conv_a_w"][l] = mred[0:CONV_A_K]
        gsmall["conv_b_w"][l] = mred[CONV_A_K:CONV_A_K + CONV_B_K]
        gsmall["conv_b_b"][l] = mred[7]
        gsmall["lru_ba"][l] = mred[8]
        gsmall["lru_bx"][l] = mred[9]
        gsmall["lru_lambda"][l] = mred[10]
        gsmall["lru_wa"][l] = _diag_heads(dwab[:, :, :LRU_BLOCK])
        gsmall["lru_wx"][l] = _diag_heads(dwab[:, :, LRU_BLOCK:])
        gsmall["gate_bias"][l] = dgbias
    gsmall = {k: jnp.stack(v) for k, v in gsmall.items()}
    gsmall["final_g"] = d_final_g
    return loss_row, dx, gsmall


def _small_allreduce(buf):
    R, C = buf.shape
    n_dev = 8
    rp = R // n_dev
    rel = [(k >> 2 & 1, k >> 1 & 1, k & 1) for k in range(1, n_dev)]

    def body(in_ref, out_ref, recv, s1, r1, s2, r2):
        x, y, c, _ = _place()
        flip = lambda v, bit: 1 - v if bit else v
        peers = [(flip(x, kx), flip(y, ky), flip(c, kc)) for kx, ky, kc in rel]
        dev = lambda p: 4 * p[0] + 2 * p[1] + p[2]
        part = lambda ref, d: ref.at[pl.ds(pl.multiple_of(d * rp, SUBLANES), rp), :]
        me = dev((x, y, c))

        def scatter(k, src_dev, to):
            return pltpu.make_async_remote_copy(src_ref=part(in_ref, dev(to)), dst_ref=recv.at[src_dev], send_sem=s1.at[k],
                                                recv_sem=r1.at[k], device_id=to, device_id_type=MESH)

        def gather(k, src_dev, to):
            return pltpu.make_async_remote_copy(src_ref=part(out_ref, src_dev), dst_ref=part(out_ref, src_dev), send_sem=s2.at[k],
                                                recv_sem=r2.at[k], device_id=to, device_id_type=MESH)

        first = [scatter(k, me, p) for k, p in enumerate(peers)]
        for cp in first:
            cp.start()
        recv[me] = part(in_ref, me)[...]
        for k, p in enumerate(peers):
            scatter(k, dev(p), (x, y, c)).wait_recv()
        total = recv[0]
        for d in range(1, n_dev):
            total = total + recv[d]
        part(out_ref, me)[...] = total
        second = [gather(k, me, p) for k, p in enumerate(peers)]
        for cp in second:
            cp.start()
        for k, p in enumerate(peers):
            gather(k, dev(p), (x, y, c)).wait_recv()
        for cp in first + second:
            cp.wait_send()

    dma = pltpu.SemaphoreType.DMA
    vm = pl.BlockSpec(memory_space=pltpu.VMEM)
    return pl.pallas_call(
        body, name="small_allreduce", out_shape=jax.ShapeDtypeStruct((R, C), buf.dtype),
        in_specs=[vm], out_specs=vm,
        scratch_shapes=[pltpu.VMEM((n_dev, rp, C), buf.dtype), dma((n_dev - 1,)), dma((n_dev - 1,)), dma((n_dev - 1,)), dma((n_dev - 1,))],
    )(buf)


def _row_block(k):
    b = 256
    while k % b:
        b //= 2
    return b


def _add_halves(g, recv, place_arr, name):
    nj, hk, N = recv.shape
    bk = _row_block(hk)
    nb = hk // bk

    def body(k_ref, g_ref, r_ref, o_ref, ob_ref):
        s = g_ref[...] + r_ref[...].astype(F32)
        o_ref[...] = s
        ob_ref[...] = s.astype(BF16)

    blk = pl.BlockSpec((None, bk, N), lambda j, i, k_ref: (j, i, 0))
    grid_spec = pltpu.PrefetchScalarGridSpec(
        num_scalar_prefetch=1, grid=(nj, nb),
        in_specs=[pl.BlockSpec((None, bk, N), lambda j, i, k_ref: (j, k_ref[1] * nb + i, 0)), blk],
        out_specs=[blk, blk])
    return pl.pallas_call(
        body, name=name, grid_spec=grid_spec,
        out_shape=[jax.ShapeDtypeStruct((nj, hk, N), F32), jax.ShapeDtypeStruct((nj, hk, N), BF16)],
        compiler_params=_params("parallel", "parallel"),
    )(place_arr, g, recv)


def _add_chips(pc, recv, place_arr, layer, n_layers, prev, name):
    _, hk, N = pc.shape
    bk = _row_block(hk)
    nb = hk // bk

    def body(k_ref, p_ref, r0_ref, r1_ref, r2_ref, *rest):
        o_ref = rest[-1]
        o_ref[...] = ((p_ref[...] + r0_ref[...].astype(F32)) + r1_ref[...].astype(F32)) + r2_ref[...].astype(F32)

    rspec = lambda j: pl.BlockSpec((None, bk, N), lambda i, k_ref: (j, i, 0))
    in_specs = [pl.BlockSpec((None, bk, N), lambda i, k_ref: (k_ref[0], i, 0)), rspec(0), rspec(1), rspec(2)]
    operands = [pc, recv, recv, recv]
    aliases = {}
    if prev is not None:
        in_specs.append(ANY)
        operands.append(prev)
        aliases = {5: 0}
    grid_spec = pltpu.PrefetchScalarGridSpec(
        num_scalar_prefetch=1, grid=(nb,), in_specs=in_specs,
        out_specs=pl.BlockSpec((None, bk, N), lambda i, k_ref: (layer, k_ref[1] * nb + i, 0)))
    return pl.pallas_call(
        body, name=name, grid_spec=grid_spec, out_shape=jax.ShapeDtypeStruct((n_layers, 2 * hk, N), F32),
        input_output_aliases=aliases, compiler_params=_params("parallel"),
    )(place_arr, *operands)


def _adamw_math(w, g, m, v):
    m = ADAM_B1 * m + (1.0 - ADAM_B1) * g
    v = ADAM_B2 * v + (1.0 - ADAM_B2) * (g * g)
    m_hat = m / (1.0 - ADAM_B1 ** ADAM_STEP)
    v_hat = v / (1.0 - ADAM_B2 ** ADAM_STEP)
    delta = -ADAM_LR * (m_hat / (jnp.sqrt(v_hat) + ADAM_EPS) + ADAM_WD * w)
    return delta, m, v


def _adamw(w, g, m, v, name):
    L, K, N = w.shape
    bk = _row_block(K)

    def body(w_ref, g_ref, m_ref, v_ref, d_ref, nm_ref, nv_ref):
        d_ref[...], nm_ref[...], nv_ref[...] = _adamw_math(w_ref[...], g_ref[...], m_ref[...], v_ref[...])

    blk = pl.BlockSpec((None, bk, N), lambda l, i: (l, i, 0))
    sds = jax.ShapeDtypeStruct((L, K, N), F32)
    return pl.pallas_call(
        body, name=name, grid=(L, K // bk), in_specs=[blk] * 4, out_specs=[blk] * 3, out_shape=[sds] * 3,
        compiler_params=_params("parallel", "parallel"),
    )(w, g, m, v)


def _adamw_rows(w, g, m, v):
    def body(w_ref, g_ref, m_ref, v_ref, d_ref, nm_ref, nv_ref):
        d_ref[...], nm_ref[...], nv_ref[...] = _adamw_math(w_ref[...], g_ref[...], m_ref[...], v_ref[...])

    sds = jax.ShapeDtypeStruct(w.shape, F32)
    return pl.pallas_call(body, name="adamw_small", out_shape=[sds] * 3)(w, g, m, v)


def _cast_bf16(w, layer, place_arr, name):
    _, K, N = w.shape
    bk = _row_block(K)

    def body(k_ref, w_ref, o_ref):
        o_ref[...] = w_ref[...].astype(BF16)

    grid_spec = pltpu.PrefetchScalarGridSpec(
        num_scalar_prefetch=1, grid=(K // bk,),
        in_specs=[pl.BlockSpec((None, bk, N), lambda i, k_ref: (layer, i, 0))],
        out_specs=pl.BlockSpec((None, bk, N), lambda i, k_ref: (k_ref[0], i, 0)))
    return pl.pallas_call(
        body, name=name, grid_spec=grid_spec, out_shape=jax.ShapeDtypeStruct((N_CHIP, K, N), BF16),
        compiler_params=_params("parallel"),
    )(place_arr, w)


BIG = ("w_in", "w_out_a", "w_out_b", "w_o", "w_ffn_gate", "w_ffn_up", "w_ffn_down")
BIG_KEY = dict(w_in="win", w_out_a="woa", w_out_b="wob", w_o="wo", w_ffn_gate="wg", w_ffn_up="wu", w_ffn_down="wd")
SHARDED_SMALL = ("conv_a_w", "conv_b_w", "gate_bias")
REPLICATED = ("ln1_g", "conv_b_b", "lru_wa", "lru_ba", "lru_wx", "lru_bx", "lru_lambda", "ln2_g", "final_g")
WEIGHTS = ("ln1_g", "w_in", "conv_a_w", "conv_b_w", "conv_b_b", "lru_wa", "lru_ba", "lru_wx", "lru_bx", "lru_lambda",
           "w_out_a", "w_out_b", "gate_bias", "w_o", "ln2_g", "w_ffn_gate", "w_ffn_up", "w_ffn_down", "final_g")
LANES = 1024


def _pack_rows(arrays, row_multiple):
    flat = jnp.concatenate([a.reshape(-1) for a in arrays])
    rows = -(-flat.shape[0] // LANES)
    rows = -(-rows // row_multiple) * row_multiple
    flat = jnp.pad(flat, (0, rows * LANES - flat.shape[0]))
    return flat.reshape(rows, LANES)


def _unpack_rows(buf, shapes):
    flat = buf.reshape(-1)
    out, off = [], 0
    for s in shapes:
        n = 1
        for d in s:
            n *= d
        out.append(flat[off:off + n].reshape(s))
        off += n
    return out


OUT_KEYS = ("wo", "woa", "wob")
FFN_KEYS = ("wg", "wu", "wd")


def _items(keys, layer):
    return [(k, layer) for k in keys]


CARRY = {
    "rms_inproj_0": [("gather_ici", _items(OUT_KEYS + FFN_KEYS, 0))],
    "mixer_fwd_0": [("gather_d2d", _items(OUT_KEYS + FFN_KEYS, 0)), ("gather_ici", _items(("win",) + OUT_KEYS, 1))],
    "merge_fwd_0": [("gather_d2d", _items(("win",) + OUT_KEYS, 1))],
    "ffn_fwd_0": [("gather_ici", _items(FFN_KEYS, 1))],
    "rms_inproj_1": [("gather_d2d", _items(FFN_KEYS, 1))],
    "merge_bwd_1": [("halves", _items(FFN_KEYS, 1))],
    "mixer_bwd_1": [("chips", _items(FFN_KEYS, 1)), ("halves", _items(OUT_KEYS, 1))],
    "inproj_bwd_1": [("chips", _items(OUT_KEYS, 1)), ("share", _items(FFN_KEYS, 1))],
    "ffn_bwd_0": [("halves", [("win", 1)]), ("share", _items(OUT_KEYS, 1))],
    "wgrad_ffn_gate_up_0": [("chips", [("win", 1)])],
    "wgrad_ffn_down_0": [("share", [("win", 1)])],
    "merge_bwd_0": [("halves", _items(FFN_KEYS, 0))],
    "mixer_bwd_0": [("chips", _items(FFN_KEYS, 0)), ("halves", _items(OUT_KEYS, 0))],
    "inproj_bwd_0": [("chips", _items(OUT_KEYS, 0)), ("share", _items(FFN_KEYS, 0))],
}
AFTER = {
    "merge_bwd_1": [("add_halves", _items(FFN_KEYS, 1))],
    "mixer_bwd_1": [("add_chips", _items(FFN_KEYS, 1)), ("add_halves", _items(OUT_KEYS, 1))],
    "inproj_bwd_1": [("add_chips", _items(OUT_KEYS, 1))],
    "ffn_bwd_0": [("add_halves", [("win", 1)])],
    "wgrad_ffn_gate_up_0": [("add_chips", [("win", 1)])],
    "merge_bwd_0": [("add_halves", _items(FFN_KEYS, 0))],
    "mixer_bwd_0": [("add_chips", _items(FFN_KEYS, 0)), ("add_halves", _items(OUT_KEYS, 0))],
    "inproj_bwd_0": [("add_chips", _items(OUT_KEYS, 0))],
}


class _Schedule:
    def __init__(self, slots, place_arr, n_layers):
        self.W = slots
        self.place, self.L = place_arr, n_layers
        self.g32, self.g16 = {}, {}
        self.from_sibling, self.chip_sum, self.chip_sum16, self.from_chips = {}, {}, {}, {}
        self.reduced = {}

    def stage(self, comm, kind, items):
        bf = lambda shape: jax.ShapeDtypeStruct(shape, BF16)
        for it in items:
            if kind == "gather_ici":
                comm.add(_gather_ici, 3, io=[(self.W, it)])
            elif kind == "gather_d2d":
                comm.add(_gather_d2d, 3, io=[(self.W, it)])
            elif kind == "halves":
                nj, K, N = self.g16[it].shape
                comm.add(_reduce_halves, 1, ro=[self.g16[it]], nw=[(self.from_sibling, it, bf((nj, K // 2, N)))])
            elif kind == "chips":
                _, hk, N = self.chip_sum16[it].shape
                comm.add(_reduce_chips, 3, ro=[self.chip_sum16[it]], nw=[(self.from_chips, it, bf((3, hk, N)))])
            elif kind == "share":
                comm.add(_reduce_share(it[1]), 1, io=[(self.reduced, it[0])])
        return comm

    def carry(self, name):
        comm = _Carried()
        for kind, items in CARRY.get(name, ()):
            self.stage(comm, kind, items)
        return comm

    def run(self, name, rounds):
        _run_comm([self.stage(_Carried(), kind, items) for kind, items in rounds], name)

    def grad(self, key, layer, f32, b16):
        by_chip = lambda g: g.reshape(N_CHIP, -1, g.shape[-1])
        self.g32[key, layer], self.g16[key, layer] = by_chip(f32), by_chip(b16)

    def add(self, kind, items):
        for key, layer in items:
            it = (key, layer)
            if kind == "add_halves":
                self.chip_sum[it], self.chip_sum16[it] = _add_halves(self.g32[it], self.from_sibling[it], self.place,
                                                                    f"add_halves_{key}_{layer}")
            else:
                self.reduced[key] = _add_chips(self.chip_sum[it], self.from_chips[it], self.place, layer, self.L,
                                               self.reduced.get(key), f"add_chips_{key}_{layer}")

    def after(self, name):
        for kind, items in AFTER.get(name, ()):
            self.add(kind, items)


def _step(w, m, v, x, target):
    xi, yi, ci = lax.axis_index("x"), lax.axis_index("y"), lax.axis_index("c")
    chip = _chip_id(xi, yi)
    place_arr = jnp.stack([chip, ci]).astype(jnp.int32)
    L = w["ln1_g"].shape[0]
    assert L == 2
    D = x.shape[1]
    dc = D // N_CHIP

    slots = {(BIG_KEY[n], l): _cast_bf16(w[n], l, place_arr, f"cast_{n}_{l}") for n in BIG for l in range(L)}
    sched = _Schedule(slots, place_arr, L)
    small_shard = jnp.concatenate([w[n] for n in SHARDED_SMALL], axis=1)
    got = {}
    first = sched.stage(_Carried(), "gather_ici", [("win", 0)])
    first.add(_gather_small, 3, ro=[small_shard], nw=[(got, "small", jax.ShapeDtypeStruct((3,) + small_shard.shape, F32))])
    _run_comm([first, sched.stage(_Carried(), "gather_d2d", [("win", 0)])], "gather_first")
    small_g = jnp.zeros((N_CHIP,) + small_shard.shape, F32)
    small_g = lax.dynamic_update_index_in_dim(small_g, small_shard, chip, 0)
    for j, (cx, cy) in enumerate([(1 - xi, yi), (xi, 1 - yi), (1 - xi, 1 - yi)]):
        small_g = lax.dynamic_update_index_in_dim(small_g, got["small"][j], _chip_id(cx, cy), 0)
    small_full = jnp.transpose(small_g, (1, 2, 0, 3)).reshape(L, small_shard.shape[1], D)
    small = {n: w[n] for n in REPLICATED}
    off = 0
    for n in SHARDED_SMALL:
        k = w[n].shape[1]
        small[n] = small_full[:, off:off + k]
        off += k

    loss_row, grad_x, gsmall = _local_step(x, target, sched.W, small, _tiles(x.shape[0]), sched)

    sched.run("reduce_tail_halves", [("halves", [("win", 0)]), ("share", _items(OUT_KEYS, 0))])
    sched.add("add_halves", [("win", 0)])
    sched.run("reduce_tail_chips", [("chips", [("win", 0)])])
    sched.add("add_chips", [("win", 0)])
    sched.run("reduce_tail_share", [("share", [("win", 0)])])
    grads = {n: sched.reduced[BIG_KEY[n]] for n in BIG}

    order = [n for n in WEIGHTS if n not in BIG]
    packed = _pack_rows([gsmall[n] for n in order] + [loss_row], 8 * SUBLANES)
    summed = _small_allreduce(packed)
    parts = _unpack_rows(summed, [gsmall[n].shape for n in order] + [loss_row.shape])
    loss = jnp.sum(parts[-1])
    for n, g in zip(order, parts[:-1]):
        grads[n] = lax.dynamic_slice_in_dim(g, chip * dc, dc, axis=2) if n in SHARDED_SMALL else g

    delta, new_m, new_v = {}, {}, {}
    for n in BIG:
        delta[n], new_m[n], new_v[n] = _adamw(w[n], grads[n], m[n], v[n], f"adamw_{n}")
    pw, pg, pm, pv = (_pack_rows([d[n] for n in order], SUBLANES) for d in (w, grads, m, v))
    outs = _adamw_rows(pw, pg, pm, pv)
    shapes = [w[n].shape for n in order]
    for d, buf in zip((delta, new_m, new_v), outs):
        d.update(zip(order, _unpack_rows(buf, shapes)))
    return loss, grad_x, grads, delta, new_m, new_v


def kernel(x, ln1_g, w_in, conv_a_w, conv_b_w, conv_b_b, lru_wa, lru_ba, lru_wx, lru_bx, lru_lambda, w_out_a, w_out_b, gate_bias, w_o, ln2_g, w_ffn_gate, w_ffn_up, w_ffn_down, final_g, loss_target, m_ln1_g, m_w_in, m_conv_a_w, m_conv_b_w, m_conv_b_b, m_lru_wa, m_lru_ba, m_lru_wx, m_lru_bx, m_lru_lambda, m_w_out_a, m_w_out_b, m_gate_bias, m_w_o, m_ln2_g, m_w_ffn_gate, m_w_ffn_up, m_w_ffn_down, m_final_g, v_ln1_g, v_w_in, v_conv_a_w, v_conv_b_w, v_conv_b_b, v_lru_wa, v_lru_ba, v_lru_wx, v_lru_bx, v_lru_lambda, v_w_out_a, v_w_out_b, v_gate_bias, v_w_o, v_ln2_g, v_w_ffn_gate, v_w_ffn_up, v_w_ffn_down, v_final_g):
    w = dict(ln1_g=ln1_g, w_in=w_in, conv_a_w=conv_a_w, conv_b_w=conv_b_w, conv_b_b=conv_b_b, lru_wa=lru_wa, lru_ba=lru_ba,
             lru_wx=lru_wx, lru_bx=lru_bx, lru_lambda=lru_lambda, w_out_a=w_out_a, w_out_b=w_out_b, gate_bias=gate_bias, w_o=w_o,
             ln2_g=ln2_g, w_ffn_gate=w_ffn_gate, w_ffn_up=w_ffn_up, w_ffn_down=w_ffn_down, final_g=final_g)
    m = dict(ln1_g=m_ln1_g, w_in=m_w_in, conv_a_w=m_conv_a_w, conv_b_w=m_conv_b_w, conv_b_b=m_conv_b_b, lru_wa=m_lru_wa,
             lru_ba=m_lru_ba, lru_wx=m_lru_wx, lru_bx=m_lru_bx, lru_lambda=m_lru_lambda, w_out_a=m_w_out_a, w_out_b=m_w_out_b,
             gate_bias=m_gate_bias, w_o=m_w_o, ln2_g=m_ln2_g, w_ffn_gate=m_w_ffn_gate, w_ffn_up=m_w_ffn_up,
             w_ffn_down=m_w_ffn_down, final_g=m_final_g)
    v = dict(ln1_g=v_ln1_g, w_in=v_w_in, conv_a_w=v_conv_a_w, conv_b_w=v_conv_b_w, conv_b_b=v_conv_b_b, lru_wa=v_lru_wa,
             lru_ba=v_lru_ba, lru_wx=v_lru_wx, lru_bx=v_lru_bx, lru_lambda=v_lru_lambda, w_out_a=v_w_out_a, w_out_b=v_w_out_b,
             gate_bias=v_gate_bias, w_o=v_w_o, ln2_g=v_ln2_g, w_ffn_gate=v_w_ffn_gate, w_ffn_up=v_w_ffn_up,
             w_ffn_down=v_w_ffn_down, final_g=v_final_g)
    loss, grad_x, grads, delta, new_m, new_v = _step(w, m, v, x[0], loss_target[0])
    return (loss, grad_x[None], *[grads[n] for n in WEIGHTS], *[delta[n] for n in WEIGHTS],
            *[new_m[n] for n in WEIGHTS], *[new_v[n] for n in WEIGHTS])
```

```python
import functools

import jax
import jax.numpy as jnp
from jax import lax
from jax.experimental import pallas as pl
from jax.experimental.pallas import tpu as pltpu

F32 = jnp.float32
BF16 = jnp.bfloat16
MESH = pl.DeviceIdType.MESH

N_CHIP = 4
RMS_EPS = 1e-6
LRU_C = 8.0
LRU_HEAD_DIM = 64
LRU_BLOCK = 256
CONV_A_K = 3
CONV_B_K = 4
ADAM_LR = 0.001
ADAM_B1 = 0.9
ADAM_B2 = 0.999
ADAM_EPS = 1e-08
ADAM_WD = 0.01
ADAM_STEP = 10
SUBLANES = 8
VMEM_LIMIT = 56 * 1024 * 1024


def _params(*sem):
    return pltpu.CompilerParams(dimension_semantics=sem, vmem_limit_bytes=VMEM_LIMIT)


def _sigmoid(v):
    return 1.0 / (1.0 + jnp.exp(-v))


def _one_minus_sq(la, a):
    return jnp.tanh(-la) * (1.0 + a * a)


def _gelu_parts(v):
    k = 0.7978845608028654
    v2 = v * v
    t = jnp.tanh(k * (v + 0.044715 * v * v2))
    gelu = 0.5 * v * (1.0 + t)
    dgelu = 0.5 * (1.0 + t) + 0.5 * v * (1.0 - t * t) * k * (1.0 + 3 * 0.044715 * v2)
    return gelu, dgelu


def _shift_down(v, k, prev8):
    rolled = pltpu.roll(v, k, 0)
    r8 = lax.broadcasted_iota(jnp.int32, prev8.shape, 0)
    head = jnp.where(r8 < k, pltpu.roll(prev8, k, 0), rolled[0:SUBLANES])
    return jnp.concatenate([head, rolled[SUBLANES:]], axis=0)


def _shift_up(v, k, next8):
    tm = v.shape[0]
    rolled = pltpu.roll(v, tm - k, 0)
    r8 = lax.broadcasted_iota(jnp.int32, next8.shape, 0)
    tail = jnp.where(r8 >= SUBLANES - k, pltpu.roll(next8, SUBLANES - k, 0), rolled[tm - SUBLANES:])
    return jnp.concatenate([rolled[:tm - SUBLANES], tail], axis=0)


def _colsum8(v):
    tm, c = v.shape
    return jnp.sum(v.reshape(tm // SUBLANES, SUBLANES, c), axis=0)


def _rms_stats(xv):
    var = jnp.mean(xv * xv, axis=-1, keepdims=True)
    return lax.rsqrt(var + RMS_EPS)


def _rms_bwd(dh, xv, g):
    rstd = _rms_stats(xv)
    xhat = xv * rstd
    dxhat = dh * g
    dx = rstd * (dxhat - xhat * jnp.mean(dxhat * xhat, axis=-1, keepdims=True))
    return dx, dh * xhat


ANY = pl.BlockSpec(memory_space=pl.ANY)


def _place():
    x, y, c = lax.axis_index("x"), lax.axis_index("y"), lax.axis_index("c")
    other_chips = [(1 - x, y), (x, 1 - y), (1 - x, 1 - y)]
    return x, y, c, other_chips


def _chip_id(x, y):
    return 2 * x + y


def _half(c, hk):
    return pl.ds(pl.multiple_of(c * hk, 16), hk)


def _remote(src, dst, to, sems):
    return pltpu.make_async_remote_copy(src_ref=src, dst_ref=dst, device_id=to, device_id_type=MESH, **sems)


class _Carried:
    def __init__(self):
        self.ro, self.io, self.nw, self.parts, self.n = [], [], [], [], 0

    def add(self, maker, n, ro=(), io=(), nw=()):
        def index(items, item, same):
            for k, other in enumerate(items):
                if same(other, item):
                    return k
            items.append(item)
            return len(items) - 1

        r = [index(self.ro, a, lambda p, q: p is q) for a in ro]
        i = [index(self.io, a, lambda p, q: p[0] is q[0] and p[1] == q[1]) for a in io]
        w = [index(self.nw, a, lambda p, q: False) for a in nw]
        self.parts.append((maker, r, i, w, self.n))
        self.n += n
        return self

    def pairs(self, ro, io, nw, ssem, rsem):
        out = []
        for maker, r, i, w, base in self.parts:
            sems = lambda k, base=base: dict(send_sem=ssem.at[base + k], recv_sem=rsem.at[base + k])
            out += maker([ro[k] for k in r], [io[k] for k in i], [nw[k] for k in w], sems)
        return out

    def start(self, *refs):
        for send, _ in self.pairs(*refs):
            send.start()

    def finish(self, *refs):
        pairs = self.pairs(*refs)
        for _, recv in pairs:
            recv.wait_recv()
        for send, _ in pairs:
            send.wait_send()

    def operands(self):
        return list(self.ro) + [store[key] for store, key in self.io]

    def out_shapes(self):
        return [jax.ShapeDtypeStruct(store[key].shape, store[key].dtype) for store, key in self.io] + [s for _, _, s in self.nw]

    def keep(self, results):
        for (store, key), arr in zip(self.io, results[:len(self.io)]):
            store[key] = arr
        for (store, key, _), arr in zip(self.nw, results[len(self.io):]):
            store[key] = arr


def _call(body, comm, *, name, grid, in_specs, out_specs, out_shape, compiler_params, scratch_shapes=(), aliases=None):
    aliases = dict(aliases or {})
    if comm is None or not comm.parts:
        return pl.pallas_call(body, name=name, grid=grid, in_specs=in_specs, out_specs=out_specs, out_shape=out_shape,
                              scratch_shapes=list(scratch_shapes), input_output_aliases=aliases, compiler_params=compiler_params)
    n_in, n_out, n_scr = len(in_specs), len(out_shape), len(scratch_shapes)
    n_ro, n_io, n_nw = len(comm.ro), len(comm.io), len(comm.nw)

    def carried(*refs):
        base_in = refs[:n_in]
        ro = refs[n_in:n_in + n_ro]
        pos = n_in + n_ro + n_io
        base_out = refs[pos:pos + n_out]
        io = refs[pos + n_out:pos + n_out + n_io]
        nw = refs[pos + n_out + n_io:pos + n_out + n_io + n_nw]
        pos += n_out + n_io + n_nw
        scr = refs[pos:pos + n_scr]
        ssem, rsem = refs[pos + n_scr], refs[pos + n_scr + 1]
        first = pl.program_id(0) == 0
        last = pl.program_id(0) == grid[0] - 1
        for axis in range(1, len(grid)):
            first = first & (pl.program_id(axis) == 0)
            last = last & (pl.program_id(axis) == grid[axis] - 1)

        @pl.when(first)
        def _():
            comm.start(ro, io, nw, ssem, rsem)

        body(*base_in, *base_out, *scr)

        @pl.when(last)
        def _():
            comm.finish(ro, io, nw, ssem, rsem)

    aliases.update({n_in + n_ro + k: n_out + k for k in range(n_io)})
    dma = pltpu.SemaphoreType.DMA
    call = pl.pallas_call(
        carried, name=name, grid=grid,
        in_specs=list(in_specs) + [ANY] * (n_ro + n_io), out_specs=list(out_specs) + [ANY] * (n_io + n_nw),
        out_shape=list(out_shape) + comm.out_shapes(), input_output_aliases=aliases,
        scratch_shapes=list(scratch_shapes) + [dma((comm.n,)), dma((comm.n,))], compiler_params=compiler_params)

    def run(*operands):
        res = call(*operands, *comm.operands())
        comm.keep(res[n_out:])
        return res[:n_out]

    return run


def _run_comm(rounds, name):
    ro, io, nw, uses = [], [], [], []
    for r in rounds:
        def index(items, item, same):
            for k, other in enumerate(items):
                if same(other, item):
                    return k
            items.append(item)
            return len(items) - 1
        uses.append(([index(ro, a, lambda p, q: p is q) for a in r.ro],
                     [index(io, a, lambda p, q: p[0] is q[0] and p[1] == q[1]) for a in r.io],
                     [index(nw, a, lambda p, q: False) for a in r.nw]))
    n_ro, n_io, n_nw = len(ro), len(io), len(nw)

    def body(*refs):
        ro_refs = refs[:n_ro]
        io_refs = refs[n_ro + n_io:n_ro + 2 * n_io]
        nw_refs = refs[n_ro + 2 * n_io:n_ro + 2 * n_io + n_nw]
        sems = refs[n_ro + 2 * n_io + n_nw:]
        for k, (r, (a, b, c)) in enumerate(zip(rounds, uses)):
            args = ([ro_refs[i] for i in a], [io_refs[i] for i in b], [nw_refs[i] for i in c], sems[2 * k], sems[2 * k + 1])
            r.start(*args)
            r.finish(*args)

    operands = ro + [store[key] for store, key in io]
    out_shape = [jax.ShapeDtypeStruct(store[key].shape, store[key].dtype) for store, key in io] + [s for _, _, s in nw]
    dma = pltpu.SemaphoreType.DMA
    res = pl.pallas_call(
        body, name=name, out_shape=out_shape,
        in_specs=[ANY] * (n_ro + n_io), out_specs=[ANY] * (n_io + n_nw),
        input_output_aliases={n_ro + k: k for k in range(n_io)},
        scratch_shapes=[dma((r.n,)) for r in rounds for _ in range(2)],
    )(*operands)
    for (store, key), arr in zip(io, res[:n_io]):
        store[key] = arr
    for (store, key, _), arr in zip(nw, res[n_io:]):
        store[key] = arr


def _gather_ici(ro, io, nw, sems):
    s = io[0]
    x, y, c, chips = _place()
    hk = s.shape[1] // 2
    mine = s.at[_chip_id(x, y), _half(c, hk)]
    pairs = []
    for j, (cx, cy) in enumerate(chips):
        theirs = s.at[_chip_id(cx, cy), _half(c, hk)]
        pairs.append((_remote(mine, mine, (cx, cy, c), sems(j)), _remote(theirs, theirs, (cx, cy, c), sems(j))))
    return pairs


def _gather_d2d(ro, io, nw, sems):
    s = io[0]
    x, y, c, chips = _place()
    hk = s.shape[1] // 2
    sib = (x, y, 1 - c)
    pairs = []
    for j, (cx, cy) in enumerate(chips):
        here = s.at[_chip_id(cx, cy), _half(c, hk)]
        there = s.at[_chip_id(cx, cy), _half(1 - c, hk)]
        pairs.append((_remote(here, here, sib, sems(j)), _remote(there, there, sib, sems(j))))
    return pairs


def _gather_small(ro, io, nw, sems):
    x, y, c, chips = _place()
    return [(_remote(ro[0], nw[0].at[j], (cx, cy, c), sems(j)),) * 2 for j, (cx, cy) in enumerate(chips)]


def _reduce_halves(ro, io, nw, sems):
    x, y, c, _ = _place()
    g = ro[0]
    hk = g.shape[1] // 2
    sib = (x, y, 1 - c)
    return [(_remote(g.at[:, _half(1 - c, hk)], nw[0], sib, sems(0)), _remote(g.at[:, _half(c, hk)], nw[0], sib, sems(0)))]


def _reduce_chips(ro, io, nw, sems):
    x, y, c, chips = _place()
    return [(_remote(ro[0].at[_chip_id(cx, cy)], nw[0].at[j], (cx, cy, c), sems(j)),) * 2 for j, (cx, cy) in enumerate(chips)]


def _reduce_share(layer):
    def maker(ro, io, nw, sems):
        g = io[0]
        x, y, c, _ = _place()
        hk = g.shape[1] // 2
        sib = (x, y, 1 - c)
        mine, theirs = g.at[layer, _half(c, hk)], g.at[layer, _half(1 - c, hk)]
        return [(_remote(mine, mine, sib, sems(0)), _remote(theirs, theirs, sib, sems(0)))]
    return maker


def _rms_inproj(x, g_row, win, layer, tm, comm=None):
    T, D = x.shape
    ns = win.shape[-1]

    def body(x_ref, g_ref, w_ref, p_ref, h_ref):
        @pl.when(pl.program_id(1) == 0)
        def _():
            xv = x_ref[...]
            h_ref[...] = (xv * _rms_stats(xv) * g_ref[...]).astype(BF16)
        p_ref[...] = jnp.dot(h_ref[...], w_ref[...], preferred_element_type=F32).astype(BF16)

    return _call(
        body, comm, name=f"rms_inproj_{layer}", grid=(T // tm, N_CHIP),
        in_specs=[pl.BlockSpec((tm, D), lambda i, j: (i, 0)),
                  pl.BlockSpec((1, D), lambda i, j: (0, 0)),
                  pl.BlockSpec((None, D, ns), lambda i, j: (j, 0, 0))],
        out_specs=[pl.BlockSpec((tm, ns), lambda i, j: (i, j)),
                   pl.BlockSpec((tm, D), lambda i, j: (i, 0))],
        out_shape=[jax.ShapeDtypeStruct((T, N_CHIP * ns), BF16), jax.ShapeDtypeStruct((T, D), BF16)],
        compiler_params=_params("parallel", "arbitrary"),
    )(x, g_row, win)


def _mixer_recompute(ca, xa, xb, zprev8, xbprev8, cw_ref, wab_ref, sp):
    row = lambda k: cw_ref[pl.ds(k, 1), :]
    z = ca * xa
    z1 = _shift_down(z, 1, zprev8)
    z2 = _shift_down(z, 2, zprev8)
    cz = row(2) * z + row(1) * z1 + row(0) * z2
    x1 = _shift_down(xb, 1, xbprev8)
    x2 = _shift_down(xb, 2, xbprev8)
    x3 = _shift_down(xb, 3, xbprev8)
    u = row(6) * xb + row(5) * x1 + row(4) * x2 + row(3) * x3 + row(7)
    ub = u.astype(BF16)
    nb = wab_ref.shape[0]
    ras, ixs = [], []
    for b in range(nb):
        ri = jnp.dot(ub[:, b * LRU_BLOCK:(b + 1) * LRU_BLOCK], wab_ref[b], preferred_element_type=F32)
        ras.append(ri[:, :LRU_BLOCK])
        ixs.append(ri[:, LRU_BLOCK:])
    r = _sigmoid(jnp.concatenate(ras, axis=1) + row(8))
    gi = _sigmoid(jnp.concatenate(ixs, axis=1) + row(9))
    la = (-LRU_C) * r * sp
    a = jnp.exp(la)
    m = jnp.sqrt(_one_minus_sq(la, a))
    return dict(z=z, z1=z1, z2=z2, cz=cz, x1=x1, x2=x2, x3=x3, u=u, ub=ub, r=r, gi=gi, a=a, m=m)


def _softplus_neg(lam):
    v = -lam
    return jnp.maximum(v, 0.0) + jnp.log1p(jnp.exp(-jnp.abs(v)))


def _mixer_fwd(p, cw, wab, lam_row, layer, tm, comm=None):
    T = p.shape[0]
    D = p.shape[1] // 7
    ngroups = tm // SUBLANES

    def body(ba_ref, ca_ref, xa_ref, xb_ref, gb_ref, cw_ref, wab_ref, lam_ref, ya_ref, yb_ref, h_ref,
             zprev, xbprev, hcarry, a_s, h_s):
        @pl.when(pl.program_id(0) == 0)
        def _():
            zprev[...] = jnp.zeros_like(zprev)
            xbprev[...] = jnp.zeros_like(xbprev)
            hcarry[...] = jnp.zeros_like(hcarry)

        ca = ca_ref[...].astype(F32)
        xa = xa_ref[...].astype(F32)
        xb = xb_ref[...].astype(F32)
        sp = _softplus_neg(lam_ref[...])
        c = _mixer_recompute(ca, xa, xb, zprev[...], xbprev[...], cw_ref, wab_ref, sp)
        zprev[...] = c["z"][tm - SUBLANES:]
        xbprev[...] = xb[tm - SUBLANES:]
        ya_ref[...] = (ba_ref[...].astype(F32) * c["cz"]).astype(BF16)

        A = c["a"]
        B = c["m"] * c["gi"] * c["u"]
        q = lax.broadcasted_iota(jnp.int32, A.shape, 0) & (SUBLANES - 1)
        for s in (1, 2, 4):
            msk = q >= s
            B = jnp.where(msk, A * pltpu.roll(B, s, 0) + B, B)
            A = jnp.where(msk, A * pltpu.roll(A, s, 0), A)
        a_s[...] = A
        h_s[...] = B

        def step(g, carry):
            off = pl.multiple_of(g * SUBLANES, SUBLANES)
            hg = h_s[pl.ds(off, SUBLANES), :] + a_s[pl.ds(off, SUBLANES), :] * carry
            h_s[pl.ds(off, SUBLANES), :] = hg
            return jnp.broadcast_to(hg[SUBLANES - 1:SUBLANES, :], hg.shape)

        hcarry[...] = lax.fori_loop(0, ngroups, step, hcarry[...], unroll=4)
        h = h_s[...]
        h_ref[...] = h
        gelu, _ = _gelu_parts(gb_ref[...].astype(F32))
        yb_ref[...] = (h * gelu).astype(BF16)

    col = lambda k: pl.BlockSpec((tm, D), lambda i: (i, k))
    full = lambda a: pl.BlockSpec(a.shape, lambda i: (0,) * a.ndim)
    tok = pl.BlockSpec((tm, D), lambda i: (i, 0))
    return _call(
        body, comm, name=f"mixer_fwd_{layer}", grid=(T // tm,),
        in_specs=[col(0), col(1), col(2), col(3), col(4), full(cw), full(wab), full(lam_row)],
        out_specs=[tok, tok, tok],
        out_shape=[jax.ShapeDtypeStruct((T, D), BF16), jax.ShapeDtypeStruct((T, D), BF16), jax.ShapeDtypeStruct((T, D), F32)],
        scratch_shapes=[pltpu.VMEM((SUBLANES, D), F32), pltpu.VMEM((SUBLANES, D), F32), pltpu.VMEM((SUBLANES, D), F32),
                        pltpu.VMEM((tm, D), F32), pltpu.VMEM((tm, D), F32)],
        compiler_params=_params("arbitrary"),
    )(p, p, p, p, p, cw, wab, lam_row)


def _merge_fwd(x, p, ya, yb, woa, wob, wo, gbias, layer, tm, comm=None):
    T, D = x.shape

    def body(x_ref, ga_ref, gb_ref, ya_ref, yb_ref, woa_ref, wob_ref, wo_ref, bias_ref, oa_ref, ob_ref, mg_ref, x1_ref):
        oa = jnp.dot(ya_ref[...], woa_ref[...], preferred_element_type=F32)
        ob = jnp.dot(yb_ref[...], wob_ref[...], preferred_element_type=F32)
        sa = _sigmoid(ga_ref[...].astype(F32) + bias_ref[pl.ds(0, 1), :])
        sb = _sigmoid(gb_ref[...].astype(F32) + bias_ref[pl.ds(1, 1), :])
        mg = (sa * oa + sb * ob).astype(BF16)
        oa_ref[...] = oa.astype(BF16)
        ob_ref[...] = ob.astype(BF16)
        mg_ref[...] = mg
        x1_ref[...] = x_ref[...] + jnp.dot(mg, wo_ref[...], preferred_element_type=F32)

    tok = pl.BlockSpec((tm, D), lambda i: (i, 0))
    wsp = pl.BlockSpec((D, D), lambda i: (0, 0))
    bf = jax.ShapeDtypeStruct((T, D), BF16)
    return _call(
        body, comm, name=f"merge_fwd_{layer}", grid=(T // tm,),
        in_specs=[tok, pl.BlockSpec((tm, D), lambda i: (i, 5)), pl.BlockSpec((tm, D), lambda i: (i, 6)), tok, tok,
                  wsp, wsp, wsp, pl.BlockSpec(gbias.shape, lambda i: (0, 0))],
        out_specs=[tok, tok, tok, tok],
        out_shape=[bf, bf, bf, jax.ShapeDtypeStruct((T, D), F32)],
        compiler_params=_params("parallel"),
    )(x, p, p, ya, yb, woa, wob, wo, gbias)


def _ffn_fwd(x1, g_row, wg, wu, wd, layer, tm, comm=None):
    T, D = x1.shape
    fs = wg.shape[-2]
    nt = (((1,), (1,)), ((), ()))

    def body(x_ref, g_ref, wg_ref, wu_ref, wd_ref, h_ref, gg_ref, uu_ref, x2_ref, acc):
        j = pl.program_id(1)

        @pl.when(j == 0)
        def _():
            xv = x_ref[...]
            h_ref[...] = (xv * _rms_stats(xv) * g_ref[...]).astype(BF16)
            acc[...] = xv

        h = h_ref[...]
        gg = lax.dot_general(h, wg_ref[...], nt, preferred_element_type=F32)
        uu = lax.dot_general(h, wu_ref[...], nt, preferred_element_type=F32)
        gg_ref[...] = gg.astype(BF16)
        uu_ref[...] = uu.astype(BF16)
        act = (gg * _sigmoid(gg) * uu).astype(BF16)
        acc[...] += jnp.dot(act, wd_ref[...], preferred_element_type=F32)

        @pl.when(j == N_CHIP - 1)
        def _():
            x2_ref[...] = acc[...]

    tok = pl.BlockSpec((tm, D), lambda i, j: (i, 0))
    cm = pl.BlockSpec((None, tm, fs), lambda i, j: (j, i, 0))
    return _call(
        body, comm, name=f"ffn_fwd_{layer}", grid=(T // tm, N_CHIP),
        in_specs=[tok, pl.BlockSpec((1, D), lambda i, j: (0, 0)),
                  pl.BlockSpec((None, fs, D), lambda i, j: (j, 0, 0)),
                  pl.BlockSpec((None, fs, D), lambda i, j: (j, 0, 0)),
                  pl.BlockSpec((None, fs, D), lambda i, j: (j, 0, 0))],
        out_specs=[tok, cm, cm, tok],
        out_shape=[jax.ShapeDtypeStruct((T, D), BF16), jax.ShapeDtypeStruct((N_CHIP, T, fs), BF16),
                   jax.ShapeDtypeStruct((N_CHIP, T, fs), BF16), jax.ShapeDtypeStruct((T, D), F32)],
        scratch_shapes=[pltpu.VMEM((tm, D), F32)],
        compiler_params=_params("parallel", "arbitrary"),
    )(x1, g_row, wg, wu, wd)


def _final_loss(x, g_row, target, tm):
    T, D = x.shape
    n = T // tm

    def body(x_ref, g_ref, t_ref, dx_ref, red_ref, acc):
        i = pl.program_id(0)

        @pl.when(i == 0)
        def _():
            acc[...] = jnp.zeros_like(acc)

        xv = x_ref[...]
        g = g_ref[...]
        rstd = _rms_stats(xv)
        xhat = xv * rstd
        err = xhat * g - t_ref[...]
        dy = err * (1.0 / D)
        dxhat = dy * g
        dx_ref[...] = rstd * (dxhat - xhat * jnp.mean(dxhat * xhat, axis=-1, keepdims=True))
        acc[0] += _colsum8(err * err)
        acc[1] += _colsum8(dy * xhat)

        @pl.when(i == n - 1)
        def _():
            red_ref[pl.ds(0, 1), :] = jnp.sum(acc[0], axis=0, keepdims=True) * (0.5 / D)
            red_ref[pl.ds(1, 1), :] = jnp.sum(acc[1], axis=0, keepdims=True)

    tok = pl.BlockSpec((tm, D), lambda i: (i, 0))
    return pl.pallas_call(
        body, name="final_loss", grid=(n,),
        in_specs=[tok, pl.BlockSpec((1, D), lambda i: (0, 0)), tok],
        out_specs=[tok, pl.BlockSpec((2, D), lambda i: (0, 0))],
        out_shape=[jax.ShapeDtypeStruct((T, D), F32), jax.ShapeDtypeStruct((2, D), F32)],
        scratch_shapes=[pltpu.VMEM((2, SUBLANES, D), F32)],
        compiler_params=_params("arbitrary"),
    )(x, g_row, target)


def _ffn_bwd(dx2, x1, g_row, gg, uu, wg, wu, wd, layer, tm, comm=None):
    T, D = dx2.shape
    fs = wg.shape[-2]
    n = T // tm
    nt = (((1,), (1,)), ((), ()))

    def body(dx_ref, x_ref, g_ref, gg_ref, uu_ref, wg_ref, wu_ref, wd_ref, dg_ref, du_ref, act_ref, dx1_ref, dxb_ref, red_ref,
             acc, racc):
        i = pl.program_id(0)
        j = pl.program_id(1)

        @pl.when((i == 0) & (j == 0))
        def _():
            racc[...] = jnp.zeros_like(racc)

        @pl.when(j == 0)
        def _():
            dxb_ref[...] = dx_ref[...].astype(BF16)
            acc[...] = jnp.zeros_like(acc)

        dact = lax.dot_general(dxb_ref[...], wd_ref[j], nt, preferred_element_type=F32)
        g = gg_ref[...].astype(F32)
        u = uu_ref[...].astype(F32)
        s = _sigmoid(g)
        silu = g * s
        dg = (dact * u * (s * (1.0 + g * (1.0 - s)))).astype(BF16)
        du = (dact * silu).astype(BF16)
        dg_ref[...] = dg
        du_ref[...] = du
        act_ref[...] = (silu * u).astype(BF16)
        acc[...] += (jnp.dot(dg, wg_ref[j], preferred_element_type=F32)
                     + jnp.dot(du, wu_ref[j], preferred_element_type=F32))

        @pl.when(j == N_CHIP - 1)
        def _():
            dx, dgain = _rms_bwd(acc[...], x_ref[...], g_ref[...])
            dx1_ref[...] = dx_ref[...] + dx
            racc[...] += _colsum8(dgain)

        @pl.when((i == n - 1) & (j == N_CHIP - 1))
        def _():
            red_ref[...] = jnp.sum(racc[...], axis=0, keepdims=True)

    tok = pl.BlockSpec((tm, D), lambda i, j: (i, 0))
    cm = pl.BlockSpec((None, tm, fs), lambda i, j: (j, i, 0))
    cms = jax.ShapeDtypeStruct((N_CHIP, T, fs), BF16)
    resident = pl.BlockSpec((N_CHIP, fs, D), lambda i, j: (0, 0, 0), pipeline_mode=pl.Buffered(1))
    return _call(
        body, comm, name=f"ffn_bwd_{layer}", grid=(n, N_CHIP),
        in_specs=[tok, tok, pl.BlockSpec((1, D), lambda i, j: (0, 0)), cm, cm, resident, resident, resident],
        out_specs=[cm, cm, cm, tok, tok, pl.BlockSpec((1, D), lambda i, j: (0, 0))],
        out_shape=[cms, cms, cms, jax.ShapeDtypeStruct((T, D), F32), jax.ShapeDtypeStruct((T, D), BF16),
                   jax.ShapeDtypeStruct((1, D), F32)],
        scratch_shapes=[pltpu.VMEM((tm, D), F32), pltpu.VMEM((SUBLANES, D), F32)],
        compiler_params=_params("arbitrary", "arbitrary"),
    )(dx2, x1, g_row, gg, uu, wg, wu, wd)


def _merge_bwd(dx1, p, oa, ob, woa, wob, wo, gbias, layer, tm, comm=None):
    T, D = dx1.shape
    n = T // tm
    nt = (((1,), (1,)), ((), ()))

    def body(dx_ref, ga_ref, gb_ref, oa_ref, ob_ref, woa_ref, wob_ref, wo_ref, bias_ref,
             dya_ref, dyb_ref, doa_ref, dob_ref, dgl_ref, dxb_ref, red_ref, racc):
        i = pl.program_id(0)

        @pl.when(i == 0)
        def _():
            racc[...] = jnp.zeros_like(racc)

        dxb = dx_ref[...].astype(BF16)
        dxb_ref[...] = dxb
        dm = lax.dot_general(dxb, wo_ref[...], nt, preferred_element_type=F32)
        sa = _sigmoid(ga_ref[...].astype(F32) + bias_ref[pl.ds(0, 1), :])
        sb = _sigmoid(gb_ref[...].astype(F32) + bias_ref[pl.ds(1, 1), :])
        doa = (dm * sa).astype(BF16)
        dob = (dm * sb).astype(BF16)
        dga = dm * oa_ref[...].astype(F32) * (sa * (1.0 - sa))
        dgb = dm * ob_ref[...].astype(F32) * (sb * (1.0 - sb))
        doa_ref[...] = doa
        dob_ref[...] = dob
        dgl_ref[:, 0:D] = dga.astype(BF16)
        dgl_ref[:, D:2 * D] = dgb.astype(BF16)
        racc[0] += _colsum8(dga)
        racc[1] += _colsum8(dgb)
        dya_ref[...] = lax.dot_general(doa, woa_ref[...], nt, preferred_element_type=F32).astype(BF16)
        dyb_ref[...] = lax.dot_general(dob, wob_ref[...], nt, preferred_element_type=F32).astype(BF16)

        @pl.when(i == n - 1)
        def _():
            red_ref[pl.ds(0, 1), :] = jnp.sum(racc[0], axis=0, keepdims=True)
            red_ref[pl.ds(1, 1), :] = jnp.sum(racc[1], axis=0, keepdims=True)

    tok = pl.BlockSpec((tm, D), lambda i: (i, 0))
    wsp = pl.BlockSpec((D, D), lambda i: (0, 0))
    bf = jax.ShapeDtypeStruct((T, D), BF16)
    return _call(
        body, comm, name=f"merge_bwd_{layer}", grid=(n,),
        in_specs=[tok, pl.BlockSpec((tm, D), lambda i: (i, 5)), pl.BlockSpec((tm, D), lambda i: (i, 6)), tok, tok,
                  wsp, wsp, wsp, pl.BlockSpec(gbias.shape, lambda i: (0, 0))],
        out_specs=[tok, tok, tok, tok, pl.BlockSpec((tm, 2 * D), lambda i: (i, 0)), tok, pl.BlockSpec((2, D), lambda i: (0, 0))],
        out_shape=[bf, bf, bf, bf, jax.ShapeDtypeStruct((T, 2 * D), BF16), bf, jax.ShapeDtypeStruct((2, D), F32)],
        scratch_shapes=[pltpu.VMEM((2, SUBLANES, D), F32)],
        compiler_params=_params("arbitrary"),
    )(dx1, p, p, oa, ob, woa, wob, wo, gbias)


N_MIXER_RED = 16


def _mixer_bwd(p, hseq, dya, dyb, dgl, cw, wab, wabt, lam_row, layer, tm, comm=None):
    T = p.shape[0]
    D = p.shape[1] // 7
    n = T // tm
    ngroups = tm // SUBLANES
    nb = wab.shape[0]
    hb = 16
    tn = (((0,), (0,)), ((), ()))

    def body(ba_ref, ca_ref, xa_ref, xb_ref, gb_ref, h_ref, dya_ref, dyb_ref, dgl_ref,
             cap_ref, xap_ref, xbp_ref, hp_ref, ban_ref, dyan_ref,
             cw_ref, wab_ref, wabt_ref, lam_ref,
             dp_ref, red_ref, dwab_ref,
             racc, wacc, anext, gnext, dunext, c_s, g_s):
        i = pl.program_id(0)
        first_tile = i == n - 1
        last_tile = i == 0

        @pl.when(i == 0)
        def _():
            racc[...] = jnp.zeros_like(racc)
            wacc[...] = jnp.zeros_like(wacc)
            anext[...] = jnp.zeros_like(anext)
            gnext[...] = jnp.zeros_like(gnext)
            dunext[...] = jnp.zeros_like(dunext)

        keep_prev = jnp.where(first_tile, 0.0, 1.0)
        keep_next = jnp.where(last_tile, 0.0, 1.0)
        ba = ba_ref[...].astype(F32)
        ca = ca_ref[...].astype(F32)
        xa = xa_ref[...].astype(F32)
        xb = xb_ref[...].astype(F32)
        h = h_ref[...]
        dya = dya_ref[...].astype(F32)
        dyb = dyb_ref[...].astype(F32)
        zprev8 = (cap_ref[...].astype(F32) * xap_ref[...].astype(F32))[hb - SUBLANES:] * keep_prev
        xbprev8 = xbp_ref[...].astype(F32)[hb - SUBLANES:] * keep_prev
        hprev8 = hp_ref[...] * keep_prev
        dcznext8 = (dyan_ref[...].astype(F32) * ban_ref[...].astype(F32))[:SUBLANES] * keep_next

        lam = lam_ref[...]
        sp = _softplus_neg(lam)
        c = _mixer_recompute(ca, xa, xb, zprev8, xbprev8, cw_ref, wab_ref, sp)
        row = lambda k: cw_ref[pl.ds(k, 1), :]
        a, m, r, gi, u = c["a"], c["m"], c["r"], c["gi"], c["u"]

        gelu, dgelu = _gelu_parts(gb_ref[...].astype(F32))
        dgb = dyb * h * dgelu
        Dv = dyb * gelu
        Cv = _shift_up(a, 1, anext[...])
        q = lax.broadcasted_iota(jnp.int32, Cv.shape, 0) & (SUBLANES - 1)
        for s in (1, 2, 4):
            msk = q < SUBLANES - s
            Dv = jnp.where(msk, Dv + Cv * pltpu.roll(Dv, tm - s, 0), Dv)
            Cv = jnp.where(msk, Cv * pltpu.roll(Cv, tm - s, 0), Cv)
        c_s[...] = Cv
        g_s[...] = Dv

        def step(k, carry):
            off = pl.multiple_of((ngroups - 1 - k) * SUBLANES, SUBLANES)
            gg = g_s[pl.ds(off, SUBLANES), :] + c_s[pl.ds(off, SUBLANES), :] * carry
            g_s[pl.ds(off, SUBLANES), :] = gg
            return jnp.broadcast_to(gg[0:1, :], gg.shape)

        gnext[...] = lax.fori_loop(0, ngroups, step, gnext[...], unroll=4)
        anext[...] = a[0:SUBLANES]
        g = g_s[...]

        hprev = _shift_down(h, 1, hprev8)
        da = g * hprev
        gm = g * m
        dgi = gm * u
        du = gm * gi
        dmv = g * gi * u
        dla = a * (da - dmv * a / m)
        dra = dla * ((-LRU_C) * sp) * (r * (1.0 - r))
        dix = dgi * (gi * (1.0 - gi))
        racc[10] += _colsum8(dla * r)
        racc[8] += _colsum8(dra)
        racc[9] += _colsum8(dix)
        drab = dra.astype(BF16)
        dixb = dix.astype(BF16)
        ub = c["ub"]
        dus = []
        for b in range(nb):
            sl = slice(b * LRU_BLOCK, (b + 1) * LRU_BLOCK)
            dri = jnp.concatenate([drab[:, sl], dixb[:, sl]], axis=1)
            dus.append(jnp.dot(dri, wabt_ref[b], preferred_element_type=F32))
            wacc[b] += lax.dot_general(ub[:, sl], dri, tn, preferred_element_type=F32)
        du = du + jnp.concatenate(dus, axis=1)

        dun = dunext[...]
        du1 = _shift_up(du, 1, dun)
        du2 = _shift_up(du, 2, dun)
        du3 = _shift_up(du, 3, dun)
        dxb = row(6) * du + row(5) * du1 + row(4) * du2 + row(3) * du3
        dunext[...] = du[0:SUBLANES]
        racc[6] += _colsum8(du * xb)
        racc[5] += _colsum8(du * c["x1"])
        racc[4] += _colsum8(du * c["x2"])
        racc[3] += _colsum8(du * c["x3"])
        racc[7] += _colsum8(du)

        dba = dya * c["cz"]
        dcz = dya * ba
        dcz1 = _shift_up(dcz, 1, dcznext8)
        dcz2 = _shift_up(dcz, 2, dcznext8)
        dz = row(2) * dcz + row(1) * dcz1 + row(0) * dcz2
        racc[2] += _colsum8(dcz * c["z"])
        racc[1] += _colsum8(dcz * c["z1"])
        racc[0] += _colsum8(dcz * c["z2"])

        dp_ref[:, 0:D] = dba.astype(BF16)
        dp_ref[:, D:2 * D] = (dz * xa).astype(BF16)
        dp_ref[:, 2 * D:3 * D] = (dz * ca).astype(BF16)
        dp_ref[:, 3 * D:4 * D] = dxb.astype(BF16)
        dp_ref[:, 4 * D:5 * D] = dgb.astype(BF16)
        dp_ref[:, 5 * D:7 * D] = dgl_ref[...]

        @pl.when(i == n - 1)
        def _():
            dlam_scale = LRU_C * _sigmoid(-lam)
            for k in range(N_MIXER_RED):
                tot = jnp.sum(racc[k], axis=0, keepdims=True)
                red_ref[pl.ds(k, 1), :] = tot * dlam_scale if k == 10 else tot
            dwab_ref[...] = wacc[...]

    rt = lambda i: n - 1 - i
    col = lambda k: pl.BlockSpec((tm, D), lambda i: (rt(i), k))
    tok = pl.BlockSpec((tm, D), lambda i: (rt(i), 0))
    full = lambda a: pl.BlockSpec(a.shape, lambda i: (0,) * a.ndim)
    prev16 = lambda k: pl.BlockSpec((hb, D), lambda i: (jnp.maximum(rt(i) * (tm // hb) - 1, 0), k))
    next16 = lambda k: pl.BlockSpec((hb, D), lambda i: (jnp.minimum((rt(i) + 1) * (tm // hb), T // hb - 1), k))
    hprev = pl.BlockSpec((SUBLANES, D), lambda i: (jnp.maximum(rt(i) * ngroups - 1, 0), 0))
    return _call(
        body, comm, name=f"mixer_bwd_{layer}", grid=(n,),
        in_specs=[col(0), col(1), col(2), col(3), col(4), tok, tok, tok, pl.BlockSpec((tm, 2 * D), lambda i: (rt(i), 0)),
                  prev16(1), prev16(2), prev16(3), hprev, next16(0), next16(0),
                  full(cw), full(wab), full(wabt), full(lam_row)],
        out_specs=[pl.BlockSpec((tm, 7 * D), lambda i: (rt(i), 0)),
                   pl.BlockSpec((N_MIXER_RED, D), lambda i: (0, 0)),
                   pl.BlockSpec((nb, LRU_BLOCK, 2 * LRU_BLOCK), lambda i: (0, 0, 0))],
        out_shape=[jax.ShapeDtypeStruct((T, 7 * D), BF16), jax.ShapeDtypeStruct((N_MIXER_RED, D), F32),
                   jax.ShapeDtypeStruct((nb, LRU_BLOCK, 2 * LRU_BLOCK), F32)],
        scratch_shapes=[pltpu.VMEM((N_MIXER_RED, SUBLANES, D), F32), pltpu.VMEM((nb, LRU_BLOCK, 2 * LRU_BLOCK), F32),
                        pltpu.VMEM((SUBLANES, D), F32), pltpu.VMEM((SUBLANES, D), F32), pltpu.VMEM((SUBLANES, D), F32),
                        pltpu.VMEM((tm, D), F32), pltpu.VMEM((tm, D), F32)],
        compiler_params=_params("arbitrary"),
    )(p, p, p, p, p, hseq, dya, dyb, dgl, p, p, p, hseq, p, dya, cw, wab, wabt, lam_row)


def _inproj_bwd(dp, dx1, x, g_row, win, layer, tm, comm=None):
    T, D = x.shape
    ns = win.shape[-1]
    n = T // tm
    nt = (((1,), (1,)), ((), ()))

    def body(dp_ref, dx_ref, x_ref, g_ref, w_ref, dx0_ref, red_ref, acc, racc):
        i = pl.program_id(0)
        j = pl.program_id(1)

        @pl.when((i == 0) & (j == 0))
        def _():
            racc[...] = jnp.zeros_like(racc)

        @pl.when(j == 0)
        def _():
            acc[...] = jnp.zeros_like(acc)

        acc[...] += lax.dot_general(dp_ref[...], w_ref[...], nt, preferred_element_type=F32)

        @pl.when(j == N_CHIP - 1)
        def _():
            dx, dgain = _rms_bwd(acc[...], x_ref[...], g_ref[...])
            dx0_ref[...] = dx_ref[...] + dx
            racc[...] += _colsum8(dgain)

        @pl.when((i == n - 1) & (j == N_CHIP - 1))
        def _():
            red_ref[...] = jnp.sum(racc[...], axis=0, keepdims=True)

    tok = pl.BlockSpec((tm, D), lambda i, j: (i, 0))
    return _call(
        body, comm, name=f"inproj_bwd_{layer}", grid=(n, N_CHIP),
        in_specs=[pl.BlockSpec((tm, ns), lambda i, j: (i, j)), tok, tok, pl.BlockSpec((1, D), lambda i, j: (0, 0)),
                  pl.BlockSpec((None, D, ns), lambda i, j: (j, 0, 0))],
        out_specs=[tok, pl.BlockSpec((1, D), lambda i, j: (0, 0))],
        out_shape=[jax.ShapeDtypeStruct((T, D), F32), jax.ShapeDtypeStruct((1, D), F32)],
        scratch_shapes=[pltpu.VMEM((tm, D), F32), pltpu.VMEM((SUBLANES, D), F32)],
        compiler_params=_params("arbitrary", "arbitrary"),
    )(dp, dx1, x, g_row, win)


def _wgrad(a, b, name, tk, a_kind="whole", b_kind="whole", nj=1, comm=None):
    T = a.shape[-2]
    width = lambda v, kind: v.shape[-1] // nj if kind == "cols" else v.shape[-1]
    ka, kb = width(a, a_kind), width(b, b_kind)
    nt = T // tk
    tn = (((0,), (0,)), ((), ()))

    def spec(k, kind):
        if kind == "cm":
            return pl.BlockSpec((None, tk, k), lambda j, t: (j, t, 0))
        if kind == "cols":
            return pl.BlockSpec((tk, k), lambda j, t: (t, j))
        return pl.BlockSpec((tk, k), lambda j, t: (t, 0))

    def body(a_ref, b_ref, o_ref, ob_ref):
        t = pl.program_id(1)

        @pl.when(t == 0)
        def _():
            o_ref[...] = jnp.zeros_like(o_ref)

        o_ref[...] += lax.dot_general(a_ref[...], b_ref[...], tn, preferred_element_type=F32)

        @pl.when(t == nt - 1)
        def _():
            ob_ref[...] = o_ref[...].astype(BF16)

    o_spec = pl.BlockSpec((None, ka, kb), lambda j, t: (j, 0, 0))
    return _call(
        body, comm, name=name, grid=(nj, nt),
        in_specs=[spec(ka, a_kind), spec(kb, b_kind)], out_specs=[o_spec, o_spec],
        out_shape=[jax.ShapeDtypeStruct((nj, ka, kb), F32), jax.ShapeDtypeStruct((nj, ka, kb), BF16)],
        compiler_params=_params("parallel", "arbitrary"),
    )(a, b)


def _wgrad_pair(a, b1, b2, name, tk, comm=None):
    T, ka = a.shape
    nj, _, kb = b1.shape
    nt = T // tk
    tn = (((0,), (0,)), ((), ()))

    def body(a_ref, b1_ref, b2_ref, o1_ref, o1b_ref, o2_ref, o2b_ref):
        t = pl.program_id(1)

        @pl.when(t == 0)
        def _():
            o1_ref[...] = jnp.zeros_like(o1_ref)
            o2_ref[...] = jnp.zeros_like(o2_ref)

        av = a_ref[...]
        o1_ref[...] += lax.dot_general(b1_ref[...], av, tn, preferred_element_type=F32)
        o2_ref[...] += lax.dot_general(b2_ref[...], av, tn, preferred_element_type=F32)

        @pl.when(t == nt - 1)
        def _():
            o1b_ref[...] = o1_ref[...].astype(BF16)
            o2b_ref[...] = o2_ref[...].astype(BF16)

    b_spec = pl.BlockSpec((None, tk, kb), lambda j, t: (j, t, 0))
    o_spec = pl.BlockSpec((None, kb, ka), lambda j, t: (j, 0, 0))
    f32 = jax.ShapeDtypeStruct((nj, kb, ka), F32)
    b16 = jax.ShapeDtypeStruct((nj, kb, ka), BF16)
    return _call(
        body, comm, name=name, grid=(nj, nt),
        in_specs=[pl.BlockSpec((tk, ka), lambda j, t: (t, 0)), b_spec, b_spec], out_specs=[o_spec] * 4,
        out_shape=[f32, b16, f32, b16], compiler_params=_params("parallel", "arbitrary"),
    )(a, b1, b2)


def _block_diag(w):
    hb = LRU_BLOCK // LRU_HEAD_DIM
    nb = w.shape[0] // hb
    w4 = w.reshape(nb, hb, LRU_HEAD_DIM, LRU_HEAD_DIM)
    eye = jnp.eye(hb, dtype=w.dtype)
    return jnp.einsum("bide,ij->bidje", w4, eye).reshape(nb, LRU_BLOCK, LRU_BLOCK)


def _diag_heads(m):
    hb = LRU_BLOCK // LRU_HEAD_DIM
    nb = m.shape[0]
    m5 = m.reshape(nb, hb, LRU_HEAD_DIM, hb, LRU_HEAD_DIM)
    eye = jnp.eye(hb, dtype=m.dtype)
    return jnp.einsum("bidje,ij->bide", m5, eye).reshape(nb * hb, LRU_HEAD_DIM, LRU_HEAD_DIM)


def _tiles(T):
    cap = lambda n: min(n, T)
    return dict(inproj=cap(1024), mixer=cap(256), merge=cap(512), ffn=cap(1024), ffn_bwd=cap(512), loss=cap(512), inproj_bwd=cap(1024),
                wgrad_in=cap(1024), wgrad=cap(2048))


class _NoSchedule:
    def carry(self, name):
        return None

    def after(self, name):
        pass

    def grad(self, key, layer, f32, b16):
        pass


def _local_step(x, target, W, small, tiles, sched):
    L = small["ln1_g"].shape[0]
    D = x.shape[1]
    square = lambda a: a.reshape(D, D)
    saved = []
    h = x
    for l in range(L):
        cw = jnp.concatenate([small["conv_a_w"][l], small["conv_b_w"][l], small["conv_b_b"][l][None],
                              small["lru_ba"][l][None], small["lru_bx"][l][None]], axis=0)
        wab = jnp.concatenate([_block_diag(small["lru_wa"][l]), _block_diag(small["lru_wx"][l])], axis=2).astype(BF16)
        wabt = jnp.swapaxes(wab, 1, 2)
        lam_row = small["lru_lambda"][l][None]
        ln1_row = small["ln1_g"][l][None]
        ln2_row = small["ln2_g"][l][None]
        p, h1 = _rms_inproj(h, ln1_row, W["win", l], l, tiles["inproj"], sched.carry(f"rms_inproj_{l}"))
        ya, yb, hseq = _mixer_fwd(p, cw, wab, lam_row, l, tiles["mixer"], sched.carry(f"mixer_fwd_{l}"))
        oa, ob, mg, x1 = _merge_fwd(h, p, ya, yb, square(W["woa", l]), square(W["wob", l]), square(W["wo", l]),
                                    small["gate_bias"][l], l, tiles["merge"], sched.carry(f"merge_fwd_{l}"))
        h2, gg, uu, x2 = _ffn_fwd(x1, ln2_row, W["wg", l], W["wu", l], W["wd", l], l, tiles["ffn"], sched.carry(f"ffn_fwd_{l}"))
        saved.append(dict(x0=h, p=p, h1=h1, ya=ya, yb=yb, hseq=hseq, oa=oa, ob=ob, mg=mg, x1=x1, h2=h2, gg=gg, uu=uu,
                          cw=cw, wab=wab, wabt=wabt, lam_row=lam_row, ln1_row=ln1_row, ln2_row=ln2_row))
        h = x2

    dx, red = _final_loss(h, small["final_g"][None], target, tiles["loss"])
    loss_row, d_final_g = red[0], red[1]

    gsmall = {k: [None] * L for k in ("ln1_g", "ln2_g", "conv_a_w", "conv_b_w", "conv_b_b", "lru_wa", "lru_ba", "lru_wx",
                                      "lru_bx", "lru_lambda", "gate_bias")}
    tk = tiles["wgrad"]
    for l in reversed(range(L)):
        s = saved[l]
        dgg, duu, act, dx1, dx2b, dln2 = _ffn_bwd(dx, s["x1"], s["ln2_row"], s["gg"], s["uu"], W["wg", l], W["wu", l], W["wd", l],
                                                 l, tiles["ffn_bwd"], sched.carry(f"ffn_bwd_{l}"))
        sched.after(f"ffn_bwd_{l}")
        gate_up = _wgrad_pair(s["h2"], dgg, duu, f"wgrad_ffn_gate_up_{l}", tk, sched.carry(f"wgrad_ffn_gate_up_{l}"))
        sched.grad("wg", l, *gate_up[0:2])
        sched.grad("wu", l, *gate_up[2:4])
        sched.after(f"wgrad_ffn_gate_up_{l}")
        sched.grad("wd", l, *_wgrad(act, dx2b, f"wgrad_ffn_down_{l}", tk, "cm", "whole", N_CHIP, sched.carry(f"wgrad_ffn_down_{l}")))
        dya, dyb, doa, dob, dgl, dx1b, dgbias = _merge_bwd(dx1, s["p"], s["oa"], s["ob"], square(W["woa", l]), square(W["wob", l]),
                                                         square(W["wo", l]), small["gate_bias"][l], l, tiles["merge"],
                                                         sched.carry(f"merge_bwd_{l}"))
        sched.after(f"merge_bwd_{l}")
        sched.grad("wo", l, *_wgrad(s["mg"], dx1b, f"wgrad_w_o_{l}", tk))
        sched.grad("woa", l, *_wgrad(s["ya"], doa, f"wgrad_w_out_a_{l}", tk))
        sched.grad("wob", l, *_wgrad(s["yb"], dob, f"wgrad_w_out_b_{l}", tk))
        dp, mred, dwab = _mixer_bwd(s["p"], s["hseq"], dya, dyb, dgl, s["cw"], s["wab"], s["wabt"], s["lam_row"], l,
                                    tiles["mixer"], sched.carry(f"mixer_bwd_{l}"))
        sched.after(f"mixer_bwd_{l}")
        dx, dln1 = _inproj_bwd(dp, dx1, s["x0"], s["ln1_row"], W["win", l], l, tiles["inproj_bwd"], sched.carry(f"inproj_bwd_{l}"))
        sched.after(f"inproj_bwd_{l}")
        sched.grad("win", l, *_wgrad(s["h1"], dp, f"wgrad_w_in_{l}", tiles["wgrad_in"], "whole", "cols", N_CHIP,
                                     sched.carry(f"wgrad_w_in_{l}")))
        sched.after(f"wgrad_w_in_{l}")
        gsmall["ln1_g"][l] = dln1[0]
        gsmall["ln2_g"][l] = dln2[0]
        gsmall["conv_a_w"][l] = mred[0:CONV_A_K]
        gsmall["conv_b_w"][l] = mred[CONV_A_K:CONV_A_K + CONV_B_K]
        gsmall["conv_b_b"][l] = mred[7]
        gsmall["lru_ba"][l] = mred[8]
        gsmall["lru_bx"][l] = mred[9]
        gsmall["lru_lambda"][l] = mred[10]
        gsmall["lru_wa"][l] = _diag_heads(dwab[:, :, :LRU_BLOCK])
        gsmall["lru_wx"][l] = _diag_heads(dwab[:, :, LRU_BLOCK:])
        gsmall["gate_bias"][l] = dgbias
    gsmall = {k: jnp.stack(v) for k, v in gsmall.items()}
    gsmall["final_g"] = d_final_g
    return loss_row, dx, gsmall


def _small_allreduce(buf):
    R, C = buf.shape
    n_dev = 8
    rp = R // n_dev
    rel = [(k >> 2 & 1, k >> 1 & 1, k & 1) for k in range(1, n_dev)]

    def body(in_ref, out_ref, recv, s1, r1, s2, r2):
        x, y, c, _ = _place()
        flip = lambda v, bit: 1 - v if bit else v
        peers = [(flip(x, kx), flip(y, ky), flip(c, kc)) for kx, ky, kc in rel]
        dev = lambda p: 4 * p[0] + 2 * p[1] + p[2]
        part = lambda ref, d: ref.at[pl.ds(pl.multiple_of(d * rp, SUBLANES), rp), :]
        me = dev((x, y, c))

        def scatter(k, src_dev, to):
            return pltpu.make_async_remote_copy(src_ref=part(in_ref, dev(to)), dst_ref=recv.at[src_dev], send_sem=s1.at[k],
                                                recv_sem=r1.at[k], device_id=to, device_id_type=MESH)

        def gather(k, src_dev, to):
            return pltpu.make_async_remote_copy(src_ref=part(out_ref, src_dev), dst_ref=part(out_ref, src_dev), send_sem=s2.at[k],
                                                recv_sem=r2.at[k], device_id=to, device_id_type=MESH)

        first = [scatter(k, me, p) for k, p in enumerate(peers)]
        for cp in first:
            cp.start()
        recv[me] = part(in_ref, me)[...]
        for k, p in enumerate(peers):
            scatter(k, dev(p), (x, y, c)).wait_recv()
        total = recv[0]
        for d in range(1, n_dev):
            total = total + recv[d]
        part(out_ref, me)[...] = total
        second = [gather(k, me, p) for k, p in enumerate(peers)]
        for cp in second:
            cp.start()
        for k, p in enumerate(peers):
            gather(k, dev(p), (x, y, c)).wait_recv()
        for cp in first + second:
            cp.wait_send()

    dma = pltpu.SemaphoreType.DMA
    vm = pl.BlockSpec(memory_space=pltpu.VMEM)
    return pl.pallas_call(
        body, name="small_allreduce", out_shape=jax.ShapeDtypeStruct((R, C), buf.dtype),
        in_specs=[vm], out_specs=vm,
        scratch_shapes=[pltpu.VMEM((n_dev, rp, C), buf.dtype), dma((n_dev - 1,)), dma((n_dev - 1,)), dma((n_dev - 1,)), dma((n_dev - 1,))],
    )(buf)


def _row_block(k):
    b = 256
    while k % b:
        b //= 2
    return b


def _add_halves(g, recv, place_arr, name):
    nj, hk, N = recv.shape
    bk = _row_block(hk)
    nb = hk // bk

    def body(k_ref, g_ref, r_ref, o_ref, ob_ref):
        s = g_ref[...] + r_ref[...].astype(F32)
        o_ref[...] = s
        ob_ref[...] = s.astype(BF16)

    blk = pl.BlockSpec((None, bk, N), lambda j, i, k_ref: (j, i, 0))
    grid_spec = pltpu.PrefetchScalarGridSpec(
        num_scalar_prefetch=1, grid=(nj, nb),
        in_specs=[pl.BlockSpec((None, bk, N), lambda j, i, k_ref: (j, k_ref[1] * nb + i, 0)), blk],
        out_specs=[blk, blk])
    return pl.pallas_call(
        body, name=name, grid_spec=grid_spec,
        out_shape=[jax.ShapeDtypeStruct((nj, hk, N), F32), jax.ShapeDtypeStruct((nj, hk, N), BF16)],
        compiler_params=_params("parallel", "parallel"),
    )(place_arr, g, recv)


def _add_chips(pc, recv, place_arr, layer, n_layers, prev, name):
    _, hk, N = pc.shape
    bk = _row_block(hk)
    nb = hk // bk

    def body(k_ref, p_ref, r0_ref, r1_ref, r2_ref, *rest):
        o_ref = rest[-1]
        o_ref[...] = ((p_ref[...] + r0_ref[...].astype(F32)) + r1_ref[...].astype(F32)) + r2_ref[...].astype(F32)

    rspec = lambda j: pl.BlockSpec((None, bk, N), lambda i, k_ref: (j, i, 0))
    in_specs = [pl.BlockSpec((None, bk, N), lambda i, k_ref: (k_ref[0], i, 0)), rspec(0), rspec(1), rspec(2)]
    operands = [pc, recv, recv, recv]
    aliases = {}
    if prev is not None:
        in_specs.append(ANY)
        operands.append(prev)
        aliases = {5: 0}
    grid_spec = pltpu.PrefetchScalarGridSpec(
        num_scalar_prefetch=1, grid=(nb,), in_specs=in_specs,
        out_specs=pl.BlockSpec((None, bk, N), lambda i, k_ref: (layer, k_ref[1] * nb + i, 0)))
    return pl.pallas_call(
        body, name=name, grid_spec=grid_spec, out_shape=jax.ShapeDtypeStruct((n_layers, 2 * hk, N), F32),
        input_output_aliases=aliases, compiler_params=_params("parallel"),
    )(place_arr, *operands)


def _adamw_math(w, g, m, v):
    m = ADAM_B1 * m + (1.0 - ADAM_B1) * g
    v = ADAM_B2 * v + (1.0 - ADAM_B2) * (g * g)
    m_hat = m / (1.0 - ADAM_B1 ** ADAM_STEP)
    v_hat = v / (1.0 - ADAM_B2 ** ADAM_STEP)
    delta = -ADAM_LR * (m_hat / (jnp.sqrt(v_hat) + ADAM_EPS) + ADAM_WD * w)
    return delta, m, v


def _adamw(w, g, m, v, name):
    L, K, N = w.shape
    bk = _row_block(K)

    def body(w_ref, g_ref, m_ref, v_ref, d_ref, nm_ref, nv_ref):
        d_ref[...], nm_ref[...], nv_ref[...] = _adamw_math(w_ref[...], g_ref[...], m_ref[...], v_ref[...])

    blk = pl.BlockSpec((None, bk, N), lambda l, i: (l, i, 0))
    sds = jax.ShapeDtypeStruct((L, K, N), F32)
    return pl.pallas_call(
        body, name=name, grid=(L, K // bk), in_specs=[blk] * 4, out_specs=[blk] * 3, out_shape=[sds] * 3,
        compiler_params=_params("parallel", "parallel"),
    )(w, g, m, v)


def _adamw_rows(w, g, m, v):
    def body(w_ref, g_ref, m_ref, v_ref, d_ref, nm_ref, nv_ref):
        d_ref[...], nm_ref[...], nv_ref[...] = _adamw_math(w_ref[...], g_ref[...], m_ref[...], v_ref[...])

    sds = jax.ShapeDtypeStruct(w.shape, F32)
    return pl.pallas_call(body, name="adamw_small", out_shape=[sds] * 3)(w, g, m, v)


def _cast_bf16(w, layer, place_arr, name):
    _, K, N = w.shape
    bk = _row_block(K)

    def body(k_ref, w_ref, o_ref):
        o_ref[...] = w_ref[...].astype(BF16)

    grid_spec = pltpu.PrefetchScalarGridSpec(
        num_scalar_prefetch=1, grid=(K // bk,),
        in_specs=[pl.BlockSpec((None, bk, N), lambda i, k_ref: (layer, i, 0))],
        out_specs=pl.BlockSpec((None, bk, N), lambda i, k_ref: (k_ref[0], i, 0)))
    return pl.pallas_call(
        body, name=name, grid_spec=grid_spec, out_shape=jax.ShapeDtypeStruct((N_CHIP, K, N), BF16),
        compiler_params=_params("parallel"),
    )(place_arr, w)


BIG = ("w_in", "w_out_a", "w_out_b", "w_o", "w_ffn_gate", "w_ffn_up", "w_ffn_down")
BIG_KEY = dict(w_in="win", w_out_a="woa", w_out_b="wob", w_o="wo", w_ffn_gate="wg", w_ffn_up="wu", w_ffn_down="wd")
SHARDED_SMALL = ("conv_a_w", "conv_b_w", "gate_bias")
REPLICATED = ("ln1_g", "conv_b_b", "lru_wa", "lru_ba", "lru_wx", "lru_bx", "lru_lambda", "ln2_g", "final_g")
WEIGHTS = ("ln1_g", "w_in", "conv_a_w", "conv_b_w", "conv_b_b", "lru_wa", "lru_ba", "lru_wx", "lru_bx", "lru_lambda",
           "w_out_a", "w_out_b", "gate_bias", "w_o", "ln2_g", "w_ffn_gate", "w_ffn_up", "w_ffn_down", "final_g")
LANES = 1024


def _pack_rows(arrays, row_multiple):
    flat = jnp.concatenate([a.reshape(-1) for a in arrays])
    rows = -(-flat.shape[0] // LANES)
    rows = -(-rows // row_multiple) * row_multiple
    flat = jnp.pad(flat, (0, rows * LANES - flat.shape[0]))
    return flat.reshape(rows, LANES)


def _unpack_rows(buf, shapes):
    flat = buf.reshape(-1)
    out, off = [], 0
    for s in shapes:
        n = 1
        for d in s:
            n *= d
        out.append(flat[off:off + n].reshape(s))
        off += n
    return out


OUT_KEYS = ("wo", "woa", "wob")
FFN_KEYS = ("wg", "wu", "wd")


def _items(keys, layer):
    return [(k, layer) for k in keys]


CARRY = {
    "rms_inproj_0": [("gather_ici", _items(OUT_KEYS + FFN_KEYS, 0))],
    "mixer_fwd_0": [("gather_d2d", _items(OUT_KEYS + FFN_KEYS, 0)), ("gather_ici", _items(("win",) + OUT_KEYS, 1))],
    "merge_fwd_0": [("gather_d2d", _items(("win",) + OUT_KEYS, 1))],
    "ffn_fwd_0": [("gather_ici", _items(FFN_KEYS, 1))],
    "rms_inproj_1": [("gather_d2d", _items(FFN_KEYS, 1))],
    "merge_bwd_1": [("halves", _items(FFN_KEYS, 1))],
    "mixer_bwd_1": [("chips", _items(FFN_KEYS, 1)), ("halves", _items(OUT_KEYS, 1))],
    "inproj_bwd_1": [("chips", _items(OUT_KEYS, 1)), ("share", _items(FFN_KEYS, 1))],
    "ffn_bwd_0": [("halves", [("win", 1)]), ("share", _items(OUT_KEYS, 1))],
    "wgrad_ffn_gate_up_0": [("chips", [("win", 1)])],
    "wgrad_ffn_down_0": [("share", [("win", 1)])],
    "merge_bwd_0": [("halves", _items(FFN_KEYS, 0))],
    "mixer_bwd_0": [("chips", _items(FFN_KEYS, 0)), ("halves", _items(OUT_KEYS, 0))],
    "wgrad_w_in_0": [("chips", _items(OUT_KEYS, 0)), ("share", _items(FFN_KEYS, 0))],
}
AFTER = {
    "merge_bwd_1": [("add_halves", _items(FFN_KEYS, 1))],
    "mixer_bwd_1": [("add_chips", _items(FFN_KEYS, 1)), ("add_halves", _items(OUT_KEYS, 1))],
    "inproj_bwd_1": [("add_chips", _items(OUT_KEYS, 1))],
    "ffn_bwd_0": [("add_halves", [("win", 1)])],
    "wgrad_ffn_gate_up_0": [("add_chips", [("win", 1)])],
    "merge_bwd_0": [("add_halves", _items(FFN_KEYS, 0))],
    "mixer_bwd_0": [("add_chips", _items(FFN_KEYS, 0)), ("add_halves", _items(OUT_KEYS, 0))],
    "wgrad_w_in_0": [("add_chips", _items(OUT_KEYS, 0))],
}


class _Schedule:
    def __init__(self, slots, place_arr, n_layers):
        self.W = slots
        self.place, self.L = place_arr, n_layers
        self.g32, self.g16 = {}, {}
        self.from_sibling, self.chip_sum, self.chip_sum16, self.from_chips = {}, {}, {}, {}
        self.reduced = {}

    def stage(self, comm, kind, items):
        bf = lambda shape: jax.ShapeDtypeStruct(shape, BF16)
        for it in items:
            if kind == "gather_ici":
                comm.add(_gather_ici, 3, io=[(self.W, it)])
            elif kind == "gather_d2d":
                comm.add(_gather_d2d, 3, io=[(self.W, it)])
            elif kind == "halves":
                nj, K, N = self.g16[it].shape
                comm.add(_reduce_halves, 1, ro=[self.g16[it]], nw=[(self.from_sibling, it, bf((nj, K // 2, N)))])
            elif kind == "chips":
                _, hk, N = self.chip_sum16[it].shape
                comm.add(_reduce_chips, 3, ro=[self.chip_sum16[it]], nw=[(self.from_chips, it, bf((3, hk, N)))])
            elif kind == "share":
                comm.add(_reduce_share(it[1]), 1, io=[(self.reduced, it[0])])
        return comm

    def carry(self, name):
        comm = _Carried()
        for kind, items in CARRY.get(name, ()):
            self.stage(comm, kind, items)
        return comm

    def run(self, name, rounds):
        _run_comm([self.stage(_Carried(), kind, items) for kind, items in rounds], name)

    def grad(self, key, layer, f32, b16):
        by_chip = lambda g: g.reshape(N_CHIP, -1, g.shape[-1])
        self.g32[key, layer], self.g16[key, layer] = by_chip(f32), by_chip(b16)

    def add(self, kind, items):
        for key, layer in items:
            it = (key, layer)
            if kind == "add_halves":
                self.chip_sum[it], self.chip_sum16[it] = _add_halves(self.g32[it], self.from_sibling[it], self.place,
                                                                    f"add_halves_{key}_{layer}")
            else:
                self.reduced[key] = _add_chips(self.chip_sum[it], self.from_chips[it], self.place, layer, self.L,
                                               self.reduced.get(key), f"add_chips_{key}_{layer}")

    def after(self, name):
        for kind, items in AFTER.get(name, ()):
            self.add(kind, items)


def _step(w, m, v, x, target):
    xi, yi, ci = lax.axis_index("x"), lax.axis_index("y"), lax.axis_index("c")
    chip = _chip_id(xi, yi)
    place_arr = jnp.stack([chip, ci]).astype(jnp.int32)
    L = w["ln1_g"].shape[0]
    assert L == 2
    D = x.shape[1]
    dc = D // N_CHIP

    stored = lambda n, a: jnp.swapaxes(a, 1, 2) if n in ("w_ffn_gate", "w_ffn_up") else a
    slots = {(BIG_KEY[n], l): _cast_bf16(stored(n, w[n]), l, place_arr, f"cast_{n}_{l}") for n in BIG for l in range(L)}
    sched = _Schedule(slots, place_arr, L)
    small_shard = jnp.concatenate([w[n] for n in SHARDED_SMALL], axis=1)
    got = {}
    first = sched.stage(_Carried(), "gather_ici", [("win", 0)])
    first.add(_gather_small, 3, ro=[small_shard], nw=[(got, "small", jax.ShapeDtypeStruct((3,) + small_shard.shape, F32))])
    _run_comm([first, sched.stage(_Carried(), "gather_d2d", [("win", 0)])], "gather_first")
    small_g = jnp.zeros((N_CHIP,) + small_shard.shape, F32)
    small_g = lax.dynamic_update_index_in_dim(small_g, small_shard, chip, 0)
    for j, (cx, cy) in enumerate([(1 - xi, yi), (xi, 1 - yi), (1 - xi, 1 - yi)]):
        small_g = lax.dynamic_update_index_in_dim(small_g, got["small"][j], _chip_id(cx, cy), 0)
    small_full = jnp.transpose(small_g, (1, 2, 0, 3)).reshape(L, small_shard.shape[1], D)
    small = {n: w[n] for n in REPLICATED}
    off = 0
    for n in SHARDED_SMALL:
        k = w[n].shape[1]
        small[n] = small_full[:, off:off + k]
        off += k

    loss_row, grad_x, gsmall = _local_step(x, target, sched.W, small, _tiles(x.shape[0]), sched)

    sched.run("reduce_tail_halves", [("halves", [("win", 0)]), ("share", _items(OUT_KEYS, 0))])
    sched.add("add_halves", [("win", 0)])
    sched.run("reduce_tail_chips", [("chips", [("win", 0)])])
    sched.add("add_chips", [("win", 0)])
    sched.run("reduce_tail_share", [("share", [("win", 0)])])
    grads = {n: stored(n, sched.reduced[BIG_KEY[n]]) for n in BIG}

    order = [n for n in WEIGHTS if n not in BIG]
    packed = _pack_rows([gsmall[n] for n in order] + [loss_row], 8 * SUBLANES)
    summed = _small_allreduce(packed)
    parts = _unpack_rows(summed, [gsmall[n].shape for n in order] + [loss_row.shape])
    loss = jnp.sum(parts[-1])
    for n, g in zip(order, parts[:-1]):
        grads[n] = lax.dynamic_slice_in_dim(g, chip * dc, dc, axis=2) if n in SHARDED_SMALL else g

    delta, new_m, new_v = {}, {}, {}
    for n in BIG:
        d, nm, nv = _adamw(stored(n, w[n]), sched.reduced[BIG_KEY[n]], stored(n, m[n]), stored(n, v[n]), f"adamw_{n}")
        delta[n], new_m[n], new_v[n] = stored(n, d), stored(n, nm), stored(n, nv)
    pw, pg, pm, pv = (_pack_rows([d[n] for n in order], SUBLANES) for d in (w, grads, m, v))
    outs = _adamw_rows(pw, pg, pm, pv)
    shapes = [w[n].shape for n in order]
    for d, buf in zip((delta, new_m, new_v), outs):
        d.update(zip(order, _unpack_rows(buf, shapes)))
    return loss, grad_x, grads, delta, new_m, new_v


def kernel(x, ln1_g, w_in, conv_a_w, conv_b_w, conv_b_b, lru_wa, lru_ba, lru_wx, lru_bx, lru_lambda, w_out_a, w_out_b, gate_bias, w_o, ln2_g, w_ffn_gate, w_ffn_up, w_ffn_down, final_g, loss_target, m_ln1_g, m_w_in, m_conv_a_w, m_conv_b_w, m_conv_b_b, m_lru_wa, m_lru_ba, m_lru_wx, m_lru_bx, m_lru_lambda, m_w_out_a, m_w_out_b, m_gate_bias, m_w_o, m_ln2_g, m_w_ffn_gate, m_w_ffn_up, m_w_ffn_down, m_final_g, v_ln1_g, v_w_in, v_conv_a_w, v_conv_b_w, v_conv_b_b, v_lru_wa, v_lru_ba, v_lru_wx, v_lru_bx, v_lru_lambda, v_w_out_a, v_w_out_b, v_gate_bias, v_w_o, v_ln2_g, v_w_ffn_gate, v_w_ffn_up, v_w_ffn_down, v_final_g):
    w = dict(ln1_g=ln1_g, w_in=w_in, conv_a_w=conv_a_w, conv_b_w=conv_b_w, conv_b_b=conv_b_b, lru_wa=lru_wa, lru_ba=lru_ba,
             lru_wx=lru_wx, lru_bx=lru_bx, lru_lambda=lru_lambda, w_out_a=w_out_a, w_out_b=w_out_b, gate_bias=gate_bias, w_o=w_o,
             ln2_g=ln2_g, w_ffn_gate=w_ffn_gate, w_ffn_up=w_ffn_up, w_ffn_down=w_ffn_down, final_g=final_g)
    m = dict(ln1_g=m_ln1_g, w_in=m_w_in, conv_a_w=m_conv_a_w, conv_b_w=m_conv_b_w, conv_b_b=m_conv_b_b, lru_wa=m_lru_wa,
             lru_ba=m_lru_ba, lru_wx=m_lru_wx, lru_bx=m_lru_bx, lru_lambda=m_lru_lambda, w_out_a=m_w_out_a, w_out_b=m_w_out_b,
             gate_bias=m_gate_bias, w_o=m_w_o, ln2_g=m_ln2_g, w_ffn_gate=m_w_ffn_gate, w_ffn_up=m_w_ffn_up,
             w_ffn_down=m_w_ffn_down, final_g=m_final_g)
    v = dict(ln1_g=v_ln1_g, w_in=v_w_in, conv_a_w=v_conv_a_w, conv_b_w=v_conv_b_w, conv_b_b=v_conv_b_b, lru_wa=v_lru_wa,
             lru_ba=v_lru_ba, lru_wx=v_lru_wx, lru_bx=v_lru_bx, lru_lambda=v_lru_lambda, w_out_a=v_w_out_a, w_out_b=v_w_out_b,
             gate_bias=v_gate_bias, w_o=v_w_o, ln2_g=v_ln2_g, w_ffn_gate=v_w_ffn_gate, w_ffn_up=v_w_ffn_up,
             w_ffn_down=v_w_ffn_down, final_g=v_final_g)
    loss, grad_x, grads, delta, new_m, new_v = _step(w, m, v, x[0], loss_target[0])
    return (loss, grad_x[None], *[grads[n] for n in WEIGHTS], *[delta[n] for n in WEIGHTS],
            *[new_m[n] for n in WEIGHTS], *[new_v[n] for n in WEIGHTS])
```

```python
import functools

import jax
import jax.numpy as jnp
from jax import lax
from jax.experimental import pallas as pl
from jax.experimental.pallas import tpu as pltpu

F32 = jnp.float32
BF16 = jnp.bfloat16
MESH = pl.DeviceIdType.MESH

N_CHIP = 4
RMS_EPS = 1e-6
LRU_C = 8.0
LRU_HEAD_DIM = 64
LRU_BLOCK = 256
CONV_A_K = 3
CONV_B_K = 4
ADAM_LR = 0.001
ADAM_B1 = 0.9
ADAM_B2 = 0.999
ADAM_EPS = 1e-08
ADAM_WD = 0.01
ADAM_STEP = 10
SUBLANES = 8
VMEM_LIMIT = 56 * 1024 * 1024


def _params(*sem):
    return pltpu.CompilerParams(dimension_semantics=sem, vmem_limit_bytes=VMEM_LIMIT)


def _sigmoid(v):
    return 1.0 / (1.0 + jnp.exp(-v))


def _one_minus_sq(la, a):
    return jnp.tanh(-la) * (1.0 + a * a)


def _gelu_parts(v):
    k = 0.7978845608028654
    v2 = v * v
    t = jnp.tanh(k * (v + 0.044715 * v * v2))
    gelu = 0.5 * v * (1.0 + t)
    dgelu = 0.5 * (1.0 + t) + 0.5 * v * (1.0 - t * t) * k * (1.0 + 3 * 0.044715 * v2)
    return gelu, dgelu


def _shift_down(v, k, prev8):
    rolled = pltpu.roll(v, k, 0)
    r8 = lax.broadcasted_iota(jnp.int32, prev8.shape, 0)
    head = jnp.where(r8 < k, pltpu.roll(prev8, k, 0), rolled[0:SUBLANES])
    return jnp.concatenate([head, rolled[SUBLANES:]], axis=0)


def _shift_up(v, k, next8):
    tm = v.shape[0]
    rolled = pltpu.roll(v, tm - k, 0)
    r8 = lax.broadcasted_iota(jnp.int32, next8.shape, 0)
    tail = jnp.where(r8 >= SUBLANES - k, pltpu.roll(next8, SUBLANES - k, 0), rolled[tm - SUBLANES:])
    return jnp.concatenate([rolled[:tm - SUBLANES], tail], axis=0)


def _colsum8(v):
    tm, c = v.shape
    return jnp.sum(v.reshape(tm // SUBLANES, SUBLANES, c), axis=0)


def _rms_stats(xv):
    var = jnp.mean(xv * xv, axis=-1, keepdims=True)
    return lax.rsqrt(var + RMS_EPS)


def _rms_bwd(dh, xv, g):
    rstd = _rms_stats(xv)
    xhat = xv * rstd
    dxhat = dh * g
    dx = rstd * (dxhat - xhat * jnp.mean(dxhat * xhat, axis=-1, keepdims=True))
    return dx, dh * xhat


ANY = pl.BlockSpec(memory_space=pl.ANY)


def _place():
    x, y, c = lax.axis_index("x"), lax.axis_index("y"), lax.axis_index("c")
    other_chips = [(1 - x, y), (x, 1 - y), (1 - x, 1 - y)]
    return x, y, c, other_chips


def _chip_id(x, y):
    return 2 * x + y


def _half(c, hk):
    return pl.ds(pl.multiple_of(c * hk, 16), hk)


def _remote(src, dst, to, sems):
    return pltpu.make_async_remote_copy(src_ref=src, dst_ref=dst, device_id=to, device_id_type=MESH, **sems)


class _Carried:
    def __init__(self):
        self.ro, self.io, self.nw, self.parts, self.n = [], [], [], [], 0

    def add(self, maker, n, ro=(), io=(), nw=()):
        def index(items, item, same):
            for k, other in enumerate(items):
                if same(other, item):
                    return k
            items.append(item)
            return len(items) - 1

        r = [index(self.ro, a, lambda p, q: p is q) for a in ro]
        i = [index(self.io, a, lambda p, q: p[0] is q[0] and p[1] == q[1]) for a in io]
        w = [index(self.nw, a, lambda p, q: False) for a in nw]
        self.parts.append((maker, r, i, w, self.n))
        self.n += n
        return self

    def pairs(self, ro, io, nw, ssem, rsem):
        out = []
        for maker, r, i, w, base in self.parts:
            sems = lambda k, base=base: dict(send_sem=ssem.at[base + k], recv_sem=rsem.at[base + k])
            out += maker([ro[k] for k in r], [io[k] for k in i], [nw[k] for k in w], sems)
        return out

    def start(self, *refs):
        for send, _ in self.pairs(*refs):
            send.start()

    def finish(self, *refs):
        pairs = self.pairs(*refs)
        for _, recv in pairs:
            recv.wait_recv()
        for send, _ in pairs:
            send.wait_send()

    def operands(self):
        return list(self.ro) + [store[key] for store, key in self.io]

    def out_shapes(self):
        return [jax.ShapeDtypeStruct(store[key].shape, store[key].dtype) for store, key in self.io] + [s for _, _, s in self.nw]

    def keep(self, results):
        for (store, key), arr in zip(self.io, results[:len(self.io)]):
            store[key] = arr
        for (store, key, _), arr in zip(self.nw, results[len(self.io):]):
            store[key] = arr


def _call(body, comm, *, name, grid, in_specs, out_specs, out_shape, compiler_params, scratch_shapes=(), aliases=None):
    aliases = dict(aliases or {})
    if comm is None or not comm.parts:
        return pl.pallas_call(body, name=name, grid=grid, in_specs=in_specs, out_specs=out_specs, out_shape=out_shape,
                              scratch_shapes=list(scratch_shapes), input_output_aliases=aliases, compiler_params=compiler_params)
    n_in, n_out, n_scr = len(in_specs), len(out_shape), len(scratch_shapes)
    n_ro, n_io, n_nw = len(comm.ro), len(comm.io), len(comm.nw)

    def carried(*refs):
        base_in = refs[:n_in]
        ro = refs[n_in:n_in + n_ro]
        pos = n_in + n_ro + n_io
        base_out = refs[pos:pos + n_out]
        io = refs[pos + n_out:pos + n_out + n_io]
        nw = refs[pos + n_out + n_io:pos + n_out + n_io + n_nw]
        pos += n_out + n_io + n_nw
        scr = refs[pos:pos + n_scr]
        ssem, rsem = refs[pos + n_scr], refs[pos + n_scr + 1]
        first = pl.program_id(0) == 0
        last = pl.program_id(0) == grid[0] - 1
        for axis in range(1, len(grid)):
            first = first & (pl.program_id(axis) == 0)
            last = last & (pl.program_id(axis) == grid[axis] - 1)

        @pl.when(first)
        def _():
            comm.start(ro, io, nw, ssem, rsem)

        body(*base_in, *base_out, *scr)

        @pl.when(last)
        def _():
            comm.finish(ro, io, nw, ssem, rsem)

    aliases.update({n_in + n_ro + k: n_out + k for k in range(n_io)})
    dma = pltpu.SemaphoreType.DMA
    call = pl.pallas_call(
        carried, name=name, grid=grid,
        in_specs=list(in_specs) + [ANY] * (n_ro + n_io), out_specs=list(out_specs) + [ANY] * (n_io + n_nw),
        out_shape=list(out_shape) + comm.out_shapes(), input_output_aliases=aliases,
        scratch_shapes=list(scratch_shapes) + [dma((comm.n,)), dma((comm.n,))], compiler_params=compiler_params)

    def run(*operands):
        res = call(*operands, *comm.operands())
        comm.keep(res[n_out:])
        return res[:n_out]

    return run


def _run_comm(rounds, name):
    ro, io, nw, uses = [], [], [], []
    for r in rounds:
        def index(items, item, same):
            for k, other in enumerate(items):
                if same(other, item):
                    return k
            items.append(item)
            return len(items) - 1
        uses.append(([index(ro, a, lambda p, q: p is q) for a in r.ro],
                     [index(io, a, lambda p, q: p[0] is q[0] and p[1] == q[1]) for a in r.io],
                     [index(nw, a, lambda p, q: False) for a in r.nw]))
    n_ro, n_io, n_nw = len(ro), len(io), len(nw)

    def body(*refs):
        ro_refs = refs[:n_ro]
        io_refs = refs[n_ro + n_io:n_ro + 2 * n_io]
        nw_refs = refs[n_ro + 2 * n_io:n_ro + 2 * n_io + n_nw]
        sems = refs[n_ro + 2 * n_io + n_nw:]
        for k, (r, (a, b, c)) in enumerate(zip(rounds, uses)):
            args = ([ro_refs[i] for i in a], [io_refs[i] for i in b], [nw_refs[i] for i in c], sems[2 * k], sems[2 * k + 1])
            r.start(*args)
            r.finish(*args)

    operands = ro + [store[key] for store, key in io]
    out_shape = [jax.ShapeDtypeStruct(store[key].shape, store[key].dtype) for store, key in io] + [s for _, _, s in nw]
    dma = pltpu.SemaphoreType.DMA
    res = pl.pallas_call(
        body, name=name, out_shape=out_shape,
        in_specs=[ANY] * (n_ro + n_io), out_specs=[ANY] * (n_io + n_nw),
        input_output_aliases={n_ro + k: k for k in range(n_io)},
        scratch_shapes=[dma((r.n,)) for r in rounds for _ in range(2)],
    )(*operands)
    for (store, key), arr in zip(io, res[:n_io]):
        store[key] = arr
    for (store, key, _), arr in zip(nw, res[n_io:]):
        store[key] = arr


def _gather_ici(ro, io, nw, sems):
    s = io[0]
    x, y, c, chips = _place()
    hk = s.shape[1] // 2
    mine = s.at[_chip_id(x, y), _half(c, hk)]
    pairs = []
    for j, (cx, cy) in enumerate(chips):
        theirs = s.at[_chip_id(cx, cy), _half(c, hk)]
        pairs.append((_remote(mine, mine, (cx, cy, c), sems(j)), _remote(theirs, theirs, (cx, cy, c), sems(j))))
    return pairs


def _gather_d2d(ro, io, nw, sems):
    s = io[0]
    x, y, c, chips = _place()
    hk = s.shape[1] // 2
    sib = (x, y, 1 - c)
    pairs = []
    for j, (cx, cy) in enumerate(chips):
        here = s.at[_chip_id(cx, cy), _half(c, hk)]
        there = s.at[_chip_id(cx, cy), _half(1 - c, hk)]
        pairs.append((_remote(here, here, sib, sems(j)), _remote(there, there, sib, sems(j))))
    return pairs


def _gather_small(ro, io, nw, sems):
    x, y, c, chips = _place()
    return [(_remote(ro[0], nw[0].at[j], (cx, cy, c), sems(j)),) * 2 for j, (cx, cy) in enumerate(chips)]


def _reduce_halves(ro, io, nw, sems):
    x, y, c, _ = _place()
    g = ro[0]
    hk = g.shape[1] // 2
    sib = (x, y, 1 - c)
    return [(_remote(g.at[:, _half(1 - c, hk)], nw[0], sib, sems(0)), _remote(g.at[:, _half(c, hk)], nw[0], sib, sems(0)))]


def _reduce_chips(ro, io, nw, sems):
    x, y, c, chips = _place()
    return [(_remote(ro[0].at[_chip_id(cx, cy)], nw[0].at[j], (cx, cy, c), sems(j)),) * 2 for j, (cx, cy) in enumerate(chips)]


def _reduce_share(layer):
    def maker(ro, io, nw, sems):
        g = io[0]
        x, y, c, _ = _place()
        hk = g.shape[1] // 2
        sib = (x, y, 1 - c)
        mine, theirs = g.at[layer, _half(c, hk)], g.at[layer, _half(1 - c, hk)]
        return [(_remote(mine, mine, sib, sems(0)), _remote(theirs, theirs, sib, sems(0)))]
    return maker


def _rms_inproj(x, g_row, win, layer, tm, comm=None):
    T, D = x.shape
    ns = win.shape[-1]

    def body(x_ref, g_ref, w_ref, p_ref, h_ref):
        @pl.when(pl.program_id(1) == 0)
        def _():
            xv = x_ref[...]
            h_ref[...] = (xv * _rms_stats(xv) * g_ref[...]).astype(BF16)
        p_ref[...] = jnp.dot(h_ref[...], w_ref[...], preferred_element_type=F32).astype(BF16)

    return _call(
        body, comm, name=f"rms_inproj_{layer}", grid=(T // tm, N_CHIP),
        in_specs=[pl.BlockSpec((tm, D), lambda i, j: (i, 0)),
                  pl.BlockSpec((1, D), lambda i, j: (0, 0)),
                  pl.BlockSpec((None, D, ns), lambda i, j: (j, 0, 0))],
        out_specs=[pl.BlockSpec((tm, ns), lambda i, j: (i, j)),
                   pl.BlockSpec((tm, D), lambda i, j: (i, 0))],
        out_shape=[jax.ShapeDtypeStruct((T, N_CHIP * ns), BF16), jax.ShapeDtypeStruct((T, D), BF16)],
        compiler_params=_params("parallel", "arbitrary"),
    )(x, g_row, win)


def _mixer_recompute(ca, xa, xb, zprev8, xbprev8, cw_ref, wab_ref, sp):
    row = lambda k: cw_ref[pl.ds(k, 1), :]
    z = ca * xa
    z1 = _shift_down(z, 1, zprev8)
    z2 = _shift_down(z, 2, zprev8)
    cz = row(2) * z + row(1) * z1 + row(0) * z2
    x1 = _shift_down(xb, 1, xbprev8)
    x2 = _shift_down(xb, 2, xbprev8)
    x3 = _shift_down(xb, 3, xbprev8)
    u = row(6) * xb + row(5) * x1 + row(4) * x2 + row(3) * x3 + row(7)
    ub = u.astype(BF16)
    nb = wab_ref.shape[0]
    ras, ixs = [], []
    for b in range(nb):
        ri = jnp.dot(ub[:, b * LRU_BLOCK:(b + 1) * LRU_BLOCK], wab_ref[b], preferred_element_type=F32)
        ras.append(ri[:, :LRU_BLOCK])
        ixs.append(ri[:, LRU_BLOCK:])
    r = _sigmoid(jnp.concatenate(ras, axis=1) + row(8))
    gi = _sigmoid(jnp.concatenate(ixs, axis=1) + row(9))
    la = (-LRU_C) * r * sp
    a = jnp.exp(la)
    m = jnp.sqrt(_one_minus_sq(la, a))
    return dict(z=z, z1=z1, z2=z2, cz=cz, x1=x1, x2=x2, x3=x3, u=u, ub=ub, r=r, gi=gi, a=a, m=m)


def _softplus_neg(lam):
    v = -lam
    return jnp.maximum(v, 0.0) + jnp.log1p(jnp.exp(-jnp.abs(v)))


def _mixer_fwd(p, cw, wab, lam_row, layer, tm, comm=None):
    T = p.shape[0]
    D = p.shape[1] // 7
    ngroups = tm // SUBLANES

    def body(ba_ref, ca_ref, xa_ref, xb_ref, gb_ref, cw_ref, wab_ref, lam_ref, ya_ref, yb_ref, h_ref,
             zprev, xbprev, hcarry, a_s, h_s):
        @pl.when(pl.program_id(0) == 0)
        def _():
            zprev[...] = jnp.zeros_like(zprev)
            xbprev[...] = jnp.zeros_like(xbprev)
            hcarry[...] = jnp.zeros_like(hcarry)

        ca = ca_ref[...].astype(F32)
        xa = xa_ref[...].astype(F32)
        xb = xb_ref[...].astype(F32)
        sp = _softplus_neg(lam_ref[...])
        c = _mixer_recompute(ca, xa, xb, zprev[...], xbprev[...], cw_ref, wab_ref, sp)
        zprev[...] = c["z"][tm - SUBLANES:]
        xbprev[...] = xb[tm - SUBLANES:]
        ya_ref[...] = (ba_ref[...].astype(F32) * c["cz"]).astype(BF16)

        A = c["a"]
        B = c["m"] * c["gi"] * c["u"]
        q = lax.broadcasted_iota(jnp.int32, A.shape, 0) & (SUBLANES - 1)
        for s in (1, 2, 4):
            msk = q >= s
            B = jnp.where(msk, A * pltpu.roll(B, s, 0) + B, B)
            A = jnp.where(msk, A * pltpu.roll(A, s, 0), A)
        a_s[...] = A
        h_s[...] = B

        def step(g, carry):
            off = pl.multiple_of(g * SUBLANES, SUBLANES)
            hg = h_s[pl.ds(off, SUBLANES), :] + a_s[pl.ds(off, SUBLANES), :] * carry
            h_s[pl.ds(off, SUBLANES), :] = hg
            return jnp.broadcast_to(hg[SUBLANES - 1:SUBLANES, :], hg.shape)

        hcarry[...] = lax.fori_loop(0, ngroups, step, hcarry[...], unroll=4)
        h = h_s[...]
        h_ref[...] = h
        gelu, _ = _gelu_parts(gb_ref[...].astype(F32))
        yb_ref[...] = (h * gelu).astype(BF16)

    col = lambda k: pl.BlockSpec((tm, D), lambda i: (i, k))
    full = lambda a: pl.BlockSpec(a.shape, lambda i: (0,) * a.ndim)
    tok = pl.BlockSpec((tm, D), lambda i: (i, 0))
    return _call(
        body, comm, name=f"mixer_fwd_{layer}", grid=(T // tm,),
        in_specs=[col(0), col(1), col(2), col(3), col(4), full(cw), full(wab), full(lam_row)],
        out_specs=[tok, tok, tok],
        out_shape=[jax.ShapeDtypeStruct((T, D), BF16), jax.ShapeDtypeStruct((T, D), BF16), jax.ShapeDtypeStruct((T, D), F32)],
        scratch_shapes=[pltpu.VMEM((SUBLANES, D), F32), pltpu.VMEM((SUBLANES, D), F32), pltpu.VMEM((SUBLANES, D), F32),
                        pltpu.VMEM((tm, D), F32), pltpu.VMEM((tm, D), F32)],
        compiler_params=_params("arbitrary"),
    )(p, p, p, p, p, cw, wab, lam_row)


def _merge_fwd(x, p, ya, yb, woa, wob, wo, gbias, layer, tm, comm=None):
    T, D = x.shape

    def body(x_ref, ga_ref, gb_ref, ya_ref, yb_ref, woa_ref, wob_ref, wo_ref, bias_ref, oa_ref, ob_ref, mg_ref, x1_ref):
        oa = jnp.dot(ya_ref[...], woa_ref[...], preferred_element_type=F32)
        ob = jnp.dot(yb_ref[...], wob_ref[...], preferred_element_type=F32)
        sa = _sigmoid(ga_ref[...].astype(F32) + bias_ref[pl.ds(0, 1), :])
        sb = _sigmoid(gb_ref[...].astype(F32) + bias_ref[pl.ds(1, 1), :])
        mg = (sa * oa + sb * ob).astype(BF16)
        oa_ref[...] = oa.astype(BF16)
        ob_ref[...] = ob.astype(BF16)
        mg_ref[...] = mg
        x1_ref[...] = x_ref[...] + jnp.dot(mg, wo_ref[...], preferred_element_type=F32)

    tok = pl.BlockSpec((tm, D), lambda i: (i, 0))
    wsp = pl.BlockSpec((D, D), lambda i: (0, 0))
    bf = jax.ShapeDtypeStruct((T, D), BF16)
    return _call(
        body, comm, name=f"merge_fwd_{layer}", grid=(T // tm,),
        in_specs=[tok, pl.BlockSpec((tm, D), lambda i: (i, 5)), pl.BlockSpec((tm, D), lambda i: (i, 6)), tok, tok,
                  wsp, wsp, wsp, pl.BlockSpec(gbias.shape, lambda i: (0, 0))],
        out_specs=[tok, tok, tok, tok],
        out_shape=[bf, bf, bf, jax.ShapeDtypeStruct((T, D), F32)],
        compiler_params=_params("parallel"),
    )(x, p, p, ya, yb, woa, wob, wo, gbias)


def _ffn_fwd(x1, g_row, wg, wu, wd, layer, tm, comm=None):
    T, D = x1.shape
    fs = wg.shape[-2]
    nt = (((1,), (1,)), ((), ()))

    def body(x_ref, g_ref, wg_ref, wu_ref, wd_ref, h_ref, gg_ref, uu_ref, x2_ref, acc):
        j = pl.program_id(1)

        @pl.when(j == 0)
        def _():
            xv = x_ref[...]
            h_ref[...] = (xv * _rms_stats(xv) * g_ref[...]).astype(BF16)
            acc[...] = xv

        h = h_ref[...]
        gg = lax.dot_general(h, wg_ref[...], nt, preferred_element_type=F32)
        uu = lax.dot_general(h, wu_ref[...], nt, preferred_element_type=F32)
        gg_ref[...] = gg.astype(BF16)
        uu_ref[...] = uu.astype(BF16)
        act = (gg * _sigmoid(gg) * uu).astype(BF16)
        acc[...] += jnp.dot(act, wd_ref[...], preferred_element_type=F32)

        @pl.when(j == N_CHIP - 1)
        def _():
            x2_ref[...] = acc[...]

    tok = pl.BlockSpec((tm, D), lambda i, j: (i, 0))
    cm = pl.BlockSpec((None, tm, fs), lambda i, j: (j, i, 0))
    return _call(
        body, comm, name=f"ffn_fwd_{layer}", grid=(T // tm, N_CHIP),
        in_specs=[tok, pl.BlockSpec((1, D), lambda i, j: (0, 0)),
                  pl.BlockSpec((None, fs, D), lambda i, j: (j, 0, 0)),
                  pl.BlockSpec((None, fs, D), lambda i, j: (j, 0, 0)),
                  pl.BlockSpec((None, fs, D), lambda i, j: (j, 0, 0))],
        out_specs=[tok, cm, cm, tok],
        out_shape=[jax.ShapeDtypeStruct((T, D), BF16), jax.ShapeDtypeStruct((N_CHIP, T, fs), BF16),
                   jax.ShapeDtypeStruct((N_CHIP, T, fs), BF16), jax.ShapeDtypeStruct((T, D), F32)],
        scratch_shapes=[pltpu.VMEM((tm, D), F32)],
        compiler_params=_params("parallel", "arbitrary"),
    )(x1, g_row, wg, wu, wd)


def _final_loss(x, g_row, target, tm):
    T, D = x.shape
    n = T // tm

    def body(x_ref, g_ref, t_ref, dx_ref, red_ref, acc):
        i = pl.program_id(0)

        @pl.when(i == 0)
        def _():
            acc[...] = jnp.zeros_like(acc)

        xv = x_ref[...]
        g = g_ref[...]
        rstd = _rms_stats(xv)
        xhat = xv * rstd
        err = xhat * g - t_ref[...]
        dy = err * (1.0 / D)
        dxhat = dy * g
        dx_ref[...] = rstd * (dxhat - xhat * jnp.mean(dxhat * xhat, axis=-1, keepdims=True))
        acc[0] += _colsum8(err * err)
        acc[1] += _colsum8(dy * xhat)

        @pl.when(i == n - 1)
        def _():
            red_ref[pl.ds(0, 1), :] = jnp.sum(acc[0], axis=0, keepdims=True) * (0.5 / D)
            red_ref[pl.ds(1, 1), :] = jnp.sum(acc[1], axis=0, keepdims=True)

    tok = pl.BlockSpec((tm, D), lambda i: (i, 0))
    return pl.pallas_call(
        body, name="final_loss", grid=(n,),
        in_specs=[tok, pl.BlockSpec((1, D), lambda i: (0, 0)), tok],
        out_specs=[tok, pl.BlockSpec((2, D), lambda i: (0, 0))],
        out_shape=[jax.ShapeDtypeStruct((T, D), F32), jax.ShapeDtypeStruct((2, D), F32)],
        scratch_shapes=[pltpu.VMEM((2, SUBLANES, D), F32)],
        compiler_params=_params("arbitrary"),
    )(x, g_row, target)


def _ffn_bwd(dx2, x1, g_row, gg, uu, wg, wu, wd, layer, tm, comm=None):
    T, D = dx2.shape
    fs = wg.shape[-2]
    n = T // tm
    nt = (((1,), (1,)), ((), ()))

    def body(dx_ref, x_ref, g_ref, gg_ref, uu_ref, wg_ref, wu_ref, wd_ref, dg_ref, du_ref, act_ref, dx1_ref, dxb_ref, red_ref,
             acc, racc):
        i = pl.program_id(0)
        j = pl.program_id(1)

        @pl.when((i == 0) & (j == 0))
        def _():
            racc[...] = jnp.zeros_like(racc)

        @pl.when(j == 0)
        def _():
            dxb_ref[...] = dx_ref[...].astype(BF16)
            acc[...] = jnp.zeros_like(acc)

        dact = lax.dot_general(dxb_ref[...], wd_ref[j], nt, preferred_element_type=F32)
        g = gg_ref[...].astype(F32)
        u = uu_ref[...].astype(F32)
        s = _sigmoid(g)
        silu = g * s
        dg = (dact * u * (s * (1.0 + g * (1.0 - s)))).astype(BF16)
        du = (dact * silu).astype(BF16)
        dg_ref[...] = dg
        du_ref[...] = du
        act_ref[...] = (silu * u).astype(BF16)
        acc[...] += (jnp.dot(dg, wg_ref[j], preferred_element_type=F32)
                     + jnp.dot(du, wu_ref[j], preferred_element_type=F32))

        @pl.when(j == N_CHIP - 1)
        def _():
            dx, dgain = _rms_bwd(acc[...], x_ref[...], g_ref[...])
            dx1_ref[...] = dx_ref[...] + dx
            racc[...] += _colsum8(dgain)

        @pl.when((i == n - 1) & (j == N_CHIP - 1))
        def _():
            red_ref[...] = jnp.sum(racc[...], axis=0, keepdims=True)

    tok = pl.BlockSpec((tm, D), lambda i, j: (i, 0))
    cm = pl.BlockSpec((None, tm, fs), lambda i, j: (j, i, 0))
    cms = jax.ShapeDtypeStruct((N_CHIP, T, fs), BF16)
    resident = pl.BlockSpec((N_CHIP, fs, D), lambda i, j: (0, 0, 0), pipeline_mode=pl.Buffered(1))
    return _call(
        body, comm, name=f"ffn_bwd_{layer}", grid=(n, N_CHIP),
        in_specs=[tok, tok, pl.BlockSpec((1, D), lambda i, j: (0, 0)), cm, cm, resident, resident, resident],
        out_specs=[cm, cm, cm, tok, tok, pl.BlockSpec((1, D), lambda i, j: (0, 0))],
        out_shape=[cms, cms, cms, jax.ShapeDtypeStruct((T, D), F32), jax.ShapeDtypeStruct((T, D), BF16),
                   jax.ShapeDtypeStruct((1, D), F32)],
        scratch_shapes=[pltpu.VMEM((tm, D), F32), pltpu.VMEM((SUBLANES, D), F32)],
        compiler_params=_params("arbitrary", "arbitrary"),
    )(dx2, x1, g_row, gg, uu, wg, wu, wd)


def _merge_bwd(dx1, p, oa, ob, woa, wob, wo, gbias, layer, tm, comm=None):
    T, D = dx1.shape
    n = T // tm
    nt = (((1,), (1,)), ((), ()))

    def body(dx_ref, ga_ref, gb_ref, oa_ref, ob_ref, woa_ref, wob_ref, wo_ref, bias_ref,
             dya_ref, dyb_ref, doa_ref, dob_ref, dgl_ref, dxb_ref, red_ref, racc):
        i = pl.program_id(0)

        @pl.when(i == 0)
        def _():
            racc[...] = jnp.zeros_like(racc)

        dxb = dx_ref[...].astype(BF16)
        dxb_ref[...] = dxb
        dm = lax.dot_general(dxb, wo_ref[...], nt, preferred_element_type=F32)
        sa = _sigmoid(ga_ref[...].astype(F32) + bias_ref[pl.ds(0, 1), :])
        sb = _sigmoid(gb_ref[...].astype(F32) + bias_ref[pl.ds(1, 1), :])
        doa = (dm * sa).astype(BF16)
        dob = (dm * sb).astype(BF16)
        dga = dm * oa_ref[...].astype(F32) * (sa * (1.0 - sa))
        dgb = dm * ob_ref[...].astype(F32) * (sb * (1.0 - sb))
        doa_ref[...] = doa
        dob_ref[...] = dob
        dgl_ref[:, 0:D] = dga.astype(BF16)
        dgl_ref[:, D:2 * D] = dgb.astype(BF16)
        racc[0] += _colsum8(dga)
        racc[1] += _colsum8(dgb)
        dya_ref[...] = lax.dot_general(doa, woa_ref[...], nt, preferred_element_type=F32).astype(BF16)
        dyb_ref[...] = lax.dot_general(dob, wob_ref[...], nt, preferred_element_type=F32).astype(BF16)

        @pl.when(i == n - 1)
        def _():
            red_ref[pl.ds(0, 1), :] = jnp.sum(racc[0], axis=0, keepdims=True)
            red_ref[pl.ds(1, 1), :] = jnp.sum(racc[1], axis=0, keepdims=True)

    tok = pl.BlockSpec((tm, D), lambda i: (i, 0))
    wsp = pl.BlockSpec((D, D), lambda i: (0, 0))
    bf = jax.ShapeDtypeStruct((T, D), BF16)
    return _call(
        body, comm, name=f"merge_bwd_{layer}", grid=(n,),
        in_specs=[tok, pl.BlockSpec((tm, D), lambda i: (i, 5)), pl.BlockSpec((tm, D), lambda i: (i, 6)), tok, tok,
                  wsp, wsp, wsp, pl.BlockSpec(gbias.shape, lambda i: (0, 0))],
        out_specs=[tok, tok, tok, tok, pl.BlockSpec((tm, 2 * D), lambda i: (i, 0)), tok, pl.BlockSpec((2, D), lambda i: (0, 0))],
        out_shape=[bf, bf, bf, bf, jax.ShapeDtypeStruct((T, 2 * D), BF16), bf, jax.ShapeDtypeStruct((2, D), F32)],
        scratch_shapes=[pltpu.VMEM((2, SUBLANES, D), F32)],
        compiler_params=_params("arbitrary"),
    )(dx1, p, p, oa, ob, woa, wob, wo, gbias)


N_MIXER_RED = 16


def _mixer_bwd(p, hseq, dya, dyb, dgl, cw, wab, wabt, lam_row, layer, tm, comm=None):
    T = p.shape[0]
    D = p.shape[1] // 7
    n = T // tm
    ngroups = tm // SUBLANES
    nb = wab.shape[0]
    hb = 16
    tn = (((0,), (0,)), ((), ()))

    def body(ba_ref, ca_ref, xa_ref, xb_ref, gb_ref, h_ref, dya_ref, dyb_ref, dgl_ref,
             cap_ref, xap_ref, xbp_ref, hp_ref, ban_ref, dyan_ref,
             cw_ref, wab_ref, wabt_ref, lam_ref,
             dp_ref, red_ref, dwab_ref,
             racc, wacc, anext, gnext, dunext, c_s, g_s):
        i = pl.program_id(0)
        first_tile = i == n - 1
        last_tile = i == 0

        @pl.when(i == 0)
        def _():
            racc[...] = jnp.zeros_like(racc)
            wacc[...] = jnp.zeros_like(wacc)
            anext[...] = jnp.zeros_like(anext)
            gnext[...] = jnp.zeros_like(gnext)
            dunext[...] = jnp.zeros_like(dunext)

        keep_prev = jnp.where(first_tile, 0.0, 1.0)
        keep_next = jnp.where(last_tile, 0.0, 1.0)
        ba = ba_ref[...].astype(F32)
        ca = ca_ref[...].astype(F32)
        xa = xa_ref[...].astype(F32)
        xb = xb_ref[...].astype(F32)
        h = h_ref[...]
        dya = dya_ref[...].astype(F32)
        dyb = dyb_ref[...].astype(F32)
        zprev8 = (cap_ref[...].astype(F32) * xap_ref[...].astype(F32))[hb - SUBLANES:] * keep_prev
        xbprev8 = xbp_ref[...].astype(F32)[hb - SUBLANES:] * keep_prev
        hprev8 = hp_ref[...] * keep_prev
        dcznext8 = (dyan_ref[...].astype(F32) * ban_ref[...].astype(F32))[:SUBLANES] * keep_next

        lam = lam_ref[...]
        sp = _softplus_neg(lam)
        c = _mixer_recompute(ca, xa, xb, zprev8, xbprev8, cw_ref, wab_ref, sp)
        row = lambda k: cw_ref[pl.ds(k, 1), :]
        a, m, r, gi, u = c["a"], c["m"], c["r"], c["gi"], c["u"]

        gelu, dgelu = _gelu_parts(gb_ref[...].astype(F32))
        dgb = dyb * h * dgelu
        Dv = dyb * gelu
        Cv = _shift_up(a, 1, anext[...])
        q = lax.broadcasted_iota(jnp.int32, Cv.shape, 0) & (SUBLANES - 1)
        for s in (1, 2, 4):
            msk = q < SUBLANES - s
            Dv = jnp.where(msk, Dv + Cv * pltpu.roll(Dv, tm - s, 0), Dv)
            Cv = jnp.where(msk, Cv * pltpu.roll(Cv, tm - s, 0), Cv)
        c_s[...] = Cv
        g_s[...] = Dv

        def step(k, carry):
            off = pl.multiple_of((ngroups - 1 - k) * SUBLANES, SUBLANES)
            gg = g_s[pl.ds(off, SUBLANES), :] + c_s[pl.ds(off, SUBLANES), :] * carry
            g_s[pl.ds(off, SUBLANES), :] = gg
            return jnp.broadcast_to(gg[0:1, :], gg.shape)

        gnext[...] = lax.fori_loop(0, ngroups, step, gnext[...], unroll=4)
        anext[...] = a[0:SUBLANES]
        g = g_s[...]

        hprev = _shift_down(h, 1, hprev8)
        da = g * hprev
        gm = g * m
        dgi = gm * u
        du = gm * gi
        dmv = g * gi * u
        dla = a * (da - dmv * a / m)
        dra = dla * ((-LRU_C) * sp) * (r * (1.0 - r))
        dix = dgi * (gi * (1.0 - gi))
        racc[10] += _colsum8(dla * r)
        racc[8] += _colsum8(dra)
        racc[9] += _colsum8(dix)
        drab = dra.astype(BF16)
        dixb = dix.astype(BF16)
        ub = c["ub"]
        dus = []
        for b in range(nb):
            sl = slice(b * LRU_BLOCK, (b + 1) * LRU_BLOCK)
            dri = jnp.concatenate([drab[:, sl], dixb[:, sl]], axis=1)
            dus.append(jnp.dot(dri, wabt_ref[b], preferred_element_type=F32))
            wacc[b] += lax.dot_general(ub[:, sl], dri, tn, preferred_element_type=F32)
        du = du + jnp.concatenate(dus, axis=1)

        dun = dunext[...]
        du1 = _shift_up(du, 1, dun)
        du2 = _shift_up(du, 2, dun)
        du3 = _shift_up(du, 3, dun)
        dxb = row(6) * du + row(5) * du1 + row(4) * du2 + row(3) * du3
        dunext[...] = du[0:SUBLANES]
        racc[6] += _colsum8(du * xb)
        racc[5] += _colsum8(du * c["x1"])
        racc[4] += _colsum8(du * c["x2"])
        racc[3] += _colsum8(du * c["x3"])
        racc[7] += _colsum8(du)

        dba = dya * c["cz"]
        dcz = dya * ba
        dcz1 = _shift_up(dcz, 1, dcznext8)
        dcz2 = _shift_up(dcz, 2, dcznext8)
        dz = row(2) * dcz + row(1) * dcz1 + row(0) * dcz2
        racc[2] += _colsum8(dcz * c["z"])
        racc[1] += _colsum8(dcz * c["z1"])
        racc[0] += _colsum8(dcz * c["z2"])

        dp_ref[:, 0:D] = dba.astype(BF16)
        dp_ref[:, D:2 * D] = (dz * xa).astype(BF16)
        dp_ref[:, 2 * D:3 * D] = (dz * ca).astype(BF16)
        dp_ref[:, 3 * D:4 * D] = dxb.astype(BF16)
        dp_ref[:, 4 * D:5 * D] = dgb.astype(BF16)
        dp_ref[:, 5 * D:7 * D] = dgl_ref[...]

        @pl.when(i == n - 1)
        def _():
            dlam_scale = LRU_C * _sigmoid(-lam)
            for k in range(N_MIXER_RED):
                tot = jnp.sum(racc[k], axis=0, keepdims=True)
                red_ref[pl.ds(k, 1), :] = tot * dlam_scale if k == 10 else tot
            dwab_ref[...] = wacc[...]

    rt = lambda i: n - 1 - i
    col = lambda k: pl.BlockSpec((tm, D), lambda i: (rt(i), k))
    tok = pl.BlockSpec((tm, D), lambda i: (rt(i), 0))
    full = lambda a: pl.BlockSpec(a.shape, lambda i: (0,) * a.ndim)
    prev16 = lambda k: pl.BlockSpec((hb, D), lambda i: (jnp.maximum(rt(i) * (tm // hb) - 1, 0), k))
    next16 = lambda k: pl.BlockSpec((hb, D), lambda i: (jnp.minimum((rt(i) + 1) * (tm // hb), T // hb - 1), k))
    hprev = pl.BlockSpec((SUBLANES, D), lambda i: (jnp.maximum(rt(i) * ngroups - 1, 0), 0))
    return _call(
        body, comm, name=f"mixer_bwd_{layer}", grid=(n,),
        in_specs=[col(0), col(1), col(2), col(3), col(4), tok, tok, tok, pl.BlockSpec((tm, 2 * D), lambda i: (rt(i), 0)),
                  prev16(1), prev16(2), prev16(3), hprev, next16(0), next16(0),
                  full(cw), full(wab), full(wabt), full(lam_row)],
        out_specs=[pl.BlockSpec((tm, 7 * D), lambda i: (rt(i), 0)),
                   pl.BlockSpec((N_MIXER_RED, D), lambda i: (0, 0)),
                   pl.BlockSpec((nb, LRU_BLOCK, 2 * LRU_BLOCK), lambda i: (0, 0, 0))],
        out_shape=[jax.ShapeDtypeStruct((T, 7 * D), BF16), jax.ShapeDtypeStruct((N_MIXER_RED, D), F32),
                   jax.ShapeDtypeStruct((nb, LRU_BLOCK, 2 * LRU_BLOCK), F32)],
        scratch_shapes=[pltpu.VMEM((N_MIXER_RED, SUBLANES, D), F32), pltpu.VMEM((nb, LRU_BLOCK, 2 * LRU_BLOCK), F32),
                        pltpu.VMEM((SUBLANES, D), F32), pltpu.VMEM((SUBLANES, D), F32), pltpu.VMEM((SUBLANES, D), F32),
                        pltpu.VMEM((tm, D), F32), pltpu.VMEM((tm, D), F32)],
        compiler_params=_params("arbitrary"),
    )(p, p, p, p, p, hseq, dya, dyb, dgl, p, p, p, hseq, p, dya, cw, wab, wabt, lam_row)


def _inproj_bwd(dp, dx1, x, g_row, win, layer, tm, comm=None):
    T, D = x.shape
    ns = win.shape[-1]
    n = T // tm
    nt = (((1,), (1,)), ((), ()))

    def body(dp_ref, dx_ref, x_ref, g_ref, w_ref, dx0_ref, red_ref, acc, racc):
        i = pl.program_id(0)
        j = pl.program_id(1)

        @pl.when((i == 0) & (j == 0))
        def _():
            racc[...] = jnp.zeros_like(racc)

        @pl.when(j == 0)
        def _():
            acc[...] = jnp.zeros_like(acc)

        acc[...] += lax.dot_general(dp_ref[...], w_ref[...], nt, preferred_element_type=F32)

        @pl.when(j == N_CHIP - 1)
        def _():
            dx, dgain = _rms_bwd(acc[...], x_ref[...], g_ref[...])
            dx0_ref[...] = dx_ref[...] + dx
            racc[...] += _colsum8(dgain)

        @pl.when((i == n - 1) & (j == N_CHIP - 1))
        def _():
            red_ref[...] = jnp.sum(racc[...], axis=0, keepdims=True)

    tok = pl.BlockSpec((tm, D), lambda i, j: (i, 0))
    return _call(
        body, comm, name=f"inproj_bwd_{layer}", grid=(n, N_CHIP),
        in_specs=[pl.BlockSpec((tm, ns), lambda i, j: (i, j)), tok, tok, pl.BlockSpec((1, D), lambda i, j: (0, 0)),
                  pl.BlockSpec((None, D, ns), lambda i, j: (j, 0, 0))],
        out_specs=[tok, pl.BlockSpec((1, D), lambda i, j: (0, 0))],
        out_shape=[jax.ShapeDtypeStruct((T, D), F32), jax.ShapeDtypeStruct((1, D), F32)],
        scratch_shapes=[pltpu.VMEM((tm, D), F32), pltpu.VMEM((SUBLANES, D), F32)],
        compiler_params=_params("arbitrary", "arbitrary"),
    )(dp, dx1, x, g_row, win)


def _wgrad(a, b, name, tk, a_kind="whole", b_kind="whole", nj=1, comm=None):
    T = a.shape[-2]
    width = lambda v, kind: v.shape[-1] // nj if kind == "cols" else v.shape[-1]
    ka, kb = width(a, a_kind), width(b, b_kind)
    nt = T // tk
    tn = (((0,), (0,)), ((), ()))

    def spec(k, kind):
        if kind == "cm":
            return pl.BlockSpec((None, tk, k), lambda j, t: (j, t, 0))
        if kind == "cols":
            return pl.BlockSpec((tk, k), lambda j, t: (t, j))
        return pl.BlockSpec((tk, k), lambda j, t: (t, 0))

    def body(a_ref, b_ref, o_ref, ob_ref):
        t = pl.program_id(1)

        @pl.when(t == 0)
        def _():
            o_ref[...] = jnp.zeros_like(o_ref)

        o_ref[...] += lax.dot_general(a_ref[...], b_ref[...], tn, preferred_element_type=F32)

        @pl.when(t == nt - 1)
        def _():
            ob_ref[...] = o_ref[...].astype(BF16)

    o_spec = pl.BlockSpec((None, ka, kb), lambda j, t: (j, 0, 0))
    return _call(
        body, comm, name=name, grid=(nj, nt),
        in_specs=[spec(ka, a_kind), spec(kb, b_kind)], out_specs=[o_spec, o_spec],
        out_shape=[jax.ShapeDtypeStruct((nj, ka, kb), F32), jax.ShapeDtypeStruct((nj, ka, kb), BF16)],
        compiler_params=_params("parallel", "arbitrary"),
    )(a, b)


def _wgrad_pair(a, b1, b2, name, tk, comm=None):
    T, ka = a.shape
    nj, _, kb = b1.shape
    nt = T // tk
    tn = (((0,), (0,)), ((), ()))

    def body(a_ref, b1_ref, b2_ref, o1_ref, o1b_ref, o2_ref, o2b_ref):
        t = pl.program_id(1)

        @pl.when(t == 0)
        def _():
            o1_ref[...] = jnp.zeros_like(o1_ref)
            o2_ref[...] = jnp.zeros_like(o2_ref)

        av = a_ref[...]
        o1_ref[...] += lax.dot_general(b1_ref[...], av, tn, preferred_element_type=F32)
        o2_ref[...] += lax.dot_general(b2_ref[...], av, tn, preferred_element_type=F32)

        @pl.when(t == nt - 1)
        def _():
            o1b_ref[...] = o1_ref[...].astype(BF16)
            o2b_ref[...] = o2_ref[...].astype(BF16)

    b_spec = pl.BlockSpec((None, tk, kb), lambda j, t: (j, t, 0))
    o_spec = pl.BlockSpec((None, kb, ka), lambda j, t: (j, 0, 0))
    f32 = jax.ShapeDtypeStruct((nj, kb, ka), F32)
    b16 = jax.ShapeDtypeStruct((nj, kb, ka), BF16)
    return _call(
        body, comm, name=name, grid=(nj, nt),
        in_specs=[pl.BlockSpec((tk, ka), lambda j, t: (t, 0)), b_spec, b_spec], out_specs=[o_spec] * 4,
        out_shape=[f32, b16, f32, b16], compiler_params=_params("parallel", "arbitrary"),
    )(a, b1, b2)


def _block_diag(w):
    hb = LRU_BLOCK // LRU_HEAD_DIM
    nb = w.shape[0] // hb
    w4 = w.reshape(nb, hb, LRU_HEAD_DIM, LRU_HEAD_DIM)
    eye = jnp.eye(hb, dtype=w.dtype)
    return jnp.einsum("bide,ij->bidje", w4, eye).reshape(nb, LRU_BLOCK, LRU_BLOCK)


def _diag_heads(m):
    hb = LRU_BLOCK // LRU_HEAD_DIM
    nb = m.shape[0]
    m5 = m.reshape(nb, hb, LRU_HEAD_DIM, hb, LRU_HEAD_DIM)
    eye = jnp.eye(hb, dtype=m.dtype)
    return jnp.einsum("bidje,ij->bide", m5, eye).reshape(nb * hb, LRU_HEAD_DIM, LRU_HEAD_DIM)


def _tiles(T):
    cap = lambda n: min(n, T)
    return dict(inproj=cap(1024), mixer=cap(256), merge=cap(512), ffn=cap(1024), ffn_bwd=cap(512), loss=cap(512), inproj_bwd=cap(1024),
                wgrad_in=cap(1024), wgrad=cap(2048))


class _NoSchedule:
    def carry(self, name):
        return None

    def after(self, name):
        pass

    def grad(self, key, layer, f32, b16):
        pass


def _local_step(x, target, W, small, tiles, sched):
    L = small["ln1_g"].shape[0]
    D = x.shape[1]
    square = lambda a: a.reshape(D, D)
    saved = []
    h = x
    for l in range(L):
        cw = jnp.concatenate([small["conv_a_w"][l], small["conv_b_w"][l], small["conv_b_b"][l][None],
                              small["lru_ba"][l][None], small["lru_bx"][l][None]], axis=0)
        wab = jnp.concatenate([_block_diag(small["lru_wa"][l]), _block_diag(small["lru_wx"][l])], axis=2).astype(BF16)
        wabt = jnp.swapaxes(wab, 1, 2)
        lam_row = small["lru_lambda"][l][None]
        ln1_row = small["ln1_g"][l][None]
        ln2_row = small["ln2_g"][l][None]
        p, h1 = _rms_inproj(h, ln1_row, W["win", l], l, tiles["inproj"], sched.carry(f"rms_inproj_{l}"))
        ya, yb, hseq = _mixer_fwd(p, cw, wab, lam_row, l, tiles["mixer"], sched.carry(f"mixer_fwd_{l}"))
        oa, ob, mg, x1 = _merge_fwd(h, p, ya, yb, square(W["woa", l]), square(W["wob", l]), square(W["wo", l]),
                                    small["gate_bias"][l], l, tiles["merge"], sched.carry(f"merge_fwd_{l}"))
        h2, gg, uu, x2 = _ffn_fwd(x1, ln2_row, W["wg", l], W["wu", l], W["wd", l], l, tiles["ffn"], sched.carry(f"ffn_fwd_{l}"))
        saved.append(dict(x0=h, p=p, h1=h1, ya=ya, yb=yb, hseq=hseq, oa=oa, ob=ob, mg=mg, x1=x1, h2=h2, gg=gg, uu=uu,
                          cw=cw, wab=wab, wabt=wabt, lam_row=lam_row, ln1_row=ln1_row, ln2_row=ln2_row))
        h = x2

    dx, red = _final_loss(h, small["final_g"][None], target, tiles["loss"])
    loss_row, d_final_g = red[0], red[1]

    gsmall = {k: [None] * L for k in ("ln1_g", "ln2_g", "conv_a_w", "conv_b_w", "conv_b_b", "lru_wa", "lru_ba", "lru_wx",
                                      "lru_bx", "lru_lambda", "gate_bias")}
    tk = tiles["wgrad"]
    for l in reversed(range(L)):
        s = saved[l]
        dgg, duu, act, dx1, dx2b, dln2 = _ffn_bwd(dx, s["x1"], s["ln2_row"], s["gg"], s["uu"], W["wg", l], W["wu", l], W["wd", l],
                                                 l, tiles["ffn_bwd"], sched.carry(f"ffn_bwd_{l}"))
        sched.after(f"ffn_bwd_{l}")
        gate_up = _wgrad_pair(s["h2"], dgg, duu, f"wgrad_ffn_gate_up_{l}", tk, sched.carry(f"wgrad_ffn_gate_up_{l}"))
        sched.grad("wg", l, *gate_up[0:2])
        sched.grad("wu", l, *gate_up[2:4])
        sched.after(f"wgrad_ffn_gate_up_{l}")
        sched.grad("wd", l, *_wgrad(act, dx2b, f"wgrad_ffn_down_{l}", tk, "cm", "whole", N_CHIP, sched.carry(f"wgrad_ffn_down_{l}")))
        dya, dyb, doa, dob, dgl, dx1b, dgbias = _merge_bwd(dx1, s["p"], s["oa"], s["ob"], square(W["woa", l]), square(W["wob", l]),
                                                         square(W["wo", l]), small["gate_bias"][l], l, tiles["merge"],
                                                         sched.carry(f"merge_bwd_{l}"))
        sched.after(f"merge_bwd_{l}")
        sched.grad("wo", l, *_wgrad(s["mg"], dx1b, f"wgrad_w_o_{l}", tk))
        sched.grad("woa", l, *_wgrad(s["ya"], doa, f"wgrad_w_out_a_{l}", tk))
        sched.grad("wob", l, *_wgrad(s["yb"], dob, f"wgrad_w_out_b_{l}", tk))
        dp, mred, dwab = _mixer_bwd(s["p"], s["hseq"], dya, dyb, dgl, s["cw"], s["wab"], s["wabt"], s["lam_row"], l,
                                    tiles["mixer"], sched.carry(f"mixer_bwd_{l}"))
        sched.after(f"mixer_bwd_{l}")
        dx, dln1 = _inproj_bwd(dp, dx1, s["x0"], s["ln1_row"], W["win", l], l, tiles["inproj_bwd"], sched.carry(f"inproj_bwd_{l}"))
        sched.after(f"inproj_bwd_{l}")
        sched.grad("win", l, *_wgrad(s["h1"], dp, f"wgrad_w_in_{l}", tiles["wgrad_in"], "whole", "cols", N_CHIP,
                                     sched.carry(f"wgrad_w_in_{l}")))
        sched.after(f"wgrad_w_in_{l}")
        gsmall["ln1_g"][l] = dln1[0]
        gsmall["ln2_g"][l] = dln2[0]
        gsmall["conv_a_w"][l] = mred[0:CONV_A_K]
        gsmall["conv_b_w"][l] = mred[CONV_A_K:CONV_A_K + CONV_B_K]
        gsmall["conv_b_b"][l] = mred[7]
        gsmall["lru_ba"][l] = mred[8]
        gsmall["lru_bx"][l] = mred[9]
        gsmall["lru_lambda"][l] = mred[10]
        gsmall["lru_wa"][l] = _diag_heads(dwab[:, :, :LRU_BLOCK])
        gsmall["lru_wx"][l] = _diag_heads(dwab[:, :, LRU_BLOCK:])
        gsmall["gate_bias"][l] = dgbias
    gsmall = {k: jnp.stack(v) for k, v in gsmall.items()}
    gsmall["final_g"] = d_final_g
    return loss_row, dx, gsmall


def _small_allreduce(buf):
    R, C = buf.shape
    n_dev = 8
    rp = R // n_dev
    rel = [(k >> 2 & 1, k >> 1 & 1, k & 1) for k in range(1, n_dev)]

    def body(in_ref, out_ref, recv, s1, r1, s2, r2):
        x, y, c, _ = _place()
        flip = lambda v, bit: 1 - v if bit else v
        peers = [(flip(x, kx), flip(y, ky), flip(c, kc)) for kx, ky, kc in rel]
        dev = lambda p: 4 * p[0] + 2 * p[1] + p[2]
        part = lambda ref, d: ref.at[pl.ds(pl.multiple_of(d * rp, SUBLANES), rp), :]
        me = dev((x, y, c))

        def scatter(k, src_dev, to):
            return pltpu.make_async_remote_copy(src_ref=part(in_ref, dev(to)), dst_ref=recv.at[src_dev], send_sem=s1.at[k],
                                                recv_sem=r1.at[k], device_id=to, device_id_type=MESH)

        def gather(k, src_dev, to):
            return pltpu.make_async_remote_copy(src_ref=part(out_ref, src_dev), dst_ref=part(out_ref, src_dev), send_sem=s2.at[k],
                                                recv_sem=r2.at[k], device_id=to, device_id_type=MESH)

        first = [scatter(k, me, p) for k, p in enumerate(peers)]
        for cp in first:
            cp.start()
        recv[me] = part(in_ref, me)[...]
        for k, p in enumerate(peers):
            scatter(k, dev(p), (x, y, c)).wait_recv()
        total = recv[0]
        for d in range(1, n_dev):
            total = total + recv[d]
        part(out_ref, me)[...] = total
        second = [gather(k, me, p) for k, p in enumerate(peers)]
        for cp in second:
            cp.start()
        for k, p in enumerate(peers):
            gather(k, dev(p), (x, y, c)).wait_recv()
        for cp in first + second:
            cp.wait_send()

    dma = pltpu.SemaphoreType.DMA
    vm = pl.BlockSpec(memory_space=pltpu.VMEM)
    return pl.pallas_call(
        body, name="small_allreduce", out_shape=jax.ShapeDtypeStruct((R, C), buf.dtype),
        in_specs=[vm], out_specs=vm,
        scratch_shapes=[pltpu.VMEM((n_dev, rp, C), buf.dtype), dma((n_dev - 1,)), dma((n_dev - 1,)), dma((n_dev - 1,)), dma((n_dev - 1,))],
    )(buf)


ELEMENTWISE_BLOCK_BYTES = 2 * 1024 * 1024


def _row_block(k, n):
    best = None
    for b in range(16, k + 1, 16):
        if k % b == 0 and b * n * 4 <= ELEMENTWISE_BLOCK_BYTES:
            best = b
    return best or k


def _add_halves(g, recv, place_arr, name):
    nj, hk, N = recv.shape
    bk = _row_block(hk, N)
    nb = hk // bk

    def body(k_ref, g_ref, r_ref, o_ref, ob_ref):
        s = g_ref[...] + r_ref[...].astype(F32)
        ob_ref[...] = s.astype(BF16)

        @pl.when(pl.program_id(1) == k_ref[0])
        def _():
            o_ref[...] = s

    blk = pl.BlockSpec((None, bk, N), lambda i, j, k_ref: (j, i, 0))
    grid_spec = pltpu.PrefetchScalarGridSpec(
        num_scalar_prefetch=1, grid=(nb, nj),
        in_specs=[pl.BlockSpec((None, bk, N), lambda i, j, k_ref: (j, k_ref[1] * nb + i, 0)), blk],
        out_specs=[pl.BlockSpec((bk, N), lambda i, j, k_ref: (i, 0)), blk])
    return pl.pallas_call(
        body, name=name, grid_spec=grid_spec,
        out_shape=[jax.ShapeDtypeStruct((hk, N), F32), jax.ShapeDtypeStruct((nj, hk, N), BF16)],
        compiler_params=_params("parallel", "arbitrary"),
    )(place_arr, g, recv)


def _add_chips(pc, recv, place_arr, layer, n_layers, prev, name):
    hk, N = pc.shape
    bk = _row_block(hk, N)
    nb = hk // bk

    def body(k_ref, p_ref, r0_ref, r1_ref, r2_ref, *rest):
        o_ref = rest[-1]
        o_ref[...] = ((p_ref[...] + r0_ref[...].astype(F32)) + r1_ref[...].astype(F32)) + r2_ref[...].astype(F32)

    rspec = lambda j: pl.BlockSpec((None, bk, N), lambda i, k_ref: (j, i, 0))
    in_specs = [pl.BlockSpec((bk, N), lambda i, k_ref: (i, 0)), rspec(0), rspec(1), rspec(2)]
    operands = [pc, recv, recv, recv]
    aliases = {}
    if prev is not None:
        in_specs.append(ANY)
        operands.append(prev)
        aliases = {5: 0}
    grid_spec = pltpu.PrefetchScalarGridSpec(
        num_scalar_prefetch=1, grid=(nb,), in_specs=in_specs,
        out_specs=pl.BlockSpec((None, bk, N), lambda i, k_ref: (layer, k_ref[1] * nb + i, 0)))
    return pl.pallas_call(
        body, name=name, grid_spec=grid_spec, out_shape=jax.ShapeDtypeStruct((n_layers, 2 * hk, N), F32),
        input_output_aliases=aliases, compiler_params=_params("parallel"),
    )(place_arr, *operands)


def _adamw_math(w, g, m, v):
    m = ADAM_B1 * m + (1.0 - ADAM_B1) * g
    v = ADAM_B2 * v + (1.0 - ADAM_B2) * (g * g)
    m_hat = m / (1.0 - ADAM_B1 ** ADAM_STEP)
    v_hat = v / (1.0 - ADAM_B2 ** ADAM_STEP)
    delta = -ADAM_LR * (m_hat / (jnp.sqrt(v_hat) + ADAM_EPS) + ADAM_WD * w)
    return delta, m, v


def _adamw(w, g, m, v, name):
    L, K, N = w.shape
    bk = _row_block(K, N)

    def body(w_ref, g_ref, m_ref, v_ref, d_ref, nm_ref, nv_ref):
        d_ref[...], nm_ref[...], nv_ref[...] = _adamw_math(w_ref[...], g_ref[...], m_ref[...], v_ref[...])

    blk = pl.BlockSpec((None, bk, N), lambda l, i: (l, i, 0))
    sds = jax.ShapeDtypeStruct((L, K, N), F32)
    return pl.pallas_call(
        body, name=name, grid=(L, K // bk), in_specs=[blk] * 4, out_specs=[blk] * 3, out_shape=[sds] * 3,
        compiler_params=_params("parallel", "parallel"),
    )(w, g, m, v)


def _adamw_small(ws, gs, ms, vs):
    n = len(ws)

    def body(*refs):
        w, g, m, v, d, nm, nv = (refs[k * n:(k + 1) * n] for k in range(7))
        for k in range(n):
            d[k][...], nm[k][...], nv[k][...] = _adamw_math(w[k][...], g[k][...], m[k][...], v[k][...])

    sds = [jax.ShapeDtypeStruct(a.shape, F32) for a in ws]
    out = pl.pallas_call(body, name="adamw_small", out_shape=sds * 3)(*ws, *gs, *ms, *vs)
    return out[:n], out[n:2 * n], out[2 * n:]


def _cast_bf16(w, layer, place_arr, name):
    _, K, N = w.shape
    bk = _row_block(K, N)

    def body(k_ref, w_ref, o_ref):
        o_ref[...] = w_ref[...].astype(BF16)

    grid_spec = pltpu.PrefetchScalarGridSpec(
        num_scalar_prefetch=1, grid=(K // bk,),
        in_specs=[pl.BlockSpec((None, bk, N), lambda i, k_ref: (layer, i, 0))],
        out_specs=pl.BlockSpec((None, bk, N), lambda i, k_ref: (k_ref[0], i, 0)))
    return pl.pallas_call(
        body, name=name, grid_spec=grid_spec, out_shape=jax.ShapeDtypeStruct((N_CHIP, K, N), BF16),
        compiler_params=_params("parallel"),
    )(place_arr, w)


BIG = ("w_in", "w_out_a", "w_out_b", "w_o", "w_ffn_gate", "w_ffn_up", "w_ffn_down")
BIG_KEY = dict(w_in="win", w_out_a="woa", w_out_b="wob", w_o="wo", w_ffn_gate="wg", w_ffn_up="wu", w_ffn_down="wd")
SHARDED_SMALL = ("conv_a_w", "conv_b_w", "gate_bias")
REPLICATED = ("ln1_g", "conv_b_b", "lru_wa", "lru_ba", "lru_wx", "lru_bx", "lru_lambda", "ln2_g", "final_g")
WEIGHTS = ("ln1_g", "w_in", "conv_a_w", "conv_b_w", "conv_b_b", "lru_wa", "lru_ba", "lru_wx", "lru_bx", "lru_lambda",
           "w_out_a", "w_out_b", "gate_bias", "w_o", "ln2_g", "w_ffn_gate", "w_ffn_up", "w_ffn_down", "final_g")
LANES = 1024


def _pack_rows(arrays, row_multiple):
    flat = jnp.concatenate([a.reshape(-1) for a in arrays])
    rows = -(-flat.shape[0] // LANES)
    rows = -(-rows // row_multiple) * row_multiple
    flat = jnp.pad(flat, (0, rows * LANES - flat.shape[0]))
    return flat.reshape(rows, LANES)


def _unpack_rows(buf, shapes):
    flat = buf.reshape(-1)
    out, off = [], 0
    for s in shapes:
        n = 1
        for d in s:
            n *= d
        out.append(flat[off:off + n].reshape(s))
        off += n
    return out


OUT_KEYS = ("wo", "woa", "wob")
FFN_KEYS = ("wg", "wu", "wd")


def _items(keys, layer):
    return [(k, layer) for k in keys]


CARRY = {
    "rms_inproj_0": [("gather_ici", _items(OUT_KEYS + FFN_KEYS, 0))],
    "mixer_fwd_0": [("gather_d2d", _items(OUT_KEYS + FFN_KEYS, 0)), ("gather_ici", _items(("win",) + OUT_KEYS, 1))],
    "merge_fwd_0": [("gather_d2d", _items(("win",) + OUT_KEYS, 1))],
    "ffn_fwd_0": [("gather_ici", _items(FFN_KEYS, 1))],
    "rms_inproj_1": [("gather_d2d", _items(FFN_KEYS, 1))],
    "merge_bwd_1": [("halves", _items(FFN_KEYS, 1))],
    "mixer_bwd_1": [("chips", _items(FFN_KEYS, 1)), ("halves", _items(OUT_KEYS, 1))],
    "inproj_bwd_1": [("chips", _items(OUT_KEYS, 1)), ("share", _items(FFN_KEYS, 1))],
    "ffn_bwd_0": [("halves", [("win", 1)]), ("share", _items(OUT_KEYS, 1))],
    "wgrad_ffn_gate_up_0": [("chips", [("win", 1)])],
    "wgrad_ffn_down_0": [("share", [("win", 1)])],
    "merge_bwd_0": [("halves", _items(FFN_KEYS, 0))],
    "mixer_bwd_0": [("chips", _items(FFN_KEYS, 0)), ("halves", _items(OUT_KEYS, 0))],
    "wgrad_w_in_0": [("chips", _items(OUT_KEYS, 0)), ("share", _items(FFN_KEYS, 0))],
}
AFTER = {
    "merge_bwd_1": [("add_halves", _items(FFN_KEYS, 1))],
    "mixer_bwd_1": [("add_chips", _items(FFN_KEYS, 1)), ("add_halves", _items(OUT_KEYS, 1))],
    "inproj_bwd_1": [("add_chips", _items(OUT_KEYS, 1))],
    "ffn_bwd_0": [("add_halves", [("win", 1)])],
    "wgrad_ffn_gate_up_0": [("add_chips", [("win", 1)])],
    "merge_bwd_0": [("add_halves", _items(FFN_KEYS, 0))],
    "mixer_bwd_0": [("add_chips", _items(FFN_KEYS, 0)), ("add_halves", _items(OUT_KEYS, 0))],
    "wgrad_w_in_0": [("add_chips", _items(OUT_KEYS, 0))],
}


class _Schedule:
    def __init__(self, slots, place_arr, n_layers):
        self.W = slots
        self.place, self.L = place_arr, n_layers
        self.g32, self.g16 = {}, {}
        self.from_sibling, self.chip_sum, self.chip_sum16, self.from_chips = {}, {}, {}, {}
        self.reduced = {}

    def stage(self, comm, kind, items):
        bf = lambda shape: jax.ShapeDtypeStruct(shape, BF16)
        for it in items:
            if kind == "gather_ici":
                comm.add(_gather_ici, 3, io=[(self.W, it)])
            elif kind == "gather_d2d":
                comm.add(_gather_d2d, 3, io=[(self.W, it)])
            elif kind == "halves":
                nj, K, N = self.g16[it].shape
                comm.add(_reduce_halves, 1, ro=[self.g16[it]], nw=[(self.from_sibling, it, bf((nj, K // 2, N)))])
            elif kind == "chips":
                _, hk, N = self.chip_sum16[it].shape
                comm.add(_reduce_chips, 3, ro=[self.chip_sum16[it]], nw=[(self.from_chips, it, bf((3, hk, N)))])
            elif kind == "share":
                comm.add(_reduce_share(it[1]), 1, io=[(self.reduced, it[0])])
        return comm

    def carry(self, name):
        comm = _Carried()
        for kind, items in CARRY.get(name, ()):
            self.stage(comm, kind, items)
        return comm

    def run(self, name, rounds):
        _run_comm([self.stage(_Carried(), kind, items) for kind, items in rounds], name)

    def grad(self, key, layer, f32, b16):
        by_chip = lambda g: g.reshape(N_CHIP, -1, g.shape[-1])
        self.g32[key, layer], self.g16[key, layer] = by_chip(f32), by_chip(b16)

    def add(self, kind, items):
        for key, layer in items:
            it = (key, layer)
            if kind == "add_halves":
                self.chip_sum[it], self.chip_sum16[it] = _add_halves(self.g32[it], self.from_sibling[it], self.place,
                                                                    f"add_halves_{key}_{layer}")
            else:
                self.reduced[key] = _add_chips(self.chip_sum[it], self.from_chips[it], self.place, layer, self.L,
                                               self.reduced.get(key), f"add_chips_{key}_{layer}")

    def after(self, name):
        for kind, items in AFTER.get(name, ()):
            self.add(kind, items)


def _step(w, m, v, x, target):
    xi, yi, ci = lax.axis_index("x"), lax.axis_index("y"), lax.axis_index("c")
    chip = _chip_id(xi, yi)
    place_arr = jnp.stack([chip, ci]).astype(jnp.int32)
    L = w["ln1_g"].shape[0]
    assert L == 2
    D = x.shape[1]
    dc = D // N_CHIP

    stored = lambda n, a: jnp.swapaxes(a, 1, 2) if n in ("w_ffn_gate", "w_ffn_up") else a
    slots = {(BIG_KEY[n], l): _cast_bf16(stored(n, w[n]), l, place_arr, f"cast_{n}_{l}") for n in BIG for l in range(L)}
    sched = _Schedule(slots, place_arr, L)
    small_shard = jnp.concatenate([w[n] for n in SHARDED_SMALL], axis=1)
    got = {}
    first = sched.stage(_Carried(), "gather_ici", [("win", 0)])
    first.add(_gather_small, 3, ro=[small_shard], nw=[(got, "small", jax.ShapeDtypeStruct((3,) + small_shard.shape, F32))])
    _run_comm([first, sched.stage(_Carried(), "gather_d2d", [("win", 0)])], "gather_first")
    small_g = jnp.zeros((N_CHIP,) + small_shard.shape, F32)
    small_g = lax.dynamic_update_index_in_dim(small_g, small_shard, chip, 0)
    for j, (cx, cy) in enumerate([(1 - xi, yi), (xi, 1 - yi), (1 - xi, 1 - yi)]):
        small_g = lax.dynamic_update_index_in_dim(small_g, got["small"][j], _chip_id(cx, cy), 0)
    small_full = jnp.transpose(small_g, (1, 2, 0, 3)).reshape(L, small_shard.shape[1], D)
    small = {n: w[n] for n in REPLICATED}
    off = 0
    for n in SHARDED_SMALL:
        k = w[n].shape[1]
        small[n] = small_full[:, off:off + k]
        off += k

    loss_row, grad_x, gsmall = _local_step(x, target, sched.W, small, _tiles(x.shape[0]), sched)

    sched.run("reduce_tail_halves", [("halves", [("win", 0)]), ("share", _items(OUT_KEYS, 0))])
    sched.add("add_halves", [("win", 0)])
    sched.run("reduce_tail_chips", [("chips", [("win", 0)])])
    sched.add("add_chips", [("win", 0)])
    sched.run("reduce_tail_share", [("share", [("win", 0)])])
    grads = {n: stored(n, sched.reduced[BIG_KEY[n]]) for n in BIG}

    order = [n for n in WEIGHTS if n not in BIG]
    packed = _pack_rows([gsmall[n] for n in order] + [loss_row], 8 * SUBLANES)
    summed = _small_allreduce(packed)
    parts = _unpack_rows(summed, [gsmall[n].shape for n in order] + [loss_row.shape])
    loss = jnp.sum(parts[-1])
    for n, g in zip(order, parts[:-1]):
        grads[n] = lax.dynamic_slice_in_dim(g, chip * dc, dc, axis=2) if n in SHARDED_SMALL else g

    delta, new_m, new_v = {}, {}, {}
    for n in BIG:
        d, nm, nv = _adamw(stored(n, w[n]), sched.reduced[BIG_KEY[n]], stored(n, m[n]), stored(n, v[n]), f"adamw_{n}")
        delta[n], new_m[n], new_v[n] = stored(n, d), stored(n, nm), stored(n, nv)
    for d, arrays in zip((delta, new_m, new_v), _adamw_small(*([d[n] for n in order] for d in (w, grads, m, v)))):
        d.update(zip(order, arrays))
    return loss, grad_x, grads, delta, new_m, new_v


def kernel(x, ln1_g, w_in, conv_a_w, conv_b_w, conv_b_b, lru_wa, lru_ba, lru_wx, lru_bx, lru_lambda, w_out_a, w_out_b, gate_bias, w_o, ln2_g, w_ffn_gate, w_ffn_up, w_ffn_down, final_g, loss_target, m_ln1_g, m_w_in, m_conv_a_w, m_conv_b_w, m_conv_b_b, m_lru_wa, m_lru_ba, m_lru_wx, m_lru_bx, m_lru_lambda, m_w_out_a, m_w_out_b, m_gate_bias, m_w_o, m_ln2_g, m_w_ffn_gate, m_w_ffn_up, m_w_ffn_down, m_final_g, v_ln1_g, v_w_in, v_conv_a_w, v_conv_b_w, v_conv_b_b, v_lru_wa, v_lru_ba, v_lru_wx, v_lru_bx, v_lru_lambda, v_w_out_a, v_w_out_b, v_gate_bias, v_w_o, v_ln2_g, v_w_ffn_gate, v_w_ffn_up, v_w_ffn_down, v_final_g):
    w = dict(ln1_g=ln1_g, w_in=w_in, conv_a_w=conv_a_w, conv_b_w=conv_b_w, conv_b_b=conv_b_b, lru_wa=lru_wa, lru_ba=lru_ba,
             lru_wx=lru_wx, lru_bx=lru_bx, lru_lambda=lru_lambda, w_out_a=w_out_a, w_out_b=w_out_b, gate_bias=gate_bias, w_o=w_o,
             ln2_g=ln2_g, w_ffn_gate=w_ffn_gate, w_ffn_up=w_ffn_up, w_ffn_down=w_ffn_down, final_g=final_g)
    m = dict(ln1_g=m_ln1_g, w_in=m_w_in, conv_a_w=m_conv_a_w, conv_b_w=m_conv_b_w, conv_b_b=m_conv_b_b, lru_wa=m_lru_wa,
             lru_ba=m_lru_ba, lru_wx=m_lru_wx, lru_bx=m_lru_bx, lru_lambda=m_lru_lambda, w_out_a=m_w_out_a, w_out_b=m_w_out_b,
             gate_bias=m_gate_bias, w_o=m_w_o, ln2_g=m_ln2_g, w_ffn_gate=m_w_ffn_gate, w_ffn_up=m_w_ffn_up,
             w_ffn_down=m_w_ffn_down, final_g=m_final_g)
    v = dict(ln1_g=v_ln1_g, w_in=v_w_in, conv_a_w=v_conv_a_w, conv_b_w=v_conv_b_w, conv_b_b=v_conv_b_b, lru_wa=v_lru_wa,
             lru_ba=v_lru_ba, lru_wx=v_lru_wx, lru_bx=v_lru_bx, lru_lambda=v_lru_lambda, w_out_a=v_w_out_a, w_out_b=v_w_out_b,
             gate_bias=v_gate_bias, w_o=v_w_o, ln2_g=v_ln2_g, w_ffn_gate=v_w_ffn_gate, w_ffn_up=v_w_ffn_up,
             w_ffn_down=v_w_ffn_down, final_g=v_final_g)
    loss, grad_x, grads, delta, new_m, new_v = _step(w, m, v, x[0], loss_target[0])
    return (loss, grad_x[None], *[grads[n] for n in WEIGHTS], *[delta[n] for n in WEIGHTS],
            *[new_m[n] for n in WEIGHTS], *[new_v[n] for n in WEIGHTS])
```

```python
import functools

import jax
import jax.numpy as jnp
from jax import lax
from jax.experimental import pallas as pl
from jax.experimental.pallas import tpu as pltpu

F32 = jnp.float32
BF16 = jnp.bfloat16
MESH = pl.DeviceIdType.MESH

N_CHIP = 4
RMS_EPS = 1e-6
LRU_C = 8.0
LRU_HEAD_DIM = 64
LRU_BLOCK = 256
CONV_A_K = 3
CONV_B_K = 4
ADAM_LR = 0.001
ADAM_B1 = 0.9
ADAM_B2 = 0.999
ADAM_EPS = 1e-08
ADAM_WD = 0.01
ADAM_STEP = 10
SUBLANES = 8
VMEM_LIMIT = 56 * 1024 * 1024


def _params(*sem):
    return pltpu.CompilerParams(dimension_semantics=sem, vmem_limit_bytes=VMEM_LIMIT)


def _sigmoid(v):
    return 1.0 / (1.0 + jnp.exp(-v))


def _one_minus_sq(la, a):
    return jnp.tanh(-la) * (1.0 + a * a)


def _gelu_parts(v):
    k = 0.7978845608028654
    v2 = v * v
    t = jnp.tanh(k * (v + 0.044715 * v * v2))
    gelu = 0.5 * v * (1.0 + t)
    dgelu = 0.5 * (1.0 + t) + 0.5 * v * (1.0 - t * t) * k * (1.0 + 3 * 0.044715 * v2)
    return gelu, dgelu


def _shift_down(v, k, prev8):
    rolled = pltpu.roll(v, k, 0)
    r8 = lax.broadcasted_iota(jnp.int32, prev8.shape, 0)
    head = jnp.where(r8 < k, pltpu.roll(prev8, k, 0), rolled[0:SUBLANES])
    return jnp.concatenate([head, rolled[SUBLANES:]], axis=0)


def _shift_up(v, k, next8):
    tm = v.shape[0]
    rolled = pltpu.roll(v, tm - k, 0)
    r8 = lax.broadcasted_iota(jnp.int32, next8.shape, 0)
    tail = jnp.where(r8 >= SUBLANES - k, pltpu.roll(next8, SUBLANES - k, 0), rolled[tm - SUBLANES:])
    return jnp.concatenate([rolled[:tm - SUBLANES], tail], axis=0)


def _group_scan(a, b, reverse):
    tm, c = a.shape
    a = a.reshape(tm // SUBLANES, SUBLANES, c)
    b = b.reshape(tm // SUBLANES, SUBLANES, c)
    q = lax.broadcasted_iota(jnp.int32, a.shape, 1)
    for s in (1, 2, 4):
        msk = q < SUBLANES - s if reverse else q >= s
        shift = SUBLANES - s if reverse else s
        b = jnp.where(msk, a * pltpu.roll(b, shift, 1) + b, b)
        a = jnp.where(msk, a * pltpu.roll(a, shift, 1), a)
    return a.reshape(tm, c), b.reshape(tm, c)


def _colsum8(v):
    tm, c = v.shape
    return jnp.sum(v.reshape(tm // SUBLANES, SUBLANES, c), axis=0)


def _rms_stats(xv):
    var = jnp.mean(xv * xv, axis=-1, keepdims=True)
    return lax.rsqrt(var + RMS_EPS)


def _rms_bwd(dh, xv, g):
    rstd = _rms_stats(xv)
    xhat = xv * rstd
    dxhat = dh * g
    dx = rstd * (dxhat - xhat * jnp.mean(dxhat * xhat, axis=-1, keepdims=True))
    return dx, dh * xhat


ANY = pl.BlockSpec(memory_space=pl.ANY)


def _place():
    x, y, c = lax.axis_index("x"), lax.axis_index("y"), lax.axis_index("c")
    other_chips = [(1 - x, y), (x, 1 - y), (1 - x, 1 - y)]
    return x, y, c, other_chips


def _chip_id(x, y):
    return 2 * x + y


def _half(c, hk):
    return pl.ds(pl.multiple_of(c * hk, 16), hk)


def _remote(src, dst, to, sems):
    return pltpu.make_async_remote_copy(src_ref=src, dst_ref=dst, device_id=to, device_id_type=MESH, **sems)


class _Carried:
    def __init__(self):
        self.ro, self.io, self.nw, self.parts, self.n = [], [], [], [], 0

    def add(self, maker, n, ro=(), io=(), nw=()):
        def index(items, item, same):
            for k, other in enumerate(items):
                if same(other, item):
                    return k
            items.append(item)
            return len(items) - 1

        r = [index(self.ro, a, lambda p, q: p is q) for a in ro]
        i = [index(self.io, a, lambda p, q: p[0] is q[0] and p[1] == q[1]) for a in io]
        w = [index(self.nw, a, lambda p, q: False) for a in nw]
        self.parts.append((maker, r, i, w, self.n))
        self.n += n
        return self

    def pairs(self, ro, io, nw, ssem, rsem):
        out = []
        for maker, r, i, w, base in self.parts:
            sems = lambda k, base=base: dict(send_sem=ssem.at[base + k], recv_sem=rsem.at[base + k])
            out += maker([ro[k] for k in r], [io[k] for k in i], [nw[k] for k in w], sems)
        return out

    def start(self, *refs):
        for send, _ in self.pairs(*refs):
            send.start()

    def finish(self, *refs):
        pairs = self.pairs(*refs)
        for _, recv in pairs:
            recv.wait_recv()
        for send, _ in pairs:
            send.wait_send()

    def operands(self):
        return list(self.ro) + [store[key] for store, key in self.io]

    def out_shapes(self):
        return [jax.ShapeDtypeStruct(store[key].shape, store[key].dtype) for store, key in self.io] + [s for _, _, s in self.nw]

    def keep(self, results):
        for (store, key), arr in zip(self.io, results[:len(self.io)]):
            store[key] = arr
        for (store, key, _), arr in zip(self.nw, results[len(self.io):]):
            store[key] = arr


def _call(body, comm, *, name, grid, in_specs, out_specs, out_shape, compiler_params, scratch_shapes=(), aliases=None):
    aliases = dict(aliases or {})
    if comm is None or not comm.parts:
        return pl.pallas_call(body, name=name, grid=grid, in_specs=in_specs, out_specs=out_specs, out_shape=out_shape,
                              scratch_shapes=list(scratch_shapes), input_output_aliases=aliases, compiler_params=compiler_params)
    n_in, n_out, n_scr = len(in_specs), len(out_shape), len(scratch_shapes)
    n_ro, n_io, n_nw = len(comm.ro), len(comm.io), len(comm.nw)

    def carried(*refs):
        base_in = refs[:n_in]
        ro = refs[n_in:n_in + n_ro]
        pos = n_in + n_ro + n_io
        base_out = refs[pos:pos + n_out]
        io = refs[pos + n_out:pos + n_out + n_io]
        nw = refs[pos + n_out + n_io:pos + n_out + n_io + n_nw]
        pos += n_out + n_io + n_nw
        scr = refs[pos:pos + n_scr]
        ssem, rsem = refs[pos + n_scr], refs[pos + n_scr + 1]
        first = pl.program_id(0) == 0
        last = pl.program_id(0) == grid[0] - 1
        for axis in range(1, len(grid)):
            first = first & (pl.program_id(axis) == 0)
            last = last & (pl.program_id(axis) == grid[axis] - 1)

        @pl.when(first)
        def _():
            comm.start(ro, io, nw, ssem, rsem)

        body(*base_in, *base_out, *scr)

        @pl.when(last)
        def _():
            comm.finish(ro, io, nw, ssem, rsem)

    aliases.update({n_in + n_ro + k: n_out + k for k in range(n_io)})
    dma = pltpu.SemaphoreType.DMA
    call = pl.pallas_call(
        carried, name=name, grid=grid,
        in_specs=list(in_specs) + [ANY] * (n_ro + n_io), out_specs=list(out_specs) + [ANY] * (n_io + n_nw),
        out_shape=list(out_shape) + comm.out_shapes(), input_output_aliases=aliases,
        scratch_shapes=list(scratch_shapes) + [dma((comm.n,)), dma((comm.n,))], compiler_params=compiler_params)

    def run(*operands):
        res = call(*operands, *comm.operands())
        comm.keep(res[n_out:])
        return res[:n_out]

    return run


def _run_comm(rounds, name):
    ro, io, nw, uses = [], [], [], []
    for r in rounds:
        def index(items, item, same):
            for k, other in enumerate(items):
                if same(other, item):
                    return k
            items.append(item)
            return len(items) - 1
        uses.append(([index(ro, a, lambda p, q: p is q) for a in r.ro],
                     [index(io, a, lambda p, q: p[0] is q[0] and p[1] == q[1]) for a in r.io],
                     [index(nw, a, lambda p, q: False) for a in r.nw]))
    n_ro, n_io, n_nw = len(ro), len(io), len(nw)

    def body(*refs):
        ro_refs = refs[:n_ro]
        io_refs = refs[n_ro + n_io:n_ro + 2 * n_io]
        nw_refs = refs[n_ro + 2 * n_io:n_ro + 2 * n_io + n_nw]
        sems = refs[n_ro + 2 * n_io + n_nw:]
        for k, (r, (a, b, c)) in enumerate(zip(rounds, uses)):
            args = ([ro_refs[i] for i in a], [io_refs[i] for i in b], [nw_refs[i] for i in c], sems[2 * k], sems[2 * k + 1])
            r.start(*args)
            r.finish(*args)

    operands = ro + [store[key] for store, key in io]
    out_shape = [jax.ShapeDtypeStruct(store[key].shape, store[key].dtype) for store, key in io] + [s for _, _, s in nw]
    dma = pltpu.SemaphoreType.DMA
    res = pl.pallas_call(
        body, name=name, out_shape=out_shape,
        in_specs=[ANY] * (n_ro + n_io), out_specs=[ANY] * (n_io + n_nw),
        input_output_aliases={n_ro + k: k for k in range(n_io)},
        scratch_shapes=[dma((r.n,)) for r in rounds for _ in range(2)],
    )(*operands)
    for (store, key), arr in zip(io, res[:n_io]):
        store[key] = arr
    for (store, key, _), arr in zip(nw, res[n_io:]):
        store[key] = arr


def _gather_ici(ro, io, nw, sems):
    s = io[0]
    x, y, c, chips = _place()
    hk = s.shape[1] // 2
    mine = s.at[_chip_id(x, y), _half(c, hk)]
    pairs = []
    for j, (cx, cy) in enumerate(chips):
        theirs = s.at[_chip_id(cx, cy), _half(c, hk)]
        pairs.append((_remote(mine, mine, (cx, cy, c), sems(j)), _remote(theirs, theirs, (cx, cy, c), sems(j))))
    return pairs


def _gather_d2d(ro, io, nw, sems):
    s = io[0]
    x, y, c, chips = _place()
    hk = s.shape[1] // 2
    sib = (x, y, 1 - c)
    pairs = []
    for j, (cx, cy) in enumerate(chips):
        here = s.at[_chip_id(cx, cy), _half(c, hk)]
        there = s.at[_chip_id(cx, cy), _half(1 - c, hk)]
        pairs.append((_remote(here, here, sib, sems(j)), _remote(there, there, sib, sems(j))))
    return pairs


def _gather_small(ro, io, nw, sems):
    x, y, c, chips = _place()
    return [(_remote(ro[0], nw[0].at[j], (cx, cy, c), sems(j)),) * 2 for j, (cx, cy) in enumerate(chips)]


def _reduce_halves(ro, io, nw, sems):
    x, y, c, _ = _place()
    g = ro[0]
    hk = g.shape[1] // 2
    sib = (x, y, 1 - c)
    return [(_remote(g.at[:, _half(1 - c, hk)], nw[0], sib, sems(0)), _remote(g.at[:, _half(c, hk)], nw[0], sib, sems(0)))]


def _reduce_chips(ro, io, nw, sems):
    x, y, c, chips = _place()
    return [(_remote(ro[0].at[_chip_id(cx, cy)], nw[0].at[j], (cx, cy, c), sems(j)),) * 2 for j, (cx, cy) in enumerate(chips)]


def _reduce_share(layer):
    def maker(ro, io, nw, sems):
        g = io[0]
        x, y, c, _ = _place()
        hk = g.shape[1] // 2
        sib = (x, y, 1 - c)
        mine, theirs = g.at[layer, _half(c, hk)], g.at[layer, _half(1 - c, hk)]
        return [(_remote(mine, mine, sib, sems(0)), _remote(theirs, theirs, sib, sems(0)))]
    return maker


def _rms_inproj(x, g_row, win, layer, tm, comm=None):
    T, D = x.shape
    ns = win.shape[-1]

    def body(x_ref, g_ref, w_ref, p_ref, h_ref):
        @pl.when(pl.program_id(1) == 0)
        def _():
            xv = x_ref[...]
            h_ref[...] = (xv * _rms_stats(xv) * g_ref[...]).astype(BF16)
        p_ref[...] = jnp.dot(h_ref[...], w_ref[...], preferred_element_type=F32).astype(BF16)

    return _call(
        body, comm, name=f"rms_inproj_{layer}", grid=(T // tm, N_CHIP),
        in_specs=[pl.BlockSpec((tm, D), lambda i, j: (i, 0)),
                  pl.BlockSpec((1, D), lambda i, j: (0, 0)),
                  pl.BlockSpec((None, D, ns), lambda i, j: (j, 0, 0))],
        out_specs=[pl.BlockSpec((tm, ns), lambda i, j: (i, j)),
                   pl.BlockSpec((tm, D), lambda i, j: (i, 0))],
        out_shape=[jax.ShapeDtypeStruct((T, N_CHIP * ns), BF16), jax.ShapeDtypeStruct((T, D), BF16)],
        compiler_params=_params("parallel", "arbitrary"),
    )(x, g_row, win)


def _mixer_recompute(ca, xa, xb, zprev8, xbprev8, cw_ref, wab_ref, sp):
    row = lambda k: cw_ref[pl.ds(k, 1), :]
    z = ca * xa
    z1 = _shift_down(z, 1, zprev8)
    z2 = _shift_down(z, 2, zprev8)
    cz = row(2) * z + row(1) * z1 + row(0) * z2
    x1 = _shift_down(xb, 1, xbprev8)
    x2 = _shift_down(xb, 2, xbprev8)
    x3 = _shift_down(xb, 3, xbprev8)
    u = row(6) * xb + row(5) * x1 + row(4) * x2 + row(3) * x3 + row(7)
    ub = u.astype(BF16)
    nb = wab_ref.shape[0]
    ras, ixs = [], []
    for b in range(nb):
        ri = jnp.dot(ub[:, b * LRU_BLOCK:(b + 1) * LRU_BLOCK], wab_ref[b], preferred_element_type=F32)
        ras.append(ri[:, :LRU_BLOCK])
        ixs.append(ri[:, LRU_BLOCK:])
    r = _sigmoid(jnp.concatenate(ras, axis=1) + row(8))
    gi = _sigmoid(jnp.concatenate(ixs, axis=1) + row(9))
    la = (-LRU_C) * r * sp
    a = jnp.exp(la)
    m = jnp.sqrt(_one_minus_sq(la, a))
    return dict(z=z, z1=z1, z2=z2, cz=cz, x1=x1, x2=x2, x3=x3, u=u, ub=ub, r=r, gi=gi, a=a, m=m)


def _softplus_neg(lam):
    v = -lam
    return jnp.maximum(v, 0.0) + jnp.log1p(jnp.exp(-jnp.abs(v)))


def _mixer_fwd(p, cw, wab, lam_row, layer, tm, comm=None):
    T = p.shape[0]
    D = p.shape[1] // 7
    ngroups = tm // SUBLANES

    def body(ba_ref, ca_ref, xa_ref, xb_ref, gb_ref, cw_ref, wab_ref, lam_ref, ya_ref, yb_ref, h_ref,
             zprev, xbprev, hcarry, a_s, h_s):
        @pl.when(pl.program_id(0) == 0)
        def _():
            zprev[...] = jnp.zeros_like(zprev)
            xbprev[...] = jnp.zeros_like(xbprev)
            hcarry[...] = jnp.zeros_like(hcarry)

        ca = ca_ref[...].astype(F32)
        xa = xa_ref[...].astype(F32)
        xb = xb_ref[...].astype(F32)
        sp = _softplus_neg(lam_ref[...])
        c = _mixer_recompute(ca, xa, xb, zprev[...], xbprev[...], cw_ref, wab_ref, sp)
        zprev[...] = c["z"][tm - SUBLANES:]
        xbprev[...] = xb[tm - SUBLANES:]
        ya_ref[...] = (ba_ref[...].astype(F32) * c["cz"]).astype(BF16)

        a_s[...], h_s[...] = _group_scan(c["a"], c["m"] * c["gi"] * c["u"], reverse=False)

        def step(g, carry):
            off = pl.multiple_of(g * SUBLANES, SUBLANES)
            hg = h_s[pl.ds(off, SUBLANES), :] + a_s[pl.ds(off, SUBLANES), :] * carry
            h_s[pl.ds(off, SUBLANES), :] = hg
            return jnp.broadcast_to(hg[SUBLANES - 1:SUBLANES, :], hg.shape)

        hcarry[...] = lax.fori_loop(0, ngroups, step, hcarry[...], unroll=4)
        h = h_s[...]
        h_ref[...] = h
        gelu, _ = _gelu_parts(gb_ref[...].astype(F32))
        yb_ref[...] = (h * gelu).astype(BF16)

    col = lambda k: pl.BlockSpec((tm, D), lambda i: (i, k))
    full = lambda a: pl.BlockSpec(a.shape, lambda i: (0,) * a.ndim)
    tok = pl.BlockSpec((tm, D), lambda i: (i, 0))
    return _call(
        body, comm, name=f"mixer_fwd_{layer}", grid=(T // tm,),
        in_specs=[col(0), col(1), col(2), col(3), col(4), full(cw), full(wab), full(lam_row)],
        out_specs=[tok, tok, tok],
        out_shape=[jax.ShapeDtypeStruct((T, D), BF16), jax.ShapeDtypeStruct((T, D), BF16), jax.ShapeDtypeStruct((T, D), F32)],
        scratch_shapes=[pltpu.VMEM((SUBLANES, D), F32), pltpu.VMEM((SUBLANES, D), F32), pltpu.VMEM((SUBLANES, D), F32),
                        pltpu.VMEM((tm, D), F32), pltpu.VMEM((tm, D), F32)],
        compiler_params=_params("arbitrary"),
    )(p, p, p, p, p, cw, wab, lam_row)


def _merge_fwd(x, p, ya, yb, woa, wob, wo, gbias, layer, tm, comm=None):
    T, D = x.shape

    def body(x_ref, ga_ref, gb_ref, ya_ref, yb_ref, woa_ref, wob_ref, wo_ref, bias_ref, oa_ref, ob_ref, mg_ref, x1_ref):
        oa = jnp.dot(ya_ref[...], woa_ref[...], preferred_element_type=F32)
        ob = jnp.dot(yb_ref[...], wob_ref[...], preferred_element_type=F32)
        sa = _sigmoid(ga_ref[...].astype(F32) + bias_ref[pl.ds(0, 1), :])
        sb = _sigmoid(gb_ref[...].astype(F32) + bias_ref[pl.ds(1, 1), :])
        mg = (sa * oa + sb * ob).astype(BF16)
        oa_ref[...] = oa.astype(BF16)
        ob_ref[...] = ob.astype(BF16)
        mg_ref[...] = mg
        x1_ref[...] = x_ref[...] + jnp.dot(mg, wo_ref[...], preferred_element_type=F32)

    tok = pl.BlockSpec((tm, D), lambda i: (i, 0))
    wsp = pl.BlockSpec((D, D), lambda i: (0, 0))
    bf = jax.ShapeDtypeStruct((T, D), BF16)
    return _call(
        body, comm, name=f"merge_fwd_{layer}", grid=(T // tm,),
        in_specs=[tok, pl.BlockSpec((tm, D), lambda i: (i, 5)), pl.BlockSpec((tm, D), lambda i: (i, 6)), tok, tok,
                  wsp, wsp, wsp, pl.BlockSpec(gbias.shape, lambda i: (0, 0))],
        out_specs=[tok, tok, tok, tok],
        out_shape=[bf, bf, bf, jax.ShapeDtypeStruct((T, D), F32)],
        compiler_params=_params("parallel"),
    )(x, p, p, ya, yb, woa, wob, wo, gbias)


def _loss_tile(xv, g, tgt):
    d = xv.shape[-1]
    rstd = _rms_stats(xv)
    xhat = xv * rstd
    err = xhat * g - tgt
    dy = err * (1.0 / d)
    dxhat = dy * g
    dx = rstd * (dxhat - xhat * jnp.mean(dxhat * xhat, axis=-1, keepdims=True))
    return dx, _colsum8(err * err), _colsum8(dy * xhat)


def _ffn_fwd(x1, g_row, wg, wu, wd, layer, tm, comm=None, head=None):
    T, D = x1.shape
    fs = wg.shape[-2]
    n = T // tm
    nt = (((1,), (1,)), ((), ()))

    def body(x_ref, g_ref, wg_ref, wu_ref, wd_ref, *rest):
        if head is None:
            h_ref, gg_ref, uu_ref, x2_ref, acc = rest
        else:
            gf_ref, t_ref, h_ref, gg_ref, uu_ref, x2_ref, red_ref, acc, racc = rest
        i = pl.program_id(0)
        j = pl.program_id(1)

        @pl.when(j == 0)
        def _():
            xv = x_ref[...]
            h_ref[...] = (xv * _rms_stats(xv) * g_ref[...]).astype(BF16)
            acc[...] = xv

        h = h_ref[...]
        gg = lax.dot_general(h, wg_ref[...], nt, preferred_element_type=F32)
        uu = lax.dot_general(h, wu_ref[...], nt, preferred_element_type=F32)
        gg_ref[...] = gg.astype(BF16)
        uu_ref[...] = uu.astype(BF16)
        act = (gg * _sigmoid(gg) * uu).astype(BF16)
        acc[...] += jnp.dot(act, wd_ref[...], preferred_element_type=F32)

        if head is None:
            @pl.when(j == N_CHIP - 1)
            def _():
                x2_ref[...] = acc[...]
        else:
            @pl.when((i == 0) & (j == 0))
            def _():
                racc[...] = jnp.zeros_like(racc)

            @pl.when(j == N_CHIP - 1)
            def _():
                dx, sq, dg = _loss_tile(acc[...], gf_ref[...], t_ref[...])
                x2_ref[...] = dx
                racc[0] += sq
                racc[1] += dg

            @pl.when((i == n - 1) & (j == N_CHIP - 1))
            def _():
                red_ref[pl.ds(0, 1), :] = jnp.sum(racc[0], axis=0, keepdims=True) * (0.5 / D)
                red_ref[pl.ds(1, 1), :] = jnp.sum(racc[1], axis=0, keepdims=True)

    tok = pl.BlockSpec((tm, D), lambda i, j: (i, 0))
    row = pl.BlockSpec((1, D), lambda i, j: (0, 0))
    cm = pl.BlockSpec((None, tm, fs), lambda i, j: (j, i, 0))
    wsp = pl.BlockSpec((None, fs, D), lambda i, j: (j, 0, 0))
    in_specs, operands = [tok, row, wsp, wsp, wsp], [x1, g_row, wg, wu, wd]
    out_specs = [tok, cm, cm, tok]
    out_shape = [jax.ShapeDtypeStruct((T, D), BF16), jax.ShapeDtypeStruct((N_CHIP, T, fs), BF16),
                 jax.ShapeDtypeStruct((N_CHIP, T, fs), BF16), jax.ShapeDtypeStruct((T, D), F32)]
    scratch = [pltpu.VMEM((tm, D), F32)]
    if head is not None:
        in_specs += [row, pl.BlockSpec((tm, D), lambda i, j: (i, 0), pipeline_mode=pl.Buffered(1))]
        operands += list(head)
        out_specs.append(pl.BlockSpec((2, D), lambda i, j: (0, 0)))
        out_shape.append(jax.ShapeDtypeStruct((2, D), F32))
        scratch.append(pltpu.VMEM((2, SUBLANES, D), F32))
    return _call(
        body, comm, name=f"ffn_fwd_{layer}", grid=(n, N_CHIP), in_specs=in_specs, out_specs=out_specs, out_shape=out_shape,
        scratch_shapes=scratch, compiler_params=_params("arbitrary", "arbitrary"),
    )(*operands)


def _ffn_bwd(dx2, x1, g_row, gg, uu, wg, wu, wd, layer, tm, comm=None):
    T, D = dx2.shape
    fs = wg.shape[-2]
    n = T // tm
    nt = (((1,), (1,)), ((), ()))

    def body(dx_ref, x_ref, g_ref, gg_ref, uu_ref, wg_ref, wu_ref, wd_ref, dg_ref, du_ref, act_ref, dx1_ref, dxb_ref, red_ref,
             acc, racc):
        i = pl.program_id(0)
        j = pl.program_id(1)

        @pl.when((i == 0) & (j == 0))
        def _():
            racc[...] = jnp.zeros_like(racc)

        @pl.when(j == 0)
        def _():
            dxb_ref[...] = dx_ref[...].astype(BF16)
            acc[...] = jnp.zeros_like(acc)

        dact = lax.dot_general(dxb_ref[...], wd_ref[j], nt, preferred_element_type=F32)
        g = gg_ref[...].astype(F32)
        u = uu_ref[...].astype(F32)
        s = _sigmoid(g)
        silu = g * s
        dg = (dact * u * (s * (1.0 + g * (1.0 - s)))).astype(BF16)
        du = (dact * silu).astype(BF16)
        dg_ref[...] = dg
        du_ref[...] = du
        act_ref[...] = (silu * u).astype(BF16)
        acc[...] += (jnp.dot(dg, wg_ref[j], preferred_element_type=F32)
                     + jnp.dot(du, wu_ref[j], preferred_element_type=F32))

        @pl.when(j == N_CHIP - 1)
        def _():
            dx, dgain = _rms_bwd(acc[...], x_ref[...], g_ref[...])
            dx1_ref[...] = dx_ref[...] + dx
            racc[...] += _colsum8(dgain)

        @pl.when((i == n - 1) & (j == N_CHIP - 1))
        def _():
            red_ref[...] = jnp.sum(racc[...], axis=0, keepdims=True)

    tok = pl.BlockSpec((tm, D), lambda i, j: (i, 0))
    cm = pl.BlockSpec((None, tm, fs), lambda i, j: (j, i, 0))
    cms = jax.ShapeDtypeStruct((N_CHIP, T, fs), BF16)
    resident = pl.BlockSpec((N_CHIP, fs, D), lambda i, j: (0, 0, 0), pipeline_mode=pl.Buffered(1))
    return _call(
        body, comm, name=f"ffn_bwd_{layer}", grid=(n, N_CHIP),
        in_specs=[tok, tok, pl.BlockSpec((1, D), lambda i, j: (0, 0)), cm, cm, resident, resident, resident],
        out_specs=[cm, cm, cm, tok, tok, pl.BlockSpec((1, D), lambda i, j: (0, 0))],
        out_shape=[cms, cms, cms, jax.ShapeDtypeStruct((T, D), F32), jax.ShapeDtypeStruct((T, D), BF16),
                   jax.ShapeDtypeStruct((1, D), F32)],
        scratch_shapes=[pltpu.VMEM((tm, D), F32), pltpu.VMEM((SUBLANES, D), F32)],
        compiler_params=_params("arbitrary", "arbitrary"),
    )(dx2, x1, g_row, gg, uu, wg, wu, wd)


def _merge_bwd(dx1, p, oa, ob, woa, wob, wo, gbias, layer, tm, comm=None):
    T, D = dx1.shape
    n = T // tm
    nt = (((1,), (1,)), ((), ()))

    def body(dx_ref, ga_ref, gb_ref, oa_ref, ob_ref, woa_ref, wob_ref, wo_ref, bias_ref,
             dya_ref, dyb_ref, doa_ref, dob_ref, dgl_ref, dxb_ref, red_ref, racc):
        i = pl.program_id(0)

        @pl.when(i == 0)
        def _():
            racc[...] = jnp.zeros_like(racc)

        dxb = dx_ref[...].astype(BF16)
        dxb_ref[...] = dxb
        dm = lax.dot_general(dxb, wo_ref[...], nt, preferred_element_type=F32)
        sa = _sigmoid(ga_ref[...].astype(F32) + bias_ref[pl.ds(0, 1), :])
        sb = _sigmoid(gb_ref[...].astype(F32) + bias_ref[pl.ds(1, 1), :])
        doa = (dm * sa).astype(BF16)
        dob = (dm * sb).astype(BF16)
        dga = dm * oa_ref[...].astype(F32) * (sa * (1.0 - sa))
        dgb = dm * ob_ref[...].astype(F32) * (sb * (1.0 - sb))
        doa_ref[...] = doa
        dob_ref[...] = dob
        dgl_ref[:, 0:D] = dga.astype(BF16)
        dgl_ref[:, D:2 * D] = dgb.astype(BF16)
        racc[0] += _colsum8(dga)
        racc[1] += _colsum8(dgb)
        dya_ref[...] = lax.dot_general(doa, woa_ref[...], nt, preferred_element_type=F32).astype(BF16)
        dyb_ref[...] = lax.dot_general(dob, wob_ref[...], nt, preferred_element_type=F32).astype(BF16)

        @pl.when(i == n - 1)
        def _():
            red_ref[pl.ds(0, 1), :] = jnp.sum(racc[0], axis=0, keepdims=True)
            red_ref[pl.ds(1, 1), :] = jnp.sum(racc[1], axis=0, keepdims=True)

    tok = pl.BlockSpec((tm, D), lambda i: (i, 0))
    wsp = pl.BlockSpec((D, D), lambda i: (0, 0))
    bf = jax.ShapeDtypeStruct((T, D), BF16)
    return _call(
        body, comm, name=f"merge_bwd_{layer}", grid=(n,),
        in_specs=[tok, pl.BlockSpec((tm, D), lambda i: (i, 5)), pl.BlockSpec((tm, D), lambda i: (i, 6)), tok, tok,
                  wsp, wsp, wsp, pl.BlockSpec(gbias.shape, lambda i: (0, 0))],
        out_specs=[tok, tok, tok, tok, pl.BlockSpec((tm, 2 * D), lambda i: (i, 0)), tok, pl.BlockSpec((2, D), lambda i: (0, 0))],
        out_shape=[bf, bf, bf, bf, jax.ShapeDtypeStruct((T, 2 * D), BF16), bf, jax.ShapeDtypeStruct((2, D), F32)],
        scratch_shapes=[pltpu.VMEM((2, SUBLANES, D), F32)],
        compiler_params=_params("arbitrary"),
    )(dx1, p, p, oa, ob, woa, wob, wo, gbias)


N_MIXER_RED = 16


def _mixer_bwd(p, hseq, dya, dyb, dgl, cw, wab, wabt, lam_row, layer, tm, comm=None):
    T = p.shape[0]
    D = p.shape[1] // 7
    n = T // tm
    ngroups = tm // SUBLANES
    nb = wab.shape[0]
    hb = 16
    tn = (((0,), (0,)), ((), ()))

    def body(ba_ref, ca_ref, xa_ref, xb_ref, gb_ref, h_ref, dya_ref, dyb_ref, dgl_ref,
             cap_ref, xap_ref, xbp_ref, hp_ref, ban_ref, dyan_ref,
             cw_ref, wab_ref, wabt_ref, lam_ref,
             dp_ref, red_ref, dwab_ref,
             racc, wacc, anext, gnext, dunext, c_s, g_s):
        i = pl.program_id(0)
        first_tile = i == n - 1
        last_tile = i == 0

        @pl.when(i == 0)
        def _():
            racc[...] = jnp.zeros_like(racc)
            wacc[...] = jnp.zeros_like(wacc)
            anext[...] = jnp.zeros_like(anext)
            gnext[...] = jnp.zeros_like(gnext)
            dunext[...] = jnp.zeros_like(dunext)

        keep_prev = jnp.where(first_tile, 0.0, 1.0)
        keep_next = jnp.where(last_tile, 0.0, 1.0)
        ba = ba_ref[...].astype(F32)
        ca = ca_ref[...].astype(F32)
        xa = xa_ref[...].astype(F32)
        xb = xb_ref[...].astype(F32)
        h = h_ref[...]
        dya = dya_ref[...].astype(F32)
        dyb = dyb_ref[...].astype(F32)
        zprev8 = (cap_ref[...].astype(F32) * xap_ref[...].astype(F32))[hb - SUBLANES:] * keep_prev
        xbprev8 = xbp_ref[...].astype(F32)[hb - SUBLANES:] * keep_prev
        hprev8 = hp_ref[...] * keep_prev
        dcznext8 = (dyan_ref[...].astype(F32) * ban_ref[...].astype(F32))[:SUBLANES] * keep_next

        lam = lam_ref[...]
        sp = _softplus_neg(lam)
        c = _mixer_recompute(ca, xa, xb, zprev8, xbprev8, cw_ref, wab_ref, sp)
        row = lambda k: cw_ref[pl.ds(k, 1), :]
        a, m, r, gi, u = c["a"], c["m"], c["r"], c["gi"], c["u"]

        gelu, dgelu = _gelu_parts(gb_ref[...].astype(F32))
        dgb = dyb * h * dgelu
        c_s[...], g_s[...] = _group_scan(_shift_up(a, 1, anext[...]), dyb * gelu, reverse=True)

        def step(k, carry):
            off = pl.multiple_of((ngroups - 1 - k) * SUBLANES, SUBLANES)
            gg = g_s[pl.ds(off, SUBLANES), :] + c_s[pl.ds(off, SUBLANES), :] * carry
            g_s[pl.ds(off, SUBLANES), :] = gg
            return jnp.broadcast_to(gg[0:1, :], gg.shape)

        gnext[...] = lax.fori_loop(0, ngroups, step, gnext[...], unroll=4)
        anext[...] = a[0:SUBLANES]
        g = g_s[...]

        hprev = _shift_down(h, 1, hprev8)
        da = g * hprev
        gm = g * m
        dgi = gm * u
        du = gm * gi
        dmv = g * gi * u
        dla = a * (da - dmv * a / m)
        dra = dla * ((-LRU_C) * sp) * (r * (1.0 - r))
        dix = dgi * (gi * (1.0 - gi))
        racc[10] += _colsum8(dla * r)
        racc[8] += _colsum8(dra)
        racc[9] += _colsum8(dix)
        drab = dra.astype(BF16)
        dixb = dix.astype(BF16)
        ub = c["ub"]
        dus = []
        for b in range(nb):
            sl = slice(b * LRU_BLOCK, (b + 1) * LRU_BLOCK)
            dri = jnp.concatenate([drab[:, sl], dixb[:, sl]], axis=1)
            dus.append(jnp.dot(dri, wabt_ref[b], preferred_element_type=F32))
            wacc[b] += lax.dot_general(ub[:, sl], dri, tn, preferred_element_type=F32)
        du = du + jnp.concatenate(dus, axis=1)

        dun = dunext[...]
        du1 = _shift_up(du, 1, dun)
        du2 = _shift_up(du, 2, dun)
        du3 = _shift_up(du, 3, dun)
        dxb = row(6) * du + row(5) * du1 + row(4) * du2 + row(3) * du3
        dunext[...] = du[0:SUBLANES]
        racc[6] += _colsum8(du * xb)
        racc[5] += _colsum8(du * c["x1"])
        racc[4] += _colsum8(du * c["x2"])
        racc[3] += _colsum8(du * c["x3"])
        racc[7] += _colsum8(du)

        dba = dya * c["cz"]
        dcz = dya * ba
        dcz1 = _shift_up(dcz, 1, dcznext8)
        dcz2 = _shift_up(dcz, 2, dcznext8)
        dz = row(2) * dcz + row(1) * dcz1 + row(0) * dcz2
        racc[2] += _colsum8(dcz * c["z"])
        racc[1] += _colsum8(dcz * c["z1"])
        racc[0] += _colsum8(dcz * c["z2"])

        dp_ref[:, 0:D] = dba.astype(BF16)
        dp_ref[:, D:2 * D] = (dz * xa).astype(BF16)
        dp_ref[:, 2 * D:3 * D] = (dz * ca).astype(BF16)
        dp_ref[:, 3 * D:4 * D] = dxb.astype(BF16)
        dp_ref[:, 4 * D:5 * D] = dgb.astype(BF16)
        dp_ref[:, 5 * D:7 * D] = dgl_ref[...]

        @pl.when(i == n - 1)
        def _():
            dlam_scale = LRU_C * _sigmoid(-lam)
            for k in range(N_MIXER_RED):
                tot = jnp.sum(racc[k], axis=0, keepdims=True)
                red_ref[pl.ds(k, 1), :] = tot * dlam_scale if k == 10 else tot
            dwab_ref[...] = wacc[...]

    rt = lambda i: n - 1 - i
    col = lambda k: pl.BlockSpec((tm, D), lambda i: (rt(i), k))
    tok = pl.BlockSpec((tm, D), lambda i: (rt(i), 0))
    full = lambda a: pl.BlockSpec(a.shape, lambda i: (0,) * a.ndim)
    prev16 = lambda k: pl.BlockSpec((hb, D), lambda i: (jnp.maximum(rt(i) * (tm // hb) - 1, 0), k))
    next16 = lambda k: pl.BlockSpec((hb, D), lambda i: (jnp.minimum((rt(i) + 1) * (tm // hb), T // hb - 1), k))
    hprev = pl.BlockSpec((SUBLANES, D), lambda i: (jnp.maximum(rt(i) * ngroups - 1, 0), 0))
    return _call(
        body, comm, name=f"mixer_bwd_{layer}", grid=(n,),
        in_specs=[col(0), col(1), col(2), col(3), col(4), tok, tok, tok, pl.BlockSpec((tm, 2 * D), lambda i: (rt(i), 0)),
                  prev16(1), prev16(2), prev16(3), hprev, next16(0), next16(0),
                  full(cw), full(wab), full(wabt), full(lam_row)],
        out_specs=[pl.BlockSpec((tm, 7 * D), lambda i: (rt(i), 0)),
                   pl.BlockSpec((N_MIXER_RED, D), lambda i: (0, 0)),
                   pl.BlockSpec((nb, LRU_BLOCK, 2 * LRU_BLOCK), lambda i: (0, 0, 0))],
        out_shape=[jax.ShapeDtypeStruct((T, 7 * D), BF16), jax.ShapeDtypeStruct((N_MIXER_RED, D), F32),
                   jax.ShapeDtypeStruct((nb, LRU_BLOCK, 2 * LRU_BLOCK), F32)],
        scratch_shapes=[pltpu.VMEM((N_MIXER_RED, SUBLANES, D), F32), pltpu.VMEM((nb, LRU_BLOCK, 2 * LRU_BLOCK), F32),
                        pltpu.VMEM((SUBLANES, D), F32), pltpu.VMEM((SUBLANES, D), F32), pltpu.VMEM((SUBLANES, D), F32),
                        pltpu.VMEM((tm, D), F32), pltpu.VMEM((tm, D), F32)],
        compiler_params=_params("arbitrary"),
    )(p, p, p, p, p, hseq, dya, dyb, dgl, p, p, p, hseq, p, dya, cw, wab, wabt, lam_row)


def _inproj_bwd(dp, dx1, x, g_row, win, layer, tm, comm=None):
    T, D = x.shape
    ns = win.shape[-1]
    n = T // tm
    nt = (((1,), (1,)), ((), ()))

    def body(dp_ref, dx_ref, x_ref, g_ref, w_ref, dx0_ref, red_ref, acc, racc):
        i = pl.program_id(0)
        j = pl.program_id(1)

        @pl.when((i == 0) & (j == 0))
        def _():
            racc[...] = jnp.zeros_like(racc)

        @pl.when(j == 0)
        def _():
            acc[...] = jnp.zeros_like(acc)

        acc[...] += lax.dot_general(dp_ref[...], w_ref[...], nt, preferred_element_type=F32)

        @pl.when(j == N_CHIP - 1)
        def _():
            dx, dgain = _rms_bwd(acc[...], x_ref[...], g_ref[...])
            dx0_ref[...] = dx_ref[...] + dx
            racc[...] += _colsum8(dgain)

        @pl.when((i == n - 1) & (j == N_CHIP - 1))
        def _():
            red_ref[...] = jnp.sum(racc[...], axis=0, keepdims=True)

    tok = pl.BlockSpec((tm, D), lambda i, j: (i, 0))
    return _call(
        body, comm, name=f"inproj_bwd_{layer}", grid=(n, N_CHIP),
        in_specs=[pl.BlockSpec((tm, ns), lambda i, j: (i, j)), tok, tok, pl.BlockSpec((1, D), lambda i, j: (0, 0)),
                  pl.BlockSpec((None, D, ns), lambda i, j: (j, 0, 0))],
        out_specs=[tok, pl.BlockSpec((1, D), lambda i, j: (0, 0))],
        out_shape=[jax.ShapeDtypeStruct((T, D), F32), jax.ShapeDtypeStruct((1, D), F32)],
        scratch_shapes=[pltpu.VMEM((tm, D), F32), pltpu.VMEM((SUBLANES, D), F32)],
        compiler_params=_params("arbitrary", "arbitrary"),
    )(dp, dx1, x, g_row, win)


def _wgrad(a, b, name, tk, a_kind="whole", b_kind="whole", nj=1, comm=None):
    T = a.shape[-2]
    width = lambda v, kind: v.shape[-1] // nj if kind == "cols" else v.shape[-1]
    ka, kb = width(a, a_kind), width(b, b_kind)
    nt = T // tk
    tn = (((0,), (0,)), ((), ()))

    def spec(k, kind):
        if kind == "cm":
            return pl.BlockSpec((None, tk, k), lambda j, t: (j, t, 0))
        if kind == "cols":
            return pl.BlockSpec((tk, k), lambda j, t: (t, j))
        return pl.BlockSpec((tk, k), lambda j, t: (t, 0))

    def body(a_ref, b_ref, o_ref, ob_ref):
        t = pl.program_id(1)

        @pl.when(t == 0)
        def _():
            o_ref[...] = jnp.zeros_like(o_ref)

        o_ref[...] += lax.dot_general(a_ref[...], b_ref[...], tn, preferred_element_type=F32)

        @pl.when(t == nt - 1)
        def _():
            ob_ref[...] = o_ref[...].astype(BF16)

    o_spec = pl.BlockSpec((None, ka, kb), lambda j, t: (j, 0, 0))
    return _call(
        body, comm, name=name, grid=(nj, nt),
        in_specs=[spec(ka, a_kind), spec(kb, b_kind)], out_specs=[o_spec, o_spec],
        out_shape=[jax.ShapeDtypeStruct((nj, ka, kb), F32), jax.ShapeDtypeStruct((nj, ka, kb), BF16)],
        compiler_params=_params("parallel", "arbitrary"),
    )(a, b)


def _wgrad_pair(a, b1, b2, name, tk, comm=None):
    T, ka = a.shape
    nj, _, kb = b1.shape
    nt = T // tk
    tn = (((0,), (0,)), ((), ()))

    def body(a_ref, b1_ref, b2_ref, o1_ref, o1b_ref, o2_ref, o2b_ref):
        t = pl.program_id(1)

        @pl.when(t == 0)
        def _():
            o1_ref[...] = jnp.zeros_like(o1_ref)
            o2_ref[...] = jnp.zeros_like(o2_ref)

        av = a_ref[...]
        o1_ref[...] += lax.dot_general(b1_ref[...], av, tn, preferred_element_type=F32)
        o2_ref[...] += lax.dot_general(b2_ref[...], av, tn, preferred_element_type=F32)

        @pl.when(t == nt - 1)
        def _():
            o1b_ref[...] = o1_ref[...].astype(BF16)
            o2b_ref[...] = o2_ref[...].astype(BF16)

    b_spec = pl.BlockSpec((None, tk, kb), lambda j, t: (j, t, 0))
    o_spec = pl.BlockSpec((None, kb, ka), lambda j, t: (j, 0, 0))
    f32 = jax.ShapeDtypeStruct((nj, kb, ka), F32)
    b16 = jax.ShapeDtypeStruct((nj, kb, ka), BF16)
    return _call(
        body, comm, name=name, grid=(nj, nt),
        in_specs=[pl.BlockSpec((tk, ka), lambda j, t: (t, 0)), b_spec, b_spec], out_specs=[o_spec] * 4,
        out_shape=[f32, b16, f32, b16], compiler_params=_params("parallel", "arbitrary"),
    )(a, b1, b2)


def _block_diag(w):
    hb = LRU_BLOCK // LRU_HEAD_DIM
    nb = w.shape[0] // hb
    w4 = w.reshape(nb, hb, LRU_HEAD_DIM, LRU_HEAD_DIM)
    eye = jnp.eye(hb, dtype=w.dtype)
    return jnp.einsum("bide,ij->bidje", w4, eye).reshape(nb, LRU_BLOCK, LRU_BLOCK)


def _diag_heads(m):
    hb = LRU_BLOCK // LRU_HEAD_DIM
    nb = m.shape[0]
    m5 = m.reshape(nb, hb, LRU_HEAD_DIM, hb, LRU_HEAD_DIM)
    eye = jnp.eye(hb, dtype=m.dtype)
    return jnp.einsum("bidje,ij->bide", m5, eye).reshape(nb * hb, LRU_HEAD_DIM, LRU_HEAD_DIM)


def _tiles(T):
    cap = lambda n: min(n, T)
    return dict(inproj=cap(1024), mixer=cap(256), merge=cap(512), ffn=cap(1024), ffn_bwd=cap(512), inproj_bwd=cap(1024),
                wgrad_in=cap(2048), wgrad=cap(2048))


class _NoSchedule:
    def carry(self, name):
        return None

    def after(self, name):
        pass

    def grad(self, key, layer, f32, b16):
        pass


def _local_step(x, target, W, small, tiles, sched):
    L = small["ln1_g"].shape[0]
    D = x.shape[1]
    square = lambda a: a.reshape(D, D)
    saved = []
    h = x
    for l in range(L):
        cw = jnp.concatenate([small["conv_a_w"][l], small["conv_b_w"][l], small["conv_b_b"][l][None],
                              small["lru_ba"][l][None], small["lru_bx"][l][None]], axis=0)
        wab = jnp.concatenate([_block_diag(small["lru_wa"][l]), _block_diag(small["lru_wx"][l])], axis=2).astype(BF16)
        wabt = jnp.swapaxes(wab, 1, 2)
        lam_row = small["lru_lambda"][l][None]
        ln1_row = small["ln1_g"][l][None]
        ln2_row = small["ln2_g"][l][None]
        p, h1 = _rms_inproj(h, ln1_row, W["win", l], l, tiles["inproj"], sched.carry(f"rms_inproj_{l}"))
        ya, yb, hseq = _mixer_fwd(p, cw, wab, lam_row, l, tiles["mixer"], sched.carry(f"mixer_fwd_{l}"))
        oa, ob, mg, x1 = _merge_fwd(h, p, ya, yb, square(W["woa", l]), square(W["wob", l]), square(W["wo", l]),
                                    small["gate_bias"][l], l, tiles["merge"], sched.carry(f"merge_fwd_{l}"))
        head = (small["final_g"][None], target) if l == L - 1 else None
        h2, gg, uu, x2, *red = _ffn_fwd(x1, ln2_row, W["wg", l], W["wu", l], W["wd", l], l, tiles["ffn"],
                                        sched.carry(f"ffn_fwd_{l}"), head)
        saved.append(dict(x0=h, p=p, h1=h1, ya=ya, yb=yb, hseq=hseq, oa=oa, ob=ob, mg=mg, x1=x1, h2=h2, gg=gg, uu=uu,
                          cw=cw, wab=wab, wabt=wabt, lam_row=lam_row, ln1_row=ln1_row, ln2_row=ln2_row))
        h = x2

    dx = h
    loss_row, d_final_g = red[0][0], red[0][1]

    gsmall = {k: [None] * L for k in ("ln1_g", "ln2_g", "conv_a_w", "conv_b_w", "conv_b_b", "lru_wa", "lru_ba", "lru_wx",
                                      "lru_bx", "lru_lambda", "gate_bias")}
    tk = tiles["wgrad"]
    for l in reversed(range(L)):
        s = saved[l]
        dgg, duu, act, dx1, dx2b, dln2 = _ffn_bwd(dx, s["x1"], s["ln2_row"], s["gg"], s["uu"], W["wg", l], W["wu", l], W["wd", l],
                                                 l, tiles["ffn_bwd"], sched.carry(f"ffn_bwd_{l}"))
        sched.after(f"ffn_bwd_{l}")
        gate_up = _wgrad_pair(s["h2"], dgg, duu, f"wgrad_ffn_gate_up_{l}", tk, sched.carry(f"wgrad_ffn_gate_up_{l}"))
        sched.grad("wg", l, *gate_up[0:2])
        sched.grad("wu", l, *gate_up[2:4])
        sched.after(f"wgrad_ffn_gate_up_{l}")
        sched.grad("wd", l, *_wgrad(act, dx2b, f"wgrad_ffn_down_{l}", tk, "cm", "whole", N_CHIP, sched.carry(f"wgrad_ffn_down_{l}")))
        dya, dyb, doa, dob, dgl, dx1b, dgbias = _merge_bwd(dx1, s["p"], s["oa"], s["ob"], square(W["woa", l]), square(W["wob", l]),
                                                         square(W["wo", l]), small["gate_bias"][l], l, tiles["merge"],
                                                         sched.carry(f"merge_bwd_{l}"))
        sched.after(f"merge_bwd_{l}")
        sched.grad("wo", l, *_wgrad(s["mg"], dx1b, f"wgrad_w_o_{l}", tk))
        sched.grad("woa", l, *_wgrad(s["ya"], doa, f"wgrad_w_out_a_{l}", tk))
        sched.grad("wob", l, *_wgrad(s["yb"], dob, f"wgrad_w_out_b_{l}", tk))
        dp, mred, dwab = _mixer_bwd(s["p"], s["hseq"], dya, dyb, dgl, s["cw"], s["wab"], s["wabt"], s["lam_row"], l,
                                    tiles["mixer"], sched.carry(f"mixer_bwd_{l}"))
        sched.after(f"mixer_bwd_{l}")
        dx, dln1 = _inproj_bwd(dp, dx1, s["x0"], s["ln1_row"], W["win", l], l, tiles["inproj_bwd"], sched.carry(f"inproj_bwd_{l}"))
        sched.after(f"inproj_bwd_{l}")
        sched.grad("win", l, *_wgrad(s["h1"], dp, f"wgrad_w_in_{l}", tiles["wgrad_in"], "whole", "cols", N_CHIP,
                                     sched.carry(f"wgrad_w_in_{l}")))
        sched.after(f"wgrad_w_in_{l}")
        gsmall["ln1_g"][l] = dln1[0]
        gsmall["ln2_g"][l] = dln2[0]
        gsmall["conv_a_w"][l] = mred[0:CONV_A_K]
        gsmall["conv_b_w"][l] = mred[CONV_A_K:CONV_A_K + CONV_B_K]
        gsmall["conv_b_b"][l] = mred[7]
        gsmall["lru_ba"][l] = mred[8]
        gsmall["lru_bx"][l] = mred[9]
        gsmall["lru_lambda"][l] = mred[10]
        gsmall["lru_wa"][l] = _diag_heads(dwab[:, :, :LRU_BLOCK])
        gsmall["lru_wx"][l] = _diag_heads(dwab[:, :, LRU_BLOCK:])
        gsmall["gate_bias"][l] = dgbias
    gsmall = {k: jnp.stack(v) for k, v in gsmall.items()}
    gsmall["final_g"] = d_final_g
    return loss_row, dx, gsmall


def _small_allreduce(buf, comm):
    R, C = buf.shape
    n_dev = 8
    rp = R // n_dev
    rel = [(k >> 2 & 1, k >> 1 & 1, k & 1) for k in range(1, n_dev)]
    assert not comm.io
    n_ro, n_nw = len(comm.ro), len(comm.nw)

    def body(in_ref, *rest):
        ro, out_ref, nw = rest[:n_ro], rest[n_ro], rest[n_ro + 1:n_ro + 1 + n_nw]
        recv, s1, r1, s2, r2, cs, cr = rest[n_ro + 1 + n_nw:]
        comm.start(ro, [], nw, cs, cr)
        x, y, c, _ = _place()
        flip = lambda v, bit: 1 - v if bit else v
        peers = [(flip(x, kx), flip(y, ky), flip(c, kc)) for kx, ky, kc in rel]
        dev = lambda p: 4 * p[0] + 2 * p[1] + p[2]
        part = lambda ref, d: ref.at[pl.ds(pl.multiple_of(d * rp, SUBLANES), rp), :]
        me = dev((x, y, c))

        def scatter(k, src_dev, to):
            return pltpu.make_async_remote_copy(src_ref=part(in_ref, dev(to)), dst_ref=recv.at[src_dev], send_sem=s1.at[k],
                                                recv_sem=r1.at[k], device_id=to, device_id_type=MESH)

        def gather(k, src_dev, to):
            return pltpu.make_async_remote_copy(src_ref=part(out_ref, src_dev), dst_ref=part(out_ref, src_dev), send_sem=s2.at[k],
                                                recv_sem=r2.at[k], device_id=to, device_id_type=MESH)

        first = [scatter(k, me, p) for k, p in enumerate(peers)]
        for cp in first:
            cp.start()
        recv[me] = part(in_ref, me)[...]
        for k, p in enumerate(peers):
            scatter(k, dev(p), (x, y, c)).wait_recv()
        total = recv[0]
        for d in range(1, n_dev):
            total = total + recv[d]
        part(out_ref, me)[...] = total
        second = [gather(k, me, p) for k, p in enumerate(peers)]
        for cp in second:
            cp.start()
        for k, p in enumerate(peers):
            gather(k, dev(p), (x, y, c)).wait_recv()
        for cp in first + second:
            cp.wait_send()
        comm.finish(ro, [], nw, cs, cr)

    dma = pltpu.SemaphoreType.DMA
    vm = pl.BlockSpec(memory_space=pltpu.VMEM)
    res = pl.pallas_call(
        body, name="small_allreduce", out_shape=[jax.ShapeDtypeStruct((R, C), buf.dtype)] + comm.out_shapes(),
        in_specs=[vm] + [ANY] * n_ro, out_specs=[vm] + [ANY] * n_nw,
        scratch_shapes=[pltpu.VMEM((n_dev, rp, C), buf.dtype), dma((n_dev - 1,)), dma((n_dev - 1,)), dma((n_dev - 1,)), dma((n_dev - 1,)),
                        dma((comm.n,)), dma((comm.n,))],
    )(buf, *comm.operands())
    comm.keep(res[1:])
    return res[0]


ELEMENTWISE_BLOCK_BYTES = 2 * 1024 * 1024


def _row_block(k, n):
    best = None
    for b in range(16, k + 1, 16):
        if k % b == 0 and b * n * 4 <= ELEMENTWISE_BLOCK_BYTES:
            best = b
    return best or k


def _add_halves(g, recv, place_arr, name):
    nj, hk, N = recv.shape
    bk = _row_block(hk, N)
    nb = hk // bk

    def body(k_ref, g_ref, r_ref, o_ref, ob_ref):
        s = g_ref[...] + r_ref[...].astype(F32)
        ob_ref[...] = s.astype(BF16)

        @pl.when(pl.program_id(1) == k_ref[0])
        def _():
            o_ref[...] = s

    blk = pl.BlockSpec((None, bk, N), lambda i, j, k_ref: (j, i, 0))
    grid_spec = pltpu.PrefetchScalarGridSpec(
        num_scalar_prefetch=1, grid=(nb, nj),
        in_specs=[pl.BlockSpec((None, bk, N), lambda i, j, k_ref: (j, k_ref[1] * nb + i, 0)), blk],
        out_specs=[pl.BlockSpec((bk, N), lambda i, j, k_ref: (i, 0)), blk])
    return pl.pallas_call(
        body, name=name, grid_spec=grid_spec,
        out_shape=[jax.ShapeDtypeStruct((hk, N), F32), jax.ShapeDtypeStruct((nj, hk, N), BF16)],
        compiler_params=_params("parallel", "arbitrary"),
    )(place_arr, g, recv)


def _add_chips(pc, recv, place_arr, layer, n_layers, prev, name):
    hk, N = pc.shape
    bk = _row_block(hk, N)
    nb = hk // bk

    def body(k_ref, p_ref, r0_ref, r1_ref, r2_ref, *rest):
        o_ref = rest[-1]
        o_ref[...] = ((p_ref[...] + r0_ref[...].astype(F32)) + r1_ref[...].astype(F32)) + r2_ref[...].astype(F32)

    rspec = lambda j: pl.BlockSpec((None, bk, N), lambda i, k_ref: (j, i, 0))
    in_specs = [pl.BlockSpec((bk, N), lambda i, k_ref: (i, 0)), rspec(0), rspec(1), rspec(2)]
    operands = [pc, recv, recv, recv]
    aliases = {}
    if prev is not None:
        in_specs.append(ANY)
        operands.append(prev)
        aliases = {5: 0}
    grid_spec = pltpu.PrefetchScalarGridSpec(
        num_scalar_prefetch=1, grid=(nb,), in_specs=in_specs,
        out_specs=pl.BlockSpec((None, bk, N), lambda i, k_ref: (layer, k_ref[1] * nb + i, 0)))
    return pl.pallas_call(
        body, name=name, grid_spec=grid_spec, out_shape=jax.ShapeDtypeStruct((n_layers, 2 * hk, N), F32),
        input_output_aliases=aliases, compiler_params=_params("parallel"),
    )(place_arr, *operands)


def _adamw_math(w, g, m, v):
    m = ADAM_B1 * m + (1.0 - ADAM_B1) * g
    v = ADAM_B2 * v + (1.0 - ADAM_B2) * (g * g)
    m_hat = m / (1.0 - ADAM_B1 ** ADAM_STEP)
    v_hat = v / (1.0 - ADAM_B2 ** ADAM_STEP)
    delta = -ADAM_LR * (m_hat / (jnp.sqrt(v_hat) + ADAM_EPS) + ADAM_WD * w)
    return delta, m, v


def _adamw(w, g, m, v, name):
    L, K, N = w.shape
    bk = _row_block(K, N)

    def body(w_ref, g_ref, m_ref, v_ref, d_ref, nm_ref, nv_ref):
        d_ref[...], nm_ref[...], nv_ref[...] = _adamw_math(w_ref[...], g_ref[...], m_ref[...], v_ref[...])

    blk = pl.BlockSpec((None, bk, N), lambda l, i: (l, i, 0))
    sds = jax.ShapeDtypeStruct((L, K, N), F32)
    return pl.pallas_call(
        body, name=name, grid=(L, K // bk), in_specs=[blk] * 4, out_specs=[blk] * 3, out_shape=[sds] * 3,
        compiler_params=_params("parallel", "parallel"),
    )(w, g, m, v)


def _adamw_small(ws, gs, ms, vs):
    n = len(ws)

    def body(*refs):
        w, g, m, v, d, nm, nv = (refs[k * n:(k + 1) * n] for k in range(7))
        for k in range(n):
            d[k][...], nm[k][...], nv[k][...] = _adamw_math(w[k][...], g[k][...], m[k][...], v[k][...])

    sds = [jax.ShapeDtypeStruct(a.shape, F32) for a in ws]
    out = pl.pallas_call(body, name="adamw_small", out_shape=sds * 3)(*ws, *gs, *ms, *vs)
    return out[:n], out[n:2 * n], out[2 * n:]


def _cast_bf16(w, layer, place_arr, name):
    _, K, N = w.shape
    bk = _row_block(K, N)

    def body(k_ref, w_ref, o_ref):
        o_ref[...] = w_ref[...].astype(BF16)

    grid_spec = pltpu.PrefetchScalarGridSpec(
        num_scalar_prefetch=1, grid=(K // bk,),
        in_specs=[pl.BlockSpec((None, bk, N), lambda i, k_ref: (layer, i, 0))],
        out_specs=pl.BlockSpec((None, bk, N), lambda i, k_ref: (k_ref[0], i, 0)))
    return pl.pallas_call(
        body, name=name, grid_spec=grid_spec, out_shape=jax.ShapeDtypeStruct((N_CHIP, K, N), BF16),
        compiler_params=_params("parallel"),
    )(place_arr, w)


BIG = ("w_in", "w_out_a", "w_out_b", "w_o", "w_ffn_gate", "w_ffn_up", "w_ffn_down")
BIG_KEY = dict(w_in="win", w_out_a="woa", w_out_b="wob", w_o="wo", w_ffn_gate="wg", w_ffn_up="wu", w_ffn_down="wd")
SHARDED_SMALL = ("conv_a_w", "conv_b_w", "gate_bias")
REPLICATED = ("ln1_g", "conv_b_b", "lru_wa", "lru_ba", "lru_wx", "lru_bx", "lru_lambda", "ln2_g", "final_g")
WEIGHTS = ("ln1_g", "w_in", "conv_a_w", "conv_b_w", "conv_b_b", "lru_wa", "lru_ba", "lru_wx", "lru_bx", "lru_lambda",
           "w_out_a", "w_out_b", "gate_bias", "w_o", "ln2_g", "w_ffn_gate", "w_ffn_up", "w_ffn_down", "final_g")
LANES = 1024


def _pack_rows(arrays, row_multiple):
    flat = jnp.concatenate([a.reshape(-1) for a in arrays])
    rows = -(-flat.shape[0] // LANES)
    rows = -(-rows // row_multiple) * row_multiple
    flat = jnp.pad(flat, (0, rows * LANES - flat.shape[0]))
    return flat.reshape(rows, LANES)


def _unpack_rows(buf, shapes):
    flat = buf.reshape(-1)
    out, off = [], 0
    for s in shapes:
        n = 1
        for d in s:
            n *= d
        out.append(flat[off:off + n].reshape(s))
        off += n
    return out


OUT_KEYS = ("wo", "woa", "wob")
FFN_KEYS = ("wg", "wu", "wd")


def _items(keys, layer):
    return [(k, layer) for k in keys]


CARRY = {
    "rms_inproj_0": [("gather_ici", _items(OUT_KEYS + FFN_KEYS, 0))],
    "mixer_fwd_0": [("gather_d2d", _items(OUT_KEYS + FFN_KEYS, 0)), ("gather_ici", _items(("win",) + OUT_KEYS, 1))],
    "merge_fwd_0": [("gather_d2d", _items(("win",) + OUT_KEYS, 1))],
    "ffn_fwd_0": [("gather_ici", _items(FFN_KEYS, 1))],
    "rms_inproj_1": [("gather_d2d", _items(FFN_KEYS, 1))],
    "merge_bwd_1": [("halves", _items(FFN_KEYS, 1))],
    "mixer_bwd_1": [("chips", _items(FFN_KEYS, 1)), ("halves", _items(OUT_KEYS, 1))],
    "inproj_bwd_1": [("chips", _items(OUT_KEYS, 1)), ("share", _items(FFN_KEYS, 1))],
    "ffn_bwd_0": [("halves", [("win", 1)]), ("share", _items(OUT_KEYS, 1))],
    "wgrad_ffn_gate_up_0": [("chips", [("win", 1)])],
    "wgrad_ffn_down_0": [("share", [("win", 1)])],
    "merge_bwd_0": [("halves", _items(FFN_KEYS, 0))],
    "mixer_bwd_0": [("chips", _items(FFN_KEYS, 0)), ("halves", _items(OUT_KEYS, 0))],
    "wgrad_w_in_0": [("chips", _items(OUT_KEYS, 0)), ("share", _items(FFN_KEYS, 0))],
}
AFTER = {
    "merge_bwd_1": [("add_halves", _items(FFN_KEYS, 1))],
    "mixer_bwd_1": [("add_chips", _items(FFN_KEYS, 1)), ("add_halves", _items(OUT_KEYS, 1))],
    "inproj_bwd_1": [("add_chips", _items(OUT_KEYS, 1))],
    "ffn_bwd_0": [("add_halves", [("win", 1)])],
    "wgrad_ffn_gate_up_0": [("add_chips", [("win", 1)])],
    "merge_bwd_0": [("add_halves", _items(FFN_KEYS, 0))],
    "mixer_bwd_0": [("add_chips", _items(FFN_KEYS, 0)), ("add_halves", _items(OUT_KEYS, 0))],
    "wgrad_w_in_0": [("add_chips", _items(OUT_KEYS, 0))],
}


class _Schedule:
    def __init__(self, slots, place_arr, n_layers):
        self.W = slots
        self.place, self.L = place_arr, n_layers
        self.g32, self.g16 = {}, {}
        self.from_sibling, self.chip_sum, self.chip_sum16, self.from_chips = {}, {}, {}, {}
        self.reduced = {}

    def stage(self, comm, kind, items):
        bf = lambda shape: jax.ShapeDtypeStruct(shape, BF16)
        for it in items:
            if kind == "gather_ici":
                comm.add(_gather_ici, 3, io=[(self.W, it)])
            elif kind == "gather_d2d":
                comm.add(_gather_d2d, 3, io=[(self.W, it)])
            elif kind == "halves":
                nj, K, N = self.g16[it].shape
                comm.add(_reduce_halves, 1, ro=[self.g16[it]], nw=[(self.from_sibling, it, bf((nj, K // 2, N)))])
            elif kind == "chips":
                _, hk, N = self.chip_sum16[it].shape
                comm.add(_reduce_chips, 3, ro=[self.chip_sum16[it]], nw=[(self.from_chips, it, bf((3, hk, N)))])
            elif kind == "share":
                comm.add(_reduce_share(it[1]), 1, io=[(self.reduced, it[0])])
        return comm

    def carry(self, name):
        comm = _Carried()
        for kind, items in CARRY.get(name, ()):
            self.stage(comm, kind, items)
        return comm

    def run(self, name, rounds):
        _run_comm([self.stage(_Carried(), kind, items) for kind, items in rounds], name)

    def grad(self, key, layer, f32, b16):
        by_chip = lambda g: g.reshape(N_CHIP, -1, g.shape[-1])
        self.g32[key, layer], self.g16[key, layer] = by_chip(f32), by_chip(b16)

    def add(self, kind, items):
        for key, layer in items:
            it = (key, layer)
            if kind == "add_halves":
                self.chip_sum[it], self.chip_sum16[it] = _add_halves(self.g32[it], self.from_sibling[it], self.place,
                                                                    f"add_halves_{key}_{layer}")
            else:
                self.reduced[key] = _add_chips(self.chip_sum[it], self.from_chips[it], self.place, layer, self.L,
                                               self.reduced.get(key), f"add_chips_{key}_{layer}")

    def after(self, name):
        for kind, items in AFTER.get(name, ()):
            self.add(kind, items)


def _step(w, m, v, x, target):
    xi, yi, ci = lax.axis_index("x"), lax.axis_index("y"), lax.axis_index("c")
    chip = _chip_id(xi, yi)
    place_arr = jnp.stack([chip, ci]).astype(jnp.int32)
    L = w["ln1_g"].shape[0]
    assert L == 2
    D = x.shape[1]
    dc = D // N_CHIP

    stored = lambda n, a: jnp.swapaxes(a, 1, 2) if n in ("w_ffn_gate", "w_ffn_up") else a
    slots = {(BIG_KEY[n], l): _cast_bf16(stored(n, w[n]), l, place_arr, f"cast_{n}_{l}") for n in BIG for l in range(L)}
    sched = _Schedule(slots, place_arr, L)
    small_shard = jnp.concatenate([w[n] for n in SHARDED_SMALL], axis=1)
    got = {}
    first = sched.stage(_Carried(), "gather_ici", [("win", 0)])
    first.add(_gather_small, 3, ro=[small_shard], nw=[(got, "small", jax.ShapeDtypeStruct((3,) + small_shard.shape, F32))])
    _run_comm([first, sched.stage(_Carried(), "gather_d2d", [("win", 0)])], "gather_first")
    small_g = jnp.zeros((N_CHIP,) + small_shard.shape, F32)
    small_g = lax.dynamic_update_index_in_dim(small_g, small_shard, chip, 0)
    for j, (cx, cy) in enumerate([(1 - xi, yi), (xi, 1 - yi), (1 - xi, 1 - yi)]):
        small_g = lax.dynamic_update_index_in_dim(small_g, got["small"][j], _chip_id(cx, cy), 0)
    small_full = jnp.transpose(small_g, (1, 2, 0, 3)).reshape(L, small_shard.shape[1], D)
    small = {n: w[n] for n in REPLICATED}
    off = 0
    for n in SHARDED_SMALL:
        k = w[n].shape[1]
        small[n] = small_full[:, off:off + k]
        off += k

    loss_row, grad_x, gsmall = _local_step(x, target, sched.W, small, _tiles(x.shape[0]), sched)

    sched.run("reduce_tail_halves", [("halves", [("win", 0)]), ("share", _items(OUT_KEYS, 0))])
    sched.add("add_halves", [("win", 0)])
    order = [n for n in WEIGHTS if n not in BIG]
    packed = _pack_rows([gsmall[n] for n in order] + [loss_row], 8 * SUBLANES)
    summed = _small_allreduce(packed, sched.stage(_Carried(), "chips", [("win", 0)]))
    sched.add("add_chips", [("win", 0)])
    sched.run("reduce_tail_share", [("share", [("win", 0)])])
    grads = {n: stored(n, sched.reduced[BIG_KEY[n]]) for n in BIG}
    parts = _unpack_rows(summed, [gsmall[n].shape for n in order] + [loss_row.shape])
    loss = jnp.sum(parts[-1])
    for n, g in zip(order, parts[:-1]):
        grads[n] = lax.dynamic_slice_in_dim(g, chip * dc, dc, axis=2) if n in SHARDED_SMALL else g

    delta, new_m, new_v = {}, {}, {}
    for n in BIG:
        d, nm, nv = _adamw(stored(n, w[n]), sched.reduced[BIG_KEY[n]], stored(n, m[n]), stored(n, v[n]), f"adamw_{n}")
        delta[n], new_m[n], new_v[n] = stored(n, d), stored(n, nm), stored(n, nv)
    for d, arrays in zip((delta, new_m, new_v), _adamw_small(*([d[n] for n in order] for d in (w, grads, m, v)))):
        d.update(zip(order, arrays))
    return loss, grad_x, grads, delta, new_m, new_v


def kernel(x, ln1_g, w_in, conv_a_w, conv_b_w, conv_b_b, lru_wa, lru_ba, lru_wx, lru_bx, lru_lambda, w_out_a, w_out_b, gate_bias, w_o, ln2_g, w_ffn_gate, w_ffn_up, w_ffn_down, final_g, loss_target, m_ln1_g, m_w_in, m_conv_a_w, m_conv_b_w, m_conv_b_b, m_lru_wa, m_lru_ba, m_lru_wx, m_lru_bx, m_lru_lambda, m_w_out_a, m_w_out_b, m_gate_bias, m_w_o, m_ln2_g, m_w_ffn_gate, m_w_ffn_up, m_w_ffn_down, m_final_g, v_ln1_g, v_w_in, v_conv_a_w, v_conv_b_w, v_conv_b_b, v_lru_wa, v_lru_ba, v_lru_wx, v_lru_bx, v_lru_lambda, v_w_out_a, v_w_out_b, v_gate_bias, v_w_o, v_ln2_g, v_w_ffn_gate, v_w_ffn_up, v_w_ffn_down, v_final_g):
    w = dict(ln1_g=ln1_g, w_in=w_in, conv_a_w=conv_a_w, conv_b_w=conv_b_w, conv_b_b=conv_b_b, lru_wa=lru_wa, lru_ba=lru_ba,
             lru_wx=lru_wx, lru_bx=lru_bx, lru_lambda=lru_lambda, w_out_a=w_out_a, w_out_b=w_out_b, gate_bias=gate_bias, w_o=w_o,
             ln2_g=ln2_g, w_ffn_gate=w_ffn_gate, w_ffn_up=w_ffn_up, w_ffn_down=w_ffn_down, final_g=final_g)
    m = dict(ln1_g=m_ln1_g, w_in=m_w_in, conv_a_w=m_conv_a_w, conv_b_w=m_conv_b_w, conv_b_b=m_conv_b_b, lru_wa=m_lru_wa,
             lru_ba=m_lru_ba, lru_wx=m_lru_wx, lru_bx=m_lru_bx, lru_lambda=m_lru_lambda, w_out_a=m_w_out_a, w_out_b=m_w_out_b,
             gate_bias=m_gate_bias, w_o=m_w_o, ln2_g=m_ln2_g, w_ffn_gate=m_w_ffn_gate, w_ffn_up=m_w_ffn_up,
             w_ffn_down=m_w_ffn_down, final_g=m_final_g)
    v = dict(ln1_g=v_ln1_g, w_in=v_w_in, conv_a_w=v_conv_a_w, conv_b_w=v_conv_b_w, conv_b_b=v_conv_b_b, lru_wa=v_lru_wa,
             lru_ba=v_lru_ba, lru_wx=v_lru_wx, lru_bx=v_lru_bx, lru_lambda=v_lru_lambda, w_out_a=v_w_out_a, w_out_b=v_w_out_b,
             gate_bias=v_gate_bias, w_o=v_w_o, ln2_g=v_ln2_g, w_ffn_gate=v_w_ffn_gate, w_ffn_up=v_w_ffn_up,
             w_ffn_down=v_w_ffn_down, final_g=v_final_g)
    loss, grad_x, grads, delta, new_m, new_v = _step(w, m, v, x[0], loss_target[0])
    return (loss, grad_x[None], *[grads[n] for n in WEIGHTS], *[delta[n] for n in WEIGHTS],
            *[new_m[n] for n in WEIGHTS], *[new_v[n] for n in WEIGHTS])
```

```python
import functools

import jax
import jax.numpy as jnp
from jax import lax
from jax.experimental import pallas as pl
from jax.experimental.pallas import tpu as pltpu

F32 = jnp.float32
BF16 = jnp.bfloat16
MESH = pl.DeviceIdType.MESH

N_CHIP = 4
RMS_EPS = 1e-6
LRU_C = 8.0
LRU_HEAD_DIM = 64
LRU_BLOCK = 256
CONV_A_K = 3
CONV_B_K = 4
ADAM_LR = 0.001
ADAM_B1 = 0.9
ADAM_B2 = 0.999
ADAM_EPS = 1e-08
ADAM_WD = 0.01
ADAM_STEP = 10
SUBLANES = 8
VMEM_LIMIT = 56 * 1024 * 1024


def _params(*sem):
    return pltpu.CompilerParams(dimension_semantics=sem, vmem_limit_bytes=VMEM_LIMIT)


def _sigmoid(v):
    return 1.0 / (1.0 + jnp.exp(-v))


def _one_minus_sq(la, a):
    return jnp.tanh(-la) * (1.0 + a * a)


def _gelu_parts(v):
    k = 0.7978845608028654
    v2 = v * v
    t = jnp.tanh(k * (v + 0.044715 * v * v2))
    gelu = 0.5 * v * (1.0 + t)
    dgelu = 0.5 * (1.0 + t) + 0.5 * v * (1.0 - t * t) * k * (1.0 + 3 * 0.044715 * v2)
    return gelu, dgelu


def _shift_down(v, k, prev8):
    rolled = pltpu.roll(v, k, 0)
    r8 = lax.broadcasted_iota(jnp.int32, prev8.shape, 0)
    head = jnp.where(r8 < k, pltpu.roll(prev8, k, 0), rolled[0:SUBLANES])
    return jnp.concatenate([head, rolled[SUBLANES:]], axis=0)


def _shift_up(v, k, next8):
    tm = v.shape[0]
    rolled = pltpu.roll(v, tm - k, 0)
    r8 = lax.broadcasted_iota(jnp.int32, next8.shape, 0)
    tail = jnp.where(r8 >= SUBLANES - k, pltpu.roll(next8, SUBLANES - k, 0), rolled[tm - SUBLANES:])
    return jnp.concatenate([rolled[:tm - SUBLANES], tail], axis=0)


def _group_scan(a, b, reverse):
    tm, c = a.shape
    a = a.reshape(tm // SUBLANES, SUBLANES, c)
    b = b.reshape(tm // SUBLANES, SUBLANES, c)
    q = lax.broadcasted_iota(jnp.int32, a.shape, 1)
    for s in (1, 2, 4):
        msk = q < SUBLANES - s if reverse else q >= s
        shift = SUBLANES - s if reverse else s
        b = jnp.where(msk, a * pltpu.roll(b, shift, 1) + b, b)
        a = jnp.where(msk, a * pltpu.roll(a, shift, 1), a)
    return a.reshape(tm, c), b.reshape(tm, c)


def _colsum8(v):
    tm, c = v.shape
    return jnp.sum(v.reshape(tm // SUBLANES, SUBLANES, c), axis=0)


def _rms_stats(xv):
    var = jnp.mean(xv * xv, axis=-1, keepdims=True)
    return lax.rsqrt(var + RMS_EPS)


def _rms_bwd(dh, xv, g):
    rstd = _rms_stats(xv)
    xhat = xv * rstd
    dxhat = dh * g
    dx = rstd * (dxhat - xhat * jnp.mean(dxhat * xhat, axis=-1, keepdims=True))
    return dx, dh * xhat


ANY = pl.BlockSpec(memory_space=pl.ANY)


def _place():
    x, y, c = lax.axis_index("x"), lax.axis_index("y"), lax.axis_index("c")
    other_chips = [(1 - x, y), (x, 1 - y), (1 - x, 1 - y)]
    return x, y, c, other_chips


def _chip_id(x, y):
    return 2 * x + y


def _half(c, hk):
    return pl.ds(pl.multiple_of(c * hk, 16), hk)


def _remote(src, dst, to, sems):
    return pltpu.make_async_remote_copy(src_ref=src, dst_ref=dst, device_id=to, device_id_type=MESH, **sems)


class _Carried:
    def __init__(self):
        self.ro, self.io, self.nw, self.parts, self.n = [], [], [], [], 0

    def add(self, maker, n, ro=(), io=(), nw=()):
        def index(items, item, same):
            for k, other in enumerate(items):
                if same(other, item):
                    return k
            items.append(item)
            return len(items) - 1

        r = [index(self.ro, a, lambda p, q: p is q) for a in ro]
        i = [index(self.io, a, lambda p, q: p[0] is q[0] and p[1] == q[1]) for a in io]
        w = [index(self.nw, a, lambda p, q: False) for a in nw]
        self.parts.append((maker, r, i, w, self.n))
        self.n += n
        return self

    def pairs(self, ro, io, nw, ssem, rsem):
        out = []
        for maker, r, i, w, base in self.parts:
            sems = lambda k, base=base: dict(send_sem=ssem.at[base + k], recv_sem=rsem.at[base + k])
            out += maker([ro[k] for k in r], [io[k] for k in i], [nw[k] for k in w], sems)
        return out

    def start(self, *refs):
        for send, _ in self.pairs(*refs):
            send.start()

    def finish(self, *refs):
        pairs = self.pairs(*refs)
        for _, recv in pairs:
            recv.wait_recv()
        for send, _ in pairs:
            send.wait_send()

    def operands(self):
        return list(self.ro) + [store[key] for store, key in self.io]

    def out_shapes(self):
        return [jax.ShapeDtypeStruct(store[key].shape, store[key].dtype) for store, key in self.io] + [s for _, _, s in self.nw]

    def keep(self, results):
        for (store, key), arr in zip(self.io, results[:len(self.io)]):
            store[key] = arr
        for (store, key, _), arr in zip(self.nw, results[len(self.io):]):
            store[key] = arr


def _call(body, comm, *, name, grid, in_specs, out_specs, out_shape, compiler_params, scratch_shapes=(), aliases=None):
    aliases = dict(aliases or {})
    if comm is None or not comm.parts:
        return pl.pallas_call(body, name=name, grid=grid, in_specs=in_specs, out_specs=out_specs, out_shape=out_shape,
                              scratch_shapes=list(scratch_shapes), input_output_aliases=aliases, compiler_params=compiler_params)
    n_in, n_out, n_scr = len(in_specs), len(out_shape), len(scratch_shapes)
    n_ro, n_io, n_nw = len(comm.ro), len(comm.io), len(comm.nw)

    def carried(*refs):
        base_in = refs[:n_in]
        ro = refs[n_in:n_in + n_ro]
        pos = n_in + n_ro + n_io
        base_out = refs[pos:pos + n_out]
        io = refs[pos + n_out:pos + n_out + n_io]
        nw = refs[pos + n_out + n_io:pos + n_out + n_io + n_nw]
        pos += n_out + n_io + n_nw
        scr = refs[pos:pos + n_scr]
        ssem, rsem = refs[pos + n_scr], refs[pos + n_scr + 1]
        first = pl.program_id(0) == 0
        last = pl.program_id(0) == grid[0] - 1
        for axis in range(1, len(grid)):
            first = first & (pl.program_id(axis) == 0)
            last = last & (pl.program_id(axis) == grid[axis] - 1)

        @pl.when(first)
        def _():
            comm.start(ro, io, nw, ssem, rsem)

        body(*base_in, *base_out, *scr)

        @pl.when(last)
        def _():
            comm.finish(ro, io, nw, ssem, rsem)

    aliases.update({n_in + n_ro + k: n_out + k for k in range(n_io)})
    dma = pltpu.SemaphoreType.DMA
    call = pl.pallas_call(
        carried, name=name, grid=grid,
        in_specs=list(in_specs) + [ANY] * (n_ro + n_io), out_specs=list(out_specs) + [ANY] * (n_io + n_nw),
        out_shape=list(out_shape) + comm.out_shapes(), input_output_aliases=aliases,
        scratch_shapes=list(scratch_shapes) + [dma((comm.n,)), dma((comm.n,))], compiler_params=compiler_params)

    def run(*operands):
        res = call(*operands, *comm.operands())
        comm.keep(res[n_out:])
        return res[:n_out]

    return run


def _run_comm(rounds, name):
    ro, io, nw, uses = [], [], [], []
    for r in rounds:
        def index(items, item, same):
            for k, other in enumerate(items):
                if same(other, item):
                    return k
            items.append(item)
            return len(items) - 1
        uses.append(([index(ro, a, lambda p, q: p is q) for a in r.ro],
                     [index(io, a, lambda p, q: p[0] is q[0] and p[1] == q[1]) for a in r.io],
                     [index(nw, a, lambda p, q: False) for a in r.nw]))
    n_ro, n_io, n_nw = len(ro), len(io), len(nw)

    def body(*refs):
        ro_refs = refs[:n_ro]
        io_refs = refs[n_ro + n_io:n_ro + 2 * n_io]
        nw_refs = refs[n_ro + 2 * n_io:n_ro + 2 * n_io + n_nw]
        sems = refs[n_ro + 2 * n_io + n_nw:]
        for k, (r, (a, b, c)) in enumerate(zip(rounds, uses)):
            args = ([ro_refs[i] for i in a], [io_refs[i] for i in b], [nw_refs[i] for i in c], sems[2 * k], sems[2 * k + 1])
            r.start(*args)
            r.finish(*args)

    operands = ro + [store[key] for store, key in io]
    out_shape = [jax.ShapeDtypeStruct(store[key].shape, store[key].dtype) for store, key in io] + [s for _, _, s in nw]
    dma = pltpu.SemaphoreType.DMA
    res = pl.pallas_call(
        body, name=name, out_shape=out_shape,
        in_specs=[ANY] * (n_ro + n_io), out_specs=[ANY] * (n_io + n_nw),
        input_output_aliases={n_ro + k: k for k in range(n_io)},
        scratch_shapes=[dma((r.n,)) for r in rounds for _ in range(2)],
    )(*operands)
    for (store, key), arr in zip(io, res[:n_io]):
        store[key] = arr
    for (store, key, _), arr in zip(nw, res[n_io:]):
        store[key] = arr


def _gather_ici(ro, io, nw, sems):
    s = io[0]
    x, y, c, chips = _place()
    hk = s.shape[1] // 2
    mine = s.at[_chip_id(x, y), _half(c, hk)]
    pairs = []
    for j, (cx, cy) in enumerate(chips):
        theirs = s.at[_chip_id(cx, cy), _half(c, hk)]
        pairs.append((_remote(mine, mine, (cx, cy, c), sems(j)), _remote(theirs, theirs, (cx, cy, c), sems(j))))
    return pairs


def _gather_d2d(ro, io, nw, sems):
    s = io[0]
    x, y, c, chips = _place()
    hk = s.shape[1] // 2
    sib = (x, y, 1 - c)
    pairs = []
    for j, (cx, cy) in enumerate(chips):
        here = s.at[_chip_id(cx, cy), _half(c, hk)]
        there = s.at[_chip_id(cx, cy), _half(1 - c, hk)]
        pairs.append((_remote(here, here, sib, sems(j)), _remote(there, there, sib, sems(j))))
    return pairs


def _gather_small(ro, io, nw, sems):
    x, y, c, chips = _place()
    return [(_remote(ro[0], nw[0].at[j], (cx, cy, c), sems(j)),) * 2 for j, (cx, cy) in enumerate(chips)]


def _reduce_halves(ro, io, nw, sems):
    x, y, c, _ = _place()
    g = ro[0]
    hk = g.shape[1] // 2
    sib = (x, y, 1 - c)
    return [(_remote(g.at[:, _half(1 - c, hk)], nw[0], sib, sems(0)), _remote(g.at[:, _half(c, hk)], nw[0], sib, sems(0)))]


def _reduce_chips(ro, io, nw, sems):
    x, y, c, chips = _place()
    return [(_remote(ro[0].at[_chip_id(cx, cy)], nw[0].at[j], (cx, cy, c), sems(j)),) * 2 for j, (cx, cy) in enumerate(chips)]


def _reduce_share(layer):
    def maker(ro, io, nw, sems):
        g = io[0]
        x, y, c, _ = _place()
        hk = g.shape[1] // 2
        sib = (x, y, 1 - c)
        mine, theirs = g.at[layer, _half(c, hk)], g.at[layer, _half(1 - c, hk)]
        return [(_remote(mine, mine, sib, sems(0)), _remote(theirs, theirs, sib, sems(0)))]
    return maker


def _rms_inproj(x, g_row, win, layer, tm, comm=None):
    T, D = x.shape
    ns = win.shape[-1]

    def body(x_ref, g_ref, w_ref, p_ref, h_ref):
        @pl.when(pl.program_id(1) == 0)
        def _():
            xv = x_ref[...]
            h_ref[...] = (xv * _rms_stats(xv) * g_ref[...]).astype(BF16)
        p_ref[...] = jnp.dot(h_ref[...], w_ref[...], preferred_element_type=F32).astype(BF16)

    return _call(
        body, comm, name=f"rms_inproj_{layer}", grid=(T // tm, N_CHIP),
        in_specs=[pl.BlockSpec((tm, D), lambda i, j: (i, 0)),
                  pl.BlockSpec((1, D), lambda i, j: (0, 0)),
                  pl.BlockSpec((None, D, ns), lambda i, j: (j, 0, 0))],
        out_specs=[pl.BlockSpec((tm, ns), lambda i, j: (i, j)),
                   pl.BlockSpec((tm, D), lambda i, j: (i, 0))],
        out_shape=[jax.ShapeDtypeStruct((T, N_CHIP * ns), BF16), jax.ShapeDtypeStruct((T, D), BF16)],
        compiler_params=_params("parallel", "arbitrary"),
    )(x, g_row, win)


def _mixer_recompute(ca, xa, xb, zprev8, xbprev8, cw_ref, wab_ref, sp):
    row = lambda k: cw_ref[pl.ds(k, 1), :]
    z = ca * xa
    z1 = _shift_down(z, 1, zprev8)
    z2 = _shift_down(z, 2, zprev8)
    cz = row(2) * z + row(1) * z1 + row(0) * z2
    x1 = _shift_down(xb, 1, xbprev8)
    x2 = _shift_down(xb, 2, xbprev8)
    x3 = _shift_down(xb, 3, xbprev8)
    u = row(6) * xb + row(5) * x1 + row(4) * x2 + row(3) * x3 + row(7)
    ub = u.astype(BF16)
    nb = wab_ref.shape[0]
    ras, ixs = [], []
    for b in range(nb):
        ri = jnp.dot(ub[:, b * LRU_BLOCK:(b + 1) * LRU_BLOCK], wab_ref[b], preferred_element_type=F32)
        ras.append(ri[:, :LRU_BLOCK])
        ixs.append(ri[:, LRU_BLOCK:])
    r = _sigmoid(jnp.concatenate(ras, axis=1) + row(8))
    gi = _sigmoid(jnp.concatenate(ixs, axis=1) + row(9))
    la = (-LRU_C) * r * sp
    a = jnp.exp(la)
    m = jnp.sqrt(_one_minus_sq(la, a))
    return dict(z=z, z1=z1, z2=z2, cz=cz, x1=x1, x2=x2, x3=x3, u=u, ub=ub, r=r, gi=gi, a=a, m=m)


def _softplus_neg(lam):
    v = -lam
    return jnp.maximum(v, 0.0) + jnp.log1p(jnp.exp(-jnp.abs(v)))


def _mixer_fwd(p, cw, wab, lam_row, layer, tm, comm=None):
    T = p.shape[0]
    D = p.shape[1] // 7
    ngroups = tm // SUBLANES

    def body(ba_ref, ca_ref, xa_ref, xb_ref, gb_ref, cw_ref, wab_ref, lam_ref, ya_ref, yb_ref, h_ref,
             zprev, xbprev, hcarry, a_s, h_s):
        @pl.when(pl.program_id(0) == 0)
        def _():
            zprev[...] = jnp.zeros_like(zprev)
            xbprev[...] = jnp.zeros_like(xbprev)
            hcarry[...] = jnp.zeros_like(hcarry)

        ca = ca_ref[...].astype(F32)
        xa = xa_ref[...].astype(F32)
        xb = xb_ref[...].astype(F32)
        sp = _softplus_neg(lam_ref[...])
        c = _mixer_recompute(ca, xa, xb, zprev[...], xbprev[...], cw_ref, wab_ref, sp)
        zprev[...] = c["z"][tm - SUBLANES:]
        xbprev[...] = xb[tm - SUBLANES:]
        ya_ref[...] = (ba_ref[...].astype(F32) * c["cz"]).astype(BF16)

        a_s[...], h_s[...] = _group_scan(c["a"], c["m"] * c["gi"] * c["u"], reverse=False)

        def step(g, carry):
            off = pl.multiple_of(g * SUBLANES, SUBLANES)
            hg = h_s[pl.ds(off, SUBLANES), :] + a_s[pl.ds(off, SUBLANES), :] * carry
            h_s[pl.ds(off, SUBLANES), :] = hg
            return jnp.broadcast_to(hg[SUBLANES - 1:SUBLANES, :], hg.shape)

        hcarry[...] = lax.fori_loop(0, ngroups, step, hcarry[...], unroll=4)
        h = h_s[...]
        h_ref[...] = h
        gelu, _ = _gelu_parts(gb_ref[...].astype(F32))
        yb_ref[...] = (h * gelu).astype(BF16)

    col = lambda k: pl.BlockSpec((tm, D), lambda i: (i, k))
    full = lambda a: pl.BlockSpec(a.shape, lambda i: (0,) * a.ndim)
    tok = pl.BlockSpec((tm, D), lambda i: (i, 0))
    return _call(
        body, comm, name=f"mixer_fwd_{layer}", grid=(T // tm,),
        in_specs=[col(0), col(1), col(2), col(3), col(4), full(cw), full(wab), full(lam_row)],
        out_specs=[tok, tok, tok],
        out_shape=[jax.ShapeDtypeStruct((T, D), BF16), jax.ShapeDtypeStruct((T, D), BF16), jax.ShapeDtypeStruct((T, D), F32)],
        scratch_shapes=[pltpu.VMEM((SUBLANES, D), F32), pltpu.VMEM((SUBLANES, D), F32), pltpu.VMEM((SUBLANES, D), F32),
                        pltpu.VMEM((tm, D), F32), pltpu.VMEM((tm, D), F32)],
        compiler_params=_params("arbitrary"),
    )(p, p, p, p, p, cw, wab, lam_row)


def _merge_fwd(x, p, ya, yb, woa, wob, wo, gbias, layer, tm, comm=None):
    T, D = x.shape

    def body(x_ref, ga_ref, gb_ref, ya_ref, yb_ref, woa_ref, wob_ref, wo_ref, bias_ref, oa_ref, ob_ref, mg_ref, x1_ref):
        oa = jnp.dot(ya_ref[...], woa_ref[...], preferred_element_type=F32)
        ob = jnp.dot(yb_ref[...], wob_ref[...], preferred_element_type=F32)
        sa = _sigmoid(ga_ref[...].astype(F32) + bias_ref[pl.ds(0, 1), :])
        sb = _sigmoid(gb_ref[...].astype(F32) + bias_ref[pl.ds(1, 1), :])
        mg = (sa * oa + sb * ob).astype(BF16)
        oa_ref[...] = oa.astype(BF16)
        ob_ref[...] = ob.astype(BF16)
        mg_ref[...] = mg
        x1_ref[...] = x_ref[...] + jnp.dot(mg, wo_ref[...], preferred_element_type=F32)

    tok = pl.BlockSpec((tm, D), lambda i: (i, 0))
    wsp = pl.BlockSpec((D, D), lambda i: (0, 0))
    bf = jax.ShapeDtypeStruct((T, D), BF16)
    return _call(
        body, comm, name=f"merge_fwd_{layer}", grid=(T // tm,),
        in_specs=[tok, pl.BlockSpec((tm, D), lambda i: (i, 5)), pl.BlockSpec((tm, D), lambda i: (i, 6)), tok, tok,
                  wsp, wsp, wsp, pl.BlockSpec(gbias.shape, lambda i: (0, 0))],
        out_specs=[tok, tok, tok, tok],
        out_shape=[bf, bf, bf, jax.ShapeDtypeStruct((T, D), F32)],
        compiler_params=_params("parallel"),
    )(x, p, p, ya, yb, woa, wob, wo, gbias)


def _loss_tile(xv, g, tgt):
    d = xv.shape[-1]
    rstd = _rms_stats(xv)
    xhat = xv * rstd
    err = xhat * g - tgt
    dy = err * (1.0 / d)
    dxhat = dy * g
    dx = rstd * (dxhat - xhat * jnp.mean(dxhat * xhat, axis=-1, keepdims=True))
    return dx, _colsum8(err * err), _colsum8(dy * xhat)


def _ffn_fwd(x1, g_row, wg, wu, wd, layer, tm, comm=None):
    T, D = x1.shape
    fs = wg.shape[-2]
    n = T // tm
    nt = (((1,), (1,)), ((), ()))

    def body(x_ref, g_ref, wg_ref, wu_ref, wd_ref, h_ref, gg_ref, uu_ref, x2_ref, acc):
        j = pl.program_id(1)

        @pl.when(j == 0)
        def _():
            xv = x_ref[...]
            h_ref[...] = (xv * _rms_stats(xv) * g_ref[...]).astype(BF16)
            acc[...] = xv

        h = h_ref[...]
        gg = lax.dot_general(h, wg_ref[...], nt, preferred_element_type=F32)
        uu = lax.dot_general(h, wu_ref[...], nt, preferred_element_type=F32)
        gg_ref[...] = gg.astype(BF16)
        uu_ref[...] = uu.astype(BF16)
        act = (gg * _sigmoid(gg) * uu).astype(BF16)
        acc[...] += jnp.dot(act, wd_ref[...], preferred_element_type=F32)

        @pl.when(j == N_CHIP - 1)
        def _():
            x2_ref[...] = acc[...]

    tok = pl.BlockSpec((tm, D), lambda i, j: (i, 0))
    cm = pl.BlockSpec((None, tm, fs), lambda i, j: (j, i, 0))
    wsp = pl.BlockSpec((None, fs, D), lambda i, j: (j, 0, 0))
    return _call(
        body, comm, name=f"ffn_fwd_{layer}", grid=(n, N_CHIP),
        in_specs=[tok, pl.BlockSpec((1, D), lambda i, j: (0, 0)), wsp, wsp, wsp], out_specs=[tok, cm, cm, tok],
        out_shape=[jax.ShapeDtypeStruct((T, D), BF16), jax.ShapeDtypeStruct((N_CHIP, T, fs), BF16),
                   jax.ShapeDtypeStruct((N_CHIP, T, fs), BF16), jax.ShapeDtypeStruct((T, D), F32)],
        scratch_shapes=[pltpu.VMEM((tm, D), F32)], compiler_params=_params("parallel", "arbitrary"),
    )(x1, g_row, wg, wu, wd)


def _final_loss(x, g_row, target, tm):
    T, D = x.shape
    n = T // tm

    def body(x_ref, g_ref, t_ref, dx_ref, red_ref, racc):
        i = pl.program_id(0)

        @pl.when(i == 0)
        def _():
            racc[...] = jnp.zeros_like(racc)

        dx_ref[...], sq, dg = _loss_tile(x_ref[...], g_ref[...], t_ref[...])
        racc[0] += sq
        racc[1] += dg

        @pl.when(i == n - 1)
        def _():
            red_ref[pl.ds(0, 1), :] = jnp.sum(racc[0], axis=0, keepdims=True) * (0.5 / D)
            red_ref[pl.ds(1, 1), :] = jnp.sum(racc[1], axis=0, keepdims=True)

    tok = pl.BlockSpec((tm, D), lambda i: (i, 0))
    return pl.pallas_call(
        body, name="final_loss", grid=(n,),
        in_specs=[tok, pl.BlockSpec((1, D), lambda i: (0, 0)), tok],
        out_specs=[tok, pl.BlockSpec((2, D), lambda i: (0, 0))],
        out_shape=[jax.ShapeDtypeStruct((T, D), F32), jax.ShapeDtypeStruct((2, D), F32)],
        scratch_shapes=[pltpu.VMEM((2, SUBLANES, D), F32)],
        compiler_params=_params("arbitrary"),
    )(x, g_row, target)


def _ffn_bwd(dx2, x1, g_row, gg, uu, wg, wu, wd, layer, tm, comm=None):
    T, D = dx2.shape
    fs = wg.shape[-2]
    n = T // tm
    nt = (((1,), (1,)), ((), ()))

    def body(dx_ref, x_ref, g_ref, gg_ref, uu_ref, wg_ref, wu_ref, wd_ref, dg_ref, du_ref, act_ref, dx1_ref, dxb_ref, red_ref,
             acc, racc):
        i = pl.program_id(0)
        j = pl.program_id(1)

        @pl.when((i == 0) & (j == 0))
        def _():
            racc[...] = jnp.zeros_like(racc)

        @pl.when(j == 0)
        def _():
            dxb_ref[...] = dx_ref[...].astype(BF16)
            acc[...] = jnp.zeros_like(acc)

        dact = lax.dot_general(dxb_ref[...], wd_ref[j], nt, preferred_element_type=F32)
        g = gg_ref[...].astype(F32)
        u = uu_ref[...].astype(F32)
        s = _sigmoid(g)
        silu = g * s
        dg = (dact * u * (s * (1.0 + g * (1.0 - s)))).astype(BF16)
        du = (dact * silu).astype(BF16)
        dg_ref[...] = dg
        du_ref[...] = du
        act_ref[...] = (silu * u).astype(BF16)
        acc[...] += (jnp.dot(dg, wg_ref[j], preferred_element_type=F32)
                     + jnp.dot(du, wu_ref[j], preferred_element_type=F32))

        @pl.when(j == N_CHIP - 1)
        def _():
            dx, dgain = _rms_bwd(acc[...], x_ref[...], g_ref[...])
            dx1_ref[...] = dx_ref[...] + dx
            racc[...] += _colsum8(dgain)

        @pl.when((i == n - 1) & (j == N_CHIP - 1))
        def _():
            red_ref[...] = jnp.sum(racc[...], axis=0, keepdims=True)

    tok = pl.BlockSpec((tm, D), lambda i, j: (i, 0))
    cm = pl.BlockSpec((None, tm, fs), lambda i, j: (j, i, 0))
    cms = jax.ShapeDtypeStruct((N_CHIP, T, fs), BF16)
    resident = pl.BlockSpec((N_CHIP, fs, D), lambda i, j: (0, 0, 0), pipeline_mode=pl.Buffered(1))
    return _call(
        body, comm, name=f"ffn_bwd_{layer}", grid=(n, N_CHIP),
        in_specs=[tok, tok, pl.BlockSpec((1, D), lambda i, j: (0, 0)), cm, cm, resident, resident, resident],
        out_specs=[cm, cm, cm, tok, tok, pl.BlockSpec((1, D), lambda i, j: (0, 0))],
        out_shape=[cms, cms, cms, jax.ShapeDtypeStruct((T, D), F32), jax.ShapeDtypeStruct((T, D), BF16),
                   jax.ShapeDtypeStruct((1, D), F32)],
        scratch_shapes=[pltpu.VMEM((tm, D), F32), pltpu.VMEM((SUBLANES, D), F32)],
        compiler_params=_params("arbitrary", "arbitrary"),
    )(dx2, x1, g_row, gg, uu, wg, wu, wd)


def _merge_bwd(dx1, p, oa, ob, woa, wob, wo, gbias, layer, tm, comm=None):
    T, D = dx1.shape
    n = T // tm
    nt = (((1,), (1,)), ((), ()))

    def body(dx_ref, ga_ref, gb_ref, oa_ref, ob_ref, woa_ref, wob_ref, wo_ref, bias_ref,
             dya_ref, dyb_ref, doa_ref, dob_ref, dgl_ref, dxb_ref, red_ref, racc):
        i = pl.program_id(0)

        @pl.when(i == 0)
        def _():
            racc[...] = jnp.zeros_like(racc)

        dxb = dx_ref[...].astype(BF16)
        dxb_ref[...] = dxb
        dm = lax.dot_general(dxb, wo_ref[...], nt, preferred_element_type=F32)
        sa = _sigmoid(ga_ref[...].astype(F32) + bias_ref[pl.ds(0, 1), :])
        sb = _sigmoid(gb_ref[...].astype(F32) + bias_ref[pl.ds(1, 1), :])
        doa = (dm * sa).astype(BF16)
        dob = (dm * sb).astype(BF16)
        dga = dm * oa_ref[...].astype(F32) * (sa * (1.0 - sa))
        dgb = dm * ob_ref[...].astype(F32) * (sb * (1.0 - sb))
        doa_ref[...] = doa
        dob_ref[...] = dob
        dgl_ref[:, 0:D] = dga.astype(BF16)
        dgl_ref[:, D:2 * D] = dgb.astype(BF16)
        racc[0] += _colsum8(dga)
        racc[1] += _colsum8(dgb)
        dya_ref[...] = lax.dot_general(doa, woa_ref[...], nt, preferred_element_type=F32).astype(BF16)
        dyb_ref[...] = lax.dot_general(dob, wob_ref[...], nt, preferred_element_type=F32).astype(BF16)

        @pl.when(i == n - 1)
        def _():
            red_ref[pl.ds(0, 1), :] = jnp.sum(racc[0], axis=0, keepdims=True)
            red_ref[pl.ds(1, 1), :] = jnp.sum(racc[1], axis=0, keepdims=True)

    tok = pl.BlockSpec((tm, D), lambda i: (i, 0))
    wsp = pl.BlockSpec((D, D), lambda i: (0, 0))
    bf = jax.ShapeDtypeStruct((T, D), BF16)
    return _call(
        body, comm, name=f"merge_bwd_{layer}", grid=(n,),
        in_specs=[tok, pl.BlockSpec((tm, D), lambda i: (i, 5)), pl.BlockSpec((tm, D), lambda i: (i, 6)), tok, tok,
                  wsp, wsp, wsp, pl.BlockSpec(gbias.shape, lambda i: (0, 0))],
        out_specs=[tok, tok, tok, tok, pl.BlockSpec((tm, 2 * D), lambda i: (i, 0)), tok, pl.BlockSpec((2, D), lambda i: (0, 0))],
        out_shape=[bf, bf, bf, bf, jax.ShapeDtypeStruct((T, 2 * D), BF16), bf, jax.ShapeDtypeStruct((2, D), F32)],
        scratch_shapes=[pltpu.VMEM((2, SUBLANES, D), F32)],
        compiler_params=_params("arbitrary"),
    )(dx1, p, p, oa, ob, woa, wob, wo, gbias)


N_MIXER_RED = 16


def _mixer_bwd(p, hseq, dya, dyb, dgl, cw, wab, wabt, lam_row, layer, tm, comm=None):
    T = p.shape[0]
    D = p.shape[1] // 7
    n = T // tm
    ngroups = tm // SUBLANES
    nb = wab.shape[0]
    hb = 16
    tn = (((0,), (0,)), ((), ()))

    def body(ba_ref, ca_ref, xa_ref, xb_ref, gb_ref, h_ref, dya_ref, dyb_ref, dgl_ref,
             cap_ref, xap_ref, xbp_ref, hp_ref, ban_ref, dyan_ref,
             cw_ref, wab_ref, wabt_ref, lam_ref,
             dp_ref, red_ref, dwab_ref,
             racc, wacc, anext, gnext, dunext, c_s, g_s):
        i = pl.program_id(0)
        first_tile = i == n - 1
        last_tile = i == 0

        @pl.when(i == 0)
        def _():
            racc[...] = jnp.zeros_like(racc)
            wacc[...] = jnp.zeros_like(wacc)
            anext[...] = jnp.zeros_like(anext)
            gnext[...] = jnp.zeros_like(gnext)
            dunext[...] = jnp.zeros_like(dunext)

        keep_prev = jnp.where(first_tile, 0.0, 1.0)
        keep_next = jnp.where(last_tile, 0.0, 1.0)
        ba = ba_ref[...].astype(F32)
        ca = ca_ref[...].astype(F32)
        xa = xa_ref[...].astype(F32)
        xb = xb_ref[...].astype(F32)
        h = h_ref[...]
        dya = dya_ref[...].astype(F32)
        dyb = dyb_ref[...].astype(F32)
        zprev8 = (cap_ref[...].astype(F32) * xap_ref[...].astype(F32))[hb - SUBLANES:] * keep_prev
        xbprev8 = xbp_ref[...].astype(F32)[hb - SUBLANES:] * keep_prev
        hprev8 = hp_ref[...] * keep_prev
        dcznext8 = (dyan_ref[...].astype(F32) * ban_ref[...].astype(F32))[:SUBLANES] * keep_next

        lam = lam_ref[...]
        sp = _softplus_neg(lam)
        c = _mixer_recompute(ca, xa, xb, zprev8, xbprev8, cw_ref, wab_ref, sp)
        row = lambda k: cw_ref[pl.ds(k, 1), :]
        a, m, r, gi, u = c["a"], c["m"], c["r"], c["gi"], c["u"]

        gelu, dgelu = _gelu_parts(gb_ref[...].astype(F32))
        dgb = dyb * h * dgelu
        c_s[...], g_s[...] = _group_scan(_shift_up(a, 1, anext[...]), dyb * gelu, reverse=True)

        def step(k, carry):
            off = pl.multiple_of((ngroups - 1 - k) * SUBLANES, SUBLANES)
            gg = g_s[pl.ds(off, SUBLANES), :] + c_s[pl.ds(off, SUBLANES), :] * carry
            g_s[pl.ds(off, SUBLANES), :] = gg
            return jnp.broadcast_to(gg[0:1, :], gg.shape)

        gnext[...] = lax.fori_loop(0, ngroups, step, gnext[...], unroll=4)
        anext[...] = a[0:SUBLANES]
        g = g_s[...]

        hprev = _shift_down(h, 1, hprev8)
        da = g * hprev
        gm = g * m
        dgi = gm * u
        du = gm * gi
        dmv = g * gi * u
        dla = a * (da - dmv * a / m)
        dra = dla * ((-LRU_C) * sp) * (r * (1.0 - r))
        dix = dgi * (gi * (1.0 - gi))
        racc[10] += _colsum8(dla * r)
        racc[8] += _colsum8(dra)
        racc[9] += _colsum8(dix)
        drab = dra.astype(BF16)
        dixb = dix.astype(BF16)
        ub = c["ub"]
        dus = []
        for b in range(nb):
            sl = slice(b * LRU_BLOCK, (b + 1) * LRU_BLOCK)
            dri = jnp.concatenate([drab[:, sl], dixb[:, sl]], axis=1)
            dus.append(jnp.dot(dri, wabt_ref[b], preferred_element_type=F32))
            wacc[b] += lax.dot_general(ub[:, sl], dri, tn, preferred_element_type=F32)
        du = du + jnp.concatenate(dus, axis=1)

        dun = dunext[...]
        du1 = _shift_up(du, 1, dun)
        du2 = _shift_up(du, 2, dun)
        du3 = _shift_up(du, 3, dun)
        dxb = row(6) * du + row(5) * du1 + row(4) * du2 + row(3) * du3
        dunext[...] = du[0:SUBLANES]
        racc[6] += _colsum8(du * xb)
        racc[5] += _colsum8(du * c["x1"])
        racc[4] += _colsum8(du * c["x2"])
        racc[3] += _colsum8(du * c["x3"])
        racc[7] += _colsum8(du)

        dba = dya * c["cz"]
        dcz = dya * ba
        dcz1 = _shift_up(dcz, 1, dcznext8)
        dcz2 = _shift_up(dcz, 2, dcznext8)
        dz = row(2) * dcz + row(1) * dcz1 + row(0) * dcz2
        racc[2] += _colsum8(dcz * c["z"])
        racc[1] += _colsum8(dcz * c["z1"])
        racc[0] += _colsum8(dcz * c["z2"])

        dp_ref[:, 0:D] = dba.astype(BF16)
        dp_ref[:, D:2 * D] = (dz * xa).astype(BF16)
        dp_ref[:, 2 * D:3 * D] = (dz * ca).astype(BF16)
        dp_ref[:, 3 * D:4 * D] = dxb.astype(BF16)
        dp_ref[:, 4 * D:5 * D] = dgb.astype(BF16)
        dp_ref[:, 5 * D:7 * D] = dgl_ref[...]

        @pl.when(i == n - 1)
        def _():
            dlam_scale = LRU_C * _sigmoid(-lam)
            for k in range(N_MIXER_RED):
                tot = jnp.sum(racc[k], axis=0, keepdims=True)
                red_ref[pl.ds(k, 1), :] = tot * dlam_scale if k == 10 else tot
            dwab_ref[...] = wacc[...]

    rt = lambda i: n - 1 - i
    col = lambda k: pl.BlockSpec((tm, D), lambda i: (rt(i), k))
    tok = pl.BlockSpec((tm, D), lambda i: (rt(i), 0))
    full = lambda a: pl.BlockSpec(a.shape, lambda i: (0,) * a.ndim)
    prev16 = lambda k: pl.BlockSpec((hb, D), lambda i: (jnp.maximum(rt(i) * (tm // hb) - 1, 0), k))
    next16 = lambda k: pl.BlockSpec((hb, D), lambda i: (jnp.minimum((rt(i) + 1) * (tm // hb), T // hb - 1), k))
    hprev = pl.BlockSpec((SUBLANES, D), lambda i: (jnp.maximum(rt(i) * ngroups - 1, 0), 0))
    return _call(
        body, comm, name=f"mixer_bwd_{layer}", grid=(n,),
        in_specs=[col(0), col(1), col(2), col(3), col(4), tok, tok, tok, pl.BlockSpec((tm, 2 * D), lambda i: (rt(i), 0)),
                  prev16(1), prev16(2), prev16(3), hprev, next16(0), next16(0),
                  full(cw), full(wab), full(wabt), full(lam_row)],
        out_specs=[pl.BlockSpec((tm, 7 * D), lambda i: (rt(i), 0)),
                   pl.BlockSpec((N_MIXER_RED, D), lambda i: (0, 0)),
                   pl.BlockSpec((nb, LRU_BLOCK, 2 * LRU_BLOCK), lambda i: (0, 0, 0))],
        out_shape=[jax.ShapeDtypeStruct((T, 7 * D), BF16), jax.ShapeDtypeStruct((N_MIXER_RED, D), F32),
                   jax.ShapeDtypeStruct((nb, LRU_BLOCK, 2 * LRU_BLOCK), F32)],
        scratch_shapes=[pltpu.VMEM((N_MIXER_RED, SUBLANES, D), F32), pltpu.VMEM((nb, LRU_BLOCK, 2 * LRU_BLOCK), F32),
                        pltpu.VMEM((SUBLANES, D), F32), pltpu.VMEM((SUBLANES, D), F32), pltpu.VMEM((SUBLANES, D), F32),
                        pltpu.VMEM((tm, D), F32), pltpu.VMEM((tm, D), F32)],
        compiler_params=_params("arbitrary"),
    )(p, p, p, p, p, hseq, dya, dyb, dgl, p, p, p, hseq, p, dya, cw, wab, wabt, lam_row)


def _inproj_bwd(dp, dx1, x, g_row, win, layer, tm, comm=None):
    T, D = x.shape
    ns = win.shape[-1]
    n = T // tm
    nt = (((1,), (1,)), ((), ()))

    def body(dp_ref, dx_ref, x_ref, g_ref, w_ref, dx0_ref, red_ref, acc, racc):
        i = pl.program_id(0)
        j = pl.program_id(1)

        @pl.when((i == 0) & (j == 0))
        def _():
            racc[...] = jnp.zeros_like(racc)

        @pl.when(j == 0)
        def _():
            acc[...] = jnp.zeros_like(acc)

        acc[...] += lax.dot_general(dp_ref[...], w_ref[...], nt, preferred_element_type=F32)

        @pl.when(j == N_CHIP - 1)
        def _():
            dx, dgain = _rms_bwd(acc[...], x_ref[...], g_ref[...])
            dx0_ref[...] = dx_ref[...] + dx
            racc[...] += _colsum8(dgain)

        @pl.when((i == n - 1) & (j == N_CHIP - 1))
        def _():
            red_ref[...] = jnp.sum(racc[...], axis=0, keepdims=True)

    tok = pl.BlockSpec((tm, D), lambda i, j: (i, 0))
    return _call(
        body, comm, name=f"inproj_bwd_{layer}", grid=(n, N_CHIP),
        in_specs=[pl.BlockSpec((tm, ns), lambda i, j: (i, j)), tok, tok, pl.BlockSpec((1, D), lambda i, j: (0, 0)),
                  pl.BlockSpec((None, D, ns), lambda i, j: (j, 0, 0))],
        out_specs=[tok, pl.BlockSpec((1, D), lambda i, j: (0, 0))],
        out_shape=[jax.ShapeDtypeStruct((T, D), F32), jax.ShapeDtypeStruct((1, D), F32)],
        scratch_shapes=[pltpu.VMEM((tm, D), F32), pltpu.VMEM((SUBLANES, D), F32)],
        compiler_params=_params("arbitrary", "arbitrary"),
    )(dp, dx1, x, g_row, win)


def _wgrad(a, b, name, tk, a_kind="whole", b_kind="whole", nj=1, comm=None):
    T = a.shape[-2]
    width = lambda v, kind: v.shape[-1] // nj if kind == "cols" else v.shape[-1]
    ka, kb = width(a, a_kind), width(b, b_kind)
    nt = T // tk
    tn = (((0,), (0,)), ((), ()))

    def spec(k, kind):
        if kind == "cm":
            return pl.BlockSpec((None, tk, k), lambda j, t: (j, t, 0))
        if kind == "cols":
            return pl.BlockSpec((tk, k), lambda j, t: (t, j))
        return pl.BlockSpec((tk, k), lambda j, t: (t, 0))

    def body(a_ref, b_ref, o_ref, ob_ref):
        t = pl.program_id(1)

        @pl.when(t == 0)
        def _():
            o_ref[...] = jnp.zeros_like(o_ref)

        o_ref[...] += lax.dot_general(a_ref[...], b_ref[...], tn, preferred_element_type=F32)

        @pl.when(t == nt - 1)
        def _():
            ob_ref[...] = o_ref[...].astype(BF16)

    o_spec = pl.BlockSpec((None, ka, kb), lambda j, t: (j, 0, 0))
    return _call(
        body, comm, name=name, grid=(nj, nt),
        in_specs=[spec(ka, a_kind), spec(kb, b_kind)], out_specs=[o_spec, o_spec],
        out_shape=[jax.ShapeDtypeStruct((nj, ka, kb), F32), jax.ShapeDtypeStruct((nj, ka, kb), BF16)],
        compiler_params=_params("parallel", "arbitrary"),
    )(a, b)


def _wgrad_pair(a, b1, b2, name, tk, comm=None):
    T, ka = a.shape
    nj, _, kb = b1.shape
    nt = T // tk
    tn = (((0,), (0,)), ((), ()))

    def body(a_ref, b1_ref, b2_ref, o1_ref, o1b_ref, o2_ref, o2b_ref):
        t = pl.program_id(1)

        @pl.when(t == 0)
        def _():
            o1_ref[...] = jnp.zeros_like(o1_ref)
            o2_ref[...] = jnp.zeros_like(o2_ref)

        av = a_ref[...]
        o1_ref[...] += lax.dot_general(b1_ref[...], av, tn, preferred_element_type=F32)
        o2_ref[...] += lax.dot_general(b2_ref[...], av, tn, preferred_element_type=F32)

        @pl.when(t == nt - 1)
        def _():
            o1b_ref[...] = o1_ref[...].astype(BF16)
            o2b_ref[...] = o2_ref[...].astype(BF16)

    b_spec = pl.BlockSpec((None, tk, kb), lambda j, t: (j, t, 0))
    o_spec = pl.BlockSpec((None, kb, ka), lambda j, t: (j, 0, 0))
    f32 = jax.ShapeDtypeStruct((nj, kb, ka), F32)
    b16 = jax.ShapeDtypeStruct((nj, kb, ka), BF16)
    return _call(
        body, comm, name=name, grid=(nj, nt),
        in_specs=[pl.BlockSpec((tk, ka), lambda j, t: (t, 0)), b_spec, b_spec], out_specs=[o_spec] * 4,
        out_shape=[f32, b16, f32, b16], compiler_params=_params("parallel", "arbitrary"),
    )(a, b1, b2)


def _block_diag(w):
    hb = LRU_BLOCK // LRU_HEAD_DIM
    nb = w.shape[0] // hb
    w4 = w.reshape(nb, hb, LRU_HEAD_DIM, LRU_HEAD_DIM)
    eye = jnp.eye(hb, dtype=w.dtype)
    return jnp.einsum("bide,ij->bidje", w4, eye).reshape(nb, LRU_BLOCK, LRU_BLOCK)


def _diag_heads(m):
    hb = LRU_BLOCK // LRU_HEAD_DIM
    nb = m.shape[0]
    m5 = m.reshape(nb, hb, LRU_HEAD_DIM, hb, LRU_HEAD_DIM)
    eye = jnp.eye(hb, dtype=m.dtype)
    return jnp.einsum("bidje,ij->bide", m5, eye).reshape(nb * hb, LRU_HEAD_DIM, LRU_HEAD_DIM)


def _tiles(T):
    cap = lambda n: min(n, T)
    return dict(inproj=cap(1024), mixer=cap(256), merge=cap(512), ffn=cap(1024), ffn_bwd=cap(512), loss=cap(512), inproj_bwd=cap(1024),
                wgrad_in=cap(2048), wgrad=cap(2048))


class _NoSchedule:
    def carry(self, name):
        return None

    def after(self, name):
        pass

    def grad(self, key, layer, f32, b16):
        pass


def _local_step(x, target, W, small, tiles, sched):
    L = small["ln1_g"].shape[0]
    D = x.shape[1]
    square = lambda a: a.reshape(D, D)
    saved = []
    h = x
    for l in range(L):
        cw = jnp.concatenate([small["conv_a_w"][l], small["conv_b_w"][l], small["conv_b_b"][l][None],
                              small["lru_ba"][l][None], small["lru_bx"][l][None]], axis=0)
        wab = jnp.concatenate([_block_diag(small["lru_wa"][l]), _block_diag(small["lru_wx"][l])], axis=2).astype(BF16)
        wabt = jnp.swapaxes(wab, 1, 2)
        lam_row = small["lru_lambda"][l][None]
        ln1_row = small["ln1_g"][l][None]
        ln2_row = small["ln2_g"][l][None]
        p, h1 = _rms_inproj(h, ln1_row, W["win", l], l, tiles["inproj"], sched.carry(f"rms_inproj_{l}"))
        ya, yb, hseq = _mixer_fwd(p, cw, wab, lam_row, l, tiles["mixer"], sched.carry(f"mixer_fwd_{l}"))
        oa, ob, mg, x1 = _merge_fwd(h, p, ya, yb, square(W["woa", l]), square(W["wob", l]), square(W["wo", l]),
                                    small["gate_bias"][l], l, tiles["merge"], sched.carry(f"merge_fwd_{l}"))
        h2, gg, uu, x2 = _ffn_fwd(x1, ln2_row, W["wg", l], W["wu", l], W["wd", l], l, tiles["ffn"], sched.carry(f"ffn_fwd_{l}"))
        saved.append(dict(x0=h, p=p, h1=h1, ya=ya, yb=yb, hseq=hseq, oa=oa, ob=ob, mg=mg, x1=x1, h2=h2, gg=gg, uu=uu,
                          cw=cw, wab=wab, wabt=wabt, lam_row=lam_row, ln1_row=ln1_row, ln2_row=ln2_row))
        h = x2

    dx, red = _final_loss(h, small["final_g"][None], target, tiles["loss"])
    loss_row, d_final_g = red[0], red[1]

    gsmall = {k: [None] * L for k in ("ln1_g", "ln2_g", "conv_a_w", "conv_b_w", "conv_b_b", "lru_wa", "lru_ba", "lru_wx",
                                      "lru_bx", "lru_lambda", "gate_bias")}
    tk = tiles["wgrad"]
    for l in reversed(range(L)):
        s = saved[l]
        dgg, duu, act, dx1, dx2b, dln2 = _ffn_bwd(dx, s["x1"], s["ln2_row"], s["gg"], s["uu"], W["wg", l], W["wu", l], W["wd", l],
                                                 l, tiles["ffn_bwd"], sched.carry(f"ffn_bwd_{l}"))
        sched.after(f"ffn_bwd_{l}")
        gate_up = _wgrad_pair(s["h2"], dgg, duu, f"wgrad_ffn_gate_up_{l}", tk, sched.carry(f"wgrad_ffn_gate_up_{l}"))
        sched.grad("wg", l, *gate_up[0:2])
        sched.grad("wu", l, *gate_up[2:4])
        sched.after(f"wgrad_ffn_gate_up_{l}")
        sched.grad("wd", l, *_wgrad(act, dx2b, f"wgrad_ffn_down_{l}", tk, "cm", "whole", N_CHIP, sched.carry(f"wgrad_ffn_down_{l}")))
        dya, dyb, doa, dob, dgl, dx1b, dgbias = _merge_bwd(dx1, s["p"], s["oa"], s["ob"], square(W["woa", l]), square(W["wob", l]),
                                                         square(W["wo", l]), small["gate_bias"][l], l, tiles["merge"],
                                                         sched.carry(f"merge_bwd_{l}"))
        sched.after(f"merge_bwd_{l}")
        sched.grad("wo", l, *_wgrad(s["mg"], dx1b, f"wgrad_w_o_{l}", tk))
        sched.grad("woa", l, *_wgrad(s["ya"], doa, f"wgrad_w_out_a_{l}", tk))
        sched.grad("wob", l, *_wgrad(s["yb"], dob, f"wgrad_w_out_b_{l}", tk))
        dp, mred, dwab = _mixer_bwd(s["p"], s["hseq"], dya, dyb, dgl, s["cw"], s["wab"], s["wabt"], s["lam_row"], l,
                                    tiles["mixer"], sched.carry(f"mixer_bwd_{l}"))
        sched.after(f"mixer_bwd_{l}")
        sched.grad("win", l, *_wgrad(s["h1"], dp, f"wgrad_w_in_{l}", tiles["wgrad_in"], "whole", "cols", N_CHIP,
                                     sched.carry(f"wgrad_w_in_{l}")))
        sched.after(f"wgrad_w_in_{l}")
        dx, dln1 = _inproj_bwd(dp, dx1, s["x0"], s["ln1_row"], W["win", l], l, tiles["inproj_bwd"], sched.carry(f"inproj_bwd_{l}"))
        sched.after(f"inproj_bwd_{l}")
        gsmall["ln1_g"][l] = dln1[0]
        gsmall["ln2_g"][l] = dln2[0]
        gsmall["conv_a_w"][l] = mred[0:CONV_A_K]
        gsmall["conv_b_w"][l] = mred[CONV_A_K:CONV_A_K + CONV_B_K]
        gsmall["conv_b_b"][l] = mred[7]
        gsmall["lru_ba"][l] = mred[8]
        gsmall["lru_bx"][l] = mred[9]
        gsmall["lru_lambda"][l] = mred[10]
        gsmall["lru_wa"][l] = _diag_heads(dwab[:, :, :LRU_BLOCK])
        gsmall["lru_wx"][l] = _diag_heads(dwab[:, :, LRU_BLOCK:])
        gsmall["gate_bias"][l] = dgbias
    gsmall = {k: jnp.stack(v) for k, v in gsmall.items()}
    gsmall["final_g"] = d_final_g
    return loss_row, dx, gsmall


def _small_allreduce(buf):
    R, C = buf.shape
    n_dev = 8
    rp = R // n_dev
    rel = [(k >> 2 & 1, k >> 1 & 1, k & 1) for k in range(1, n_dev)]

    def body(in_ref, out_ref, recv, s1, r1, s2, r2):
        x, y, c, _ = _place()
        flip = lambda v, bit: 1 - v if bit else v
        peers = [(flip(x, kx), flip(y, ky), flip(c, kc)) for kx, ky, kc in rel]
        dev = lambda p: 4 * p[0] + 2 * p[1] + p[2]
        part = lambda ref, d: ref.at[pl.ds(pl.multiple_of(d * rp, SUBLANES), rp), :]
        me = dev((x, y, c))

        def scatter(k, src_dev, to):
            return pltpu.make_async_remote_copy(src_ref=part(in_ref, dev(to)), dst_ref=recv.at[src_dev], send_sem=s1.at[k],
                                                recv_sem=r1.at[k], device_id=to, device_id_type=MESH)

        def gather(k, src_dev, to):
            return pltpu.make_async_remote_copy(src_ref=part(out_ref, src_dev), dst_ref=part(out_ref, src_dev), send_sem=s2.at[k],
                                                recv_sem=r2.at[k], device_id=to, device_id_type=MESH)

        first = [scatter(k, me, p) for k, p in enumerate(peers)]
        for cp in first:
            cp.start()
        recv[me] = part(in_ref, me)[...]
        for k, p in enumerate(peers):
            scatter(k, dev(p), (x, y, c)).wait_recv()
        total = recv[0]
        for d in range(1, n_dev):
            total = total + recv[d]
        part(out_ref, me)[...] = total
        second = [gather(k, me, p) for k, p in enumerate(peers)]
        for cp in second:
            cp.start()
        for k, p in enumerate(peers):
            gather(k, dev(p), (x, y, c)).wait_recv()
        for cp in first + second:
            cp.wait_send()

    dma = pltpu.SemaphoreType.DMA
    vm = pl.BlockSpec(memory_space=pltpu.VMEM)
    return pl.pallas_call(
        body, name="small_allreduce", out_shape=jax.ShapeDtypeStruct((R, C), buf.dtype),
        in_specs=[vm], out_specs=vm,
        scratch_shapes=[pltpu.VMEM((n_dev, rp, C), buf.dtype), dma((n_dev - 1,)), dma((n_dev - 1,)), dma((n_dev - 1,)), dma((n_dev - 1,))],
    )(buf)


ELEMENTWISE_BLOCK_BYTES = 2 * 1024 * 1024


def _row_block(k, n):
    best = None
    for b in range(16, k + 1, 16):
        if k % b == 0 and b * n * 4 <= ELEMENTWISE_BLOCK_BYTES:
            best = b
    return best or k


def _add_halves(g, recv, place_arr, name):
    nj, hk, N = recv.shape
    bk = _row_block(hk, N)
    nb = hk // bk

    def body(k_ref, g_ref, r_ref, o_ref, ob_ref):
        s = g_ref[...] + r_ref[...].astype(F32)
        ob_ref[...] = s.astype(BF16)

        @pl.when(pl.program_id(1) == k_ref[0])
        def _():
            o_ref[...] = s

    blk = pl.BlockSpec((None, bk, N), lambda i, j, k_ref: (j, i, 0))
    grid_spec = pltpu.PrefetchScalarGridSpec(
        num_scalar_prefetch=1, grid=(nb, nj),
        in_specs=[pl.BlockSpec((None, bk, N), lambda i, j, k_ref: (j, k_ref[1] * nb + i, 0)), blk],
        out_specs=[pl.BlockSpec((bk, N), lambda i, j, k_ref: (i, 0)), blk])
    return pl.pallas_call(
        body, name=name, grid_spec=grid_spec,
        out_shape=[jax.ShapeDtypeStruct((hk, N), F32), jax.ShapeDtypeStruct((nj, hk, N), BF16)],
        compiler_params=_params("parallel", "arbitrary"),
    )(place_arr, g, recv)


def _add_chips(pc, recv, place_arr, layer, n_layers, prev, name):
    hk, N = pc.shape
    bk = _row_block(hk, N)
    nb = hk // bk

    def body(k_ref, p_ref, r0_ref, r1_ref, r2_ref, *rest):
        o_ref = rest[-1]
        o_ref[...] = ((p_ref[...] + r0_ref[...].astype(F32)) + r1_ref[...].astype(F32)) + r2_ref[...].astype(F32)

    rspec = lambda j: pl.BlockSpec((None, bk, N), lambda i, k_ref: (j, i, 0))
    in_specs = [pl.BlockSpec((bk, N), lambda i, k_ref: (i, 0)), rspec(0), rspec(1), rspec(2)]
    operands = [pc, recv, recv, recv]
    aliases = {}
    if prev is not None:
        in_specs.append(ANY)
        operands.append(prev)
        aliases = {5: 0}
    grid_spec = pltpu.PrefetchScalarGridSpec(
        num_scalar_prefetch=1, grid=(nb,), in_specs=in_specs,
        out_specs=pl.BlockSpec((None, bk, N), lambda i, k_ref: (layer, k_ref[1] * nb + i, 0)))
    return pl.pallas_call(
        body, name=name, grid_spec=grid_spec, out_shape=jax.ShapeDtypeStruct((n_layers, 2 * hk, N), F32),
        input_output_aliases=aliases, compiler_params=_params("parallel"),
    )(place_arr, *operands)


def _adamw_math(w, g, m, v):
    m = ADAM_B1 * m + (1.0 - ADAM_B1) * g
    v = ADAM_B2 * v + (1.0 - ADAM_B2) * (g * g)
    m_hat = m / (1.0 - ADAM_B1 ** ADAM_STEP)
    v_hat = v / (1.0 - ADAM_B2 ** ADAM_STEP)
    delta = -ADAM_LR * (m_hat / (jnp.sqrt(v_hat) + ADAM_EPS) + ADAM_WD * w)
    return delta, m, v


def _adamw(w, g, m, v, name):
    L, K, N = w.shape
    bk = _row_block(K, N)

    def body(w_ref, g_ref, m_ref, v_ref, d_ref, nm_ref, nv_ref):
        d_ref[...], nm_ref[...], nv_ref[...] = _adamw_math(w_ref[...], g_ref[...], m_ref[...], v_ref[...])

    blk = pl.BlockSpec((None, bk, N), lambda l, i: (l, i, 0))
    sds = jax.ShapeDtypeStruct((L, K, N), F32)
    return pl.pallas_call(
        body, name=name, grid=(L, K // bk), in_specs=[blk] * 4, out_specs=[blk] * 3, out_shape=[sds] * 3,
        compiler_params=_params("parallel", "parallel"),
    )(w, g, m, v)


def _adamw_small(ws, gs, ms, vs):
    n = len(ws)

    def body(*refs):
        w, g, m, v, d, nm, nv = (refs[k * n:(k + 1) * n] for k in range(7))
        for k in range(n):
            d[k][...], nm[k][...], nv[k][...] = _adamw_math(w[k][...], g[k][...], m[k][...], v[k][...])

    sds = [jax.ShapeDtypeStruct(a.shape, F32) for a in ws]
    out = pl.pallas_call(body, name="adamw_small", out_shape=sds * 3)(*ws, *gs, *ms, *vs)
    return out[:n], out[n:2 * n], out[2 * n:]


def _cast_bf16(w, layer, place_arr, name):
    _, K, N = w.shape
    bk = _row_block(K, N)

    def body(k_ref, w_ref, o_ref):
        o_ref[...] = w_ref[...].astype(BF16)

    grid_spec = pltpu.PrefetchScalarGridSpec(
        num_scalar_prefetch=1, grid=(K // bk,),
        in_specs=[pl.BlockSpec((None, bk, N), lambda i, k_ref: (layer, i, 0))],
        out_specs=pl.BlockSpec((None, bk, N), lambda i, k_ref: (k_ref[0], i, 0)))
    return pl.pallas_call(
        body, name=name, grid_spec=grid_spec, out_shape=jax.ShapeDtypeStruct((N_CHIP, K, N), BF16),
        compiler_params=_params("parallel"),
    )(place_arr, w)


BIG = ("w_in", "w_out_a", "w_out_b", "w_o", "w_ffn_gate", "w_ffn_up", "w_ffn_down")
BIG_KEY = dict(w_in="win", w_out_a="woa", w_out_b="wob", w_o="wo", w_ffn_gate="wg", w_ffn_up="wu", w_ffn_down="wd")
SHARDED_SMALL = ("conv_a_w", "conv_b_w", "gate_bias")
REPLICATED = ("ln1_g", "conv_b_b", "lru_wa", "lru_ba", "lru_wx", "lru_bx", "lru_lambda", "ln2_g", "final_g")
WEIGHTS = ("ln1_g", "w_in", "conv_a_w", "conv_b_w", "conv_b_b", "lru_wa", "lru_ba", "lru_wx", "lru_bx", "lru_lambda",
           "w_out_a", "w_out_b", "gate_bias", "w_o", "ln2_g", "w_ffn_gate", "w_ffn_up", "w_ffn_down", "final_g")
LANES = 1024


def _pack_rows(arrays, row_multiple):
    flat = jnp.concatenate([a.reshape(-1) for a in arrays])
    rows = -(-flat.shape[0] // LANES)
    rows = -(-rows // row_multiple) * row_multiple
    flat = jnp.pad(flat, (0, rows * LANES - flat.shape[0]))
    return flat.reshape(rows, LANES)


def _unpack_rows(buf, shapes):
    flat = buf.reshape(-1)
    out, off = [], 0
    for s in shapes:
        n = 1
        for d in s:
            n *= d
        out.append(flat[off:off + n].reshape(s))
        off += n
    return out


OUT_KEYS = ("wo", "woa", "wob")
FFN_KEYS = ("wg", "wu", "wd")


def _items(keys, layer):
    return [(k, layer) for k in keys]


CARRY = {
    "rms_inproj_0": [("gather_ici", _items(OUT_KEYS + FFN_KEYS, 0))],
    "mixer_fwd_0": [("gather_d2d", _items(OUT_KEYS + FFN_KEYS, 0)), ("gather_ici", _items(("win",) + OUT_KEYS, 1))],
    "merge_fwd_0": [("gather_d2d", _items(("win",) + OUT_KEYS, 1))],
    "ffn_fwd_0": [("gather_ici", _items(FFN_KEYS, 1))],
    "rms_inproj_1": [("gather_d2d", _items(FFN_KEYS, 1))],
    "merge_bwd_1": [("halves", _items(FFN_KEYS, 1))],
    "mixer_bwd_1": [("chips", _items(FFN_KEYS, 1)), ("halves", _items(OUT_KEYS, 1))],
    "inproj_bwd_1": [("chips", _items(OUT_KEYS, 1)), ("share", _items(FFN_KEYS, 1))],
    "ffn_bwd_0": [("halves", [("win", 1)]), ("share", _items(OUT_KEYS, 1))],
    "wgrad_ffn_gate_up_0": [("chips", [("win", 1)])],
    "wgrad_ffn_down_0": [("share", [("win", 1)])],
    "merge_bwd_0": [("halves", _items(FFN_KEYS, 0))],
    "mixer_bwd_0": [("chips", _items(FFN_KEYS, 0)), ("halves", _items(OUT_KEYS, 0))],
    "wgrad_w_in_0": [("chips", _items(OUT_KEYS, 0)), ("share", _items(FFN_KEYS, 0))],
    "inproj_bwd_0": [("chips", [("win", 0)])],
}
AFTER = {
    "merge_bwd_1": [("add_halves", _items(FFN_KEYS, 1))],
    "mixer_bwd_1": [("add_chips", _items(FFN_KEYS, 1)), ("add_halves", _items(OUT_KEYS, 1))],
    "inproj_bwd_1": [("add_chips", _items(OUT_KEYS, 1))],
    "ffn_bwd_0": [("add_halves", [("win", 1)])],
    "wgrad_ffn_gate_up_0": [("add_chips", [("win", 1)])],
    "merge_bwd_0": [("add_halves", _items(FFN_KEYS, 0))],
    "mixer_bwd_0": [("add_chips", _items(FFN_KEYS, 0)), ("add_halves", _items(OUT_KEYS, 0))],
    "wgrad_w_in_0": [("add_chips", _items(OUT_KEYS, 0)),
                     ("run", ("reduce_halves_w_in_0", [("halves", [("win", 0)]), ("share", _items(OUT_KEYS, 0))])),
                     ("add_halves", [("win", 0)])],
    "inproj_bwd_0": [("add_chips", [("win", 0)]), ("run", ("reduce_share_w_in_0", [("share", [("win", 0)])]))],
}


class _Schedule:
    def __init__(self, slots, place_arr, n_layers):
        self.W = slots
        self.place, self.L = place_arr, n_layers
        self.g32, self.g16 = {}, {}
        self.from_sibling, self.chip_sum, self.chip_sum16, self.from_chips = {}, {}, {}, {}
        self.reduced = {}

    def stage(self, comm, kind, items):
        bf = lambda shape: jax.ShapeDtypeStruct(shape, BF16)
        for it in items:
            if kind == "gather_ici":
                comm.add(_gather_ici, 3, io=[(self.W, it)])
            elif kind == "gather_d2d":
                comm.add(_gather_d2d, 3, io=[(self.W, it)])
            elif kind == "halves":
                nj, K, N = self.g16[it].shape
                comm.add(_reduce_halves, 1, ro=[self.g16[it]], nw=[(self.from_sibling, it, bf((nj, K // 2, N)))])
            elif kind == "chips":
                _, hk, N = self.chip_sum16[it].shape
                comm.add(_reduce_chips, 3, ro=[self.chip_sum16[it]], nw=[(self.from_chips, it, bf((3, hk, N)))])
            elif kind == "share":
                comm.add(_reduce_share(it[1]), 1, io=[(self.reduced, it[0])])
        return comm

    def carry(self, name):
        comm = _Carried()
        for kind, items in CARRY.get(name, ()):
            self.stage(comm, kind, items)
        return comm

    def run(self, name, rounds):
        _run_comm([self.stage(_Carried(), kind, items) for kind, items in rounds], name)

    def grad(self, key, layer, f32, b16):
        by_chip = lambda g: g.reshape(N_CHIP, -1, g.shape[-1])
        self.g32[key, layer], self.g16[key, layer] = by_chip(f32), by_chip(b16)

    def add(self, kind, items):
        for key, layer in items:
            it = (key, layer)
            if kind == "add_halves":
                self.chip_sum[it], self.chip_sum16[it] = _add_halves(self.g32[it], self.from_sibling[it], self.place,
                                                                    f"add_halves_{key}_{layer}")
            else:
                self.reduced[key] = _add_chips(self.chip_sum[it], self.from_chips[it], self.place, layer, self.L,
                                               self.reduced.get(key), f"add_chips_{key}_{layer}")

    def after(self, name):
        for kind, items in AFTER.get(name, ()):
            if kind == "run":
                self.run(*items)
            else:
                self.add(kind, items)


def _step(w, m, v, x, target):
    xi, yi, ci = lax.axis_index("x"), lax.axis_index("y"), lax.axis_index("c")
    chip = _chip_id(xi, yi)
    place_arr = jnp.stack([chip, ci]).astype(jnp.int32)
    L = w["ln1_g"].shape[0]
    assert L == 2
    D = x.shape[1]
    dc = D // N_CHIP

    stored = lambda n, a: jnp.swapaxes(a, 1, 2) if n in ("w_ffn_gate", "w_ffn_up") else a
    slots = {(BIG_KEY[n], l): _cast_bf16(stored(n, w[n]), l, place_arr, f"cast_{n}_{l}") for n in BIG for l in range(L)}
    sched = _Schedule(slots, place_arr, L)
    small_shard = jnp.concatenate([w[n] for n in SHARDED_SMALL], axis=1)
    got = {}
    first = sched.stage(_Carried(), "gather_ici", [("win", 0)])
    first.add(_gather_small, 3, ro=[small_shard], nw=[(got, "small", jax.ShapeDtypeStruct((3,) + small_shard.shape, F32))])
    _run_comm([first, sched.stage(_Carried(), "gather_d2d", [("win", 0)])], "gather_first")
    small_g = jnp.zeros((N_CHIP,) + small_shard.shape, F32)
    small_g = lax.dynamic_update_index_in_dim(small_g, small_shard, chip, 0)
    for j, (cx, cy) in enumerate([(1 - xi, yi), (xi, 1 - yi), (1 - xi, 1 - yi)]):
        small_g = lax.dynamic_update_index_in_dim(small_g, got["small"][j], _chip_id(cx, cy), 0)
    small_full = jnp.transpose(small_g, (1, 2, 0, 3)).reshape(L, small_shard.shape[1], D)
    small = {n: w[n] for n in REPLICATED}
    off = 0
    for n in SHARDED_SMALL:
        k = w[n].shape[1]
        small[n] = small_full[:, off:off + k]
        off += k

    loss_row, grad_x, gsmall = _local_step(x, target, sched.W, small, _tiles(x.shape[0]), sched)

    grads = {n: stored(n, sched.reduced[BIG_KEY[n]]) for n in BIG}

    order = [n for n in WEIGHTS if n not in BIG]
    packed = _pack_rows([gsmall[n] for n in order] + [loss_row], 8 * SUBLANES)
    summed = _small_allreduce(packed)
    parts = _unpack_rows(summed, [gsmall[n].shape for n in order] + [loss_row.shape])
    loss = jnp.sum(parts[-1])
    for n, g in zip(order, parts[:-1]):
        grads[n] = lax.dynamic_slice_in_dim(g, chip * dc, dc, axis=2) if n in SHARDED_SMALL else g

    delta, new_m, new_v = {}, {}, {}
    for n in BIG:
        d, nm, nv = _adamw(stored(n, w[n]), sched.reduced[BIG_KEY[n]], stored(n, m[n]), stored(n, v[n]), f"adamw_{n}")
        delta[n], new_m[n], new_v[n] = stored(n, d), stored(n, nm), stored(n, nv)
    for d, arrays in zip((delta, new_m, new_v), _adamw_small(*([d[n] for n in order] for d in (w, grads, m, v)))):
        d.update(zip(order, arrays))
    return loss, grad_x, grads, delta, new_m, new_v


def kernel(x, ln1_g, w_in, conv_a_w, conv_b_w, conv_b_b, lru_wa, lru_ba, lru_wx, lru_bx, lru_lambda, w_out_a, w_out_b, gate_bias, w_o, ln2_g, w_ffn_gate, w_ffn_up, w_ffn_down, final_g, loss_target, m_ln1_g, m_w_in, m_conv_a_w, m_conv_b_w, m_conv_b_b, m_lru_wa, m_lru_ba, m_lru_wx, m_lru_bx, m_lru_lambda, m_w_out_a, m_w_out_b, m_gate_bias, m_w_o, m_ln2_g, m_w_ffn_gate, m_w_ffn_up, m_w_ffn_down, m_final_g, v_ln1_g, v_w_in, v_conv_a_w, v_conv_b_w, v_conv_b_b, v_lru_wa, v_lru_ba, v_lru_wx, v_lru_bx, v_lru_lambda, v_w_out_a, v_w_out_b, v_gate_bias, v_w_o, v_ln2_g, v_w_ffn_gate, v_w_ffn_up, v_w_ffn_down, v_final_g):
    w = dict(ln1_g=ln1_g, w_in=w_in, conv_a_w=conv_a_w, conv_b_w=conv_b_w, conv_b_b=conv_b_b, lru_wa=lru_wa, lru_ba=lru_ba,
             lru_wx=lru_wx, lru_bx=lru_bx, lru_lambda=lru_lambda, w_out_a=w_out_a, w_out_b=w_out_b, gate_bias=gate_bias, w_o=w_o,
             ln2_g=ln2_g, w_ffn_gate=w_ffn_gate, w_ffn_up=w_ffn_up, w_ffn_down=w_ffn_down, final_g=final_g)
    m = dict(ln1_g=m_ln1_g, w_in=m_w_in, conv_a_w=m_conv_a_w, conv_b_w=m_conv_b_w, conv_b_b=m_conv_b_b, lru_wa=m_lru_wa,
             lru_ba=m_lru_ba, lru_wx=m_lru_wx, lru_bx=m_lru_bx, lru_lambda=m_lru_lambda, w_out_a=m_w_out_a, w_out_b=m_w_out_b,
             gate_bias=m_gate_bias, w_o=m_w_o, ln2_g=m_ln2_g, w_ffn_gate=m_w_ffn_gate, w_ffn_up=m_w_ffn_up,
             w_ffn_down=m_w_ffn_down, final_g=m_final_g)
    v = dict(ln1_g=v_ln1_g, w_in=v_w_in, conv_a_w=v_conv_a_w, conv_b_w=v_conv_b_w, conv_b_b=v_conv_b_b, lru_wa=v_lru_wa,
             lru_ba=v_lru_ba, lru_wx=v_lru_wx, lru_bx=v_lru_bx, lru_lambda=v_lru_lambda, w_out_a=v_w_out_a, w_out_b=v_w_out_b,
             gate_bias=v_gate_bias, w_o=v_w_o, ln2_g=v_ln2_g, w_ffn_gate=v_w_ffn_gate, w_ffn_up=v_w_ffn_up,
             w_ffn_down=v_w_ffn_down, final_g=v_final_g)
    loss, grad_x, grads, delta, new_m, new_v = _step(w, m, v, x[0], loss_target[0])
    return (loss, grad_x[None], *[grads[n] for n in WEIGHTS], *[delta[n] for n in WEIGHTS],
            *[new_m[n] for n in WEIGHTS], *[new_v[n] for n in WEIGHTS])
```

```python
import jax
import jax.numpy as jnp
from jax import lax
from jax.experimental import pallas as pl
from jax.experimental.pallas import tpu as pltpu

F32 = jnp.float32
BF16 = jnp.bfloat16
MESH = pl.DeviceIdType.MESH

N_CHIP = 4
RMS_EPS = 1e-6
LRU_C = 8.0
LRU_HEAD_DIM = 64
LRU_BLOCK = 256
CONV_A_K = 3
CONV_B_K = 4
ADAM_LR = 0.001
ADAM_B1 = 0.9
ADAM_B2 = 0.999
ADAM_EPS = 1e-08
ADAM_WD = 0.01
ADAM_STEP = 10
SUBLANES = 8
VMEM_LIMIT = 56 * 1024 * 1024


def _params(*sem):
    return pltpu.CompilerParams(dimension_semantics=sem, vmem_limit_bytes=VMEM_LIMIT)


def _sigmoid(v):
    return 1.0 / (1.0 + jnp.exp(-v))


def _one_minus_sq(la, a):
    return jnp.tanh(-la) * (1.0 + a * a)


def _gelu_parts(v):
    k = 0.7978845608028654
    v2 = v * v
    t = jnp.tanh(k * (v + 0.044715 * v * v2))
    gelu = 0.5 * v * (1.0 + t)
    dgelu = 0.5 * (1.0 + t) + 0.5 * v * (1.0 - t * t) * k * (1.0 + 3 * 0.044715 * v2)
    return gelu, dgelu


def _shift_down(v, k, prev8):
    rolled = pltpu.roll(v, k, 0)
    r8 = lax.broadcasted_iota(jnp.int32, prev8.shape, 0)
    head = jnp.where(r8 < k, pltpu.roll(prev8, k, 0), rolled[0:SUBLANES])
    return jnp.concatenate([head, rolled[SUBLANES:]], axis=0)


def _shift_up(v, k, next8):
    tm = v.shape[0]
    rolled = pltpu.roll(v, tm - k, 0)
    r8 = lax.broadcasted_iota(jnp.int32, next8.shape, 0)
    tail = jnp.where(r8 >= SUBLANES - k, pltpu.roll(next8, SUBLANES - k, 0), rolled[tm - SUBLANES:])
    return jnp.concatenate([rolled[:tm - SUBLANES], tail], axis=0)


def _group_scan(a, b, reverse):
    tm, c = a.shape
    a = a.reshape(tm // SUBLANES, SUBLANES, c)
    b = b.reshape(tm // SUBLANES, SUBLANES, c)
    q = lax.broadcasted_iota(jnp.int32, a.shape, 1)
    for s in (1, 2, 4):
        msk = q < SUBLANES - s if reverse else q >= s
        shift = SUBLANES - s if reverse else s
        b = jnp.where(msk, a * pltpu.roll(b, shift, 1) + b, b)
        a = jnp.where(msk, a * pltpu.roll(a, shift, 1), a)
    return a.reshape(tm, c), b.reshape(tm, c)


def _colsum8(v):
    tm, c = v.shape
    return jnp.sum(v.reshape(tm // SUBLANES, SUBLANES, c), axis=0)


def _rms_stats(xv):
    var = jnp.mean(xv * xv, axis=-1, keepdims=True)
    return lax.rsqrt(var + RMS_EPS)


def _rms_bwd(dh, xv, g):
    rstd = _rms_stats(xv)
    xhat = xv * rstd
    dxhat = dh * g
    dx = rstd * (dxhat - xhat * jnp.mean(dxhat * xhat, axis=-1, keepdims=True))
    return dx, dh * xhat


ANY = pl.BlockSpec(memory_space=pl.ANY)


def _place():
    x, y, c = lax.axis_index("x"), lax.axis_index("y"), lax.axis_index("c")
    other_chips = [(1 - x, y), (x, 1 - y), (1 - x, 1 - y)]
    return x, y, c, other_chips


def _chip_id(x, y):
    return 2 * x + y


def _half(c, hk):
    return pl.ds(pl.multiple_of(c * hk, 16), hk)


def _remote(src, dst, to, sems):
    return pltpu.make_async_remote_copy(src_ref=src, dst_ref=dst, device_id=to, device_id_type=MESH, **sems)


class _Carried:
    def __init__(self):
        self.ro, self.io, self.nw, self.parts, self.n = [], [], [], [], 0

    def add(self, maker, n, ro=(), io=(), nw=()):
        def index(items, item, same):
            for k, other in enumerate(items):
                if same(other, item):
                    return k
            items.append(item)
            return len(items) - 1

        r = [index(self.ro, a, lambda p, q: p is q) for a in ro]
        i = [index(self.io, a, lambda p, q: p[0] is q[0] and p[1] == q[1]) for a in io]
        w = [index(self.nw, a, lambda p, q: False) for a in nw]
        self.parts.append((maker, r, i, w, self.n))
        self.n += n
        return self

    def pairs(self, ro, io, nw, ssem, rsem):
        out = []
        for maker, r, i, w, base in self.parts:
            sems = lambda k, base=base: dict(send_sem=ssem.at[base + k], recv_sem=rsem.at[base + k])
            out += maker([ro[k] for k in r], [io[k] for k in i], [nw[k] for k in w], sems)
        return out

    def start(self, *refs):
        for send, _ in self.pairs(*refs):
            send.start()

    def finish(self, *refs):
        pairs = self.pairs(*refs)
        for _, recv in pairs:
            recv.wait_recv()
        for send, _ in pairs:
            send.wait_send()

    def operands(self):
        return list(self.ro) + [store[key] for store, key in self.io]

    def out_shapes(self):
        return [jax.ShapeDtypeStruct(store[key].shape, store[key].dtype) for store, key in self.io] + [s for _, _, s in self.nw]

    def keep(self, results):
        for (store, key), arr in zip(self.io, results[:len(self.io)]):
            store[key] = arr
        for (store, key, _), arr in zip(self.nw, results[len(self.io):]):
            store[key] = arr


def _call(body, comm, *, name, grid, in_specs, out_specs, out_shape, compiler_params, scratch_shapes=(), aliases=None):
    aliases = dict(aliases or {})
    if comm is None or not comm.parts:
        return pl.pallas_call(body, name=name, grid=grid, in_specs=in_specs, out_specs=out_specs, out_shape=out_shape,
                              scratch_shapes=list(scratch_shapes), input_output_aliases=aliases, compiler_params=compiler_params)
    n_in, n_out, n_scr = len(in_specs), len(out_shape), len(scratch_shapes)
    n_ro, n_io, n_nw = len(comm.ro), len(comm.io), len(comm.nw)

    def carried(*refs):
        base_in = refs[:n_in]
        ro = refs[n_in:n_in + n_ro]
        pos = n_in + n_ro + n_io
        base_out = refs[pos:pos + n_out]
        io = refs[pos + n_out:pos + n_out + n_io]
        nw = refs[pos + n_out + n_io:pos + n_out + n_io + n_nw]
        pos += n_out + n_io + n_nw
        scr = refs[pos:pos + n_scr]
        ssem, rsem = refs[pos + n_scr], refs[pos + n_scr + 1]
        first = pl.program_id(0) == 0
        last = pl.program_id(0) == grid[0] - 1
        for axis in range(1, len(grid)):
            first = first & (pl.program_id(axis) == 0)
            last = last & (pl.program_id(axis) == grid[axis] - 1)

        @pl.when(first)
        def _():
            comm.start(ro, io, nw, ssem, rsem)

        body(*base_in, *base_out, *scr)

        @pl.when(last)
        def _():
            comm.finish(ro, io, nw, ssem, rsem)

    aliases.update({n_in + n_ro + k: n_out + k for k in range(n_io)})
    dma = pltpu.SemaphoreType.DMA
    call = pl.pallas_call(
        carried, name=name, grid=grid,
        in_specs=list(in_specs) + [ANY] * (n_ro + n_io), out_specs=list(out_specs) + [ANY] * (n_io + n_nw),
        out_shape=list(out_shape) + comm.out_shapes(), input_output_aliases=aliases,
        scratch_shapes=list(scratch_shapes) + [dma((comm.n,)), dma((comm.n,))], compiler_params=compiler_params)

    def run(*operands):
        res = call(*operands, *comm.operands())
        comm.keep(res[n_out:])
        return res[:n_out]

    return run


def _run_comm(rounds, name):
    ro, io, nw, uses = [], [], [], []
    for r in rounds:
        def index(items, item, same):
            for k, other in enumerate(items):
                if same(other, item):
                    return k
            items.append(item)
            return len(items) - 1
        uses.append(([index(ro, a, lambda p, q: p is q) for a in r.ro],
                     [index(io, a, lambda p, q: p[0] is q[0] and p[1] == q[1]) for a in r.io],
                     [index(nw, a, lambda p, q: False) for a in r.nw]))
    n_ro, n_io, n_nw = len(ro), len(io), len(nw)

    def body(*refs):
        ro_refs = refs[:n_ro]
        io_refs = refs[n_ro + n_io:n_ro + 2 * n_io]
        nw_refs = refs[n_ro + 2 * n_io:n_ro + 2 * n_io + n_nw]
        sems = refs[n_ro + 2 * n_io + n_nw:]
        for k, (r, (a, b, c)) in enumerate(zip(rounds, uses)):
            args = ([ro_refs[i] for i in a], [io_refs[i] for i in b], [nw_refs[i] for i in c], sems[2 * k], sems[2 * k + 1])
            r.start(*args)
            r.finish(*args)

    operands = ro + [store[key] for store, key in io]
    out_shape = [jax.ShapeDtypeStruct(store[key].shape, store[key].dtype) for store, key in io] + [s for _, _, s in nw]
    dma = pltpu.SemaphoreType.DMA
    res = pl.pallas_call(
        body, name=name, out_shape=out_shape,
        in_specs=[ANY] * (n_ro + n_io), out_specs=[ANY] * (n_io + n_nw),
        input_output_aliases={n_ro + k: k for k in range(n_io)},
        scratch_shapes=[dma((r.n,)) for r in rounds for _ in range(2)],
    )(*operands)
    for (store, key), arr in zip(io, res[:n_io]):
        store[key] = arr
    for (store, key, _), arr in zip(nw, res[n_io:]):
        store[key] = arr


def _gather_ici(ro, io, nw, sems):
    s = io[0]
    x, y, c, chips = _place()
    hk = s.shape[1] // 2
    mine = s.at[_chip_id(x, y), _half(c, hk)]
    pairs = []
    for j, (cx, cy) in enumerate(chips):
        theirs = s.at[_chip_id(cx, cy), _half(c, hk)]
        pairs.append((_remote(mine, mine, (cx, cy, c), sems(j)), _remote(theirs, theirs, (cx, cy, c), sems(j))))
    return pairs


def _gather_d2d(ro, io, nw, sems):
    s = io[0]
    x, y, c, chips = _place()
    hk = s.shape[1] // 2
    sib = (x, y, 1 - c)
    pairs = []
    for j, (cx, cy) in enumerate(chips):
        here = s.at[_chip_id(cx, cy), _half(c, hk)]
        there = s.at[_chip_id(cx, cy), _half(1 - c, hk)]
        pairs.append((_remote(here, here, sib, sems(j)), _remote(there, there, sib, sems(j))))
    return pairs


def _gather_small(ro, io, nw, sems):
    x, y, c, chips = _place()
    return [(_remote(ro[0], nw[0].at[j], (cx, cy, c), sems(j)),) * 2 for j, (cx, cy) in enumerate(chips)]


def _reduce_halves(ro, io, nw, sems):
    x, y, c, _ = _place()
    g = ro[0]
    hk = g.shape[1] // 2
    sib = (x, y, 1 - c)
    return [(_remote(g.at[:, _half(1 - c, hk)], nw[0], sib, sems(0)), _remote(g.at[:, _half(c, hk)], nw[0], sib, sems(0)))]


def _reduce_chips(ro, io, nw, sems):
    x, y, c, chips = _place()
    return [(_remote(ro[0].at[_chip_id(cx, cy)], nw[0].at[j], (cx, cy, c), sems(j)),) * 2 for j, (cx, cy) in enumerate(chips)]


def _reduce_share(layer):
    def maker(ro, io, nw, sems):
        g = io[0]
        x, y, c, _ = _place()
        hk = g.shape[1] // 2
        sib = (x, y, 1 - c)
        mine, theirs = g.at[layer, _half(c, hk)], g.at[layer, _half(1 - c, hk)]
        return [(_remote(mine, mine, sib, sems(0)), _remote(theirs, theirs, sib, sems(0)))]
    return maker


def _rms_inproj(x, g_row, win, layer, tm, comm=None):
    T, D = x.shape
    ns = win.shape[-1]

    def body(x_ref, g_ref, w_ref, p_ref, h_ref):
        @pl.when(pl.program_id(1) == 0)
        def _():
            xv = x_ref[...]
            h_ref[...] = (xv * _rms_stats(xv) * g_ref[...]).astype(BF16)
        p_ref[...] = jnp.dot(h_ref[...], w_ref[...], preferred_element_type=F32).astype(BF16)

    return _call(
        body, comm, name=f"rms_inproj_{layer}", grid=(T // tm, N_CHIP),
        in_specs=[pl.BlockSpec((tm, D), lambda i, j: (i, 0)),
                  pl.BlockSpec((1, D), lambda i, j: (0, 0)),
                  pl.BlockSpec((None, D, ns), lambda i, j: (j, 0, 0))],
        out_specs=[pl.BlockSpec((tm, ns), lambda i, j: (i, j)),
                   pl.BlockSpec((tm, D), lambda i, j: (i, 0))],
        out_shape=[jax.ShapeDtypeStruct((T, N_CHIP * ns), BF16), jax.ShapeDtypeStruct((T, D), BF16)],
        compiler_params=_params("parallel", "arbitrary"),
    )(x, g_row, win)


def _mixer_recompute(ca, xa, xb, zprev8, xbprev8, cw_ref, wab_ref, sp):
    row = lambda k: cw_ref[pl.ds(k, 1), :]
    z = ca * xa
    z1 = _shift_down(z, 1, zprev8)
    z2 = _shift_down(z, 2, zprev8)
    cz = row(2) * z + row(1) * z1 + row(0) * z2
    x1 = _shift_down(xb, 1, xbprev8)
    x2 = _shift_down(xb, 2, xbprev8)
    x3 = _shift_down(xb, 3, xbprev8)
    u = row(6) * xb + row(5) * x1 + row(4) * x2 + row(3) * x3 + row(7)
    ub = u.astype(BF16)
    nb = wab_ref.shape[0]
    ras, ixs = [], []
    for b in range(nb):
        ri = jnp.dot(ub[:, b * LRU_BLOCK:(b + 1) * LRU_BLOCK], wab_ref[b], preferred_element_type=F32)
        ras.append(ri[:, :LRU_BLOCK])
        ixs.append(ri[:, LRU_BLOCK:])
    r = _sigmoid(jnp.concatenate(ras, axis=1) + row(8))
    gi = _sigmoid(jnp.concatenate(ixs, axis=1) + row(9))
    la = (-LRU_C) * r * sp
    a = jnp.exp(la)
    m = jnp.sqrt(_one_minus_sq(la, a))
    return dict(z=z, z1=z1, z2=z2, cz=cz, x1=x1, x2=x2, x3=x3, u=u, ub=ub, r=r, gi=gi, a=a, m=m)


def _softplus_neg(lam):
    v = -lam
    return jnp.maximum(v, 0.0) + jnp.log1p(jnp.exp(-jnp.abs(v)))


def _mixer_fwd(p, cw, wab, lam_row, layer, tm, comm=None):
    T = p.shape[0]
    D = p.shape[1] // 7
    ngroups = tm // SUBLANES

    def body(ba_ref, ca_ref, xa_ref, xb_ref, gb_ref, cw_ref, wab_ref, lam_ref, ya_ref, yb_ref, h_ref,
             zprev, xbprev, hcarry, a_s, h_s):
        @pl.when(pl.program_id(0) == 0)
        def _():
            zprev[...] = jnp.zeros_like(zprev)
            xbprev[...] = jnp.zeros_like(xbprev)
            hcarry[...] = jnp.zeros_like(hcarry)

        ca = ca_ref[...].astype(F32)
        xa = xa_ref[...].astype(F32)
        xb = xb_ref[...].astype(F32)
        sp = _softplus_neg(lam_ref[...])
        c = _mixer_recompute(ca, xa, xb, zprev[...], xbprev[...], cw_ref, wab_ref, sp)
        zprev[...] = c["z"][tm - SUBLANES:]
        xbprev[...] = xb[tm - SUBLANES:]
        ya_ref[...] = (ba_ref[...].astype(F32) * c["cz"]).astype(BF16)

        a_s[...], h_s[...] = _group_scan(c["a"], c["m"] * c["gi"] * c["u"], reverse=False)

        def step(g, carry):
            off = pl.multiple_of(g * SUBLANES, SUBLANES)
            hg = h_s[pl.ds(off, SUBLANES), :] + a_s[pl.ds(off, SUBLANES), :] * carry
            h_s[pl.ds(off, SUBLANES), :] = hg
            return jnp.broadcast_to(hg[SUBLANES - 1:SUBLANES, :], hg.shape)

        hcarry[...] = lax.fori_loop(0, ngroups, step, hcarry[...], unroll=4)
        h = h_s[...]
        h_ref[...] = h
        gelu, _ = _gelu_parts(gb_ref[...].astype(F32))
        yb_ref[...] = (h * gelu).astype(BF16)

    col = lambda k: pl.BlockSpec((tm, D), lambda i: (i, k))
    full = lambda a: pl.BlockSpec(a.shape, lambda i: (0,) * a.ndim)
    tok = pl.BlockSpec((tm, D), lambda i: (i, 0))
    return _call(
        body, comm, name=f"mixer_fwd_{layer}", grid=(T // tm,),
        in_specs=[col(0), col(1), col(2), col(3), col(4), full(cw), full(wab), full(lam_row)],
        out_specs=[tok, tok, tok],
        out_shape=[jax.ShapeDtypeStruct((T, D), BF16), jax.ShapeDtypeStruct((T, D), BF16), jax.ShapeDtypeStruct((T, D), F32)],
        scratch_shapes=[pltpu.VMEM((SUBLANES, D), F32), pltpu.VMEM((SUBLANES, D), F32), pltpu.VMEM((SUBLANES, D), F32),
                        pltpu.VMEM((tm, D), F32), pltpu.VMEM((tm, D), F32)],
        compiler_params=_params("arbitrary"),
    )(p, p, p, p, p, cw, wab, lam_row)


def _merge_fwd(x, p, ya, yb, woa, wob, wo, gbias, layer, tm, comm=None):
    T, D = x.shape

    def body(x_ref, ga_ref, gb_ref, ya_ref, yb_ref, woa_ref, wob_ref, wo_ref, bias_ref, oa_ref, ob_ref, mg_ref, x1_ref):
        oa = jnp.dot(ya_ref[...], woa_ref[...], preferred_element_type=F32)
        ob = jnp.dot(yb_ref[...], wob_ref[...], preferred_element_type=F32)
        sa = _sigmoid(ga_ref[...].astype(F32) + bias_ref[pl.ds(0, 1), :])
        sb = _sigmoid(gb_ref[...].astype(F32) + bias_ref[pl.ds(1, 1), :])
        mg = (sa * oa + sb * ob).astype(BF16)
        oa_ref[...] = oa.astype(BF16)
        ob_ref[...] = ob.astype(BF16)
        mg_ref[...] = mg
        x1_ref[...] = x_ref[...] + jnp.dot(mg, wo_ref[...], preferred_element_type=F32)

    tok = pl.BlockSpec((tm, D), lambda i: (i, 0))
    wsp = pl.BlockSpec((D, D), lambda i: (0, 0))
    bf = jax.ShapeDtypeStruct((T, D), BF16)
    return _call(
        body, comm, name=f"merge_fwd_{layer}", grid=(T // tm,),
        in_specs=[tok, pl.BlockSpec((tm, D), lambda i: (i, 5)), pl.BlockSpec((tm, D), lambda i: (i, 6)), tok, tok,
                  wsp, wsp, wsp, pl.BlockSpec(gbias.shape, lambda i: (0, 0))],
        out_specs=[tok, tok, tok, tok],
        out_shape=[bf, bf, bf, jax.ShapeDtypeStruct((T, D), F32)],
        compiler_params=_params("parallel"),
    )(x, p, p, ya, yb, woa, wob, wo, gbias)


def _loss_tile(xv, g, tgt):
    d = xv.shape[-1]
    rstd = _rms_stats(xv)
    xhat = xv * rstd
    err = xhat * g - tgt
    dy = err * (1.0 / d)
    dxhat = dy * g
    dx = rstd * (dxhat - xhat * jnp.mean(dxhat * xhat, axis=-1, keepdims=True))
    return dx, _colsum8(err * err), _colsum8(dy * xhat)


def _ffn_fwd(x1, g_row, wg, wu, wd, layer, tm, comm=None):
    T, D = x1.shape
    fs = wg.shape[-2]
    n = T // tm
    nt = (((1,), (1,)), ((), ()))

    def body(x_ref, g_ref, wg_ref, wu_ref, wd_ref, h_ref, gg_ref, uu_ref, x2_ref, acc):
        j = pl.program_id(1)

        @pl.when(j == 0)
        def _():
            xv = x_ref[...]
            h_ref[...] = (xv * _rms_stats(xv) * g_ref[...]).astype(BF16)
            acc[...] = xv

        h = h_ref[...]
        gg = lax.dot_general(h, wg_ref[...], nt, preferred_element_type=F32)
        uu = lax.dot_general(h, wu_ref[...], nt, preferred_element_type=F32)
        gg_ref[...] = gg.astype(BF16)
        uu_ref[...] = uu.astype(BF16)
        act = (gg * _sigmoid(gg) * uu).astype(BF16)
        acc[...] += jnp.dot(act, wd_ref[...], preferred_element_type=F32)

        @pl.when(j == N_CHIP - 1)
        def _():
            x2_ref[...] = acc[...]

    tok = pl.BlockSpec((tm, D), lambda i, j: (i, 0))
    cm = pl.BlockSpec((None, tm, fs), lambda i, j: (j, i, 0))
    wsp = pl.BlockSpec((None, fs, D), lambda i, j: (j, 0, 0))
    return _call(
        body, comm, name=f"ffn_fwd_{layer}", grid=(n, N_CHIP),
        in_specs=[tok, pl.BlockSpec((1, D), lambda i, j: (0, 0)), wsp, wsp, wsp], out_specs=[tok, cm, cm, tok],
        out_shape=[jax.ShapeDtypeStruct((T, D), BF16), jax.ShapeDtypeStruct((N_CHIP, T, fs), BF16),
                   jax.ShapeDtypeStruct((N_CHIP, T, fs), BF16), jax.ShapeDtypeStruct((T, D), F32)],
        scratch_shapes=[pltpu.VMEM((tm, D), F32)], compiler_params=_params("parallel", "arbitrary"),
    )(x1, g_row, wg, wu, wd)


def _final_loss(x, g_row, target, tm):
    T, D = x.shape
    n = T // tm

    def body(x_ref, g_ref, t_ref, dx_ref, red_ref, racc):
        i = pl.program_id(0)

        @pl.when(i == 0)
        def _():
            racc[...] = jnp.zeros_like(racc)

        dx_ref[...], sq, dg = _loss_tile(x_ref[...], g_ref[...], t_ref[...])
        racc[0] += sq
        racc[1] += dg

        @pl.when(i == n - 1)
        def _():
            red_ref[pl.ds(0, 1), :] = jnp.sum(racc[0], axis=0, keepdims=True) * (0.5 / D)
            red_ref[pl.ds(1, 1), :] = jnp.sum(racc[1], axis=0, keepdims=True)

    tok = pl.BlockSpec((tm, D), lambda i: (i, 0))
    return pl.pallas_call(
        body, name="final_loss", grid=(n,),
        in_specs=[tok, pl.BlockSpec((1, D), lambda i: (0, 0)), tok],
        out_specs=[tok, pl.BlockSpec((2, D), lambda i: (0, 0))],
        out_shape=[jax.ShapeDtypeStruct((T, D), F32), jax.ShapeDtypeStruct((2, D), F32)],
        scratch_shapes=[pltpu.VMEM((2, SUBLANES, D), F32)],
        compiler_params=_params("arbitrary"),
    )(x, g_row, target)


def _ffn_bwd(dx2, x1, g_row, gg, uu, wg, wu, wd, layer, tm, comm=None):
    T, D = dx2.shape
    fs = wg.shape[-2]
    n = T // tm
    nt = (((1,), (1,)), ((), ()))

    def body(dx_ref, x_ref, g_ref, gg_ref, uu_ref, wg_ref, wu_ref, wd_ref, dg_ref, du_ref, act_ref, dx1_ref, dxb_ref, red_ref,
             acc, racc):
        i = pl.program_id(0)
        j = pl.program_id(1)

        @pl.when((i == 0) & (j == 0))
        def _():
            racc[...] = jnp.zeros_like(racc)

        @pl.when(j == 0)
        def _():
            dxb_ref[...] = dx_ref[...].astype(BF16)
            acc[...] = jnp.zeros_like(acc)

        dact = lax.dot_general(dxb_ref[...], wd_ref[j], nt, preferred_element_type=F32)
        g = gg_ref[...].astype(F32)
        u = uu_ref[...].astype(F32)
        s = _sigmoid(g)
        silu = g * s
        dg = (dact * u * (s * (1.0 + g * (1.0 - s)))).astype(BF16)
        du = (dact * silu).astype(BF16)
        dg_ref[...] = dg
        du_ref[...] = du
        act_ref[...] = (silu * u).astype(BF16)
        acc[...] += (jnp.dot(dg, wg_ref[j], preferred_element_type=F32)
                     + jnp.dot(du, wu_ref[j], preferred_element_type=F32))

        @pl.when(j == N_CHIP - 1)
        def _():
            dx, dgain = _rms_bwd(acc[...], x_ref[...], g_ref[...])
            dx1_ref[...] = dx_ref[...] + dx
            racc[...] += _colsum8(dgain)

        @pl.when((i == n - 1) & (j == N_CHIP - 1))
        def _():
            red_ref[...] = jnp.sum(racc[...], axis=0, keepdims=True)

    tok = pl.BlockSpec((tm, D), lambda i, j: (i, 0))
    cm = pl.BlockSpec((None, tm, fs), lambda i, j: (j, i, 0))
    cms = jax.ShapeDtypeStruct((N_CHIP, T, fs), BF16)
    resident = pl.BlockSpec((N_CHIP, fs, D), lambda i, j: (0, 0, 0), pipeline_mode=pl.Buffered(1))
    return _call(
        body, comm, name=f"ffn_bwd_{layer}", grid=(n, N_CHIP),
        in_specs=[tok, tok, pl.BlockSpec((1, D), lambda i, j: (0, 0)), cm, cm, resident, resident, resident],
        out_specs=[cm, cm, cm, tok, tok, pl.BlockSpec((1, D), lambda i, j: (0, 0))],
        out_shape=[cms, cms, cms, jax.ShapeDtypeStruct((T, D), F32), jax.ShapeDtypeStruct((T, D), BF16),
                   jax.ShapeDtypeStruct((1, D), F32)],
        scratch_shapes=[pltpu.VMEM((tm, D), F32), pltpu.VMEM((SUBLANES, D), F32)],
        compiler_params=_params("arbitrary", "arbitrary"),
    )(dx2, x1, g_row, gg, uu, wg, wu, wd)


def _merge_bwd(dx1, p, oa, ob, woa, wob, wo, gbias, layer, tm, comm=None):
    T, D = dx1.shape
    n = T // tm
    nt = (((1,), (1,)), ((), ()))

    def body(dx_ref, ga_ref, gb_ref, oa_ref, ob_ref, woa_ref, wob_ref, wo_ref, bias_ref,
             dya_ref, dyb_ref, doa_ref, dob_ref, dgl_ref, dxb_ref, red_ref, racc):
        i = pl.program_id(0)

        @pl.when(i == 0)
        def _():
            racc[...] = jnp.zeros_like(racc)

        dxb = dx_ref[...].astype(BF16)
        dxb_ref[...] = dxb
        dm = lax.dot_general(dxb, wo_ref[...], nt, preferred_element_type=F32)
        sa = _sigmoid(ga_ref[...].astype(F32) + bias_ref[pl.ds(0, 1), :])
        sb = _sigmoid(gb_ref[...].astype(F32) + bias_ref[pl.ds(1, 1), :])
        doa = (dm * sa).astype(BF16)
        dob = (dm * sb).astype(BF16)
        dga = dm * oa_ref[...].astype(F32) * (sa * (1.0 - sa))
        dgb = dm * ob_ref[...].astype(F32) * (sb * (1.0 - sb))
        doa_ref[...] = doa
        dob_ref[...] = dob
        dgl_ref[:, 0:D] = dga.astype(BF16)
        dgl_ref[:, D:2 * D] = dgb.astype(BF16)
        racc[0] += _colsum8(dga)
        racc[1] += _colsum8(dgb)
        dya_ref[...] = lax.dot_general(doa, woa_ref[...], nt, preferred_element_type=F32).astype(BF16)
        dyb_ref[...] = lax.dot_general(dob, wob_ref[...], nt, preferred_element_type=F32).astype(BF16)

        @pl.when(i == n - 1)
        def _():
            red_ref[pl.ds(0, 1), :] = jnp.sum(racc[0], axis=0, keepdims=True)
            red_ref[pl.ds(1, 1), :] = jnp.sum(racc[1], axis=0, keepdims=True)

    tok = pl.BlockSpec((tm, D), lambda i: (i, 0))
    wsp = pl.BlockSpec((D, D), lambda i: (0, 0))
    bf = jax.ShapeDtypeStruct((T, D), BF16)
    return _call(
        body, comm, name=f"merge_bwd_{layer}", grid=(n,),
        in_specs=[tok, pl.BlockSpec((tm, D), lambda i: (i, 5)), pl.BlockSpec((tm, D), lambda i: (i, 6)), tok, tok,
                  wsp, wsp, wsp, pl.BlockSpec(gbias.shape, lambda i: (0, 0))],
        out_specs=[tok, tok, tok, tok, pl.BlockSpec((tm, 2 * D), lambda i: (i, 0)), tok, pl.BlockSpec((2, D), lambda i: (0, 0))],
        out_shape=[bf, bf, bf, bf, jax.ShapeDtypeStruct((T, 2 * D), BF16), bf, jax.ShapeDtypeStruct((2, D), F32)],
        scratch_shapes=[pltpu.VMEM((2, SUBLANES, D), F32)],
        compiler_params=_params("arbitrary"),
    )(dx1, p, p, oa, ob, woa, wob, wo, gbias)


N_MIXER_RED = 16


def _mixer_bwd(p, hseq, dya, dyb, dgl, cw, wab, wabt, lam_row, layer, tm, comm=None):
    T = p.shape[0]
    D = p.shape[1] // 7
    n = T // tm
    ngroups = tm // SUBLANES
    nb = wab.shape[0]
    hb = 16
    tn = (((0,), (0,)), ((), ()))

    def body(ba_ref, ca_ref, xa_ref, xb_ref, gb_ref, h_ref, dya_ref, dyb_ref, dgl_ref,
             cap_ref, xap_ref, xbp_ref, hp_ref, ban_ref, dyan_ref,
             cw_ref, wab_ref, wabt_ref, lam_ref,
             dp_ref, red_ref, dwab_ref,
             racc, wacc, anext, gnext, dunext, c_s, g_s):
        i = pl.program_id(0)
        first_tile = i == n - 1
        last_tile = i == 0

        @pl.when(i == 0)
        def _():
            racc[...] = jnp.zeros_like(racc)
            wacc[...] = jnp.zeros_like(wacc)
            anext[...] = jnp.zeros_like(anext)
            gnext[...] = jnp.zeros_like(gnext)
            dunext[...] = jnp.zeros_like(dunext)

        keep_prev = jnp.where(first_tile, 0.0, 1.0)
        keep_next = jnp.where(last_tile, 0.0, 1.0)
        ba = ba_ref[...].astype(F32)
        ca = ca_ref[...].astype(F32)
        xa = xa_ref[...].astype(F32)
        xb = xb_ref[...].astype(F32)
        h = h_ref[...]
        dya = dya_ref[...].astype(F32)
        dyb = dyb_ref[...].astype(F32)
        zprev8 = (cap_ref[...].astype(F32) * xap_ref[...].astype(F32))[hb - SUBLANES:] * keep_prev
        xbprev8 = xbp_ref[...].astype(F32)[hb - SUBLANES:] * keep_prev
        hprev8 = hp_ref[...] * keep_prev
        dcznext8 = (dyan_ref[...].astype(F32) * ban_ref[...].astype(F32))[:SUBLANES] * keep_next

        lam = lam_ref[...]
        sp = _softplus_neg(lam)
        c = _mixer_recompute(ca, xa, xb, zprev8, xbprev8, cw_ref, wab_ref, sp)
        row = lambda k: cw_ref[pl.ds(k, 1), :]
        a, m, r, gi, u = c["a"], c["m"], c["r"], c["gi"], c["u"]

        gelu, dgelu = _gelu_parts(gb_ref[...].astype(F32))
        dgb = dyb * h * dgelu
        c_s[...], g_s[...] = _group_scan(_shift_up(a, 1, anext[...]), dyb * gelu, reverse=True)

        def step(k, carry):
            off = pl.multiple_of((ngroups - 1 - k) * SUBLANES, SUBLANES)
            gg = g_s[pl.ds(off, SUBLANES), :] + c_s[pl.ds(off, SUBLANES), :] * carry
            g_s[pl.ds(off, SUBLANES), :] = gg
            return jnp.broadcast_to(gg[0:1, :], gg.shape)

        gnext[...] = lax.fori_loop(0, ngroups, step, gnext[...], unroll=4)
        anext[...] = a[0:SUBLANES]
        g = g_s[...]

        hprev = _shift_down(h, 1, hprev8)
        da = g * hprev
        gm = g * m
        dgi = gm * u
        du = gm * gi
        dmv = g * gi * u
        dla = a * (da - dmv * a / m)
        dra = dla * ((-LRU_C) * sp) * (r * (1.0 - r))
        dix = dgi * (gi * (1.0 - gi))
        racc[10] += _colsum8(dla * r)
        racc[8] += _colsum8(dra)
        racc[9] += _colsum8(dix)
        drab = dra.astype(BF16)
        dixb = dix.astype(BF16)
        ub = c["ub"]
        dus = []
        for b in range(nb):
            sl = slice(b * LRU_BLOCK, (b + 1) * LRU_BLOCK)
            dri = jnp.concatenate([drab[:, sl], dixb[:, sl]], axis=1)
            dus.append(jnp.dot(dri, wabt_ref[b], preferred_element_type=F32))
            wacc[b] += lax.dot_general(ub[:, sl], dri, tn, preferred_element_type=F32)
        du = du + jnp.concatenate(dus, axis=1)

        dun = dunext[...]
        du1 = _shift_up(du, 1, dun)
        du2 = _shift_up(du, 2, dun)
        du3 = _shift_up(du, 3, dun)
        dxb = row(6) * du + row(5) * du1 + row(4) * du2 + row(3) * du3
        dunext[...] = du[0:SUBLANES]
        racc[6] += _colsum8(du * xb)
        racc[5] += _colsum8(du * c["x1"])
        racc[4] += _colsum8(du * c["x2"])
        racc[3] += _colsum8(du * c["x3"])
        racc[7] += _colsum8(du)

        dba = dya * c["cz"]
        dcz = dya * ba
        dcz1 = _shift_up(dcz, 1, dcznext8)
        dcz2 = _shift_up(dcz, 2, dcznext8)
        dz = row(2) * dcz + row(1) * dcz1 + row(0) * dcz2
        racc[2] += _colsum8(dcz * c["z"])
        racc[1] += _colsum8(dcz * c["z1"])
        racc[0] += _colsum8(dcz * c["z2"])

        dp_ref[:, 0:D] = dba.astype(BF16)
        dp_ref[:, D:2 * D] = (dz * xa).astype(BF16)
        dp_ref[:, 2 * D:3 * D] = (dz * ca).astype(BF16)
        dp_ref[:, 3 * D:4 * D] = dxb.astype(BF16)
        dp_ref[:, 4 * D:5 * D] = dgb.astype(BF16)
        dp_ref[:, 5 * D:7 * D] = dgl_ref[...]

        @pl.when(i == n - 1)
        def _():
            dlam_scale = LRU_C * _sigmoid(-lam)
            for k in range(N_MIXER_RED):
                tot = jnp.sum(racc[k], axis=0, keepdims=True)
                red_ref[pl.ds(k, 1), :] = tot * dlam_scale if k == 10 else tot
            dwab_ref[...] = wacc[...]

    rt = lambda i: n - 1 - i
    col = lambda k: pl.BlockSpec((tm, D), lambda i: (rt(i), k))
    tok = pl.BlockSpec((tm, D), lambda i: (rt(i), 0))
    full = lambda a: pl.BlockSpec(a.shape, lambda i: (0,) * a.ndim)
    prev16 = lambda k: pl.BlockSpec((hb, D), lambda i: (jnp.maximum(rt(i) * (tm // hb) - 1, 0), k))
    next16 = lambda k: pl.BlockSpec((hb, D), lambda i: (jnp.minimum((rt(i) + 1) * (tm // hb), T // hb - 1), k))
    hprev = pl.BlockSpec((SUBLANES, D), lambda i: (jnp.maximum(rt(i) * ngroups - 1, 0), 0))
    return _call(
        body, comm, name=f"mixer_bwd_{layer}", grid=(n,),
        in_specs=[col(0), col(1), col(2), col(3), col(4), tok, tok, tok, pl.BlockSpec((tm, 2 * D), lambda i: (rt(i), 0)),
                  prev16(1), prev16(2), prev16(3), hprev, next16(0), next16(0),
                  full(cw), full(wab), full(wabt), full(lam_row)],
        out_specs=[pl.BlockSpec((tm, 7 * D), lambda i: (rt(i), 0)),
                   pl.BlockSpec((N_MIXER_RED, D), lambda i: (0, 0)),
                   pl.BlockSpec((nb, LRU_BLOCK, 2 * LRU_BLOCK), lambda i: (0, 0, 0))],
        out_shape=[jax.ShapeDtypeStruct((T, 7 * D), BF16), jax.ShapeDtypeStruct((N_MIXER_RED, D), F32),
                   jax.ShapeDtypeStruct((nb, LRU_BLOCK, 2 * LRU_BLOCK), F32)],
        scratch_shapes=[pltpu.VMEM((N_MIXER_RED, SUBLANES, D), F32), pltpu.VMEM((nb, LRU_BLOCK, 2 * LRU_BLOCK), F32),
                        pltpu.VMEM((SUBLANES, D), F32), pltpu.VMEM((SUBLANES, D), F32), pltpu.VMEM((SUBLANES, D), F32),
                        pltpu.VMEM((tm, D), F32), pltpu.VMEM((tm, D), F32)],
        compiler_params=_params("arbitrary"),
    )(p, p, p, p, p, hseq, dya, dyb, dgl, p, p, p, hseq, p, dya, cw, wab, wabt, lam_row)


def _inproj_bwd(dp, dx1, x, g_row, win, layer, tm, comm=None):
    T, D = x.shape
    ns = win.shape[-1]
    n = T // tm
    nt = (((1,), (1,)), ((), ()))

    def body(dp_ref, dx_ref, x_ref, g_ref, w_ref, dx0_ref, red_ref, acc, racc):
        i = pl.program_id(0)
        j = pl.program_id(1)

        @pl.when((i == 0) & (j == 0))
        def _():
            racc[...] = jnp.zeros_like(racc)

        @pl.when(j == 0)
        def _():
            acc[...] = jnp.zeros_like(acc)

        acc[...] += lax.dot_general(dp_ref[...], w_ref[...], nt, preferred_element_type=F32)

        @pl.when(j == N_CHIP - 1)
        def _():
            dx, dgain = _rms_bwd(acc[...], x_ref[...], g_ref[...])
            dx0_ref[...] = dx_ref[...] + dx
            racc[...] += _colsum8(dgain)

        @pl.when((i == n - 1) & (j == N_CHIP - 1))
        def _():
            red_ref[...] = jnp.sum(racc[...], axis=0, keepdims=True)

    tok = pl.BlockSpec((tm, D), lambda i, j: (i, 0))
    return _call(
        body, comm, name=f"inproj_bwd_{layer}", grid=(n, N_CHIP),
        in_specs=[pl.BlockSpec((tm, ns), lambda i, j: (i, j)), tok, tok, pl.BlockSpec((1, D), lambda i, j: (0, 0)),
                  pl.BlockSpec((None, D, ns), lambda i, j: (j, 0, 0))],
        out_specs=[tok, pl.BlockSpec((1, D), lambda i, j: (0, 0))],
        out_shape=[jax.ShapeDtypeStruct((T, D), F32), jax.ShapeDtypeStruct((1, D), F32)],
        scratch_shapes=[pltpu.VMEM((tm, D), F32), pltpu.VMEM((SUBLANES, D), F32)],
        compiler_params=_params("arbitrary", "arbitrary"),
    )(dp, dx1, x, g_row, win)


def _wgrad(a, b, name, tk, a_kind="whole", b_kind="whole", nj=1, comm=None):
    T = a.shape[-2]
    width = lambda v, kind: v.shape[-1] // nj if kind == "cols" else v.shape[-1]
    ka, kb = width(a, a_kind), width(b, b_kind)
    nt = T // tk
    tn = (((0,), (0,)), ((), ()))

    def spec(k, kind):
        if kind == "cm":
            return pl.BlockSpec((None, tk, k), lambda j, t: (j, t, 0))
        if kind == "cols":
            return pl.BlockSpec((tk, k), lambda j, t: (t, j))
        return pl.BlockSpec((tk, k), lambda j, t: (t, 0))

    def body(a_ref, b_ref, o_ref, ob_ref):
        t = pl.program_id(1)

        @pl.when(t == 0)
        def _():
            o_ref[...] = jnp.zeros_like(o_ref)

        o_ref[...] += lax.dot_general(a_ref[...], b_ref[...], tn, preferred_element_type=F32)

        @pl.when(t == nt - 1)
        def _():
            ob_ref[...] = o_ref[...].astype(BF16)

    o_spec = pl.BlockSpec((None, ka, kb), lambda j, t: (j, 0, 0))
    return _call(
        body, comm, name=name, grid=(nj, nt),
        in_specs=[spec(ka, a_kind), spec(kb, b_kind)], out_specs=[o_spec, o_spec],
        out_shape=[jax.ShapeDtypeStruct((nj, ka, kb), F32), jax.ShapeDtypeStruct((nj, ka, kb), BF16)],
        compiler_params=_params("parallel", "arbitrary"),
    )(a, b)


def _wgrad_pair(a, b1, b2, name, tk, comm=None):
    T, ka = a.shape
    nj, _, kb = b1.shape
    nt = T // tk
    tn = (((0,), (0,)), ((), ()))

    def body(a_ref, b1_ref, b2_ref, o1_ref, o1b_ref, o2_ref, o2b_ref):
        t = pl.program_id(1)

        @pl.when(t == 0)
        def _():
            o1_ref[...] = jnp.zeros_like(o1_ref)
            o2_ref[...] = jnp.zeros_like(o2_ref)

        av = a_ref[...]
        o1_ref[...] += lax.dot_general(b1_ref[...], av, tn, preferred_element_type=F32)
        o2_ref[...] += lax.dot_general(b2_ref[...], av, tn, preferred_element_type=F32)

        @pl.when(t == nt - 1)
        def _():
            o1b_ref[...] = o1_ref[...].astype(BF16)
            o2b_ref[...] = o2_ref[...].astype(BF16)

    b_spec = pl.BlockSpec((None, tk, kb), lambda j, t: (j, t, 0))
    o_spec = pl.BlockSpec((None, kb, ka), lambda j, t: (j, 0, 0))
    f32 = jax.ShapeDtypeStruct((nj, kb, ka), F32)
    b16 = jax.ShapeDtypeStruct((nj, kb, ka), BF16)
    return _call(
        body, comm, name=name, grid=(nj, nt),
        in_specs=[pl.BlockSpec((tk, ka), lambda j, t: (t, 0)), b_spec, b_spec], out_specs=[o_spec] * 4,
        out_shape=[f32, b16, f32, b16], compiler_params=_params("parallel", "arbitrary"),
    )(a, b1, b2)


def _block_diag(w):
    hb = LRU_BLOCK // LRU_HEAD_DIM
    nb = w.shape[0] // hb
    w4 = w.reshape(nb, hb, LRU_HEAD_DIM, LRU_HEAD_DIM)
    eye = jnp.eye(hb, dtype=w.dtype)
    return jnp.einsum("bide,ij->bidje", w4, eye).reshape(nb, LRU_BLOCK, LRU_BLOCK)


def _diag_heads(m):
    hb = LRU_BLOCK // LRU_HEAD_DIM
    nb = m.shape[0]
    m5 = m.reshape(nb, hb, LRU_HEAD_DIM, hb, LRU_HEAD_DIM)
    eye = jnp.eye(hb, dtype=m.dtype)
    return jnp.einsum("bidje,ij->bide", m5, eye).reshape(nb * hb, LRU_HEAD_DIM, LRU_HEAD_DIM)


def _tiles(T):
    cap = lambda n: min(n, T)
    return dict(inproj=cap(1024), mixer=cap(256), merge=cap(512), ffn=cap(1024), ffn_bwd=cap(512), loss=cap(512), inproj_bwd=cap(1024),
                wgrad_in=cap(2048), wgrad=cap(2048))


class _NoSchedule:
    def carry(self, name):
        return None

    def after(self, name):
        pass

    def grad(self, key, layer, f32, b16):
        pass


def _local_step(x, target, W, small, tiles, sched):
    L = small["ln1_g"].shape[0]
    D = x.shape[1]
    square = lambda a: a.reshape(D, D)
    saved = []
    h = x
    for l in range(L):
        cw = jnp.concatenate([small["conv_a_w"][l], small["conv_b_w"][l], small["conv_b_b"][l][None],
                              small["lru_ba"][l][None], small["lru_bx"][l][None]], axis=0)
        wab = jnp.concatenate([_block_diag(small["lru_wa"][l]), _block_diag(small["lru_wx"][l])], axis=2).astype(BF16)
        wabt = jnp.swapaxes(wab, 1, 2)
        lam_row = small["lru_lambda"][l][None]
        ln1_row = small["ln1_g"][l][None]
        ln2_row = small["ln2_g"][l][None]
        p, h1 = _rms_inproj(h, ln1_row, W["win", l], l, tiles["inproj"], sched.carry(f"rms_inproj_{l}"))
        ya, yb, hseq = _mixer_fwd(p, cw, wab, lam_row, l, tiles["mixer"], sched.carry(f"mixer_fwd_{l}"))
        oa, ob, mg, x1 = _merge_fwd(h, p, ya, yb, square(W["woa", l]), square(W["wob", l]), square(W["wo", l]),
                                    small["gate_bias"][l], l, tiles["merge"], sched.carry(f"merge_fwd_{l}"))
        h2, gg, uu, x2 = _ffn_fwd(x1, ln2_row, W["wg", l], W["wu", l], W["wd", l], l, tiles["ffn"], sched.carry(f"ffn_fwd_{l}"))
        saved.append(dict(x0=h, p=p, h1=h1, ya=ya, yb=yb, hseq=hseq, oa=oa, ob=ob, mg=mg, x1=x1, h2=h2, gg=gg, uu=uu,
                          cw=cw, wab=wab, wabt=wabt, lam_row=lam_row, ln1_row=ln1_row, ln2_row=ln2_row))
        h = x2

    dx, red = _final_loss(h, small["final_g"][None], target, tiles["loss"])
    loss_row, d_final_g = red[0], red[1]

    gsmall = {k: [None] * L for k in ("ln1_g", "ln2_g", "conv_a_w", "conv_b_w", "conv_b_b", "lru_wa", "lru_ba", "lru_wx",
                                      "lru_bx", "lru_lambda", "gate_bias")}
    tk = tiles["wgrad"]
    for l in reversed(range(L)):
        s = saved[l]
        dgg, duu, act, dx1, dx2b, dln2 = _ffn_bwd(dx, s["x1"], s["ln2_row"], s["gg"], s["uu"], W["wg", l], W["wu", l], W["wd", l],
                                                 l, tiles["ffn_bwd"], sched.carry(f"ffn_bwd_{l}"))
        sched.after(f"ffn_bwd_{l}")
        gate_up = _wgrad_pair(s["h2"], dgg, duu, f"wgrad_ffn_gate_up_{l}", tk, sched.carry(f"wgrad_ffn_gate_up_{l}"))
        sched.grad("wg", l, *gate_up[0:2])
        sched.grad("wu", l, *gate_up[2:4])
        sched.after(f"wgrad_ffn_gate_up_{l}")
        sched.grad("wd", l, *_wgrad(act, dx2b, f"wgrad_ffn_down_{l}", tk, "cm", "whole", N_CHIP, sched.carry(f"wgrad_ffn_down_{l}")))
        dya, dyb, doa, dob, dgl, dx1b, dgbias = _merge_bwd(dx1, s["p"], s["oa"], s["ob"], square(W["woa", l]), square(W["wob", l]),
                                                         square(W["wo", l]), small["gate_bias"][l], l, tiles["merge"],
                                                         sched.carry(f"merge_bwd_{l}"))
        sched.after(f"merge_bwd_{l}")
        sched.grad("wo", l, *_wgrad(s["mg"], dx1b, f"wgrad_w_o_{l}", tk))
        sched.grad("woa", l, *_wgrad(s["ya"], doa, f"wgrad_w_out_a_{l}", tk))
        sched.grad("wob", l, *_wgrad(s["yb"], dob, f"wgrad_w_out_b_{l}", tk))
        dp, mred, dwab = _mixer_bwd(s["p"], s["hseq"], dya, dyb, dgl, s["cw"], s["wab"], s["wabt"], s["lam_row"], l,
                                    tiles["mixer"], sched.carry(f"mixer_bwd_{l}"))
        sched.after(f"mixer_bwd_{l}")
        sched.grad("win", l, *_wgrad(s["h1"], dp, f"wgrad_w_in_{l}", tiles["wgrad_in"], "whole", "cols", N_CHIP,
                                     sched.carry(f"wgrad_w_in_{l}")))
        sched.after(f"wgrad_w_in_{l}")
        dx, dln1 = _inproj_bwd(dp, dx1, s["x0"], s["ln1_row"], W["win", l], l, tiles["inproj_bwd"], sched.carry(f"inproj_bwd_{l}"))
        sched.after(f"inproj_bwd_{l}")
        gsmall["ln1_g"][l] = dln1[0]
        gsmall["ln2_g"][l] = dln2[0]
        gsmall["conv_a_w"][l] = mred[0:CONV_A_K]
        gsmall["conv_b_w"][l] = mred[CONV_A_K:CONV_A_K + CONV_B_K]
        gsmall["conv_b_b"][l] = mred[7]
        gsmall["lru_ba"][l] = mred[8]
        gsmall["lru_bx"][l] = mred[9]
        gsmall["lru_lambda"][l] = mred[10]
        gsmall["lru_wa"][l] = _diag_heads(dwab[:, :, :LRU_BLOCK])
        gsmall["lru_wx"][l] = _diag_heads(dwab[:, :, LRU_BLOCK:])
        gsmall["gate_bias"][l] = dgbias
    gsmall = {k: jnp.stack(v) for k, v in gsmall.items()}
    gsmall["final_g"] = d_final_g
    return loss_row, dx, gsmall


def _small_allreduce(buf):
    R, C = buf.shape
    n_dev = 8
    rp = R // n_dev
    rel = [(k >> 2 & 1, k >> 1 & 1, k & 1) for k in range(1, n_dev)]

    def body(in_ref, out_ref, recv, s1, r1, s2, r2):
        x, y, c, _ = _place()
        flip = lambda v, bit: 1 - v if bit else v
        peers = [(flip(x, kx), flip(y, ky), flip(c, kc)) for kx, ky, kc in rel]
        dev = lambda p: 4 * p[0] + 2 * p[1] + p[2]
        part = lambda ref, d: ref.at[pl.ds(pl.multiple_of(d * rp, SUBLANES), rp), :]
        me = dev((x, y, c))

        def scatter(k, src_dev, to):
            return pltpu.make_async_remote_copy(src_ref=part(in_ref, dev(to)), dst_ref=recv.at[src_dev], send_sem=s1.at[k],
                                                recv_sem=r1.at[k], device_id=to, device_id_type=MESH)

        def gather(k, src_dev, to):
            return pltpu.make_async_remote_copy(src_ref=part(out_ref, src_dev), dst_ref=part(out_ref, src_dev), send_sem=s2.at[k],
                                                recv_sem=r2.at[k], device_id=to, device_id_type=MESH)

        first = [scatter(k, me, p) for k, p in enumerate(peers)]
        for cp in first:
            cp.start()
        recv[me] = part(in_ref, me)[...]
        for k, p in enumerate(peers):
            scatter(k, dev(p), (x, y, c)).wait_recv()
        total = recv[0]
        for d in range(1, n_dev):
            total = total + recv[d]
        part(out_ref, me)[...] = total
        second = [gather(k, me, p) for k, p in enumerate(peers)]
        for cp in second:
            cp.start()
        for k, p in enumerate(peers):
            gather(k, dev(p), (x, y, c)).wait_recv()
        for cp in first + second:
            cp.wait_send()

    dma = pltpu.SemaphoreType.DMA
    vm = pl.BlockSpec(memory_space=pltpu.VMEM)
    return pl.pallas_call(
        body, name="small_allreduce", out_shape=jax.ShapeDtypeStruct((R, C), buf.dtype),
        in_specs=[vm], out_specs=vm,
        scratch_shapes=[pltpu.VMEM((n_dev, rp, C), buf.dtype), dma((n_dev - 1,)), dma((n_dev - 1,)), dma((n_dev - 1,)), dma((n_dev - 1,))],
    )(buf)


ELEMENTWISE_BLOCK_BYTES = 2 * 1024 * 1024


def _row_block(k, n):
    best = None
    for b in range(16, k + 1, 16):
        if k % b == 0 and b * n * 4 <= ELEMENTWISE_BLOCK_BYTES:
            best = b
    return best or k


def _add_halves(g, recv, place_arr, name):
    nj, hk, N = recv.shape
    bk = _row_block(hk, N)
    nb = hk // bk

    def body(k_ref, g_ref, r_ref, o_ref, ob_ref):
        s = g_ref[...] + r_ref[...].astype(F32)
        ob_ref[...] = s.astype(BF16)

        @pl.when(pl.program_id(1) == k_ref[0])
        def _():
            o_ref[...] = s

    blk = pl.BlockSpec((None, bk, N), lambda i, j, k_ref: (j, i, 0))
    grid_spec = pltpu.PrefetchScalarGridSpec(
        num_scalar_prefetch=1, grid=(nb, nj),
        in_specs=[pl.BlockSpec((None, bk, N), lambda i, j, k_ref: (j, k_ref[1] * nb + i, 0)), blk],
        out_specs=[pl.BlockSpec((bk, N), lambda i, j, k_ref: (i, 0)), blk])
    return pl.pallas_call(
        body, name=name, grid_spec=grid_spec,
        out_shape=[jax.ShapeDtypeStruct((hk, N), F32), jax.ShapeDtypeStruct((nj, hk, N), BF16)],
        compiler_params=_params("parallel", "arbitrary"),
    )(place_arr, g, recv)


def _add_chips(pc, recv, place_arr, layer, n_layers, prev, name):
    hk, N = pc.shape
    bk = _row_block(hk, N)
    nb = hk // bk

    def body(k_ref, p_ref, r0_ref, r1_ref, r2_ref, *rest):
        o_ref = rest[-1]
        o_ref[...] = ((p_ref[...] + r0_ref[...].astype(F32)) + r1_ref[...].astype(F32)) + r2_ref[...].astype(F32)

    rspec = lambda j: pl.BlockSpec((None, bk, N), lambda i, k_ref: (j, i, 0))
    in_specs = [pl.BlockSpec((bk, N), lambda i, k_ref: (i, 0)), rspec(0), rspec(1), rspec(2)]
    operands = [pc, recv, recv, recv]
    aliases = {}
    if prev is not None:
        in_specs.append(ANY)
        operands.append(prev)
        aliases = {5: 0}
    grid_spec = pltpu.PrefetchScalarGridSpec(
        num_scalar_prefetch=1, grid=(nb,), in_specs=in_specs,
        out_specs=pl.BlockSpec((None, bk, N), lambda i, k_ref: (layer, k_ref[1] * nb + i, 0)))
    return pl.pallas_call(
        body, name=name, grid_spec=grid_spec, out_shape=jax.ShapeDtypeStruct((n_layers, 2 * hk, N), F32),
        input_output_aliases=aliases, compiler_params=_params("parallel"),
    )(place_arr, *operands)


def _adamw_math(w, g, m, v):
    m = ADAM_B1 * m + (1.0 - ADAM_B1) * g
    v = ADAM_B2 * v + (1.0 - ADAM_B2) * (g * g)
    m_hat = m / (1.0 - ADAM_B1 ** ADAM_STEP)
    v_hat = v / (1.0 - ADAM_B2 ** ADAM_STEP)
    delta = -ADAM_LR * (m_hat / (jnp.sqrt(v_hat) + ADAM_EPS) + ADAM_WD * w)
    return delta, m, v


def _adamw(w, g, m, v, name):
    L, K, N = w.shape
    bk = _row_block(K, N)

    def body(w_ref, g_ref, m_ref, v_ref, d_ref, nm_ref, nv_ref, go_ref):
        g = g_ref[...]
        d_ref[...], nm_ref[...], nv_ref[...] = _adamw_math(w_ref[...], g, m_ref[...], v_ref[...])
        go_ref[...] = g

    blk = pl.BlockSpec((None, bk, N), lambda l, i: (l, i, 0))
    sds = jax.ShapeDtypeStruct((L, K, N), F32)
    return pl.pallas_call(
        body, name=name, grid=(L, K // bk), in_specs=[blk] * 4, out_specs=[blk] * 4, out_shape=[sds] * 4,
        compiler_params=_params("parallel", "parallel"),
    )(w, g, m, v)


def _adamw_small(ws, gs, ms, vs):
    n = len(ws)

    def body(*refs):
        w, g, m, v, d, nm, nv = (refs[k * n:(k + 1) * n] for k in range(7))
        for k in range(n):
            d[k][...], nm[k][...], nv[k][...] = _adamw_math(w[k][...], g[k][...], m[k][...], v[k][...])

    sds = [jax.ShapeDtypeStruct(a.shape, F32) for a in ws]
    out = pl.pallas_call(body, name="adamw_small", out_shape=sds * 3)(*ws, *gs, *ms, *vs)
    return out[:n], out[n:2 * n], out[2 * n:]


def _cast_bf16(w, layer, place_arr, name):
    _, K, N = w.shape
    bk = _row_block(K, N)

    def body(k_ref, w_ref, o_ref):
        o_ref[...] = w_ref[...].astype(BF16)

    grid_spec = pltpu.PrefetchScalarGridSpec(
        num_scalar_prefetch=1, grid=(K // bk,),
        in_specs=[pl.BlockSpec((None, bk, N), lambda i, k_ref: (layer, i, 0))],
        out_specs=pl.BlockSpec((None, bk, N), lambda i, k_ref: (k_ref[0], i, 0)))
    return pl.pallas_call(
        body, name=name, grid_spec=grid_spec, out_shape=jax.ShapeDtypeStruct((N_CHIP, K, N), BF16),
        compiler_params=_params("parallel"),
    )(place_arr, w)


BIG = ("w_in", "w_out_a", "w_out_b", "w_o", "w_ffn_gate", "w_ffn_up", "w_ffn_down")
BIG_KEY = dict(w_in="win", w_out_a="woa", w_out_b="wob", w_o="wo", w_ffn_gate="wg", w_ffn_up="wu", w_ffn_down="wd")
SHARDED_SMALL = ("conv_a_w", "conv_b_w", "gate_bias")
REPLICATED = ("ln1_g", "conv_b_b", "lru_wa", "lru_ba", "lru_wx", "lru_bx", "lru_lambda", "ln2_g", "final_g")
WEIGHTS = ("ln1_g", "w_in", "conv_a_w", "conv_b_w", "conv_b_b", "lru_wa", "lru_ba", "lru_wx", "lru_bx", "lru_lambda",
           "w_out_a", "w_out_b", "gate_bias", "w_o", "ln2_g", "w_ffn_gate", "w_ffn_up", "w_ffn_down", "final_g")
LANES = 1024


def _pack_rows(arrays, row_multiple):
    flat = jnp.concatenate([a.reshape(-1) for a in arrays])
    rows = -(-flat.shape[0] // LANES)
    rows = -(-rows // row_multiple) * row_multiple
    flat = jnp.pad(flat, (0, rows * LANES - flat.shape[0]))
    return flat.reshape(rows, LANES)


def _unpack_rows(buf, shapes):
    flat = buf.reshape(-1)
    out, off = [], 0
    for s in shapes:
        n = 1
        for d in s:
            n *= d
        out.append(flat[off:off + n].reshape(s))
        off += n
    return out


OUT_KEYS = ("wo", "woa", "wob")
FFN_KEYS = ("wg", "wu", "wd")


def _items(keys, layer):
    return [(k, layer) for k in keys]


CARRY = {
    "rms_inproj_0": [("gather_ici", _items(OUT_KEYS + FFN_KEYS, 0))],
    "mixer_fwd_0": [("gather_d2d", _items(OUT_KEYS + FFN_KEYS, 0)), ("gather_ici", _items(("win",) + OUT_KEYS, 1))],
    "merge_fwd_0": [("gather_d2d", _items(("win",) + OUT_KEYS, 1))],
    "ffn_fwd_0": [("gather_ici", _items(FFN_KEYS, 1))],
    "rms_inproj_1": [("gather_d2d", _items(FFN_KEYS, 1))],
    "merge_bwd_1": [("halves", _items(FFN_KEYS, 1))],
    "mixer_bwd_1": [("chips", _items(FFN_KEYS, 1)), ("halves", _items(OUT_KEYS, 1))],
    "inproj_bwd_1": [("chips", _items(OUT_KEYS, 1)), ("share", _items(FFN_KEYS, 1))],
    "ffn_bwd_0": [("halves", [("win", 1)]), ("share", _items(OUT_KEYS, 1))],
    "wgrad_ffn_gate_up_0": [("chips", [("win", 1)])],
    "wgrad_ffn_down_0": [("share", [("win", 1)])],
    "merge_bwd_0": [("halves", _items(FFN_KEYS, 0))],
    "mixer_bwd_0": [("chips", _items(FFN_KEYS, 0)), ("halves", _items(OUT_KEYS, 0))],
    "wgrad_w_in_0": [("chips", _items(OUT_KEYS, 0)), ("share", _items(FFN_KEYS, 0))],
    "inproj_bwd_0": [("chips", [("win", 0)])],
}
AFTER = {
    "merge_bwd_1": [("add_halves", _items(FFN_KEYS, 1))],
    "mixer_bwd_1": [("add_chips", _items(FFN_KEYS, 1)), ("add_halves", _items(OUT_KEYS, 1))],
    "inproj_bwd_1": [("add_chips", _items(OUT_KEYS, 1))],
    "ffn_bwd_0": [("add_halves", [("win", 1)])],
    "wgrad_ffn_gate_up_0": [("add_chips", [("win", 1)])],
    "merge_bwd_0": [("add_halves", _items(FFN_KEYS, 0))],
    "mixer_bwd_0": [("add_chips", _items(FFN_KEYS, 0)), ("add_halves", _items(OUT_KEYS, 0))],
    "wgrad_w_in_0": [("add_chips", _items(OUT_KEYS, 0)),
                     ("run", ("reduce_halves_w_in_0", [("halves", [("win", 0)]), ("share", _items(OUT_KEYS, 0))])),
                     ("add_halves", [("win", 0)])],
    "inproj_bwd_0": [("add_chips", [("win", 0)]), ("run", ("reduce_share_w_in_0", [("share", [("win", 0)])]))],
}


class _Schedule:
    def __init__(self, slots, place_arr, n_layers):
        self.W = slots
        self.place, self.L = place_arr, n_layers
        self.g32, self.g16 = {}, {}
        self.from_sibling, self.chip_sum, self.chip_sum16, self.from_chips = {}, {}, {}, {}
        self.reduced = {}

    def stage(self, comm, kind, items):
        bf = lambda shape: jax.ShapeDtypeStruct(shape, BF16)
        for it in items:
            if kind == "gather_ici":
                comm.add(_gather_ici, 3, io=[(self.W, it)])
            elif kind == "gather_d2d":
                comm.add(_gather_d2d, 3, io=[(self.W, it)])
            elif kind == "halves":
                nj, K, N = self.g16[it].shape
                comm.add(_reduce_halves, 1, ro=[self.g16[it]], nw=[(self.from_sibling, it, bf((nj, K // 2, N)))])
            elif kind == "chips":
                _, hk, N = self.chip_sum16[it].shape
                comm.add(_reduce_chips, 3, ro=[self.chip_sum16[it]], nw=[(self.from_chips, it, bf((3, hk, N)))])
            elif kind == "share":
                comm.add(_reduce_share(it[1]), 1, io=[(self.reduced, it[0])])
        return comm

    def carry(self, name):
        comm = _Carried()
        for kind, items in CARRY.get(name, ()):
            self.stage(comm, kind, items)
        return comm

    def run(self, name, rounds):
        _run_comm([self.stage(_Carried(), kind, items) for kind, items in rounds], name)

    def grad(self, key, layer, f32, b16):
        by_chip = lambda g: g.reshape(N_CHIP, -1, g.shape[-1])
        self.g32[key, layer], self.g16[key, layer] = by_chip(f32), by_chip(b16)

    def add(self, kind, items):
        for key, layer in items:
            it = (key, layer)
            if kind == "add_halves":
                self.chip_sum[it], self.chip_sum16[it] = _add_halves(self.g32[it], self.from_sibling[it], self.place,
                                                                    f"add_halves_{key}_{layer}")
            else:
                self.reduced[key] = _add_chips(self.chip_sum[it], self.from_chips[it], self.place, layer, self.L,
                                               self.reduced.get(key), f"add_chips_{key}_{layer}")

    def after(self, name):
        for kind, items in AFTER.get(name, ()):
            if kind == "run":
                self.run(*items)
            else:
                self.add(kind, items)


def _step(w, m, v, x, target):
    xi, yi, ci = lax.axis_index("x"), lax.axis_index("y"), lax.axis_index("c")
    chip = _chip_id(xi, yi)
    place_arr = jnp.stack([chip, ci]).astype(jnp.int32)
    L = w["ln1_g"].shape[0]
    assert L == 2
    D = x.shape[1]
    dc = D // N_CHIP

    stored = lambda n, a: jnp.swapaxes(a, 1, 2) if n in ("w_ffn_gate", "w_ffn_up") else a
    slots = {(BIG_KEY[n], l): _cast_bf16(stored(n, w[n]), l, place_arr, f"cast_{n}_{l}") for n in BIG for l in range(L)}
    sched = _Schedule(slots, place_arr, L)
    small_shard = jnp.concatenate([w[n] for n in SHARDED_SMALL], axis=1)
    got = {}
    first = sched.stage(_Carried(), "gather_ici", [("win", 0)])
    first.add(_gather_small, 3, ro=[small_shard], nw=[(got, "small", jax.ShapeDtypeStruct((3,) + small_shard.shape, F32))])
    _run_comm([first, sched.stage(_Carried(), "gather_d2d", [("win", 0)])], "gather_first")
    small_g = jnp.zeros((N_CHIP,) + small_shard.shape, F32)
    small_g = lax.dynamic_update_index_in_dim(small_g, small_shard, chip, 0)
    for j, (cx, cy) in enumerate([(1 - xi, yi), (xi, 1 - yi), (1 - xi, 1 - yi)]):
        small_g = lax.dynamic_update_index_in_dim(small_g, got["small"][j], _chip_id(cx, cy), 0)
    small_full = jnp.transpose(small_g, (1, 2, 0, 3)).reshape(L, small_shard.shape[1], D)
    small = {n: w[n] for n in REPLICATED}
    off = 0
    for n in SHARDED_SMALL:
        k = w[n].shape[1]
        small[n] = small_full[:, off:off + k]
        off += k

    loss_row, grad_x, gsmall = _local_step(x, target, sched.W, small, _tiles(x.shape[0]), sched)

    grads = {}

    order = [n for n in WEIGHTS if n not in BIG]
    packed = _pack_rows([gsmall[n] for n in order] + [loss_row], 8 * SUBLANES)
    summed = _small_allreduce(packed)
    parts = _unpack_rows(summed, [gsmall[n].shape for n in order] + [loss_row.shape])
    loss = jnp.sum(parts[-1])
    for n, g in zip(order, parts[:-1]):
        grads[n] = lax.dynamic_slice_in_dim(g, chip * dc, dc, axis=2) if n in SHARDED_SMALL else g

    delta, new_m, new_v = {}, {}, {}
    for n in BIG:
        d, nm, nv, g = _adamw(stored(n, w[n]), sched.reduced[BIG_KEY[n]], stored(n, m[n]), stored(n, v[n]), f"adamw_{n}")
        delta[n], new_m[n], new_v[n], grads[n] = stored(n, d), stored(n, nm), stored(n, nv), stored(n, g)
    for d, arrays in zip((delta, new_m, new_v), _adamw_small(*([d[n] for n in order] for d in (w, grads, m, v)))):
        d.update(zip(order, arrays))
    return loss, grad_x, grads, delta, new_m, new_v


def kernel(x, ln1_g, w_in, conv_a_w, conv_b_w, conv_b_b, lru_wa, lru_ba, lru_wx, lru_bx, lru_lambda, w_out_a, w_out_b, gate_bias, w_o, ln2_g, w_ffn_gate, w_ffn_up, w_ffn_down, final_g, loss_target, m_ln1_g, m_w_in, m_conv_a_w, m_conv_b_w, m_conv_b_b, m_lru_wa, m_lru_ba, m_lru_wx, m_lru_bx, m_lru_lambda, m_w_out_a, m_w_out_b, m_gate_bias, m_w_o, m_ln2_g, m_w_ffn_gate, m_w_ffn_up, m_w_ffn_down, m_final_g, v_ln1_g, v_w_in, v_conv_a_w, v_conv_b_w, v_conv_b_b, v_lru_wa, v_lru_ba, v_lru_wx, v_lru_bx, v_lru_lambda, v_w_out_a, v_w_out_b, v_gate_bias, v_w_o, v_ln2_g, v_w_ffn_gate, v_w_ffn_up, v_w_ffn_down, v_final_g):
    w = dict(ln1_g=ln1_g, w_in=w_in, conv_a_w=conv_a_w, conv_b_w=conv_b_w, conv_b_b=conv_b_b, lru_wa=lru_wa, lru_ba=lru_ba,
             lru_wx=lru_wx, lru_bx=lru_bx, lru_lambda=lru_lambda, w_out_a=w_out_a, w_out_b=w_out_b, gate_bias=gate_bias, w_o=w_o,
             ln2_g=ln2_g, w_ffn_gate=w_ffn_gate, w_ffn_up=w_ffn_up, w_ffn_down=w_ffn_down, final_g=final_g)
    m = dict(ln1_g=m_ln1_g, w_in=m_w_in, conv_a_w=m_conv_a_w, conv_b_w=m_conv_b_w, conv_b_b=m_conv_b_b, lru_wa=m_lru_wa,
             lru_ba=m_lru_ba, lru_wx=m_lru_wx, lru_bx=m_lru_bx, lru_lambda=m_lru_lambda, w_out_a=m_w_out_a, w_out_b=m_w_out_b,
             gate_bias=m_gate_bias, w_o=m_w_o, ln2_g=m_ln2_g, w_ffn_gate=m_w_ffn_gate, w_ffn_up=m_w_ffn_up,
             w_ffn_down=m_w_ffn_down, final_g=m_final_g)
    v = dict(ln1_g=v_ln1_g, w_in=v_w_in, conv_a_w=v_conv_a_w, conv_b_w=v_conv_b_w, conv_b_b=v_conv_b_b, lru_wa=v_lru_wa,
             lru_ba=v_lru_ba, lru_wx=v_lru_wx, lru_bx=v_lru_bx, lru_lambda=v_lru_lambda, w_out_a=v_w_out_a, w_out_b=v_w_out_b,
             gate_bias=v_gate_bias, w_o=v_w_o, ln2_g=v_ln2_g, w_ffn_gate=v_w_ffn_gate, w_ffn_up=v_w_ffn_up,
             w_ffn_down=v_w_ffn_down, final_g=v_final_g)
    loss, grad_x, grads, delta, new_m, new_v = _step(w, m, v, x[0], loss_target[0])
    return (loss, grad_x[None], *[grads[n] for n in WEIGHTS], *[delta[n] for n in WEIGHTS],
            *[new_m[n] for n in WEIGHTS], *[new_v[n] for n in WEIGHTS])
```

```python
import jax
import jax.numpy as jnp
from jax import lax
from jax.experimental import pallas as pl
from jax.experimental.pallas import tpu as pltpu

F32 = jnp.float32
BF16 = jnp.bfloat16
MESH = pl.DeviceIdType.MESH

N_CHIP = 4
RMS_EPS = 1e-6
LRU_C = 8.0
LRU_HEAD_DIM = 64
LRU_BLOCK = 256
CONV_A_K = 3
CONV_B_K = 4
ADAM_LR = 0.001
ADAM_B1 = 0.9
ADAM_B2 = 0.999
ADAM_EPS = 1e-08
ADAM_WD = 0.01
ADAM_STEP = 10
SUBLANES = 8
VMEM_LIMIT = 56 * 1024 * 1024


def _params(*sem):
    return pltpu.CompilerParams(dimension_semantics=sem, vmem_limit_bytes=VMEM_LIMIT)


def _sigmoid(v):
    return 1.0 / (1.0 + jnp.exp(-v))


def _one_minus_sq(la, a):
    return jnp.tanh(-la) * (1.0 + a * a)


def _gelu_parts(v):
    k = 0.7978845608028654
    v2 = v * v
    t = jnp.tanh(k * (v + 0.044715 * v * v2))
    gelu = 0.5 * v * (1.0 + t)
    dgelu = 0.5 * (1.0 + t) + 0.5 * v * (1.0 - t * t) * k * (1.0 + 3 * 0.044715 * v2)
    return gelu, dgelu


def _shift_down(v, k, prev8):
    rolled = pltpu.roll(v, k, 0)
    r8 = lax.broadcasted_iota(jnp.int32, prev8.shape, 0)
    head = jnp.where(r8 < k, pltpu.roll(prev8, k, 0), rolled[0:SUBLANES])
    return jnp.concatenate([head, rolled[SUBLANES:]], axis=0)


def _shift_up(v, k, next8):
    tm = v.shape[0]
    rolled = pltpu.roll(v, tm - k, 0)
    r8 = lax.broadcasted_iota(jnp.int32, next8.shape, 0)
    tail = jnp.where(r8 >= SUBLANES - k, pltpu.roll(next8, SUBLANES - k, 0), rolled[tm - SUBLANES:])
    return jnp.concatenate([rolled[:tm - SUBLANES], tail], axis=0)


def _group_scan(a, b, reverse):
    tm, c = a.shape
    a = a.reshape(tm // SUBLANES, SUBLANES, c)
    b = b.reshape(tm // SUBLANES, SUBLANES, c)
    q = lax.broadcasted_iota(jnp.int32, a.shape, 1)
    for s in (1, 2, 4):
        msk = q < SUBLANES - s if reverse else q >= s
        shift = SUBLANES - s if reverse else s
        b = jnp.where(msk, a * pltpu.roll(b, shift, 1) + b, b)
        a = jnp.where(msk, a * pltpu.roll(a, shift, 1), a)
    return a.reshape(tm, c), b.reshape(tm, c)


def _colsum8(v):
    tm, c = v.shape
    return jnp.sum(v.reshape(tm // SUBLANES, SUBLANES, c), axis=0)


def _rms_stats(xv):
    var = jnp.mean(xv * xv, axis=-1, keepdims=True)
    return lax.rsqrt(var + RMS_EPS)


def _rms_bwd(dh, xv, g):
    rstd = _rms_stats(xv)
    xhat = xv * rstd
    dxhat = dh * g
    dx = rstd * (dxhat - xhat * jnp.mean(dxhat * xhat, axis=-1, keepdims=True))
    return dx, dh * xhat


ANY = pl.BlockSpec(memory_space=pl.ANY)


def _place():
    x, y, c = lax.axis_index("x"), lax.axis_index("y"), lax.axis_index("c")
    other_chips = [(1 - x, y), (x, 1 - y), (1 - x, 1 - y)]
    return x, y, c, other_chips


def _chip_id(x, y):
    return 2 * x + y


def _half(c, hk):
    return pl.ds(pl.multiple_of(c * hk, 16), hk)


def _remote(src, dst, to, sems):
    return pltpu.make_async_remote_copy(src_ref=src, dst_ref=dst, device_id=to, device_id_type=MESH, **sems)


class _Carried:
    def __init__(self):
        self.ro, self.io, self.nw, self.parts, self.n = [], [], [], [], 0

    def add(self, maker, n, ro=(), io=(), nw=()):
        def index(items, item, same):
            for k, other in enumerate(items):
                if same(other, item):
                    return k
            items.append(item)
            return len(items) - 1

        r = [index(self.ro, a, lambda p, q: p is q) for a in ro]
        i = [index(self.io, a, lambda p, q: p[0] is q[0] and p[1] == q[1]) for a in io]
        w = [index(self.nw, a, lambda p, q: False) for a in nw]
        self.parts.append((maker, r, i, w, self.n))
        self.n += n
        return self

    def pairs(self, ro, io, nw, ssem, rsem):
        out = []
        for maker, r, i, w, base in self.parts:
            sems = lambda k, base=base: dict(send_sem=ssem.at[base + k], recv_sem=rsem.at[base + k])
            out += maker([ro[k] for k in r], [io[k] for k in i], [nw[k] for k in w], sems)
        return out

    def start(self, *refs):
        for send, _ in self.pairs(*refs):
            send.start()

    def finish(self, *refs):
        pairs = self.pairs(*refs)
        for _, recv in pairs:
            recv.wait_recv()
        for send, _ in pairs:
            send.wait_send()

    def operands(self):
        return list(self.ro) + [store[key] for store, key in self.io]

    def out_shapes(self):
        return [jax.ShapeDtypeStruct(store[key].shape, store[key].dtype) for store, key in self.io] + [s for _, _, s in self.nw]

    def keep(self, results):
        for (store, key), arr in zip(self.io, results[:len(self.io)]):
            store[key] = arr
        for (store, key, _), arr in zip(self.nw, results[len(self.io):]):
            store[key] = arr


def _call(body, comm, *, name, grid, in_specs, out_specs, out_shape, compiler_params, scratch_shapes=(), aliases=None):
    aliases = dict(aliases or {})
    if comm is None or not comm.parts:
        return pl.pallas_call(body, name=name, grid=grid, in_specs=in_specs, out_specs=out_specs, out_shape=out_shape,
                              scratch_shapes=list(scratch_shapes), input_output_aliases=aliases, compiler_params=compiler_params)
    n_in, n_out, n_scr = len(in_specs), len(out_shape), len(scratch_shapes)
    n_ro, n_io, n_nw = len(comm.ro), len(comm.io), len(comm.nw)

    def carried(*refs):
        base_in = refs[:n_in]
        ro = refs[n_in:n_in + n_ro]
        pos = n_in + n_ro + n_io
        base_out = refs[pos:pos + n_out]
        io = refs[pos + n_out:pos + n_out + n_io]
        nw = refs[pos + n_out + n_io:pos + n_out + n_io + n_nw]
        pos += n_out + n_io + n_nw
        scr = refs[pos:pos + n_scr]
        ssem, rsem = refs[pos + n_scr], refs[pos + n_scr + 1]
        first = pl.program_id(0) == 0
        last = pl.program_id(0) == grid[0] - 1
        for axis in range(1, len(grid)):
            first = first & (pl.program_id(axis) == 0)
            last = last & (pl.program_id(axis) == grid[axis] - 1)

        @pl.when(first)
        def _():
            comm.start(ro, io, nw, ssem, rsem)

        body(*base_in, *base_out, *scr)

        @pl.when(last)
        def _():
            comm.finish(ro, io, nw, ssem, rsem)

    aliases.update({n_in + n_ro + k: n_out + k for k in range(n_io)})
    dma = pltpu.SemaphoreType.DMA
    call = pl.pallas_call(
        carried, name=name, grid=grid,
        in_specs=list(in_specs) + [ANY] * (n_ro + n_io), out_specs=list(out_specs) + [ANY] * (n_io + n_nw),
        out_shape=list(out_shape) + comm.out_shapes(), input_output_aliases=aliases,
        scratch_shapes=list(scratch_shapes) + [dma((comm.n,)), dma((comm.n,))], compiler_params=compiler_params)

    def run(*operands):
        res = call(*operands, *comm.operands())
        comm.keep(res[n_out:])
        return res[:n_out]

    return run


def _run_comm(rounds, name):
    ro, io, nw, uses = [], [], [], []
    for r in rounds:
        def index(items, item, same):
            for k, other in enumerate(items):
                if same(other, item):
                    return k
            items.append(item)
            return len(items) - 1
        uses.append(([index(ro, a, lambda p, q: p is q) for a in r.ro],
                     [index(io, a, lambda p, q: p[0] is q[0] and p[1] == q[1]) for a in r.io],
                     [index(nw, a, lambda p, q: False) for a in r.nw]))
    n_ro, n_io, n_nw = len(ro), len(io), len(nw)

    def body(*refs):
        ro_refs = refs[:n_ro]
        io_refs = refs[n_ro + n_io:n_ro + 2 * n_io]
        nw_refs = refs[n_ro + 2 * n_io:n_ro + 2 * n_io + n_nw]
        sems = refs[n_ro + 2 * n_io + n_nw:]
        for k, (r, (a, b, c)) in enumerate(zip(rounds, uses)):
            args = ([ro_refs[i] for i in a], [io_refs[i] for i in b], [nw_refs[i] for i in c], sems[2 * k], sems[2 * k + 1])
            r.start(*args)
            r.finish(*args)

    operands = ro + [store[key] for store, key in io]
    out_shape = [jax.ShapeDtypeStruct(store[key].shape, store[key].dtype) for store, key in io] + [s for _, _, s in nw]
    dma = pltpu.SemaphoreType.DMA
    res = pl.pallas_call(
        body, name=name, out_shape=out_shape,
        in_specs=[ANY] * (n_ro + n_io), out_specs=[ANY] * (n_io + n_nw),
        input_output_aliases={n_ro + k: k for k in range(n_io)},
        scratch_shapes=[dma((r.n,)) for r in rounds for _ in range(2)],
    )(*operands)
    for (store, key), arr in zip(io, res[:n_io]):
        store[key] = arr
    for (store, key, _), arr in zip(nw, res[n_io:]):
        store[key] = arr


def _gather_ici(ro, io, nw, sems):
    s = io[0]
    x, y, c, chips = _place()
    hk = s.shape[1] // 2
    mine = s.at[_chip_id(x, y), _half(c, hk)]
    pairs = []
    for j, (cx, cy) in enumerate(chips):
        theirs = s.at[_chip_id(cx, cy), _half(c, hk)]
        pairs.append((_remote(mine, mine, (cx, cy, c), sems(j)), _remote(theirs, theirs, (cx, cy, c), sems(j))))
    return pairs


def _gather_d2d(ro, io, nw, sems):
    s = io[0]
    x, y, c, chips = _place()
    hk = s.shape[1] // 2
    sib = (x, y, 1 - c)
    pairs = []
    for j, (cx, cy) in enumerate(chips):
        here = s.at[_chip_id(cx, cy), _half(c, hk)]
        there = s.at[_chip_id(cx, cy), _half(1 - c, hk)]
        pairs.append((_remote(here, here, sib, sems(j)), _remote(there, there, sib, sems(j))))
    return pairs


def _gather_small(ro, io, nw, sems):
    x, y, c, chips = _place()
    return [(_remote(ro[0], nw[0].at[j], (cx, cy, c), sems(j)),) * 2 for j, (cx, cy) in enumerate(chips)]


def _reduce_halves(ro, io, nw, sems):
    x, y, c, _ = _place()
    g = ro[0]
    hk = g.shape[1] // 2
    sib = (x, y, 1 - c)
    return [(_remote(g.at[:, _half(1 - c, hk)], nw[0], sib, sems(0)), _remote(g.at[:, _half(c, hk)], nw[0], sib, sems(0)))]


def _reduce_chips(ro, io, nw, sems):
    x, y, c, chips = _place()
    return [(_remote(ro[0].at[_chip_id(cx, cy)], nw[0].at[j], (cx, cy, c), sems(j)),) * 2 for j, (cx, cy) in enumerate(chips)]


def _reduce_share(layer):
    def maker(ro, io, nw, sems):
        g = io[0]
        x, y, c, _ = _place()
        hk = g.shape[1] // 2
        sib = (x, y, 1 - c)
        mine, theirs = g.at[layer, _half(c, hk)], g.at[layer, _half(1 - c, hk)]
        return [(_remote(mine, mine, sib, sems(0)), _remote(theirs, theirs, sib, sems(0)))]
    return maker


def _rms_inproj(x, g_row, win, layer, tm, comm=None):
    T, D = x.shape
    ns = win.shape[-1]

    def body(x_ref, g_ref, w_ref, p_ref, h_ref):
        @pl.when(pl.program_id(1) == 0)
        def _():
            xv = x_ref[...]
            h_ref[...] = (xv * _rms_stats(xv) * g_ref[...]).astype(BF16)
        p_ref[...] = jnp.dot(h_ref[...], w_ref[...], preferred_element_type=F32).astype(BF16)

    return _call(
        body, comm, name=f"rms_inproj_{layer}", grid=(T // tm, N_CHIP),
        in_specs=[pl.BlockSpec((tm, D), lambda i, j: (i, 0)),
                  pl.BlockSpec((1, D), lambda i, j: (0, 0)),
                  pl.BlockSpec((None, D, ns), lambda i, j: (j, 0, 0))],
        out_specs=[pl.BlockSpec((tm, ns), lambda i, j: (i, j)),
                   pl.BlockSpec((tm, D), lambda i, j: (i, 0))],
        out_shape=[jax.ShapeDtypeStruct((T, N_CHIP * ns), BF16), jax.ShapeDtypeStruct((T, D), BF16)],
        compiler_params=_params("parallel", "arbitrary"),
    )(x, g_row, win)


def _mixer_recompute(ca, xa, xb, zprev8, xbprev8, cw_ref, wab_ref, sp):
    row = lambda k: cw_ref[pl.ds(k, 1), :]
    z = ca * xa
    z1 = _shift_down(z, 1, zprev8)
    z2 = _shift_down(z, 2, zprev8)
    cz = row(2) * z + row(1) * z1 + row(0) * z2
    x1 = _shift_down(xb, 1, xbprev8)
    x2 = _shift_down(xb, 2, xbprev8)
    x3 = _shift_down(xb, 3, xbprev8)
    u = row(6) * xb + row(5) * x1 + row(4) * x2 + row(3) * x3 + row(7)
    ub = u.astype(BF16)
    nb = wab_ref.shape[0]
    ras, ixs = [], []
    for b in range(nb):
        ri = jnp.dot(ub[:, b * LRU_BLOCK:(b + 1) * LRU_BLOCK], wab_ref[b], preferred_element_type=F32)
        ras.append(ri[:, :LRU_BLOCK])
        ixs.append(ri[:, LRU_BLOCK:])
    r = _sigmoid(jnp.concatenate(ras, axis=1) + row(8))
    gi = _sigmoid(jnp.concatenate(ixs, axis=1) + row(9))
    la = (-LRU_C) * r * sp
    a = jnp.exp(la)
    m = jnp.sqrt(_one_minus_sq(la, a))
    return dict(z=z, z1=z1, z2=z2, cz=cz, x1=x1, x2=x2, x3=x3, u=u, ub=ub, r=r, gi=gi, a=a, m=m)


def _softplus_neg(lam):
    v = -lam
    return jnp.maximum(v, 0.0) + jnp.log1p(jnp.exp(-jnp.abs(v)))


def _mixer_fwd(p, cw, wab, lam_row, layer, tm, comm=None):
    T = p.shape[0]
    D = p.shape[1] // 7
    ngroups = tm // SUBLANES

    def body(ba_ref, ca_ref, xa_ref, xb_ref, gb_ref, cw_ref, wab_ref, lam_ref, ya_ref, yb_ref, h_ref,
             zprev, xbprev, hcarry, a_s, h_s):
        @pl.when(pl.program_id(0) == 0)
        def _():
            zprev[...] = jnp.zeros_like(zprev)
            xbprev[...] = jnp.zeros_like(xbprev)
            hcarry[...] = jnp.zeros_like(hcarry)

        ca = ca_ref[...].astype(F32)
        xa = xa_ref[...].astype(F32)
        xb = xb_ref[...].astype(F32)
        sp = _softplus_neg(lam_ref[...])
        c = _mixer_recompute(ca, xa, xb, zprev[...], xbprev[...], cw_ref, wab_ref, sp)
        zprev[...] = c["z"][tm - SUBLANES:]
        xbprev[...] = xb[tm - SUBLANES:]
        ya_ref[...] = (ba_ref[...].astype(F32) * c["cz"]).astype(BF16)

        a_s[...], h_s[...] = _group_scan(c["a"], c["m"] * c["gi"] * c["u"], reverse=False)

        def step(g, carry):
            off = pl.multiple_of(g * SUBLANES, SUBLANES)
            hg = h_s[pl.ds(off, SUBLANES), :] + a_s[pl.ds(off, SUBLANES), :] * carry
            h_s[pl.ds(off, SUBLANES), :] = hg
            return jnp.broadcast_to(hg[SUBLANES - 1:SUBLANES, :], hg.shape)

        hcarry[...] = lax.fori_loop(0, ngroups, step, hcarry[...], unroll=4)
        h = h_s[...]
        h_ref[...] = h
        gelu, _ = _gelu_parts(gb_ref[...].astype(F32))
        yb_ref[...] = (h * gelu).astype(BF16)

    col = lambda k: pl.BlockSpec((tm, D), lambda i: (i, k))
    full = lambda a: pl.BlockSpec(a.shape, lambda i: (0,) * a.ndim)
    tok = pl.BlockSpec((tm, D), lambda i: (i, 0))
    return _call(
        body, comm, name=f"mixer_fwd_{layer}", grid=(T // tm,),
        in_specs=[col(0), col(1), col(2), col(3), col(4), full(cw), full(wab), full(lam_row)],
        out_specs=[tok, tok, tok],
        out_shape=[jax.ShapeDtypeStruct((T, D), BF16), jax.ShapeDtypeStruct((T, D), BF16), jax.ShapeDtypeStruct((T, D), F32)],
        scratch_shapes=[pltpu.VMEM((SUBLANES, D), F32), pltpu.VMEM((SUBLANES, D), F32), pltpu.VMEM((SUBLANES, D), F32),
                        pltpu.VMEM((tm, D), F32), pltpu.VMEM((tm, D), F32)],
        compiler_params=_params("arbitrary"),
    )(p, p, p, p, p, cw, wab, lam_row)


def _merge_fwd(x, p, ya, yb, woa, wob, wo, gbias, layer, tm, comm=None):
    T, D = x.shape

    def body(x_ref, ga_ref, gb_ref, ya_ref, yb_ref, woa_ref, wob_ref, wo_ref, bias_ref, oa_ref, ob_ref, mg_ref, x1_ref):
        oa = jnp.dot(ya_ref[...], woa_ref[...], preferred_element_type=F32)
        ob = jnp.dot(yb_ref[...], wob_ref[...], preferred_element_type=F32)
        sa = _sigmoid(ga_ref[...].astype(F32) + bias_ref[pl.ds(0, 1), :])
        sb = _sigmoid(gb_ref[...].astype(F32) + bias_ref[pl.ds(1, 1), :])
        mg = (sa * oa + sb * ob).astype(BF16)
        oa_ref[...] = oa.astype(BF16)
        ob_ref[...] = ob.astype(BF16)
        mg_ref[...] = mg
        x1_ref[...] = x_ref[...] + jnp.dot(mg, wo_ref[...], preferred_element_type=F32)

    tok = pl.BlockSpec((tm, D), lambda i: (i, 0))
    wsp = pl.BlockSpec((D, D), lambda i: (0, 0))
    bf = jax.ShapeDtypeStruct((T, D), BF16)
    return _call(
        body, comm, name=f"merge_fwd_{layer}", grid=(T // tm,),
        in_specs=[tok, pl.BlockSpec((tm, D), lambda i: (i, 5)), pl.BlockSpec((tm, D), lambda i: (i, 6)), tok, tok,
                  wsp, wsp, wsp, pl.BlockSpec(gbias.shape, lambda i: (0, 0))],
        out_specs=[tok, tok, tok, tok],
        out_shape=[bf, bf, bf, jax.ShapeDtypeStruct((T, D), F32)],
        compiler_params=_params("parallel"),
    )(x, p, p, ya, yb, woa, wob, wo, gbias)


def _loss_tile(xv, g, tgt):
    d = xv.shape[-1]
    rstd = _rms_stats(xv)
    xhat = xv * rstd
    err = xhat * g - tgt
    dy = err * (1.0 / d)
    dxhat = dy * g
    dx = rstd * (dxhat - xhat * jnp.mean(dxhat * xhat, axis=-1, keepdims=True))
    return dx, _colsum8(err * err), _colsum8(dy * xhat)


def _ffn_fwd(x1, g_row, wg, wu, wd, layer, tm, comm=None):
    T, D = x1.shape
    fs = wg.shape[-2]
    n = T // tm
    nt = (((1,), (1,)), ((), ()))

    def body(x_ref, g_ref, wg_ref, wu_ref, wd_ref, h_ref, gg_ref, uu_ref, x2_ref, acc):
        j = pl.program_id(1)

        @pl.when(j == 0)
        def _():
            xv = x_ref[...]
            h_ref[...] = (xv * _rms_stats(xv) * g_ref[...]).astype(BF16)
            acc[...] = xv

        h = h_ref[...]
        gg = lax.dot_general(h, wg_ref[...], nt, preferred_element_type=F32)
        uu = lax.dot_general(h, wu_ref[...], nt, preferred_element_type=F32)
        gg_ref[...] = gg.astype(BF16)
        uu_ref[...] = uu.astype(BF16)
        act = (gg * _sigmoid(gg) * uu).astype(BF16)
        acc[...] += jnp.dot(act, wd_ref[...], preferred_element_type=F32)

        @pl.when(j == N_CHIP - 1)
        def _():
            x2_ref[...] = acc[...]

    tok = pl.BlockSpec((tm, D), lambda i, j: (i, 0))
    cm = pl.BlockSpec((None, tm, fs), lambda i, j: (j, i, 0))
    wsp = pl.BlockSpec((None, fs, D), lambda i, j: (j, 0, 0))
    return _call(
        body, comm, name=f"ffn_fwd_{layer}", grid=(n, N_CHIP),
        in_specs=[tok, pl.BlockSpec((1, D), lambda i, j: (0, 0)), wsp, wsp, wsp], out_specs=[tok, cm, cm, tok],
        out_shape=[jax.ShapeDtypeStruct((T, D), BF16), jax.ShapeDtypeStruct((N_CHIP, T, fs), BF16),
                   jax.ShapeDtypeStruct((N_CHIP, T, fs), BF16), jax.ShapeDtypeStruct((T, D), F32)],
        scratch_shapes=[pltpu.VMEM((tm, D), F32)], compiler_params=_params("parallel", "arbitrary"),
    )(x1, g_row, wg, wu, wd)


def _final_loss(x, g_row, target, tm):
    T, D = x.shape
    n = T // tm

    def body(x_ref, g_ref, t_ref, dx_ref, red_ref, racc):
        i = pl.program_id(0)

        @pl.when(i == 0)
        def _():
            racc[...] = jnp.zeros_like(racc)

        dx_ref[...], sq, dg = _loss_tile(x_ref[...], g_ref[...], t_ref[...])
        racc[0] += sq
        racc[1] += dg

        @pl.when(i == n - 1)
        def _():
            red_ref[pl.ds(0, 1), :] = jnp.sum(racc[0], axis=0, keepdims=True) * (0.5 / D)
            red_ref[pl.ds(1, 1), :] = jnp.sum(racc[1], axis=0, keepdims=True)

    tok = pl.BlockSpec((tm, D), lambda i: (i, 0))
    return pl.pallas_call(
        body, name="final_loss", grid=(n,),
        in_specs=[tok, pl.BlockSpec((1, D), lambda i: (0, 0)), tok],
        out_specs=[tok, pl.BlockSpec((2, D), lambda i: (0, 0))],
        out_shape=[jax.ShapeDtypeStruct((T, D), F32), jax.ShapeDtypeStruct((2, D), F32)],
        scratch_shapes=[pltpu.VMEM((2, SUBLANES, D), F32)],
        compiler_params=_params("arbitrary"),
    )(x, g_row, target)


def _ffn_bwd_gates(dx2, gg, uu, wd, layer, tm):
    T, D = dx2.shape
    fs = wd.shape[-2]
    nt = (((1,), (1,)), ((), ()))

    def body(dx_ref, gg_ref, uu_ref, wd_ref, dg_ref, du_ref, act_ref, dxb_ref):
        @pl.when(pl.program_id(1) == 0)
        def _():
            dxb_ref[...] = dx_ref[...].astype(BF16)

        dact = lax.dot_general(dxb_ref[...], wd_ref[...], nt, preferred_element_type=F32)
        g = gg_ref[...].astype(F32)
        u = uu_ref[...].astype(F32)
        s = _sigmoid(g)
        silu = g * s
        dg_ref[...] = (dact * u * (s * (1.0 + g * (1.0 - s)))).astype(BF16)
        du_ref[...] = (dact * silu).astype(BF16)
        act_ref[...] = (silu * u).astype(BF16)

    tok = pl.BlockSpec((tm, D), lambda i, j: (i, 0))
    cm = pl.BlockSpec((None, tm, fs), lambda i, j: (j, i, 0))
    cms = jax.ShapeDtypeStruct((N_CHIP, T, fs), BF16)
    return pl.pallas_call(
        body, name=f"ffn_bwd_gates_{layer}", grid=(T // tm, N_CHIP),
        in_specs=[tok, cm, cm, pl.BlockSpec((None, fs, D), lambda i, j: (j, 0, 0))], out_specs=[cm, cm, cm, tok],
        out_shape=[cms, cms, cms, jax.ShapeDtypeStruct((T, D), BF16)],
        compiler_params=_params("parallel", "arbitrary"),
    )(dx2, gg, uu, wd)


def _ffn_bwd(dgg, duu, wg, wu, dx2, x1, g_row, layer, tm, comm=None):
    T, D = dx2.shape
    fs = wg.shape[-2]
    n = T // tm

    def body(dg_ref, du_ref, wg_ref, wu_ref, dx_ref, x_ref, g_ref, dx1_ref, red_ref, acc, racc):
        i = pl.program_id(0)
        j = pl.program_id(1)

        @pl.when((i == 0) & (j == 0))
        def _():
            racc[...] = jnp.zeros_like(racc)

        @pl.when(j == 0)
        def _():
            acc[...] = jnp.zeros_like(acc)

        acc[...] += (jnp.dot(dg_ref[...], wg_ref[...], preferred_element_type=F32)
                     + jnp.dot(du_ref[...], wu_ref[...], preferred_element_type=F32))

        @pl.when(j == N_CHIP - 1)
        def _():
            dx, dgain = _rms_bwd(acc[...], x_ref[...], g_ref[...])
            dx1_ref[...] = dx_ref[...] + dx
            racc[...] += _colsum8(dgain)

        @pl.when((i == n - 1) & (j == N_CHIP - 1))
        def _():
            red_ref[...] = jnp.sum(racc[...], axis=0, keepdims=True)

    tok = pl.BlockSpec((tm, D), lambda i, j: (i, 0))
    cm = pl.BlockSpec((None, tm, fs), lambda i, j: (j, i, 0))
    wsp = pl.BlockSpec((None, fs, D), lambda i, j: (j, 0, 0))
    row = pl.BlockSpec((1, D), lambda i, j: (0, 0))
    return _call(
        body, comm, name=f"ffn_bwd_{layer}", grid=(n, N_CHIP),
        in_specs=[cm, cm, wsp, wsp, tok, tok, row], out_specs=[tok, row],
        out_shape=[jax.ShapeDtypeStruct((T, D), F32), jax.ShapeDtypeStruct((1, D), F32)],
        scratch_shapes=[pltpu.VMEM((tm, D), F32), pltpu.VMEM((SUBLANES, D), F32)],
        compiler_params=_params("arbitrary", "arbitrary"),
    )(dgg, duu, wg, wu, dx2, x1, g_row)


def _merge_bwd(dx1, p, oa, ob, woa, wob, wo, gbias, layer, tm, comm=None):
    T, D = dx1.shape
    n = T // tm
    nt = (((1,), (1,)), ((), ()))

    def body(dx_ref, ga_ref, gb_ref, oa_ref, ob_ref, woa_ref, wob_ref, wo_ref, bias_ref,
             dya_ref, dyb_ref, doa_ref, dob_ref, dgl_ref, dxb_ref, red_ref, racc):
        i = pl.program_id(0)

        @pl.when(i == 0)
        def _():
            racc[...] = jnp.zeros_like(racc)

        dxb = dx_ref[...].astype(BF16)
        dxb_ref[...] = dxb
        dm = lax.dot_general(dxb, wo_ref[...], nt, preferred_element_type=F32)
        sa = _sigmoid(ga_ref[...].astype(F32) + bias_ref[pl.ds(0, 1), :])
        sb = _sigmoid(gb_ref[...].astype(F32) + bias_ref[pl.ds(1, 1), :])
        doa = (dm * sa).astype(BF16)
        dob = (dm * sb).astype(BF16)
        dga = dm * oa_ref[...].astype(F32) * (sa * (1.0 - sa))
        dgb = dm * ob_ref[...].astype(F32) * (sb * (1.0 - sb))
        doa_ref[...] = doa
        dob_ref[...] = dob
        dgl_ref[:, 0:D] = dga.astype(BF16)
        dgl_ref[:, D:2 * D] = dgb.astype(BF16)
        racc[0] += _colsum8(dga)
        racc[1] += _colsum8(dgb)
        dya_ref[...] = lax.dot_general(doa, woa_ref[...], nt, preferred_element_type=F32).astype(BF16)
        dyb_ref[...] = lax.dot_general(dob, wob_ref[...], nt, preferred_element_type=F32).astype(BF16)

        @pl.when(i == n - 1)
        def _():
            red_ref[pl.ds(0, 1), :] = jnp.sum(racc[0], axis=0, keepdims=True)
            red_ref[pl.ds(1, 1), :] = jnp.sum(racc[1], axis=0, keepdims=True)

    tok = pl.BlockSpec((tm, D), lambda i: (i, 0))
    wsp = pl.BlockSpec((D, D), lambda i: (0, 0))
    bf = jax.ShapeDtypeStruct((T, D), BF16)
    return _call(
        body, comm, name=f"merge_bwd_{layer}", grid=(n,),
        in_specs=[tok, pl.BlockSpec((tm, D), lambda i: (i, 5)), pl.BlockSpec((tm, D), lambda i: (i, 6)), tok, tok,
                  wsp, wsp, wsp, pl.BlockSpec(gbias.shape, lambda i: (0, 0))],
        out_specs=[tok, tok, tok, tok, pl.BlockSpec((tm, 2 * D), lambda i: (i, 0)), tok, pl.BlockSpec((2, D), lambda i: (0, 0))],
        out_shape=[bf, bf, bf, bf, jax.ShapeDtypeStruct((T, 2 * D), BF16), bf, jax.ShapeDtypeStruct((2, D), F32)],
        scratch_shapes=[pltpu.VMEM((2, SUBLANES, D), F32)],
        compiler_params=_params("arbitrary"),
    )(dx1, p, p, oa, ob, woa, wob, wo, gbias)


N_MIXER_RED = 16


def _mixer_bwd(p, hseq, dya, dyb, dgl, cw, wab, wabt, lam_row, layer, tm, comm=None):
    T = p.shape[0]
    D = p.shape[1] // 7
    n = T // tm
    ngroups = tm // SUBLANES
    nb = wab.shape[0]
    hb = 16
    tn = (((0,), (0,)), ((), ()))

    def body(ba_ref, ca_ref, xa_ref, xb_ref, gb_ref, h_ref, dya_ref, dyb_ref, dgl_ref,
             cap_ref, xap_ref, xbp_ref, hp_ref, ban_ref, dyan_ref,
             cw_ref, wab_ref, wabt_ref, lam_ref,
             dp_ref, red_ref, dwab_ref,
             racc, wacc, anext, gnext, dunext, c_s, g_s):
        i = pl.program_id(0)
        first_tile = i == n - 1
        last_tile = i == 0

        @pl.when(i == 0)
        def _():
            racc[...] = jnp.zeros_like(racc)
            wacc[...] = jnp.zeros_like(wacc)
            anext[...] = jnp.zeros_like(anext)
            gnext[...] = jnp.zeros_like(gnext)
            dunext[...] = jnp.zeros_like(dunext)

        keep_prev = jnp.where(first_tile, 0.0, 1.0)
        keep_next = jnp.where(last_tile, 0.0, 1.0)
        ba = ba_ref[...].astype(F32)
        ca = ca_ref[...].astype(F32)
        xa = xa_ref[...].astype(F32)
        xb = xb_ref[...].astype(F32)
        h = h_ref[...]
        dya = dya_ref[...].astype(F32)
        dyb = dyb_ref[...].astype(F32)
        zprev8 = (cap_ref[...].astype(F32) * xap_ref[...].astype(F32))[hb - SUBLANES:] * keep_prev
        xbprev8 = xbp_ref[...].astype(F32)[hb - SUBLANES:] * keep_prev
        hprev8 = hp_ref[...] * keep_prev
        dcznext8 = (dyan_ref[...].astype(F32) * ban_ref[...].astype(F32))[:SUBLANES] * keep_next

        lam = lam_ref[...]
        sp = _softplus_neg(lam)
        c = _mixer_recompute(ca, xa, xb, zprev8, xbprev8, cw_ref, wab_ref, sp)
        row = lambda k: cw_ref[pl.ds(k, 1), :]
        a, m, r, gi, u = c["a"], c["m"], c["r"], c["gi"], c["u"]

        gelu, dgelu = _gelu_parts(gb_ref[...].astype(F32))
        dgb = dyb * h * dgelu
        c_s[...], g_s[...] = _group_scan(_shift_up(a, 1, anext[...]), dyb * gelu, reverse=True)

        def step(k, carry):
            off = pl.multiple_of((ngroups - 1 - k) * SUBLANES, SUBLANES)
            gg = g_s[pl.ds(off, SUBLANES), :] + c_s[pl.ds(off, SUBLANES), :] * carry
            g_s[pl.ds(off, SUBLANES), :] = gg
            return jnp.broadcast_to(gg[0:1, :], gg.shape)

        gnext[...] = lax.fori_loop(0, ngroups, step, gnext[...], unroll=4)
        anext[...] = a[0:SUBLANES]
        g = g_s[...]

        hprev = _shift_down(h, 1, hprev8)
        da = g * hprev
        gm = g * m
        dgi = gm * u
        du = gm * gi
        dmv = g * gi * u
        dla = a * (da - dmv * a / m)
        dra = dla * ((-LRU_C) * sp) * (r * (1.0 - r))
        dix = dgi * (gi * (1.0 - gi))
        racc[10] += _colsum8(dla * r)
        racc[8] += _colsum8(dra)
        racc[9] += _colsum8(dix)
        drab = dra.astype(BF16)
        dixb = dix.astype(BF16)
        ub = c["ub"]
        dus = []
        for b in range(nb):
            sl = slice(b * LRU_BLOCK, (b + 1) * LRU_BLOCK)
            dri = jnp.concatenate([drab[:, sl], dixb[:, sl]], axis=1)
            dus.append(jnp.dot(dri, wabt_ref[b], preferred_element_type=F32))
            wacc[b] += lax.dot_general(ub[:, sl], dri, tn, preferred_element_type=F32)
        du = du + jnp.concatenate(dus, axis=1)

        dun = dunext[...]
        du1 = _shift_up(du, 1, dun)
        du2 = _shift_up(du, 2, dun)
        du3 = _shift_up(du, 3, dun)
        dxb = row(6) * du + row(5) * du1 + row(4) * du2 + row(3) * du3
        dunext[...] = du[0:SUBLANES]
        racc[6] += _colsum8(du * xb)
        racc[5] += _colsum8(du * c["x1"])
        racc[4] += _colsum8(du * c["x2"])
        racc[3] += _colsum8(du * c["x3"])
        racc[7] += _colsum8(du)

        dba = dya * c["cz"]
        dcz = dya * ba
        dcz1 = _shift_up(dcz, 1, dcznext8)
        dcz2 = _shift_up(dcz, 2, dcznext8)
        dz = row(2) * dcz + row(1) * dcz1 + row(0) * dcz2
        racc[2] += _colsum8(dcz * c["z"])
        racc[1] += _colsum8(dcz * c["z1"])
        racc[0] += _colsum8(dcz * c["z2"])

        dp_ref[:, 0:D] = dba.astype(BF16)
        dp_ref[:, D:2 * D] = (dz * xa).astype(BF16)
        dp_ref[:, 2 * D:3 * D] = (dz * ca).astype(BF16)
        dp_ref[:, 3 * D:4 * D] = dxb.astype(BF16)
        dp_ref[:, 4 * D:5 * D] = dgb.astype(BF16)
        dp_ref[:, 5 * D:7 * D] = dgl_ref[...]

        @pl.when(i == n - 1)
        def _():
            dlam_scale = LRU_C * _sigmoid(-lam)
            for k in range(N_MIXER_RED):
                tot = jnp.sum(racc[k], axis=0, keepdims=True)
                red_ref[pl.ds(k, 1), :] = tot * dlam_scale if k == 10 else tot
            dwab_ref[...] = wacc[...]

    rt = lambda i: n - 1 - i
    col = lambda k: pl.BlockSpec((tm, D), lambda i: (rt(i), k))
    tok = pl.BlockSpec((tm, D), lambda i: (rt(i), 0))
    full = lambda a: pl.BlockSpec(a.shape, lambda i: (0,) * a.ndim)
    prev16 = lambda k: pl.BlockSpec((hb, D), lambda i: (jnp.maximum(rt(i) * (tm // hb) - 1, 0), k))
    next16 = lambda k: pl.BlockSpec((hb, D), lambda i: (jnp.minimum((rt(i) + 1) * (tm // hb), T // hb - 1), k))
    hprev = pl.BlockSpec((SUBLANES, D), lambda i: (jnp.maximum(rt(i) * ngroups - 1, 0), 0))
    return _call(
        body, comm, name=f"mixer_bwd_{layer}", grid=(n,),
        in_specs=[col(0), col(1), col(2), col(3), col(4), tok, tok, tok, pl.BlockSpec((tm, 2 * D), lambda i: (rt(i), 0)),
                  prev16(1), prev16(2), prev16(3), hprev, next16(0), next16(0),
                  full(cw), full(wab), full(wabt), full(lam_row)],
        out_specs=[pl.BlockSpec((tm, 7 * D), lambda i: (rt(i), 0)),
                   pl.BlockSpec((N_MIXER_RED, D), lambda i: (0, 0)),
                   pl.BlockSpec((nb, LRU_BLOCK, 2 * LRU_BLOCK), lambda i: (0, 0, 0))],
        out_shape=[jax.ShapeDtypeStruct((T, 7 * D), BF16), jax.ShapeDtypeStruct((N_MIXER_RED, D), F32),
                   jax.ShapeDtypeStruct((nb, LRU_BLOCK, 2 * LRU_BLOCK), F32)],
        scratch_shapes=[pltpu.VMEM((N_MIXER_RED, SUBLANES, D), F32), pltpu.VMEM((nb, LRU_BLOCK, 2 * LRU_BLOCK), F32),
                        pltpu.VMEM((SUBLANES, D), F32), pltpu.VMEM((SUBLANES, D), F32), pltpu.VMEM((SUBLANES, D), F32),
                        pltpu.VMEM((tm, D), F32), pltpu.VMEM((tm, D), F32)],
        compiler_params=_params("arbitrary"),
    )(p, p, p, p, p, hseq, dya, dyb, dgl, p, p, p, hseq, p, dya, cw, wab, wabt, lam_row)


def _inproj_bwd(dp, dx1, x, g_row, win, layer, tm, comm=None):
    T, D = x.shape
    ns = win.shape[-1]
    n = T // tm
    nt = (((1,), (1,)), ((), ()))

    def body(dp_ref, dx_ref, x_ref, g_ref, w_ref, dx0_ref, red_ref, acc, racc):
        i = pl.program_id(0)
        j = pl.program_id(1)

        @pl.when((i == 0) & (j == 0))
        def _():
            racc[...] = jnp.zeros_like(racc)

        @pl.when(j == 0)
        def _():
            acc[...] = jnp.zeros_like(acc)

        acc[...] += lax.dot_general(dp_ref[...], w_ref[...], nt, preferred_element_type=F32)

        @pl.when(j == N_CHIP - 1)
        def _():
            dx, dgain = _rms_bwd(acc[...], x_ref[...], g_ref[...])
            dx0_ref[...] = dx_ref[...] + dx
            racc[...] += _colsum8(dgain)

        @pl.when((i == n - 1) & (j == N_CHIP - 1))
        def _():
            red_ref[...] = jnp.sum(racc[...], axis=0, keepdims=True)

    tok = pl.BlockSpec((tm, D), lambda i, j: (i, 0))
    return _call(
        body, comm, name=f"inproj_bwd_{layer}", grid=(n, N_CHIP),
        in_specs=[pl.BlockSpec((tm, ns), lambda i, j: (i, j)), tok, tok, pl.BlockSpec((1, D), lambda i, j: (0, 0)),
                  pl.BlockSpec((None, D, ns), lambda i, j: (j, 0, 0))],
        out_specs=[tok, pl.BlockSpec((1, D), lambda i, j: (0, 0))],
        out_shape=[jax.ShapeDtypeStruct((T, D), F32), jax.ShapeDtypeStruct((1, D), F32)],
        scratch_shapes=[pltpu.VMEM((tm, D), F32), pltpu.VMEM((SUBLANES, D), F32)],
        compiler_params=_params("arbitrary", "arbitrary"),
    )(dp, dx1, x, g_row, win)


def _wgrad(a, b, name, tk, a_kind="whole", b_kind="whole", nj=1, comm=None):
    T = a.shape[-2]
    width = lambda v, kind: v.shape[-1] // nj if kind == "cols" else v.shape[-1]
    ka, kb = width(a, a_kind), width(b, b_kind)
    nt = T // tk
    tn = (((0,), (0,)), ((), ()))

    def spec(k, kind):
        if kind == "cm":
            return pl.BlockSpec((None, tk, k), lambda j, t: (j, t, 0))
        if kind == "cols":
            return pl.BlockSpec((tk, k), lambda j, t: (t, j))
        return pl.BlockSpec((tk, k), lambda j, t: (t, 0))

    def body(a_ref, b_ref, o_ref, ob_ref):
        t = pl.program_id(1)

        @pl.when(t == 0)
        def _():
            o_ref[...] = jnp.zeros_like(o_ref)

        o_ref[...] += lax.dot_general(a_ref[...], b_ref[...], tn, preferred_element_type=F32)

        @pl.when(t == nt - 1)
        def _():
            ob_ref[...] = o_ref[...].astype(BF16)

    o_spec = pl.BlockSpec((None, ka, kb), lambda j, t: (j, 0, 0))
    return _call(
        body, comm, name=name, grid=(nj, nt),
        in_specs=[spec(ka, a_kind), spec(kb, b_kind)], out_specs=[o_spec, o_spec],
        out_shape=[jax.ShapeDtypeStruct((nj, ka, kb), F32), jax.ShapeDtypeStruct((nj, ka, kb), BF16)],
        compiler_params=_params("parallel", "arbitrary"),
    )(a, b)


def _wgrad_pair(a, b1, b2, name, tk, comm=None):
    T, ka = a.shape
    nj, _, kb = b1.shape
    nt = T // tk
    tn = (((0,), (0,)), ((), ()))

    def body(a_ref, b1_ref, b2_ref, o1_ref, o1b_ref, o2_ref, o2b_ref):
        t = pl.program_id(1)

        @pl.when(t == 0)
        def _():
            o1_ref[...] = jnp.zeros_like(o1_ref)
            o2_ref[...] = jnp.zeros_like(o2_ref)

        av = a_ref[...]
        o1_ref[...] += lax.dot_general(b1_ref[...], av, tn, preferred_element_type=F32)
        o2_ref[...] += lax.dot_general(b2_ref[...], av, tn, preferred_element_type=F32)

        @pl.when(t == nt - 1)
        def _():
            o1b_ref[...] = o1_ref[...].astype(BF16)
            o2b_ref[...] = o2_ref[...].astype(BF16)

    b_spec = pl.BlockSpec((None, tk, kb), lambda j, t: (j, t, 0))
    o_spec = pl.BlockSpec((None, kb, ka), lambda j, t: (j, 0, 0))
    f32 = jax.ShapeDtypeStruct((nj, kb, ka), F32)
    b16 = jax.ShapeDtypeStruct((nj, kb, ka), BF16)
    return _call(
        body, comm, name=name, grid=(nj, nt),
        in_specs=[pl.BlockSpec((tk, ka), lambda j, t: (t, 0)), b_spec, b_spec], out_specs=[o_spec] * 4,
        out_shape=[f32, b16, f32, b16], compiler_params=_params("parallel", "arbitrary"),
    )(a, b1, b2)


def _block_diag(w):
    hb = LRU_BLOCK // LRU_HEAD_DIM
    nb = w.shape[0] // hb
    w4 = w.reshape(nb, hb, LRU_HEAD_DIM, LRU_HEAD_DIM)
    eye = jnp.eye(hb, dtype=w.dtype)
    return jnp.einsum("bide,ij->bidje", w4, eye).reshape(nb, LRU_BLOCK, LRU_BLOCK)


def _diag_heads(m):
    hb = LRU_BLOCK // LRU_HEAD_DIM
    nb = m.shape[0]
    m5 = m.reshape(nb, hb, LRU_HEAD_DIM, hb, LRU_HEAD_DIM)
    eye = jnp.eye(hb, dtype=m.dtype)
    return jnp.einsum("bidje,ij->bide", m5, eye).reshape(nb * hb, LRU_HEAD_DIM, LRU_HEAD_DIM)


def _tiles(T):
    cap = lambda n: min(n, T)
    return dict(inproj=cap(1024), mixer=cap(256), merge=cap(512), ffn=cap(1024), ffn_bwd=cap(1024), loss=cap(512), inproj_bwd=cap(1024),
                wgrad_in=cap(2048), wgrad=cap(2048))


class _NoSchedule:
    def carry(self, name):
        return None

    def after(self, name):
        pass

    def grad(self, key, layer, f32, b16):
        pass


def _local_step(x, target, W, small, tiles, sched):
    L = small["ln1_g"].shape[0]
    D = x.shape[1]
    square = lambda a: a.reshape(D, D)
    saved = []
    h = x
    for l in range(L):
        cw = jnp.concatenate([small["conv_a_w"][l], small["conv_b_w"][l], small["conv_b_b"][l][None],
                              small["lru_ba"][l][None], small["lru_bx"][l][None]], axis=0)
        wab = jnp.concatenate([_block_diag(small["lru_wa"][l]), _block_diag(small["lru_wx"][l])], axis=2).astype(BF16)
        wabt = jnp.swapaxes(wab, 1, 2)
        lam_row = small["lru_lambda"][l][None]
        ln1_row = small["ln1_g"][l][None]
        ln2_row = small["ln2_g"][l][None]
        p, h1 = _rms_inproj(h, ln1_row, W["win", l], l, tiles["inproj"], sched.carry(f"rms_inproj_{l}"))
        ya, yb, hseq = _mixer_fwd(p, cw, wab, lam_row, l, tiles["mixer"], sched.carry(f"mixer_fwd_{l}"))
        oa, ob, mg, x1 = _merge_fwd(h, p, ya, yb, square(W["woa", l]), square(W["wob", l]), square(W["wo", l]),
                                    small["gate_bias"][l], l, tiles["merge"], sched.carry(f"merge_fwd_{l}"))
        h2, gg, uu, x2 = _ffn_fwd(x1, ln2_row, W["wg", l], W["wu", l], W["wd", l], l, tiles["ffn"], sched.carry(f"ffn_fwd_{l}"))
        saved.append(dict(x0=h, p=p, h1=h1, ya=ya, yb=yb, hseq=hseq, oa=oa, ob=ob, mg=mg, x1=x1, h2=h2, gg=gg, uu=uu,
                          cw=cw, wab=wab, wabt=wabt, lam_row=lam_row, ln1_row=ln1_row, ln2_row=ln2_row))
        h = x2

    dx, red = _final_loss(h, small["final_g"][None], target, tiles["loss"])
    loss_row, d_final_g = red[0], red[1]

    gsmall = {k: [None] * L for k in ("ln1_g", "ln2_g", "conv_a_w", "conv_b_w", "conv_b_b", "lru_wa", "lru_ba", "lru_wx",
                                      "lru_bx", "lru_lambda", "gate_bias")}
    tk = tiles["wgrad"]
    for l in reversed(range(L)):
        s = saved[l]
        dgg, duu, act, dx2b = _ffn_bwd_gates(dx, s["gg"], s["uu"], W["wd", l], l, tiles["ffn_bwd"])
        dx1, dln2 = _ffn_bwd(dgg, duu, W["wg", l], W["wu", l], dx, s["x1"], s["ln2_row"], l, tiles["ffn_bwd"],
                             sched.carry(f"ffn_bwd_{l}"))
        sched.after(f"ffn_bwd_{l}")
        gate_up = _wgrad_pair(s["h2"], dgg, duu, f"wgrad_ffn_gate_up_{l}", tk, sched.carry(f"wgrad_ffn_gate_up_{l}"))
        sched.grad("wg", l, *gate_up[0:2])
        sched.grad("wu", l, *gate_up[2:4])
        sched.after(f"wgrad_ffn_gate_up_{l}")
        sched.grad("wd", l, *_wgrad(act, dx2b, f"wgrad_ffn_down_{l}", tk, "cm", "whole", N_CHIP, sched.carry(f"wgrad_ffn_down_{l}")))
        dya, dyb, doa, dob, dgl, dx1b, dgbias = _merge_bwd(dx1, s["p"], s["oa"], s["ob"], square(W["woa", l]), square(W["wob", l]),
                                                         square(W["wo", l]), small["gate_bias"][l], l, tiles["merge"],
                                                         sched.carry(f"merge_bwd_{l}"))
        sched.after(f"merge_bwd_{l}")
        sched.grad("wo", l, *_wgrad(s["mg"], dx1b, f"wgrad_w_o_{l}", tk))
        sched.grad("woa", l, *_wgrad(s["ya"], doa, f"wgrad_w_out_a_{l}", tk))
        sched.grad("wob", l, *_wgrad(s["yb"], dob, f"wgrad_w_out_b_{l}", tk))
        dp, mred, dwab = _mixer_bwd(s["p"], s["hseq"], dya, dyb, dgl, s["cw"], s["wab"], s["wabt"], s["lam_row"], l,
                                    tiles["mixer"], sched.carry(f"mixer_bwd_{l}"))
        sched.after(f"mixer_bwd_{l}")
        sched.grad("win", l, *_wgrad(s["h1"], dp, f"wgrad_w_in_{l}", tiles["wgrad_in"], "whole", "cols", N_CHIP,
                                     sched.carry(f"wgrad_w_in_{l}")))
        sched.after(f"wgrad_w_in_{l}")
        dx, dln1 = _inproj_bwd(dp, dx1, s["x0"], s["ln1_row"], W["win", l], l, tiles["inproj_bwd"], sched.carry(f"inproj_bwd_{l}"))
        sched.after(f"inproj_bwd_{l}")
        gsmall["ln1_g"][l] = dln1[0]
        gsmall["ln2_g"][l] = dln2[0]
        gsmall["conv_a_w"][l] = mred[0:CONV_A_K]
        gsmall["conv_b_w"][l] = mred[CONV_A_K:CONV_A_K + CONV_B_K]
        gsmall["conv_b_b"][l] = mred[7]
        gsmall["lru_ba"][l] = mred[8]
        gsmall["lru_bx"][l] = mred[9]
        gsmall["lru_lambda"][l] = mred[10]
        gsmall["lru_wa"][l] = _diag_heads(dwab[:, :, :LRU_BLOCK])
        gsmall["lru_wx"][l] = _diag_heads(dwab[:, :, LRU_BLOCK:])
        gsmall["gate_bias"][l] = dgbias
    gsmall = {k: jnp.stack(v) for k, v in gsmall.items()}
    gsmall["final_g"] = d_final_g
    return loss_row, dx, gsmall


def _small_allreduce(buf):
    R, C = buf.shape
    n_dev = 8
    rp = R // n_dev
    rel = [(k >> 2 & 1, k >> 1 & 1, k & 1) for k in range(1, n_dev)]

    def body(in_ref, out_ref, recv, s1, r1, s2, r2):
        x, y, c, _ = _place()
        flip = lambda v, bit: 1 - v if bit else v
        peers = [(flip(x, kx), flip(y, ky), flip(c, kc)) for kx, ky, kc in rel]
        dev = lambda p: 4 * p[0] + 2 * p[1] + p[2]
        part = lambda ref, d: ref.at[pl.ds(pl.multiple_of(d * rp, SUBLANES), rp), :]
        me = dev((x, y, c))

        def scatter(k, src_dev, to):
            return pltpu.make_async_remote_copy(src_ref=part(in_ref, dev(to)), dst_ref=recv.at[src_dev], send_sem=s1.at[k],
                                                recv_sem=r1.at[k], device_id=to, device_id_type=MESH)

        def gather(k, src_dev, to):
            return pltpu.make_async_remote_copy(src_ref=part(out_ref, src_dev), dst_ref=part(out_ref, src_dev), send_sem=s2.at[k],
                                                recv_sem=r2.at[k], device_id=to, device_id_type=MESH)

        first = [scatter(k, me, p) for k, p in enumerate(peers)]
        for cp in first:
            cp.start()
        recv[me] = part(in_ref, me)[...]
        for k, p in enumerate(peers):
            scatter(k, dev(p), (x, y, c)).wait_recv()
        total = recv[0]
        for d in range(1, n_dev):
            total = total + recv[d]
        part(out_ref, me)[...] = total
        second = [gather(k, me, p) for k, p in enumerate(peers)]
        for cp in second:
            cp.start()
        for k, p in enumerate(peers):
            gather(k, dev(p), (x, y, c)).wait_recv()
        for cp in first + second:
            cp.wait_send()

    dma = pltpu.SemaphoreType.DMA
    vm = pl.BlockSpec(memory_space=pltpu.VMEM)
    return pl.pallas_call(
        body, name="small_allreduce", out_shape=jax.ShapeDtypeStruct((R, C), buf.dtype),
        in_specs=[vm], out_specs=vm,
        scratch_shapes=[pltpu.VMEM((n_dev, rp, C), buf.dtype), dma((n_dev - 1,)), dma((n_dev - 1,)), dma((n_dev - 1,)), dma((n_dev - 1,))],
    )(buf)


ELEMENTWISE_BLOCK_BYTES = 2 * 1024 * 1024


def _row_block(k, n):
    best = None
    for b in range(16, k + 1, 16):
        if k % b == 0 and b * n * 4 <= ELEMENTWISE_BLOCK_BYTES:
            best = b
    return best or k


def _add_halves(g, recv, place_arr, name):
    nj, hk, N = recv.shape
    bk = _row_block(hk, N)
    nb = hk // bk

    def body(k_ref, g_ref, r_ref, o_ref, ob_ref):
        s = g_ref[...] + r_ref[...].astype(F32)
        ob_ref[...] = s.astype(BF16)

        @pl.when(pl.program_id(1) == k_ref[0])
        def _():
            o_ref[...] = s

    blk = pl.BlockSpec((None, bk, N), lambda i, j, k_ref: (j, i, 0))
    grid_spec = pltpu.PrefetchScalarGridSpec(
        num_scalar_prefetch=1, grid=(nb, nj),
        in_specs=[pl.BlockSpec((None, bk, N), lambda i, j, k_ref: (j, k_ref[1] * nb + i, 0)), blk],
        out_specs=[pl.BlockSpec((bk, N), lambda i, j, k_ref: (i, 0)), blk])
    return pl.pallas_call(
        body, name=name, grid_spec=grid_spec,
        out_shape=[jax.ShapeDtypeStruct((hk, N), F32), jax.ShapeDtypeStruct((nj, hk, N), BF16)],
        compiler_params=_params("parallel", "arbitrary"),
    )(place_arr, g, recv)


def _add_chips(pc, recv, place_arr, layer, n_layers, prev, name):
    hk, N = pc.shape
    bk = _row_block(hk, N)
    nb = hk // bk

    def body(k_ref, p_ref, r0_ref, r1_ref, r2_ref, *rest):
        o_ref = rest[-1]
        o_ref[...] = ((p_ref[...] + r0_ref[...].astype(F32)) + r1_ref[...].astype(F32)) + r2_ref[...].astype(F32)

    rspec = lambda j: pl.BlockSpec((None, bk, N), lambda i, k_ref: (j, i, 0))
    in_specs = [pl.BlockSpec((bk, N), lambda i, k_ref: (i, 0)), rspec(0), rspec(1), rspec(2)]
    operands = [pc, recv, recv, recv]
    aliases = {}
    if prev is not None:
        in_specs.append(ANY)
        operands.append(prev)
        aliases = {5: 0}
    grid_spec = pltpu.PrefetchScalarGridSpec(
        num_scalar_prefetch=1, grid=(nb,), in_specs=in_specs,
        out_specs=pl.BlockSpec((None, bk, N), lambda i, k_ref: (layer, k_ref[1] * nb + i, 0)))
    return pl.pallas_call(
        body, name=name, grid_spec=grid_spec, out_shape=jax.ShapeDtypeStruct((n_layers, 2 * hk, N), F32),
        input_output_aliases=aliases, compiler_params=_params("parallel"),
    )(place_arr, *operands)


def _adamw_math(w, g, m, v):
    m = ADAM_B1 * m + (1.0 - ADAM_B1) * g
    v = ADAM_B2 * v + (1.0 - ADAM_B2) * (g * g)
    m_hat = m / (1.0 - ADAM_B1 ** ADAM_STEP)
    v_hat = v / (1.0 - ADAM_B2 ** ADAM_STEP)
    delta = -ADAM_LR * (m_hat / (jnp.sqrt(v_hat) + ADAM_EPS) + ADAM_WD * w)
    return delta, m, v


def _adamw(w, g, m, v, name):
    L, K, N = w.shape
    bk = _row_block(K, N)

    def body(w_ref, g_ref, m_ref, v_ref, d_ref, nm_ref, nv_ref, go_ref):
        g = g_ref[...]
        d_ref[...], nm_ref[...], nv_ref[...] = _adamw_math(w_ref[...], g, m_ref[...], v_ref[...])
        go_ref[...] = g

    blk = pl.BlockSpec((None, bk, N), lambda l, i: (l, i, 0))
    sds = jax.ShapeDtypeStruct((L, K, N), F32)
    return pl.pallas_call(
        body, name=name, grid=(L, K // bk), in_specs=[blk] * 4, out_specs=[blk] * 4, out_shape=[sds] * 4,
        compiler_params=_params("parallel", "parallel"),
    )(w, g, m, v)


def _adamw_small(ws, gs, ms, vs):
    n = len(ws)

    def body(*refs):
        w, g, m, v, d, nm, nv = (refs[k * n:(k + 1) * n] for k in range(7))
        for k in range(n):
            d[k][...], nm[k][...], nv[k][...] = _adamw_math(w[k][...], g[k][...], m[k][...], v[k][...])

    sds = [jax.ShapeDtypeStruct(a.shape, F32) for a in ws]
    out = pl.pallas_call(body, name="adamw_small", out_shape=sds * 3)(*ws, *gs, *ms, *vs)
    return out[:n], out[n:2 * n], out[2 * n:]


def _cast_bf16(w, layer, place_arr, name):
    _, K, N = w.shape
    bk = _row_block(K, N)

    def body(k_ref, w_ref, o_ref):
        o_ref[...] = w_ref[...].astype(BF16)

    grid_spec = pltpu.PrefetchScalarGridSpec(
        num_scalar_prefetch=1, grid=(K // bk,),
        in_specs=[pl.BlockSpec((None, bk, N), lambda i, k_ref: (layer, i, 0))],
        out_specs=pl.BlockSpec((None, bk, N), lambda i, k_ref: (k_ref[0], i, 0)))
    return pl.pallas_call(
        body, name=name, grid_spec=grid_spec, out_shape=jax.ShapeDtypeStruct((N_CHIP, K, N), BF16),
        compiler_params=_params("parallel"),
    )(place_arr, w)


BIG = ("w_in", "w_out_a", "w_out_b", "w_o", "w_ffn_gate", "w_ffn_up", "w_ffn_down")
BIG_KEY = dict(w_in="win", w_out_a="woa", w_out_b="wob", w_o="wo", w_ffn_gate="wg", w_ffn_up="wu", w_ffn_down="wd")
SHARDED_SMALL = ("conv_a_w", "conv_b_w", "gate_bias")
REPLICATED = ("ln1_g", "conv_b_b", "lru_wa", "lru_ba", "lru_wx", "lru_bx", "lru_lambda", "ln2_g", "final_g")
WEIGHTS = ("ln1_g", "w_in", "conv_a_w", "conv_b_w", "conv_b_b", "lru_wa", "lru_ba", "lru_wx", "lru_bx", "lru_lambda",
           "w_out_a", "w_out_b", "gate_bias", "w_o", "ln2_g", "w_ffn_gate", "w_ffn_up", "w_ffn_down", "final_g")
LANES = 1024


def _pack_rows(arrays, row_multiple):
    flat = jnp.concatenate([a.reshape(-1) for a in arrays])
    rows = -(-flat.shape[0] // LANES)
    rows = -(-rows // row_multiple) * row_multiple
    flat = jnp.pad(flat, (0, rows * LANES - flat.shape[0]))
    return flat.reshape(rows, LANES)


def _unpack_rows(buf, shapes):
    flat = buf.reshape(-1)
    out, off = [], 0
    for s in shapes:
        n = 1
        for d in s:
            n *= d
        out.append(flat[off:off + n].reshape(s))
        off += n
    return out


OUT_KEYS = ("wo", "woa", "wob")
FFN_KEYS = ("wg", "wu", "wd")


def _items(keys, layer):
    return [(k, layer) for k in keys]


CARRY = {
    "rms_inproj_0": [("gather_ici", _items(OUT_KEYS + FFN_KEYS, 0))],
    "mixer_fwd_0": [("gather_d2d", _items(OUT_KEYS + FFN_KEYS, 0)), ("gather_ici", _items(("win",) + OUT_KEYS, 1))],
    "merge_fwd_0": [("gather_d2d", _items(("win",) + OUT_KEYS, 1))],
    "ffn_fwd_0": [("gather_ici", _items(FFN_KEYS, 1))],
    "rms_inproj_1": [("gather_d2d", _items(FFN_KEYS, 1))],
    "merge_bwd_1": [("halves", _items(FFN_KEYS, 1))],
    "mixer_bwd_1": [("chips", _items(FFN_KEYS, 1)), ("halves", _items(OUT_KEYS, 1))],
    "inproj_bwd_1": [("chips", _items(OUT_KEYS, 1)), ("share", _items(FFN_KEYS, 1))],
    "ffn_bwd_0": [("halves", [("win", 1)]), ("share", _items(OUT_KEYS, 1))],
    "wgrad_ffn_gate_up_0": [("chips", [("win", 1)])],
    "wgrad_ffn_down_0": [("share", [("win", 1)])],
    "merge_bwd_0": [("halves", _items(FFN_KEYS, 0))],
    "mixer_bwd_0": [("chips", _items(FFN_KEYS, 0)), ("halves", _items(OUT_KEYS, 0))],
    "wgrad_w_in_0": [("chips", _items(OUT_KEYS, 0)), ("share", _items(FFN_KEYS, 0))],
    "inproj_bwd_0": [("chips", [("win", 0)])],
}
AFTER = {
    "merge_bwd_1": [("add_halves", _items(FFN_KEYS, 1))],
    "mixer_bwd_1": [("add_chips", _items(FFN_KEYS, 1)), ("add_halves", _items(OUT_KEYS, 1))],
    "inproj_bwd_1": [("add_chips", _items(OUT_KEYS, 1))],
    "ffn_bwd_0": [("add_halves", [("win", 1)])],
    "wgrad_ffn_gate_up_0": [("add_chips", [("win", 1)])],
    "merge_bwd_0": [("add_halves", _items(FFN_KEYS, 0))],
    "mixer_bwd_0": [("add_chips", _items(FFN_KEYS, 0)), ("add_halves", _items(OUT_KEYS, 0))],
    "wgrad_w_in_0": [("add_chips", _items(OUT_KEYS, 0)),
                     ("run", ("reduce_halves_w_in_0", [("halves", [("win", 0)]), ("share", _items(OUT_KEYS, 0))])),
                     ("add_halves", [("win", 0)])],
    "inproj_bwd_0": [("add_chips", [("win", 0)]), ("run", ("reduce_share_w_in_0", [("share", [("win", 0)])]))],
}


class _Schedule:
    def __init__(self, slots, place_arr, n_layers):
        self.W = slots
        self.place, self.L = place_arr, n_layers
        self.g32, self.g16 = {}, {}
        self.from_sibling, self.chip_sum, self.chip_sum16, self.from_chips = {}, {}, {}, {}
        self.reduced = {}

    def stage(self, comm, kind, items):
        bf = lambda shape: jax.ShapeDtypeStruct(shape, BF16)
        for it in items:
            if kind == "gather_ici":
                comm.add(_gather_ici, 3, io=[(self.W, it)])
            elif kind == "gather_d2d":
                comm.add(_gather_d2d, 3, io=[(self.W, it)])
            elif kind == "halves":
                nj, K, N = self.g16[it].shape
                comm.add(_reduce_halves, 1, ro=[self.g16[it]], nw=[(self.from_sibling, it, bf((nj, K // 2, N)))])
            elif kind == "chips":
                _, hk, N = self.chip_sum16[it].shape
                comm.add(_reduce_chips, 3, ro=[self.chip_sum16[it]], nw=[(self.from_chips, it, bf((3, hk, N)))])
            elif kind == "share":
                comm.add(_reduce_share(it[1]), 1, io=[(self.reduced, it[0])])
        return comm

    def carry(self, name):
        comm = _Carried()
        for kind, items in CARRY.get(name, ()):
            self.stage(comm, kind, items)
        return comm

    def run(self, name, rounds):
        _run_comm([self.stage(_Carried(), kind, items) for kind, items in rounds], name)

    def grad(self, key, layer, f32, b16):
        by_chip = lambda g: g.reshape(N_CHIP, -1, g.shape[-1])
        self.g32[key, layer], self.g16[key, layer] = by_chip(f32), by_chip(b16)

    def add(self, kind, items):
        for key, layer in items:
            it = (key, layer)
            if kind == "add_halves":
                self.chip_sum[it], self.chip_sum16[it] = _add_halves(self.g32[it], self.from_sibling[it], self.place,
                                                                    f"add_halves_{key}_{layer}")
            else:
                self.reduced[key] = _add_chips(self.chip_sum[it], self.from_chips[it], self.place, layer, self.L,
                                               self.reduced.get(key), f"add_chips_{key}_{layer}")

    def after(self, name):
        for kind, items in AFTER.get(name, ()):
            if kind == "run":
                self.run(*items)
            else:
                self.add(kind, items)


def _step(w, m, v, x, target):
    xi, yi, ci = lax.axis_index("x"), lax.axis_index("y"), lax.axis_index("c")
    chip = _chip_id(xi, yi)
    place_arr = jnp.stack([chip, ci]).astype(jnp.int32)
    L = w["ln1_g"].shape[0]
    assert L == 2
    D = x.shape[1]
    dc = D // N_CHIP

    stored = lambda n, a: jnp.swapaxes(a, 1, 2) if n in ("w_ffn_gate", "w_ffn_up") else a
    slots = {(BIG_KEY[n], l): _cast_bf16(stored(n, w[n]), l, place_arr, f"cast_{n}_{l}") for n in BIG for l in range(L)}
    sched = _Schedule(slots, place_arr, L)
    small_shard = jnp.concatenate([w[n] for n in SHARDED_SMALL], axis=1)
    got = {}
    first = sched.stage(_Carried(), "gather_ici", [("win", 0)])
    first.add(_gather_small, 3, ro=[small_shard], nw=[(got, "small", jax.ShapeDtypeStruct((3,) + small_shard.shape, F32))])
    _run_comm([first, sched.stage(_Carried(), "gather_d2d", [("win", 0)])], "gather_first")
    small_g = jnp.zeros((N_CHIP,) + small_shard.shape, F32)
    small_g = lax.dynamic_update_index_in_dim(small_g, small_shard, chip, 0)
    for j, (cx, cy) in enumerate([(1 - xi, yi), (xi, 1 - yi), (1 - xi, 1 - yi)]):
        small_g = lax.dynamic_update_index_in_dim(small_g, got["small"][j], _chip_id(cx, cy), 0)
    small_full = jnp.transpose(small_g, (1, 2, 0, 3)).reshape(L, small_shard.shape[1], D)
    small = {n: w[n] for n in REPLICATED}
    off = 0
    for n in SHARDED_SMALL:
        k = w[n].shape[1]
        small[n] = small_full[:, off:off + k]
        off += k

    loss_row, grad_x, gsmall = _local_step(x, target, sched.W, small, _tiles(x.shape[0]), sched)

    grads = {}

    order = [n for n in WEIGHTS if n not in BIG]
    packed = _pack_rows([gsmall[n] for n in order] + [loss_row], 8 * SUBLANES)
    summed = _small_allreduce(packed)
    parts = _unpack_rows(summed, [gsmall[n].shape for n in order] + [loss_row.shape])
    loss = jnp.sum(parts[-1])
    for n, g in zip(order, parts[:-1]):
        grads[n] = lax.dynamic_slice_in_dim(g, chip * dc, dc, axis=2) if n in SHARDED_SMALL else g

    delta, new_m, new_v = {}, {}, {}
    for n in BIG:
        d, nm, nv, g = _adamw(stored(n, w[n]), sched.reduced[BIG_KEY[n]], stored(n, m[n]), stored(n, v[n]), f"adamw_{n}")
        delta[n], new_m[n], new_v[n], grads[n] = stored(n, d), stored(n, nm), stored(n, nv), stored(n, g)
    for d, arrays in zip((delta, new_m, new_v), _adamw_small(*([d[n] for n in order] for d in (w, grads, m, v)))):
        d.update(zip(order, arrays))
    return loss, grad_x, grads, delta, new_m, new_v


def kernel(x, ln1_g, w_in, conv_a_w, conv_b_w, conv_b_b, lru_wa, lru_ba, lru_wx, lru_bx, lru_lambda, w_out_a, w_out_b, gate_bias, w_o, ln2_g, w_ffn_gate, w_ffn_up, w_ffn_down, final_g, loss_target, m_ln1_g, m_w_in, m_conv_a_w, m_conv_b_w, m_conv_b_b, m_lru_wa, m_lru_ba, m_lru_wx, m_lru_bx, m_lru_lambda, m_w_out_a, m_w_out_b, m_gate_bias, m_w_o, m_ln2_g, m_w_ffn_gate, m_w_ffn_up, m_w_ffn_down, m_final_g, v_ln1_g, v_w_in, v_conv_a_w, v_conv_b_w, v_conv_b_b, v_lru_wa, v_lru_ba, v_lru_wx, v_lru_bx, v_lru_lambda, v_w_out_a, v_w_out_b, v_gate_bias, v_w_o, v_ln2_g, v_w_ffn_gate, v_w_ffn_up, v_w_ffn_down, v_final_g):
    w = dict(ln1_g=ln1_g, w_in=w_in, conv_a_w=conv_a_w, conv_b_w=conv_b_w, conv_b_b=conv_b_b, lru_wa=lru_wa, lru_ba=lru_ba,
             lru_wx=lru_wx, lru_bx=lru_bx, lru_lambda=lru_lambda, w_out_a=w_out_a, w_out_b=w_out_b, gate_bias=gate_bias, w_o=w_o,
             ln2_g=ln2_g, w_ffn_gate=w_ffn_gate, w_ffn_up=w_ffn_up, w_ffn_down=w_ffn_down, final_g=final_g)
    m = dict(ln1_g=m_ln1_g, w_in=m_w_in, conv_a_w=m_conv_a_w, conv_b_w=m_conv_b_w, conv_b_b=m_conv_b_b, lru_wa=m_lru_wa,
             lru_ba=m_lru_ba, lru_wx=m_lru_wx, lru_bx=m_lru_bx, lru_lambda=m_lru_lambda, w_out_a=m_w_out_a, w_out_b=m_w_out_b,
             gate_bias=m_gate_bias, w_o=m_w_o, ln2_g=m_ln2_g, w_ffn_gate=m_w_ffn_gate, w_ffn_up=m_w_ffn_up,
             w_ffn_down=m_w_ffn_down, final_g=m_final_g)
    v = dict(ln1_g=v_ln1_g, w_in=v_w_in, conv_a_w=v_conv_a_w, conv_b_w=v_conv_b_w, conv_b_b=v_conv_b_b, lru_wa=v_lru_wa,
             lru_ba=v_lru_ba, lru_wx=v_lru_wx, lru_bx=v_lru_bx, lru_lambda=v_lru_lambda, w_out_a=v_w_out_a, w_out_b=v_w_out_b,
             gate_bias=v_gate_bias, w_o=v_w_o, ln2_g=v_ln2_g, w_ffn_gate=v_w_ffn_gate, w_ffn_up=v_w_ffn_up,
             w_ffn_down=v_w_ffn_down, final_g=v_final_g)
    loss, grad_x, grads, delta, new_m, new_v = _step(w, m, v, x[0], loss_target[0])
    return (loss, grad_x[None], *[grads[n] for n in WEIGHTS], *[delta[n] for n in WEIGHTS],
            *[new_m[n] for n in WEIGHTS], *[new_v[n] for n in WEIGHTS])
```

```python
import jax
import jax.numpy as jnp
from jax import lax
from jax.experimental import pallas as pl
from jax.experimental.pallas import tpu as pltpu

F32 = jnp.float32
BF16 = jnp.bfloat16
MESH = pl.DeviceIdType.MESH

N_CHIP = 4
RMS_EPS = 1e-6
LRU_C = 8.0
LRU_HEAD_DIM = 64
LRU_BLOCK = 256
CONV_A_K = 3
CONV_B_K = 4
ADAM_LR = 0.001
ADAM_B1 = 0.9
ADAM_B2 = 0.999
ADAM_EPS = 1e-08
ADAM_WD = 0.01
ADAM_STEP = 10
SUBLANES = 8
VMEM_LIMIT = 56 * 1024 * 1024


def _params(*sem):
    return pltpu.CompilerParams(dimension_semantics=sem, vmem_limit_bytes=VMEM_LIMIT)


def _sigmoid(v):
    return 1.0 / (1.0 + jnp.exp(-v))


def _one_minus_sq(la, a):
    return jnp.tanh(-la) * (1.0 + a * a)


def _gelu_parts(v):
    k = 0.7978845608028654
    v2 = v * v
    t = jnp.tanh(k * (v + 0.044715 * v * v2))
    gelu = 0.5 * v * (1.0 + t)
    dgelu = 0.5 * (1.0 + t) + 0.5 * v * (1.0 - t * t) * k * (1.0 + 3 * 0.044715 * v2)
    return gelu, dgelu


def _shift_down(v, k, prev8):
    rolled = pltpu.roll(v, k, 0)
    r8 = lax.broadcasted_iota(jnp.int32, prev8.shape, 0)
    head = jnp.where(r8 < k, pltpu.roll(prev8, k, 0), rolled[0:SUBLANES])
    return jnp.concatenate([head, rolled[SUBLANES:]], axis=0)


def _shift_up(v, k, next8):
    tm = v.shape[0]
    rolled = pltpu.roll(v, tm - k, 0)
    r8 = lax.broadcasted_iota(jnp.int32, next8.shape, 0)
    tail = jnp.where(r8 >= SUBLANES - k, pltpu.roll(next8, SUBLANES - k, 0), rolled[tm - SUBLANES:])
    return jnp.concatenate([rolled[:tm - SUBLANES], tail], axis=0)


def _group_scan(a, b, reverse):
    tm, c = a.shape
    a = a.reshape(tm // SUBLANES, SUBLANES, c)
    b = b.reshape(tm // SUBLANES, SUBLANES, c)
    q = lax.broadcasted_iota(jnp.int32, a.shape, 1)
    for s in (1, 2, 4):
        msk = q < SUBLANES - s if reverse else q >= s
        shift = SUBLANES - s if reverse else s
        b = jnp.where(msk, a * pltpu.roll(b, shift, 1) + b, b)
        a = jnp.where(msk, a * pltpu.roll(a, shift, 1), a)
    return a.reshape(tm, c), b.reshape(tm, c)


def _colsum8(v):
    tm, c = v.shape
    return jnp.sum(v.reshape(tm // SUBLANES, SUBLANES, c), axis=0)


def _rms_stats(xv):
    var = jnp.mean(xv * xv, axis=-1, keepdims=True)
    return lax.rsqrt(var + RMS_EPS)


def _rms_bwd(dh, xv, g):
    rstd = _rms_stats(xv)
    xhat = xv * rstd
    dxhat = dh * g
    dx = rstd * (dxhat - xhat * jnp.mean(dxhat * xhat, axis=-1, keepdims=True))
    return dx, dh * xhat


ANY = pl.BlockSpec(memory_space=pl.ANY)


def _place():
    x, y, c = lax.axis_index("x"), lax.axis_index("y"), lax.axis_index("c")
    other_chips = [(1 - x, y), (x, 1 - y), (1 - x, 1 - y)]
    return x, y, c, other_chips


def _chip_id(x, y):
    return 2 * x + y


def _half(c, hk):
    return pl.ds(pl.multiple_of(c * hk, 16), hk)


def _remote(src, dst, to, sems):
    return pltpu.make_async_remote_copy(src_ref=src, dst_ref=dst, device_id=to, device_id_type=MESH, **sems)


class _Carried:
    def __init__(self):
        self.ro, self.io, self.nw, self.parts, self.n = [], [], [], [], 0

    def add(self, maker, n, ro=(), io=(), nw=()):
        def index(items, item, same):
            for k, other in enumerate(items):
                if same(other, item):
                    return k
            items.append(item)
            return len(items) - 1

        r = [index(self.ro, a, lambda p, q: p is q) for a in ro]
        i = [index(self.io, a, lambda p, q: p[0] is q[0] and p[1] == q[1]) for a in io]
        w = [index(self.nw, a, lambda p, q: False) for a in nw]
        self.parts.append((maker, r, i, w, self.n))
        self.n += n
        return self

    def pairs(self, ro, io, nw, ssem, rsem):
        out = []
        for maker, r, i, w, base in self.parts:
            sems = lambda k, base=base: dict(send_sem=ssem.at[base + k], recv_sem=rsem.at[base + k])
            out += maker([ro[k] for k in r], [io[k] for k in i], [nw[k] for k in w], sems)
        return out

    def start(self, *refs):
        for send, _ in self.pairs(*refs):
            send.start()

    def finish(self, *refs):
        pairs = self.pairs(*refs)
        for _, recv in pairs:
            recv.wait_recv()
        for send, _ in pairs:
            send.wait_send()

    def operands(self):
        return list(self.ro) + [store[key] for store, key in self.io]

    def out_shapes(self):
        return [jax.ShapeDtypeStruct(store[key].shape, store[key].dtype) for store, key in self.io] + [s for _, _, s in self.nw]

    def keep(self, results):
        for (store, key), arr in zip(self.io, results[:len(self.io)]):
            store[key] = arr
        for (store, key, _), arr in zip(self.nw, results[len(self.io):]):
            store[key] = arr


def _call(body, comm, *, name, grid, in_specs, out_specs, out_shape, compiler_params, scratch_shapes=(), aliases=None):
    aliases = dict(aliases or {})
    if comm is None or not comm.parts:
        return pl.pallas_call(body, name=name, grid=grid, in_specs=in_specs, out_specs=out_specs, out_shape=out_shape,
                              scratch_shapes=list(scratch_shapes), input_output_aliases=aliases, compiler_params=compiler_params)
    n_in, n_out, n_scr = len(in_specs), len(out_shape), len(scratch_shapes)
    n_ro, n_io, n_nw = len(comm.ro), len(comm.io), len(comm.nw)

    def carried(*refs):
        base_in = refs[:n_in]
        ro = refs[n_in:n_in + n_ro]
        pos = n_in + n_ro + n_io
        base_out = refs[pos:pos + n_out]
        io = refs[pos + n_out:pos + n_out + n_io]
        nw = refs[pos + n_out + n_io:pos + n_out + n_io + n_nw]
        pos += n_out + n_io + n_nw
        scr = refs[pos:pos + n_scr]
        ssem, rsem = refs[pos + n_scr], refs[pos + n_scr + 1]
        first = pl.program_id(0) == 0
        last = pl.program_id(0) == grid[0] - 1
        for axis in range(1, len(grid)):
            first = first & (pl.program_id(axis) == 0)
            last = last & (pl.program_id(axis) == grid[axis] - 1)

        @pl.when(first)
        def _():
            comm.start(ro, io, nw, ssem, rsem)

        body(*base_in, *base_out, *scr)

        @pl.when(last)
        def _():
            comm.finish(ro, io, nw, ssem, rsem)

    aliases.update({n_in + n_ro + k: n_out + k for k in range(n_io)})
    dma = pltpu.SemaphoreType.DMA
    call = pl.pallas_call(
        carried, name=name, grid=grid,
        in_specs=list(in_specs) + [ANY] * (n_ro + n_io), out_specs=list(out_specs) + [ANY] * (n_io + n_nw),
        out_shape=list(out_shape) + comm.out_shapes(), input_output_aliases=aliases,
        scratch_shapes=list(scratch_shapes) + [dma((comm.n,)), dma((comm.n,))], compiler_params=compiler_params)

    def run(*operands):
        res = call(*operands, *comm.operands())
        comm.keep(res[n_out:])
        return res[:n_out]

    return run


def _run_comm(rounds, name):
    ro, io, nw, uses = [], [], [], []
    for r in rounds:
        def index(items, item, same):
            for k, other in enumerate(items):
                if same(other, item):
                    return k
            items.append(item)
            return len(items) - 1
        uses.append(([index(ro, a, lambda p, q: p is q) for a in r.ro],
                     [index(io, a, lambda p, q: p[0] is q[0] and p[1] == q[1]) for a in r.io],
                     [index(nw, a, lambda p, q: False) for a in r.nw]))
    n_ro, n_io, n_nw = len(ro), len(io), len(nw)

    def body(*refs):
        ro_refs = refs[:n_ro]
        io_refs = refs[n_ro + n_io:n_ro + 2 * n_io]
        nw_refs = refs[n_ro + 2 * n_io:n_ro + 2 * n_io + n_nw]
        sems = refs[n_ro + 2 * n_io + n_nw:]
        for k, (r, (a, b, c)) in enumerate(zip(rounds, uses)):
            args = ([ro_refs[i] for i in a], [io_refs[i] for i in b], [nw_refs[i] for i in c], sems[2 * k], sems[2 * k + 1])
            r.start(*args)
            r.finish(*args)

    operands = ro + [store[key] for store, key in io]
    out_shape = [jax.ShapeDtypeStruct(store[key].shape, store[key].dtype) for store, key in io] + [s for _, _, s in nw]
    dma = pltpu.SemaphoreType.DMA
    res = pl.pallas_call(
        body, name=name, out_shape=out_shape,
        in_specs=[ANY] * (n_ro + n_io), out_specs=[ANY] * (n_io + n_nw),
        input_output_aliases={n_ro + k: k for k in range(n_io)},
        scratch_shapes=[dma((r.n,)) for r in rounds for _ in range(2)],
    )(*operands)
    for (store, key), arr in zip(io, res[:n_io]):
        store[key] = arr
    for (store, key, _), arr in zip(nw, res[n_io:]):
        store[key] = arr


HBM = pl.BlockSpec(memory_space=pltpu.HBM)
SEM = pl.BlockSpec(memory_space=pltpu.SEMAPHORE)
SPLIT_COPY = pltpu.CompilerParams(has_side_effects=pltpu.SideEffectType.DATAFLOW_SIDE_EFFECTING)


def _first_gather_copies(slot, send_sems, recv_sems):
    x, y, c, chips = _place()
    hk = slot.shape[1] // 2
    mine = slot.at[_chip_id(x, y), _half(c, hk)]
    pairs = []
    for j, (cx, cy) in enumerate(chips):
        theirs = slot.at[_chip_id(cx, cy), _half(c, hk)]
        sems = dict(send_sem=send_sems.at[j], recv_sem=recv_sems.at[j])
        pairs.append((_remote(mine, mine, (cx, cy, c), sems), _remote(theirs, theirs, (cx, cy, c), sems)))
    return pairs


def _gather_first_start(slot):
    def body(slot_ref, send_sems, recv_sems, slot_thru, token):
        for send, _ in _first_gather_copies(slot_ref, send_sems, recv_sems):
            send.start()
        token[...] = jnp.zeros_like(token)

    dma = pltpu.SemaphoreType.DMA
    return pl.pallas_call(
        body, name="gather_first_start",
        out_shape=(dma((3,)), dma((3,)), pltpu.HBM(slot.shape, slot.dtype), jax.ShapeDtypeStruct((SUBLANES, 128), jnp.int32)),
        in_specs=(HBM,), out_specs=(SEM, SEM, HBM, pl.BlockSpec(memory_space=pltpu.VMEM)), input_output_aliases={0: 2},
        compiler_params=SPLIT_COPY,
    )(pltpu.with_memory_space_constraint(slot, pltpu.HBM))


def _gather_first_wait(send_sems, recv_sems, slot, after):
    def body(slot_ref, send_sems, recv_sems, after_ref, slot_done):
        for send, recv in _first_gather_copies(slot_ref, send_sems, recv_sems):
            send.wait_send()
            recv.wait_recv()

    return pl.pallas_call(
        body, name="gather_first_wait", out_shape=(pltpu.HBM(slot.shape, slot.dtype),),
        in_specs=(HBM, SEM, SEM, ANY), out_specs=(HBM,), input_output_aliases={0: 0}, compiler_params=SPLIT_COPY,
    )(slot, send_sems, recv_sems, after)[0]


def _gather_ici(ro, io, nw, sems):
    s = io[0]
    x, y, c, chips = _place()
    hk = s.shape[1] // 2
    mine = s.at[_chip_id(x, y), _half(c, hk)]
    pairs = []
    for j, (cx, cy) in enumerate(chips):
        theirs = s.at[_chip_id(cx, cy), _half(c, hk)]
        pairs.append((_remote(mine, mine, (cx, cy, c), sems(j)), _remote(theirs, theirs, (cx, cy, c), sems(j))))
    return pairs


def _gather_d2d(ro, io, nw, sems):
    s = io[0]
    x, y, c, chips = _place()
    hk = s.shape[1] // 2
    sib = (x, y, 1 - c)
    pairs = []
    for j, (cx, cy) in enumerate(chips):
        here = s.at[_chip_id(cx, cy), _half(c, hk)]
        there = s.at[_chip_id(cx, cy), _half(1 - c, hk)]
        pairs.append((_remote(here, here, sib, sems(j)), _remote(there, there, sib, sems(j))))
    return pairs


def _gather_small(ro, io, nw, sems):
    x, y, c, chips = _place()
    return [(_remote(ro[0], nw[0].at[j], (cx, cy, c), sems(j)),) * 2 for j, (cx, cy) in enumerate(chips)]


def _reduce_halves(ro, io, nw, sems):
    x, y, c, _ = _place()
    g = ro[0]
    hk = g.shape[1] // 2
    sib = (x, y, 1 - c)
    return [(_remote(g.at[:, _half(1 - c, hk)], nw[0], sib, sems(0)), _remote(g.at[:, _half(c, hk)], nw[0], sib, sems(0)))]


def _reduce_chips(ro, io, nw, sems):
    x, y, c, chips = _place()
    return [(_remote(ro[0].at[_chip_id(cx, cy)], nw[0].at[j], (cx, cy, c), sems(j)),) * 2 for j, (cx, cy) in enumerate(chips)]


def _reduce_share(layer):
    def maker(ro, io, nw, sems):
        g = io[0]
        x, y, c, _ = _place()
        hk = g.shape[1] // 2
        sib = (x, y, 1 - c)
        mine, theirs = g.at[layer, _half(c, hk)], g.at[layer, _half(1 - c, hk)]
        return [(_remote(mine, mine, sib, sems(0)), _remote(theirs, theirs, sib, sems(0)))]
    return maker


def _rms_inproj(x, g_row, win, layer, tm, comm=None):
    T, D = x.shape
    ns = win.shape[-1]

    def body(x_ref, g_ref, w_ref, p_ref, h_ref):
        @pl.when(pl.program_id(1) == 0)
        def _():
            xv = x_ref[...]
            h_ref[...] = (xv * _rms_stats(xv) * g_ref[...]).astype(BF16)
        p_ref[...] = jnp.dot(h_ref[...], w_ref[...], preferred_element_type=F32).astype(BF16)

    return _call(
        body, comm, name=f"rms_inproj_{layer}", grid=(T // tm, N_CHIP),
        in_specs=[pl.BlockSpec((tm, D), lambda i, j: (i, 0)),
                  pl.BlockSpec((1, D), lambda i, j: (0, 0)),
                  pl.BlockSpec((None, D, ns), lambda i, j: (j, 0, 0))],
        out_specs=[pl.BlockSpec((tm, ns), lambda i, j: (i, j)),
                   pl.BlockSpec((tm, D), lambda i, j: (i, 0))],
        out_shape=[jax.ShapeDtypeStruct((T, N_CHIP * ns), BF16), jax.ShapeDtypeStruct((T, D), BF16)],
        compiler_params=_params("parallel", "arbitrary"),
    )(x, g_row, win)


def _mixer_recompute(ca, xa, xb, zprev8, xbprev8, cw_ref, wab_ref, sp):
    row = lambda k: cw_ref[pl.ds(k, 1), :]
    z = ca * xa
    z1 = _shift_down(z, 1, zprev8)
    z2 = _shift_down(z, 2, zprev8)
    cz = row(2) * z + row(1) * z1 + row(0) * z2
    x1 = _shift_down(xb, 1, xbprev8)
    x2 = _shift_down(xb, 2, xbprev8)
    x3 = _shift_down(xb, 3, xbprev8)
    u = row(6) * xb + row(5) * x1 + row(4) * x2 + row(3) * x3 + row(7)
    ub = u.astype(BF16)
    nb = wab_ref.shape[0]
    ras, ixs = [], []
    for b in range(nb):
        ri = jnp.dot(ub[:, b * LRU_BLOCK:(b + 1) * LRU_BLOCK], wab_ref[b], preferred_element_type=F32)
        ras.append(ri[:, :LRU_BLOCK])
        ixs.append(ri[:, LRU_BLOCK:])
    r = _sigmoid(jnp.concatenate(ras, axis=1) + row(8))
    gi = _sigmoid(jnp.concatenate(ixs, axis=1) + row(9))
    la = (-LRU_C) * r * sp
    a = jnp.exp(la)
    m = jnp.sqrt(_one_minus_sq(la, a))
    return dict(z=z, z1=z1, z2=z2, cz=cz, x1=x1, x2=x2, x3=x3, u=u, ub=ub, r=r, gi=gi, a=a, m=m)


def _softplus_neg(lam):
    v = -lam
    return jnp.maximum(v, 0.0) + jnp.log1p(jnp.exp(-jnp.abs(v)))


def _mixer_fwd(p, cw, wab, lam_row, layer, tm, comm=None):
    T = p.shape[0]
    D = p.shape[1] // 7
    ngroups = tm // SUBLANES

    def body(ba_ref, ca_ref, xa_ref, xb_ref, gb_ref, cw_ref, wab_ref, lam_ref, ya_ref, yb_ref, h_ref,
             zprev, xbprev, hcarry, a_s, h_s):
        @pl.when(pl.program_id(0) == 0)
        def _():
            zprev[...] = jnp.zeros_like(zprev)
            xbprev[...] = jnp.zeros_like(xbprev)
            hcarry[...] = jnp.zeros_like(hcarry)

        ca = ca_ref[...].astype(F32)
        xa = xa_ref[...].astype(F32)
        xb = xb_ref[...].astype(F32)
        sp = _softplus_neg(lam_ref[...])
        c = _mixer_recompute(ca, xa, xb, zprev[...], xbprev[...], cw_ref, wab_ref, sp)
        zprev[...] = c["z"][tm - SUBLANES:]
        xbprev[...] = xb[tm - SUBLANES:]
        ya_ref[...] = (ba_ref[...].astype(F32) * c["cz"]).astype(BF16)

        a_s[...], h_s[...] = _group_scan(c["a"], c["m"] * c["gi"] * c["u"], reverse=False)

        def step(g, carry):
            off = pl.multiple_of(g * SUBLANES, SUBLANES)
            hg = h_s[pl.ds(off, SUBLANES), :] + a_s[pl.ds(off, SUBLANES), :] * carry
            h_s[pl.ds(off, SUBLANES), :] = hg
            return jnp.broadcast_to(hg[SUBLANES - 1:SUBLANES, :], hg.shape)

        hcarry[...] = lax.fori_loop(0, ngroups, step, hcarry[...], unroll=4)
        h = h_s[...]
        h_ref[...] = h
        gelu, _ = _gelu_parts(gb_ref[...].astype(F32))
        yb_ref[...] = (h * gelu).astype(BF16)

    col = lambda k: pl.BlockSpec((tm, D), lambda i: (i, k))
    full = lambda a: pl.BlockSpec(a.shape, lambda i: (0,) * a.ndim)
    tok = pl.BlockSpec((tm, D), lambda i: (i, 0))
    return _call(
        body, comm, name=f"mixer_fwd_{layer}", grid=(T // tm,),
        in_specs=[col(0), col(1), col(2), col(3), col(4), full(cw), full(wab), full(lam_row)],
        out_specs=[tok, tok, tok],
        out_shape=[jax.ShapeDtypeStruct((T, D), BF16), jax.ShapeDtypeStruct((T, D), BF16), jax.ShapeDtypeStruct((T, D), F32)],
        scratch_shapes=[pltpu.VMEM((SUBLANES, D), F32), pltpu.VMEM((SUBLANES, D), F32), pltpu.VMEM((SUBLANES, D), F32),
                        pltpu.VMEM((tm, D), F32), pltpu.VMEM((tm, D), F32)],
        compiler_params=_params("arbitrary"),
    )(p, p, p, p, p, cw, wab, lam_row)


def _merge_fwd(x, p, ya, yb, woa, wob, wo, gbias, layer, tm, comm=None):
    T, D = x.shape

    def body(x_ref, ga_ref, gb_ref, ya_ref, yb_ref, woa_ref, wob_ref, wo_ref, bias_ref, oa_ref, ob_ref, mg_ref, x1_ref):
        oa = jnp.dot(ya_ref[...], woa_ref[...], preferred_element_type=F32)
        ob = jnp.dot(yb_ref[...], wob_ref[...], preferred_element_type=F32)
        sa = _sigmoid(ga_ref[...].astype(F32) + bias_ref[pl.ds(0, 1), :])
        sb = _sigmoid(gb_ref[...].astype(F32) + bias_ref[pl.ds(1, 1), :])
        mg = (sa * oa + sb * ob).astype(BF16)
        oa_ref[...] = oa.astype(BF16)
        ob_ref[...] = ob.astype(BF16)
        mg_ref[...] = mg
        x1_ref[...] = x_ref[...] + jnp.dot(mg, wo_ref[...], preferred_element_type=F32)

    tok = pl.BlockSpec((tm, D), lambda i: (i, 0))
    wsp = pl.BlockSpec((D, D), lambda i: (0, 0))
    bf = jax.ShapeDtypeStruct((T, D), BF16)
    return _call(
        body, comm, name=f"merge_fwd_{layer}", grid=(T // tm,),
        in_specs=[tok, pl.BlockSpec((tm, D), lambda i: (i, 5)), pl.BlockSpec((tm, D), lambda i: (i, 6)), tok, tok,
                  wsp, wsp, wsp, pl.BlockSpec(gbias.shape, lambda i: (0, 0))],
        out_specs=[tok, tok, tok, tok],
        out_shape=[bf, bf, bf, jax.ShapeDtypeStruct((T, D), F32)],
        compiler_params=_params("parallel"),
    )(x, p, p, ya, yb, woa, wob, wo, gbias)


def _loss_tile(xv, g, tgt):
    d = xv.shape[-1]
    rstd = _rms_stats(xv)
    xhat = xv * rstd
    err = xhat * g - tgt
    dy = err * (1.0 / d)
    dxhat = dy * g
    dx = rstd * (dxhat - xhat * jnp.mean(dxhat * xhat, axis=-1, keepdims=True))
    return dx, _colsum8(err * err), _colsum8(dy * xhat)


def _ffn_fwd(x1, g_row, wg, wu, wd, layer, tm, comm=None):
    T, D = x1.shape
    fs = wg.shape[-2]
    n = T // tm
    nt = (((1,), (1,)), ((), ()))

    def body(x_ref, g_ref, wg_ref, wu_ref, wd_ref, h_ref, gg_ref, uu_ref, x2_ref, acc):
        j = pl.program_id(1)

        @pl.when(j == 0)
        def _():
            xv = x_ref[...]
            h_ref[...] = (xv * _rms_stats(xv) * g_ref[...]).astype(BF16)
            acc[...] = xv

        h = h_ref[...]
        gg = lax.dot_general(h, wg_ref[...], nt, preferred_element_type=F32)
        uu = lax.dot_general(h, wu_ref[...], nt, preferred_element_type=F32)
        gg_ref[...] = gg.astype(BF16)
        uu_ref[...] = uu.astype(BF16)
        act = (gg * _sigmoid(gg) * uu).astype(BF16)
        acc[...] += jnp.dot(act, wd_ref[...], preferred_element_type=F32)

        @pl.when(j == N_CHIP - 1)
        def _():
            x2_ref[...] = acc[...]

    tok = pl.BlockSpec((tm, D), lambda i, j: (i, 0))
    cm = pl.BlockSpec((None, tm, fs), lambda i, j: (j, i, 0))
    wsp = pl.BlockSpec((None, fs, D), lambda i, j: (j, 0, 0))
    return _call(
        body, comm, name=f"ffn_fwd_{layer}", grid=(n, N_CHIP),
        in_specs=[tok, pl.BlockSpec((1, D), lambda i, j: (0, 0)), wsp, wsp, wsp], out_specs=[tok, cm, cm, tok],
        out_shape=[jax.ShapeDtypeStruct((T, D), BF16), jax.ShapeDtypeStruct((N_CHIP, T, fs), BF16),
                   jax.ShapeDtypeStruct((N_CHIP, T, fs), BF16), jax.ShapeDtypeStruct((T, D), F32)],
        scratch_shapes=[pltpu.VMEM((tm, D), F32)], compiler_params=_params("parallel", "arbitrary"),
    )(x1, g_row, wg, wu, wd)


def _final_loss(x, g_row, target, tm):
    T, D = x.shape
    n = T // tm

    def body(x_ref, g_ref, t_ref, dx_ref, red_ref, racc):
        i = pl.program_id(0)

        @pl.when(i == 0)
        def _():
            racc[...] = jnp.zeros_like(racc)

        dx_ref[...], sq, dg = _loss_tile(x_ref[...], g_ref[...], t_ref[...])
        racc[0] += sq
        racc[1] += dg

        @pl.when(i == n - 1)
        def _():
            red_ref[pl.ds(0, 1), :] = jnp.sum(racc[0], axis=0, keepdims=True) * (0.5 / D)
            red_ref[pl.ds(1, 1), :] = jnp.sum(racc[1], axis=0, keepdims=True)

    tok = pl.BlockSpec((tm, D), lambda i: (i, 0))
    return pl.pallas_call(
        body, name="final_loss", grid=(n,),
        in_specs=[tok, pl.BlockSpec((1, D), lambda i: (0, 0)), tok],
        out_specs=[tok, pl.BlockSpec((2, D), lambda i: (0, 0))],
        out_shape=[jax.ShapeDtypeStruct((T, D), F32), jax.ShapeDtypeStruct((2, D), F32)],
        scratch_shapes=[pltpu.VMEM((2, SUBLANES, D), F32)],
        compiler_params=_params("arbitrary"),
    )(x, g_row, target)


def _ffn_bwd_gates(dx2, gg, uu, wd, layer, tm):
    T, D = dx2.shape
    fs = wd.shape[-2]
    nt = (((1,), (1,)), ((), ()))

    def body(dx_ref, gg_ref, uu_ref, wd_ref, dg_ref, du_ref, act_ref, dxb_ref):
        @pl.when(pl.program_id(1) == 0)
        def _():
            dxb_ref[...] = dx_ref[...].astype(BF16)

        dact = lax.dot_general(dxb_ref[...], wd_ref[...], nt, preferred_element_type=F32)
        g = gg_ref[...].astype(F32)
        u = uu_ref[...].astype(F32)
        s = _sigmoid(g)
        silu = g * s
        dg_ref[...] = (dact * u * (s * (1.0 + g * (1.0 - s)))).astype(BF16)
        du_ref[...] = (dact * silu).astype(BF16)
        act_ref[...] = (silu * u).astype(BF16)

    tok = pl.BlockSpec((tm, D), lambda i, j: (i, 0))
    cm = pl.BlockSpec((None, tm, fs), lambda i, j: (j, i, 0))
    cms = jax.ShapeDtypeStruct((N_CHIP, T, fs), BF16)
    return pl.pallas_call(
        body, name=f"ffn_bwd_gates_{layer}", grid=(T // tm, N_CHIP),
        in_specs=[tok, cm, cm, pl.BlockSpec((None, fs, D), lambda i, j: (j, 0, 0))], out_specs=[cm, cm, cm, tok],
        out_shape=[cms, cms, cms, jax.ShapeDtypeStruct((T, D), BF16)],
        compiler_params=_params("parallel", "arbitrary"),
    )(dx2, gg, uu, wd)


def _ffn_bwd(dgg, duu, wg, wu, dx2, x1, g_row, layer, tm, comm=None):
    T, D = dx2.shape
    fs = wg.shape[-2]
    n = T // tm

    def body(dg_ref, du_ref, wg_ref, wu_ref, dx_ref, x_ref, g_ref, dx1_ref, red_ref, acc, racc):
        i = pl.program_id(0)
        j = pl.program_id(1)

        @pl.when((i == 0) & (j == 0))
        def _():
            racc[...] = jnp.zeros_like(racc)

        @pl.when(j == 0)
        def _():
            acc[...] = jnp.zeros_like(acc)

        acc[...] += (jnp.dot(dg_ref[...], wg_ref[...], preferred_element_type=F32)
                     + jnp.dot(du_ref[...], wu_ref[...], preferred_element_type=F32))

        @pl.when(j == N_CHIP - 1)
        def _():
            dx, dgain = _rms_bwd(acc[...], x_ref[...], g_ref[...])
            dx1_ref[...] = dx_ref[...] + dx
            racc[...] += _colsum8(dgain)

        @pl.when((i == n - 1) & (j == N_CHIP - 1))
        def _():
            red_ref[...] = jnp.sum(racc[...], axis=0, keepdims=True)

    tok = pl.BlockSpec((tm, D), lambda i, j: (i, 0))
    cm = pl.BlockSpec((None, tm, fs), lambda i, j: (j, i, 0))
    wsp = pl.BlockSpec((None, fs, D), lambda i, j: (j, 0, 0))
    row = pl.BlockSpec((1, D), lambda i, j: (0, 0))
    return _call(
        body, comm, name=f"ffn_bwd_{layer}", grid=(n, N_CHIP),
        in_specs=[cm, cm, wsp, wsp, tok, tok, row], out_specs=[tok, row],
        out_shape=[jax.ShapeDtypeStruct((T, D), F32), jax.ShapeDtypeStruct((1, D), F32)],
        scratch_shapes=[pltpu.VMEM((tm, D), F32), pltpu.VMEM((SUBLANES, D), F32)],
        compiler_params=_params("arbitrary", "arbitrary"),
    )(dgg, duu, wg, wu, dx2, x1, g_row)


def _merge_bwd(dx1, p, oa, ob, woa, wob, wo, gbias, layer, tm, comm=None):
    T, D = dx1.shape
    n = T // tm
    nt = (((1,), (1,)), ((), ()))

    def body(dx_ref, ga_ref, gb_ref, oa_ref, ob_ref, woa_ref, wob_ref, wo_ref, bias_ref,
             dya_ref, dyb_ref, doa_ref, dob_ref, dgl_ref, dxb_ref, red_ref, racc):
        i = pl.program_id(0)

        @pl.when(i == 0)
        def _():
            racc[...] = jnp.zeros_like(racc)

        dxb = dx_ref[...].astype(BF16)
        dxb_ref[...] = dxb
        dm = lax.dot_general(dxb, wo_ref[...], nt, preferred_element_type=F32)
        sa = _sigmoid(ga_ref[...].astype(F32) + bias_ref[pl.ds(0, 1), :])
        sb = _sigmoid(gb_ref[...].astype(F32) + bias_ref[pl.ds(1, 1), :])
        doa = (dm * sa).astype(BF16)
        dob = (dm * sb).astype(BF16)
        dga = dm * oa_ref[...].astype(F32) * (sa * (1.0 - sa))
        dgb = dm * ob_ref[...].astype(F32) * (sb * (1.0 - sb))
        doa_ref[...] = doa
        dob_ref[...] = dob
        dgl_ref[:, 0:D] = dga.astype(BF16)
        dgl_ref[:, D:2 * D] = dgb.astype(BF16)
        racc[0] += _colsum8(dga)
        racc[1] += _colsum8(dgb)
        dya_ref[...] = lax.dot_general(doa, woa_ref[...], nt, preferred_element_type=F32).astype(BF16)
        dyb_ref[...] = lax.dot_general(dob, wob_ref[...], nt, preferred_element_type=F32).astype(BF16)

        @pl.when(i == n - 1)
        def _():
            red_ref[pl.ds(0, 1), :] = jnp.sum(racc[0], axis=0, keepdims=True)
            red_ref[pl.ds(1, 1), :] = jnp.sum(racc[1], axis=0, keepdims=True)

    tok = pl.BlockSpec((tm, D), lambda i: (i, 0))
    wsp = pl.BlockSpec((D, D), lambda i: (0, 0))
    bf = jax.ShapeDtypeStruct((T, D), BF16)
    return _call(
        body, comm, name=f"merge_bwd_{layer}", grid=(n,),
        in_specs=[tok, pl.BlockSpec((tm, D), lambda i: (i, 5)), pl.BlockSpec((tm, D), lambda i: (i, 6)), tok, tok,
                  wsp, wsp, wsp, pl.BlockSpec(gbias.shape, lambda i: (0, 0))],
        out_specs=[tok, tok, tok, tok, pl.BlockSpec((tm, 2 * D), lambda i: (i, 0)), tok, pl.BlockSpec((2, D), lambda i: (0, 0))],
        out_shape=[bf, bf, bf, bf, jax.ShapeDtypeStruct((T, 2 * D), BF16), bf, jax.ShapeDtypeStruct((2, D), F32)],
        scratch_shapes=[pltpu.VMEM((2, SUBLANES, D), F32)],
        compiler_params=_params("arbitrary"),
    )(dx1, p, p, oa, ob, woa, wob, wo, gbias)


N_MIXER_RED = 16


def _mixer_bwd(p, hseq, dya, dyb, dgl, cw, wab, wabt, lam_row, layer, tm, comm=None):
    T = p.shape[0]
    D = p.shape[1] // 7
    n = T // tm
    ngroups = tm // SUBLANES
    nb = wab.shape[0]
    hb = 16
    tn = (((0,), (0,)), ((), ()))

    def body(ba_ref, ca_ref, xa_ref, xb_ref, gb_ref, h_ref, dya_ref, dyb_ref, dgl_ref,
             cap_ref, xap_ref, xbp_ref, hp_ref, ban_ref, dyan_ref,
             cw_ref, wab_ref, wabt_ref, lam_ref,
             dp_ref, red_ref, dwab_ref,
             racc, wacc, anext, gnext, dunext, c_s, g_s):
        i = pl.program_id(0)
        first_tile = i == n - 1
        last_tile = i == 0

        @pl.when(i == 0)
        def _():
            racc[...] = jnp.zeros_like(racc)
            wacc[...] = jnp.zeros_like(wacc)
            anext[...] = jnp.zeros_like(anext)
            gnext[...] = jnp.zeros_like(gnext)
            dunext[...] = jnp.zeros_like(dunext)

        keep_prev = jnp.where(first_tile, 0.0, 1.0)
        keep_next = jnp.where(last_tile, 0.0, 1.0)
        ba = ba_ref[...].astype(F32)
        ca = ca_ref[...].astype(F32)
        xa = xa_ref[...].astype(F32)
        xb = xb_ref[...].astype(F32)
        h = h_ref[...]
        dya = dya_ref[...].astype(F32)
        dyb = dyb_ref[...].astype(F32)
        zprev8 = (cap_ref[...].astype(F32) * xap_ref[...].astype(F32))[hb - SUBLANES:] * keep_prev
        xbprev8 = xbp_ref[...].astype(F32)[hb - SUBLANES:] * keep_prev
        hprev8 = hp_ref[...] * keep_prev
        dcznext8 = (dyan_ref[...].astype(F32) * ban_ref[...].astype(F32))[:SUBLANES] * keep_next

        lam = lam_ref[...]
        sp = _softplus_neg(lam)
        c = _mixer_recompute(ca, xa, xb, zprev8, xbprev8, cw_ref, wab_ref, sp)
        row = lambda k: cw_ref[pl.ds(k, 1), :]
        a, m, r, gi, u = c["a"], c["m"], c["r"], c["gi"], c["u"]

        gelu, dgelu = _gelu_parts(gb_ref[...].astype(F32))
        dgb = dyb * h * dgelu
        c_s[...], g_s[...] = _group_scan(_shift_up(a, 1, anext[...]), dyb * gelu, reverse=True)

        def step(k, carry):
            off = pl.multiple_of((ngroups - 1 - k) * SUBLANES, SUBLANES)
            gg = g_s[pl.ds(off, SUBLANES), :] + c_s[pl.ds(off, SUBLANES), :] * carry
            g_s[pl.ds(off, SUBLANES), :] = gg
            return jnp.broadcast_to(gg[0:1, :], gg.shape)

        gnext[...] = lax.fori_loop(0, ngroups, step, gnext[...], unroll=4)
        anext[...] = a[0:SUBLANES]
        g = g_s[...]

        hprev = _shift_down(h, 1, hprev8)
        da = g * hprev
        gm = g * m
        dgi = gm * u
        du = gm * gi
        dmv = g * gi * u
        dla = a * (da - dmv * a / m)
        dra = dla * ((-LRU_C) * sp) * (r * (1.0 - r))
        dix = dgi * (gi * (1.0 - gi))
        racc[10] += _colsum8(dla * r)
        racc[8] += _colsum8(dra)
        racc[9] += _colsum8(dix)
        drab = dra.astype(BF16)
        dixb = dix.astype(BF16)
        ub = c["ub"]
        dus = []
        for b in range(nb):
            sl = slice(b * LRU_BLOCK, (b + 1) * LRU_BLOCK)
            dri = jnp.concatenate([drab[:, sl], dixb[:, sl]], axis=1)
            dus.append(jnp.dot(dri, wabt_ref[b], preferred_element_type=F32))
            wacc[b] += lax.dot_general(ub[:, sl], dri, tn, preferred_element_type=F32)
        du = du + jnp.concatenate(dus, axis=1)

        dun = dunext[...]
        du1 = _shift_up(du, 1, dun)
        du2 = _shift_up(du, 2, dun)
        du3 = _shift_up(du, 3, dun)
        dxb = row(6) * du + row(5) * du1 + row(4) * du2 + row(3) * du3
        dunext[...] = du[0:SUBLANES]
        racc[6] += _colsum8(du * xb)
        racc[5] += _colsum8(du * c["x1"])
        racc[4] += _colsum8(du * c["x2"])
        racc[3] += _colsum8(du * c["x3"])
        racc[7] += _colsum8(du)

        dba = dya * c["cz"]
        dcz = dya * ba
        dcz1 = _shift_up(dcz, 1, dcznext8)
        dcz2 = _shift_up(dcz, 2, dcznext8)
        dz = row(2) * dcz + row(1) * dcz1 + row(0) * dcz2
        racc[2] += _colsum8(dcz * c["z"])
        racc[1] += _colsum8(dcz * c["z1"])
        racc[0] += _colsum8(dcz * c["z2"])

        dp_ref[:, 0:D] = dba.astype(BF16)
        dp_ref[:, D:2 * D] = (dz * xa).astype(BF16)
        dp_ref[:, 2 * D:3 * D] = (dz * ca).astype(BF16)
        dp_ref[:, 3 * D:4 * D] = dxb.astype(BF16)
        dp_ref[:, 4 * D:5 * D] = dgb.astype(BF16)
        dp_ref[:, 5 * D:7 * D] = dgl_ref[...]

        @pl.when(i == n - 1)
        def _():
            dlam_scale = LRU_C * _sigmoid(-lam)
            for k in range(N_MIXER_RED):
                tot = jnp.sum(racc[k], axis=0, keepdims=True)
                red_ref[pl.ds(k, 1), :] = tot * dlam_scale if k == 10 else tot
            dwab_ref[...] = wacc[...]

    rt = lambda i: n - 1 - i
    col = lambda k: pl.BlockSpec((tm, D), lambda i: (rt(i), k))
    tok = pl.BlockSpec((tm, D), lambda i: (rt(i), 0))
    full = lambda a: pl.BlockSpec(a.shape, lambda i: (0,) * a.ndim)
    prev16 = lambda k: pl.BlockSpec((hb, D), lambda i: (jnp.maximum(rt(i) * (tm // hb) - 1, 0), k))
    next16 = lambda k: pl.BlockSpec((hb, D), lambda i: (jnp.minimum((rt(i) + 1) * (tm // hb), T // hb - 1), k))
    hprev = pl.BlockSpec((SUBLANES, D), lambda i: (jnp.maximum(rt(i) * ngroups - 1, 0), 0))
    return _call(
        body, comm, name=f"mixer_bwd_{layer}", grid=(n,),
        in_specs=[col(0), col(1), col(2), col(3), col(4), tok, tok, tok, pl.BlockSpec((tm, 2 * D), lambda i: (rt(i), 0)),
                  prev16(1), prev16(2), prev16(3), hprev, next16(0), next16(0),
                  full(cw), full(wab), full(wabt), full(lam_row)],
        out_specs=[pl.BlockSpec((tm, 7 * D), lambda i: (rt(i), 0)),
                   pl.BlockSpec((N_MIXER_RED, D), lambda i: (0, 0)),
                   pl.BlockSpec((nb, LRU_BLOCK, 2 * LRU_BLOCK), lambda i: (0, 0, 0))],
        out_shape=[jax.ShapeDtypeStruct((T, 7 * D), BF16), jax.ShapeDtypeStruct((N_MIXER_RED, D), F32),
                   jax.ShapeDtypeStruct((nb, LRU_BLOCK, 2 * LRU_BLOCK), F32)],
        scratch_shapes=[pltpu.VMEM((N_MIXER_RED, SUBLANES, D), F32), pltpu.VMEM((nb, LRU_BLOCK, 2 * LRU_BLOCK), F32),
                        pltpu.VMEM((SUBLANES, D), F32), pltpu.VMEM((SUBLANES, D), F32), pltpu.VMEM((SUBLANES, D), F32),
                        pltpu.VMEM((tm, D), F32), pltpu.VMEM((tm, D), F32)],
        compiler_params=_params("arbitrary"),
    )(p, p, p, p, p, hseq, dya, dyb, dgl, p, p, p, hseq, p, dya, cw, wab, wabt, lam_row)


def _inproj_bwd(dp, dx1, x, g_row, win, layer, tm, comm=None):
    T, D = x.shape
    ns = win.shape[-1]
    n = T // tm
    nt = (((1,), (1,)), ((), ()))

    def body(dp_ref, dx_ref, x_ref, g_ref, w_ref, dx0_ref, red_ref, acc, racc):
        i = pl.program_id(0)
        j = pl.program_id(1)

        @pl.when((i == 0) & (j == 0))
        def _():
            racc[...] = jnp.zeros_like(racc)

        @pl.when(j == 0)
        def _():
            acc[...] = jnp.zeros_like(acc)

        acc[...] += lax.dot_general(dp_ref[...], w_ref[...], nt, preferred_element_type=F32)

        @pl.when(j == N_CHIP - 1)
        def _():
            dx, dgain = _rms_bwd(acc[...], x_ref[...], g_ref[...])
            dx0_ref[...] = dx_ref[...] + dx
            racc[...] += _colsum8(dgain)

        @pl.when((i == n - 1) & (j == N_CHIP - 1))
        def _():
            red_ref[...] = jnp.sum(racc[...], axis=0, keepdims=True)

    tok = pl.BlockSpec((tm, D), lambda i, j: (i, 0))
    return _call(
        body, comm, name=f"inproj_bwd_{layer}", grid=(n, N_CHIP),
        in_specs=[pl.BlockSpec((tm, ns), lambda i, j: (i, j)), tok, tok, pl.BlockSpec((1, D), lambda i, j: (0, 0)),
                  pl.BlockSpec((None, D, ns), lambda i, j: (j, 0, 0))],
        out_specs=[tok, pl.BlockSpec((1, D), lambda i, j: (0, 0))],
        out_shape=[jax.ShapeDtypeStruct((T, D), F32), jax.ShapeDtypeStruct((1, D), F32)],
        scratch_shapes=[pltpu.VMEM((tm, D), F32), pltpu.VMEM((SUBLANES, D), F32)],
        compiler_params=_params("arbitrary", "arbitrary"),
    )(dp, dx1, x, g_row, win)


def _wgrad(a, b, name, tk, a_kind="whole", b_kind="whole", nj=1, comm=None):
    T = a.shape[-2]
    width = lambda v, kind: v.shape[-1] // nj if kind == "cols" else v.shape[-1]
    ka, kb = width(a, a_kind), width(b, b_kind)
    nt = T // tk
    tn = (((0,), (0,)), ((), ()))

    def spec(k, kind):
        if kind == "cm":
            return pl.BlockSpec((None, tk, k), lambda j, t: (j, t, 0))
        if kind == "cols":
            return pl.BlockSpec((tk, k), lambda j, t: (t, j))
        return pl.BlockSpec((tk, k), lambda j, t: (t, 0))

    def body(a_ref, b_ref, o_ref, ob_ref):
        t = pl.program_id(1)

        @pl.when(t == 0)
        def _():
            o_ref[...] = jnp.zeros_like(o_ref)

        o_ref[...] += lax.dot_general(a_ref[...], b_ref[...], tn, preferred_element_type=F32)

        @pl.when(t == nt - 1)
        def _():
            ob_ref[...] = o_ref[...].astype(BF16)

    o_spec = pl.BlockSpec((None, ka, kb), lambda j, t: (j, 0, 0))
    return _call(
        body, comm, name=name, grid=(nj, nt),
        in_specs=[spec(ka, a_kind), spec(kb, b_kind)], out_specs=[o_spec, o_spec],
        out_shape=[jax.ShapeDtypeStruct((nj, ka, kb), F32), jax.ShapeDtypeStruct((nj, ka, kb), BF16)],
        compiler_params=_params("parallel", "arbitrary"),
    )(a, b)


def _wgrad_pair(a, b1, b2, name, tk, comm=None):
    T, ka = a.shape
    nj, _, kb = b1.shape
    nt = T // tk
    tn = (((0,), (0,)), ((), ()))

    def body(a_ref, b1_ref, b2_ref, o1_ref, o1b_ref, o2_ref, o2b_ref):
        t = pl.program_id(1)

        @pl.when(t == 0)
        def _():
            o1_ref[...] = jnp.zeros_like(o1_ref)
            o2_ref[...] = jnp.zeros_like(o2_ref)

        av = a_ref[...]
        o1_ref[...] += lax.dot_general(b1_ref[...], av, tn, preferred_element_type=F32)
        o2_ref[...] += lax.dot_general(b2_ref[...], av, tn, preferred_element_type=F32)

        @pl.when(t == nt - 1)
        def _():
            o1b_ref[...] = o1_ref[...].astype(BF16)
            o2b_ref[...] = o2_ref[...].astype(BF16)

    b_spec = pl.BlockSpec((None, tk, kb), lambda j, t: (j, t, 0))
    o_spec = pl.BlockSpec((None, kb, ka), lambda j, t: (j, 0, 0))
    f32 = jax.ShapeDtypeStruct((nj, kb, ka), F32)
    b16 = jax.ShapeDtypeStruct((nj, kb, ka), BF16)
    return _call(
        body, comm, name=name, grid=(nj, nt),
        in_specs=[pl.BlockSpec((tk, ka), lambda j, t: (t, 0)), b_spec, b_spec], out_specs=[o_spec] * 4,
        out_shape=[f32, b16, f32, b16], compiler_params=_params("parallel", "arbitrary"),
    )(a, b1, b2)


def _block_diag(w):
    hb = LRU_BLOCK // LRU_HEAD_DIM
    nb = w.shape[0] // hb
    w4 = w.reshape(nb, hb, LRU_HEAD_DIM, LRU_HEAD_DIM)
    eye = jnp.eye(hb, dtype=w.dtype)
    return jnp.einsum("bide,ij->bidje", w4, eye).reshape(nb, LRU_BLOCK, LRU_BLOCK)


def _diag_heads(m):
    hb = LRU_BLOCK // LRU_HEAD_DIM
    nb = m.shape[0]
    m5 = m.reshape(nb, hb, LRU_HEAD_DIM, hb, LRU_HEAD_DIM)
    eye = jnp.eye(hb, dtype=m.dtype)
    return jnp.einsum("bidje,ij->bide", m5, eye).reshape(nb * hb, LRU_HEAD_DIM, LRU_HEAD_DIM)


def _tiles(T):
    cap = lambda n: min(n, T)
    return dict(inproj=cap(1024), mixer=cap(256), merge=cap(512), ffn=cap(1024), ffn_bwd=cap(1024), loss=cap(512), inproj_bwd=cap(1024),
                wgrad_in=cap(2048), wgrad=cap(2048))


class _NoSchedule:
    def carry(self, name):
        return None

    def after(self, name):
        pass

    def grad(self, key, layer, f32, b16):
        pass


def _local_step(x, target, W, small, tiles, sched):
    L = small["ln1_g"].shape[0]
    D = x.shape[1]
    square = lambda a: a.reshape(D, D)
    saved = []
    h = x
    for l in range(L):
        cw = jnp.concatenate([small["conv_a_w"][l], small["conv_b_w"][l], small["conv_b_b"][l][None],
                              small["lru_ba"][l][None], small["lru_bx"][l][None]], axis=0)
        wab = jnp.concatenate([_block_diag(small["lru_wa"][l]), _block_diag(small["lru_wx"][l])], axis=2).astype(BF16)
        wabt = jnp.swapaxes(wab, 1, 2)
        lam_row = small["lru_lambda"][l][None]
        ln1_row = small["ln1_g"][l][None]
        ln2_row = small["ln2_g"][l][None]
        p, h1 = _rms_inproj(h, ln1_row, W["win", l], l, tiles["inproj"], sched.carry(f"rms_inproj_{l}"))
        ya, yb, hseq = _mixer_fwd(p, cw, wab, lam_row, l, tiles["mixer"], sched.carry(f"mixer_fwd_{l}"))
        oa, ob, mg, x1 = _merge_fwd(h, p, ya, yb, square(W["woa", l]), square(W["wob", l]), square(W["wo", l]),
                                    small["gate_bias"][l], l, tiles["merge"], sched.carry(f"merge_fwd_{l}"))
        h2, gg, uu, x2 = _ffn_fwd(x1, ln2_row, W["wg", l], W["wu", l], W["wd", l], l, tiles["ffn"], sched.carry(f"ffn_fwd_{l}"))
        saved.append(dict(x0=h, p=p, h1=h1, ya=ya, yb=yb, hseq=hseq, oa=oa, ob=ob, mg=mg, x1=x1, h2=h2, gg=gg, uu=uu,
                          cw=cw, wab=wab, wabt=wabt, lam_row=lam_row, ln1_row=ln1_row, ln2_row=ln2_row))
        h = x2

    dx, red = _final_loss(h, small["final_g"][None], target, tiles["loss"])
    loss_row, d_final_g = red[0], red[1]

    gsmall = {k: [None] * L for k in ("ln1_g", "ln2_g", "conv_a_w", "conv_b_w", "conv_b_b", "lru_wa", "lru_ba", "lru_wx",
                                      "lru_bx", "lru_lambda", "gate_bias")}
    tk = tiles["wgrad"]
    for l in reversed(range(L)):
        s = saved[l]
        dgg, duu, act, dx2b = _ffn_bwd_gates(dx, s["gg"], s["uu"], W["wd", l], l, tiles["ffn_bwd"])
        dx1, dln2 = _ffn_bwd(dgg, duu, W["wg", l], W["wu", l], dx, s["x1"], s["ln2_row"], l, tiles["ffn_bwd"],
                             sched.carry(f"ffn_bwd_{l}"))
        sched.after(f"ffn_bwd_{l}")
        gate_up = _wgrad_pair(s["h2"], dgg, duu, f"wgrad_ffn_gate_up_{l}", tk, sched.carry(f"wgrad_ffn_gate_up_{l}"))
        sched.grad("wg", l, *gate_up[0:2])
        sched.grad("wu", l, *gate_up[2:4])
        sched.after(f"wgrad_ffn_gate_up_{l}")
        sched.grad("wd", l, *_wgrad(act, dx2b, f"wgrad_ffn_down_{l}", tk, "cm", "whole", N_CHIP, sched.carry(f"wgrad_ffn_down_{l}")))
        dya, dyb, doa, dob, dgl, dx1b, dgbias = _merge_bwd(dx1, s["p"], s["oa"], s["ob"], square(W["woa", l]), square(W["wob", l]),
                                                         square(W["wo", l]), small["gate_bias"][l], l, tiles["merge"],
                                                         sched.carry(f"merge_bwd_{l}"))
        sched.after(f"merge_bwd_{l}")
        sched.grad("wo", l, *_wgrad(s["mg"], dx1b, f"wgrad_w_o_{l}", tk))
        sched.grad("woa", l, *_wgrad(s["ya"], doa, f"wgrad_w_out_a_{l}", tk))
        sched.grad("wob", l, *_wgrad(s["yb"], dob, f"wgrad_w_out_b_{l}", tk))
        dp, mred, dwab = _mixer_bwd(s["p"], s["hseq"], dya, dyb, dgl, s["cw"], s["wab"], s["wabt"], s["lam_row"], l,
                                    tiles["mixer"], sched.carry(f"mixer_bwd_{l}"))
        sched.after(f"mixer_bwd_{l}")
        sched.grad("win", l, *_wgrad(s["h1"], dp, f"wgrad_w_in_{l}", tiles["wgrad_in"], "whole", "cols", N_CHIP,
                                     sched.carry(f"wgrad_w_in_{l}")))
        sched.after(f"wgrad_w_in_{l}")
        dx, dln1 = _inproj_bwd(dp, dx1, s["x0"], s["ln1_row"], W["win", l], l, tiles["inproj_bwd"], sched.carry(f"inproj_bwd_{l}"))
        sched.after(f"inproj_bwd_{l}")
        gsmall["ln1_g"][l] = dln1[0]
        gsmall["ln2_g"][l] = dln2[0]
        gsmall["conv_a_w"][l] = mred[0:CONV_A_K]
        gsmall["conv_b_w"][l] = mred[CONV_A_K:CONV_A_K + CONV_B_K]
        gsmall["conv_b_b"][l] = mred[7]
        gsmall["lru_ba"][l] = mred[8]
        gsmall["lru_bx"][l] = mred[9]
        gsmall["lru_lambda"][l] = mred[10]
        gsmall["lru_wa"][l] = _diag_heads(dwab[:, :, :LRU_BLOCK])
        gsmall["lru_wx"][l] = _diag_heads(dwab[:, :, LRU_BLOCK:])
        gsmall["gate_bias"][l] = dgbias
    gsmall = {k: jnp.stack(v) for k, v in gsmall.items()}
    gsmall["final_g"] = d_final_g
    return loss_row, dx, gsmall


def _small_allreduce(buf):
    R, C = buf.shape
    n_dev = 8
    rp = R // n_dev
    rel = [(k >> 2 & 1, k >> 1 & 1, k & 1) for k in range(1, n_dev)]

    def body(in_ref, out_ref, recv, s1, r1, s2, r2):
        x, y, c, _ = _place()
        flip = lambda v, bit: 1 - v if bit else v
        peers = [(flip(x, kx), flip(y, ky), flip(c, kc)) for kx, ky, kc in rel]
        dev = lambda p: 4 * p[0] + 2 * p[1] + p[2]
        part = lambda ref, d: ref.at[pl.ds(pl.multiple_of(d * rp, SUBLANES), rp), :]
        me = dev((x, y, c))

        def scatter(k, src_dev, to):
            return pltpu.make_async_remote_copy(src_ref=part(in_ref, dev(to)), dst_ref=recv.at[src_dev], send_sem=s1.at[k],
                                                recv_sem=r1.at[k], device_id=to, device_id_type=MESH)

        def gather(k, src_dev, to):
            return pltpu.make_async_remote_copy(src_ref=part(out_ref, src_dev), dst_ref=part(out_ref, src_dev), send_sem=s2.at[k],
                                                recv_sem=r2.at[k], device_id=to, device_id_type=MESH)

        first = [scatter(k, me, p) for k, p in enumerate(peers)]
        for cp in first:
            cp.start()
        recv[me] = part(in_ref, me)[...]
        for k, p in enumerate(peers):
            scatter(k, dev(p), (x, y, c)).wait_recv()
        total = recv[0]
        for d in range(1, n_dev):
            total = total + recv[d]
        part(out_ref, me)[...] = total
        second = [gather(k, me, p) for k, p in enumerate(peers)]
        for cp in second:
            cp.start()
        for k, p in enumerate(peers):
            gather(k, dev(p), (x, y, c)).wait_recv()
        for cp in first + second:
            cp.wait_send()

    dma = pltpu.SemaphoreType.DMA
    vm = pl.BlockSpec(memory_space=pltpu.VMEM)
    return pl.pallas_call(
        body, name="small_allreduce", out_shape=jax.ShapeDtypeStruct((R, C), buf.dtype),
        in_specs=[vm], out_specs=vm,
        scratch_shapes=[pltpu.VMEM((n_dev, rp, C), buf.dtype), dma((n_dev - 1,)), dma((n_dev - 1,)), dma((n_dev - 1,)), dma((n_dev - 1,))],
    )(buf)


ELEMENTWISE_BLOCK_BYTES = 2 * 1024 * 1024


def _row_block(k, n):
    best = None
    for b in range(16, k + 1, 16):
        if k % b == 0 and b * n * 4 <= ELEMENTWISE_BLOCK_BYTES:
            best = b
    return best or k


def _add_halves(g, recv, place_arr, name):
    nj, hk, N = recv.shape
    bk = _row_block(hk, N)
    nb = hk // bk

    def body(k_ref, g_ref, r_ref, o_ref, ob_ref):
        s = g_ref[...] + r_ref[...].astype(F32)
        ob_ref[...] = s.astype(BF16)

        @pl.when(pl.program_id(1) == k_ref[0])
        def _():
            o_ref[...] = s

    blk = pl.BlockSpec((None, bk, N), lambda i, j, k_ref: (j, i, 0))
    grid_spec = pltpu.PrefetchScalarGridSpec(
        num_scalar_prefetch=1, grid=(nb, nj),
        in_specs=[pl.BlockSpec((None, bk, N), lambda i, j, k_ref: (j, k_ref[1] * nb + i, 0)), blk],
        out_specs=[pl.BlockSpec((bk, N), lambda i, j, k_ref: (i, 0)), blk])
    return pl.pallas_call(
        body, name=name, grid_spec=grid_spec,
        out_shape=[jax.ShapeDtypeStruct((hk, N), F32), jax.ShapeDtypeStruct((nj, hk, N), BF16)],
        compiler_params=_params("parallel", "arbitrary"),
    )(place_arr, g, recv)


def _add_chips(pc, recv, place_arr, layer, n_layers, prev, name):
    hk, N = pc.shape
    bk = _row_block(hk, N)
    nb = hk // bk

    def body(k_ref, p_ref, r0_ref, r1_ref, r2_ref, *rest):
        o_ref = rest[-1]
        o_ref[...] = ((p_ref[...] + r0_ref[...].astype(F32)) + r1_ref[...].astype(F32)) + r2_ref[...].astype(F32)

    rspec = lambda j: pl.BlockSpec((None, bk, N), lambda i, k_ref: (j, i, 0))
    in_specs = [pl.BlockSpec((bk, N), lambda i, k_ref: (i, 0)), rspec(0), rspec(1), rspec(2)]
    operands = [pc, recv, recv, recv]
    aliases = {}
    if prev is not None:
        in_specs.append(ANY)
        operands.append(prev)
        aliases = {5: 0}
    grid_spec = pltpu.PrefetchScalarGridSpec(
        num_scalar_prefetch=1, grid=(nb,), in_specs=in_specs,
        out_specs=pl.BlockSpec((None, bk, N), lambda i, k_ref: (layer, k_ref[1] * nb + i, 0)))
    return pl.pallas_call(
        body, name=name, grid_spec=grid_spec, out_shape=jax.ShapeDtypeStruct((n_layers, 2 * hk, N), F32),
        input_output_aliases=aliases, compiler_params=_params("parallel"),
    )(place_arr, *operands)


def _adamw_math(w, g, m, v):
    m = ADAM_B1 * m + (1.0 - ADAM_B1) * g
    v = ADAM_B2 * v + (1.0 - ADAM_B2) * (g * g)
    m_hat = m / (1.0 - ADAM_B1 ** ADAM_STEP)
    v_hat = v / (1.0 - ADAM_B2 ** ADAM_STEP)
    delta = -ADAM_LR * (m_hat / (jnp.sqrt(v_hat) + ADAM_EPS) + ADAM_WD * w)
    return delta, m, v


def _adamw(w, g, m, v, name):
    L, K, N = w.shape
    bk = _row_block(K, N)

    def body(w_ref, g_ref, m_ref, v_ref, d_ref, nm_ref, nv_ref, go_ref):
        g = g_ref[...]
        d_ref[...], nm_ref[...], nv_ref[...] = _adamw_math(w_ref[...], g, m_ref[...], v_ref[...])
        go_ref[...] = g

    blk = pl.BlockSpec((None, bk, N), lambda l, i: (l, i, 0))
    sds = jax.ShapeDtypeStruct((L, K, N), F32)
    return pl.pallas_call(
        body, name=name, grid=(L, K // bk), in_specs=[blk] * 4, out_specs=[blk] * 4, out_shape=[sds] * 4,
        compiler_params=_params("parallel", "parallel"),
    )(w, g, m, v)


def _adamw_small(ws, gs, ms, vs):
    n = len(ws)

    def body(*refs):
        w, g, m, v, d, nm, nv = (refs[k * n:(k + 1) * n] for k in range(7))
        for k in range(n):
            d[k][...], nm[k][...], nv[k][...] = _adamw_math(w[k][...], g[k][...], m[k][...], v[k][...])

    sds = [jax.ShapeDtypeStruct(a.shape, F32) for a in ws]
    out = pl.pallas_call(body, name="adamw_small", out_shape=sds * 3)(*ws, *gs, *ms, *vs)
    return out[:n], out[n:2 * n], out[2 * n:]


def _cast_bf16(w, layer, place_arr, name):
    _, K, N = w.shape
    bk = _row_block(K, N)

    def body(k_ref, w_ref, o_ref):
        o_ref[...] = w_ref[...].astype(BF16)

    grid_spec = pltpu.PrefetchScalarGridSpec(
        num_scalar_prefetch=1, grid=(K // bk,),
        in_specs=[pl.BlockSpec((None, bk, N), lambda i, k_ref: (layer, i, 0))],
        out_specs=pl.BlockSpec((None, bk, N), lambda i, k_ref: (k_ref[0], i, 0)))
    return pl.pallas_call(
        body, name=name, grid_spec=grid_spec, out_shape=jax.ShapeDtypeStruct((N_CHIP, K, N), BF16),
        compiler_params=_params("parallel"),
    )(place_arr, w)


BIG = ("w_in", "w_out_a", "w_out_b", "w_o", "w_ffn_gate", "w_ffn_up", "w_ffn_down")
BIG_KEY = dict(w_in="win", w_out_a="woa", w_out_b="wob", w_o="wo", w_ffn_gate="wg", w_ffn_up="wu", w_ffn_down="wd")
SHARDED_SMALL = ("conv_a_w", "conv_b_w", "gate_bias")
REPLICATED = ("ln1_g", "conv_b_b", "lru_wa", "lru_ba", "lru_wx", "lru_bx", "lru_lambda", "ln2_g", "final_g")
WEIGHTS = ("ln1_g", "w_in", "conv_a_w", "conv_b_w", "conv_b_b", "lru_wa", "lru_ba", "lru_wx", "lru_bx", "lru_lambda",
           "w_out_a", "w_out_b", "gate_bias", "w_o", "ln2_g", "w_ffn_gate", "w_ffn_up", "w_ffn_down", "final_g")
LANES = 1024


def _pack_rows(arrays, row_multiple):
    flat = jnp.concatenate([a.reshape(-1) for a in arrays])
    rows = -(-flat.shape[0] // LANES)
    rows = -(-rows // row_multiple) * row_multiple
    flat = jnp.pad(flat, (0, rows * LANES - flat.shape[0]))
    return flat.reshape(rows, LANES)


def _unpack_rows(buf, shapes):
    flat = buf.reshape(-1)
    out, off = [], 0
    for s in shapes:
        n = 1
        for d in s:
            n *= d
        out.append(flat[off:off + n].reshape(s))
        off += n
    return out


OUT_KEYS = ("wo", "woa", "wob")
FFN_KEYS = ("wg", "wu", "wd")


def _items(keys, layer):
    return [(k, layer) for k in keys]


CARRY = {
    "rms_inproj_0": [("gather_ici", _items(OUT_KEYS + FFN_KEYS, 0))],
    "mixer_fwd_0": [("gather_d2d", _items(OUT_KEYS + FFN_KEYS, 0)), ("gather_ici", _items(("win",) + OUT_KEYS, 1))],
    "merge_fwd_0": [("gather_d2d", _items(("win",) + OUT_KEYS, 1))],
    "ffn_fwd_0": [("gather_ici", _items(FFN_KEYS, 1))],
    "rms_inproj_1": [("gather_d2d", _items(FFN_KEYS, 1))],
    "merge_bwd_1": [("halves", _items(FFN_KEYS, 1))],
    "mixer_bwd_1": [("chips", _items(FFN_KEYS, 1)), ("halves", _items(OUT_KEYS, 1))],
    "inproj_bwd_1": [("chips", _items(OUT_KEYS, 1)), ("share", _items(FFN_KEYS, 1))],
    "ffn_bwd_0": [("halves", [("win", 1)]), ("share", _items(OUT_KEYS, 1))],
    "wgrad_ffn_gate_up_0": [("chips", [("win", 1)])],
    "wgrad_ffn_down_0": [("share", [("win", 1)])],
    "merge_bwd_0": [("halves", _items(FFN_KEYS, 0))],
    "mixer_bwd_0": [("chips", _items(FFN_KEYS, 0)), ("halves", _items(OUT_KEYS, 0))],
    "wgrad_w_in_0": [("chips", _items(OUT_KEYS, 0)), ("share", _items(FFN_KEYS, 0))],
    "inproj_bwd_0": [("chips", [("win", 0)])],
}
AFTER = {
    "merge_bwd_1": [("add_halves", _items(FFN_KEYS, 1))],
    "mixer_bwd_1": [("add_chips", _items(FFN_KEYS, 1)), ("add_halves", _items(OUT_KEYS, 1))],
    "inproj_bwd_1": [("add_chips", _items(OUT_KEYS, 1))],
    "ffn_bwd_0": [("add_halves", [("win", 1)])],
    "wgrad_ffn_gate_up_0": [("add_chips", [("win", 1)])],
    "merge_bwd_0": [("add_halves", _items(FFN_KEYS, 0))],
    "mixer_bwd_0": [("add_chips", _items(FFN_KEYS, 0)), ("add_halves", _items(OUT_KEYS, 0))],
    "wgrad_w_in_0": [("add_chips", _items(OUT_KEYS, 0)),
                     ("run", ("reduce_halves_w_in_0", [("halves", [("win", 0)]), ("share", _items(OUT_KEYS, 0))])),
                     ("add_halves", [("win", 0)])],
    "inproj_bwd_0": [("add_chips", [("win", 0)]), ("run", ("reduce_share_w_in_0", [("share", [("win", 0)])]))],
}


class _Schedule:
    def __init__(self, slots, place_arr, n_layers):
        self.W = slots
        self.place, self.L = place_arr, n_layers
        self.g32, self.g16 = {}, {}
        self.from_sibling, self.chip_sum, self.chip_sum16, self.from_chips = {}, {}, {}, {}
        self.reduced = {}

    def stage(self, comm, kind, items):
        bf = lambda shape: jax.ShapeDtypeStruct(shape, BF16)
        for it in items:
            if kind == "gather_ici":
                comm.add(_gather_ici, 3, io=[(self.W, it)])
            elif kind == "gather_d2d":
                comm.add(_gather_d2d, 3, io=[(self.W, it)])
            elif kind == "halves":
                nj, K, N = self.g16[it].shape
                comm.add(_reduce_halves, 1, ro=[self.g16[it]], nw=[(self.from_sibling, it, bf((nj, K // 2, N)))])
            elif kind == "chips":
                _, hk, N = self.chip_sum16[it].shape
                comm.add(_reduce_chips, 3, ro=[self.chip_sum16[it]], nw=[(self.from_chips, it, bf((3, hk, N)))])
            elif kind == "share":
                comm.add(_reduce_share(it[1]), 1, io=[(self.reduced, it[0])])
        return comm

    def carry(self, name):
        comm = _Carried()
        for kind, items in CARRY.get(name, ()):
            self.stage(comm, kind, items)
        return comm

    def run(self, name, rounds):
        _run_comm([self.stage(_Carried(), kind, items) for kind, items in rounds], name)

    def grad(self, key, layer, f32, b16):
        by_chip = lambda g: g.reshape(N_CHIP, -1, g.shape[-1])
        self.g32[key, layer], self.g16[key, layer] = by_chip(f32), by_chip(b16)

    def add(self, kind, items):
        for key, layer in items:
            it = (key, layer)
            if kind == "add_halves":
                self.chip_sum[it], self.chip_sum16[it] = _add_halves(self.g32[it], self.from_sibling[it], self.place,
                                                                    f"add_halves_{key}_{layer}")
            else:
                self.reduced[key] = _add_chips(self.chip_sum[it], self.from_chips[it], self.place, layer, self.L,
                                               self.reduced.get(key), f"add_chips_{key}_{layer}")

    def after(self, name):
        for kind, items in AFTER.get(name, ()):
            if kind == "run":
                self.run(*items)
            else:
                self.add(kind, items)


def _step(w, m, v, x, target):
    xi, yi, ci = lax.axis_index("x"), lax.axis_index("y"), lax.axis_index("c")
    chip = _chip_id(xi, yi)
    place_arr = jnp.stack([chip, ci]).astype(jnp.int32)
    L = w["ln1_g"].shape[0]
    assert L == 2
    D = x.shape[1]
    dc = D // N_CHIP

    stored = lambda n, a: jnp.swapaxes(a, 1, 2) if n in ("w_ffn_gate", "w_ffn_up") else a
    slots = {("win", 0): _cast_bf16(w["w_in"], 0, place_arr, "cast_w_in_0")}
    send_sems, recv_sems, slots["win", 0], token = _gather_first_start(slots["win", 0])
    place_after = place_arr + token[0, 0]
    for n in BIG:
        for l in range(L):
            if (BIG_KEY[n], l) not in slots:
                slots[BIG_KEY[n], l] = _cast_bf16(stored(n, w[n]), l, place_after, f"cast_{n}_{l}")
    slots["win", 0] = _gather_first_wait(send_sems, recv_sems, slots["win", 0], slots["wd", L - 1])
    sched = _Schedule(slots, place_arr, L)
    small_shard = jnp.concatenate([w[n] for n in SHARDED_SMALL], axis=1)
    got = {}
    second = sched.stage(_Carried(), "gather_d2d", [("win", 0)])
    second.add(_gather_small, 3, ro=[small_shard], nw=[(got, "small", jax.ShapeDtypeStruct((3,) + small_shard.shape, F32))])
    _run_comm([second], "gather_first_d2d")
    small_g = jnp.zeros((N_CHIP,) + small_shard.shape, F32)
    small_g = lax.dynamic_update_index_in_dim(small_g, small_shard, chip, 0)
    for j, (cx, cy) in enumerate([(1 - xi, yi), (xi, 1 - yi), (1 - xi, 1 - yi)]):
        small_g = lax.dynamic_update_index_in_dim(small_g, got["small"][j], _chip_id(cx, cy), 0)
    small_full = jnp.transpose(small_g, (1, 2, 0, 3)).reshape(L, small_shard.shape[1], D)
    small = {n: w[n] for n in REPLICATED}
    off = 0
    for n in SHARDED_SMALL:
        k = w[n].shape[1]
        small[n] = small_full[:, off:off + k]
        off += k

    loss_row, grad_x, gsmall = _local_step(x, target, sched.W, small, _tiles(x.shape[0]), sched)

    grads = {}

    order = [n for n in WEIGHTS if n not in BIG]
    packed = _pack_rows([gsmall[n] for n in order] + [loss_row], 8 * SUBLANES)
    summed = _small_allreduce(packed)
    parts = _unpack_rows(summed, [gsmall[n].shape for n in order] + [loss_row.shape])
    loss = jnp.sum(parts[-1])
    for n, g in zip(order, parts[:-1]):
        grads[n] = lax.dynamic_slice_in_dim(g, chip * dc, dc, axis=2) if n in SHARDED_SMALL else g

    delta, new_m, new_v = {}, {}, {}
    for n in BIG:
        d, nm, nv, g = _adamw(stored(n, w[n]), sched.reduced[BIG_KEY[n]], stored(n, m[n]), stored(n, v[n]), f"adamw_{n}")
        delta[n], new_m[n], new_v[n], grads[n] = stored(n, d), stored(n, nm), stored(n, nv), stored(n, g)
    for d, arrays in zip((delta, new_m, new_v), _adamw_small(*([d[n] for n in order] for d in (w, grads, m, v)))):
        d.update(zip(order, arrays))
    return loss, grad_x, grads, delta, new_m, new_v


def kernel(x, ln1_g, w_in, conv_a_w, conv_b_w, conv_b_b, lru_wa, lru_ba, lru_wx, lru_bx, lru_lambda, w_out_a, w_out_b, gate_bias, w_o, ln2_g, w_ffn_gate, w_ffn_up, w_ffn_down, final_g, loss_target, m_ln1_g, m_w_in, m_conv_a_w, m_conv_b_w, m_conv_b_b, m_lru_wa, m_lru_ba, m_lru_wx, m_lru_bx, m_lru_lambda, m_w_out_a, m_w_out_b, m_gate_bias, m_w_o, m_ln2_g, m_w_ffn_gate, m_w_ffn_up, m_w_ffn_down, m_final_g, v_ln1_g, v_w_in, v_conv_a_w, v_conv_b_w, v_conv_b_b, v_lru_wa, v_lru_ba, v_lru_wx, v_lru_bx, v_lru_lambda, v_w_out_a, v_w_out_b, v_gate_bias, v_w_o, v_ln2_g, v_w_ffn_gate, v_w_ffn_up, v_w_ffn_down, v_final_g):
    w = dict(ln1_g=ln1_g, w_in=w_in, conv_a_w=conv_a_w, conv_b_w=conv_b_w, conv_b_b=conv_b_b, lru_wa=lru_wa, lru_ba=lru_ba,
             lru_wx=lru_wx, lru_bx=lru_bx, lru_lambda=lru_lambda, w_out_a=w_out_a, w_out_b=w_out_b, gate_bias=gate_bias, w_o=w_o,
             ln2_g=ln2_g, w_ffn_gate=w_ffn_gate, w_ffn_up=w_ffn_up, w_ffn_down=w_ffn_down, final_g=final_g)
    m = dict(ln1_g=m_ln1_g, w_in=m_w_in, conv_a_w=m_conv_a_w, conv_b_w=m_conv_b_w, conv_b_b=m_conv_b_b, lru_wa=m_lru_wa,
             lru_ba=m_lru_ba, lru_wx=m_lru_wx, lru_bx=m_lru_bx, lru_lambda=m_lru_lambda, w_out_a=m_w_out_a, w_out_b=m_w_out_b,
             gate_bias=m_gate_bias, w_o=m_w_o, ln2_g=m_ln2_g, w_ffn_gate=m_w_ffn_gate, w_ffn_up=m_w_ffn_up,
             w_ffn_down=m_w_ffn_down, final_g=m_final_g)
    v = dict(ln1_g=v_ln1_g, w_in=v_w_in, conv_a_w=v_conv_a_w, conv_b_w=v_conv_b_w, conv_b_b=v_conv_b_b, lru_wa=v_lru_wa,
             lru_ba=v_lru_ba, lru_wx=v_lru_wx, lru_bx=v_lru_bx, lru_lambda=v_lru_lambda, w_out_a=v_w_out_a, w_out_b=v_w_out_b,
             gate_bias=v_gate_bias, w_o=v_w_o, ln2_g=v_ln2_g, w_ffn_gate=v_w_ffn_gate, w_ffn_up=v_w_ffn_up,
             w_ffn_down=v_w_ffn_down, final_g=v_final_g)
    loss, grad_x, grads, delta, new_m, new_v = _step(w, m, v, x[0], loss_target[0])
    return (loss, grad_x[None], *[grads[n] for n in WEIGHTS], *[delta[n] for n in WEIGHTS],
            *[new_m[n] for n in WEIGHTS], *[new_v[n] for n in WEIGHTS])
```

```python
import jax
import jax.numpy as jnp
from jax import lax
from jax.experimental import pallas as pl
from jax.experimental.pallas import tpu as pltpu

F32 = jnp.float32
BF16 = jnp.bfloat16
MESH = pl.DeviceIdType.MESH

N_CHIP = 4
RMS_EPS = 1e-6
LRU_C = 8.0
LRU_HEAD_DIM = 64
LRU_BLOCK = 256
CONV_A_K = 3
CONV_B_K = 4
ADAM_LR = 0.001
ADAM_B1 = 0.9
ADAM_B2 = 0.999
ADAM_EPS = 1e-08
ADAM_WD = 0.01
ADAM_STEP = 10
SUBLANES = 8
VMEM_LIMIT = 56 * 1024 * 1024


def _params(*sem):
    return pltpu.CompilerParams(dimension_semantics=sem, vmem_limit_bytes=VMEM_LIMIT)


def _sigmoid(v):
    return 1.0 / (1.0 + jnp.exp(-v))


def _one_minus_sq(la, a):
    return jnp.tanh(-la) * (1.0 + a * a)


def _gelu_parts(v):
    k = 0.7978845608028654
    v2 = v * v
    t = jnp.tanh(k * (v + 0.044715 * v * v2))
    gelu = 0.5 * v * (1.0 + t)
    dgelu = 0.5 * (1.0 + t) + 0.5 * v * (1.0 - t * t) * k * (1.0 + 3 * 0.044715 * v2)
    return gelu, dgelu


def _shift_down(v, k, prev8):
    rolled = pltpu.roll(v, k, 0)
    r8 = lax.broadcasted_iota(jnp.int32, prev8.shape, 0)
    head = jnp.where(r8 < k, pltpu.roll(prev8, k, 0), rolled[0:SUBLANES])
    return jnp.concatenate([head, rolled[SUBLANES:]], axis=0)


def _shift_up(v, k, next8):
    tm = v.shape[0]
    rolled = pltpu.roll(v, tm - k, 0)
    r8 = lax.broadcasted_iota(jnp.int32, next8.shape, 0)
    tail = jnp.where(r8 >= SUBLANES - k, pltpu.roll(next8, SUBLANES - k, 0), rolled[tm - SUBLANES:])
    return jnp.concatenate([rolled[:tm - SUBLANES], tail], axis=0)


def _group_scan(a, b, reverse):
    tm, c = a.shape
    a = a.reshape(tm // SUBLANES, SUBLANES, c)
    b = b.reshape(tm // SUBLANES, SUBLANES, c)
    q = lax.broadcasted_iota(jnp.int32, a.shape, 1)
    for s in (1, 2, 4):
        msk = q < SUBLANES - s if reverse else q >= s
        shift = SUBLANES - s if reverse else s
        b = jnp.where(msk, a * pltpu.roll(b, shift, 1) + b, b)
        a = jnp.where(msk, a * pltpu.roll(a, shift, 1), a)
    return a.reshape(tm, c), b.reshape(tm, c)


def _colsum8(v):
    tm, c = v.shape
    return jnp.sum(v.reshape(tm // SUBLANES, SUBLANES, c), axis=0)


def _rms_stats(xv):
    var = jnp.mean(xv * xv, axis=-1, keepdims=True)
    return lax.rsqrt(var + RMS_EPS)


def _rms_bwd(dh, xv, g):
    rstd = _rms_stats(xv)
    xhat = xv * rstd
    dxhat = dh * g
    dx = rstd * (dxhat - xhat * jnp.mean(dxhat * xhat, axis=-1, keepdims=True))
    return dx, dh * xhat


ANY = pl.BlockSpec(memory_space=pl.ANY)


def _place():
    x, y, c = lax.axis_index("x"), lax.axis_index("y"), lax.axis_index("c")
    other_chips = [(1 - x, y), (x, 1 - y), (1 - x, 1 - y)]
    return x, y, c, other_chips


def _chip_id(x, y):
    return 2 * x + y


def _half(c, hk):
    return pl.ds(pl.multiple_of(c * hk, 16), hk)


def _remote(src, dst, to, sems):
    return pltpu.make_async_remote_copy(src_ref=src, dst_ref=dst, device_id=to, device_id_type=MESH, **sems)


class _Carried:
    def __init__(self):
        self.ro, self.io, self.nw, self.parts, self.n = [], [], [], [], 0

    def add(self, maker, n, ro=(), io=(), nw=()):
        def index(items, item, same):
            for k, other in enumerate(items):
                if same(other, item):
                    return k
            items.append(item)
            return len(items) - 1

        r = [index(self.ro, a, lambda p, q: p is q) for a in ro]
        i = [index(self.io, a, lambda p, q: p[0] is q[0] and p[1] == q[1]) for a in io]
        w = [index(self.nw, a, lambda p, q: False) for a in nw]
        self.parts.append((maker, r, i, w, self.n))
        self.n += n
        return self

    def pairs(self, ro, io, nw, ssem, rsem):
        out = []
        for maker, r, i, w, base in self.parts:
            sems = lambda k, base=base: dict(send_sem=ssem.at[base + k], recv_sem=rsem.at[base + k])
            out += maker([ro[k] for k in r], [io[k] for k in i], [nw[k] for k in w], sems)
        return out

    def start(self, *refs):
        for send, _ in self.pairs(*refs):
            send.start()

    def finish(self, *refs):
        pairs = self.pairs(*refs)
        for _, recv in pairs:
            recv.wait_recv()
        for send, _ in pairs:
            send.wait_send()

    def operands(self):
        return list(self.ro) + [store[key] for store, key in self.io]

    def out_shapes(self):
        return [jax.ShapeDtypeStruct(store[key].shape, store[key].dtype) for store, key in self.io] + [s for _, _, s in self.nw]

    def keep(self, results):
        for (store, key), arr in zip(self.io, results[:len(self.io)]):
            store[key] = arr
        for (store, key, _), arr in zip(self.nw, results[len(self.io):]):
            store[key] = arr


def _call(body, comm, *, name, grid, in_specs, out_specs, out_shape, compiler_params, scratch_shapes=(), aliases=None):
    aliases = dict(aliases or {})
    if comm is None or not comm.parts:
        return pl.pallas_call(body, name=name, grid=grid, in_specs=in_specs, out_specs=out_specs, out_shape=out_shape,
                              scratch_shapes=list(scratch_shapes), input_output_aliases=aliases, compiler_params=compiler_params)
    n_in, n_out, n_scr = len(in_specs), len(out_shape), len(scratch_shapes)
    n_ro, n_io, n_nw = len(comm.ro), len(comm.io), len(comm.nw)

    def carried(*refs):
        base_in = refs[:n_in]
        ro = refs[n_in:n_in + n_ro]
        pos = n_in + n_ro + n_io
        base_out = refs[pos:pos + n_out]
        io = refs[pos + n_out:pos + n_out + n_io]
        nw = refs[pos + n_out + n_io:pos + n_out + n_io + n_nw]
        pos += n_out + n_io + n_nw
        scr = refs[pos:pos + n_scr]
        ssem, rsem = refs[pos + n_scr], refs[pos + n_scr + 1]
        first = pl.program_id(0) == 0
        last = pl.program_id(0) == grid[0] - 1
        for axis in range(1, len(grid)):
            first = first & (pl.program_id(axis) == 0)
            last = last & (pl.program_id(axis) == grid[axis] - 1)

        @pl.when(first)
        def _():
            comm.start(ro, io, nw, ssem, rsem)

        body(*base_in, *base_out, *scr)

        @pl.when(last)
        def _():
            comm.finish(ro, io, nw, ssem, rsem)

    aliases.update({n_in + n_ro + k: n_out + k for k in range(n_io)})
    dma = pltpu.SemaphoreType.DMA
    call = pl.pallas_call(
        carried, name=name, grid=grid,
        in_specs=list(in_specs) + [ANY] * (n_ro + n_io), out_specs=list(out_specs) + [ANY] * (n_io + n_nw),
        out_shape=list(out_shape) + comm.out_shapes(), input_output_aliases=aliases,
        scratch_shapes=list(scratch_shapes) + [dma((comm.n,)), dma((comm.n,))], compiler_params=compiler_params)

    def run(*operands):
        res = call(*operands, *comm.operands())
        comm.keep(res[n_out:])
        return res[:n_out]

    return run


def _run_comm(rounds, name):
    ro, io, nw, uses = [], [], [], []
    for r in rounds:
        def index(items, item, same):
            for k, other in enumerate(items):
                if same(other, item):
                    return k
            items.append(item)
            return len(items) - 1
        uses.append(([index(ro, a, lambda p, q: p is q) for a in r.ro],
                     [index(io, a, lambda p, q: p[0] is q[0] and p[1] == q[1]) for a in r.io],
                     [index(nw, a, lambda p, q: False) for a in r.nw]))
    n_ro, n_io, n_nw = len(ro), len(io), len(nw)

    def body(*refs):
        ro_refs = refs[:n_ro]
        io_refs = refs[n_ro + n_io:n_ro + 2 * n_io]
        nw_refs = refs[n_ro + 2 * n_io:n_ro + 2 * n_io + n_nw]
        sems = refs[n_ro + 2 * n_io + n_nw:]
        for k, (r, (a, b, c)) in enumerate(zip(rounds, uses)):
            args = ([ro_refs[i] for i in a], [io_refs[i] for i in b], [nw_refs[i] for i in c], sems[2 * k], sems[2 * k + 1])
            r.start(*args)
            r.finish(*args)

    operands = ro + [store[key] for store, key in io]
    out_shape = [jax.ShapeDtypeStruct(store[key].shape, store[key].dtype) for store, key in io] + [s for _, _, s in nw]
    dma = pltpu.SemaphoreType.DMA
    res = pl.pallas_call(
        body, name=name, out_shape=out_shape,
        in_specs=[ANY] * (n_ro + n_io), out_specs=[ANY] * (n_io + n_nw),
        input_output_aliases={n_ro + k: k for k in range(n_io)},
        scratch_shapes=[dma((r.n,)) for r in rounds for _ in range(2)],
    )(*operands)
    for (store, key), arr in zip(io, res[:n_io]):
        store[key] = arr
    for (store, key, _), arr in zip(nw, res[n_io:]):
        store[key] = arr


HBM = pl.BlockSpec(memory_space=pltpu.HBM)
SEM = pl.BlockSpec(memory_space=pltpu.SEMAPHORE)
SPLIT_COPY = pltpu.CompilerParams(has_side_effects=pltpu.SideEffectType.DATAFLOW_SIDE_EFFECTING)


def _first_gather_copies(slot, send_sems, recv_sems):
    x, y, c, chips = _place()
    hk = slot.shape[1] // 2
    mine = slot.at[_chip_id(x, y), _half(c, hk)]
    pairs = []
    for j, (cx, cy) in enumerate(chips):
        theirs = slot.at[_chip_id(cx, cy), _half(c, hk)]
        sems = dict(send_sem=send_sems.at[j], recv_sem=recv_sems.at[j])
        pairs.append((_remote(mine, mine, (cx, cy, c), sems), _remote(theirs, theirs, (cx, cy, c), sems)))
    return pairs


def _gather_first_start(slot):
    def body(slot_ref, send_sems, recv_sems, slot_thru, token):
        for send, _ in _first_gather_copies(slot_ref, send_sems, recv_sems):
            send.start()
        token[...] = jnp.zeros_like(token)

    dma = pltpu.SemaphoreType.DMA
    return pl.pallas_call(
        body, name="gather_first_start",
        out_shape=(dma((3,)), dma((3,)), pltpu.HBM(slot.shape, slot.dtype), jax.ShapeDtypeStruct((SUBLANES, 128), jnp.int32)),
        in_specs=(HBM,), out_specs=(SEM, SEM, HBM, pl.BlockSpec(memory_space=pltpu.VMEM)), input_output_aliases={0: 2},
        compiler_params=SPLIT_COPY,
    )(pltpu.with_memory_space_constraint(slot, pltpu.HBM))


def _gather_first_wait(send_sems, recv_sems, slot, after):
    def body(slot_ref, send_sems, recv_sems, *rest):
        for send, recv in _first_gather_copies(slot_ref, send_sems, recv_sems):
            send.wait_send()
            recv.wait_recv()

    return pl.pallas_call(
        body, name="gather_first_wait", out_shape=(pltpu.HBM(slot.shape, slot.dtype),),
        in_specs=(HBM, SEM, SEM) + (ANY,) * len(after), out_specs=(HBM,), input_output_aliases={0: 0},
        compiler_params=SPLIT_COPY,
    )(slot, send_sems, recv_sems, *after)[0]


def _gather_ici(ro, io, nw, sems):
    s = io[0]
    x, y, c, chips = _place()
    hk = s.shape[1] // 2
    mine = s.at[_chip_id(x, y), _half(c, hk)]
    pairs = []
    for j, (cx, cy) in enumerate(chips):
        theirs = s.at[_chip_id(cx, cy), _half(c, hk)]
        pairs.append((_remote(mine, mine, (cx, cy, c), sems(j)), _remote(theirs, theirs, (cx, cy, c), sems(j))))
    return pairs


def _gather_d2d(ro, io, nw, sems):
    s = io[0]
    x, y, c, chips = _place()
    hk = s.shape[1] // 2
    sib = (x, y, 1 - c)
    pairs = []
    for j, (cx, cy) in enumerate(chips):
        here = s.at[_chip_id(cx, cy), _half(c, hk)]
        there = s.at[_chip_id(cx, cy), _half(1 - c, hk)]
        pairs.append((_remote(here, here, sib, sems(j)), _remote(there, there, sib, sems(j))))
    return pairs


def _gather_small(ro, io, nw, sems):
    x, y, c, chips = _place()
    return [(_remote(ro[0], nw[0].at[j], (cx, cy, c), sems(j)),) * 2 for j, (cx, cy) in enumerate(chips)]


def _reduce_halves(ro, io, nw, sems):
    x, y, c, _ = _place()
    g = ro[0]
    hk = g.shape[1] // 2
    sib = (x, y, 1 - c)
    return [(_remote(g.at[:, _half(1 - c, hk)], nw[0], sib, sems(0)), _remote(g.at[:, _half(c, hk)], nw[0], sib, sems(0)))]


def _reduce_chips(ro, io, nw, sems):
    x, y, c, chips = _place()
    return [(_remote(ro[0].at[_chip_id(cx, cy)], nw[0].at[j], (cx, cy, c), sems(j)),) * 2 for j, (cx, cy) in enumerate(chips)]


def _reduce_share(layer):
    def maker(ro, io, nw, sems):
        g = io[0]
        x, y, c, _ = _place()
        hk = g.shape[1] // 2
        sib = (x, y, 1 - c)
        mine, theirs = g.at[layer, _half(c, hk)], g.at[layer, _half(1 - c, hk)]
        return [(_remote(mine, mine, sib, sems(0)), _remote(theirs, theirs, sib, sems(0)))]
    return maker


def _rms_inproj(x, g_row, win, layer, tm, comm=None):
    T, D = x.shape
    ns = win.shape[-1]

    def body(x_ref, g_ref, w_ref, p_ref, h_ref):
        @pl.when(pl.program_id(1) == 0)
        def _():
            xv = x_ref[...]
            h_ref[...] = (xv * _rms_stats(xv) * g_ref[...]).astype(BF16)
        p_ref[...] = jnp.dot(h_ref[...], w_ref[...], preferred_element_type=F32).astype(BF16)

    return _call(
        body, comm, name=f"rms_inproj_{layer}", grid=(T // tm, N_CHIP),
        in_specs=[pl.BlockSpec((tm, D), lambda i, j: (i, 0)),
                  pl.BlockSpec((1, D), lambda i, j: (0, 0)),
                  pl.BlockSpec((None, D, ns), lambda i, j: (j, 0, 0))],
        out_specs=[pl.BlockSpec((tm, ns), lambda i, j: (i, j)),
                   pl.BlockSpec((tm, D), lambda i, j: (i, 0))],
        out_shape=[jax.ShapeDtypeStruct((T, N_CHIP * ns), BF16), jax.ShapeDtypeStruct((T, D), BF16)],
        compiler_params=_params("parallel", "arbitrary"),
    )(x, g_row, win)


def _mixer_recompute(ca, xa, xb, zprev8, xbprev8, cw_ref, wab_ref, sp):
    row = lambda k: cw_ref[pl.ds(k, 1), :]
    z = ca * xa
    z1 = _shift_down(z, 1, zprev8)
    z2 = _shift_down(z, 2, zprev8)
    cz = row(2) * z + row(1) * z1 + row(0) * z2
    x1 = _shift_down(xb, 1, xbprev8)
    x2 = _shift_down(xb, 2, xbprev8)
    x3 = _shift_down(xb, 3, xbprev8)
    u = row(6) * xb + row(5) * x1 + row(4) * x2 + row(3) * x3 + row(7)
    ub = u.astype(BF16)
    nb = wab_ref.shape[0]
    ras, ixs = [], []
    for b in range(nb):
        ri = jnp.dot(ub[:, b * LRU_BLOCK:(b + 1) * LRU_BLOCK], wab_ref[b], preferred_element_type=F32)
        ras.append(ri[:, :LRU_BLOCK])
        ixs.append(ri[:, LRU_BLOCK:])
    r = _sigmoid(jnp.concatenate(ras, axis=1) + row(8))
    gi = _sigmoid(jnp.concatenate(ixs, axis=1) + row(9))
    la = (-LRU_C) * r * sp
    a = jnp.exp(la)
    m = jnp.sqrt(_one_minus_sq(la, a))
    return dict(z=z, z1=z1, z2=z2, cz=cz, x1=x1, x2=x2, x3=x3, u=u, ub=ub, r=r, gi=gi, a=a, m=m)


def _softplus_neg(lam):
    v = -lam
    return jnp.maximum(v, 0.0) + jnp.log1p(jnp.exp(-jnp.abs(v)))


def _mixer_fwd(p, cw, wab, lam_row, layer, tm, comm=None):
    T = p.shape[0]
    D = p.shape[1] // 7
    ngroups = tm // SUBLANES

    def body(ba_ref, ca_ref, xa_ref, xb_ref, gb_ref, cw_ref, wab_ref, lam_ref, ya_ref, yb_ref, h_ref,
             zprev, xbprev, hcarry, a_s, h_s):
        @pl.when(pl.program_id(0) == 0)
        def _():
            zprev[...] = jnp.zeros_like(zprev)
            xbprev[...] = jnp.zeros_like(xbprev)
            hcarry[...] = jnp.zeros_like(hcarry)

        ca = ca_ref[...].astype(F32)
        xa = xa_ref[...].astype(F32)
        xb = xb_ref[...].astype(F32)
        sp = _softplus_neg(lam_ref[...])
        c = _mixer_recompute(ca, xa, xb, zprev[...], xbprev[...], cw_ref, wab_ref, sp)
        zprev[...] = c["z"][tm - SUBLANES:]
        xbprev[...] = xb[tm - SUBLANES:]
        ya_ref[...] = (ba_ref[...].astype(F32) * c["cz"]).astype(BF16)

        a_s[...], h_s[...] = _group_scan(c["a"], c["m"] * c["gi"] * c["u"], reverse=False)

        def step(g, carry):
            off = pl.multiple_of(g * SUBLANES, SUBLANES)
            hg = h_s[pl.ds(off, SUBLANES), :] + a_s[pl.ds(off, SUBLANES), :] * carry
            h_s[pl.ds(off, SUBLANES), :] = hg
            return jnp.broadcast_to(hg[SUBLANES - 1:SUBLANES, :], hg.shape)

        hcarry[...] = lax.fori_loop(0, ngroups, step, hcarry[...], unroll=4)
        h = h_s[...]
        h_ref[...] = h
        gelu, _ = _gelu_parts(gb_ref[...].astype(F32))
        yb_ref[...] = (h * gelu).astype(BF16)

    col = lambda k: pl.BlockSpec((tm, D), lambda i: (i, k))
    full = lambda a: pl.BlockSpec(a.shape, lambda i: (0,) * a.ndim)
    tok = pl.BlockSpec((tm, D), lambda i: (i, 0))
    return _call(
        body, comm, name=f"mixer_fwd_{layer}", grid=(T // tm,),
        in_specs=[col(0), col(1), col(2), col(3), col(4), full(cw), full(wab), full(lam_row)],
        out_specs=[tok, tok, tok],
        out_shape=[jax.ShapeDtypeStruct((T, D), BF16), jax.ShapeDtypeStruct((T, D), BF16), jax.ShapeDtypeStruct((T, D), F32)],
        scratch_shapes=[pltpu.VMEM((SUBLANES, D), F32), pltpu.VMEM((SUBLANES, D), F32), pltpu.VMEM((SUBLANES, D), F32),
                        pltpu.VMEM((tm, D), F32), pltpu.VMEM((tm, D), F32)],
        compiler_params=_params("arbitrary"),
    )(p, p, p, p, p, cw, wab, lam_row)


def _merge_fwd(x, p, ya, yb, woa, wob, wo, gbias, layer, tm, comm=None):
    T, D = x.shape

    def body(x_ref, ga_ref, gb_ref, ya_ref, yb_ref, woa_ref, wob_ref, wo_ref, bias_ref, oa_ref, ob_ref, mg_ref, x1_ref):
        oa = jnp.dot(ya_ref[...], woa_ref[...], preferred_element_type=F32)
        ob = jnp.dot(yb_ref[...], wob_ref[...], preferred_element_type=F32)
        sa = _sigmoid(ga_ref[...].astype(F32) + bias_ref[pl.ds(0, 1), :])
        sb = _sigmoid(gb_ref[...].astype(F32) + bias_ref[pl.ds(1, 1), :])
        mg = (sa * oa + sb * ob).astype(BF16)
        oa_ref[...] = oa.astype(BF16)
        ob_ref[...] = ob.astype(BF16)
        mg_ref[...] = mg
        x1_ref[...] = x_ref[...] + jnp.dot(mg, wo_ref[...], preferred_element_type=F32)

    tok = pl.BlockSpec((tm, D), lambda i: (i, 0))
    wsp = pl.BlockSpec((D, D), lambda i: (0, 0))
    bf = jax.ShapeDtypeStruct((T, D), BF16)
    return _call(
        body, comm, name=f"merge_fwd_{layer}", grid=(T // tm,),
        in_specs=[tok, pl.BlockSpec((tm, D), lambda i: (i, 5)), pl.BlockSpec((tm, D), lambda i: (i, 6)), tok, tok,
                  wsp, wsp, wsp, pl.BlockSpec(gbias.shape, lambda i: (0, 0))],
        out_specs=[tok, tok, tok, tok],
        out_shape=[bf, bf, bf, jax.ShapeDtypeStruct((T, D), F32)],
        compiler_params=_params("parallel"),
    )(x, p, p, ya, yb, woa, wob, wo, gbias)


def _loss_tile(xv, g, tgt):
    d = xv.shape[-1]
    rstd = _rms_stats(xv)
    xhat = xv * rstd
    err = xhat * g - tgt
    dy = err * (1.0 / d)
    dxhat = dy * g
    dx = rstd * (dxhat - xhat * jnp.mean(dxhat * xhat, axis=-1, keepdims=True))
    return dx, _colsum8(err * err), _colsum8(dy * xhat)


def _ffn_fwd(x1, g_row, wg, wu, wd, layer, tm, comm=None):
    T, D = x1.shape
    fs = wg.shape[-2]
    n = T // tm
    nt = (((1,), (1,)), ((), ()))

    def body(x_ref, g_ref, wg_ref, wu_ref, wd_ref, h_ref, gg_ref, uu_ref, x2_ref, acc):
        j = pl.program_id(1)

        @pl.when(j == 0)
        def _():
            xv = x_ref[...]
            h_ref[...] = (xv * _rms_stats(xv) * g_ref[...]).astype(BF16)
            acc[...] = xv

        h = h_ref[...]
        gg = lax.dot_general(h, wg_ref[...], nt, preferred_element_type=F32)
        uu = lax.dot_general(h, wu_ref[...], nt, preferred_element_type=F32)
        gg_ref[...] = gg.astype(BF16)
        uu_ref[...] = uu.astype(BF16)
        act = (gg * _sigmoid(gg) * uu).astype(BF16)
        acc[...] += jnp.dot(act, wd_ref[...], preferred_element_type=F32)

        @pl.when(j == N_CHIP - 1)
        def _():
            x2_ref[...] = acc[...]

    tok = pl.BlockSpec((tm, D), lambda i, j: (i, 0))
    cm = pl.BlockSpec((None, tm, fs), lambda i, j: (j, i, 0))
    wsp = pl.BlockSpec((None, fs, D), lambda i, j: (j, 0, 0))
    return _call(
        body, comm, name=f"ffn_fwd_{layer}", grid=(n, N_CHIP),
        in_specs=[tok, pl.BlockSpec((1, D), lambda i, j: (0, 0)), wsp, wsp, wsp], out_specs=[tok, cm, cm, tok],
        out_shape=[jax.ShapeDtypeStruct((T, D), BF16), jax.ShapeDtypeStruct((N_CHIP, T, fs), BF16),
                   jax.ShapeDtypeStruct((N_CHIP, T, fs), BF16), jax.ShapeDtypeStruct((T, D), F32)],
        scratch_shapes=[pltpu.VMEM((tm, D), F32)], compiler_params=_params("parallel", "arbitrary"),
    )(x1, g_row, wg, wu, wd)


def _final_loss(x, g_row, target, tm):
    T, D = x.shape
    n = T // tm

    def body(x_ref, g_ref, t_ref, dx_ref, red_ref, racc):
        i = pl.program_id(0)

        @pl.when(i == 0)
        def _():
            racc[...] = jnp.zeros_like(racc)

        dx_ref[...], sq, dg = _loss_tile(x_ref[...], g_ref[...], t_ref[...])
        racc[0] += sq
        racc[1] += dg

        @pl.when(i == n - 1)
        def _():
            red_ref[pl.ds(0, 1), :] = jnp.sum(racc[0], axis=0, keepdims=True) * (0.5 / D)
            red_ref[pl.ds(1, 1), :] = jnp.sum(racc[1], axis=0, keepdims=True)

    tok = pl.BlockSpec((tm, D), lambda i: (i, 0))
    return pl.pallas_call(
        body, name="final_loss", grid=(n,),
        in_specs=[tok, pl.BlockSpec((1, D), lambda i: (0, 0)), tok],
        out_specs=[tok, pl.BlockSpec((2, D), lambda i: (0, 0))],
        out_shape=[jax.ShapeDtypeStruct((T, D), F32), jax.ShapeDtypeStruct((2, D), F32)],
        scratch_shapes=[pltpu.VMEM((2, SUBLANES, D), F32)],
        compiler_params=_params("arbitrary"),
    )(x, g_row, target)


def _ffn_bwd_gates(dx2, gg, uu, wd, layer, tm):
    T, D = dx2.shape
    fs = wd.shape[-2]
    nt = (((1,), (1,)), ((), ()))

    def body(dx_ref, gg_ref, uu_ref, wd_ref, dg_ref, du_ref, act_ref, dxb_ref):
        @pl.when(pl.program_id(1) == 0)
        def _():
            dxb_ref[...] = dx_ref[...].astype(BF16)

        dact = lax.dot_general(dxb_ref[...], wd_ref[...], nt, preferred_element_type=F32)
        g = gg_ref[...].astype(F32)
        u = uu_ref[...].astype(F32)
        s = _sigmoid(g)
        silu = g * s
        dg_ref[...] = (dact * u * (s * (1.0 + g * (1.0 - s)))).astype(BF16)
        du_ref[...] = (dact * silu).astype(BF16)
        act_ref[...] = (silu * u).astype(BF16)

    tok = pl.BlockSpec((tm, D), lambda i, j: (i, 0))
    cm = pl.BlockSpec((None, tm, fs), lambda i, j: (j, i, 0))
    cms = jax.ShapeDtypeStruct((N_CHIP, T, fs), BF16)
    return pl.pallas_call(
        body, name=f"ffn_bwd_gates_{layer}", grid=(T // tm, N_CHIP),
        in_specs=[tok, cm, cm, pl.BlockSpec((None, fs, D), lambda i, j: (j, 0, 0))], out_specs=[cm, cm, cm, tok],
        out_shape=[cms, cms, cms, jax.ShapeDtypeStruct((T, D), BF16)],
        compiler_params=_params("parallel", "arbitrary"),
    )(dx2, gg, uu, wd)


def _ffn_bwd(dgg, duu, wg, wu, dx2, x1, g_row, layer, tm, comm=None):
    T, D = dx2.shape
    fs = wg.shape[-2]
    n = T // tm

    def body(dg_ref, du_ref, wg_ref, wu_ref, dx_ref, x_ref, g_ref, dx1_ref, red_ref, acc, racc):
        i = pl.program_id(0)
        j = pl.program_id(1)

        @pl.when((i == 0) & (j == 0))
        def _():
            racc[...] = jnp.zeros_like(racc)

        @pl.when(j == 0)
        def _():
            acc[...] = jnp.zeros_like(acc)

        acc[...] += (jnp.dot(dg_ref[...], wg_ref[...], preferred_element_type=F32)
                     + jnp.dot(du_ref[...], wu_ref[...], preferred_element_type=F32))

        @pl.when(j == N_CHIP - 1)
        def _():
            dx, dgain = _rms_bwd(acc[...], x_ref[...], g_ref[...])
            dx1_ref[...] = dx_ref[...] + dx
            racc[...] += _colsum8(dgain)

        @pl.when((i == n - 1) & (j == N_CHIP - 1))
        def _():
            red_ref[...] = jnp.sum(racc[...], axis=0, keepdims=True)

    tok = pl.BlockSpec((tm, D), lambda i, j: (i, 0))
    cm = pl.BlockSpec((None, tm, fs), lambda i, j: (j, i, 0))
    wsp = pl.BlockSpec((None, fs, D), lambda i, j: (j, 0, 0))
    row = pl.BlockSpec((1, D), lambda i, j: (0, 0))
    return _call(
        body, comm, name=f"ffn_bwd_{layer}", grid=(n, N_CHIP),
        in_specs=[cm, cm, wsp, wsp, tok, tok, row], out_specs=[tok, row],
        out_shape=[jax.ShapeDtypeStruct((T, D), F32), jax.ShapeDtypeStruct((1, D), F32)],
        scratch_shapes=[pltpu.VMEM((tm, D), F32), pltpu.VMEM((SUBLANES, D), F32)],
        compiler_params=_params("arbitrary", "arbitrary"),
    )(dgg, duu, wg, wu, dx2, x1, g_row)


def _merge_bwd(dx1, p, oa, ob, woa, wob, wo, gbias, layer, tm, comm=None):
    T, D = dx1.shape
    n = T // tm
    nt = (((1,), (1,)), ((), ()))

    def body(dx_ref, ga_ref, gb_ref, oa_ref, ob_ref, woa_ref, wob_ref, wo_ref, bias_ref,
             dya_ref, dyb_ref, doa_ref, dob_ref, dgl_ref, dxb_ref, red_ref, racc):
        i = pl.program_id(0)

        @pl.when(i == 0)
        def _():
            racc[...] = jnp.zeros_like(racc)

        dxb = dx_ref[...].astype(BF16)
        dxb_ref[...] = dxb
        dm = lax.dot_general(dxb, wo_ref[...], nt, preferred_element_type=F32)
        sa = _sigmoid(ga_ref[...].astype(F32) + bias_ref[pl.ds(0, 1), :])
        sb = _sigmoid(gb_ref[...].astype(F32) + bias_ref[pl.ds(1, 1), :])
        doa = (dm * sa).astype(BF16)
        dob = (dm * sb).astype(BF16)
        dga = dm * oa_ref[...].astype(F32) * (sa * (1.0 - sa))
        dgb = dm * ob_ref[...].astype(F32) * (sb * (1.0 - sb))
        doa_ref[...] = doa
        dob_ref[...] = dob
        dgl_ref[:, 0:D] = dga.astype(BF16)
        dgl_ref[:, D:2 * D] = dgb.astype(BF16)
        racc[0] += _colsum8(dga)
        racc[1] += _colsum8(dgb)
        dya_ref[...] = lax.dot_general(doa, woa_ref[...], nt, preferred_element_type=F32).astype(BF16)
        dyb_ref[...] = lax.dot_general(dob, wob_ref[...], nt, preferred_element_type=F32).astype(BF16)

        @pl.when(i == n - 1)
        def _():
            red_ref[pl.ds(0, 1), :] = jnp.sum(racc[0], axis=0, keepdims=True)
            red_ref[pl.ds(1, 1), :] = jnp.sum(racc[1], axis=0, keepdims=True)

    tok = pl.BlockSpec((tm, D), lambda i: (i, 0))
    wsp = pl.BlockSpec((D, D), lambda i: (0, 0))
    bf = jax.ShapeDtypeStruct((T, D), BF16)
    return _call(
        body, comm, name=f"merge_bwd_{layer}", grid=(n,),
        in_specs=[tok, pl.BlockSpec((tm, D), lambda i: (i, 5)), pl.BlockSpec((tm, D), lambda i: (i, 6)), tok, tok,
                  wsp, wsp, wsp, pl.BlockSpec(gbias.shape, lambda i: (0, 0))],
        out_specs=[tok, tok, tok, tok, pl.BlockSpec((tm, 2 * D), lambda i: (i, 0)), tok, pl.BlockSpec((2, D), lambda i: (0, 0))],
        out_shape=[bf, bf, bf, bf, jax.ShapeDtypeStruct((T, 2 * D), BF16), bf, jax.ShapeDtypeStruct((2, D), F32)],
        scratch_shapes=[pltpu.VMEM((2, SUBLANES, D), F32)],
        compiler_params=_params("arbitrary"),
    )(dx1, p, p, oa, ob, woa, wob, wo, gbias)


N_MIXER_RED = 16


def _mixer_bwd(p, hseq, dya, dyb, dgl, cw, wab, wabt, lam_row, layer, tm, comm=None):
    T = p.shape[0]
    D = p.shape[1] // 7
    n = T // tm
    ngroups = tm // SUBLANES
    nb = wab.shape[0]
    hb = 16
    tn = (((0,), (0,)), ((), ()))

    def body(ba_ref, ca_ref, xa_ref, xb_ref, gb_ref, h_ref, dya_ref, dyb_ref, dgl_ref,
             cap_ref, xap_ref, xbp_ref, hp_ref, ban_ref, dyan_ref,
             cw_ref, wab_ref, wabt_ref, lam_ref,
             dp_ref, red_ref, dwab_ref,
             racc, wacc, anext, gnext, dunext, c_s, g_s):
        i = pl.program_id(0)
        first_tile = i == n - 1
        last_tile = i == 0

        @pl.when(i == 0)
        def _():
            racc[...] = jnp.zeros_like(racc)
            wacc[...] = jnp.zeros_like(wacc)
            anext[...] = jnp.zeros_like(anext)
            gnext[...] = jnp.zeros_like(gnext)
            dunext[...] = jnp.zeros_like(dunext)

        keep_prev = jnp.where(first_tile, 0.0, 1.0)
        keep_next = jnp.where(last_tile, 0.0, 1.0)
        ba = ba_ref[...].astype(F32)
        ca = ca_ref[...].astype(F32)
        xa = xa_ref[...].astype(F32)
        xb = xb_ref[...].astype(F32)
        h = h_ref[...]
        dya = dya_ref[...].astype(F32)
        dyb = dyb_ref[...].astype(F32)
        zprev8 = (cap_ref[...].astype(F32) * xap_ref[...].astype(F32))[hb - SUBLANES:] * keep_prev
        xbprev8 = xbp_ref[...].astype(F32)[hb - SUBLANES:] * keep_prev
        hprev8 = hp_ref[...] * keep_prev
        dcznext8 = (dyan_ref[...].astype(F32) * ban_ref[...].astype(F32))[:SUBLANES] * keep_next

        lam = lam_ref[...]
        sp = _softplus_neg(lam)
        c = _mixer_recompute(ca, xa, xb, zprev8, xbprev8, cw_ref, wab_ref, sp)
        row = lambda k: cw_ref[pl.ds(k, 1), :]
        a, m, r, gi, u = c["a"], c["m"], c["r"], c["gi"], c["u"]

        gelu, dgelu = _gelu_parts(gb_ref[...].astype(F32))
        dgb = dyb * h * dgelu
        c_s[...], g_s[...] = _group_scan(_shift_up(a, 1, anext[...]), dyb * gelu, reverse=True)

        def step(k, carry):
            off = pl.multiple_of((ngroups - 1 - k) * SUBLANES, SUBLANES)
            gg = g_s[pl.ds(off, SUBLANES), :] + c_s[pl.ds(off, SUBLANES), :] * carry
            g_s[pl.ds(off, SUBLANES), :] = gg
            return jnp.broadcast_to(gg[0:1, :], gg.shape)

        gnext[...] = lax.fori_loop(0, ngroups, step, gnext[...], unroll=4)
        anext[...] = a[0:SUBLANES]
        g = g_s[...]

        hprev = _shift_down(h, 1, hprev8)
        da = g * hprev
        gm = g * m
        dgi = gm * u
        du = gm * gi
        dmv = g * gi * u
        dla = a * (da - dmv * a / m)
        dra = dla * ((-LRU_C) * sp) * (r * (1.0 - r))
        dix = dgi * (gi * (1.0 - gi))
        racc[10] += _colsum8(dla * r)
        racc[8] += _colsum8(dra)
        racc[9] += _colsum8(dix)
        drab = dra.astype(BF16)
        dixb = dix.astype(BF16)
        ub = c["ub"]
        dus = []
        for b in range(nb):
            sl = slice(b * LRU_BLOCK, (b + 1) * LRU_BLOCK)
            dri = jnp.concatenate([drab[:, sl], dixb[:, sl]], axis=1)
            dus.append(jnp.dot(dri, wabt_ref[b], preferred_element_type=F32))
            wacc[b] += lax.dot_general(ub[:, sl], dri, tn, preferred_element_type=F32)
        du = du + jnp.concatenate(dus, axis=1)

        dun = dunext[...]
        du1 = _shift_up(du, 1, dun)
        du2 = _shift_up(du, 2, dun)
        du3 = _shift_up(du, 3, dun)
        dxb = row(6) * du + row(5) * du1 + row(4) * du2 + row(3) * du3
        dunext[...] = du[0:SUBLANES]
        racc[6] += _colsum8(du * xb)
        racc[5] += _colsum8(du * c["x1"])
        racc[4] += _colsum8(du * c["x2"])
        racc[3] += _colsum8(du * c["x3"])
        racc[7] += _colsum8(du)

        dba = dya * c["cz"]
        dcz = dya * ba
        dcz1 = _shift_up(dcz, 1, dcznext8)
        dcz2 = _shift_up(dcz, 2, dcznext8)
        dz = row(2) * dcz + row(1) * dcz1 + row(0) * dcz2
        racc[2] += _colsum8(dcz * c["z"])
        racc[1] += _colsum8(dcz * c["z1"])
        racc[0] += _colsum8(dcz * c["z2"])

        dp_ref[:, 0:D] = dba.astype(BF16)
        dp_ref[:, D:2 * D] = (dz * xa).astype(BF16)
        dp_ref[:, 2 * D:3 * D] = (dz * ca).astype(BF16)
        dp_ref[:, 3 * D:4 * D] = dxb.astype(BF16)
        dp_ref[:, 4 * D:5 * D] = dgb.astype(BF16)
        dp_ref[:, 5 * D:7 * D] = dgl_ref[...]

        @pl.when(i == n - 1)
        def _():
            dlam_scale = LRU_C * _sigmoid(-lam)
            for k in range(N_MIXER_RED):
                tot = jnp.sum(racc[k], axis=0, keepdims=True)
                red_ref[pl.ds(k, 1), :] = tot * dlam_scale if k == 10 else tot
            dwab_ref[...] = wacc[...]

    rt = lambda i: n - 1 - i
    col = lambda k: pl.BlockSpec((tm, D), lambda i: (rt(i), k))
    tok = pl.BlockSpec((tm, D), lambda i: (rt(i), 0))
    full = lambda a: pl.BlockSpec(a.shape, lambda i: (0,) * a.ndim)
    prev16 = lambda k: pl.BlockSpec((hb, D), lambda i: (jnp.maximum(rt(i) * (tm // hb) - 1, 0), k))
    next16 = lambda k: pl.BlockSpec((hb, D), lambda i: (jnp.minimum((rt(i) + 1) * (tm // hb), T // hb - 1), k))
    hprev = pl.BlockSpec((SUBLANES, D), lambda i: (jnp.maximum(rt(i) * ngroups - 1, 0), 0))
    return _call(
        body, comm, name=f"mixer_bwd_{layer}", grid=(n,),
        in_specs=[col(0), col(1), col(2), col(3), col(4), tok, tok, tok, pl.BlockSpec((tm, 2 * D), lambda i: (rt(i), 0)),
                  prev16(1), prev16(2), prev16(3), hprev, next16(0), next16(0),
                  full(cw), full(wab), full(wabt), full(lam_row)],
        out_specs=[pl.BlockSpec((tm, 7 * D), lambda i: (rt(i), 0)),
                   pl.BlockSpec((N_MIXER_RED, D), lambda i: (0, 0)),
                   pl.BlockSpec((nb, LRU_BLOCK, 2 * LRU_BLOCK), lambda i: (0, 0, 0))],
        out_shape=[jax.ShapeDtypeStruct((T, 7 * D), BF16), jax.ShapeDtypeStruct((N_MIXER_RED, D), F32),
                   jax.ShapeDtypeStruct((nb, LRU_BLOCK, 2 * LRU_BLOCK), F32)],
        scratch_shapes=[pltpu.VMEM((N_MIXER_RED, SUBLANES, D), F32), pltpu.VMEM((nb, LRU_BLOCK, 2 * LRU_BLOCK), F32),
                        pltpu.VMEM((SUBLANES, D), F32), pltpu.VMEM((SUBLANES, D), F32), pltpu.VMEM((SUBLANES, D), F32),
                        pltpu.VMEM((tm, D), F32), pltpu.VMEM((tm, D), F32)],
        compiler_params=_params("arbitrary"),
    )(p, p, p, p, p, hseq, dya, dyb, dgl, p, p, p, hseq, p, dya, cw, wab, wabt, lam_row)


def _inproj_bwd(dp, dx1, x, g_row, win, layer, tm, comm=None):
    T, D = x.shape
    ns = win.shape[-1]
    n = T // tm
    nt = (((1,), (1,)), ((), ()))

    def body(dp_ref, dx_ref, x_ref, g_ref, w_ref, dx0_ref, red_ref, acc, racc):
        i = pl.program_id(0)
        j = pl.program_id(1)

        @pl.when((i == 0) & (j == 0))
        def _():
            racc[...] = jnp.zeros_like(racc)

        @pl.when(j == 0)
        def _():
            acc[...] = jnp.zeros_like(acc)

        acc[...] += lax.dot_general(dp_ref[...], w_ref[...], nt, preferred_element_type=F32)

        @pl.when(j == N_CHIP - 1)
        def _():
            dx, dgain = _rms_bwd(acc[...], x_ref[...], g_ref[...])
            dx0_ref[...] = dx_ref[...] + dx
            racc[...] += _colsum8(dgain)

        @pl.when((i == n - 1) & (j == N_CHIP - 1))
        def _():
            red_ref[...] = jnp.sum(racc[...], axis=0, keepdims=True)

    tok = pl.BlockSpec((tm, D), lambda i, j: (i, 0))
    return _call(
        body, comm, name=f"inproj_bwd_{layer}", grid=(n, N_CHIP),
        in_specs=[pl.BlockSpec((tm, ns), lambda i, j: (i, j)), tok, tok, pl.BlockSpec((1, D), lambda i, j: (0, 0)),
                  pl.BlockSpec((None, D, ns), lambda i, j: (j, 0, 0))],
        out_specs=[tok, pl.BlockSpec((1, D), lambda i, j: (0, 0))],
        out_shape=[jax.ShapeDtypeStruct((T, D), F32), jax.ShapeDtypeStruct((1, D), F32)],
        scratch_shapes=[pltpu.VMEM((tm, D), F32), pltpu.VMEM((SUBLANES, D), F32)],
        compiler_params=_params("arbitrary", "arbitrary"),
    )(dp, dx1, x, g_row, win)


def _wgrad(a, b, name, tk, a_kind="whole", b_kind="whole", nj=1, comm=None):
    T = a.shape[-2]
    width = lambda v, kind: v.shape[-1] // nj if kind == "cols" else v.shape[-1]
    ka, kb = width(a, a_kind), width(b, b_kind)
    nt = T // tk
    tn = (((0,), (0,)), ((), ()))

    def spec(k, kind):
        if kind == "cm":
            return pl.BlockSpec((None, tk, k), lambda j, t: (j, t, 0))
        if kind == "cols":
            return pl.BlockSpec((tk, k), lambda j, t: (t, j))
        return pl.BlockSpec((tk, k), lambda j, t: (t, 0))

    def body(a_ref, b_ref, o_ref, ob_ref):
        t = pl.program_id(1)

        @pl.when(t == 0)
        def _():
            o_ref[...] = jnp.zeros_like(o_ref)

        o_ref[...] += lax.dot_general(a_ref[...], b_ref[...], tn, preferred_element_type=F32)

        @pl.when(t == nt - 1)
        def _():
            ob_ref[...] = o_ref[...].astype(BF16)

    o_spec = pl.BlockSpec((None, ka, kb), lambda j, t: (j, 0, 0))
    return _call(
        body, comm, name=name, grid=(nj, nt),
        in_specs=[spec(ka, a_kind), spec(kb, b_kind)], out_specs=[o_spec, o_spec],
        out_shape=[jax.ShapeDtypeStruct((nj, ka, kb), F32), jax.ShapeDtypeStruct((nj, ka, kb), BF16)],
        compiler_params=_params("parallel", "arbitrary"),
    )(a, b)


def _wgrad_pair(a, b1, b2, name, tk, comm=None):
    T, ka = a.shape
    nj, _, kb = b1.shape
    nt = T // tk
    tn = (((0,), (0,)), ((), ()))

    def body(a_ref, b1_ref, b2_ref, o1_ref, o1b_ref, o2_ref, o2b_ref):
        t = pl.program_id(1)

        @pl.when(t == 0)
        def _():
            o1_ref[...] = jnp.zeros_like(o1_ref)
            o2_ref[...] = jnp.zeros_like(o2_ref)

        av = a_ref[...]
        o1_ref[...] += lax.dot_general(b1_ref[...], av, tn, preferred_element_type=F32)
        o2_ref[...] += lax.dot_general(b2_ref[...], av, tn, preferred_element_type=F32)

        @pl.when(t == nt - 1)
        def _():
            o1b_ref[...] = o1_ref[...].astype(BF16)
            o2b_ref[...] = o2_ref[...].astype(BF16)

    b_spec = pl.BlockSpec((None, tk, kb), lambda j, t: (j, t, 0))
    o_spec = pl.BlockSpec((None, kb, ka), lambda j, t: (j, 0, 0))
    f32 = jax.ShapeDtypeStruct((nj, kb, ka), F32)
    b16 = jax.ShapeDtypeStruct((nj, kb, ka), BF16)
    return _call(
        body, comm, name=name, grid=(nj, nt),
        in_specs=[pl.BlockSpec((tk, ka), lambda j, t: (t, 0)), b_spec, b_spec], out_specs=[o_spec] * 4,
        out_shape=[f32, b16, f32, b16], compiler_params=_params("parallel", "arbitrary"),
    )(a, b1, b2)


def _block_diag(w):
    hb = LRU_BLOCK // LRU_HEAD_DIM
    nb = w.shape[0] // hb
    w4 = w.reshape(nb, hb, LRU_HEAD_DIM, LRU_HEAD_DIM)
    eye = jnp.eye(hb, dtype=w.dtype)
    return jnp.einsum("bide,ij->bidje", w4, eye).reshape(nb, LRU_BLOCK, LRU_BLOCK)


def _diag_heads(m):
    hb = LRU_BLOCK // LRU_HEAD_DIM
    nb = m.shape[0]
    m5 = m.reshape(nb, hb, LRU_HEAD_DIM, hb, LRU_HEAD_DIM)
    eye = jnp.eye(hb, dtype=m.dtype)
    return jnp.einsum("bidje,ij->bide", m5, eye).reshape(nb * hb, LRU_HEAD_DIM, LRU_HEAD_DIM)


def _tiles(T):
    cap = lambda n: min(n, T)
    return dict(inproj=cap(1024), mixer=cap(256), merge=cap(512), ffn=cap(1024), ffn_bwd=cap(1024), loss=cap(512), inproj_bwd=cap(1024),
                wgrad_in=cap(2048), wgrad=cap(2048))


class _NoSchedule:
    def carry(self, name):
        return None

    def after(self, name):
        pass

    def grad(self, key, layer, f32, b16):
        pass


def _local_step(x, target, W, small, tiles, sched):
    L = small["ln1_g"].shape[0]
    D = x.shape[1]
    square = lambda a: a.reshape(D, D)
    saved = []
    h = x
    for l in range(L):
        cw = jnp.concatenate([small["conv_a_w"][l], small["conv_b_w"][l], small["conv_b_b"][l][None],
                              small["lru_ba"][l][None], small["lru_bx"][l][None]], axis=0)
        wab = jnp.concatenate([_block_diag(small["lru_wa"][l]), _block_diag(small["lru_wx"][l])], axis=2).astype(BF16)
        wabt = jnp.swapaxes(wab, 1, 2)
        lam_row = small["lru_lambda"][l][None]
        ln1_row = small["ln1_g"][l][None]
        ln2_row = small["ln2_g"][l][None]
        p, h1 = _rms_inproj(h, ln1_row, W["win", l], l, tiles["inproj"], sched.carry(f"rms_inproj_{l}"))
        ya, yb, hseq = _mixer_fwd(p, cw, wab, lam_row, l, tiles["mixer"], sched.carry(f"mixer_fwd_{l}"))
        oa, ob, mg, x1 = _merge_fwd(h, p, ya, yb, square(W["woa", l]), square(W["wob", l]), square(W["wo", l]),
                                    small["gate_bias"][l], l, tiles["merge"], sched.carry(f"merge_fwd_{l}"))
        h2, gg, uu, x2 = _ffn_fwd(x1, ln2_row, W["wg", l], W["wu", l], W["wd", l], l, tiles["ffn"], sched.carry(f"ffn_fwd_{l}"))
        saved.append(dict(x0=h, p=p, h1=h1, ya=ya, yb=yb, hseq=hseq, oa=oa, ob=ob, mg=mg, x1=x1, h2=h2, gg=gg, uu=uu,
                          cw=cw, wab=wab, wabt=wabt, lam_row=lam_row, ln1_row=ln1_row, ln2_row=ln2_row))
        h = x2

    dx, red = _final_loss(h, small["final_g"][None], target, tiles["loss"])
    loss_row, d_final_g = red[0], red[1]

    gsmall = {k: [None] * L for k in ("ln1_g", "ln2_g", "conv_a_w", "conv_b_w", "conv_b_b", "lru_wa", "lru_ba", "lru_wx",
                                      "lru_bx", "lru_lambda", "gate_bias")}
    tk = tiles["wgrad"]
    for l in reversed(range(L)):
        s = saved[l]
        dgg, duu, act, dx2b = _ffn_bwd_gates(dx, s["gg"], s["uu"], W["wd", l], l, tiles["ffn_bwd"])
        dx1, dln2 = _ffn_bwd(dgg, duu, W["wg", l], W["wu", l], dx, s["x1"], s["ln2_row"], l, tiles["ffn_bwd"],
                             sched.carry(f"ffn_bwd_{l}"))
        sched.after(f"ffn_bwd_{l}")
        gate_up = _wgrad_pair(s["h2"], dgg, duu, f"wgrad_ffn_gate_up_{l}", tk, sched.carry(f"wgrad_ffn_gate_up_{l}"))
        sched.grad("wg", l, *gate_up[0:2])
        sched.grad("wu", l, *gate_up[2:4])
        sched.after(f"wgrad_ffn_gate_up_{l}")
        sched.grad("wd", l, *_wgrad(act, dx2b, f"wgrad_ffn_down_{l}", tk, "cm", "whole", N_CHIP, sched.carry(f"wgrad_ffn_down_{l}")))
        dya, dyb, doa, dob, dgl, dx1b, dgbias = _merge_bwd(dx1, s["p"], s["oa"], s["ob"], square(W["woa", l]), square(W["wob", l]),
                                                         square(W["wo", l]), small["gate_bias"][l], l, tiles["merge"],
                                                         sched.carry(f"merge_bwd_{l}"))
        sched.after(f"merge_bwd_{l}")
        sched.grad("wo", l, *_wgrad(s["mg"], dx1b, f"wgrad_w_o_{l}", tk))
        sched.grad("woa", l, *_wgrad(s["ya"], doa, f"wgrad_w_out_a_{l}", tk))
        sched.grad("wob", l, *_wgrad(s["yb"], dob, f"wgrad_w_out_b_{l}", tk))
        dp, mred, dwab = _mixer_bwd(s["p"], s["hseq"], dya, dyb, dgl, s["cw"], s["wab"], s["wabt"], s["lam_row"], l,
                                    tiles["mixer"], sched.carry(f"mixer_bwd_{l}"))
        sched.after(f"mixer_bwd_{l}")
        sched.grad("win", l, *_wgrad(s["h1"], dp, f"wgrad_w_in_{l}", tiles["wgrad_in"], "whole", "cols", N_CHIP,
                                     sched.carry(f"wgrad_w_in_{l}")))
        sched.after(f"wgrad_w_in_{l}")
        dx, dln1 = _inproj_bwd(dp, dx1, s["x0"], s["ln1_row"], W["win", l], l, tiles["inproj_bwd"], sched.carry(f"inproj_bwd_{l}"))
        sched.after(f"inproj_bwd_{l}")
        gsmall["ln1_g"][l] = dln1[0]
        gsmall["ln2_g"][l] = dln2[0]
        gsmall["conv_a_w"][l] = mred[0:CONV_A_K]
        gsmall["conv_b_w"][l] = mred[CONV_A_K:CONV_A_K + CONV_B_K]
        gsmall["conv_b_b"][l] = mred[7]
        gsmall["lru_ba"][l] = mred[8]
        gsmall["lru_bx"][l] = mred[9]
        gsmall["lru_lambda"][l] = mred[10]
        gsmall["lru_wa"][l] = _diag_heads(dwab[:, :, :LRU_BLOCK])
        gsmall["lru_wx"][l] = _diag_heads(dwab[:, :, LRU_BLOCK:])
        gsmall["gate_bias"][l] = dgbias
    gsmall = {k: jnp.stack(v) for k, v in gsmall.items()}
    gsmall["final_g"] = d_final_g
    return loss_row, dx, gsmall


def _small_allreduce(buf):
    R, C = buf.shape
    n_dev = 8
    rp = R // n_dev
    rel = [(k >> 2 & 1, k >> 1 & 1, k & 1) for k in range(1, n_dev)]

    def body(in_ref, out_ref, recv, s1, r1, s2, r2):
        x, y, c, _ = _place()
        flip = lambda v, bit: 1 - v if bit else v
        peers = [(flip(x, kx), flip(y, ky), flip(c, kc)) for kx, ky, kc in rel]
        dev = lambda p: 4 * p[0] + 2 * p[1] + p[2]
        part = lambda ref, d: ref.at[pl.ds(pl.multiple_of(d * rp, SUBLANES), rp), :]
        me = dev((x, y, c))

        def scatter(k, src_dev, to):
            return pltpu.make_async_remote_copy(src_ref=part(in_ref, dev(to)), dst_ref=recv.at[src_dev], send_sem=s1.at[k],
                                                recv_sem=r1.at[k], device_id=to, device_id_type=MESH)

        def gather(k, src_dev, to):
            return pltpu.make_async_remote_copy(src_ref=part(out_ref, src_dev), dst_ref=part(out_ref, src_dev), send_sem=s2.at[k],
                                                recv_sem=r2.at[k], device_id=to, device_id_type=MESH)

        first = [scatter(k, me, p) for k, p in enumerate(peers)]
        for cp in first:
            cp.start()
        recv[me] = part(in_ref, me)[...]
        for k, p in enumerate(peers):
            scatter(k, dev(p), (x, y, c)).wait_recv()
        total = recv[0]
        for d in range(1, n_dev):
            total = total + recv[d]
        part(out_ref, me)[...] = total
        second = [gather(k, me, p) for k, p in enumerate(peers)]
        for cp in second:
            cp.start()
        for k, p in enumerate(peers):
            gather(k, dev(p), (x, y, c)).wait_recv()
        for cp in first + second:
            cp.wait_send()

    dma = pltpu.SemaphoreType.DMA
    vm = pl.BlockSpec(memory_space=pltpu.VMEM)
    return pl.pallas_call(
        body, name="small_allreduce", out_shape=jax.ShapeDtypeStruct((R, C), buf.dtype),
        in_specs=[vm], out_specs=vm,
        scratch_shapes=[pltpu.VMEM((n_dev, rp, C), buf.dtype), dma((n_dev - 1,)), dma((n_dev - 1,)), dma((n_dev - 1,)), dma((n_dev - 1,))],
    )(buf)


ELEMENTWISE_BLOCK_BYTES = 2 * 1024 * 1024


def _row_block(k, n):
    best = None
    for b in range(16, k + 1, 16):
        if k % b == 0 and b * n * 4 <= ELEMENTWISE_BLOCK_BYTES:
            best = b
    return best or k


def _add_halves(g, recv, place_arr, name):
    nj, hk, N = recv.shape
    bk = _row_block(hk, N)
    nb = hk // bk

    def body(k_ref, g_ref, r_ref, o_ref, ob_ref):
        s = g_ref[...] + r_ref[...].astype(F32)
        ob_ref[...] = s.astype(BF16)

        @pl.when(pl.program_id(1) == k_ref[0])
        def _():
            o_ref[...] = s

    blk = pl.BlockSpec((None, bk, N), lambda i, j, k_ref: (j, i, 0))
    grid_spec = pltpu.PrefetchScalarGridSpec(
        num_scalar_prefetch=1, grid=(nb, nj),
        in_specs=[pl.BlockSpec((None, bk, N), lambda i, j, k_ref: (j, k_ref[1] * nb + i, 0)), blk],
        out_specs=[pl.BlockSpec((bk, N), lambda i, j, k_ref: (i, 0)), blk])
    return pl.pallas_call(
        body, name=name, grid_spec=grid_spec,
        out_shape=[jax.ShapeDtypeStruct((hk, N), F32), jax.ShapeDtypeStruct((nj, hk, N), BF16)],
        compiler_params=_params("parallel", "arbitrary"),
    )(place_arr, g, recv)


def _add_chips(pc, recv, place_arr, layer, n_layers, prev, name):
    hk, N = pc.shape
    bk = _row_block(hk, N)
    nb = hk // bk

    def body(k_ref, p_ref, r0_ref, r1_ref, r2_ref, *rest):
        o_ref = rest[-1]
        o_ref[...] = ((p_ref[...] + r0_ref[...].astype(F32)) + r1_ref[...].astype(F32)) + r2_ref[...].astype(F32)

    rspec = lambda j: pl.BlockSpec((None, bk, N), lambda i, k_ref: (j, i, 0))
    in_specs = [pl.BlockSpec((bk, N), lambda i, k_ref: (i, 0)), rspec(0), rspec(1), rspec(2)]
    operands = [pc, recv, recv, recv]
    aliases = {}
    if prev is not None:
        in_specs.append(ANY)
        operands.append(prev)
        aliases = {5: 0}
    grid_spec = pltpu.PrefetchScalarGridSpec(
        num_scalar_prefetch=1, grid=(nb,), in_specs=in_specs,
        out_specs=pl.BlockSpec((None, bk, N), lambda i, k_ref: (layer, k_ref[1] * nb + i, 0)))
    return pl.pallas_call(
        body, name=name, grid_spec=grid_spec, out_shape=jax.ShapeDtypeStruct((n_layers, 2 * hk, N), F32),
        input_output_aliases=aliases, compiler_params=_params("parallel"),
    )(place_arr, *operands)


def _adamw_math(w, g, m, v):
    m = ADAM_B1 * m + (1.0 - ADAM_B1) * g
    v = ADAM_B2 * v + (1.0 - ADAM_B2) * (g * g)
    m_hat = m / (1.0 - ADAM_B1 ** ADAM_STEP)
    v_hat = v / (1.0 - ADAM_B2 ** ADAM_STEP)
    delta = -ADAM_LR * (m_hat / (jnp.sqrt(v_hat) + ADAM_EPS) + ADAM_WD * w)
    return delta, m, v


def _adamw(w, g, m, v, name):
    L, K, N = w.shape
    bk = _row_block(K, N)

    def body(w_ref, g_ref, m_ref, v_ref, d_ref, nm_ref, nv_ref, go_ref):
        g = g_ref[...]
        d_ref[...], nm_ref[...], nv_ref[...] = _adamw_math(w_ref[...], g, m_ref[...], v_ref[...])
        go_ref[...] = g

    blk = pl.BlockSpec((None, bk, N), lambda l, i: (l, i, 0))
    sds = jax.ShapeDtypeStruct((L, K, N), F32)
    return pl.pallas_call(
        body, name=name, grid=(L, K // bk), in_specs=[blk] * 4, out_specs=[blk] * 4, out_shape=[sds] * 4,
        compiler_params=_params("parallel", "parallel"),
    )(w, g, m, v)


def _adamw_small(ws, gs, ms, vs):
    n = len(ws)

    def body(*refs):
        w, g, m, v, d, nm, nv = (refs[k * n:(k + 1) * n] for k in range(7))
        for k in range(n):
            d[k][...], nm[k][...], nv[k][...] = _adamw_math(w[k][...], g[k][...], m[k][...], v[k][...])

    sds = [jax.ShapeDtypeStruct(a.shape, F32) for a in ws]
    out = pl.pallas_call(body, name="adamw_small", out_shape=sds * 3)(*ws, *gs, *ms, *vs)
    return out[:n], out[n:2 * n], out[2 * n:]


def _cast_bf16(w, layer, place_arr, name):
    _, K, N = w.shape
    bk = _row_block(K, N)

    def body(k_ref, w_ref, o_ref):
        o_ref[...] = w_ref[...].astype(BF16)

    grid_spec = pltpu.PrefetchScalarGridSpec(
        num_scalar_prefetch=1, grid=(K // bk,),
        in_specs=[pl.BlockSpec((None, bk, N), lambda i, k_ref: (layer, i, 0))],
        out_specs=pl.BlockSpec((None, bk, N), lambda i, k_ref: (k_ref[0], i, 0)))
    return pl.pallas_call(
        body, name=name, grid_spec=grid_spec, out_shape=jax.ShapeDtypeStruct((N_CHIP, K, N), BF16),
        compiler_params=_params("parallel"),
    )(place_arr, w)


BIG = ("w_in", "w_out_a", "w_out_b", "w_o", "w_ffn_gate", "w_ffn_up", "w_ffn_down")
BIG_KEY = dict(w_in="win", w_out_a="woa", w_out_b="wob", w_o="wo", w_ffn_gate="wg", w_ffn_up="wu", w_ffn_down="wd")
SHARDED_SMALL = ("conv_a_w", "conv_b_w", "gate_bias")
REPLICATED = ("ln1_g", "conv_b_b", "lru_wa", "lru_ba", "lru_wx", "lru_bx", "lru_lambda", "ln2_g", "final_g")
WEIGHTS = ("ln1_g", "w_in", "conv_a_w", "conv_b_w", "conv_b_b", "lru_wa", "lru_ba", "lru_wx", "lru_bx", "lru_lambda",
           "w_out_a", "w_out_b", "gate_bias", "w_o", "ln2_g", "w_ffn_gate", "w_ffn_up", "w_ffn_down", "final_g")
LANES = 1024


def _pack_rows(arrays, row_multiple):
    flat = jnp.concatenate([a.reshape(-1) for a in arrays])
    rows = -(-flat.shape[0] // LANES)
    rows = -(-rows // row_multiple) * row_multiple
    flat = jnp.pad(flat, (0, rows * LANES - flat.shape[0]))
    return flat.reshape(rows, LANES)


def _unpack_rows(buf, shapes):
    flat = buf.reshape(-1)
    out, off = [], 0
    for s in shapes:
        n = 1
        for d in s:
            n *= d
        out.append(flat[off:off + n].reshape(s))
        off += n
    return out


OUT_KEYS = ("wo", "woa", "wob")
FFN_KEYS = ("wg", "wu", "wd")


def _items(keys, layer):
    return [(k, layer) for k in keys]


CARRY = {
    "rms_inproj_0": [("gather_ici", _items(OUT_KEYS + FFN_KEYS, 0))],
    "mixer_fwd_0": [("gather_d2d", _items(OUT_KEYS + FFN_KEYS, 0)), ("gather_ici", _items(("win",) + OUT_KEYS, 1))],
    "merge_fwd_0": [("gather_d2d", _items(("win",) + OUT_KEYS, 1))],
    "ffn_fwd_0": [("gather_ici", _items(FFN_KEYS, 1))],
    "rms_inproj_1": [("gather_d2d", _items(FFN_KEYS, 1))],
    "merge_bwd_1": [("halves", _items(FFN_KEYS, 1))],
    "mixer_bwd_1": [("chips", _items(FFN_KEYS, 1)), ("halves", _items(OUT_KEYS, 1))],
    "inproj_bwd_1": [("chips", _items(OUT_KEYS, 1)), ("share", _items(FFN_KEYS, 1))],
    "ffn_bwd_0": [("halves", [("win", 1)]), ("share", _items(OUT_KEYS, 1))],
    "wgrad_ffn_gate_up_0": [("chips", [("win", 1)])],
    "wgrad_ffn_down_0": [("share", [("win", 1)])],
    "merge_bwd_0": [("halves", _items(FFN_KEYS, 0))],
    "mixer_bwd_0": [("chips", _items(FFN_KEYS, 0)), ("halves", _items(OUT_KEYS, 0))],
    "wgrad_w_in_0": [("chips", _items(OUT_KEYS, 0)), ("share", _items(FFN_KEYS, 0))],
    "inproj_bwd_0": [("chips", [("win", 0)])],
}
AFTER = {
    "merge_bwd_1": [("add_halves", _items(FFN_KEYS, 1))],
    "mixer_bwd_1": [("add_chips", _items(FFN_KEYS, 1)), ("add_halves", _items(OUT_KEYS, 1))],
    "inproj_bwd_1": [("add_chips", _items(OUT_KEYS, 1))],
    "ffn_bwd_0": [("add_halves", [("win", 1)])],
    "wgrad_ffn_gate_up_0": [("add_chips", [("win", 1)])],
    "merge_bwd_0": [("add_halves", _items(FFN_KEYS, 0))],
    "mixer_bwd_0": [("add_chips", _items(FFN_KEYS, 0)), ("add_halves", _items(OUT_KEYS, 0))],
    "wgrad_w_in_0": [("add_chips", _items(OUT_KEYS, 0)),
                     ("run", ("reduce_halves_w_in_0", [("halves", [("win", 0)]), ("share", _items(OUT_KEYS, 0))])),
                     ("add_halves", [("win", 0)])],
    "inproj_bwd_0": [("add_chips", [("win", 0)]), ("run", ("reduce_share_w_in_0", [("share", [("win", 0)])]))],
}


class _Schedule:
    def __init__(self, slots, place_arr, n_layers):
        self.W = slots
        self.place, self.L = place_arr, n_layers
        self.g32, self.g16 = {}, {}
        self.from_sibling, self.chip_sum, self.chip_sum16, self.from_chips = {}, {}, {}, {}
        self.reduced = {}

    def stage(self, comm, kind, items):
        bf = lambda shape: jax.ShapeDtypeStruct(shape, BF16)
        for it in items:
            if kind == "gather_ici":
                comm.add(_gather_ici, 3, io=[(self.W, it)])
            elif kind == "gather_d2d":
                comm.add(_gather_d2d, 3, io=[(self.W, it)])
            elif kind == "halves":
                nj, K, N = self.g16[it].shape
                comm.add(_reduce_halves, 1, ro=[self.g16[it]], nw=[(self.from_sibling, it, bf((nj, K // 2, N)))])
            elif kind == "chips":
                _, hk, N = self.chip_sum16[it].shape
                comm.add(_reduce_chips, 3, ro=[self.chip_sum16[it]], nw=[(self.from_chips, it, bf((3, hk, N)))])
            elif kind == "share":
                comm.add(_reduce_share(it[1]), 1, io=[(self.reduced, it[0])])
        return comm

    def carry(self, name):
        comm = _Carried()
        for kind, items in CARRY.get(name, ()):
            self.stage(comm, kind, items)
        return comm

    def run(self, name, rounds):
        _run_comm([self.stage(_Carried(), kind, items) for kind, items in rounds], name)

    def grad(self, key, layer, f32, b16):
        by_chip = lambda g: g.reshape(N_CHIP, -1, g.shape[-1])
        self.g32[key, layer], self.g16[key, layer] = by_chip(f32), by_chip(b16)

    def add(self, kind, items):
        for key, layer in items:
            it = (key, layer)
            if kind == "add_halves":
                self.chip_sum[it], self.chip_sum16[it] = _add_halves(self.g32[it], self.from_sibling[it], self.place,
                                                                    f"add_halves_{key}_{layer}")
            else:
                self.reduced[key] = _add_chips(self.chip_sum[it], self.from_chips[it], self.place, layer, self.L,
                                               self.reduced.get(key), f"add_chips_{key}_{layer}")

    def after(self, name):
        for kind, items in AFTER.get(name, ()):
            if kind == "run":
                self.run(*items)
            else:
                self.add(kind, items)


def _step(w, m, v, x, target):
    xi, yi, ci = lax.axis_index("x"), lax.axis_index("y"), lax.axis_index("c")
    chip = _chip_id(xi, yi)
    place_arr = jnp.stack([chip, ci]).astype(jnp.int32)
    L = w["ln1_g"].shape[0]
    assert L == 2
    D = x.shape[1]
    dc = D // N_CHIP

    stored = lambda n, a: jnp.swapaxes(a, 1, 2) if n in ("w_ffn_gate", "w_ffn_up") else a
    slots = {("win", 0): _cast_bf16(w["w_in"], 0, place_arr, "cast_w_in_0")}
    send_sems, recv_sems, slots["win", 0], token = _gather_first_start(slots["win", 0])
    place_after = place_arr + token[0, 0]
    for n in BIG:
        for l in range(L):
            if (BIG_KEY[n], l) not in slots:
                slots[BIG_KEY[n], l] = _cast_bf16(stored(n, w[n]), l, place_after, f"cast_{n}_{l}")
    slots["win", 0] = _gather_first_wait(send_sems, recv_sems, slots["win", 0], [v for k, v in slots.items() if k != ("win", 0)])
    sched = _Schedule(slots, place_arr, L)
    small_shard = jnp.concatenate([w[n] for n in SHARDED_SMALL], axis=1)
    got = {}
    second = sched.stage(_Carried(), "gather_d2d", [("win", 0)])
    second.add(_gather_small, 3, ro=[small_shard], nw=[(got, "small", jax.ShapeDtypeStruct((3,) + small_shard.shape, F32))])
    _run_comm([second], "gather_first_d2d")
    small_g = jnp.zeros((N_CHIP,) + small_shard.shape, F32)
    small_g = lax.dynamic_update_index_in_dim(small_g, small_shard, chip, 0)
    for j, (cx, cy) in enumerate([(1 - xi, yi), (xi, 1 - yi), (1 - xi, 1 - yi)]):
        small_g = lax.dynamic_update_index_in_dim(small_g, got["small"][j], _chip_id(cx, cy), 0)
    small_full = jnp.transpose(small_g, (1, 2, 0, 3)).reshape(L, small_shard.shape[1], D)
    small = {n: w[n] for n in REPLICATED}
    off = 0
    for n in SHARDED_SMALL:
        k = w[n].shape[1]
        small[n] = small_full[:, off:off + k]
        off += k

    loss_row, grad_x, gsmall = _local_step(x, target, sched.W, small, _tiles(x.shape[0]), sched)

    grads = {}

    order = [n for n in WEIGHTS if n not in BIG]
    packed = _pack_rows([gsmall[n] for n in order] + [loss_row], 8 * SUBLANES)
    summed = _small_allreduce(packed)
    parts = _unpack_rows(summed, [gsmall[n].shape for n in order] + [loss_row.shape])
    loss = jnp.sum(parts[-1])
    for n, g in zip(order, parts[:-1]):
        grads[n] = lax.dynamic_slice_in_dim(g, chip * dc, dc, axis=2) if n in SHARDED_SMALL else g

    delta, new_m, new_v = {}, {}, {}
    for n in BIG:
        d, nm, nv, g = _adamw(stored(n, w[n]), sched.reduced[BIG_KEY[n]], stored(n, m[n]), stored(n, v[n]), f"adamw_{n}")
        delta[n], new_m[n], new_v[n], grads[n] = stored(n, d), stored(n, nm), stored(n, nv), stored(n, g)
    for d, arrays in zip((delta, new_m, new_v), _adamw_small(*([d[n] for n in order] for d in (w, grads, m, v)))):
        d.update(zip(order, arrays))
    return loss, grad_x, grads, delta, new_m, new_v


def kernel(x, ln1_g, w_in, conv_a_w, conv_b_w, conv_b_b, lru_wa, lru_ba, lru_wx, lru_bx, lru_lambda, w_out_a, w_out_b, gate_bias, w_o, ln2_g, w_ffn_gate, w_ffn_up, w_ffn_down, final_g, loss_target, m_ln1_g, m_w_in, m_conv_a_w, m_conv_b_w, m_conv_b_b, m_lru_wa, m_lru_ba, m_lru_wx, m_lru_bx, m_lru_lambda, m_w_out_a, m_w_out_b, m_gate_bias, m_w_o, m_ln2_g, m_w_ffn_gate, m_w_ffn_up, m_w_ffn_down, m_final_g, v_ln1_g, v_w_in, v_conv_a_w, v_conv_b_w, v_conv_b_b, v_lru_wa, v_lru_ba, v_lru_wx, v_lru_bx, v_lru_lambda, v_w_out_a, v_w_out_b, v_gate_bias, v_w_o, v_ln2_g, v_w_ffn_gate, v_w_ffn_up, v_w_ffn_down, v_final_g):
    w = dict(ln1_g=ln1_g, w_in=w_in, conv_a_w=conv_a_w, conv_b_w=conv_b_w, conv_b_b=conv_b_b, lru_wa=lru_wa, lru_ba=lru_ba,
             lru_wx=lru_wx, lru_bx=lru_bx, lru_lambda=lru_lambda, w_out_a=w_out_a, w_out_b=w_out_b, gate_bias=gate_bias, w_o=w_o,
             ln2_g=ln2_g, w_ffn_gate=w_ffn_gate, w_ffn_up=w_ffn_up, w_ffn_down=w_ffn_down, final_g=final_g)
    m = dict(ln1_g=m_ln1_g, w_in=m_w_in, conv_a_w=m_conv_a_w, conv_b_w=m_conv_b_w, conv_b_b=m_conv_b_b, lru_wa=m_lru_wa,
             lru_ba=m_lru_ba, lru_wx=m_lru_wx, lru_bx=m_lru_bx, lru_lambda=m_lru_lambda, w_out_a=m_w_out_a, w_out_b=m_w_out_b,
             gate_bias=m_gate_bias, w_o=m_w_o, ln2_g=m_ln2_g, w_ffn_gate=m_w_ffn_gate, w_ffn_up=m_w_ffn_up,
             w_ffn_down=m_w_ffn_down, final_g=m_final_g)
    v = dict(ln1_g=v_ln1_g, w_in=v_w_in, conv_a_w=v_conv_a_w, conv_b_w=v_conv_b_w, conv_b_b=v_conv_b_b, lru_wa=v_lru_wa,
             lru_ba=v_lru_ba, lru_wx=v_lru_wx, lru_bx=v_lru_bx, lru_lambda=v_lru_lambda, w_out_a=v_w_out_a, w_out_b=v_w_out_b,
             gate_bias=v_gate_bias, w_o=v_w_o, ln2_g=v_ln2_g, w_ffn_gate=v_w_ffn_gate, w_ffn_up=v_w_ffn_up,
             w_ffn_down=v_w_ffn_down, final_g=v_final_g)
    loss, grad_x, grads, delta, new_m, new_v = _step(w, m, v, x[0], loss_target[0])
    return (loss, grad_x[None], *[grads[n] for n in WEIGHTS], *[delta[n] for n in WEIGHTS],
            *[new_m[n] for n in WEIGHTS], *[new_v[n] for n in WEIGHTS])
```

```python
import jax
import jax.numpy as jnp
from jax import lax
from jax.experimental import pallas as pl
from jax.experimental.pallas import tpu as pltpu

F32 = jnp.float32
BF16 = jnp.bfloat16
MESH = pl.DeviceIdType.MESH

N_CHIP = 4
RMS_EPS = 1e-6
LRU_C = 8.0
LRU_HEAD_DIM = 64
LRU_BLOCK = 256
CONV_A_K = 3
CONV_B_K = 4
ADAM_LR = 0.001
ADAM_B1 = 0.9
ADAM_B2 = 0.999
ADAM_EPS = 1e-08
ADAM_WD = 0.01
ADAM_STEP = 10
SUBLANES = 8
VMEM_LIMIT = 56 * 1024 * 1024


def _params(*sem):
    return pltpu.CompilerParams(dimension_semantics=sem, vmem_limit_bytes=VMEM_LIMIT)


def _sigmoid(v):
    return 1.0 / (1.0 + jnp.exp(-v))


def _one_minus_sq(la, a):
    return jnp.tanh(-la) * (1.0 + a * a)


def _gelu_parts(v):
    k = 0.7978845608028654
    v2 = v * v
    t = jnp.tanh(k * (v + 0.044715 * v * v2))
    gelu = 0.5 * v * (1.0 + t)
    dgelu = 0.5 * (1.0 + t) + 0.5 * v * (1.0 - t * t) * k * (1.0 + 3 * 0.044715 * v2)
    return gelu, dgelu


def _shift_down(v, k, prev8):
    rolled = pltpu.roll(v, k, 0)
    r8 = lax.broadcasted_iota(jnp.int32, prev8.shape, 0)
    head = jnp.where(r8 < k, pltpu.roll(prev8, k, 0), rolled[0:SUBLANES])
    return jnp.concatenate([head, rolled[SUBLANES:]], axis=0)


def _shift_up(v, k, next8):
    tm = v.shape[0]
    rolled = pltpu.roll(v, tm - k, 0)
    r8 = lax.broadcasted_iota(jnp.int32, next8.shape, 0)
    tail = jnp.where(r8 >= SUBLANES - k, pltpu.roll(next8, SUBLANES - k, 0), rolled[tm - SUBLANES:])
    return jnp.concatenate([rolled[:tm - SUBLANES], tail], axis=0)


def _group_scan(a, b, reverse):
    tm, c = a.shape
    a = a.reshape(tm // SUBLANES, SUBLANES, c)
    b = b.reshape(tm // SUBLANES, SUBLANES, c)
    q = lax.broadcasted_iota(jnp.int32, a.shape, 1)
    for s in (1, 2, 4):
        msk = q < SUBLANES - s if reverse else q >= s
        shift = SUBLANES - s if reverse else s
        b = jnp.where(msk, a * pltpu.roll(b, shift, 1) + b, b)
        a = jnp.where(msk, a * pltpu.roll(a, shift, 1), a)
    return a.reshape(tm, c), b.reshape(tm, c)


def _colsum8(v):
    tm, c = v.shape
    return jnp.sum(v.reshape(tm // SUBLANES, SUBLANES, c), axis=0)


def _rms_stats(xv):
    var = jnp.mean(xv * xv, axis=-1, keepdims=True)
    return lax.rsqrt(var + RMS_EPS)


def _rms_bwd(dh, xv, g):
    rstd = _rms_stats(xv)
    xhat = xv * rstd
    dxhat = dh * g
    dx = rstd * (dxhat - xhat * jnp.mean(dxhat * xhat, axis=-1, keepdims=True))
    return dx, dh * xhat


ANY = pl.BlockSpec(memory_space=pl.ANY)


def _place():
    x, y, c = lax.axis_index("x"), lax.axis_index("y"), lax.axis_index("c")
    other_chips = [(1 - x, y), (x, 1 - y), (1 - x, 1 - y)]
    return x, y, c, other_chips


def _chip_id(x, y):
    return 2 * x + y


def _half(c, hk):
    return pl.ds(pl.multiple_of(c * hk, 16), hk)


def _remote(src, dst, to, sems):
    return pltpu.make_async_remote_copy(src_ref=src, dst_ref=dst, device_id=to, device_id_type=MESH, **sems)


class _Carried:
    def __init__(self):
        self.ro, self.io, self.nw, self.parts, self.n = [], [], [], [], 0

    def add(self, maker, n, ro=(), io=(), nw=()):
        def index(items, item, same):
            for k, other in enumerate(items):
                if same(other, item):
                    return k
            items.append(item)
            return len(items) - 1

        r = [index(self.ro, a, lambda p, q: p is q) for a in ro]
        i = [index(self.io, a, lambda p, q: p[0] is q[0] and p[1] == q[1]) for a in io]
        w = [index(self.nw, a, lambda p, q: False) for a in nw]
        self.parts.append((maker, r, i, w, self.n))
        self.n += n
        return self

    def pairs(self, ro, io, nw, ssem, rsem):
        out = []
        for maker, r, i, w, base in self.parts:
            sems = lambda k, base=base: dict(send_sem=ssem.at[base + k], recv_sem=rsem.at[base + k])
            out += maker([ro[k] for k in r], [io[k] for k in i], [nw[k] for k in w], sems)
        return out

    def start(self, *refs):
        for send, _ in self.pairs(*refs):
            send.start()

    def finish(self, *refs):
        pairs = self.pairs(*refs)
        for _, recv in pairs:
            recv.wait_recv()
        for send, _ in pairs:
            send.wait_send()

    def operands(self):
        return list(self.ro) + [store[key] for store, key in self.io]

    def out_shapes(self):
        return [jax.ShapeDtypeStruct(store[key].shape, store[key].dtype) for store, key in self.io] + [s for _, _, s in self.nw]

    def keep(self, results):
        for (store, key), arr in zip(self.io, results[:len(self.io)]):
            store[key] = arr
        for (store, key, _), arr in zip(self.nw, results[len(self.io):]):
            store[key] = arr


def _call(body, comm, *, name, grid, in_specs, out_specs, out_shape, compiler_params, scratch_shapes=(), aliases=None):
    aliases = dict(aliases or {})
    if comm is None or not comm.parts:
        return pl.pallas_call(body, name=name, grid=grid, in_specs=in_specs, out_specs=out_specs, out_shape=out_shape,
                              scratch_shapes=list(scratch_shapes), input_output_aliases=aliases, compiler_params=compiler_params)
    n_in, n_out, n_scr = len(in_specs), len(out_shape), len(scratch_shapes)
    n_ro, n_io, n_nw = len(comm.ro), len(comm.io), len(comm.nw)

    def carried(*refs):
        base_in = refs[:n_in]
        ro = refs[n_in:n_in + n_ro]
        pos = n_in + n_ro + n_io
        base_out = refs[pos:pos + n_out]
        io = refs[pos + n_out:pos + n_out + n_io]
        nw = refs[pos + n_out + n_io:pos + n_out + n_io + n_nw]
        pos += n_out + n_io + n_nw
        scr = refs[pos:pos + n_scr]
        ssem, rsem = refs[pos + n_scr], refs[pos + n_scr + 1]
        first = pl.program_id(0) == 0
        last = pl.program_id(0) == grid[0] - 1
        for axis in range(1, len(grid)):
            first = first & (pl.program_id(axis) == 0)
            last = last & (pl.program_id(axis) == grid[axis] - 1)

        @pl.when(first)
        def _():
            comm.start(ro, io, nw, ssem, rsem)

        body(*base_in, *base_out, *scr)

        @pl.when(last)
        def _():
            comm.finish(ro, io, nw, ssem, rsem)

    aliases.update({n_in + n_ro + k: n_out + k for k in range(n_io)})
    dma = pltpu.SemaphoreType.DMA
    call = pl.pallas_call(
        carried, name=name, grid=grid,
        in_specs=list(in_specs) + [ANY] * (n_ro + n_io), out_specs=list(out_specs) + [ANY] * (n_io + n_nw),
        out_shape=list(out_shape) + comm.out_shapes(), input_output_aliases=aliases,
        scratch_shapes=list(scratch_shapes) + [dma((comm.n,)), dma((comm.n,))], compiler_params=compiler_params)

    def run(*operands):
        res = call(*operands, *comm.operands())
        comm.keep(res[n_out:])
        return res[:n_out]

    return run


def _run_comm(rounds, name):
    ro, io, nw, uses = [], [], [], []
    for r in rounds:
        def index(items, item, same):
            for k, other in enumerate(items):
                if same(other, item):
                    return k
            items.append(item)
            return len(items) - 1
        uses.append(([index(ro, a, lambda p, q: p is q) for a in r.ro],
                     [index(io, a, lambda p, q: p[0] is q[0] and p[1] == q[1]) for a in r.io],
                     [index(nw, a, lambda p, q: False) for a in r.nw]))
    n_ro, n_io, n_nw = len(ro), len(io), len(nw)

    def body(*refs):
        ro_refs = refs[:n_ro]
        io_refs = refs[n_ro + n_io:n_ro + 2 * n_io]
        nw_refs = refs[n_ro + 2 * n_io:n_ro + 2 * n_io + n_nw]
        sems = refs[n_ro + 2 * n_io + n_nw:]
        for k, (r, (a, b, c)) in enumerate(zip(rounds, uses)):
            args = ([ro_refs[i] for i in a], [io_refs[i] for i in b], [nw_refs[i] for i in c], sems[2 * k], sems[2 * k + 1])
            r.start(*args)
            r.finish(*args)

    operands = ro + [store[key] for store, key in io]
    out_shape = [jax.ShapeDtypeStruct(store[key].shape, store[key].dtype) for store, key in io] + [s for _, _, s in nw]
    dma = pltpu.SemaphoreType.DMA
    res = pl.pallas_call(
        body, name=name, out_shape=out_shape,
        in_specs=[ANY] * (n_ro + n_io), out_specs=[ANY] * (n_io + n_nw),
        input_output_aliases={n_ro + k: k for k in range(n_io)},
        scratch_shapes=[dma((r.n,)) for r in rounds for _ in range(2)],
    )(*operands)
    for (store, key), arr in zip(io, res[:n_io]):
        store[key] = arr
    for (store, key, _), arr in zip(nw, res[n_io:]):
        store[key] = arr


HBM = pl.BlockSpec(memory_space=pltpu.HBM)
SEM = pl.BlockSpec(memory_space=pltpu.SEMAPHORE)
SPLIT_COPY = pltpu.CompilerParams(has_side_effects=pltpu.SideEffectType.DATAFLOW_SIDE_EFFECTING)


def _first_gather_copies(slot, send_sems, recv_sems):
    x, y, c, chips = _place()
    hk = slot.shape[1] // 2
    mine = slot.at[_chip_id(x, y), _half(c, hk)]
    pairs = []
    for j, (cx, cy) in enumerate(chips):
        theirs = slot.at[_chip_id(cx, cy), _half(c, hk)]
        sems = dict(send_sem=send_sems.at[j], recv_sem=recv_sems.at[j])
        pairs.append((_remote(mine, mine, (cx, cy, c), sems), _remote(theirs, theirs, (cx, cy, c), sems)))
    return pairs


def _gather_first_start(slot):
    def body(slot_ref, send_sems, recv_sems, slot_thru, token):
        for send, _ in _first_gather_copies(slot_ref, send_sems, recv_sems):
            send.start()
        token[...] = jnp.zeros_like(token)

    dma = pltpu.SemaphoreType.DMA
    return pl.pallas_call(
        body, name="gather_first_start",
        out_shape=(dma((3,)), dma((3,)), pltpu.HBM(slot.shape, slot.dtype), jax.ShapeDtypeStruct((SUBLANES, 128), jnp.int32)),
        in_specs=(HBM,), out_specs=(SEM, SEM, HBM, pl.BlockSpec(memory_space=pltpu.VMEM)), input_output_aliases={0: 2},
        compiler_params=SPLIT_COPY,
    )(pltpu.with_memory_space_constraint(slot, pltpu.HBM))


def _gather_first_wait(send_sems, recv_sems, slot, after):
    def body(slot_ref, send_sems, recv_sems, *rest):
        for send, recv in _first_gather_copies(slot_ref, send_sems, recv_sems):
            send.wait_send()
            recv.wait_recv()

    return pl.pallas_call(
        body, name="gather_first_wait", out_shape=(pltpu.HBM(slot.shape, slot.dtype),),
        in_specs=(HBM, SEM, SEM) + (ANY,) * len(after), out_specs=(HBM,), input_output_aliases={0: 0},
        compiler_params=SPLIT_COPY,
    )(slot, send_sems, recv_sems, *after)[0]


def _scatter_copies(src, land, send_sems, recv_sems):
    x, y, c, chips = _place()
    return [_remote(src.at[_chip_id(cx, cy)], land.at[j], (cx, cy, c), dict(send_sem=send_sems.at[j], recv_sem=recv_sems.at[j]))
            for j, (cx, cy) in enumerate(chips)]


def _scatter_start(src):
    land = pltpu.with_memory_space_constraint(lax.empty((3,) + src.shape[1:], src.dtype), pltpu.HBM)

    def body(src_ref, land_ref, send_sems, recv_sems, src_thru, land_thru, token):
        for cp in _scatter_copies(src_ref, land_ref, send_sems, recv_sems):
            cp.start()
        token[...] = jnp.zeros_like(token)

    dma = pltpu.SemaphoreType.DMA
    return pl.pallas_call(
        body, name="reduce_chips_start",
        out_shape=(dma((3,)), dma((3,)), pltpu.HBM(src.shape, src.dtype), pltpu.HBM(land.shape, land.dtype),
                   jax.ShapeDtypeStruct((SUBLANES, 128), F32)),
        in_specs=(HBM, HBM), out_specs=(SEM, SEM, HBM, HBM, pl.BlockSpec(memory_space=pltpu.VMEM)),
        input_output_aliases={0: 2, 1: 3}, compiler_params=SPLIT_COPY,
    )(pltpu.with_memory_space_constraint(src, pltpu.HBM), land)


def _scatter_wait(send_sems, recv_sems, src, land, after):
    def body(src_ref, land_ref, send_sems, recv_sems, after_ref, src_done, land_done):
        for cp in _scatter_copies(src_ref, land_ref, send_sems, recv_sems):
            cp.wait_send()
            cp.wait_recv()

    return pl.pallas_call(
        body, name="reduce_chips_wait", out_shape=(pltpu.HBM(src.shape, src.dtype), pltpu.HBM(land.shape, land.dtype)),
        in_specs=(HBM, HBM, SEM, SEM, ANY), out_specs=(HBM, HBM), input_output_aliases={0: 0, 1: 1}, compiler_params=SPLIT_COPY,
    )(src, land, send_sems, recv_sems, after)[1]


def _gather_ici(ro, io, nw, sems):
    s = io[0]
    x, y, c, chips = _place()
    hk = s.shape[1] // 2
    mine = s.at[_chip_id(x, y), _half(c, hk)]
    pairs = []
    for j, (cx, cy) in enumerate(chips):
        theirs = s.at[_chip_id(cx, cy), _half(c, hk)]
        pairs.append((_remote(mine, mine, (cx, cy, c), sems(j)), _remote(theirs, theirs, (cx, cy, c), sems(j))))
    return pairs


def _gather_d2d(ro, io, nw, sems):
    s = io[0]
    x, y, c, chips = _place()
    hk = s.shape[1] // 2
    sib = (x, y, 1 - c)
    pairs = []
    for j, (cx, cy) in enumerate(chips):
        here = s.at[_chip_id(cx, cy), _half(c, hk)]
        there = s.at[_chip_id(cx, cy), _half(1 - c, hk)]
        pairs.append((_remote(here, here, sib, sems(j)), _remote(there, there, sib, sems(j))))
    return pairs


def _gather_small(ro, io, nw, sems):
    x, y, c, chips = _place()
    return [(_remote(ro[0], nw[0].at[j], (cx, cy, c), sems(j)),) * 2 for j, (cx, cy) in enumerate(chips)]


def _reduce_halves(ro, io, nw, sems):
    x, y, c, _ = _place()
    g = ro[0]
    hk = g.shape[1] // 2
    sib = (x, y, 1 - c)
    return [(_remote(g.at[:, _half(1 - c, hk)], nw[0], sib, sems(0)), _remote(g.at[:, _half(c, hk)], nw[0], sib, sems(0)))]


def _reduce_chips(ro, io, nw, sems):
    x, y, c, chips = _place()
    return [(_remote(ro[0].at[_chip_id(cx, cy)], nw[0].at[j], (cx, cy, c), sems(j)),) * 2 for j, (cx, cy) in enumerate(chips)]


def _reduce_share(layer):
    def maker(ro, io, nw, sems):
        g = io[0]
        x, y, c, _ = _place()
        hk = g.shape[1] // 2
        sib = (x, y, 1 - c)
        mine, theirs = g.at[layer, _half(c, hk)], g.at[layer, _half(1 - c, hk)]
        return [(_remote(mine, mine, sib, sems(0)), _remote(theirs, theirs, sib, sems(0)))]
    return maker


def _rms_inproj(x, g_row, win, layer, tm, comm=None):
    T, D = x.shape
    ns = win.shape[-1]

    def body(x_ref, g_ref, w_ref, p_ref, h_ref):
        @pl.when(pl.program_id(1) == 0)
        def _():
            xv = x_ref[...]
            h_ref[...] = (xv * _rms_stats(xv) * g_ref[...]).astype(BF16)
        p_ref[...] = jnp.dot(h_ref[...], w_ref[...], preferred_element_type=F32).astype(BF16)

    return _call(
        body, comm, name=f"rms_inproj_{layer}", grid=(T // tm, N_CHIP),
        in_specs=[pl.BlockSpec((tm, D), lambda i, j: (i, 0)),
                  pl.BlockSpec((1, D), lambda i, j: (0, 0)),
                  pl.BlockSpec((None, D, ns), lambda i, j: (j, 0, 0))],
        out_specs=[pl.BlockSpec((tm, ns), lambda i, j: (i, j)),
                   pl.BlockSpec((tm, D), lambda i, j: (i, 0))],
        out_shape=[jax.ShapeDtypeStruct((T, N_CHIP * ns), BF16), jax.ShapeDtypeStruct((T, D), BF16)],
        compiler_params=_params("parallel", "arbitrary"),
    )(x, g_row, win)


def _mixer_recompute(ca, xa, xb, zprev8, xbprev8, cw_ref, wab_ref, sp):
    row = lambda k: cw_ref[pl.ds(k, 1), :]
    z = ca * xa
    z1 = _shift_down(z, 1, zprev8)
    z2 = _shift_down(z, 2, zprev8)
    cz = row(2) * z + row(1) * z1 + row(0) * z2
    x1 = _shift_down(xb, 1, xbprev8)
    x2 = _shift_down(xb, 2, xbprev8)
    x3 = _shift_down(xb, 3, xbprev8)
    u = row(6) * xb + row(5) * x1 + row(4) * x2 + row(3) * x3 + row(7)
    ub = u.astype(BF16)
    nb = wab_ref.shape[0]
    ras, ixs = [], []
    for b in range(nb):
        ri = jnp.dot(ub[:, b * LRU_BLOCK:(b + 1) * LRU_BLOCK], wab_ref[b], preferred_element_type=F32)
        ras.append(ri[:, :LRU_BLOCK])
        ixs.append(ri[:, LRU_BLOCK:])
    r = _sigmoid(jnp.concatenate(ras, axis=1) + row(8))
    gi = _sigmoid(jnp.concatenate(ixs, axis=1) + row(9))
    la = (-LRU_C) * r * sp
    a = jnp.exp(la)
    m = jnp.sqrt(_one_minus_sq(la, a))
    return dict(z=z, z1=z1, z2=z2, cz=cz, x1=x1, x2=x2, x3=x3, u=u, ub=ub, r=r, gi=gi, a=a, m=m)


def _softplus_neg(lam):
    v = -lam
    return jnp.maximum(v, 0.0) + jnp.log1p(jnp.exp(-jnp.abs(v)))


def _mixer_fwd(p, cw, wab, lam_row, layer, tm, comm=None):
    T = p.shape[0]
    D = p.shape[1] // 7
    ngroups = tm // SUBLANES

    def body(ba_ref, ca_ref, xa_ref, xb_ref, gb_ref, cw_ref, wab_ref, lam_ref, ya_ref, yb_ref, h_ref,
             zprev, xbprev, hcarry, a_s, h_s):
        @pl.when(pl.program_id(0) == 0)
        def _():
            zprev[...] = jnp.zeros_like(zprev)
            xbprev[...] = jnp.zeros_like(xbprev)
            hcarry[...] = jnp.zeros_like(hcarry)

        ca = ca_ref[...].astype(F32)
        xa = xa_ref[...].astype(F32)
        xb = xb_ref[...].astype(F32)
        sp = _softplus_neg(lam_ref[...])
        c = _mixer_recompute(ca, xa, xb, zprev[...], xbprev[...], cw_ref, wab_ref, sp)
        zprev[...] = c["z"][tm - SUBLANES:]
        xbprev[...] = xb[tm - SUBLANES:]
        ya_ref[...] = (ba_ref[...].astype(F32) * c["cz"]).astype(BF16)

        a_s[...], h_s[...] = _group_scan(c["a"], c["m"] * c["gi"] * c["u"], reverse=False)

        def step(g, carry):
            off = pl.multiple_of(g * SUBLANES, SUBLANES)
            hg = h_s[pl.ds(off, SUBLANES), :] + a_s[pl.ds(off, SUBLANES), :] * carry
            h_s[pl.ds(off, SUBLANES), :] = hg
            return jnp.broadcast_to(hg[SUBLANES - 1:SUBLANES, :], hg.shape)

        hcarry[...] = lax.fori_loop(0, ngroups, step, hcarry[...], unroll=4)
        h = h_s[...]
        h_ref[...] = h
        gelu, _ = _gelu_parts(gb_ref[...].astype(F32))
        yb_ref[...] = (h * gelu).astype(BF16)

    col = lambda k: pl.BlockSpec((tm, D), lambda i: (i, k))
    full = lambda a: pl.BlockSpec(a.shape, lambda i: (0,) * a.ndim)
    tok = pl.BlockSpec((tm, D), lambda i: (i, 0))
    return _call(
        body, comm, name=f"mixer_fwd_{layer}", grid=(T // tm,),
        in_specs=[col(0), col(1), col(2), col(3), col(4), full(cw), full(wab), full(lam_row)],
        out_specs=[tok, tok, tok],
        out_shape=[jax.ShapeDtypeStruct((T, D), BF16), jax.ShapeDtypeStruct((T, D), BF16), jax.ShapeDtypeStruct((T, D), F32)],
        scratch_shapes=[pltpu.VMEM((SUBLANES, D), F32), pltpu.VMEM((SUBLANES, D), F32), pltpu.VMEM((SUBLANES, D), F32),
                        pltpu.VMEM((tm, D), F32), pltpu.VMEM((tm, D), F32)],
        compiler_params=_params("arbitrary"),
    )(p, p, p, p, p, cw, wab, lam_row)


def _merge_fwd(x, p, ya, yb, woa, wob, wo, gbias, layer, tm, comm=None):
    T, D = x.shape

    def body(x_ref, ga_ref, gb_ref, ya_ref, yb_ref, woa_ref, wob_ref, wo_ref, bias_ref, oa_ref, ob_ref, mg_ref, x1_ref):
        oa = jnp.dot(ya_ref[...], woa_ref[...], preferred_element_type=F32)
        ob = jnp.dot(yb_ref[...], wob_ref[...], preferred_element_type=F32)
        sa = _sigmoid(ga_ref[...].astype(F32) + bias_ref[pl.ds(0, 1), :])
        sb = _sigmoid(gb_ref[...].astype(F32) + bias_ref[pl.ds(1, 1), :])
        mg = (sa * oa + sb * ob).astype(BF16)
        oa_ref[...] = oa.astype(BF16)
        ob_ref[...] = ob.astype(BF16)
        mg_ref[...] = mg
        x1_ref[...] = x_ref[...] + jnp.dot(mg, wo_ref[...], preferred_element_type=F32)

    tok = pl.BlockSpec((tm, D), lambda i: (i, 0))
    wsp = pl.BlockSpec((D, D), lambda i: (0, 0))
    bf = jax.ShapeDtypeStruct((T, D), BF16)
    return _call(
        body, comm, name=f"merge_fwd_{layer}", grid=(T // tm,),
        in_specs=[tok, pl.BlockSpec((tm, D), lambda i: (i, 5)), pl.BlockSpec((tm, D), lambda i: (i, 6)), tok, tok,
                  wsp, wsp, wsp, pl.BlockSpec(gbias.shape, lambda i: (0, 0))],
        out_specs=[tok, tok, tok, tok],
        out_shape=[bf, bf, bf, jax.ShapeDtypeStruct((T, D), F32)],
        compiler_params=_params("parallel"),
    )(x, p, p, ya, yb, woa, wob, wo, gbias)


def _loss_tile(xv, g, tgt):
    d = xv.shape[-1]
    rstd = _rms_stats(xv)
    xhat = xv * rstd
    err = xhat * g - tgt
    dy = err * (1.0 / d)
    dxhat = dy * g
    dx = rstd * (dxhat - xhat * jnp.mean(dxhat * xhat, axis=-1, keepdims=True))
    return dx, _colsum8(err * err), _colsum8(dy * xhat)


def _ffn_fwd(x1, g_row, wg, wu, wd, layer, tm, comm=None):
    T, D = x1.shape
    fs = wg.shape[-2]
    n = T // tm
    nt = (((1,), (1,)), ((), ()))

    def body(x_ref, g_ref, wg_ref, wu_ref, wd_ref, h_ref, gg_ref, uu_ref, x2_ref, acc):
        j = pl.program_id(1)

        @pl.when(j == 0)
        def _():
            xv = x_ref[...]
            h_ref[...] = (xv * _rms_stats(xv) * g_ref[...]).astype(BF16)
            acc[...] = xv

        h = h_ref[...]
        gg = lax.dot_general(h, wg_ref[...], nt, preferred_element_type=F32)
        uu = lax.dot_general(h, wu_ref[...], nt, preferred_element_type=F32)
        gg_ref[...] = gg.astype(BF16)
        uu_ref[...] = uu.astype(BF16)
        act = (gg * _sigmoid(gg) * uu).astype(BF16)
        acc[...] += jnp.dot(act, wd_ref[...], preferred_element_type=F32)

        @pl.when(j == N_CHIP - 1)
        def _():
            x2_ref[...] = acc[...]

    tok = pl.BlockSpec((tm, D), lambda i, j: (i, 0))
    cm = pl.BlockSpec((None, tm, fs), lambda i, j: (j, i, 0))
    wsp = pl.BlockSpec((None, fs, D), lambda i, j: (j, 0, 0))
    return _call(
        body, comm, name=f"ffn_fwd_{layer}", grid=(n, N_CHIP),
        in_specs=[tok, pl.BlockSpec((1, D), lambda i, j: (0, 0)), wsp, wsp, wsp], out_specs=[tok, cm, cm, tok],
        out_shape=[jax.ShapeDtypeStruct((T, D), BF16), jax.ShapeDtypeStruct((N_CHIP, T, fs), BF16),
                   jax.ShapeDtypeStruct((N_CHIP, T, fs), BF16), jax.ShapeDtypeStruct((T, D), F32)],
        scratch_shapes=[pltpu.VMEM((tm, D), F32)], compiler_params=_params("parallel", "arbitrary"),
    )(x1, g_row, wg, wu, wd)


def _final_loss(x, g_row, target, tm):
    T, D = x.shape
    n = T // tm

    def body(x_ref, g_ref, t_ref, dx_ref, red_ref, racc):
        i = pl.program_id(0)

        @pl.when(i == 0)
        def _():
            racc[...] = jnp.zeros_like(racc)

        dx_ref[...], sq, dg = _loss_tile(x_ref[...], g_ref[...], t_ref[...])
        racc[0] += sq
        racc[1] += dg

        @pl.when(i == n - 1)
        def _():
            red_ref[pl.ds(0, 1), :] = jnp.sum(racc[0], axis=0, keepdims=True) * (0.5 / D)
            red_ref[pl.ds(1, 1), :] = jnp.sum(racc[1], axis=0, keepdims=True)

    tok = pl.BlockSpec((tm, D), lambda i: (i, 0))
    return pl.pallas_call(
        body, name="final_loss", grid=(n,),
        in_specs=[tok, pl.BlockSpec((1, D), lambda i: (0, 0)), tok],
        out_specs=[tok, pl.BlockSpec((2, D), lambda i: (0, 0))],
        out_shape=[jax.ShapeDtypeStruct((T, D), F32), jax.ShapeDtypeStruct((2, D), F32)],
        scratch_shapes=[pltpu.VMEM((2, SUBLANES, D), F32)],
        compiler_params=_params("arbitrary"),
    )(x, g_row, target)


def _ffn_bwd_gates(dx2, gg, uu, wd, layer, tm):
    T, D = dx2.shape
    fs = wd.shape[-2]
    nt = (((1,), (1,)), ((), ()))

    def body(dx_ref, gg_ref, uu_ref, wd_ref, dg_ref, du_ref, act_ref, dxb_ref):
        @pl.when(pl.program_id(1) == 0)
        def _():
            dxb_ref[...] = dx_ref[...].astype(BF16)

        dact = lax.dot_general(dxb_ref[...], wd_ref[...], nt, preferred_element_type=F32)
        g = gg_ref[...].astype(F32)
        u = uu_ref[...].astype(F32)
        s = _sigmoid(g)
        silu = g * s
        dg_ref[...] = (dact * u * (s * (1.0 + g * (1.0 - s)))).astype(BF16)
        du_ref[...] = (dact * silu).astype(BF16)
        act_ref[...] = (silu * u).astype(BF16)

    tok = pl.BlockSpec((tm, D), lambda i, j: (i, 0))
    cm = pl.BlockSpec((None, tm, fs), lambda i, j: (j, i, 0))
    cms = jax.ShapeDtypeStruct((N_CHIP, T, fs), BF16)
    return pl.pallas_call(
        body, name=f"ffn_bwd_gates_{layer}", grid=(T // tm, N_CHIP),
        in_specs=[tok, cm, cm, pl.BlockSpec((None, fs, D), lambda i, j: (j, 0, 0))], out_specs=[cm, cm, cm, tok],
        out_shape=[cms, cms, cms, jax.ShapeDtypeStruct((T, D), BF16)],
        compiler_params=_params("parallel", "arbitrary"),
    )(dx2, gg, uu, wd)


def _ffn_bwd(dgg, duu, wg, wu, dx2, x1, g_row, layer, tm, comm=None):
    T, D = dx2.shape
    fs = wg.shape[-2]
    n = T // tm

    def body(dg_ref, du_ref, wg_ref, wu_ref, dx_ref, x_ref, g_ref, dx1_ref, red_ref, acc, racc):
        i = pl.program_id(0)
        j = pl.program_id(1)

        @pl.when((i == 0) & (j == 0))
        def _():
            racc[...] = jnp.zeros_like(racc)

        @pl.when(j == 0)
        def _():
            acc[...] = jnp.zeros_like(acc)

        acc[...] += (jnp.dot(dg_ref[...], wg_ref[...], preferred_element_type=F32)
                     + jnp.dot(du_ref[...], wu_ref[...], preferred_element_type=F32))

        @pl.when(j == N_CHIP - 1)
        def _():
            dx, dgain = _rms_bwd(acc[...], x_ref[...], g_ref[...])
            dx1_ref[...] = dx_ref[...] + dx
            racc[...] += _colsum8(dgain)

        @pl.when((i == n - 1) & (j == N_CHIP - 1))
        def _():
            red_ref[...] = jnp.sum(racc[...], axis=0, keepdims=True)

    tok = pl.BlockSpec((tm, D), lambda i, j: (i, 0))
    cm = pl.BlockSpec((None, tm, fs), lambda i, j: (j, i, 0))
    wsp = pl.BlockSpec((None, fs, D), lambda i, j: (j, 0, 0))
    row = pl.BlockSpec((1, D), lambda i, j: (0, 0))
    return _call(
        body, comm, name=f"ffn_bwd_{layer}", grid=(n, N_CHIP),
        in_specs=[cm, cm, wsp, wsp, tok, tok, row], out_specs=[tok, row],
        out_shape=[jax.ShapeDtypeStruct((T, D), F32), jax.ShapeDtypeStruct((1, D), F32)],
        scratch_shapes=[pltpu.VMEM((tm, D), F32), pltpu.VMEM((SUBLANES, D), F32)],
        compiler_params=_params("arbitrary", "arbitrary"),
    )(dgg, duu, wg, wu, dx2, x1, g_row)


def _merge_bwd(dx1, p, oa, ob, woa, wob, wo, gbias, layer, tm, comm=None):
    T, D = dx1.shape
    n = T // tm
    nt = (((1,), (1,)), ((), ()))

    def body(dx_ref, ga_ref, gb_ref, oa_ref, ob_ref, woa_ref, wob_ref, wo_ref, bias_ref,
             dya_ref, dyb_ref, doa_ref, dob_ref, dgl_ref, dxb_ref, red_ref, racc):
        i = pl.program_id(0)

        @pl.when(i == 0)
        def _():
            racc[...] = jnp.zeros_like(racc)

        dxb = dx_ref[...].astype(BF16)
        dxb_ref[...] = dxb
        dm = lax.dot_general(dxb, wo_ref[...], nt, preferred_element_type=F32)
        sa = _sigmoid(ga_ref[...].astype(F32) + bias_ref[pl.ds(0, 1), :])
        sb = _sigmoid(gb_ref[...].astype(F32) + bias_ref[pl.ds(1, 1), :])
        doa = (dm * sa).astype(BF16)
        dob = (dm * sb).astype(BF16)
        dga = dm * oa_ref[...].astype(F32) * (sa * (1.0 - sa))
        dgb = dm * ob_ref[...].astype(F32) * (sb * (1.0 - sb))
        doa_ref[...] = doa
        dob_ref[...] = dob
        dgl_ref[:, 0:D] = dga.astype(BF16)
        dgl_ref[:, D:2 * D] = dgb.astype(BF16)
        racc[0] += _colsum8(dga)
        racc[1] += _colsum8(dgb)
        dya_ref[...] = lax.dot_general(doa, woa_ref[...], nt, preferred_element_type=F32).astype(BF16)
        dyb_ref[...] = lax.dot_general(dob, wob_ref[...], nt, preferred_element_type=F32).astype(BF16)

        @pl.when(i == n - 1)
        def _():
            red_ref[pl.ds(0, 1), :] = jnp.sum(racc[0], axis=0, keepdims=True)
            red_ref[pl.ds(1, 1), :] = jnp.sum(racc[1], axis=0, keepdims=True)

    tok = pl.BlockSpec((tm, D), lambda i: (i, 0))
    wsp = pl.BlockSpec((D, D), lambda i: (0, 0))
    bf = jax.ShapeDtypeStruct((T, D), BF16)
    return _call(
        body, comm, name=f"merge_bwd_{layer}", grid=(n,),
        in_specs=[tok, pl.BlockSpec((tm, D), lambda i: (i, 5)), pl.BlockSpec((tm, D), lambda i: (i, 6)), tok, tok,
                  wsp, wsp, wsp, pl.BlockSpec(gbias.shape, lambda i: (0, 0))],
        out_specs=[tok, tok, tok, tok, pl.BlockSpec((tm, 2 * D), lambda i: (i, 0)), tok, pl.BlockSpec((2, D), lambda i: (0, 0))],
        out_shape=[bf, bf, bf, bf, jax.ShapeDtypeStruct((T, 2 * D), BF16), bf, jax.ShapeDtypeStruct((2, D), F32)],
        scratch_shapes=[pltpu.VMEM((2, SUBLANES, D), F32)],
        compiler_params=_params("arbitrary"),
    )(dx1, p, p, oa, ob, woa, wob, wo, gbias)


N_MIXER_RED = 16


def _mixer_bwd(p, hseq, dya, dyb, dgl, cw, wab, wabt, lam_row, layer, tm, comm=None):
    T = p.shape[0]
    D = p.shape[1] // 7
    n = T // tm
    ngroups = tm // SUBLANES
    nb = wab.shape[0]
    hb = 16
    tn = (((0,), (0,)), ((), ()))

    def body(ba_ref, ca_ref, xa_ref, xb_ref, gb_ref, h_ref, dya_ref, dyb_ref, dgl_ref,
             cap_ref, xap_ref, xbp_ref, hp_ref, ban_ref, dyan_ref,
             cw_ref, wab_ref, wabt_ref, lam_ref,
             dp_ref, red_ref, dwab_ref,
             racc, wacc, anext, gnext, dunext, c_s, g_s):
        i = pl.program_id(0)
        first_tile = i == n - 1
        last_tile = i == 0

        @pl.when(i == 0)
        def _():
            racc[...] = jnp.zeros_like(racc)
            wacc[...] = jnp.zeros_like(wacc)
            anext[...] = jnp.zeros_like(anext)
            gnext[...] = jnp.zeros_like(gnext)
            dunext[...] = jnp.zeros_like(dunext)

        keep_prev = jnp.where(first_tile, 0.0, 1.0)
        keep_next = jnp.where(last_tile, 0.0, 1.0)
        ba = ba_ref[...].astype(F32)
        ca = ca_ref[...].astype(F32)
        xa = xa_ref[...].astype(F32)
        xb = xb_ref[...].astype(F32)
        h = h_ref[...]
        dya = dya_ref[...].astype(F32)
        dyb = dyb_ref[...].astype(F32)
        zprev8 = (cap_ref[...].astype(F32) * xap_ref[...].astype(F32))[hb - SUBLANES:] * keep_prev
        xbprev8 = xbp_ref[...].astype(F32)[hb - SUBLANES:] * keep_prev
        hprev8 = hp_ref[...] * keep_prev
        dcznext8 = (dyan_ref[...].astype(F32) * ban_ref[...].astype(F32))[:SUBLANES] * keep_next

        lam = lam_ref[...]
        sp = _softplus_neg(lam)
        c = _mixer_recompute(ca, xa, xb, zprev8, xbprev8, cw_ref, wab_ref, sp)
        row = lambda k: cw_ref[pl.ds(k, 1), :]
        a, m, r, gi, u = c["a"], c["m"], c["r"], c["gi"], c["u"]

        gelu, dgelu = _gelu_parts(gb_ref[...].astype(F32))
        dgb = dyb * h * dgelu
        c_s[...], g_s[...] = _group_scan(_shift_up(a, 1, anext[...]), dyb * gelu, reverse=True)

        def step(k, carry):
            off = pl.multiple_of((ngroups - 1 - k) * SUBLANES, SUBLANES)
            gg = g_s[pl.ds(off, SUBLANES), :] + c_s[pl.ds(off, SUBLANES), :] * carry
            g_s[pl.ds(off, SUBLANES), :] = gg
            return jnp.broadcast_to(gg[0:1, :], gg.shape)

        gnext[...] = lax.fori_loop(0, ngroups, step, gnext[...], unroll=4)
        anext[...] = a[0:SUBLANES]
        g = g_s[...]

        hprev = _shift_down(h, 1, hprev8)
        da = g * hprev
        gm = g * m
        dgi = gm * u
        du = gm * gi
        dmv = g * gi * u
        dla = a * (da - dmv * a / m)
        dra = dla * ((-LRU_C) * sp) * (r * (1.0 - r))
        dix = dgi * (gi * (1.0 - gi))
        racc[10] += _colsum8(dla * r)
        racc[8] += _colsum8(dra)
        racc[9] += _colsum8(dix)
        drab = dra.astype(BF16)
        dixb = dix.astype(BF16)
        ub = c["ub"]
        dus = []
        for b in range(nb):
            sl = slice(b * LRU_BLOCK, (b + 1) * LRU_BLOCK)
            dri = jnp.concatenate([drab[:, sl], dixb[:, sl]], axis=1)
            dus.append(jnp.dot(dri, wabt_ref[b], preferred_element_type=F32))
            wacc[b] += lax.dot_general(ub[:, sl], dri, tn, preferred_element_type=F32)
        du = du + jnp.concatenate(dus, axis=1)

        dun = dunext[...]
        du1 = _shift_up(du, 1, dun)
        du2 = _shift_up(du, 2, dun)
        du3 = _shift_up(du, 3, dun)
        dxb = row(6) * du + row(5) * du1 + row(4) * du2 + row(3) * du3
        dunext[...] = du[0:SUBLANES]
        racc[6] += _colsum8(du * xb)
        racc[5] += _colsum8(du * c["x1"])
        racc[4] += _colsum8(du * c["x2"])
        racc[3] += _colsum8(du * c["x3"])
        racc[7] += _colsum8(du)

        dba = dya * c["cz"]
        dcz = dya * ba
        dcz1 = _shift_up(dcz, 1, dcznext8)
        dcz2 = _shift_up(dcz, 2, dcznext8)
        dz = row(2) * dcz + row(1) * dcz1 + row(0) * dcz2
        racc[2] += _colsum8(dcz * c["z"])
        racc[1] += _colsum8(dcz * c["z1"])
        racc[0] += _colsum8(dcz * c["z2"])

        dp_ref[:, 0:D] = dba.astype(BF16)
        dp_ref[:, D:2 * D] = (dz * xa).astype(BF16)
        dp_ref[:, 2 * D:3 * D] = (dz * ca).astype(BF16)
        dp_ref[:, 3 * D:4 * D] = dxb.astype(BF16)
        dp_ref[:, 4 * D:5 * D] = dgb.astype(BF16)
        dp_ref[:, 5 * D:7 * D] = dgl_ref[...]

        @pl.when(i == n - 1)
        def _():
            dlam_scale = LRU_C * _sigmoid(-lam)
            for k in range(N_MIXER_RED):
                tot = jnp.sum(racc[k], axis=0, keepdims=True)
                red_ref[pl.ds(k, 1), :] = tot * dlam_scale if k == 10 else tot
            dwab_ref[...] = wacc[...]

    rt = lambda i: n - 1 - i
    col = lambda k: pl.BlockSpec((tm, D), lambda i: (rt(i), k))
    tok = pl.BlockSpec((tm, D), lambda i: (rt(i), 0))
    full = lambda a: pl.BlockSpec(a.shape, lambda i: (0,) * a.ndim)
    prev16 = lambda k: pl.BlockSpec((hb, D), lambda i: (jnp.maximum(rt(i) * (tm // hb) - 1, 0), k))
    next16 = lambda k: pl.BlockSpec((hb, D), lambda i: (jnp.minimum((rt(i) + 1) * (tm // hb), T // hb - 1), k))
    hprev = pl.BlockSpec((SUBLANES, D), lambda i: (jnp.maximum(rt(i) * ngroups - 1, 0), 0))
    return _call(
        body, comm, name=f"mixer_bwd_{layer}", grid=(n,),
        in_specs=[col(0), col(1), col(2), col(3), col(4), tok, tok, tok, pl.BlockSpec((tm, 2 * D), lambda i: (rt(i), 0)),
                  prev16(1), prev16(2), prev16(3), hprev, next16(0), next16(0),
                  full(cw), full(wab), full(wabt), full(lam_row)],
        out_specs=[pl.BlockSpec((tm, 7 * D), lambda i: (rt(i), 0)),
                   pl.BlockSpec((N_MIXER_RED, D), lambda i: (0, 0)),
                   pl.BlockSpec((nb, LRU_BLOCK, 2 * LRU_BLOCK), lambda i: (0, 0, 0))],
        out_shape=[jax.ShapeDtypeStruct((T, 7 * D), BF16), jax.ShapeDtypeStruct((N_MIXER_RED, D), F32),
                   jax.ShapeDtypeStruct((nb, LRU_BLOCK, 2 * LRU_BLOCK), F32)],
        scratch_shapes=[pltpu.VMEM((N_MIXER_RED, SUBLANES, D), F32), pltpu.VMEM((nb, LRU_BLOCK, 2 * LRU_BLOCK), F32),
                        pltpu.VMEM((SUBLANES, D), F32), pltpu.VMEM((SUBLANES, D), F32), pltpu.VMEM((SUBLANES, D), F32),
                        pltpu.VMEM((tm, D), F32), pltpu.VMEM((tm, D), F32)],
        compiler_params=_params("arbitrary"),
    )(p, p, p, p, p, hseq, dya, dyb, dgl, p, p, p, hseq, p, dya, cw, wab, wabt, lam_row)


def _inproj_bwd(dp, dx1, x, g_row, win, layer, tm, comm=None):
    T, D = x.shape
    ns = win.shape[-1]
    n = T // tm
    nt = (((1,), (1,)), ((), ()))

    def body(dp_ref, dx_ref, x_ref, g_ref, w_ref, dx0_ref, red_ref, acc, racc):
        i = pl.program_id(0)
        j = pl.program_id(1)

        @pl.when((i == 0) & (j == 0))
        def _():
            racc[...] = jnp.zeros_like(racc)

        @pl.when(j == 0)
        def _():
            acc[...] = jnp.zeros_like(acc)

        acc[...] += lax.dot_general(dp_ref[...], w_ref[...], nt, preferred_element_type=F32)

        @pl.when(j == N_CHIP - 1)
        def _():
            dx, dgain = _rms_bwd(acc[...], x_ref[...], g_ref[...])
            dx0_ref[...] = dx_ref[...] + dx
            racc[...] += _colsum8(dgain)

        @pl.when((i == n - 1) & (j == N_CHIP - 1))
        def _():
            red_ref[...] = jnp.sum(racc[...], axis=0, keepdims=True)

    tok = pl.BlockSpec((tm, D), lambda i, j: (i, 0))
    return _call(
        body, comm, name=f"inproj_bwd_{layer}", grid=(n, N_CHIP),
        in_specs=[pl.BlockSpec((tm, ns), lambda i, j: (i, j)), tok, tok, pl.BlockSpec((1, D), lambda i, j: (0, 0)),
                  pl.BlockSpec((None, D, ns), lambda i, j: (j, 0, 0))],
        out_specs=[tok, pl.BlockSpec((1, D), lambda i, j: (0, 0))],
        out_shape=[jax.ShapeDtypeStruct((T, D), F32), jax.ShapeDtypeStruct((1, D), F32)],
        scratch_shapes=[pltpu.VMEM((tm, D), F32), pltpu.VMEM((SUBLANES, D), F32)],
        compiler_params=_params("arbitrary", "arbitrary"),
    )(dp, dx1, x, g_row, win)


def _wgrad(a, b, name, tk, a_kind="whole", b_kind="whole", nj=1, comm=None):
    T = a.shape[-2]
    width = lambda v, kind: v.shape[-1] // nj if kind == "cols" else v.shape[-1]
    ka, kb = width(a, a_kind), width(b, b_kind)
    nt = T // tk
    tn = (((0,), (0,)), ((), ()))

    def spec(k, kind):
        if kind == "cm":
            return pl.BlockSpec((None, tk, k), lambda j, t: (j, t, 0))
        if kind == "cols":
            return pl.BlockSpec((tk, k), lambda j, t: (t, j))
        return pl.BlockSpec((tk, k), lambda j, t: (t, 0))

    def body(a_ref, b_ref, o_ref, ob_ref):
        t = pl.program_id(1)

        @pl.when(t == 0)
        def _():
            o_ref[...] = jnp.zeros_like(o_ref)

        o_ref[...] += lax.dot_general(a_ref[...], b_ref[...], tn, preferred_element_type=F32)

        @pl.when(t == nt - 1)
        def _():
            ob_ref[...] = o_ref[...].astype(BF16)

    o_spec = pl.BlockSpec((None, ka, kb), lambda j, t: (j, 0, 0))
    return _call(
        body, comm, name=name, grid=(nj, nt),
        in_specs=[spec(ka, a_kind), spec(kb, b_kind)], out_specs=[o_spec, o_spec],
        out_shape=[jax.ShapeDtypeStruct((nj, ka, kb), F32), jax.ShapeDtypeStruct((nj, ka, kb), BF16)],
        compiler_params=_params("parallel", "arbitrary"),
    )(a, b)


def _wgrad_pair(a, b1, b2, name, tk, comm=None):
    T, ka = a.shape
    nj, _, kb = b1.shape
    nt = T // tk
    tn = (((0,), (0,)), ((), ()))

    def body(a_ref, b1_ref, b2_ref, o1_ref, o1b_ref, o2_ref, o2b_ref):
        t = pl.program_id(1)

        @pl.when(t == 0)
        def _():
            o1_ref[...] = jnp.zeros_like(o1_ref)
            o2_ref[...] = jnp.zeros_like(o2_ref)

        av = a_ref[...]
        o1_ref[...] += lax.dot_general(b1_ref[...], av, tn, preferred_element_type=F32)
        o2_ref[...] += lax.dot_general(b2_ref[...], av, tn, preferred_element_type=F32)

        @pl.when(t == nt - 1)
        def _():
            o1b_ref[...] = o1_ref[...].astype(BF16)
            o2b_ref[...] = o2_ref[...].astype(BF16)

    b_spec = pl.BlockSpec((None, tk, kb), lambda j, t: (j, t, 0))
    o_spec = pl.BlockSpec((None, kb, ka), lambda j, t: (j, 0, 0))
    f32 = jax.ShapeDtypeStruct((nj, kb, ka), F32)
    b16 = jax.ShapeDtypeStruct((nj, kb, ka), BF16)
    return _call(
        body, comm, name=name, grid=(nj, nt),
        in_specs=[pl.BlockSpec((tk, ka), lambda j, t: (t, 0)), b_spec, b_spec], out_specs=[o_spec] * 4,
        out_shape=[f32, b16, f32, b16], compiler_params=_params("parallel", "arbitrary"),
    )(a, b1, b2)


def _block_diag(w):
    hb = LRU_BLOCK // LRU_HEAD_DIM
    nb = w.shape[0] // hb
    w4 = w.reshape(nb, hb, LRU_HEAD_DIM, LRU_HEAD_DIM)
    eye = jnp.eye(hb, dtype=w.dtype)
    return jnp.einsum("bide,ij->bidje", w4, eye).reshape(nb, LRU_BLOCK, LRU_BLOCK)


def _diag_heads(m):
    hb = LRU_BLOCK // LRU_HEAD_DIM
    nb = m.shape[0]
    m5 = m.reshape(nb, hb, LRU_HEAD_DIM, hb, LRU_HEAD_DIM)
    eye = jnp.eye(hb, dtype=m.dtype)
    return jnp.einsum("bidje,ij->bide", m5, eye).reshape(nb * hb, LRU_HEAD_DIM, LRU_HEAD_DIM)


def _tiles(T):
    cap = lambda n: min(n, T)
    return dict(inproj=cap(1024), mixer=cap(256), merge=cap(512), ffn=cap(1024), ffn_bwd=cap(1024), loss=cap(512), inproj_bwd=cap(1024),
                wgrad_in=cap(2048), wgrad=cap(2048))


class _NoSchedule:
    def carry(self, name):
        return None

    def tie(self, name, operand):
        return operand

    def after(self, name, result=None):
        pass

    def grad(self, key, layer, f32, b16):
        pass


def _local_step(x, target, W, small, tiles, sched):
    L = small["ln1_g"].shape[0]
    D = x.shape[1]
    square = lambda a: a.reshape(D, D)
    saved = []
    h = x
    for l in range(L):
        cw = jnp.concatenate([small["conv_a_w"][l], small["conv_b_w"][l], small["conv_b_b"][l][None],
                              small["lru_ba"][l][None], small["lru_bx"][l][None]], axis=0)
        wab = jnp.concatenate([_block_diag(small["lru_wa"][l]), _block_diag(small["lru_wx"][l])], axis=2).astype(BF16)
        wabt = jnp.swapaxes(wab, 1, 2)
        lam_row = small["lru_lambda"][l][None]
        ln1_row = small["ln1_g"][l][None]
        ln2_row = small["ln2_g"][l][None]
        p, h1 = _rms_inproj(h, ln1_row, W["win", l], l, tiles["inproj"], sched.carry(f"rms_inproj_{l}"))
        ya, yb, hseq = _mixer_fwd(p, cw, wab, lam_row, l, tiles["mixer"], sched.carry(f"mixer_fwd_{l}"))
        oa, ob, mg, x1 = _merge_fwd(h, p, ya, yb, square(W["woa", l]), square(W["wob", l]), square(W["wo", l]),
                                    small["gate_bias"][l], l, tiles["merge"], sched.carry(f"merge_fwd_{l}"))
        h2, gg, uu, x2 = _ffn_fwd(x1, ln2_row, W["wg", l], W["wu", l], W["wd", l], l, tiles["ffn"], sched.carry(f"ffn_fwd_{l}"))
        saved.append(dict(x0=h, p=p, h1=h1, ya=ya, yb=yb, hseq=hseq, oa=oa, ob=ob, mg=mg, x1=x1, h2=h2, gg=gg, uu=uu,
                          cw=cw, wab=wab, wabt=wabt, lam_row=lam_row, ln1_row=ln1_row, ln2_row=ln2_row))
        h = x2

    dx, red = _final_loss(h, small["final_g"][None], target, tiles["loss"])
    loss_row, d_final_g = red[0], red[1]

    gsmall = {k: [None] * L for k in ("ln1_g", "ln2_g", "conv_a_w", "conv_b_w", "conv_b_b", "lru_wa", "lru_ba", "lru_wx",
                                      "lru_bx", "lru_lambda", "gate_bias")}
    tk = tiles["wgrad"]
    for l in reversed(range(L)):
        s = saved[l]
        dgg, duu, act, dx2b = _ffn_bwd_gates(dx, s["gg"], s["uu"], W["wd", l], l, tiles["ffn_bwd"])
        dx1, dln2 = _ffn_bwd(dgg, duu, W["wg", l], W["wu", l], dx, s["x1"], s["ln2_row"], l, tiles["ffn_bwd"],
                             sched.carry(f"ffn_bwd_{l}"))
        sched.after(f"ffn_bwd_{l}")
        gate_up = _wgrad_pair(s["h2"], dgg, duu, f"wgrad_ffn_gate_up_{l}", tk, sched.carry(f"wgrad_ffn_gate_up_{l}"))
        sched.grad("wg", l, *gate_up[0:2])
        sched.grad("wu", l, *gate_up[2:4])
        sched.after(f"wgrad_ffn_gate_up_{l}")
        sched.grad("wd", l, *_wgrad(act, dx2b, f"wgrad_ffn_down_{l}", tk, "cm", "whole", N_CHIP, sched.carry(f"wgrad_ffn_down_{l}")))
        dya, dyb, doa, dob, dgl, dx1b, dgbias = _merge_bwd(dx1, s["p"], s["oa"], s["ob"], square(W["woa", l]), square(W["wob", l]),
                                                         square(W["wo", l]), small["gate_bias"][l], l, tiles["merge"],
                                                         sched.carry(f"merge_bwd_{l}"))
        sched.after(f"merge_bwd_{l}")
        sched.grad("wo", l, *_wgrad(s["mg"], dx1b, f"wgrad_w_o_{l}", tk))
        sched.grad("woa", l, *_wgrad(s["ya"], doa, f"wgrad_w_out_a_{l}", tk))
        sched.grad("wob", l, *_wgrad(s["yb"], dob, f"wgrad_w_out_b_{l}", tk))
        dp, mred, dwab = _mixer_bwd(s["p"], s["hseq"], dya, dyb, dgl, s["cw"], s["wab"], s["wabt"], s["lam_row"], l,
                                    tiles["mixer"], sched.carry(f"mixer_bwd_{l}"))
        sched.after(f"mixer_bwd_{l}")
        sched.grad("win", l, *_wgrad(s["h1"], dp, f"wgrad_w_in_{l}", tiles["wgrad_in"], "whole", "cols", N_CHIP,
                                     sched.carry(f"wgrad_w_in_{l}")))
        sched.after(f"wgrad_w_in_{l}")
        dx, dln1 = _inproj_bwd(dp, dx1, s["x0"], sched.tie(f"inproj_bwd_{l}", s["ln1_row"]), W["win", l], l, tiles["inproj_bwd"],
                               sched.carry(f"inproj_bwd_{l}"))
        sched.after(f"inproj_bwd_{l}", dx)
        gsmall["ln1_g"][l] = dln1[0]
        gsmall["ln2_g"][l] = dln2[0]
        gsmall["conv_a_w"][l] = mred[0:CONV_A_K]
        gsmall["conv_b_w"][l] = mred[CONV_A_K:CONV_A_K + CONV_B_K]
        gsmall["conv_b_b"][l] = mred[7]
        gsmall["lru_ba"][l] = mred[8]
        gsmall["lru_bx"][l] = mred[9]
        gsmall["lru_lambda"][l] = mred[10]
        gsmall["lru_wa"][l] = _diag_heads(dwab[:, :, :LRU_BLOCK])
        gsmall["lru_wx"][l] = _diag_heads(dwab[:, :, LRU_BLOCK:])
        gsmall["gate_bias"][l] = dgbias
    gsmall = {k: jnp.stack(v) for k, v in gsmall.items()}
    gsmall["final_g"] = d_final_g
    return loss_row, dx, gsmall


def _small_allreduce(buf):
    R, C = buf.shape
    n_dev = 8
    rp = R // n_dev
    rel = [(k >> 2 & 1, k >> 1 & 1, k & 1) for k in range(1, n_dev)]

    def body(in_ref, out_ref, recv, s1, r1, s2, r2):
        x, y, c, _ = _place()
        flip = lambda v, bit: 1 - v if bit else v
        peers = [(flip(x, kx), flip(y, ky), flip(c, kc)) for kx, ky, kc in rel]
        dev = lambda p: 4 * p[0] + 2 * p[1] + p[2]
        part = lambda ref, d: ref.at[pl.ds(pl.multiple_of(d * rp, SUBLANES), rp), :]
        me = dev((x, y, c))

        def scatter(k, src_dev, to):
            return pltpu.make_async_remote_copy(src_ref=part(in_ref, dev(to)), dst_ref=recv.at[src_dev], send_sem=s1.at[k],
                                                recv_sem=r1.at[k], device_id=to, device_id_type=MESH)

        def gather(k, src_dev, to):
            return pltpu.make_async_remote_copy(src_ref=part(out_ref, src_dev), dst_ref=part(out_ref, src_dev), send_sem=s2.at[k],
                                                recv_sem=r2.at[k], device_id=to, device_id_type=MESH)

        first = [scatter(k, me, p) for k, p in enumerate(peers)]
        for cp in first:
            cp.start()
        recv[me] = part(in_ref, me)[...]
        for k, p in enumerate(peers):
            scatter(k, dev(p), (x, y, c)).wait_recv()
        total = recv[0]
        for d in range(1, n_dev):
            total = total + recv[d]
        part(out_ref, me)[...] = total
        second = [gather(k, me, p) for k, p in enumerate(peers)]
        for cp in second:
            cp.start()
        for k, p in enumerate(peers):
            gather(k, dev(p), (x, y, c)).wait_recv()
        for cp in first + second:
            cp.wait_send()

    dma = pltpu.SemaphoreType.DMA
    vm = pl.BlockSpec(memory_space=pltpu.VMEM)
    return pl.pallas_call(
        body, name="small_allreduce", out_shape=jax.ShapeDtypeStruct((R, C), buf.dtype),
        in_specs=[vm], out_specs=vm,
        scratch_shapes=[pltpu.VMEM((n_dev, rp, C), buf.dtype), dma((n_dev - 1,)), dma((n_dev - 1,)), dma((n_dev - 1,)), dma((n_dev - 1,))],
    )(buf)


ELEMENTWISE_BLOCK_BYTES = 2 * 1024 * 1024


def _row_block(k, n):
    best = None
    for b in range(16, k + 1, 16):
        if k % b == 0 and b * n * 4 <= ELEMENTWISE_BLOCK_BYTES:
            best = b
    return best or k


def _add_halves(g, recv, place_arr, name):
    nj, hk, N = recv.shape
    bk = _row_block(hk, N)
    nb = hk // bk

    def body(k_ref, g_ref, r_ref, o_ref, ob_ref):
        s = g_ref[...] + r_ref[...].astype(F32)
        ob_ref[...] = s.astype(BF16)

        @pl.when(pl.program_id(1) == k_ref[0])
        def _():
            o_ref[...] = s

    blk = pl.BlockSpec((None, bk, N), lambda i, j, k_ref: (j, i, 0))
    grid_spec = pltpu.PrefetchScalarGridSpec(
        num_scalar_prefetch=1, grid=(nb, nj),
        in_specs=[pl.BlockSpec((None, bk, N), lambda i, j, k_ref: (j, k_ref[1] * nb + i, 0)), blk],
        out_specs=[pl.BlockSpec((bk, N), lambda i, j, k_ref: (i, 0)), blk])
    return pl.pallas_call(
        body, name=name, grid_spec=grid_spec,
        out_shape=[jax.ShapeDtypeStruct((hk, N), F32), jax.ShapeDtypeStruct((nj, hk, N), BF16)],
        compiler_params=_params("parallel", "arbitrary"),
    )(place_arr, g, recv)


def _add_chips(pc, recv, place_arr, layer, n_layers, prev, name):
    hk, N = pc.shape
    bk = _row_block(hk, N)
    nb = hk // bk

    def body(k_ref, p_ref, r0_ref, r1_ref, r2_ref, *rest):
        o_ref = rest[-1]
        o_ref[...] = ((p_ref[...] + r0_ref[...].astype(F32)) + r1_ref[...].astype(F32)) + r2_ref[...].astype(F32)

    rspec = lambda j: pl.BlockSpec((None, bk, N), lambda i, k_ref: (j, i, 0))
    in_specs = [pl.BlockSpec((bk, N), lambda i, k_ref: (i, 0)), rspec(0), rspec(1), rspec(2)]
    operands = [pc, recv, recv, recv]
    aliases = {}
    if prev is not None:
        in_specs.append(ANY)
        operands.append(prev)
        aliases = {5: 0}
    grid_spec = pltpu.PrefetchScalarGridSpec(
        num_scalar_prefetch=1, grid=(nb,), in_specs=in_specs,
        out_specs=pl.BlockSpec((None, bk, N), lambda i, k_ref: (layer, k_ref[1] * nb + i, 0)))
    return pl.pallas_call(
        body, name=name, grid_spec=grid_spec, out_shape=jax.ShapeDtypeStruct((n_layers, 2 * hk, N), F32),
        input_output_aliases=aliases, compiler_params=_params("parallel"),
    )(place_arr, *operands)


def _adamw_math(w, g, m, v):
    m = ADAM_B1 * m + (1.0 - ADAM_B1) * g
    v = ADAM_B2 * v + (1.0 - ADAM_B2) * (g * g)
    m_hat = m / (1.0 - ADAM_B1 ** ADAM_STEP)
    v_hat = v / (1.0 - ADAM_B2 ** ADAM_STEP)
    delta = -ADAM_LR * (m_hat / (jnp.sqrt(v_hat) + ADAM_EPS) + ADAM_WD * w)
    return delta, m, v


def _adamw(w, g, m, v, name):
    L, K, N = w.shape
    bk = _row_block(K, N)

    def body(w_ref, g_ref, m_ref, v_ref, d_ref, nm_ref, nv_ref, go_ref):
        g = g_ref[...]
        d_ref[...], nm_ref[...], nv_ref[...] = _adamw_math(w_ref[...], g, m_ref[...], v_ref[...])
        go_ref[...] = g

    blk = pl.BlockSpec((None, bk, N), lambda l, i: (l, i, 0))
    sds = jax.ShapeDtypeStruct((L, K, N), F32)
    return pl.pallas_call(
        body, name=name, grid=(L, K // bk), in_specs=[blk] * 4, out_specs=[blk] * 4, out_shape=[sds] * 4,
        compiler_params=_params("parallel", "parallel"),
    )(w, g, m, v)


def _adamw_small(ws, gs, ms, vs):
    n = len(ws)

    def body(*refs):
        w, g, m, v, d, nm, nv = (refs[k * n:(k + 1) * n] for k in range(7))
        for k in range(n):
            d[k][...], nm[k][...], nv[k][...] = _adamw_math(w[k][...], g[k][...], m[k][...], v[k][...])

    sds = [jax.ShapeDtypeStruct(a.shape, F32) for a in ws]
    out = pl.pallas_call(body, name="adamw_small", out_shape=sds * 3)(*ws, *gs, *ms, *vs)
    return out[:n], out[n:2 * n], out[2 * n:]


def _cast_bf16(w, layer, place_arr, name):
    _, K, N = w.shape
    bk = _row_block(K, N)

    def body(k_ref, w_ref, o_ref):
        o_ref[...] = w_ref[...].astype(BF16)

    grid_spec = pltpu.PrefetchScalarGridSpec(
        num_scalar_prefetch=1, grid=(K // bk,),
        in_specs=[pl.BlockSpec((None, bk, N), lambda i, k_ref: (layer, i, 0))],
        out_specs=pl.BlockSpec((None, bk, N), lambda i, k_ref: (k_ref[0], i, 0)))
    return pl.pallas_call(
        body, name=name, grid_spec=grid_spec, out_shape=jax.ShapeDtypeStruct((N_CHIP, K, N), BF16),
        compiler_params=_params("parallel"),
    )(place_arr, w)


BIG = ("w_in", "w_out_a", "w_out_b", "w_o", "w_ffn_gate", "w_ffn_up", "w_ffn_down")
BIG_KEY = dict(w_in="win", w_out_a="woa", w_out_b="wob", w_o="wo", w_ffn_gate="wg", w_ffn_up="wu", w_ffn_down="wd")
SHARDED_SMALL = ("conv_a_w", "conv_b_w", "gate_bias")
REPLICATED = ("ln1_g", "conv_b_b", "lru_wa", "lru_ba", "lru_wx", "lru_bx", "lru_lambda", "ln2_g", "final_g")
WEIGHTS = ("ln1_g", "w_in", "conv_a_w", "conv_b_w", "conv_b_b", "lru_wa", "lru_ba", "lru_wx", "lru_bx", "lru_lambda",
           "w_out_a", "w_out_b", "gate_bias", "w_o", "ln2_g", "w_ffn_gate", "w_ffn_up", "w_ffn_down", "final_g")
LANES = 1024


def _pack_rows(arrays, row_multiple):
    flat = jnp.concatenate([a.reshape(-1) for a in arrays])
    rows = -(-flat.shape[0] // LANES)
    rows = -(-rows // row_multiple) * row_multiple
    flat = jnp.pad(flat, (0, rows * LANES - flat.shape[0]))
    return flat.reshape(rows, LANES)


def _unpack_rows(buf, shapes):
    flat = buf.reshape(-1)
    out, off = [], 0
    for s in shapes:
        n = 1
        for d in s:
            n *= d
        out.append(flat[off:off + n].reshape(s))
        off += n
    return out


OUT_KEYS = ("wo", "woa", "wob")
FFN_KEYS = ("wg", "wu", "wd")


def _items(keys, layer):
    return [(k, layer) for k in keys]


CARRY = {
    "rms_inproj_0": [("gather_ici", _items(OUT_KEYS + FFN_KEYS, 0))],
    "mixer_fwd_0": [("gather_d2d", _items(OUT_KEYS + FFN_KEYS, 0)), ("gather_ici", _items(("win",) + OUT_KEYS, 1))],
    "merge_fwd_0": [("gather_d2d", _items(("win",) + OUT_KEYS, 1))],
    "ffn_fwd_0": [("gather_ici", _items(FFN_KEYS, 1))],
    "rms_inproj_1": [("gather_d2d", _items(FFN_KEYS, 1))],
    "merge_bwd_1": [("halves", _items(FFN_KEYS, 1))],
    "mixer_bwd_1": [("chips", _items(FFN_KEYS, 1)), ("halves", _items(OUT_KEYS, 1))],
    "inproj_bwd_1": [("chips", _items(OUT_KEYS, 1)), ("share", _items(FFN_KEYS, 1))],
    "ffn_bwd_0": [("halves", [("win", 1)]), ("share", _items(OUT_KEYS, 1))],
    "wgrad_ffn_gate_up_0": [("chips", [("win", 1)])],
    "wgrad_ffn_down_0": [("share", [("win", 1)])],
    "merge_bwd_0": [("halves", _items(FFN_KEYS, 0))],
    "mixer_bwd_0": [("chips", _items(FFN_KEYS, 0)), ("halves", _items(OUT_KEYS, 0))],
    "wgrad_w_in_0": [("chips", _items(OUT_KEYS, 0)), ("share", _items(FFN_KEYS, 0))],
}
AFTER = {
    "merge_bwd_1": [("add_halves", _items(FFN_KEYS, 1))],
    "mixer_bwd_1": [("add_chips", _items(FFN_KEYS, 1)), ("add_halves", _items(OUT_KEYS, 1))],
    "inproj_bwd_1": [("add_chips", _items(OUT_KEYS, 1))],
    "ffn_bwd_0": [("add_halves", [("win", 1)])],
    "wgrad_ffn_gate_up_0": [("add_chips", [("win", 1)])],
    "merge_bwd_0": [("add_halves", _items(FFN_KEYS, 0))],
    "mixer_bwd_0": [("add_chips", _items(FFN_KEYS, 0)), ("add_halves", _items(OUT_KEYS, 0))],
    "wgrad_w_in_0": [("add_chips", _items(OUT_KEYS, 0)),
                     ("run", ("reduce_halves_w_in_0", [("halves", [("win", 0)]), ("share", _items(OUT_KEYS, 0))])),
                     ("add_halves", [("win", 0)]),
                     ("split_start", ("win", 0))],
    "inproj_bwd_0": [("split_wait", ("win", 0)), ("add_chips", [("win", 0)]),
                     ("run", ("reduce_share_w_in_0", [("share", [("win", 0)])]))],
}


SPLIT_BESIDE = {"wgrad_w_in_0": "inproj_bwd_0"}


class _Schedule:
    def __init__(self, slots, place_arr, n_layers):
        self.W = slots
        self.place, self.L = place_arr, n_layers
        self.g32, self.g16 = {}, {}
        self.from_sibling, self.chip_sum, self.chip_sum16, self.from_chips = {}, {}, {}, {}
        self.reduced = {}
        self.split, self.tokens = {}, {}

    def stage(self, comm, kind, items):
        bf = lambda shape: jax.ShapeDtypeStruct(shape, BF16)
        for it in items:
            if kind == "gather_ici":
                comm.add(_gather_ici, 3, io=[(self.W, it)])
            elif kind == "gather_d2d":
                comm.add(_gather_d2d, 3, io=[(self.W, it)])
            elif kind == "halves":
                nj, K, N = self.g16[it].shape
                comm.add(_reduce_halves, 1, ro=[self.g16[it]], nw=[(self.from_sibling, it, bf((nj, K // 2, N)))])
            elif kind == "chips":
                _, hk, N = self.chip_sum16[it].shape
                comm.add(_reduce_chips, 3, ro=[self.chip_sum16[it]], nw=[(self.from_chips, it, bf((3, hk, N)))])
            elif kind == "share":
                comm.add(_reduce_share(it[1]), 1, io=[(self.reduced, it[0])])
        return comm

    def carry(self, name):
        comm = _Carried()
        for kind, items in CARRY.get(name, ()):
            self.stage(comm, kind, items)
        return comm

    def run(self, name, rounds):
        _run_comm([self.stage(_Carried(), kind, items) for kind, items in rounds], name)

    def grad(self, key, layer, f32, b16):
        by_chip = lambda g: g.reshape(N_CHIP, -1, g.shape[-1])
        self.g32[key, layer], self.g16[key, layer] = by_chip(f32), by_chip(b16)

    def add(self, kind, items):
        for key, layer in items:
            it = (key, layer)
            if kind == "add_halves":
                self.chip_sum[it], self.chip_sum16[it] = _add_halves(self.g32[it], self.from_sibling[it], self.place,
                                                                    f"add_halves_{key}_{layer}")
            else:
                self.reduced[key] = _add_chips(self.chip_sum[it], self.from_chips[it], self.place, layer, self.L,
                                               self.reduced.get(key), f"add_chips_{key}_{layer}")

    def tie(self, name, operand):
        token = self.tokens.pop(name, None)
        return operand if token is None else operand + token[0, 0]

    def after(self, name, result=None):
        for kind, items in AFTER.get(name, ()):
            if kind == "run":
                self.run(*items)
            elif kind == "split_start":
                send_sems, recv_sems, src, land, token = _scatter_start(self.chip_sum16[items])
                self.split[items] = (send_sems, recv_sems, src, land)
                self.tokens[SPLIT_BESIDE[name]] = token
            elif kind == "split_wait":
                self.from_chips[items] = _scatter_wait(*self.split.pop(items), result)
            else:
                self.add(kind, items)


def _step(w, m, v, x, target):
    xi, yi, ci = lax.axis_index("x"), lax.axis_index("y"), lax.axis_index("c")
    chip = _chip_id(xi, yi)
    place_arr = jnp.stack([chip, ci]).astype(jnp.int32)
    L = w["ln1_g"].shape[0]
    assert L == 2
    D = x.shape[1]
    dc = D // N_CHIP

    stored = lambda n, a: jnp.swapaxes(a, 1, 2) if n in ("w_ffn_gate", "w_ffn_up") else a
    slots = {("win", 0): _cast_bf16(w["w_in"], 0, place_arr, "cast_w_in_0")}
    send_sems, recv_sems, slots["win", 0], token = _gather_first_start(slots["win", 0])
    place_after = place_arr + token[0, 0]
    for n in BIG:
        for l in range(L):
            if (BIG_KEY[n], l) not in slots:
                slots[BIG_KEY[n], l] = _cast_bf16(stored(n, w[n]), l, place_after, f"cast_{n}_{l}")
    slots["win", 0] = _gather_first_wait(send_sems, recv_sems, slots["win", 0], [v for k, v in slots.items() if k != ("win", 0)])
    sched = _Schedule(slots, place_arr, L)
    small_shard = jnp.concatenate([w[n] for n in SHARDED_SMALL], axis=1)
    got = {}
    second = sched.stage(_Carried(), "gather_d2d", [("win", 0)])
    second.add(_gather_small, 3, ro=[small_shard], nw=[(got, "small", jax.ShapeDtypeStruct((3,) + small_shard.shape, F32))])
    _run_comm([second], "gather_first_d2d")
    small_g = jnp.zeros((N_CHIP,) + small_shard.shape, F32)
    small_g = lax.dynamic_update_index_in_dim(small_g, small_shard, chip, 0)
    for j, (cx, cy) in enumerate([(1 - xi, yi), (xi, 1 - yi), (1 - xi, 1 - yi)]):
        small_g = lax.dynamic_update_index_in_dim(small_g, got["small"][j], _chip_id(cx, cy), 0)
    small_full = jnp.transpose(small_g, (1, 2, 0, 3)).reshape(L, small_shard.shape[1], D)
    small = {n: w[n] for n in REPLICATED}
    off = 0
    for n in SHARDED_SMALL:
        k = w[n].shape[1]
        small[n] = small_full[:, off:off + k]
        off += k

    loss_row, grad_x, gsmall = _local_step(x, target, sched.W, small, _tiles(x.shape[0]), sched)

    grads = {}

    order = [n for n in WEIGHTS if n not in BIG]
    packed = _pack_rows([gsmall[n] for n in order] + [loss_row], 8 * SUBLANES)
    summed = _small_allreduce(packed)
    parts = _unpack_rows(summed, [gsmall[n].shape for n in order] + [loss_row.shape])
    loss = jnp.sum(parts[-1])
    for n, g in zip(order, parts[:-1]):
        grads[n] = lax.dynamic_slice_in_dim(g, chip * dc, dc, axis=2) if n in SHARDED_SMALL else g

    delta, new_m, new_v = {}, {}, {}
    for n in BIG:
        d, nm, nv, g = _adamw(stored(n, w[n]), sched.reduced[BIG_KEY[n]], stored(n, m[n]), stored(n, v[n]), f"adamw_{n}")
        delta[n], new_m[n], new_v[n], grads[n] = stored(n, d), stored(n, nm), stored(n, nv), stored(n, g)
    for d, arrays in zip((delta, new_m, new_v), _adamw_small(*([d[n] for n in order] for d in (w, grads, m, v)))):
        d.update(zip(order, arrays))
    return loss, grad_x, grads, delta, new_m, new_v


def kernel(x, ln1_g, w_in, conv_a_w, conv_b_w, conv_b_b, lru_wa, lru_ba, lru_wx, lru_bx, lru_lambda, w_out_a, w_out_b, gate_bias, w_o, ln2_g, w_ffn_gate, w_ffn_up, w_ffn_down, final_g, loss_target, m_ln1_g, m_w_in, m_conv_a_w, m_conv_b_w, m_conv_b_b, m_lru_wa, m_lru_ba, m_lru_wx, m_lru_bx, m_lru_lambda, m_w_out_a, m_w_out_b, m_gate_bias, m_w_o, m_ln2_g, m_w_ffn_gate, m_w_ffn_up, m_w_ffn_down, m_final_g, v_ln1_g, v_w_in, v_conv_a_w, v_conv_b_w, v_conv_b_b, v_lru_wa, v_lru_ba, v_lru_wx, v_lru_bx, v_lru_lambda, v_w_out_a, v_w_out_b, v_gate_bias, v_w_o, v_ln2_g, v_w_ffn_gate, v_w_ffn_up, v_w_ffn_down, v_final_g):
    w = dict(ln1_g=ln1_g, w_in=w_in, conv_a_w=conv_a_w, conv_b_w=conv_b_w, conv_b_b=conv_b_b, lru_wa=lru_wa, lru_ba=lru_ba,
             lru_wx=lru_wx, lru_bx=lru_bx, lru_lambda=lru_lambda, w_out_a=w_out_a, w_out_b=w_out_b, gate_bias=gate_bias, w_o=w_o,
             ln2_g=ln2_g, w_ffn_gate=w_ffn_gate, w_ffn_up=w_ffn_up, w_ffn_down=w_ffn_down, final_g=final_g)
    m = dict(ln1_g=m_ln1_g, w_in=m_w_in, conv_a_w=m_conv_a_w, conv_b_w=m_conv_b_w, conv_b_b=m_conv_b_b, lru_wa=m_lru_wa,
             lru_ba=m_lru_ba, lru_wx=m_lru_wx, lru_bx=m_lru_bx, lru_lambda=m_lru_lambda, w_out_a=m_w_out_a, w_out_b=m_w_out_b,
             gate_bias=m_gate_bias, w_o=m_w_o, ln2_g=m_ln2_g, w_ffn_gate=m_w_ffn_gate, w_ffn_up=m_w_ffn_up,
             w_ffn_down=m_w_ffn_down, final_g=m_final_g)
    v = dict(ln1_g=v_ln1_g, w_in=v_w_in, conv_a_w=v_conv_a_w, conv_b_w=v_conv_b_w, conv_b_b=v_conv_b_b, lru_wa=v_lru_wa,
             lru_ba=v_lru_ba, lru_wx=v_lru_wx, lru_bx=v_lru_bx, lru_lambda=v_lru_lambda, w_out_a=v_w_out_a, w_out_b=v_w_out_b,
             gate_bias=v_gate_bias, w_o=v_w_o, ln2_g=v_ln2_g, w_ffn_gate=v_w_ffn_gate, w_ffn_up=v_w_ffn_up,
             w_ffn_down=v_w_ffn_down, final_g=v_final_g)
    loss, grad_x, grads, delta, new_m, new_v = _step(w, m, v, x[0], loss_target[0])
    return (loss, grad_x[None], *[grads[n] for n in WEIGHTS], *[delta[n] for n in WEIGHTS],
            *[new_m[n] for n in WEIGHTS], *[new_v[n] for n in WEIGHTS])
```

```python
import jax
import jax.numpy as jnp
from jax import lax
from jax.experimental import pallas as pl
from jax.experimental.pallas import tpu as pltpu

F32 = jnp.float32
BF16 = jnp.bfloat16
MESH = pl.DeviceIdType.MESH

N_CHIP = 4
RMS_EPS = 1e-6
LRU_C = 8.0
LRU_HEAD_DIM = 64
LRU_BLOCK = 256
CONV_A_K = 3
CONV_B_K = 4
ADAM_LR = 0.001
ADAM_B1 = 0.9
ADAM_B2 = 0.999
ADAM_EPS = 1e-08
ADAM_WD = 0.01
ADAM_STEP = 10
SUBLANES = 8
VMEM_LIMIT = 56 * 1024 * 1024


def _params(*sem):
    return pltpu.CompilerParams(dimension_semantics=sem, vmem_limit_bytes=VMEM_LIMIT)


def _sigmoid(v):
    return 1.0 / (1.0 + jnp.exp(-v))


def _one_minus_sq(la, a):
    return jnp.tanh(-la) * (1.0 + a * a)


def _gelu_parts(v):
    k = 0.7978845608028654
    v2 = v * v
    t = jnp.tanh(k * (v + 0.044715 * v * v2))
    gelu = 0.5 * v * (1.0 + t)
    dgelu = 0.5 * (1.0 + t) + 0.5 * v * (1.0 - t * t) * k * (1.0 + 3 * 0.044715 * v2)
    return gelu, dgelu


def _shift_down(v, k, prev8):
    rolled = pltpu.roll(v, k, 0)
    r8 = lax.broadcasted_iota(jnp.int32, prev8.shape, 0)
    head = jnp.where(r8 < k, pltpu.roll(prev8, k, 0), rolled[0:SUBLANES])
    return jnp.concatenate([head, rolled[SUBLANES:]], axis=0)


def _shift_up(v, k, next8):
    tm = v.shape[0]
    rolled = pltpu.roll(v, tm - k, 0)
    r8 = lax.broadcasted_iota(jnp.int32, next8.shape, 0)
    tail = jnp.where(r8 >= SUBLANES - k, pltpu.roll(next8, SUBLANES - k, 0), rolled[tm - SUBLANES:])
    return jnp.concatenate([rolled[:tm - SUBLANES], tail], axis=0)


def _group_scan(a, b, reverse):
    tm, c = a.shape
    a = a.reshape(tm // SUBLANES, SUBLANES, c)
    b = b.reshape(tm // SUBLANES, SUBLANES, c)
    q = lax.broadcasted_iota(jnp.int32, a.shape, 1)
    for s in (1, 2, 4):
        msk = q < SUBLANES - s if reverse else q >= s
        shift = SUBLANES - s if reverse else s
        b = jnp.where(msk, a * pltpu.roll(b, shift, 1) + b, b)
        a = jnp.where(msk, a * pltpu.roll(a, shift, 1), a)
    return a.reshape(tm, c), b.reshape(tm, c)


def _colsum8(v):
    tm, c = v.shape
    return jnp.sum(v.reshape(tm // SUBLANES, SUBLANES, c), axis=0)


def _rms_stats(xv):
    var = jnp.mean(xv * xv, axis=-1, keepdims=True)
    return lax.rsqrt(var + RMS_EPS)


def _rms_bwd(dh, xv, g):
    rstd = _rms_stats(xv)
    xhat = xv * rstd
    dxhat = dh * g
    dx = rstd * (dxhat - xhat * jnp.mean(dxhat * xhat, axis=-1, keepdims=True))
    return dx, dh * xhat


ANY = pl.BlockSpec(memory_space=pl.ANY)


def _place():
    x, y, c = lax.axis_index("x"), lax.axis_index("y"), lax.axis_index("c")
    other_chips = [(1 - x, y), (x, 1 - y), (1 - x, 1 - y)]
    return x, y, c, other_chips


def _chip_id(x, y):
    return 2 * x + y


def _half(c, hk):
    return pl.ds(pl.multiple_of(c * hk, 16), hk)


def _remote(src, dst, to, sems):
    return pltpu.make_async_remote_copy(src_ref=src, dst_ref=dst, device_id=to, device_id_type=MESH, **sems)


class _Carried:
    def __init__(self):
        self.ro, self.io, self.nw, self.parts, self.n = [], [], [], [], 0

    def add(self, maker, n, ro=(), io=(), nw=()):
        def index(items, item, same):
            for k, other in enumerate(items):
                if same(other, item):
                    return k
            items.append(item)
            return len(items) - 1

        r = [index(self.ro, a, lambda p, q: p is q) for a in ro]
        i = [index(self.io, a, lambda p, q: p[0] is q[0] and p[1] == q[1]) for a in io]
        w = [index(self.nw, a, lambda p, q: False) for a in nw]
        self.parts.append((maker, r, i, w, self.n))
        self.n += n
        return self

    def pairs(self, ro, io, nw, ssem, rsem):
        out = []
        for maker, r, i, w, base in self.parts:
            sems = lambda k, base=base: dict(send_sem=ssem.at[base + k], recv_sem=rsem.at[base + k])
            out += maker([ro[k] for k in r], [io[k] for k in i], [nw[k] for k in w], sems)
        return out

    def start(self, *refs):
        for send, _ in self.pairs(*refs):
            send.start()

    def finish(self, *refs):
        pairs = self.pairs(*refs)
        for _, recv in pairs:
            recv.wait_recv()
        for send, _ in pairs:
            send.wait_send()

    def operands(self):
        return list(self.ro) + [store[key] for store, key in self.io]

    def out_shapes(self):
        return [jax.ShapeDtypeStruct(store[key].shape, store[key].dtype) for store, key in self.io] + [s for _, _, s in self.nw]

    def keep(self, results):
        for (store, key), arr in zip(self.io, results[:len(self.io)]):
            store[key] = arr
        for (store, key, _), arr in zip(self.nw, results[len(self.io):]):
            store[key] = arr


def _call(body, comm, *, name, grid, in_specs, out_specs, out_shape, compiler_params, scratch_shapes=(), aliases=None):
    aliases = dict(aliases or {})
    if comm is None or not comm.parts:
        return pl.pallas_call(body, name=name, grid=grid, in_specs=in_specs, out_specs=out_specs, out_shape=out_shape,
                              scratch_shapes=list(scratch_shapes), input_output_aliases=aliases, compiler_params=compiler_params)
    n_in, n_out, n_scr = len(in_specs), len(out_shape), len(scratch_shapes)
    n_ro, n_io, n_nw = len(comm.ro), len(comm.io), len(comm.nw)

    def carried(*refs):
        base_in = refs[:n_in]
        ro = refs[n_in:n_in + n_ro]
        pos = n_in + n_ro + n_io
        base_out = refs[pos:pos + n_out]
        io = refs[pos + n_out:pos + n_out + n_io]
        nw = refs[pos + n_out + n_io:pos + n_out + n_io + n_nw]
        pos += n_out + n_io + n_nw
        scr = refs[pos:pos + n_scr]
        ssem, rsem = refs[pos + n_scr], refs[pos + n_scr + 1]
        first = pl.program_id(0) == 0
        last = pl.program_id(0) == grid[0] - 1
        for axis in range(1, len(grid)):
            first = first & (pl.program_id(axis) == 0)
            last = last & (pl.program_id(axis) == grid[axis] - 1)

        @pl.when(first)
        def _():
            comm.start(ro, io, nw, ssem, rsem)

        body(*base_in, *base_out, *scr)

        @pl.when(last)
        def _():
            comm.finish(ro, io, nw, ssem, rsem)

    aliases.update({n_in + n_ro + k: n_out + k for k in range(n_io)})
    dma = pltpu.SemaphoreType.DMA
    call = pl.pallas_call(
        carried, name=name, grid=grid,
        in_specs=list(in_specs) + [ANY] * (n_ro + n_io), out_specs=list(out_specs) + [ANY] * (n_io + n_nw),
        out_shape=list(out_shape) + comm.out_shapes(), input_output_aliases=aliases,
        scratch_shapes=list(scratch_shapes) + [dma((comm.n,)), dma((comm.n,))], compiler_params=compiler_params)

    def run(*operands):
        res = call(*operands, *comm.operands())
        comm.keep(res[n_out:])
        return res[:n_out]

    return run


def _run_comm(rounds, name):
    ro, io, nw, uses = [], [], [], []
    for r in rounds:
        def index(items, item, same):
            for k, other in enumerate(items):
                if same(other, item):
                    return k
            items.append(item)
            return len(items) - 1
        uses.append(([index(ro, a, lambda p, q: p is q) for a in r.ro],
                     [index(io, a, lambda p, q: p[0] is q[0] and p[1] == q[1]) for a in r.io],
                     [index(nw, a, lambda p, q: False) for a in r.nw]))
    n_ro, n_io, n_nw = len(ro), len(io), len(nw)

    def body(*refs):
        ro_refs = refs[:n_ro]
        io_refs = refs[n_ro + n_io:n_ro + 2 * n_io]
        nw_refs = refs[n_ro + 2 * n_io:n_ro + 2 * n_io + n_nw]
        sems = refs[n_ro + 2 * n_io + n_nw:]
        for k, (r, (a, b, c)) in enumerate(zip(rounds, uses)):
            args = ([ro_refs[i] for i in a], [io_refs[i] for i in b], [nw_refs[i] for i in c], sems[2 * k], sems[2 * k + 1])
            r.start(*args)
            r.finish(*args)

    operands = ro + [store[key] for store, key in io]
    out_shape = [jax.ShapeDtypeStruct(store[key].shape, store[key].dtype) for store, key in io] + [s for _, _, s in nw]
    dma = pltpu.SemaphoreType.DMA
    res = pl.pallas_call(
        body, name=name, out_shape=out_shape,
        in_specs=[ANY] * (n_ro + n_io), out_specs=[ANY] * (n_io + n_nw),
        input_output_aliases={n_ro + k: k for k in range(n_io)},
        scratch_shapes=[dma((r.n,)) for r in rounds for _ in range(2)],
    )(*operands)
    for (store, key), arr in zip(io, res[:n_io]):
        store[key] = arr
    for (store, key, _), arr in zip(nw, res[n_io:]):
        store[key] = arr


HBM = pl.BlockSpec(memory_space=pltpu.HBM)
SEM = pl.BlockSpec(memory_space=pltpu.SEMAPHORE)
SPLIT_COPY = pltpu.CompilerParams(has_side_effects=pltpu.SideEffectType.DATAFLOW_SIDE_EFFECTING)


def _first_gather_copies(slot, send_sems, recv_sems):
    x, y, c, chips = _place()
    hk = slot.shape[1] // 2
    mine = slot.at[_chip_id(x, y), _half(c, hk)]
    pairs = []
    for j, (cx, cy) in enumerate(chips):
        theirs = slot.at[_chip_id(cx, cy), _half(c, hk)]
        sems = dict(send_sem=send_sems.at[j], recv_sem=recv_sems.at[j])
        pairs.append((_remote(mine, mine, (cx, cy, c), sems), _remote(theirs, theirs, (cx, cy, c), sems)))
    return pairs


def _gather_first_start(slot):
    def body(slot_ref, send_sems, recv_sems, slot_thru, token):
        for send, _ in _first_gather_copies(slot_ref, send_sems, recv_sems):
            send.start()
        token[...] = jnp.zeros_like(token)

    dma = pltpu.SemaphoreType.DMA
    return pl.pallas_call(
        body, name="gather_first_start",
        out_shape=(dma((3,)), dma((3,)), pltpu.HBM(slot.shape, slot.dtype), jax.ShapeDtypeStruct((SUBLANES, 128), jnp.int32)),
        in_specs=(HBM,), out_specs=(SEM, SEM, HBM, pl.BlockSpec(memory_space=pltpu.VMEM)), input_output_aliases={0: 2},
        compiler_params=SPLIT_COPY,
    )(pltpu.with_memory_space_constraint(slot, pltpu.HBM))


def _gather_first_wait(send_sems, recv_sems, slot, after):
    def body(slot_ref, send_sems, recv_sems, *rest):
        for send, recv in _first_gather_copies(slot_ref, send_sems, recv_sems):
            send.wait_send()
            recv.wait_recv()

    return pl.pallas_call(
        body, name="gather_first_wait", out_shape=(pltpu.HBM(slot.shape, slot.dtype),),
        in_specs=(HBM, SEM, SEM) + (ANY,) * len(after), out_specs=(HBM,), input_output_aliases={0: 0},
        compiler_params=SPLIT_COPY,
    )(slot, send_sems, recv_sems, *after)[0]


def _scatter_copies(src, land, send_sems, recv_sems):
    x, y, c, chips = _place()
    return [_remote(src.at[_chip_id(cx, cy)], land.at[j], (cx, cy, c), dict(send_sem=send_sems.at[j], recv_sem=recv_sems.at[j]))
            for j, (cx, cy) in enumerate(chips)]


def _scatter_start(src):
    land = pltpu.with_memory_space_constraint(lax.empty((3,) + src.shape[1:], src.dtype), pltpu.HBM)

    def body(src_ref, land_ref, send_sems, recv_sems, src_thru, land_thru, token):
        for cp in _scatter_copies(src_ref, land_ref, send_sems, recv_sems):
            cp.start()
        token[...] = jnp.zeros_like(token)

    dma = pltpu.SemaphoreType.DMA
    return pl.pallas_call(
        body, name="reduce_chips_start",
        out_shape=(dma((3,)), dma((3,)), pltpu.HBM(src.shape, src.dtype), pltpu.HBM(land.shape, land.dtype),
                   jax.ShapeDtypeStruct((SUBLANES, 128), F32)),
        in_specs=(HBM, HBM), out_specs=(SEM, SEM, HBM, HBM, pl.BlockSpec(memory_space=pltpu.VMEM)),
        input_output_aliases={0: 2, 1: 3}, compiler_params=SPLIT_COPY,
    )(pltpu.with_memory_space_constraint(src, pltpu.HBM), land)


def _scatter_wait(send_sems, recv_sems, src, land, after):
    def body(src_ref, land_ref, send_sems, recv_sems, after_ref, src_done, land_done):
        for cp in _scatter_copies(src_ref, land_ref, send_sems, recv_sems):
            cp.wait_send()
            cp.wait_recv()

    return pl.pallas_call(
        body, name="reduce_chips_wait", out_shape=(pltpu.HBM(src.shape, src.dtype), pltpu.HBM(land.shape, land.dtype)),
        in_specs=(HBM, HBM, SEM, SEM, ANY), out_specs=(HBM, HBM), input_output_aliases={0: 0, 1: 1}, compiler_params=SPLIT_COPY,
    )(src, land, send_sems, recv_sems, after)[1]


def _gather_ici(ro, io, nw, sems):
    s = io[0]
    x, y, c, chips = _place()
    hk = s.shape[1] // 2
    mine = s.at[_chip_id(x, y), _half(c, hk)]
    pairs = []
    for j, (cx, cy) in enumerate(chips):
        theirs = s.at[_chip_id(cx, cy), _half(c, hk)]
        pairs.append((_remote(mine, mine, (cx, cy, c), sems(j)), _remote(theirs, theirs, (cx, cy, c), sems(j))))
    return pairs


def _gather_d2d(ro, io, nw, sems):
    s = io[0]
    x, y, c, chips = _place()
    hk = s.shape[1] // 2
    sib = (x, y, 1 - c)
    pairs = []
    for j, (cx, cy) in enumerate(chips):
        here = s.at[_chip_id(cx, cy), _half(c, hk)]
        there = s.at[_chip_id(cx, cy), _half(1 - c, hk)]
        pairs.append((_remote(here, here, sib, sems(j)), _remote(there, there, sib, sems(j))))
    return pairs


def _gather_small(ro, io, nw, sems):
    x, y, c, chips = _place()
    return [(_remote(ro[0], nw[0].at[j], (cx, cy, c), sems(j)),) * 2 for j, (cx, cy) in enumerate(chips)]


def _reduce_halves(ro, io, nw, sems):
    x, y, c, _ = _place()
    g = ro[0]
    hk = g.shape[1] // 2
    sib = (x, y, 1 - c)
    return [(_remote(g.at[:, _half(1 - c, hk)], nw[0], sib, sems(0)), _remote(g.at[:, _half(c, hk)], nw[0], sib, sems(0)))]


def _reduce_chips(ro, io, nw, sems):
    x, y, c, chips = _place()
    return [(_remote(ro[0].at[_chip_id(cx, cy)], nw[0].at[j], (cx, cy, c), sems(j)),) * 2 for j, (cx, cy) in enumerate(chips)]


def _reduce_share(layer):
    def maker(ro, io, nw, sems):
        g = io[0]
        x, y, c, _ = _place()
        hk = g.shape[1] // 2
        sib = (x, y, 1 - c)
        mine, theirs = g.at[layer, _half(c, hk)], g.at[layer, _half(1 - c, hk)]
        return [(_remote(mine, mine, sib, sems(0)), _remote(theirs, theirs, sib, sems(0)))]
    return maker


def _own_chip():
    return _chip_id(lax.axis_index("x"), lax.axis_index("y"))


def _other_chip(j):
    x, y = lax.axis_index("x"), lax.axis_index("y")
    return _chip_id(jnp.where(j == 1, x, 1 - x), jnp.where(j == 0, y, 1 - y))


def _rms_inproj_own(x, g_row, win, layer, tm):
    T, D = x.shape
    ns = win.shape[-1]

    def body(x_ref, g_ref, w_ref, p_ref, h_ref):
        xv = x_ref[...]
        h = (xv * _rms_stats(xv) * g_ref[...]).astype(BF16)
        h_ref[...] = h
        p_ref[...] = jnp.dot(h, w_ref[...], preferred_element_type=F32).astype(BF16)

    return pl.pallas_call(
        body, name=f"rms_inproj_own_{layer}", grid=(T // tm,),
        in_specs=[pl.BlockSpec((tm, D), lambda i: (i, 0)),
                  pl.BlockSpec((1, D), lambda i: (0, 0)),
                  pl.BlockSpec((None, D, ns), lambda i: (_own_chip(), 0, 0))],
        out_specs=[pl.BlockSpec((tm, ns), lambda i: (i, _own_chip())),
                   pl.BlockSpec((tm, D), lambda i: (i, 0))],
        out_shape=[jax.ShapeDtypeStruct((T, N_CHIP * ns), BF16), jax.ShapeDtypeStruct((T, D), BF16)],
        compiler_params=_params("parallel"),
    )(x, g_row, win)


def _inproj_rest(h, win, p, layer, tm, comm=None):
    T, D = h.shape
    ns = win.shape[-1]

    def body(h_ref, w_ref, p_in, p_ref):
        p_ref[...] = jnp.dot(h_ref[...], w_ref[...], preferred_element_type=F32).astype(BF16)

    return _call(
        body, comm, name=f"rms_inproj_{layer}", grid=(T // tm, N_CHIP - 1),
        in_specs=[pl.BlockSpec((tm, D), lambda i, j: (i, 0)),
                  pl.BlockSpec((None, D, ns), lambda i, j: (_other_chip(j), 0, 0)), ANY],
        out_specs=[pl.BlockSpec((tm, ns), lambda i, j: (i, _other_chip(j)))],
        out_shape=[jax.ShapeDtypeStruct(p.shape, p.dtype)], aliases={2: 0},
        compiler_params=_params("parallel", "arbitrary"),
    )(h, win, p)[0]


def _mixer_recompute(ca, xa, xb, zprev8, xbprev8, cw_ref, wab_ref, sp):
    row = lambda k: cw_ref[pl.ds(k, 1), :]
    z = ca * xa
    z1 = _shift_down(z, 1, zprev8)
    z2 = _shift_down(z, 2, zprev8)
    cz = row(2) * z + row(1) * z1 + row(0) * z2
    x1 = _shift_down(xb, 1, xbprev8)
    x2 = _shift_down(xb, 2, xbprev8)
    x3 = _shift_down(xb, 3, xbprev8)
    u = row(6) * xb + row(5) * x1 + row(4) * x2 + row(3) * x3 + row(7)
    ub = u.astype(BF16)
    nb = wab_ref.shape[0]
    ras, ixs = [], []
    for b in range(nb):
        ri = jnp.dot(ub[:, b * LRU_BLOCK:(b + 1) * LRU_BLOCK], wab_ref[b], preferred_element_type=F32)
        ras.append(ri[:, :LRU_BLOCK])
        ixs.append(ri[:, LRU_BLOCK:])
    r = _sigmoid(jnp.concatenate(ras, axis=1) + row(8))
    gi = _sigmoid(jnp.concatenate(ixs, axis=1) + row(9))
    la = (-LRU_C) * r * sp
    a = jnp.exp(la)
    m = jnp.sqrt(_one_minus_sq(la, a))
    return dict(z=z, z1=z1, z2=z2, cz=cz, x1=x1, x2=x2, x3=x3, u=u, ub=ub, r=r, gi=gi, a=a, m=m)


def _softplus_neg(lam):
    v = -lam
    return jnp.maximum(v, 0.0) + jnp.log1p(jnp.exp(-jnp.abs(v)))


def _mixer_fwd(p, cw, wab, lam_row, layer, tm, comm=None):
    T = p.shape[0]
    D = p.shape[1] // 7
    ngroups = tm // SUBLANES

    def body(ba_ref, ca_ref, xa_ref, xb_ref, gb_ref, cw_ref, wab_ref, lam_ref, ya_ref, yb_ref, h_ref,
             zprev, xbprev, hcarry, a_s, h_s):
        @pl.when(pl.program_id(0) == 0)
        def _():
            zprev[...] = jnp.zeros_like(zprev)
            xbprev[...] = jnp.zeros_like(xbprev)
            hcarry[...] = jnp.zeros_like(hcarry)

        ca = ca_ref[...].astype(F32)
        xa = xa_ref[...].astype(F32)
        xb = xb_ref[...].astype(F32)
        sp = _softplus_neg(lam_ref[...])
        c = _mixer_recompute(ca, xa, xb, zprev[...], xbprev[...], cw_ref, wab_ref, sp)
        zprev[...] = c["z"][tm - SUBLANES:]
        xbprev[...] = xb[tm - SUBLANES:]
        ya_ref[...] = (ba_ref[...].astype(F32) * c["cz"]).astype(BF16)

        a_s[...], h_s[...] = _group_scan(c["a"], c["m"] * c["gi"] * c["u"], reverse=False)

        def step(g, carry):
            off = pl.multiple_of(g * SUBLANES, SUBLANES)
            hg = h_s[pl.ds(off, SUBLANES), :] + a_s[pl.ds(off, SUBLANES), :] * carry
            h_s[pl.ds(off, SUBLANES), :] = hg
            return jnp.broadcast_to(hg[SUBLANES - 1:SUBLANES, :], hg.shape)

        hcarry[...] = lax.fori_loop(0, ngroups, step, hcarry[...], unroll=4)
        h = h_s[...]
        h_ref[...] = h
        gelu, _ = _gelu_parts(gb_ref[...].astype(F32))
        yb_ref[...] = (h * gelu).astype(BF16)

    col = lambda k: pl.BlockSpec((tm, D), lambda i: (i, k))
    full = lambda a: pl.BlockSpec(a.shape, lambda i: (0,) * a.ndim)
    tok = pl.BlockSpec((tm, D), lambda i: (i, 0))
    return _call(
        body, comm, name=f"mixer_fwd_{layer}", grid=(T // tm,),
        in_specs=[col(0), col(1), col(2), col(3), col(4), full(cw), full(wab), full(lam_row)],
        out_specs=[tok, tok, tok],
        out_shape=[jax.ShapeDtypeStruct((T, D), BF16), jax.ShapeDtypeStruct((T, D), BF16), jax.ShapeDtypeStruct((T, D), F32)],
        scratch_shapes=[pltpu.VMEM((SUBLANES, D), F32), pltpu.VMEM((SUBLANES, D), F32), pltpu.VMEM((SUBLANES, D), F32),
                        pltpu.VMEM((tm, D), F32), pltpu.VMEM((tm, D), F32)],
        compiler_params=_params("arbitrary"),
    )(p, p, p, p, p, cw, wab, lam_row)


def _merge_fwd(x, p, ya, yb, woa, wob, wo, gbias, layer, tm, comm=None):
    T, D = x.shape

    def body(x_ref, ga_ref, gb_ref, ya_ref, yb_ref, woa_ref, wob_ref, wo_ref, bias_ref, oa_ref, ob_ref, mg_ref, x1_ref):
        oa = jnp.dot(ya_ref[...], woa_ref[...], preferred_element_type=F32)
        ob = jnp.dot(yb_ref[...], wob_ref[...], preferred_element_type=F32)
        sa = _sigmoid(ga_ref[...].astype(F32) + bias_ref[pl.ds(0, 1), :])
        sb = _sigmoid(gb_ref[...].astype(F32) + bias_ref[pl.ds(1, 1), :])
        mg = (sa * oa + sb * ob).astype(BF16)
        oa_ref[...] = oa.astype(BF16)
        ob_ref[...] = ob.astype(BF16)
        mg_ref[...] = mg
        x1_ref[...] = x_ref[...] + jnp.dot(mg, wo_ref[...], preferred_element_type=F32)

    tok = pl.BlockSpec((tm, D), lambda i: (i, 0))
    wsp = pl.BlockSpec((D, D), lambda i: (0, 0))
    bf = jax.ShapeDtypeStruct((T, D), BF16)
    return _call(
        body, comm, name=f"merge_fwd_{layer}", grid=(T // tm,),
        in_specs=[tok, pl.BlockSpec((tm, D), lambda i: (i, 5)), pl.BlockSpec((tm, D), lambda i: (i, 6)), tok, tok,
                  wsp, wsp, wsp, pl.BlockSpec(gbias.shape, lambda i: (0, 0))],
        out_specs=[tok, tok, tok, tok],
        out_shape=[bf, bf, bf, jax.ShapeDtypeStruct((T, D), F32)],
        compiler_params=_params("parallel"),
    )(x, p, p, ya, yb, woa, wob, wo, gbias)


def _loss_tile(xv, g, tgt):
    d = xv.shape[-1]
    rstd = _rms_stats(xv)
    xhat = xv * rstd
    err = xhat * g - tgt
    dy = err * (1.0 / d)
    dxhat = dy * g
    dx = rstd * (dxhat - xhat * jnp.mean(dxhat * xhat, axis=-1, keepdims=True))
    return dx, _colsum8(err * err), _colsum8(dy * xhat)


def _ffn_fwd(x1, g_row, wg, wu, wd, layer, tm, comm=None):
    T, D = x1.shape
    fs = wg.shape[-2]
    n = T // tm
    nt = (((1,), (1,)), ((), ()))

    def body(x_ref, g_ref, wg_ref, wu_ref, wd_ref, h_ref, gg_ref, uu_ref, x2_ref, acc):
        j = pl.program_id(1)

        @pl.when(j == 0)
        def _():
            xv = x_ref[...]
            h_ref[...] = (xv * _rms_stats(xv) * g_ref[...]).astype(BF16)
            acc[...] = xv

        h = h_ref[...]
        gg = lax.dot_general(h, wg_ref[...], nt, preferred_element_type=F32)
        uu = lax.dot_general(h, wu_ref[...], nt, preferred_element_type=F32)
        gg_ref[...] = gg.astype(BF16)
        uu_ref[...] = uu.astype(BF16)
        act = (gg * _sigmoid(gg) * uu).astype(BF16)
        acc[...] += jnp.dot(act, wd_ref[...], preferred_element_type=F32)

        @pl.when(j == N_CHIP - 1)
        def _():
            x2_ref[...] = acc[...]

    tok = pl.BlockSpec((tm, D), lambda i, j: (i, 0))
    cm = pl.BlockSpec((None, tm, fs), lambda i, j: (j, i, 0))
    wsp = pl.BlockSpec((None, fs, D), lambda i, j: (j, 0, 0))
    return _call(
        body, comm, name=f"ffn_fwd_{layer}", grid=(n, N_CHIP),
        in_specs=[tok, pl.BlockSpec((1, D), lambda i, j: (0, 0)), wsp, wsp, wsp], out_specs=[tok, cm, cm, tok],
        out_shape=[jax.ShapeDtypeStruct((T, D), BF16), jax.ShapeDtypeStruct((N_CHIP, T, fs), BF16),
                   jax.ShapeDtypeStruct((N_CHIP, T, fs), BF16), jax.ShapeDtypeStruct((T, D), F32)],
        scratch_shapes=[pltpu.VMEM((tm, D), F32)], compiler_params=_params("parallel", "arbitrary"),
    )(x1, g_row, wg, wu, wd)


def _final_loss(x, g_row, target, tm):
    T, D = x.shape
    n = T // tm

    def body(x_ref, g_ref, t_ref, dx_ref, red_ref, racc):
        i = pl.program_id(0)

        @pl.when(i == 0)
        def _():
            racc[...] = jnp.zeros_like(racc)

        dx_ref[...], sq, dg = _loss_tile(x_ref[...], g_ref[...], t_ref[...])
        racc[0] += sq
        racc[1] += dg

        @pl.when(i == n - 1)
        def _():
            red_ref[pl.ds(0, 1), :] = jnp.sum(racc[0], axis=0, keepdims=True) * (0.5 / D)
            red_ref[pl.ds(1, 1), :] = jnp.sum(racc[1], axis=0, keepdims=True)

    tok = pl.BlockSpec((tm, D), lambda i: (i, 0))
    return pl.pallas_call(
        body, name="final_loss", grid=(n,),
        in_specs=[tok, pl.BlockSpec((1, D), lambda i: (0, 0)), tok],
        out_specs=[tok, pl.BlockSpec((2, D), lambda i: (0, 0))],
        out_shape=[jax.ShapeDtypeStruct((T, D), F32), jax.ShapeDtypeStruct((2, D), F32)],
        scratch_shapes=[pltpu.VMEM((2, SUBLANES, D), F32)],
        compiler_params=_params("arbitrary"),
    )(x, g_row, target)


def _ffn_bwd_gates(dx2, gg, uu, wd, layer, tm):
    T, D = dx2.shape
    fs = wd.shape[-2]
    nt = (((1,), (1,)), ((), ()))

    def body(dx_ref, gg_ref, uu_ref, wd_ref, dg_ref, du_ref, act_ref, dxb_ref):
        @pl.when(pl.program_id(1) == 0)
        def _():
            dxb_ref[...] = dx_ref[...].astype(BF16)

        dact = lax.dot_general(dxb_ref[...], wd_ref[...], nt, preferred_element_type=F32)
        g = gg_ref[...].astype(F32)
        u = uu_ref[...].astype(F32)
        s = _sigmoid(g)
        silu = g * s
        dg_ref[...] = (dact * u * (s * (1.0 + g * (1.0 - s)))).astype(BF16)
        du_ref[...] = (dact * silu).astype(BF16)
        act_ref[...] = (silu * u).astype(BF16)

    tok = pl.BlockSpec((tm, D), lambda i, j: (i, 0))
    cm = pl.BlockSpec((None, tm, fs), lambda i, j: (j, i, 0))
    cms = jax.ShapeDtypeStruct((N_CHIP, T, fs), BF16)
    return pl.pallas_call(
        body, name=f"ffn_bwd_gates_{layer}", grid=(T // tm, N_CHIP),
        in_specs=[tok, cm, cm, pl.BlockSpec((None, fs, D), lambda i, j: (j, 0, 0))], out_specs=[cm, cm, cm, tok],
        out_shape=[cms, cms, cms, jax.ShapeDtypeStruct((T, D), BF16)],
        compiler_params=_params("parallel", "arbitrary"),
    )(dx2, gg, uu, wd)


def _ffn_bwd(dgg, duu, wg, wu, dx2, x1, g_row, layer, tm, comm=None):
    T, D = dx2.shape
    fs = wg.shape[-2]
    n = T // tm

    def body(dg_ref, du_ref, wg_ref, wu_ref, dx_ref, x_ref, g_ref, dx1_ref, red_ref, acc, racc):
        i = pl.program_id(0)
        j = pl.program_id(1)

        @pl.when((i == 0) & (j == 0))
        def _():
            racc[...] = jnp.zeros_like(racc)

        @pl.when(j == 0)
        def _():
            acc[...] = jnp.zeros_like(acc)

        acc[...] += (jnp.dot(dg_ref[...], wg_ref[...], preferred_element_type=F32)
                     + jnp.dot(du_ref[...], wu_ref[...], preferred_element_type=F32))

        @pl.when(j == N_CHIP - 1)
        def _():
            dx, dgain = _rms_bwd(acc[...], x_ref[...], g_ref[...])
            dx1_ref[...] = dx_ref[...] + dx
            racc[...] += _colsum8(dgain)

        @pl.when((i == n - 1) & (j == N_CHIP - 1))
        def _():
            red_ref[...] = jnp.sum(racc[...], axis=0, keepdims=True)

    tok = pl.BlockSpec((tm, D), lambda i, j: (i, 0))
    cm = pl.BlockSpec((None, tm, fs), lambda i, j: (j, i, 0))
    wsp = pl.BlockSpec((None, fs, D), lambda i, j: (j, 0, 0))
    row = pl.BlockSpec((1, D), lambda i, j: (0, 0))
    return _call(
        body, comm, name=f"ffn_bwd_{layer}", grid=(n, N_CHIP),
        in_specs=[cm, cm, wsp, wsp, tok, tok, row], out_specs=[tok, row],
        out_shape=[jax.ShapeDtypeStruct((T, D), F32), jax.ShapeDtypeStruct((1, D), F32)],
        scratch_shapes=[pltpu.VMEM((tm, D), F32), pltpu.VMEM((SUBLANES, D), F32)],
        compiler_params=_params("arbitrary", "arbitrary"),
    )(dgg, duu, wg, wu, dx2, x1, g_row)


def _merge_bwd(dx1, p, oa, ob, woa, wob, wo, gbias, layer, tm, comm=None):
    T, D = dx1.shape
    n = T // tm
    nt = (((1,), (1,)), ((), ()))

    def body(dx_ref, ga_ref, gb_ref, oa_ref, ob_ref, woa_ref, wob_ref, wo_ref, bias_ref,
             dya_ref, dyb_ref, doa_ref, dob_ref, dgl_ref, dxb_ref, red_ref, racc):
        i = pl.program_id(0)

        @pl.when(i == 0)
        def _():
            racc[...] = jnp.zeros_like(racc)

        dxb = dx_ref[...].astype(BF16)
        dxb_ref[...] = dxb
        dm = lax.dot_general(dxb, wo_ref[...], nt, preferred_element_type=F32)
        sa = _sigmoid(ga_ref[...].astype(F32) + bias_ref[pl.ds(0, 1), :])
        sb = _sigmoid(gb_ref[...].astype(F32) + bias_ref[pl.ds(1, 1), :])
        doa = (dm * sa).astype(BF16)
        dob = (dm * sb).astype(BF16)
        dga = dm * oa_ref[...].astype(F32) * (sa * (1.0 - sa))
        dgb = dm * ob_ref[...].astype(F32) * (sb * (1.0 - sb))
        doa_ref[...] = doa
        dob_ref[...] = dob
        dgl_ref[:, 0:D] = dga.astype(BF16)
        dgl_ref[:, D:2 * D] = dgb.astype(BF16)
        racc[0] += _colsum8(dga)
        racc[1] += _colsum8(dgb)
        dya_ref[...] = lax.dot_general(doa, woa_ref[...], nt, preferred_element_type=F32).astype(BF16)
        dyb_ref[...] = lax.dot_general(dob, wob_ref[...], nt, preferred_element_type=F32).astype(BF16)

        @pl.when(i == n - 1)
        def _():
            red_ref[pl.ds(0, 1), :] = jnp.sum(racc[0], axis=0, keepdims=True)
            red_ref[pl.ds(1, 1), :] = jnp.sum(racc[1], axis=0, keepdims=True)

    tok = pl.BlockSpec((tm, D), lambda i: (i, 0))
    wsp = pl.BlockSpec((D, D), lambda i: (0, 0))
    bf = jax.ShapeDtypeStruct((T, D), BF16)
    return _call(
        body, comm, name=f"merge_bwd_{layer}", grid=(n,),
        in_specs=[tok, pl.BlockSpec((tm, D), lambda i: (i, 5)), pl.BlockSpec((tm, D), lambda i: (i, 6)), tok, tok,
                  wsp, wsp, wsp, pl.BlockSpec(gbias.shape, lambda i: (0, 0))],
        out_specs=[tok, tok, tok, tok, pl.BlockSpec((tm, 2 * D), lambda i: (i, 0)), tok, pl.BlockSpec((2, D), lambda i: (0, 0))],
        out_shape=[bf, bf, bf, bf, jax.ShapeDtypeStruct((T, 2 * D), BF16), bf, jax.ShapeDtypeStruct((2, D), F32)],
        scratch_shapes=[pltpu.VMEM((2, SUBLANES, D), F32)],
        compiler_params=_params("arbitrary"),
    )(dx1, p, p, oa, ob, woa, wob, wo, gbias)


N_MIXER_RED = 16


def _mixer_bwd(p, hseq, dya, dyb, dgl, cw, wab, wabt, lam_row, layer, tm, comm=None):
    T = p.shape[0]
    D = p.shape[1] // 7
    n = T // tm
    ngroups = tm // SUBLANES
    nb = wab.shape[0]
    hb = 16
    tn = (((0,), (0,)), ((), ()))

    def body(ba_ref, ca_ref, xa_ref, xb_ref, gb_ref, h_ref, dya_ref, dyb_ref, dgl_ref,
             cap_ref, xap_ref, xbp_ref, hp_ref, ban_ref, dyan_ref,
             cw_ref, wab_ref, wabt_ref, lam_ref,
             dp_ref, red_ref, dwab_ref,
             racc, wacc, anext, gnext, dunext, c_s, g_s):
        i = pl.program_id(0)
        first_tile = i == n - 1
        last_tile = i == 0

        @pl.when(i == 0)
        def _():
            racc[...] = jnp.zeros_like(racc)
            wacc[...] = jnp.zeros_like(wacc)
            anext[...] = jnp.zeros_like(anext)
            gnext[...] = jnp.zeros_like(gnext)
            dunext[...] = jnp.zeros_like(dunext)

        keep_prev = jnp.where(first_tile, 0.0, 1.0)
        keep_next = jnp.where(last_tile, 0.0, 1.0)
        ba = ba_ref[...].astype(F32)
        ca = ca_ref[...].astype(F32)
        xa = xa_ref[...].astype(F32)
        xb = xb_ref[...].astype(F32)
        h = h_ref[...]
        dya = dya_ref[...].astype(F32)
        dyb = dyb_ref[...].astype(F32)
        zprev8 = (cap_ref[...].astype(F32) * xap_ref[...].astype(F32))[hb - SUBLANES:] * keep_prev
        xbprev8 = xbp_ref[...].astype(F32)[hb - SUBLANES:] * keep_prev
        hprev8 = hp_ref[...] * keep_prev
        dcznext8 = (dyan_ref[...].astype(F32) * ban_ref[...].astype(F32))[:SUBLANES] * keep_next

        lam = lam_ref[...]
        sp = _softplus_neg(lam)
        c = _mixer_recompute(ca, xa, xb, zprev8, xbprev8, cw_ref, wab_ref, sp)
        row = lambda k: cw_ref[pl.ds(k, 1), :]
        a, m, r, gi, u = c["a"], c["m"], c["r"], c["gi"], c["u"]

        gelu, dgelu = _gelu_parts(gb_ref[...].astype(F32))
        dgb = dyb * h * dgelu
        c_s[...], g_s[...] = _group_scan(_shift_up(a, 1, anext[...]), dyb * gelu, reverse=True)

        def step(k, carry):
            off = pl.multiple_of((ngroups - 1 - k) * SUBLANES, SUBLANES)
            gg = g_s[pl.ds(off, SUBLANES), :] + c_s[pl.ds(off, SUBLANES), :] * carry
            g_s[pl.ds(off, SUBLANES), :] = gg
            return jnp.broadcast_to(gg[0:1, :], gg.shape)

        gnext[...] = lax.fori_loop(0, ngroups, step, gnext[...], unroll=4)
        anext[...] = a[0:SUBLANES]
        g = g_s[...]

        hprev = _shift_down(h, 1, hprev8)
        da = g * hprev
        gm = g * m
        dgi = gm * u
        du = gm * gi
        dmv = g * gi * u
        dla = a * (da - dmv * a / m)
        dra = dla * ((-LRU_C) * sp) * (r * (1.0 - r))
        dix = dgi * (gi * (1.0 - gi))
        racc[10] += _colsum8(dla * r)
        racc[8] += _colsum8(dra)
        racc[9] += _colsum8(dix)
        drab = dra.astype(BF16)
        dixb = dix.astype(BF16)
        ub = c["ub"]
        dus = []
        for b in range(nb):
            sl = slice(b * LRU_BLOCK, (b + 1) * LRU_BLOCK)
            dri = jnp.concatenate([drab[:, sl], dixb[:, sl]], axis=1)
            dus.append(jnp.dot(dri, wabt_ref[b], preferred_element_type=F32))
            wacc[b] += lax.dot_general(ub[:, sl], dri, tn, preferred_element_type=F32)
        du = du + jnp.concatenate(dus, axis=1)

        dun = dunext[...]
        du1 = _shift_up(du, 1, dun)
        du2 = _shift_up(du, 2, dun)
        du3 = _shift_up(du, 3, dun)
        dxb = row(6) * du + row(5) * du1 + row(4) * du2 + row(3) * du3
        dunext[...] = du[0:SUBLANES]
        racc[6] += _colsum8(du * xb)
        racc[5] += _colsum8(du * c["x1"])
        racc[4] += _colsum8(du * c["x2"])
        racc[3] += _colsum8(du * c["x3"])
        racc[7] += _colsum8(du)

        dba = dya * c["cz"]
        dcz = dya * ba
        dcz1 = _shift_up(dcz, 1, dcznext8)
        dcz2 = _shift_up(dcz, 2, dcznext8)
        dz = row(2) * dcz + row(1) * dcz1 + row(0) * dcz2
        racc[2] += _colsum8(dcz * c["z"])
        racc[1] += _colsum8(dcz * c["z1"])
        racc[0] += _colsum8(dcz * c["z2"])

        dp_ref[:, 0:D] = dba.astype(BF16)
        dp_ref[:, D:2 * D] = (dz * xa).astype(BF16)
        dp_ref[:, 2 * D:3 * D] = (dz * ca).astype(BF16)
        dp_ref[:, 3 * D:4 * D] = dxb.astype(BF16)
        dp_ref[:, 4 * D:5 * D] = dgb.astype(BF16)
        dp_ref[:, 5 * D:7 * D] = dgl_ref[...]

        @pl.when(i == n - 1)
        def _():
            dlam_scale = LRU_C * _sigmoid(-lam)
            for k in range(N_MIXER_RED):
                tot = jnp.sum(racc[k], axis=0, keepdims=True)
                red_ref[pl.ds(k, 1), :] = tot * dlam_scale if k == 10 else tot
            dwab_ref[...] = wacc[...]

    rt = lambda i: n - 1 - i
    col = lambda k: pl.BlockSpec((tm, D), lambda i: (rt(i), k))
    tok = pl.BlockSpec((tm, D), lambda i: (rt(i), 0))
    full = lambda a: pl.BlockSpec(a.shape, lambda i: (0,) * a.ndim)
    prev16 = lambda k: pl.BlockSpec((hb, D), lambda i: (jnp.maximum(rt(i) * (tm // hb) - 1, 0), k))
    next16 = lambda k: pl.BlockSpec((hb, D), lambda i: (jnp.minimum((rt(i) + 1) * (tm // hb), T // hb - 1), k))
    hprev = pl.BlockSpec((SUBLANES, D), lambda i: (jnp.maximum(rt(i) * ngroups - 1, 0), 0))
    return _call(
        body, comm, name=f"mixer_bwd_{layer}", grid=(n,),
        in_specs=[col(0), col(1), col(2), col(3), col(4), tok, tok, tok, pl.BlockSpec((tm, 2 * D), lambda i: (rt(i), 0)),
                  prev16(1), prev16(2), prev16(3), hprev, next16(0), next16(0),
                  full(cw), full(wab), full(wabt), full(lam_row)],
        out_specs=[pl.BlockSpec((tm, 7 * D), lambda i: (rt(i), 0)),
                   pl.BlockSpec((N_MIXER_RED, D), lambda i: (0, 0)),
                   pl.BlockSpec((nb, LRU_BLOCK, 2 * LRU_BLOCK), lambda i: (0, 0, 0))],
        out_shape=[jax.ShapeDtypeStruct((T, 7 * D), BF16), jax.ShapeDtypeStruct((N_MIXER_RED, D), F32),
                   jax.ShapeDtypeStruct((nb, LRU_BLOCK, 2 * LRU_BLOCK), F32)],
        scratch_shapes=[pltpu.VMEM((N_MIXER_RED, SUBLANES, D), F32), pltpu.VMEM((nb, LRU_BLOCK, 2 * LRU_BLOCK), F32),
                        pltpu.VMEM((SUBLANES, D), F32), pltpu.VMEM((SUBLANES, D), F32), pltpu.VMEM((SUBLANES, D), F32),
                        pltpu.VMEM((tm, D), F32), pltpu.VMEM((tm, D), F32)],
        compiler_params=_params("arbitrary"),
    )(p, p, p, p, p, hseq, dya, dyb, dgl, p, p, p, hseq, p, dya, cw, wab, wabt, lam_row)


def _inproj_bwd(dp, dx1, x, g_row, win, layer, tm, comm=None):
    T, D = x.shape
    ns = win.shape[-1]
    n = T // tm
    nt = (((1,), (1,)), ((), ()))

    def body(dp_ref, dx_ref, x_ref, g_ref, w_ref, dx0_ref, red_ref, acc, racc):
        i = pl.program_id(0)
        j = pl.program_id(1)

        @pl.when((i == 0) & (j == 0))
        def _():
            racc[...] = jnp.zeros_like(racc)

        @pl.when(j == 0)
        def _():
            acc[...] = jnp.zeros_like(acc)

        acc[...] += lax.dot_general(dp_ref[...], w_ref[...], nt, preferred_element_type=F32)

        @pl.when(j == N_CHIP - 1)
        def _():
            dx, dgain = _rms_bwd(acc[...], x_ref[...], g_ref[...])
            dx0_ref[...] = dx_ref[...] + dx
            racc[...] += _colsum8(dgain)

        @pl.when((i == n - 1) & (j == N_CHIP - 1))
        def _():
            red_ref[...] = jnp.sum(racc[...], axis=0, keepdims=True)

    tok = pl.BlockSpec((tm, D), lambda i, j: (i, 0))
    return _call(
        body, comm, name=f"inproj_bwd_{layer}", grid=(n, N_CHIP),
        in_specs=[pl.BlockSpec((tm, ns), lambda i, j: (i, j)), tok, tok, pl.BlockSpec((1, D), lambda i, j: (0, 0)),
                  pl.BlockSpec((None, D, ns), lambda i, j: (j, 0, 0))],
        out_specs=[tok, pl.BlockSpec((1, D), lambda i, j: (0, 0))],
        out_shape=[jax.ShapeDtypeStruct((T, D), F32), jax.ShapeDtypeStruct((1, D), F32)],
        scratch_shapes=[pltpu.VMEM((tm, D), F32), pltpu.VMEM((SUBLANES, D), F32)],
        compiler_params=_params("arbitrary", "arbitrary"),
    )(dp, dx1, x, g_row, win)


def _wgrad(a, b, name, tk, a_kind="whole", b_kind="whole", nj=1, comm=None):
    T = a.shape[-2]
    width = lambda v, kind: v.shape[-1] // nj if kind == "cols" else v.shape[-1]
    ka, kb = width(a, a_kind), width(b, b_kind)
    nt = T // tk
    tn = (((0,), (0,)), ((), ()))

    def spec(k, kind):
        if kind == "cm":
            return pl.BlockSpec((None, tk, k), lambda j, t: (j, t, 0))
        if kind == "cols":
            return pl.BlockSpec((tk, k), lambda j, t: (t, j))
        return pl.BlockSpec((tk, k), lambda j, t: (t, 0))

    def body(a_ref, b_ref, o_ref, ob_ref):
        t = pl.program_id(1)

        @pl.when(t == 0)
        def _():
            o_ref[...] = jnp.zeros_like(o_ref)

        o_ref[...] += lax.dot_general(a_ref[...], b_ref[...], tn, preferred_element_type=F32)

        @pl.when(t == nt - 1)
        def _():
            ob_ref[...] = o_ref[...].astype(BF16)

    o_spec = pl.BlockSpec((None, ka, kb), lambda j, t: (j, 0, 0))
    return _call(
        body, comm, name=name, grid=(nj, nt),
        in_specs=[spec(ka, a_kind), spec(kb, b_kind)], out_specs=[o_spec, o_spec],
        out_shape=[jax.ShapeDtypeStruct((nj, ka, kb), F32), jax.ShapeDtypeStruct((nj, ka, kb), BF16)],
        compiler_params=_params("parallel", "arbitrary"),
    )(a, b)


def _wgrad_pair(a, b1, b2, name, tk, comm=None):
    T, ka = a.shape
    nj, _, kb = b1.shape
    nt = T // tk
    tn = (((0,), (0,)), ((), ()))

    def body(a_ref, b1_ref, b2_ref, o1_ref, o1b_ref, o2_ref, o2b_ref):
        t = pl.program_id(1)

        @pl.when(t == 0)
        def _():
            o1_ref[...] = jnp.zeros_like(o1_ref)
            o2_ref[...] = jnp.zeros_like(o2_ref)

        av = a_ref[...]
        o1_ref[...] += lax.dot_general(b1_ref[...], av, tn, preferred_element_type=F32)
        o2_ref[...] += lax.dot_general(b2_ref[...], av, tn, preferred_element_type=F32)

        @pl.when(t == nt - 1)
        def _():
            o1b_ref[...] = o1_ref[...].astype(BF16)
            o2b_ref[...] = o2_ref[...].astype(BF16)

    b_spec = pl.BlockSpec((None, tk, kb), lambda j, t: (j, t, 0))
    o_spec = pl.BlockSpec((None, kb, ka), lambda j, t: (j, 0, 0))
    f32 = jax.ShapeDtypeStruct((nj, kb, ka), F32)
    b16 = jax.ShapeDtypeStruct((nj, kb, ka), BF16)
    return _call(
        body, comm, name=name, grid=(nj, nt),
        in_specs=[pl.BlockSpec((tk, ka), lambda j, t: (t, 0)), b_spec, b_spec], out_specs=[o_spec] * 4,
        out_shape=[f32, b16, f32, b16], compiler_params=_params("parallel", "arbitrary"),
    )(a, b1, b2)


def _block_diag(w):
    hb = LRU_BLOCK // LRU_HEAD_DIM
    nb = w.shape[0] // hb
    w4 = w.reshape(nb, hb, LRU_HEAD_DIM, LRU_HEAD_DIM)
    eye = jnp.eye(hb, dtype=w.dtype)
    return jnp.einsum("bide,ij->bidje", w4, eye).reshape(nb, LRU_BLOCK, LRU_BLOCK)


def _diag_heads(m):
    hb = LRU_BLOCK // LRU_HEAD_DIM
    nb = m.shape[0]
    m5 = m.reshape(nb, hb, LRU_HEAD_DIM, hb, LRU_HEAD_DIM)
    eye = jnp.eye(hb, dtype=m.dtype)
    return jnp.einsum("bidje,ij->bide", m5, eye).reshape(nb * hb, LRU_HEAD_DIM, LRU_HEAD_DIM)


def _tiles(T):
    cap = lambda n: min(n, T)
    return dict(inproj=cap(1024), mixer=cap(256), merge=cap(512), ffn=cap(1024), ffn_bwd=cap(1024), loss=cap(512), inproj_bwd=cap(1024),
                wgrad_in=cap(2048), wgrad=cap(2048))


class _NoSchedule:
    def carry(self, name):
        return None

    def tie(self, name, operand):
        return operand

    def between_inproj(self, layer, p, small):
        pass

    def after(self, name, result=None):
        pass

    def grad(self, key, layer, f32, b16):
        pass


def _local_step(x, target, W, small, tiles, sched):
    L = small["ln1_g"].shape[0]
    D = x.shape[1]
    square = lambda a: a.reshape(D, D)
    saved = []
    h = x
    for l in range(L):
        ln1_row = small["ln1_g"][l][None]
        ln2_row = small["ln2_g"][l][None]
        p, h1 = _rms_inproj_own(h, sched.tie(f"rms_inproj_own_{l}", ln1_row), W["win", l], l, tiles["inproj"])
        sched.between_inproj(l, p, small)
        p = _inproj_rest(h1, W["win", l], p, l, tiles["inproj"], sched.carry(f"rms_inproj_{l}"))
        cw = jnp.concatenate([small["conv_a_w"][l], small["conv_b_w"][l], small["conv_b_b"][l][None],
                              small["lru_ba"][l][None], small["lru_bx"][l][None]], axis=0)
        wab = jnp.concatenate([_block_diag(small["lru_wa"][l]), _block_diag(small["lru_wx"][l])], axis=2).astype(BF16)
        wabt = jnp.swapaxes(wab, 1, 2)
        lam_row = small["lru_lambda"][l][None]
        ya, yb, hseq = _mixer_fwd(p, cw, wab, lam_row, l, tiles["mixer"], sched.carry(f"mixer_fwd_{l}"))
        oa, ob, mg, x1 = _merge_fwd(h, p, ya, yb, square(W["woa", l]), square(W["wob", l]), square(W["wo", l]),
                                    small["gate_bias"][l], l, tiles["merge"], sched.carry(f"merge_fwd_{l}"))
        h2, gg, uu, x2 = _ffn_fwd(x1, ln2_row, W["wg", l], W["wu", l], W["wd", l], l, tiles["ffn"], sched.carry(f"ffn_fwd_{l}"))
        saved.append(dict(x0=h, p=p, h1=h1, ya=ya, yb=yb, hseq=hseq, oa=oa, ob=ob, mg=mg, x1=x1, h2=h2, gg=gg, uu=uu,
                          cw=cw, wab=wab, wabt=wabt, lam_row=lam_row, ln1_row=ln1_row, ln2_row=ln2_row))
        h = x2

    dx, red = _final_loss(h, small["final_g"][None], target, tiles["loss"])
    loss_row, d_final_g = red[0], red[1]

    gsmall = {k: [None] * L for k in ("ln1_g", "ln2_g", "conv_a_w", "conv_b_w", "conv_b_b", "lru_wa", "lru_ba", "lru_wx",
                                      "lru_bx", "lru_lambda", "gate_bias")}
    tk = tiles["wgrad"]
    for l in reversed(range(L)):
        s = saved[l]
        dgg, duu, act, dx2b = _ffn_bwd_gates(dx, s["gg"], s["uu"], W["wd", l], l, tiles["ffn_bwd"])
        dx1, dln2 = _ffn_bwd(dgg, duu, W["wg", l], W["wu", l], dx, s["x1"], s["ln2_row"], l, tiles["ffn_bwd"],
                             sched.carry(f"ffn_bwd_{l}"))
        sched.after(f"ffn_bwd_{l}")
        gate_up = _wgrad_pair(s["h2"], dgg, duu, f"wgrad_ffn_gate_up_{l}", tk, sched.carry(f"wgrad_ffn_gate_up_{l}"))
        sched.grad("wg", l, *gate_up[0:2])
        sched.grad("wu", l, *gate_up[2:4])
        sched.after(f"wgrad_ffn_gate_up_{l}")
        sched.grad("wd", l, *_wgrad(act, dx2b, f"wgrad_ffn_down_{l}", tk, "cm", "whole", N_CHIP, sched.carry(f"wgrad_ffn_down_{l}")))
        dya, dyb, doa, dob, dgl, dx1b, dgbias = _merge_bwd(dx1, s["p"], s["oa"], s["ob"], square(W["woa", l]), square(W["wob", l]),
                                                         square(W["wo", l]), small["gate_bias"][l], l, tiles["merge"],
                                                         sched.carry(f"merge_bwd_{l}"))
        sched.after(f"merge_bwd_{l}")
        sched.grad("wo", l, *_wgrad(s["mg"], dx1b, f"wgrad_w_o_{l}", tk))
        sched.grad("woa", l, *_wgrad(s["ya"], doa, f"wgrad_w_out_a_{l}", tk))
        sched.grad("wob", l, *_wgrad(s["yb"], dob, f"wgrad_w_out_b_{l}", tk))
        dp, mred, dwab = _mixer_bwd(s["p"], s["hseq"], dya, dyb, dgl, s["cw"], s["wab"], s["wabt"], s["lam_row"], l,
                                    tiles["mixer"], sched.carry(f"mixer_bwd_{l}"))
        sched.after(f"mixer_bwd_{l}")
        sched.grad("win", l, *_wgrad(s["h1"], dp, f"wgrad_w_in_{l}", tiles["wgrad_in"], "whole", "cols", N_CHIP,
                                     sched.carry(f"wgrad_w_in_{l}")))
        sched.after(f"wgrad_w_in_{l}")
        dx, dln1 = _inproj_bwd(dp, dx1, s["x0"], sched.tie(f"inproj_bwd_{l}", s["ln1_row"]), W["win", l], l, tiles["inproj_bwd"],
                               sched.carry(f"inproj_bwd_{l}"))
        sched.after(f"inproj_bwd_{l}", dx)
        gsmall["ln1_g"][l] = dln1[0]
        gsmall["ln2_g"][l] = dln2[0]
        gsmall["conv_a_w"][l] = mred[0:CONV_A_K]
        gsmall["conv_b_w"][l] = mred[CONV_A_K:CONV_A_K + CONV_B_K]
        gsmall["conv_b_b"][l] = mred[7]
        gsmall["lru_ba"][l] = mred[8]
        gsmall["lru_bx"][l] = mred[9]
        gsmall["lru_lambda"][l] = mred[10]
        gsmall["lru_wa"][l] = _diag_heads(dwab[:, :, :LRU_BLOCK])
        gsmall["lru_wx"][l] = _diag_heads(dwab[:, :, LRU_BLOCK:])
        gsmall["gate_bias"][l] = dgbias
    gsmall = {k: jnp.stack(v) for k, v in gsmall.items()}
    gsmall["final_g"] = d_final_g
    return loss_row, dx, gsmall


def _small_allreduce(buf):
    R, C = buf.shape
    n_dev = 8
    rp = R // n_dev
    rel = [(k >> 2 & 1, k >> 1 & 1, k & 1) for k in range(1, n_dev)]

    def body(in_ref, out_ref, recv, s1, r1, s2, r2):
        x, y, c, _ = _place()
        flip = lambda v, bit: 1 - v if bit else v
        peers = [(flip(x, kx), flip(y, ky), flip(c, kc)) for kx, ky, kc in rel]
        dev = lambda p: 4 * p[0] + 2 * p[1] + p[2]
        part = lambda ref, d: ref.at[pl.ds(pl.multiple_of(d * rp, SUBLANES), rp), :]
        me = dev((x, y, c))

        def scatter(k, src_dev, to):
            return pltpu.make_async_remote_copy(src_ref=part(in_ref, dev(to)), dst_ref=recv.at[src_dev], send_sem=s1.at[k],
                                                recv_sem=r1.at[k], device_id=to, device_id_type=MESH)

        def gather(k, src_dev, to):
            return pltpu.make_async_remote_copy(src_ref=part(out_ref, src_dev), dst_ref=part(out_ref, src_dev), send_sem=s2.at[k],
                                                recv_sem=r2.at[k], device_id=to, device_id_type=MESH)

        first = [scatter(k, me, p) for k, p in enumerate(peers)]
        for cp in first:
            cp.start()
        recv[me] = part(in_ref, me)[...]
        for k, p in enumerate(peers):
            scatter(k, dev(p), (x, y, c)).wait_recv()
        total = recv[0]
        for d in range(1, n_dev):
            total = total + recv[d]
        part(out_ref, me)[...] = total
        second = [gather(k, me, p) for k, p in enumerate(peers)]
        for cp in second:
            cp.start()
        for k, p in enumerate(peers):
            gather(k, dev(p), (x, y, c)).wait_recv()
        for cp in first + second:
            cp.wait_send()

    dma = pltpu.SemaphoreType.DMA
    vm = pl.BlockSpec(memory_space=pltpu.VMEM)
    return pl.pallas_call(
        body, name="small_allreduce", out_shape=jax.ShapeDtypeStruct((R, C), buf.dtype),
        in_specs=[vm], out_specs=vm,
        scratch_shapes=[pltpu.VMEM((n_dev, rp, C), buf.dtype), dma((n_dev - 1,)), dma((n_dev - 1,)), dma((n_dev - 1,)), dma((n_dev - 1,))],
    )(buf)


ELEMENTWISE_BLOCK_BYTES = 2 * 1024 * 1024


def _row_block(k, n):
    best = None
    for b in range(16, k + 1, 16):
        if k % b == 0 and b * n * 4 <= ELEMENTWISE_BLOCK_BYTES:
            best = b
    return best or k


def _add_halves(g, recv, place_arr, name):
    nj, hk, N = recv.shape
    bk = _row_block(hk, N)
    nb = hk // bk

    def body(k_ref, g_ref, r_ref, o_ref, ob_ref):
        s = g_ref[...] + r_ref[...].astype(F32)
        ob_ref[...] = s.astype(BF16)

        @pl.when(pl.program_id(1) == k_ref[0])
        def _():
            o_ref[...] = s

    blk = pl.BlockSpec((None, bk, N), lambda i, j, k_ref: (j, i, 0))
    grid_spec = pltpu.PrefetchScalarGridSpec(
        num_scalar_prefetch=1, grid=(nb, nj),
        in_specs=[pl.BlockSpec((None, bk, N), lambda i, j, k_ref: (j, k_ref[1] * nb + i, 0)), blk],
        out_specs=[pl.BlockSpec((bk, N), lambda i, j, k_ref: (i, 0)), blk])
    return pl.pallas_call(
        body, name=name, grid_spec=grid_spec,
        out_shape=[jax.ShapeDtypeStruct((hk, N), F32), jax.ShapeDtypeStruct((nj, hk, N), BF16)],
        compiler_params=_params("parallel", "arbitrary"),
    )(place_arr, g, recv)


def _add_chips(pc, recv, place_arr, layer, n_layers, prev, name):
    hk, N = pc.shape
    bk = _row_block(hk, N)
    nb = hk // bk

    def body(k_ref, p_ref, r0_ref, r1_ref, r2_ref, *rest):
        o_ref = rest[-1]
        o_ref[...] = ((p_ref[...] + r0_ref[...].astype(F32)) + r1_ref[...].astype(F32)) + r2_ref[...].astype(F32)

    rspec = lambda j: pl.BlockSpec((None, bk, N), lambda i, k_ref: (j, i, 0))
    in_specs = [pl.BlockSpec((bk, N), lambda i, k_ref: (i, 0)), rspec(0), rspec(1), rspec(2)]
    operands = [pc, recv, recv, recv]
    aliases = {}
    if prev is not None:
        in_specs.append(ANY)
        operands.append(prev)
        aliases = {5: 0}
    grid_spec = pltpu.PrefetchScalarGridSpec(
        num_scalar_prefetch=1, grid=(nb,), in_specs=in_specs,
        out_specs=pl.BlockSpec((None, bk, N), lambda i, k_ref: (layer, k_ref[1] * nb + i, 0)))
    return pl.pallas_call(
        body, name=name, grid_spec=grid_spec, out_shape=jax.ShapeDtypeStruct((n_layers, 2 * hk, N), F32),
        input_output_aliases=aliases, compiler_params=_params("parallel"),
    )(place_arr, *operands)


def _adamw_math(w, g, m, v):
    m = ADAM_B1 * m + (1.0 - ADAM_B1) * g
    v = ADAM_B2 * v + (1.0 - ADAM_B2) * (g * g)
    m_hat = m / (1.0 - ADAM_B1 ** ADAM_STEP)
    v_hat = v / (1.0 - ADAM_B2 ** ADAM_STEP)
    delta = -ADAM_LR * (m_hat / (jnp.sqrt(v_hat) + ADAM_EPS) + ADAM_WD * w)
    return delta, m, v


def _adamw(w, g, m, v, name):
    L, K, N = w.shape
    bk = _row_block(K, N)

    def body(w_ref, g_ref, m_ref, v_ref, d_ref, nm_ref, nv_ref, go_ref):
        g = g_ref[...]
        d_ref[...], nm_ref[...], nv_ref[...] = _adamw_math(w_ref[...], g, m_ref[...], v_ref[...])
        go_ref[...] = g

    blk = pl.BlockSpec((None, bk, N), lambda l, i: (l, i, 0))
    sds = jax.ShapeDtypeStruct((L, K, N), F32)
    return pl.pallas_call(
        body, name=name, grid=(L, K // bk), in_specs=[blk] * 4, out_specs=[blk] * 4, out_shape=[sds] * 4,
        compiler_params=_params("parallel", "parallel"),
    )(w, g, m, v)


def _adamw_small(ws, gs, ms, vs):
    n = len(ws)

    def body(*refs):
        w, g, m, v, d, nm, nv = (refs[k * n:(k + 1) * n] for k in range(7))
        for k in range(n):
            d[k][...], nm[k][...], nv[k][...] = _adamw_math(w[k][...], g[k][...], m[k][...], v[k][...])

    sds = [jax.ShapeDtypeStruct(a.shape, F32) for a in ws]
    out = pl.pallas_call(body, name="adamw_small", out_shape=sds * 3)(*ws, *gs, *ms, *vs)
    return out[:n], out[n:2 * n], out[2 * n:]


def _cast_bf16(w, layer, place_arr, name):
    _, K, N = w.shape
    bk = _row_block(K, N)

    def body(k_ref, w_ref, o_ref):
        o_ref[...] = w_ref[...].astype(BF16)

    grid_spec = pltpu.PrefetchScalarGridSpec(
        num_scalar_prefetch=1, grid=(K // bk,),
        in_specs=[pl.BlockSpec((None, bk, N), lambda i, k_ref: (layer, i, 0))],
        out_specs=pl.BlockSpec((None, bk, N), lambda i, k_ref: (k_ref[0], i, 0)))
    return pl.pallas_call(
        body, name=name, grid_spec=grid_spec, out_shape=jax.ShapeDtypeStruct((N_CHIP, K, N), BF16),
        compiler_params=_params("parallel"),
    )(place_arr, w)


BIG = ("w_in", "w_out_a", "w_out_b", "w_o", "w_ffn_gate", "w_ffn_up", "w_ffn_down")
BIG_KEY = dict(w_in="win", w_out_a="woa", w_out_b="wob", w_o="wo", w_ffn_gate="wg", w_ffn_up="wu", w_ffn_down="wd")
SHARDED_SMALL = ("conv_a_w", "conv_b_w", "gate_bias")
REPLICATED = ("ln1_g", "conv_b_b", "lru_wa", "lru_ba", "lru_wx", "lru_bx", "lru_lambda", "ln2_g", "final_g")
WEIGHTS = ("ln1_g", "w_in", "conv_a_w", "conv_b_w", "conv_b_b", "lru_wa", "lru_ba", "lru_wx", "lru_bx", "lru_lambda",
           "w_out_a", "w_out_b", "gate_bias", "w_o", "ln2_g", "w_ffn_gate", "w_ffn_up", "w_ffn_down", "final_g")
LANES = 1024


def _pack_rows(arrays, row_multiple):
    flat = jnp.concatenate([a.reshape(-1) for a in arrays])
    rows = -(-flat.shape[0] // LANES)
    rows = -(-rows // row_multiple) * row_multiple
    flat = jnp.pad(flat, (0, rows * LANES - flat.shape[0]))
    return flat.reshape(rows, LANES)


def _unpack_rows(buf, shapes):
    flat = buf.reshape(-1)
    out, off = [], 0
    for s in shapes:
        n = 1
        for d in s:
            n *= d
        out.append(flat[off:off + n].reshape(s))
        off += n
    return out


OUT_KEYS = ("wo", "woa", "wob")
FFN_KEYS = ("wg", "wu", "wd")


def _items(keys, layer):
    return [(k, layer) for k in keys]


CARRY = {
    "rms_inproj_0": [("gather_ici", _items(OUT_KEYS + FFN_KEYS, 0))],
    "mixer_fwd_0": [("gather_d2d", _items(OUT_KEYS + FFN_KEYS, 0)), ("gather_ici", _items(("win",) + OUT_KEYS, 1))],
    "merge_fwd_0": [("gather_d2d", _items(("win",) + OUT_KEYS, 1))],
    "ffn_fwd_0": [("gather_ici", _items(FFN_KEYS, 1))],
    "rms_inproj_1": [("gather_d2d", _items(FFN_KEYS, 1))],
    "merge_bwd_1": [("halves", _items(FFN_KEYS, 1))],
    "mixer_bwd_1": [("chips", _items(FFN_KEYS, 1)), ("halves", _items(OUT_KEYS, 1))],
    "inproj_bwd_1": [("chips", _items(OUT_KEYS, 1)), ("share", _items(FFN_KEYS, 1))],
    "ffn_bwd_0": [("halves", [("win", 1)]), ("share", _items(OUT_KEYS, 1))],
    "wgrad_ffn_gate_up_0": [("chips", [("win", 1)])],
    "wgrad_ffn_down_0": [("share", [("win", 1)])],
    "merge_bwd_0": [("halves", _items(FFN_KEYS, 0))],
    "mixer_bwd_0": [("chips", _items(FFN_KEYS, 0)), ("halves", _items(OUT_KEYS, 0))],
    "wgrad_w_in_0": [("chips", _items(OUT_KEYS, 0)), ("share", _items(FFN_KEYS, 0))],
}
AFTER = {
    "merge_bwd_1": [("add_halves", _items(FFN_KEYS, 1))],
    "mixer_bwd_1": [("add_chips", _items(FFN_KEYS, 1)), ("add_halves", _items(OUT_KEYS, 1))],
    "inproj_bwd_1": [("add_chips", _items(OUT_KEYS, 1))],
    "ffn_bwd_0": [("add_halves", [("win", 1)])],
    "wgrad_ffn_gate_up_0": [("add_chips", [("win", 1)])],
    "merge_bwd_0": [("add_halves", _items(FFN_KEYS, 0))],
    "mixer_bwd_0": [("add_chips", _items(FFN_KEYS, 0)), ("add_halves", _items(OUT_KEYS, 0))],
    "wgrad_w_in_0": [("add_chips", _items(OUT_KEYS, 0)),
                     ("run", ("reduce_halves_w_in_0", [("halves", [("win", 0)]), ("share", _items(OUT_KEYS, 0))])),
                     ("add_halves", [("win", 0)]),
                     ("split_start", ("win", 0))],
    "inproj_bwd_0": [("split_wait", ("win", 0)), ("add_chips", [("win", 0)]),
                     ("run", ("reduce_share_w_in_0", [("share", [("win", 0)])]))],
}


SPLIT_BESIDE = {"wgrad_w_in_0": "inproj_bwd_0"}


class _Schedule:
    def __init__(self, slots, place_arr, n_layers):
        self.W = slots
        self.place, self.L = place_arr, n_layers
        self.g32, self.g16 = {}, {}
        self.from_sibling, self.chip_sum, self.chip_sum16, self.from_chips = {}, {}, {}, {}
        self.reduced = {}
        self.split, self.tokens = {}, {}

    def stage(self, comm, kind, items):
        bf = lambda shape: jax.ShapeDtypeStruct(shape, BF16)
        for it in items:
            if kind == "gather_ici":
                comm.add(_gather_ici, 3, io=[(self.W, it)])
            elif kind == "gather_d2d":
                comm.add(_gather_d2d, 3, io=[(self.W, it)])
            elif kind == "halves":
                nj, K, N = self.g16[it].shape
                comm.add(_reduce_halves, 1, ro=[self.g16[it]], nw=[(self.from_sibling, it, bf((nj, K // 2, N)))])
            elif kind == "chips":
                _, hk, N = self.chip_sum16[it].shape
                comm.add(_reduce_chips, 3, ro=[self.chip_sum16[it]], nw=[(self.from_chips, it, bf((3, hk, N)))])
            elif kind == "share":
                comm.add(_reduce_share(it[1]), 1, io=[(self.reduced, it[0])])
        return comm

    def carry(self, name):
        comm = _Carried()
        for kind, items in CARRY.get(name, ()):
            self.stage(comm, kind, items)
        return comm

    def run(self, name, rounds):
        _run_comm([self.stage(_Carried(), kind, items) for kind, items in rounds], name)

    def grad(self, key, layer, f32, b16):
        by_chip = lambda g: g.reshape(N_CHIP, -1, g.shape[-1])
        self.g32[key, layer], self.g16[key, layer] = by_chip(f32), by_chip(b16)

    def add(self, kind, items):
        for key, layer in items:
            it = (key, layer)
            if kind == "add_halves":
                self.chip_sum[it], self.chip_sum16[it] = _add_halves(self.g32[it], self.from_sibling[it], self.place,
                                                                    f"add_halves_{key}_{layer}")
            else:
                self.reduced[key] = _add_chips(self.chip_sum[it], self.from_chips[it], self.place, layer, self.L,
                                               self.reduced.get(key), f"add_chips_{key}_{layer}")

    def between_inproj(self, layer, p, small):
        if layer != 0:
            return
        send_sems, recv_sems, before = self.first_gather
        self.W["win", 0] = _gather_first_wait(send_sems, recv_sems, self.W["win", 0], before + [p])
        w = self.sharded_small
        small_shard = jnp.concatenate([w[n] for n in SHARDED_SMALL], axis=1)
        got = {}
        second = self.stage(_Carried(), "gather_d2d", [("win", 0)])
        second.add(_gather_small, 3, ro=[small_shard], nw=[(got, "small", jax.ShapeDtypeStruct((3,) + small_shard.shape, F32))])
        _run_comm([second], "gather_first_d2d")
        xi, yi = lax.axis_index("x"), lax.axis_index("y")
        small_g = jnp.zeros((N_CHIP,) + small_shard.shape, F32)
        small_g = lax.dynamic_update_index_in_dim(small_g, small_shard, _chip_id(xi, yi), 0)
        for j, (cx, cy) in enumerate([(1 - xi, yi), (xi, 1 - yi), (1 - xi, 1 - yi)]):
            small_g = lax.dynamic_update_index_in_dim(small_g, got["small"][j], _chip_id(cx, cy), 0)
        n_layers, rows, dc = small_shard.shape
        small_full = jnp.transpose(small_g, (1, 2, 0, 3)).reshape(n_layers, rows, N_CHIP * dc)
        off = 0
        for n in SHARDED_SMALL:
            k = w[n].shape[1]
            small[n] = small_full[:, off:off + k]
            off += k

    def tie(self, name, operand):
        token = self.tokens.pop(name, None)
        return operand if token is None else operand + token[0, 0]

    def after(self, name, result=None):
        for kind, items in AFTER.get(name, ()):
            if kind == "run":
                self.run(*items)
            elif kind == "split_start":
                send_sems, recv_sems, src, land, token = _scatter_start(self.chip_sum16[items])
                self.split[items] = (send_sems, recv_sems, src, land)
                self.tokens[SPLIT_BESIDE[name]] = token
            elif kind == "split_wait":
                self.from_chips[items] = _scatter_wait(*self.split.pop(items), result)
            else:
                self.add(kind, items)


def _step(w, m, v, x, target):
    xi, yi, ci = lax.axis_index("x"), lax.axis_index("y"), lax.axis_index("c")
    chip = _chip_id(xi, yi)
    place_arr = jnp.stack([chip, ci]).astype(jnp.int32)
    L = w["ln1_g"].shape[0]
    assert L == 2
    D = x.shape[1]
    dc = D // N_CHIP

    stored = lambda n, a: jnp.swapaxes(a, 1, 2) if n in ("w_ffn_gate", "w_ffn_up") else a
    slots = {("win", 0): _cast_bf16(w["w_in"], 0, place_arr, "cast_w_in_0")}
    send_sems, recv_sems, slots["win", 0], token = _gather_first_start(slots["win", 0])
    place_after = place_arr + token[0, 0]
    for n in BIG:
        for l in range(L):
            if (BIG_KEY[n], l) not in slots:
                slots[BIG_KEY[n], l] = _cast_bf16(stored(n, w[n]), l, place_after, f"cast_{n}_{l}")
    sched = _Schedule(slots, place_arr, L)
    sched.tokens["rms_inproj_own_0"] = token
    sched.first_gather = (send_sems, recv_sems, [v for k, v in slots.items() if k != ("win", 0)])
    sched.sharded_small = {n: w[n] for n in SHARDED_SMALL}
    small = {n: w[n] for n in REPLICATED}

    loss_row, grad_x, gsmall = _local_step(x, target, sched.W, small, _tiles(x.shape[0]), sched)

    grads = {}

    order = [n for n in WEIGHTS if n not in BIG]
    packed = _pack_rows([gsmall[n] for n in order] + [loss_row], 8 * SUBLANES)
    summed = _small_allreduce(packed)
    parts = _unpack_rows(summed, [gsmall[n].shape for n in order] + [loss_row.shape])
    loss = jnp.sum(parts[-1])
    for n, g in zip(order, parts[:-1]):
        grads[n] = lax.dynamic_slice_in_dim(g, chip * dc, dc, axis=2) if n in SHARDED_SMALL else g

    delta, new_m, new_v = {}, {}, {}
    for n in BIG:
        d, nm, nv, g = _adamw(stored(n, w[n]), sched.reduced[BIG_KEY[n]], stored(n, m[n]), stored(n, v[n]), f"adamw_{n}")
        delta[n], new_m[n], new_v[n], grads[n] = stored(n, d), stored(n, nm), stored(n, nv), stored(n, g)
    for d, arrays in zip((delta, new_m, new_v), _adamw_small(*([d[n] for n in order] for d in (w, grads, m, v)))):
        d.update(zip(order, arrays))
    return loss, grad_x, grads, delta, new_m, new_v


def kernel(x, ln1_g, w_in, conv_a_w, conv_b_w, conv_b_b, lru_wa, lru_ba, lru_wx, lru_bx, lru_lambda, w_out_a, w_out_b, gate_bias, w_o, ln2_g, w_ffn_gate, w_ffn_up, w_ffn_down, final_g, loss_target, m_ln1_g, m_w_in, m_conv_a_w, m_conv_b_w, m_conv_b_b, m_lru_wa, m_lru_ba, m_lru_wx, m_lru_bx, m_lru_lambda, m_w_out_a, m_w_out_b, m_gate_bias, m_w_o, m_ln2_g, m_w_ffn_gate, m_w_ffn_up, m_w_ffn_down, m_final_g, v_ln1_g, v_w_in, v_conv_a_w, v_conv_b_w, v_conv_b_b, v_lru_wa, v_lru_ba, v_lru_wx, v_lru_bx, v_lru_lambda, v_w_out_a, v_w_out_b, v_gate_bias, v_w_o, v_ln2_g, v_w_ffn_gate, v_w_ffn_up, v_w_ffn_down, v_final_g):
    w = dict(ln1_g=ln1_g, w_in=w_in, conv_a_w=conv_a_w, conv_b_w=conv_b_w, conv_b_b=conv_b_b, lru_wa=lru_wa, lru_ba=lru_ba,
             lru_wx=lru_wx, lru_bx=lru_bx, lru_lambda=lru_lambda, w_out_a=w_out_a, w_out_b=w_out_b, gate_bias=gate_bias, w_o=w_o,
             ln2_g=ln2_g, w_ffn_gate=w_ffn_gate, w_ffn_up=w_ffn_up, w_ffn_down=w_ffn_down, final_g=final_g)
    m = dict(ln1_g=m_ln1_g, w_in=m_w_in, conv_a_w=m_conv_a_w, conv_b_w=m_conv_b_w, conv_b_b=m_conv_b_b, lru_wa=m_lru_wa,
             lru_ba=m_lru_ba, lru_wx=m_lru_wx, lru_bx=m_lru_bx, lru_lambda=m_lru_lambda, w_out_a=m_w_out_a, w_out_b=m_w_out_b,
             gate_bias=m_gate_bias, w_o=m_w_o, ln2_g=m_ln2_g, w_ffn_gate=m_w_ffn_gate, w_ffn_up=m_w_ffn_up,
             w_ffn_down=m_w_ffn_down, final_g=m_final_g)
    v = dict(ln1_g=v_ln1_g, w_in=v_w_in, conv_a_w=v_conv_a_w, conv_b_w=v_conv_b_w, conv_b_b=v_conv_b_b, lru_wa=v_lru_wa,
             lru_ba=v_lru_ba, lru_wx=v_lru_wx, lru_bx=v_lru_bx, lru_lambda=v_lru_lambda, w_out_a=v_w_out_a, w_out_b=v_w_out_b,
             gate_bias=v_gate_bias, w_o=v_w_o, ln2_g=v_ln2_g, w_ffn_gate=v_w_ffn_gate, w_ffn_up=v_w_ffn_up,
             w_ffn_down=v_w_ffn_down, final_g=v_final_g)
    loss, grad_x, grads, delta, new_m, new_v = _step(w, m, v, x[0], loss_target[0])
    return (loss, grad_x[None], *[grads[n] for n in WEIGHTS], *[delta[n] for n in WEIGHTS],
            *[new_m[n] for n in WEIGHTS], *[new_v[n] for n in WEIGHTS])
```

```python
import jax
import jax.numpy as jnp
from jax import lax
from jax.experimental import pallas as pl
from jax.experimental.pallas import tpu as pltpu

F32 = jnp.float32
BF16 = jnp.bfloat16
MESH = pl.DeviceIdType.MESH

N_CHIP = 4
RMS_EPS = 1e-6
LRU_C = 8.0
LRU_HEAD_DIM = 64
LRU_BLOCK = 256
CONV_A_K = 3
CONV_B_K = 4
ADAM_LR = 0.001
ADAM_B1 = 0.9
ADAM_B2 = 0.999
ADAM_EPS = 1e-08
ADAM_WD = 0.01
ADAM_STEP = 10
SUBLANES = 8
VMEM_LIMIT = 56 * 1024 * 1024


def _params(*sem):
    return pltpu.CompilerParams(dimension_semantics=sem, vmem_limit_bytes=VMEM_LIMIT)


def _sigmoid(v):
    return 1.0 / (1.0 + jnp.exp(-v))


def _one_minus_sq(la, a):
    return jnp.tanh(-la) * (1.0 + a * a)


def _gelu_parts(v):
    k = 0.7978845608028654
    v2 = v * v
    t = jnp.tanh(k * (v + 0.044715 * v * v2))
    gelu = 0.5 * v * (1.0 + t)
    dgelu = 0.5 * (1.0 + t) + 0.5 * v * (1.0 - t * t) * k * (1.0 + 3 * 0.044715 * v2)
    return gelu, dgelu


def _shift_down(v, k, prev8):
    rolled = pltpu.roll(v, k, 0)
    r8 = lax.broadcasted_iota(jnp.int32, prev8.shape, 0)
    head = jnp.where(r8 < k, pltpu.roll(prev8, k, 0), rolled[0:SUBLANES])
    return jnp.concatenate([head, rolled[SUBLANES:]], axis=0)


def _shift_up(v, k, next8):
    tm = v.shape[0]
    rolled = pltpu.roll(v, tm - k, 0)
    r8 = lax.broadcasted_iota(jnp.int32, next8.shape, 0)
    tail = jnp.where(r8 >= SUBLANES - k, pltpu.roll(next8, SUBLANES - k, 0), rolled[tm - SUBLANES:])
    return jnp.concatenate([rolled[:tm - SUBLANES], tail], axis=0)


def _group_scan(a, b, reverse):
    tm, c = a.shape
    a = a.reshape(tm // SUBLANES, SUBLANES, c)
    b = b.reshape(tm // SUBLANES, SUBLANES, c)
    q = lax.broadcasted_iota(jnp.int32, a.shape, 1)
    for s in (1, 2, 4):
        msk = q < SUBLANES - s if reverse else q >= s
        shift = SUBLANES - s if reverse else s
        b = jnp.where(msk, a * pltpu.roll(b, shift, 1) + b, b)
        a = jnp.where(msk, a * pltpu.roll(a, shift, 1), a)
    return a.reshape(tm, c), b.reshape(tm, c)


def _colsum8(v):
    tm, c = v.shape
    return jnp.sum(v.reshape(tm // SUBLANES, SUBLANES, c), axis=0)


def _rms_stats(xv):
    var = jnp.mean(xv * xv, axis=-1, keepdims=True)
    return lax.rsqrt(var + RMS_EPS)


def _rms_bwd(dh, xv, g):
    rstd = _rms_stats(xv)
    xhat = xv * rstd
    dxhat = dh * g
    dx = rstd * (dxhat - xhat * jnp.mean(dxhat * xhat, axis=-1, keepdims=True))
    return dx, dh * xhat


ANY = pl.BlockSpec(memory_space=pl.ANY)


def _place():
    x, y, c = lax.axis_index("x"), lax.axis_index("y"), lax.axis_index("c")
    other_chips = [(1 - x, y), (x, 1 - y), (1 - x, 1 - y)]
    return x, y, c, other_chips


def _chip_id(x, y):
    return 2 * x + y


def _half(c, hk):
    return pl.ds(pl.multiple_of(c * hk, 16), hk)


def _remote(src, dst, to, sems):
    return pltpu.make_async_remote_copy(src_ref=src, dst_ref=dst, device_id=to, device_id_type=MESH, **sems)


class _Carried:
    def __init__(self):
        self.ro, self.io, self.nw, self.parts, self.n = [], [], [], [], 0

    def add(self, maker, n, ro=(), io=(), nw=()):
        def index(items, item, same):
            for k, other in enumerate(items):
                if same(other, item):
                    return k
            items.append(item)
            return len(items) - 1

        r = [index(self.ro, a, lambda p, q: p is q) for a in ro]
        i = [index(self.io, a, lambda p, q: p[0] is q[0] and p[1] == q[1]) for a in io]
        w = [index(self.nw, a, lambda p, q: False) for a in nw]
        self.parts.append((maker, r, i, w, self.n))
        self.n += n
        return self

    def pairs(self, ro, io, nw, ssem, rsem):
        out = []
        for maker, r, i, w, base in self.parts:
            sems = lambda k, base=base: dict(send_sem=ssem.at[base + k], recv_sem=rsem.at[base + k])
            out += maker([ro[k] for k in r], [io[k] for k in i], [nw[k] for k in w], sems)
        return out

    def start(self, *refs):
        for send, _ in self.pairs(*refs):
            send.start()

    def finish(self, *refs):
        pairs = self.pairs(*refs)
        for _, recv in pairs:
            recv.wait_recv()
        for send, _ in pairs:
            send.wait_send()

    def operands(self):
        return list(self.ro) + [store[key] for store, key in self.io]

    def out_shapes(self):
        return [jax.ShapeDtypeStruct(store[key].shape, store[key].dtype) for store, key in self.io] + [s for _, _, s in self.nw]

    def keep(self, results):
        for (store, key), arr in zip(self.io, results[:len(self.io)]):
            store[key] = arr
        for (store, key, _), arr in zip(self.nw, results[len(self.io):]):
            store[key] = arr


def _call(body, comm, *, name, grid, in_specs, out_specs, out_shape, compiler_params, scratch_shapes=(), aliases=None):
    aliases = dict(aliases or {})
    if comm is None or not comm.parts:
        return pl.pallas_call(body, name=name, grid=grid, in_specs=in_specs, out_specs=out_specs, out_shape=out_shape,
                              scratch_shapes=list(scratch_shapes), input_output_aliases=aliases, compiler_params=compiler_params)
    n_in, n_out, n_scr = len(in_specs), len(out_shape), len(scratch_shapes)
    n_ro, n_io, n_nw = len(comm.ro), len(comm.io), len(comm.nw)

    def carried(*refs):
        base_in = refs[:n_in]
        ro = refs[n_in:n_in + n_ro]
        pos = n_in + n_ro + n_io
        base_out = refs[pos:pos + n_out]
        io = refs[pos + n_out:pos + n_out + n_io]
        nw = refs[pos + n_out + n_io:pos + n_out + n_io + n_nw]
        pos += n_out + n_io + n_nw
        scr = refs[pos:pos + n_scr]
        ssem, rsem = refs[pos + n_scr], refs[pos + n_scr + 1]
        first = pl.program_id(0) == 0
        last = pl.program_id(0) == grid[0] - 1
        for axis in range(1, len(grid)):
            first = first & (pl.program_id(axis) == 0)
            last = last & (pl.program_id(axis) == grid[axis] - 1)

        @pl.when(first)
        def _():
            comm.start(ro, io, nw, ssem, rsem)

        body(*base_in, *base_out, *scr)

        @pl.when(last)
        def _():
            comm.finish(ro, io, nw, ssem, rsem)

    aliases.update({n_in + n_ro + k: n_out + k for k in range(n_io)})
    dma = pltpu.SemaphoreType.DMA
    call = pl.pallas_call(
        carried, name=name, grid=grid,
        in_specs=list(in_specs) + [ANY] * (n_ro + n_io), out_specs=list(out_specs) + [ANY] * (n_io + n_nw),
        out_shape=list(out_shape) + comm.out_shapes(), input_output_aliases=aliases,
        scratch_shapes=list(scratch_shapes) + [dma((comm.n,)), dma((comm.n,))], compiler_params=compiler_params)

    def run(*operands):
        res = call(*operands, *comm.operands())
        comm.keep(res[n_out:])
        return res[:n_out]

    return run


def _run_comm(rounds, name):
    ro, io, nw, uses = [], [], [], []
    for r in rounds:
        def index(items, item, same):
            for k, other in enumerate(items):
                if same(other, item):
                    return k
            items.append(item)
            return len(items) - 1
        uses.append(([index(ro, a, lambda p, q: p is q) for a in r.ro],
                     [index(io, a, lambda p, q: p[0] is q[0] and p[1] == q[1]) for a in r.io],
                     [index(nw, a, lambda p, q: False) for a in r.nw]))
    n_ro, n_io, n_nw = len(ro), len(io), len(nw)

    def body(*refs):
        ro_refs = refs[:n_ro]
        io_refs = refs[n_ro + n_io:n_ro + 2 * n_io]
        nw_refs = refs[n_ro + 2 * n_io:n_ro + 2 * n_io + n_nw]
        sems = refs[n_ro + 2 * n_io + n_nw:]
        for k, (r, (a, b, c)) in enumerate(zip(rounds, uses)):
            args = ([ro_refs[i] for i in a], [io_refs[i] for i in b], [nw_refs[i] for i in c], sems[2 * k], sems[2 * k + 1])
            r.start(*args)
            r.finish(*args)

    operands = ro + [store[key] for store, key in io]
    out_shape = [jax.ShapeDtypeStruct(store[key].shape, store[key].dtype) for store, key in io] + [s for _, _, s in nw]
    dma = pltpu.SemaphoreType.DMA
    res = pl.pallas_call(
        body, name=name, out_shape=out_shape,
        in_specs=[ANY] * (n_ro + n_io), out_specs=[ANY] * (n_io + n_nw),
        input_output_aliases={n_ro + k: k for k in range(n_io)},
        scratch_shapes=[dma((r.n,)) for r in rounds for _ in range(2)],
    )(*operands)
    for (store, key), arr in zip(io, res[:n_io]):
        store[key] = arr
    for (store, key, _), arr in zip(nw, res[n_io:]):
        store[key] = arr


HBM = pl.BlockSpec(memory_space=pltpu.HBM)
SEM = pl.BlockSpec(memory_space=pltpu.SEMAPHORE)
SPLIT_COPY = pltpu.CompilerParams(has_side_effects=pltpu.SideEffectType.DATAFLOW_SIDE_EFFECTING)


def _first_gather_copies(slot, send_sems, recv_sems):
    x, y, c, chips = _place()
    hk = slot.shape[1] // 2
    mine = slot.at[_chip_id(x, y), _half(c, hk)]
    pairs = []
    for j, (cx, cy) in enumerate(chips):
        theirs = slot.at[_chip_id(cx, cy), _half(c, hk)]
        sems = dict(send_sem=send_sems.at[j], recv_sem=recv_sems.at[j])
        pairs.append((_remote(mine, mine, (cx, cy, c), sems), _remote(theirs, theirs, (cx, cy, c), sems)))
    return pairs


def _gather_first_start(slot):
    def body(slot_ref, send_sems, recv_sems, slot_thru, token):
        for send, _ in _first_gather_copies(slot_ref, send_sems, recv_sems):
            send.start()
        token[...] = jnp.zeros_like(token)

    dma = pltpu.SemaphoreType.DMA
    return pl.pallas_call(
        body, name="gather_first_start",
        out_shape=(dma((3,)), dma((3,)), pltpu.HBM(slot.shape, slot.dtype), jax.ShapeDtypeStruct((SUBLANES, 128), jnp.int32)),
        in_specs=(HBM,), out_specs=(SEM, SEM, HBM, pl.BlockSpec(memory_space=pltpu.VMEM)), input_output_aliases={0: 2},
        compiler_params=SPLIT_COPY,
    )(pltpu.with_memory_space_constraint(slot, pltpu.HBM))


def _gather_first_wait(send_sems, recv_sems, slot, after):
    def body(slot_ref, send_sems, recv_sems, *rest):
        for send, recv in _first_gather_copies(slot_ref, send_sems, recv_sems):
            send.wait_send()
            recv.wait_recv()

    return pl.pallas_call(
        body, name="gather_first_wait", out_shape=(pltpu.HBM(slot.shape, slot.dtype),),
        in_specs=(HBM, SEM, SEM) + (ANY,) * len(after), out_specs=(HBM,), input_output_aliases={0: 0},
        compiler_params=SPLIT_COPY,
    )(slot, send_sems, recv_sems, *after)[0]


def _scatter_copies(src, land, send_sems, recv_sems):
    x, y, c, chips = _place()
    return [_remote(src.at[_chip_id(cx, cy)], land.at[j], (cx, cy, c), dict(send_sem=send_sems.at[j], recv_sem=recv_sems.at[j]))
            for j, (cx, cy) in enumerate(chips)]


def _scatter_start(src):
    land = pltpu.with_memory_space_constraint(lax.empty((3,) + src.shape[1:], src.dtype), pltpu.HBM)

    def body(src_ref, land_ref, send_sems, recv_sems, src_thru, land_thru, token):
        for cp in _scatter_copies(src_ref, land_ref, send_sems, recv_sems):
            cp.start()
        token[...] = jnp.zeros_like(token)

    dma = pltpu.SemaphoreType.DMA
    return pl.pallas_call(
        body, name="reduce_chips_start",
        out_shape=(dma((3,)), dma((3,)), pltpu.HBM(src.shape, src.dtype), pltpu.HBM(land.shape, land.dtype),
                   jax.ShapeDtypeStruct((SUBLANES, 128), F32)),
        in_specs=(HBM, HBM), out_specs=(SEM, SEM, HBM, HBM, pl.BlockSpec(memory_space=pltpu.VMEM)),
        input_output_aliases={0: 2, 1: 3}, compiler_params=SPLIT_COPY,
    )(pltpu.with_memory_space_constraint(src, pltpu.HBM), land)


def _scatter_wait(send_sems, recv_sems, src, land, after):
    def body(src_ref, land_ref, send_sems, recv_sems, after_ref, src_done, land_done):
        for cp in _scatter_copies(src_ref, land_ref, send_sems, recv_sems):
            cp.wait_send()
            cp.wait_recv()

    return pl.pallas_call(
        body, name="reduce_chips_wait", out_shape=(pltpu.HBM(src.shape, src.dtype), pltpu.HBM(land.shape, land.dtype)),
        in_specs=(HBM, HBM, SEM, SEM, ANY), out_specs=(HBM, HBM), input_output_aliases={0: 0, 1: 1}, compiler_params=SPLIT_COPY,
    )(src, land, send_sems, recv_sems, after)[1]


def _gather_ici(ro, io, nw, sems):
    s = io[0]
    x, y, c, chips = _place()
    hk = s.shape[1] // 2
    mine = s.at[_chip_id(x, y), _half(c, hk)]
    pairs = []
    for j, (cx, cy) in enumerate(chips):
        theirs = s.at[_chip_id(cx, cy), _half(c, hk)]
        pairs.append((_remote(mine, mine, (cx, cy, c), sems(j)), _remote(theirs, theirs, (cx, cy, c), sems(j))))
    return pairs


def _gather_d2d(ro, io, nw, sems):
    s = io[0]
    x, y, c, chips = _place()
    hk = s.shape[1] // 2
    sib = (x, y, 1 - c)
    pairs = []
    for j, (cx, cy) in enumerate(chips):
        here = s.at[_chip_id(cx, cy), _half(c, hk)]
        there = s.at[_chip_id(cx, cy), _half(1 - c, hk)]
        pairs.append((_remote(here, here, sib, sems(j)), _remote(there, there, sib, sems(j))))
    return pairs


def _gather_small(ro, io, nw, sems):
    x, y, c, chips = _place()
    return [(_remote(ro[0], nw[0].at[j], (cx, cy, c), sems(j)),) * 2 for j, (cx, cy) in enumerate(chips)]


def _reduce_halves(ro, io, nw, sems):
    x, y, c, _ = _place()
    g = ro[0]
    hk = g.shape[1] // 2
    sib = (x, y, 1 - c)
    return [(_remote(g.at[:, _half(1 - c, hk)], nw[0], sib, sems(0)), _remote(g.at[:, _half(c, hk)], nw[0], sib, sems(0)))]


def _reduce_chips(ro, io, nw, sems):
    x, y, c, chips = _place()
    return [(_remote(ro[0].at[_chip_id(cx, cy)], nw[0].at[j], (cx, cy, c), sems(j)),) * 2 for j, (cx, cy) in enumerate(chips)]


def _reduce_share(layer):
    def maker(ro, io, nw, sems):
        g = io[0]
        x, y, c, _ = _place()
        hk = g.shape[1] // 2
        sib = (x, y, 1 - c)
        mine, theirs = g.at[layer, _half(c, hk)], g.at[layer, _half(1 - c, hk)]
        return [(_remote(mine, mine, sib, sems(0)), _remote(theirs, theirs, sib, sems(0)))]
    return maker


def _own_chip():
    return _chip_id(lax.axis_index("x"), lax.axis_index("y"))


def _other_chip(j):
    x, y = lax.axis_index("x"), lax.axis_index("y")
    return _chip_id(jnp.where(j == 1, x, 1 - x), jnp.where(j == 0, y, 1 - y))


def _rms_inproj_own(x, g_row, win, layer, tm):
    T, D = x.shape
    ns = win.shape[-1]

    def body(x_ref, g_ref, w_ref, p_ref, h_ref):
        xv = x_ref[...]
        h = (xv * _rms_stats(xv) * g_ref[...]).astype(BF16)
        h_ref[...] = h
        p_ref[...] = jnp.dot(h, w_ref[...], preferred_element_type=F32).astype(BF16)

    return pl.pallas_call(
        body, name=f"rms_inproj_own_{layer}", grid=(T // tm,),
        in_specs=[pl.BlockSpec((tm, D), lambda i: (i, 0)),
                  pl.BlockSpec((1, D), lambda i: (0, 0)),
                  pl.BlockSpec((None, D, ns), lambda i: (_own_chip(), 0, 0))],
        out_specs=[pl.BlockSpec((tm, ns), lambda i: (i, _own_chip())),
                   pl.BlockSpec((tm, D), lambda i: (i, 0))],
        out_shape=[jax.ShapeDtypeStruct((T, N_CHIP * ns), BF16), jax.ShapeDtypeStruct((T, D), BF16)],
        compiler_params=_params("parallel"),
    )(x, g_row, win)


def _inproj_rest(h, win, p, layer, tm, comm=None):
    T, D = h.shape
    ns = win.shape[-1]

    def body(h_ref, w_ref, p_in, p_ref):
        p_ref[...] = jnp.dot(h_ref[...], w_ref[...], preferred_element_type=F32).astype(BF16)

    return _call(
        body, comm, name=f"rms_inproj_{layer}", grid=(T // tm, N_CHIP - 1),
        in_specs=[pl.BlockSpec((tm, D), lambda i, j: (i, 0)),
                  pl.BlockSpec((None, D, ns), lambda i, j: (_other_chip(j), 0, 0)), ANY],
        out_specs=[pl.BlockSpec((tm, ns), lambda i, j: (i, _other_chip(j)))],
        out_shape=[jax.ShapeDtypeStruct(p.shape, p.dtype)], aliases={2: 0},
        compiler_params=_params("parallel", "arbitrary"),
    )(h, win, p)[0]


def _mixer_recompute(ca, xa, xb, zprev8, xbprev8, cw_ref, wab_ref, sp):
    row = lambda k: cw_ref[pl.ds(k, 1), :]
    z = ca * xa
    z1 = _shift_down(z, 1, zprev8)
    z2 = _shift_down(z, 2, zprev8)
    cz = row(2) * z + row(1) * z1 + row(0) * z2
    x1 = _shift_down(xb, 1, xbprev8)
    x2 = _shift_down(xb, 2, xbprev8)
    x3 = _shift_down(xb, 3, xbprev8)
    u = row(6) * xb + row(5) * x1 + row(4) * x2 + row(3) * x3 + row(7)
    ub = u.astype(BF16)
    nb = wab_ref.shape[0]
    ras, ixs = [], []
    for b in range(nb):
        ri = jnp.dot(ub[:, b * LRU_BLOCK:(b + 1) * LRU_BLOCK], wab_ref[b], preferred_element_type=F32)
        ras.append(ri[:, :LRU_BLOCK])
        ixs.append(ri[:, LRU_BLOCK:])
    r = _sigmoid(jnp.concatenate(ras, axis=1) + row(8))
    gi = _sigmoid(jnp.concatenate(ixs, axis=1) + row(9))
    la = (-LRU_C) * r * sp
    a = jnp.exp(la)
    m = jnp.sqrt(_one_minus_sq(la, a))
    return dict(z=z, z1=z1, z2=z2, cz=cz, x1=x1, x2=x2, x3=x3, u=u, ub=ub, r=r, gi=gi, a=a, m=m)


def _softplus_neg(lam):
    v = -lam
    return jnp.maximum(v, 0.0) + jnp.log1p(jnp.exp(-jnp.abs(v)))


def _mixer_fwd(p, cw, wab, lam_row, layer, tm, comm=None):
    T = p.shape[0]
    D = p.shape[1] // 7
    ngroups = tm // SUBLANES

    def body(ba_ref, ca_ref, xa_ref, xb_ref, gb_ref, cw_ref, wab_ref, lam_ref, ya_ref, yb_ref, h_ref,
             zprev, xbprev, hcarry, a_s, h_s):
        @pl.when(pl.program_id(0) == 0)
        def _():
            zprev[...] = jnp.zeros_like(zprev)
            xbprev[...] = jnp.zeros_like(xbprev)
            hcarry[...] = jnp.zeros_like(hcarry)

        ca = ca_ref[...].astype(F32)
        xa = xa_ref[...].astype(F32)
        xb = xb_ref[...].astype(F32)
        sp = _softplus_neg(lam_ref[...])
        c = _mixer_recompute(ca, xa, xb, zprev[...], xbprev[...], cw_ref, wab_ref, sp)
        zprev[...] = c["z"][tm - SUBLANES:]
        xbprev[...] = xb[tm - SUBLANES:]
        ya_ref[...] = (ba_ref[...].astype(F32) * c["cz"]).astype(BF16)

        a_s[...], h_s[...] = _group_scan(c["a"], c["m"] * c["gi"] * c["u"], reverse=False)

        def step(g, carry):
            off = pl.multiple_of(g * SUBLANES, SUBLANES)
            hg = h_s[pl.ds(off, SUBLANES), :] + a_s[pl.ds(off, SUBLANES), :] * carry
            h_s[pl.ds(off, SUBLANES), :] = hg
            return jnp.broadcast_to(hg[SUBLANES - 1:SUBLANES, :], hg.shape)

        hcarry[...] = lax.fori_loop(0, ngroups, step, hcarry[...], unroll=4)
        h = h_s[...]
        h_ref[...] = h
        gelu, _ = _gelu_parts(gb_ref[...].astype(F32))
        yb_ref[...] = (h * gelu).astype(BF16)

    col = lambda k: pl.BlockSpec((tm, D), lambda i: (i, k))
    full = lambda a: pl.BlockSpec(a.shape, lambda i: (0,) * a.ndim)
    tok = pl.BlockSpec((tm, D), lambda i: (i, 0))
    return _call(
        body, comm, name=f"mixer_fwd_{layer}", grid=(T // tm,),
        in_specs=[col(0), col(1), col(2), col(3), col(4), full(cw), full(wab), full(lam_row)],
        out_specs=[tok, tok, tok],
        out_shape=[jax.ShapeDtypeStruct((T, D), BF16), jax.ShapeDtypeStruct((T, D), BF16), jax.ShapeDtypeStruct((T, D), F32)],
        scratch_shapes=[pltpu.VMEM((SUBLANES, D), F32), pltpu.VMEM((SUBLANES, D), F32), pltpu.VMEM((SUBLANES, D), F32),
                        pltpu.VMEM((tm, D), F32), pltpu.VMEM((tm, D), F32)],
        compiler_params=_params("arbitrary"),
    )(p, p, p, p, p, cw, wab, lam_row)


def _merge_fwd(x, p, ya, yb, woa, wob, wo, gbias, layer, tm, comm=None):
    T, D = x.shape

    def body(x_ref, ga_ref, gb_ref, ya_ref, yb_ref, woa_ref, wob_ref, wo_ref, bias_ref, oa_ref, ob_ref, mg_ref, x1_ref):
        oa = jnp.dot(ya_ref[...], woa_ref[...], preferred_element_type=F32)
        ob = jnp.dot(yb_ref[...], wob_ref[...], preferred_element_type=F32)
        sa = _sigmoid(ga_ref[...].astype(F32) + bias_ref[pl.ds(0, 1), :])
        sb = _sigmoid(gb_ref[...].astype(F32) + bias_ref[pl.ds(1, 1), :])
        mg = (sa * oa + sb * ob).astype(BF16)
        oa_ref[...] = oa.astype(BF16)
        ob_ref[...] = ob.astype(BF16)
        mg_ref[...] = mg
        x1_ref[...] = x_ref[...] + jnp.dot(mg, wo_ref[...], preferred_element_type=F32)

    tok = pl.BlockSpec((tm, D), lambda i: (i, 0))
    wsp = pl.BlockSpec((D, D), lambda i: (0, 0))
    bf = jax.ShapeDtypeStruct((T, D), BF16)
    return _call(
        body, comm, name=f"merge_fwd_{layer}", grid=(T // tm,),
        in_specs=[tok, pl.BlockSpec((tm, D), lambda i: (i, 5)), pl.BlockSpec((tm, D), lambda i: (i, 6)), tok, tok,
                  wsp, wsp, wsp, pl.BlockSpec(gbias.shape, lambda i: (0, 0))],
        out_specs=[tok, tok, tok, tok],
        out_shape=[bf, bf, bf, jax.ShapeDtypeStruct((T, D), F32)],
        compiler_params=_params("parallel"),
    )(x, p, p, ya, yb, woa, wob, wo, gbias)


def _loss_tile(xv, g, tgt):
    d = xv.shape[-1]
    rstd = _rms_stats(xv)
    xhat = xv * rstd
    err = xhat * g - tgt
    dy = err * (1.0 / d)
    dxhat = dy * g
    dx = rstd * (dxhat - xhat * jnp.mean(dxhat * xhat, axis=-1, keepdims=True))
    return dx, _colsum8(err * err), _colsum8(dy * xhat)


def _ffn_fwd(x1, g_row, wg, wu, wd, layer, tm, comm=None):
    T, D = x1.shape
    fs = wg.shape[-2]
    n = T // tm
    nt = (((1,), (1,)), ((), ()))

    def body(x_ref, g_ref, wg_ref, wu_ref, wd_ref, h_ref, gg_ref, uu_ref, x2_ref, acc):
        j = pl.program_id(1)

        @pl.when(j == 0)
        def _():
            xv = x_ref[...]
            h_ref[...] = (xv * _rms_stats(xv) * g_ref[...]).astype(BF16)
            acc[...] = xv

        h = h_ref[...]
        gg = lax.dot_general(h, wg_ref[...], nt, preferred_element_type=F32)
        uu = lax.dot_general(h, wu_ref[...], nt, preferred_element_type=F32)
        gg_ref[...] = gg.astype(BF16)
        uu_ref[...] = uu.astype(BF16)
        act = (gg * _sigmoid(gg) * uu).astype(BF16)
        acc[...] += jnp.dot(act, wd_ref[...], preferred_element_type=F32)

        @pl.when(j == N_CHIP - 1)
        def _():
            x2_ref[...] = acc[...]

    tok = pl.BlockSpec((tm, D), lambda i, j: (i, 0))
    cm = pl.BlockSpec((None, tm, fs), lambda i, j: (j, i, 0))
    wsp = pl.BlockSpec((None, fs, D), lambda i, j: (j, 0, 0))
    return _call(
        body, comm, name=f"ffn_fwd_{layer}", grid=(n, N_CHIP),
        in_specs=[tok, pl.BlockSpec((1, D), lambda i, j: (0, 0)), wsp, wsp, wsp], out_specs=[tok, cm, cm, tok],
        out_shape=[jax.ShapeDtypeStruct((T, D), BF16), jax.ShapeDtypeStruct((N_CHIP, T, fs), BF16),
                   jax.ShapeDtypeStruct((N_CHIP, T, fs), BF16), jax.ShapeDtypeStruct((T, D), F32)],
        scratch_shapes=[pltpu.VMEM((tm, D), F32)], compiler_params=_params("parallel", "arbitrary"),
    )(x1, g_row, wg, wu, wd)


def _final_loss(x, g_row, target, tm):
    T, D = x.shape
    n = T // tm

    def body(x_ref, g_ref, t_ref, dx_ref, red_ref, racc):
        i = pl.program_id(0)

        @pl.when(i == 0)
        def _():
            racc[...] = jnp.zeros_like(racc)

        dx_ref[...], sq, dg = _loss_tile(x_ref[...], g_ref[...], t_ref[...])
        racc[0] += sq
        racc[1] += dg

        @pl.when(i == n - 1)
        def _():
            red_ref[pl.ds(0, 1), :] = jnp.sum(racc[0], axis=0, keepdims=True) * (0.5 / D)
            red_ref[pl.ds(1, 1), :] = jnp.sum(racc[1], axis=0, keepdims=True)

    tok = pl.BlockSpec((tm, D), lambda i: (i, 0))
    return pl.pallas_call(
        body, name="final_loss", grid=(n,),
        in_specs=[tok, pl.BlockSpec((1, D), lambda i: (0, 0)), tok],
        out_specs=[tok, pl.BlockSpec((2, D), lambda i: (0, 0))],
        out_shape=[jax.ShapeDtypeStruct((T, D), F32), jax.ShapeDtypeStruct((2, D), F32)],
        scratch_shapes=[pltpu.VMEM((2, SUBLANES, D), F32)],
        compiler_params=_params("arbitrary"),
    )(x, g_row, target)


def _ffn_bwd_gates(dx2, gg, uu, wd, layer, tm):
    T, D = dx2.shape
    fs = wd.shape[-2]
    nt = (((1,), (1,)), ((), ()))

    def body(dx_ref, gg_ref, uu_ref, wd_ref, dg_ref, du_ref, act_ref, dxb_ref):
        @pl.when(pl.program_id(1) == 0)
        def _():
            dxb_ref[...] = dx_ref[...].astype(BF16)

        dact = lax.dot_general(dxb_ref[...], wd_ref[...], nt, preferred_element_type=F32)
        g = gg_ref[...].astype(F32)
        u = uu_ref[...].astype(F32)
        s = _sigmoid(g)
        silu = g * s
        dg_ref[...] = (dact * u * (s * (1.0 + g * (1.0 - s)))).astype(BF16)
        du_ref[...] = (dact * silu).astype(BF16)
        act_ref[...] = (silu * u).astype(BF16)

    tok = pl.BlockSpec((tm, D), lambda i, j: (i, 0))
    cm = pl.BlockSpec((None, tm, fs), lambda i, j: (j, i, 0))
    cms = jax.ShapeDtypeStruct((N_CHIP, T, fs), BF16)
    return pl.pallas_call(
        body, name=f"ffn_bwd_gates_{layer}", grid=(T // tm, N_CHIP),
        in_specs=[tok, cm, cm, pl.BlockSpec((None, fs, D), lambda i, j: (j, 0, 0))], out_specs=[cm, cm, cm, tok],
        out_shape=[cms, cms, cms, jax.ShapeDtypeStruct((T, D), BF16)],
        compiler_params=_params("parallel", "arbitrary"),
    )(dx2, gg, uu, wd)


def _ffn_bwd(dgg, duu, wg, wu, dx2, x1, g_row, layer, tm, comm=None):
    T, D = dx2.shape
    fs = wg.shape[-2]
    n = T // tm

    def body(dg_ref, du_ref, wg_ref, wu_ref, dx_ref, x_ref, g_ref, dx1_ref, red_ref, acc, racc):
        i = pl.program_id(0)
        j = pl.program_id(1)

        @pl.when((i == 0) & (j == 0))
        def _():
            racc[...] = jnp.zeros_like(racc)

        @pl.when(j == 0)
        def _():
            acc[...] = jnp.zeros_like(acc)

        acc[...] += (jnp.dot(dg_ref[...], wg_ref[...], preferred_element_type=F32)
                     + jnp.dot(du_ref[...], wu_ref[...], preferred_element_type=F32))

        @pl.when(j == N_CHIP - 1)
        def _():
            dx, dgain = _rms_bwd(acc[...], x_ref[...], g_ref[...])
            dx1_ref[...] = dx_ref[...] + dx
            racc[...] += _colsum8(dgain)

        @pl.when((i == n - 1) & (j == N_CHIP - 1))
        def _():
            red_ref[...] = jnp.sum(racc[...], axis=0, keepdims=True)

    tok = pl.BlockSpec((tm, D), lambda i, j: (i, 0))
    cm = pl.BlockSpec((None, tm, fs), lambda i, j: (j, i, 0))
    wsp = pl.BlockSpec((None, fs, D), lambda i, j: (j, 0, 0))
    row = pl.BlockSpec((1, D), lambda i, j: (0, 0))
    return _call(
        body, comm, name=f"ffn_bwd_{layer}", grid=(n, N_CHIP),
        in_specs=[cm, cm, wsp, wsp, tok, tok, row], out_specs=[tok, row],
        out_shape=[jax.ShapeDtypeStruct((T, D), F32), jax.ShapeDtypeStruct((1, D), F32)],
        scratch_shapes=[pltpu.VMEM((tm, D), F32), pltpu.VMEM((SUBLANES, D), F32)],
        compiler_params=_params("arbitrary", "arbitrary"),
    )(dgg, duu, wg, wu, dx2, x1, g_row)


def _merge_bwd(dx1, p, oa, ob, woa, wob, wo, gbias, layer, tm, comm=None):
    T, D = dx1.shape
    n = T // tm
    nt = (((1,), (1,)), ((), ()))

    def body(dx_ref, ga_ref, gb_ref, oa_ref, ob_ref, woa_ref, wob_ref, wo_ref, bias_ref,
             dya_ref, dyb_ref, doa_ref, dob_ref, dgl_ref, dxb_ref, red_ref, racc):
        i = pl.program_id(0)

        @pl.when(i == 0)
        def _():
            racc[...] = jnp.zeros_like(racc)

        dxb = dx_ref[...].astype(BF16)
        dxb_ref[...] = dxb
        dm = lax.dot_general(dxb, wo_ref[...], nt, preferred_element_type=F32)
        sa = _sigmoid(ga_ref[...].astype(F32) + bias_ref[pl.ds(0, 1), :])
        sb = _sigmoid(gb_ref[...].astype(F32) + bias_ref[pl.ds(1, 1), :])
        doa = (dm * sa).astype(BF16)
        dob = (dm * sb).astype(BF16)
        dga = dm * oa_ref[...].astype(F32) * (sa * (1.0 - sa))
        dgb = dm * ob_ref[...].astype(F32) * (sb * (1.0 - sb))
        doa_ref[...] = doa
        dob_ref[...] = dob
        dgl_ref[:, 0:D] = dga.astype(BF16)
        dgl_ref[:, D:2 * D] = dgb.astype(BF16)
        racc[0] += _colsum8(dga)
        racc[1] += _colsum8(dgb)
        dya_ref[...] = lax.dot_general(doa, woa_ref[...], nt, preferred_element_type=F32).astype(BF16)
        dyb_ref[...] = lax.dot_general(dob, wob_ref[...], nt, preferred_element_type=F32).astype(BF16)

        @pl.when(i == n - 1)
        def _():
            red_ref[pl.ds(0, 1), :] = jnp.sum(racc[0], axis=0, keepdims=True)
            red_ref[pl.ds(1, 1), :] = jnp.sum(racc[1], axis=0, keepdims=True)

    tok = pl.BlockSpec((tm, D), lambda i: (i, 0))
    wsp = pl.BlockSpec((D, D), lambda i: (0, 0))
    bf = jax.ShapeDtypeStruct((T, D), BF16)
    return _call(
        body, comm, name=f"merge_bwd_{layer}", grid=(n,),
        in_specs=[tok, pl.BlockSpec((tm, D), lambda i: (i, 5)), pl.BlockSpec((tm, D), lambda i: (i, 6)), tok, tok,
                  wsp, wsp, wsp, pl.BlockSpec(gbias.shape, lambda i: (0, 0))],
        out_specs=[tok, tok, tok, tok, pl.BlockSpec((tm, 2 * D), lambda i: (i, 0)), tok, pl.BlockSpec((2, D), lambda i: (0, 0))],
        out_shape=[bf, bf, bf, bf, jax.ShapeDtypeStruct((T, 2 * D), BF16), bf, jax.ShapeDtypeStruct((2, D), F32)],
        scratch_shapes=[pltpu.VMEM((2, SUBLANES, D), F32)],
        compiler_params=_params("arbitrary"),
    )(dx1, p, p, oa, ob, woa, wob, wo, gbias)


N_MIXER_RED = 16


def _mixer_bwd(p, hseq, dya, dyb, dgl, cw, wab, wabt, lam_row, layer, tm, comm=None):
    T = p.shape[0]
    D = p.shape[1] // 7
    n = T // tm
    ngroups = tm // SUBLANES
    nb = wab.shape[0]
    hb = 16
    tn = (((0,), (0,)), ((), ()))

    def body(ba_ref, ca_ref, xa_ref, xb_ref, gb_ref, h_ref, dya_ref, dyb_ref, dgl_ref,
             cap_ref, xap_ref, xbp_ref, hp_ref, ban_ref, dyan_ref,
             cw_ref, wab_ref, wabt_ref, lam_ref,
             dp_ref, red_ref, dwab_ref,
             racc, wacc, anext, gnext, dunext, c_s, g_s):
        i = pl.program_id(0)
        first_tile = i == n - 1
        last_tile = i == 0

        @pl.when(i == 0)
        def _():
            racc[...] = jnp.zeros_like(racc)
            wacc[...] = jnp.zeros_like(wacc)
            anext[...] = jnp.zeros_like(anext)
            gnext[...] = jnp.zeros_like(gnext)
            dunext[...] = jnp.zeros_like(dunext)

        keep_prev = jnp.where(first_tile, 0.0, 1.0)
        keep_next = jnp.where(last_tile, 0.0, 1.0)
        ba = ba_ref[...].astype(F32)
        ca = ca_ref[...].astype(F32)
        xa = xa_ref[...].astype(F32)
        xb = xb_ref[...].astype(F32)
        h = h_ref[...]
        dya = dya_ref[...].astype(F32)
        dyb = dyb_ref[...].astype(F32)
        zprev8 = (cap_ref[...].astype(F32) * xap_ref[...].astype(F32))[hb - SUBLANES:] * keep_prev
        xbprev8 = xbp_ref[...].astype(F32)[hb - SUBLANES:] * keep_prev
        hprev8 = hp_ref[...] * keep_prev
        dcznext8 = (dyan_ref[...].astype(F32) * ban_ref[...].astype(F32))[:SUBLANES] * keep_next

        lam = lam_ref[...]
        sp = _softplus_neg(lam)
        c = _mixer_recompute(ca, xa, xb, zprev8, xbprev8, cw_ref, wab_ref, sp)
        row = lambda k: cw_ref[pl.ds(k, 1), :]
        a, m, r, gi, u = c["a"], c["m"], c["r"], c["gi"], c["u"]

        gelu, dgelu = _gelu_parts(gb_ref[...].astype(F32))
        dgb = dyb * h * dgelu
        c_s[...], g_s[...] = _group_scan(_shift_up(a, 1, anext[...]), dyb * gelu, reverse=True)

        def step(k, carry):
            off = pl.multiple_of((ngroups - 1 - k) * SUBLANES, SUBLANES)
            gg = g_s[pl.ds(off, SUBLANES), :] + c_s[pl.ds(off, SUBLANES), :] * carry
            g_s[pl.ds(off, SUBLANES), :] = gg
            return jnp.broadcast_to(gg[0:1, :], gg.shape)

        gnext[...] = lax.fori_loop(0, ngroups, step, gnext[...], unroll=4)
        anext[...] = a[0:SUBLANES]
        g = g_s[...]

        hprev = _shift_down(h, 1, hprev8)
        da = g * hprev
        gm = g * m
        dgi = gm * u
        du = gm * gi
        dmv = g * gi * u
        dla = a * (da - dmv * a / m)
        dra = dla * ((-LRU_C) * sp) * (r * (1.0 - r))
        dix = dgi * (gi * (1.0 - gi))
        racc[10] += _colsum8(dla * r)
        racc[8] += _colsum8(dra)
        racc[9] += _colsum8(dix)
        drab = dra.astype(BF16)
        dixb = dix.astype(BF16)
        ub = c["ub"]
        dus = []
        for b in range(nb):
            sl = slice(b * LRU_BLOCK, (b + 1) * LRU_BLOCK)
            dri = jnp.concatenate([drab[:, sl], dixb[:, sl]], axis=1)
            dus.append(jnp.dot(dri, wabt_ref[b], preferred_element_type=F32))
            wacc[b] += lax.dot_general(ub[:, sl], dri, tn, preferred_element_type=F32)
        du = du + jnp.concatenate(dus, axis=1)

        dun = dunext[...]
        du1 = _shift_up(du, 1, dun)
        du2 = _shift_up(du, 2, dun)
        du3 = _shift_up(du, 3, dun)
        dxb = row(6) * du + row(5) * du1 + row(4) * du2 + row(3) * du3
        dunext[...] = du[0:SUBLANES]
        racc[6] += _colsum8(du * xb)
        racc[5] += _colsum8(du * c["x1"])
        racc[4] += _colsum8(du * c["x2"])
        racc[3] += _colsum8(du * c["x3"])
        racc[7] += _colsum8(du)

        dba = dya * c["cz"]
        dcz = dya * ba
        dcz1 = _shift_up(dcz, 1, dcznext8)
        dcz2 = _shift_up(dcz, 2, dcznext8)
        dz = row(2) * dcz + row(1) * dcz1 + row(0) * dcz2
        racc[2] += _colsum8(dcz * c["z"])
        racc[1] += _colsum8(dcz * c["z1"])
        racc[0] += _colsum8(dcz * c["z2"])

        dp_ref[:, 0:D] = dba.astype(BF16)
        dp_ref[:, D:2 * D] = (dz * xa).astype(BF16)
        dp_ref[:, 2 * D:3 * D] = (dz * ca).astype(BF16)
        dp_ref[:, 3 * D:4 * D] = dxb.astype(BF16)
        dp_ref[:, 4 * D:5 * D] = dgb.astype(BF16)
        dp_ref[:, 5 * D:7 * D] = dgl_ref[...]

        @pl.when(i == n - 1)
        def _():
            dlam_scale = LRU_C * _sigmoid(-lam)
            for k in range(N_MIXER_RED):
                tot = jnp.sum(racc[k], axis=0, keepdims=True)
                red_ref[pl.ds(k, 1), :] = tot * dlam_scale if k == 10 else tot
            dwab_ref[...] = wacc[...]

    rt = lambda i: n - 1 - i
    col = lambda k: pl.BlockSpec((tm, D), lambda i: (rt(i), k))
    tok = pl.BlockSpec((tm, D), lambda i: (rt(i), 0))
    full = lambda a: pl.BlockSpec(a.shape, lambda i: (0,) * a.ndim)
    prev16 = lambda k: pl.BlockSpec((hb, D), lambda i: (jnp.maximum(rt(i) * (tm // hb) - 1, 0), k))
    next16 = lambda k: pl.BlockSpec((hb, D), lambda i: (jnp.minimum((rt(i) + 1) * (tm // hb), T // hb - 1), k))
    hprev = pl.BlockSpec((SUBLANES, D), lambda i: (jnp.maximum(rt(i) * ngroups - 1, 0), 0))
    return _call(
        body, comm, name=f"mixer_bwd_{layer}", grid=(n,),
        in_specs=[col(0), col(1), col(2), col(3), col(4), tok, tok, tok, pl.BlockSpec((tm, 2 * D), lambda i: (rt(i), 0)),
                  prev16(1), prev16(2), prev16(3), hprev, next16(0), next16(0),
                  full(cw), full(wab), full(wabt), full(lam_row)],
        out_specs=[pl.BlockSpec((tm, 7 * D), lambda i: (rt(i), 0)),
                   pl.BlockSpec((N_MIXER_RED, D), lambda i: (0, 0)),
                   pl.BlockSpec((nb, LRU_BLOCK, 2 * LRU_BLOCK), lambda i: (0, 0, 0))],
        out_shape=[jax.ShapeDtypeStruct((T, 7 * D), BF16), jax.ShapeDtypeStruct((N_MIXER_RED, D), F32),
                   jax.ShapeDtypeStruct((nb, LRU_BLOCK, 2 * LRU_BLOCK), F32)],
        scratch_shapes=[pltpu.VMEM((N_MIXER_RED, SUBLANES, D), F32), pltpu.VMEM((nb, LRU_BLOCK, 2 * LRU_BLOCK), F32),
                        pltpu.VMEM((SUBLANES, D), F32), pltpu.VMEM((SUBLANES, D), F32), pltpu.VMEM((SUBLANES, D), F32),
                        pltpu.VMEM((tm, D), F32), pltpu.VMEM((tm, D), F32)],
        compiler_params=_params("arbitrary"),
    )(p, p, p, p, p, hseq, dya, dyb, dgl, p, p, p, hseq, p, dya, cw, wab, wabt, lam_row)


def _inproj_bwd(dp, dx1, x, g_row, win, layer, tm, comm=None):
    T, D = x.shape
    ns = win.shape[-1]
    n = T // tm
    nt = (((1,), (1,)), ((), ()))

    def body(dp_ref, dx_ref, x_ref, g_ref, w_ref, dx0_ref, red_ref, acc, racc):
        i = pl.program_id(0)
        j = pl.program_id(1)

        @pl.when((i == 0) & (j == 0))
        def _():
            racc[...] = jnp.zeros_like(racc)

        @pl.when(j == 0)
        def _():
            acc[...] = jnp.zeros_like(acc)

        acc[...] += lax.dot_general(dp_ref[...], w_ref[...], nt, preferred_element_type=F32)

        @pl.when(j == N_CHIP - 1)
        def _():
            dx, dgain = _rms_bwd(acc[...], x_ref[...], g_ref[...])
            dx0_ref[...] = dx_ref[...] + dx
            racc[...] += _colsum8(dgain)

        @pl.when((i == n - 1) & (j == N_CHIP - 1))
        def _():
            red_ref[...] = jnp.sum(racc[...], axis=0, keepdims=True)

    tok = pl.BlockSpec((tm, D), lambda i, j: (i, 0))
    return _call(
        body, comm, name=f"inproj_bwd_{layer}", grid=(n, N_CHIP),
        in_specs=[pl.BlockSpec((tm, ns), lambda i, j: (i, j)), tok, tok, pl.BlockSpec((1, D), lambda i, j: (0, 0)),
                  pl.BlockSpec((None, D, ns), lambda i, j: (j, 0, 0))],
        out_specs=[tok, pl.BlockSpec((1, D), lambda i, j: (0, 0))],
        out_shape=[jax.ShapeDtypeStruct((T, D), F32), jax.ShapeDtypeStruct((1, D), F32)],
        scratch_shapes=[pltpu.VMEM((tm, D), F32), pltpu.VMEM((SUBLANES, D), F32)],
        compiler_params=_params("arbitrary", "arbitrary"),
    )(dp, dx1, x, g_row, win)


def _wgrad(a, b, name, tk, a_kind="whole", b_kind="whole", nj=1, comm=None):
    T = a.shape[-2]
    width = lambda v, kind: v.shape[-1] // nj if kind == "cols" else v.shape[-1]
    ka, kb = width(a, a_kind), width(b, b_kind)
    nt = T // tk
    tn = (((0,), (0,)), ((), ()))

    def spec(k, kind):
        if kind == "cm":
            return pl.BlockSpec((None, tk, k), lambda j, t: (j, t, 0))
        if kind == "cols":
            return pl.BlockSpec((tk, k), lambda j, t: (t, j))
        return pl.BlockSpec((tk, k), lambda j, t: (t, 0))

    def body(a_ref, b_ref, o_ref, ob_ref):
        t = pl.program_id(1)

        @pl.when(t == 0)
        def _():
            o_ref[...] = jnp.zeros_like(o_ref)

        o_ref[...] += lax.dot_general(a_ref[...], b_ref[...], tn, preferred_element_type=F32)

        @pl.when(t == nt - 1)
        def _():
            ob_ref[...] = o_ref[...].astype(BF16)

    o_spec = pl.BlockSpec((None, ka, kb), lambda j, t: (j, 0, 0))
    return _call(
        body, comm, name=name, grid=(nj, nt),
        in_specs=[spec(ka, a_kind), spec(kb, b_kind)], out_specs=[o_spec, o_spec],
        out_shape=[jax.ShapeDtypeStruct((nj, ka, kb), F32), jax.ShapeDtypeStruct((nj, ka, kb), BF16)],
        compiler_params=_params("parallel", "arbitrary"),
    )(a, b)


def _wgrad_pair(a, b1, b2, name, tk, comm=None):
    T, ka = a.shape
    nj, _, kb = b1.shape
    nt = T // tk
    tn = (((0,), (0,)), ((), ()))

    def body(a_ref, b1_ref, b2_ref, o1_ref, o1b_ref, o2_ref, o2b_ref):
        t = pl.program_id(1)

        @pl.when(t == 0)
        def _():
            o1_ref[...] = jnp.zeros_like(o1_ref)
            o2_ref[...] = jnp.zeros_like(o2_ref)

        av = a_ref[...]
        o1_ref[...] += lax.dot_general(b1_ref[...], av, tn, preferred_element_type=F32)
        o2_ref[...] += lax.dot_general(b2_ref[...], av, tn, preferred_element_type=F32)

        @pl.when(t == nt - 1)
        def _():
            o1b_ref[...] = o1_ref[...].astype(BF16)
            o2b_ref[...] = o2_ref[...].astype(BF16)

    b_spec = pl.BlockSpec((None, tk, kb), lambda j, t: (j, t, 0))
    o_spec = pl.BlockSpec((None, kb, ka), lambda j, t: (j, 0, 0))
    f32 = jax.ShapeDtypeStruct((nj, kb, ka), F32)
    b16 = jax.ShapeDtypeStruct((nj, kb, ka), BF16)
    return _call(
        body, comm, name=name, grid=(nj, nt),
        in_specs=[pl.BlockSpec((tk, ka), lambda j, t: (t, 0)), b_spec, b_spec], out_specs=[o_spec] * 4,
        out_shape=[f32, b16, f32, b16], compiler_params=_params("parallel", "arbitrary"),
    )(a, b1, b2)


def _block_diag(w):
    hb = LRU_BLOCK // LRU_HEAD_DIM
    nb = w.shape[0] // hb
    w4 = w.reshape(nb, hb, LRU_HEAD_DIM, LRU_HEAD_DIM)
    eye = jnp.eye(hb, dtype=w.dtype)
    return jnp.einsum("bide,ij->bidje", w4, eye).reshape(nb, LRU_BLOCK, LRU_BLOCK)


def _diag_heads(m):
    hb = LRU_BLOCK // LRU_HEAD_DIM
    nb = m.shape[0]
    m5 = m.reshape(nb, hb, LRU_HEAD_DIM, hb, LRU_HEAD_DIM)
    eye = jnp.eye(hb, dtype=m.dtype)
    return jnp.einsum("bidje,ij->bide", m5, eye).reshape(nb * hb, LRU_HEAD_DIM, LRU_HEAD_DIM)


def _tiles(T):
    cap = lambda n: min(n, T)
    return dict(inproj=cap(1024), mixer=cap(256), merge=cap(512), ffn=cap(1024), ffn_bwd=cap(1024), loss=cap(512), inproj_bwd=cap(1024),
                wgrad_in=cap(2048), wgrad=cap(2048))


class _NoSchedule:
    def carry(self, name):
        return None

    def tie(self, name, operand):
        return operand

    def between_inproj(self, layer, p, small):
        pass

    def after(self, name, result=None):
        pass

    def grad(self, key, layer, f32, b16):
        pass


def _local_step(x, target, W, small, tiles, sched):
    L = small["ln1_g"].shape[0]
    D = x.shape[1]
    square = lambda a: a.reshape(D, D)
    saved = []
    h = x
    for l in range(L):
        ln1_row = small["ln1_g"][l][None]
        ln2_row = small["ln2_g"][l][None]
        p, h1 = _rms_inproj_own(h, sched.tie(f"rms_inproj_own_{l}", ln1_row), W["win", l], l, tiles["inproj"])
        sched.between_inproj(l, p, small)
        p = _inproj_rest(h1, W["win", l], p, l, tiles["inproj"], sched.carry(f"rms_inproj_{l}"))
        cw = jnp.concatenate([small["conv_a_w"][l], small["conv_b_w"][l], small["conv_b_b"][l][None],
                              small["lru_ba"][l][None], small["lru_bx"][l][None]], axis=0)
        wab = jnp.concatenate([_block_diag(small["lru_wa"][l]), _block_diag(small["lru_wx"][l])], axis=2).astype(BF16)
        wabt = jnp.swapaxes(wab, 1, 2)
        lam_row = small["lru_lambda"][l][None]
        ya, yb, hseq = _mixer_fwd(p, cw, wab, lam_row, l, tiles["mixer"], sched.carry(f"mixer_fwd_{l}"))
        oa, ob, mg, x1 = _merge_fwd(h, p, ya, yb, square(W["woa", l]), square(W["wob", l]), square(W["wo", l]),
                                    small["gate_bias"][l], l, tiles["merge"], sched.carry(f"merge_fwd_{l}"))
        h2, gg, uu, x2 = _ffn_fwd(x1, ln2_row, W["wg", l], W["wu", l], W["wd", l], l, tiles["ffn"], sched.carry(f"ffn_fwd_{l}"))
        saved.append(dict(x0=h, p=p, h1=h1, ya=ya, yb=yb, hseq=hseq, oa=oa, ob=ob, mg=mg, x1=x1, h2=h2, gg=gg, uu=uu,
                          cw=cw, wab=wab, wabt=wabt, lam_row=lam_row, ln1_row=ln1_row, ln2_row=ln2_row))
        h = x2

    dx, red = _final_loss(h, small["final_g"][None], target, tiles["loss"])
    loss_row, d_final_g = red[0], red[1]

    gsmall = {k: [None] * L for k in ("ln1_g", "ln2_g", "conv_a_w", "conv_b_w", "conv_b_b", "lru_wa", "lru_ba", "lru_wx",
                                      "lru_bx", "lru_lambda", "gate_bias")}
    tk = tiles["wgrad"]
    for l in reversed(range(L)):
        s = saved[l]
        dgg, duu, act, dx2b = _ffn_bwd_gates(dx, s["gg"], s["uu"], W["wd", l], l, tiles["ffn_bwd"])
        dx1, dln2 = _ffn_bwd(dgg, duu, W["wg", l], W["wu", l], dx, s["x1"], s["ln2_row"], l, tiles["ffn_bwd"],
                             sched.carry(f"ffn_bwd_{l}"))
        sched.after(f"ffn_bwd_{l}")
        gate_up = _wgrad_pair(s["h2"], dgg, duu, f"wgrad_ffn_gate_up_{l}", tk, sched.carry(f"wgrad_ffn_gate_up_{l}"))
        sched.grad("wg", l, *gate_up[0:2])
        sched.grad("wu", l, *gate_up[2:4])
        sched.after(f"wgrad_ffn_gate_up_{l}")
        sched.grad("wd", l, *_wgrad(act, dx2b, f"wgrad_ffn_down_{l}", tk, "cm", "whole", N_CHIP, sched.carry(f"wgrad_ffn_down_{l}")))
        dya, dyb, doa, dob, dgl, dx1b, dgbias = _merge_bwd(dx1, s["p"], s["oa"], s["ob"], square(W["woa", l]), square(W["wob", l]),
                                                         square(W["wo", l]), small["gate_bias"][l], l, tiles["merge"],
                                                         sched.carry(f"merge_bwd_{l}"))
        sched.after(f"merge_bwd_{l}")
        sched.grad("wo", l, *_wgrad(s["mg"], dx1b, f"wgrad_w_o_{l}", tk))
        sched.grad("woa", l, *_wgrad(s["ya"], doa, f"wgrad_w_out_a_{l}", tk))
        sched.grad("wob", l, *_wgrad(s["yb"], dob, f"wgrad_w_out_b_{l}", tk))
        dp, mred, dwab = _mixer_bwd(s["p"], s["hseq"], dya, dyb, dgl, s["cw"], s["wab"], s["wabt"], s["lam_row"], l,
                                    tiles["mixer"], sched.carry(f"mixer_bwd_{l}"))
        sched.after(f"mixer_bwd_{l}")
        sched.grad("win", l, *_wgrad(s["h1"], dp, f"wgrad_w_in_{l}", tiles["wgrad_in"], "whole", "cols", N_CHIP,
                                     sched.carry(f"wgrad_w_in_{l}")))
        sched.after(f"wgrad_w_in_{l}")
        dx, dln1 = _inproj_bwd(dp, dx1, s["x0"], sched.tie(f"inproj_bwd_{l}", s["ln1_row"]), W["win", l], l, tiles["inproj_bwd"],
                               sched.carry(f"inproj_bwd_{l}"))
        sched.after(f"inproj_bwd_{l}", dx)
        gsmall["ln1_g"][l] = dln1[0]
        gsmall["ln2_g"][l] = dln2[0]
        gsmall["conv_a_w"][l] = mred[0:CONV_A_K]
        gsmall["conv_b_w"][l] = mred[CONV_A_K:CONV_A_K + CONV_B_K]
        gsmall["conv_b_b"][l] = mred[7]
        gsmall["lru_ba"][l] = mred[8]
        gsmall["lru_bx"][l] = mred[9]
        gsmall["lru_lambda"][l] = mred[10]
        gsmall["lru_wa"][l] = _diag_heads(dwab[:, :, :LRU_BLOCK])
        gsmall["lru_wx"][l] = _diag_heads(dwab[:, :, LRU_BLOCK:])
        gsmall["gate_bias"][l] = dgbias
    gsmall = {k: jnp.stack(v) for k, v in gsmall.items()}
    gsmall["final_g"] = d_final_g
    return loss_row, dx, gsmall


def _small_allreduce(buf):
    R, C = buf.shape
    n_dev = 8
    rp = R // n_dev
    rel = [(k >> 2 & 1, k >> 1 & 1, k & 1) for k in range(1, n_dev)]

    def body(in_ref, out_ref, recv, s1, r1, s2, r2):
        x, y, c, _ = _place()
        flip = lambda v, bit: 1 - v if bit else v
        peers = [(flip(x, kx), flip(y, ky), flip(c, kc)) for kx, ky, kc in rel]
        dev = lambda p: 4 * p[0] + 2 * p[1] + p[2]
        part = lambda ref, d: ref.at[pl.ds(pl.multiple_of(d * rp, SUBLANES), rp), :]
        me = dev((x, y, c))

        def scatter(k, src_dev, to):
            return pltpu.make_async_remote_copy(src_ref=part(in_ref, dev(to)), dst_ref=recv.at[src_dev], send_sem=s1.at[k],
                                                recv_sem=r1.at[k], device_id=to, device_id_type=MESH)

        def gather(k, src_dev, to):
            return pltpu.make_async_remote_copy(src_ref=part(out_ref, src_dev), dst_ref=part(out_ref, src_dev), send_sem=s2.at[k],
                                                recv_sem=r2.at[k], device_id=to, device_id_type=MESH)

        first = [scatter(k, me, p) for k, p in enumerate(peers)]
        for cp in first:
            cp.start()
        recv[me] = part(in_ref, me)[...]
        for k, p in enumerate(peers):
            scatter(k, dev(p), (x, y, c)).wait_recv()
        total = recv[0]
        for d in range(1, n_dev):
            total = total + recv[d]
        part(out_ref, me)[...] = total
        second = [gather(k, me, p) for k, p in enumerate(peers)]
        for cp in second:
            cp.start()
        for k, p in enumerate(peers):
            gather(k, dev(p), (x, y, c)).wait_recv()
        for cp in first + second:
            cp.wait_send()

    dma = pltpu.SemaphoreType.DMA
    vm = pl.BlockSpec(memory_space=pltpu.VMEM)
    return pl.pallas_call(
        body, name="small_allreduce", out_shape=jax.ShapeDtypeStruct((R, C), buf.dtype),
        in_specs=[vm], out_specs=vm,
        scratch_shapes=[pltpu.VMEM((n_dev, rp, C), buf.dtype), dma((n_dev - 1,)), dma((n_dev - 1,)), dma((n_dev - 1,)), dma((n_dev - 1,))],
    )(buf)


ELEMENTWISE_BLOCK_BYTES = 2 * 1024 * 1024


def _row_block(k, n):
    best = None
    for b in range(16, k + 1, 16):
        if k % b == 0 and b * n * 4 <= ELEMENTWISE_BLOCK_BYTES:
            best = b
    return best or k


def _add_halves(g, recv, place_arr, name):
    nj, hk, N = recv.shape
    bk = _row_block(hk, N)
    nb = hk // bk

    def body(k_ref, g_ref, r_ref, o_ref, ob_ref):
        s = g_ref[...] + r_ref[...].astype(F32)
        ob_ref[...] = s.astype(BF16)

        @pl.when(pl.program_id(1) == k_ref[0])
        def _():
            o_ref[...] = s

    blk = pl.BlockSpec((None, bk, N), lambda i, j, k_ref: (j, i, 0))
    grid_spec = pltpu.PrefetchScalarGridSpec(
        num_scalar_prefetch=1, grid=(nb, nj),
        in_specs=[pl.BlockSpec((None, bk, N), lambda i, j, k_ref: (j, k_ref[1] * nb + i, 0)), blk],
        out_specs=[pl.BlockSpec((bk, N), lambda i, j, k_ref: (i, 0)), blk])
    return pl.pallas_call(
        body, name=name, grid_spec=grid_spec,
        out_shape=[jax.ShapeDtypeStruct((hk, N), F32), jax.ShapeDtypeStruct((nj, hk, N), BF16)],
        compiler_params=_params("parallel", "arbitrary"),
    )(place_arr, g, recv)


def _add_chips(pc, recv, place_arr, layer, n_layers, prev, name):
    hk, N = pc.shape
    bk = _row_block(hk, N)
    nb = hk // bk

    def body(k_ref, p_ref, r0_ref, r1_ref, r2_ref, *rest):
        o_ref = rest[-1]
        o_ref[...] = ((p_ref[...] + r0_ref[...].astype(F32)) + r1_ref[...].astype(F32)) + r2_ref[...].astype(F32)

    rspec = lambda j: pl.BlockSpec((None, bk, N), lambda i, k_ref: (j, i, 0))
    in_specs = [pl.BlockSpec((bk, N), lambda i, k_ref: (i, 0)), rspec(0), rspec(1), rspec(2)]
    operands = [pc, recv, recv, recv]
    aliases = {}
    if prev is not None:
        in_specs.append(ANY)
        operands.append(prev)
        aliases = {5: 0}
    grid_spec = pltpu.PrefetchScalarGridSpec(
        num_scalar_prefetch=1, grid=(nb,), in_specs=in_specs,
        out_specs=pl.BlockSpec((None, bk, N), lambda i, k_ref: (layer, k_ref[1] * nb + i, 0)))
    return pl.pallas_call(
        body, name=name, grid_spec=grid_spec, out_shape=jax.ShapeDtypeStruct((n_layers, 2 * hk, N), F32),
        input_output_aliases=aliases, compiler_params=_params("parallel"),
    )(place_arr, *operands)


def _adamw_math(w, g, m, v):
    m = ADAM_B1 * m + (1.0 - ADAM_B1) * g
    v = ADAM_B2 * v + (1.0 - ADAM_B2) * (g * g)
    m_hat = m / (1.0 - ADAM_B1 ** ADAM_STEP)
    v_hat = v / (1.0 - ADAM_B2 ** ADAM_STEP)
    delta = -ADAM_LR * (m_hat / (jnp.sqrt(v_hat) + ADAM_EPS) + ADAM_WD * w)
    return delta, m, v


def _adamw(w, g, m, v, name):
    L, K, N = w.shape
    bk = _row_block(K, N)

    def body(w_ref, g_ref, m_ref, v_ref, d_ref, nm_ref, nv_ref, go_ref):
        g = g_ref[...]
        d_ref[...], nm_ref[...], nv_ref[...] = _adamw_math(w_ref[...], g, m_ref[...], v_ref[...])
        go_ref[...] = g

    blk = pl.BlockSpec((None, bk, N), lambda l, i: (l, i, 0))
    sds = jax.ShapeDtypeStruct((L, K, N), F32)
    return pl.pallas_call(
        body, name=name, grid=(L, K // bk), in_specs=[blk] * 4, out_specs=[blk] * 4, out_shape=[sds] * 4,
        compiler_params=_params("parallel", "parallel"),
    )(w, g, m, v)


def _adamw_small(ws, gs, ms, vs):
    n = len(ws)

    def body(*refs):
        w, g, m, v, d, nm, nv = (refs[k * n:(k + 1) * n] for k in range(7))
        for k in range(n):
            d[k][...], nm[k][...], nv[k][...] = _adamw_math(w[k][...], g[k][...], m[k][...], v[k][...])

    sds = [jax.ShapeDtypeStruct(a.shape, F32) for a in ws]
    out = pl.pallas_call(body, name="adamw_small", out_shape=sds * 3)(*ws, *gs, *ms, *vs)
    return out[:n], out[n:2 * n], out[2 * n:]


def _cast_bf16(w, layer, place_arr, name):
    _, K, N = w.shape
    bk = _row_block(K, N)

    def body(k_ref, w_ref, o_ref):
        o_ref[...] = w_ref[...].astype(BF16)

    grid_spec = pltpu.PrefetchScalarGridSpec(
        num_scalar_prefetch=1, grid=(K // bk,),
        in_specs=[pl.BlockSpec((None, bk, N), lambda i, k_ref: (layer, i, 0))],
        out_specs=pl.BlockSpec((None, bk, N), lambda i, k_ref: (k_ref[0], i, 0)))
    return pl.pallas_call(
        body, name=name, grid_spec=grid_spec, out_shape=jax.ShapeDtypeStruct((N_CHIP, K, N), BF16),
        compiler_params=_params("parallel"),
    )(place_arr, w)


BIG = ("w_in", "w_out_a", "w_out_b", "w_o", "w_ffn_gate", "w_ffn_up", "w_ffn_down")
BIG_KEY = dict(w_in="win", w_out_a="woa", w_out_b="wob", w_o="wo", w_ffn_gate="wg", w_ffn_up="wu", w_ffn_down="wd")
SHARDED_SMALL = ("conv_a_w", "conv_b_w", "gate_bias")
REPLICATED = ("ln1_g", "conv_b_b", "lru_wa", "lru_ba", "lru_wx", "lru_bx", "lru_lambda", "ln2_g", "final_g")
WEIGHTS = ("ln1_g", "w_in", "conv_a_w", "conv_b_w", "conv_b_b", "lru_wa", "lru_ba", "lru_wx", "lru_bx", "lru_lambda",
           "w_out_a", "w_out_b", "gate_bias", "w_o", "ln2_g", "w_ffn_gate", "w_ffn_up", "w_ffn_down", "final_g")
LANES = 1024


def _pack_rows(arrays, row_multiple):
    flat = jnp.concatenate([a.reshape(-1) for a in arrays])
    rows = -(-flat.shape[0] // LANES)
    rows = -(-rows // row_multiple) * row_multiple
    flat = jnp.pad(flat, (0, rows * LANES - flat.shape[0]))
    return flat.reshape(rows, LANES)


def _unpack_rows(buf, shapes):
    flat = buf.reshape(-1)
    out, off = [], 0
    for s in shapes:
        n = 1
        for d in s:
            n *= d
        out.append(flat[off:off + n].reshape(s))
        off += n
    return out


OUT_KEYS = ("wo", "woa", "wob")
FFN_KEYS = ("wg", "wu", "wd")


def _items(keys, layer):
    return [(k, layer) for k in keys]


CARRY = {
    "rms_inproj_0": [("gather_ici", _items(OUT_KEYS + ("wg", "wu"), 0))],
    "mixer_fwd_0": [("gather_d2d", _items(OUT_KEYS + ("wg", "wu"), 0)), ("gather_ici", [("wd", 0), ("win", 1)])],
    "merge_fwd_0": [("gather_d2d", [("wd", 0)]), ("gather_ici", _items(OUT_KEYS, 1))],
    "ffn_fwd_0": [("gather_d2d", _items(("win",) + OUT_KEYS, 1)), ("gather_ici", _items(FFN_KEYS, 1))],
    "rms_inproj_1": [("gather_d2d", _items(FFN_KEYS, 1))],
    "merge_bwd_1": [("halves", _items(FFN_KEYS, 1))],
    "mixer_bwd_1": [("chips", _items(FFN_KEYS, 1)), ("halves", _items(OUT_KEYS, 1))],
    "inproj_bwd_1": [("chips", _items(OUT_KEYS, 1)), ("share", _items(FFN_KEYS, 1))],
    "ffn_bwd_0": [("halves", [("win", 1)]), ("share", _items(OUT_KEYS, 1))],
    "wgrad_ffn_gate_up_0": [("chips", [("win", 1)])],
    "wgrad_ffn_down_0": [("share", [("win", 1)])],
    "merge_bwd_0": [("halves", _items(FFN_KEYS, 0))],
    "mixer_bwd_0": [("chips", _items(FFN_KEYS, 0)), ("halves", _items(OUT_KEYS, 0))],
    "wgrad_w_in_0": [("chips", _items(OUT_KEYS, 0)), ("share", _items(FFN_KEYS, 0))],
}
AFTER = {
    "merge_bwd_1": [("add_halves", _items(FFN_KEYS, 1))],
    "mixer_bwd_1": [("add_chips", _items(FFN_KEYS, 1)), ("add_halves", _items(OUT_KEYS, 1))],
    "inproj_bwd_1": [("add_chips", _items(OUT_KEYS, 1))],
    "ffn_bwd_0": [("add_halves", [("win", 1)])],
    "wgrad_ffn_gate_up_0": [("add_chips", [("win", 1)])],
    "merge_bwd_0": [("add_halves", _items(FFN_KEYS, 0))],
    "mixer_bwd_0": [("add_chips", _items(FFN_KEYS, 0)), ("add_halves", _items(OUT_KEYS, 0))],
    "wgrad_w_in_0": [("add_chips", _items(OUT_KEYS, 0)),
                     ("run", ("reduce_halves_w_in_0", [("halves", [("win", 0)]), ("share", _items(OUT_KEYS, 0))])),
                     ("add_halves", [("win", 0)]),
                     ("split_start", ("win", 0))],
    "inproj_bwd_0": [("split_wait", ("win", 0)), ("add_chips", [("win", 0)]),
                     ("run", ("reduce_share_w_in_0", [("share", [("win", 0)])]))],
}


SPLIT_BESIDE = {"wgrad_w_in_0": "inproj_bwd_0"}


class _Schedule:
    def __init__(self, slots, place_arr, n_layers):
        self.W = slots
        self.place, self.L = place_arr, n_layers
        self.g32, self.g16 = {}, {}
        self.from_sibling, self.chip_sum, self.chip_sum16, self.from_chips = {}, {}, {}, {}
        self.reduced = {}
        self.split, self.tokens = {}, {}

    def stage(self, comm, kind, items):
        bf = lambda shape: jax.ShapeDtypeStruct(shape, BF16)
        for it in items:
            if kind == "gather_ici":
                comm.add(_gather_ici, 3, io=[(self.W, it)])
            elif kind == "gather_d2d":
                comm.add(_gather_d2d, 3, io=[(self.W, it)])
            elif kind == "halves":
                nj, K, N = self.g16[it].shape
                comm.add(_reduce_halves, 1, ro=[self.g16[it]], nw=[(self.from_sibling, it, bf((nj, K // 2, N)))])
            elif kind == "chips":
                _, hk, N = self.chip_sum16[it].shape
                comm.add(_reduce_chips, 3, ro=[self.chip_sum16[it]], nw=[(self.from_chips, it, bf((3, hk, N)))])
            elif kind == "share":
                comm.add(_reduce_share(it[1]), 1, io=[(self.reduced, it[0])])
        return comm

    def carry(self, name):
        comm = _Carried()
        for kind, items in CARRY.get(name, ()):
            self.stage(comm, kind, items)
        return comm

    def run(self, name, rounds):
        _run_comm([self.stage(_Carried(), kind, items) for kind, items in rounds], name)

    def grad(self, key, layer, f32, b16):
        by_chip = lambda g: g.reshape(N_CHIP, -1, g.shape[-1])
        self.g32[key, layer], self.g16[key, layer] = by_chip(f32), by_chip(b16)

    def add(self, kind, items):
        for key, layer in items:
            it = (key, layer)
            if kind == "add_halves":
                self.chip_sum[it], self.chip_sum16[it] = _add_halves(self.g32[it], self.from_sibling[it], self.place,
                                                                    f"add_halves_{key}_{layer}")
            else:
                self.reduced[key] = _add_chips(self.chip_sum[it], self.from_chips[it], self.place, layer, self.L,
                                               self.reduced.get(key), f"add_chips_{key}_{layer}")

    def between_inproj(self, layer, p, small):
        if layer != 0:
            return
        send_sems, recv_sems, before = self.first_gather
        self.W["win", 0] = _gather_first_wait(send_sems, recv_sems, self.W["win", 0], before + [p])
        w = self.sharded_small
        small_shard = jnp.concatenate([w[n] for n in SHARDED_SMALL], axis=1)
        got = {}
        second = self.stage(_Carried(), "gather_d2d", [("win", 0)])
        second.add(_gather_small, 3, ro=[small_shard], nw=[(got, "small", jax.ShapeDtypeStruct((3,) + small_shard.shape, F32))])
        _run_comm([second], "gather_first_d2d")
        xi, yi = lax.axis_index("x"), lax.axis_index("y")
        small_g = jnp.zeros((N_CHIP,) + small_shard.shape, F32)
        small_g = lax.dynamic_update_index_in_dim(small_g, small_shard, _chip_id(xi, yi), 0)
        for j, (cx, cy) in enumerate([(1 - xi, yi), (xi, 1 - yi), (1 - xi, 1 - yi)]):
            small_g = lax.dynamic_update_index_in_dim(small_g, got["small"][j], _chip_id(cx, cy), 0)
        n_layers, rows, dc = small_shard.shape
        small_full = jnp.transpose(small_g, (1, 2, 0, 3)).reshape(n_layers, rows, N_CHIP * dc)
        off = 0
        for n in SHARDED_SMALL:
            k = w[n].shape[1]
            small[n] = small_full[:, off:off + k]
            off += k

    def tie(self, name, operand):
        token = self.tokens.pop(name, None)
        return operand if token is None else operand + token[0, 0]

    def after(self, name, result=None):
        for kind, items in AFTER.get(name, ()):
            if kind == "run":
                self.run(*items)
            elif kind == "split_start":
                send_sems, recv_sems, src, land, token = _scatter_start(self.chip_sum16[items])
                self.split[items] = (send_sems, recv_sems, src, land)
                self.tokens[SPLIT_BESIDE[name]] = token
            elif kind == "split_wait":
                self.from_chips[items] = _scatter_wait(*self.split.pop(items), result)
            else:
                self.add(kind, items)


def _step(w, m, v, x, target):
    xi, yi, ci = lax.axis_index("x"), lax.axis_index("y"), lax.axis_index("c")
    chip = _chip_id(xi, yi)
    place_arr = jnp.stack([chip, ci]).astype(jnp.int32)
    L = w["ln1_g"].shape[0]
    assert L == 2
    D = x.shape[1]
    dc = D // N_CHIP

    stored = lambda n, a: jnp.swapaxes(a, 1, 2) if n in ("w_ffn_gate", "w_ffn_up") else a
    slots = {("win", 0): _cast_bf16(w["w_in"], 0, place_arr, "cast_w_in_0")}
    send_sems, recv_sems, slots["win", 0], token = _gather_first_start(slots["win", 0])
    place_after = place_arr + token[0, 0]
    for n in BIG:
        for l in range(L):
            if (BIG_KEY[n], l) not in slots:
                slots[BIG_KEY[n], l] = _cast_bf16(stored(n, w[n]), l, place_after, f"cast_{n}_{l}")
    sched = _Schedule(slots, place_arr, L)
    sched.tokens["rms_inproj_own_0"] = token
    sched.first_gather = (send_sems, recv_sems, [v for k, v in slots.items() if k != ("win", 0)])
    sched.sharded_small = {n: w[n] for n in SHARDED_SMALL}
    small = {n: w[n] for n in REPLICATED}

    loss_row, grad_x, gsmall = _local_step(x, target, sched.W, small, _tiles(x.shape[0]), sched)

    grads = {}

    order = [n for n in WEIGHTS if n not in BIG]
    packed = _pack_rows([gsmall[n] for n in order] + [loss_row], 8 * SUBLANES)
    summed = _small_allreduce(packed)
    parts = _unpack_rows(summed, [gsmall[n].shape for n in order] + [loss_row.shape])
    loss = jnp.sum(parts[-1])
    for n, g in zip(order, parts[:-1]):
        grads[n] = lax.dynamic_slice_in_dim(g, chip * dc, dc, axis=2) if n in SHARDED_SMALL else g

    delta, new_m, new_v = {}, {}, {}
    for n in BIG:
        d, nm, nv, g = _adamw(stored(n, w[n]), sched.reduced[BIG_KEY[n]], stored(n, m[n]), stored(n, v[n]), f"adamw_{n}")
        delta[n], new_m[n], new_v[n], grads[n] = stored(n, d), stored(n, nm), stored(n, nv), stored(n, g)
    for d, arrays in zip((delta, new_m, new_v), _adamw_small(*([d[n] for n in order] for d in (w, grads, m, v)))):
        d.update(zip(order, arrays))
    return loss, grad_x, grads, delta, new_m, new_v


def kernel(x, ln1_g, w_in, conv_a_w, conv_b_w, conv_b_b, lru_wa, lru_ba, lru_wx, lru_bx, lru_lambda, w_out_a, w_out_b, gate_bias, w_o, ln2_g, w_ffn_gate, w_ffn_up, w_ffn_down, final_g, loss_target, m_ln1_g, m_w_in, m_conv_a_w, m_conv_b_w, m_conv_b_b, m_lru_wa, m_lru_ba, m_lru_wx, m_lru_bx, m_lru_lambda, m_w_out_a, m_w_out_b, m_gate_bias, m_w_o, m_ln2_g, m_w_ffn_gate, m_w_ffn_up, m_w_ffn_down, m_final_g, v_ln1_g, v_w_in, v_conv_a_w, v_conv_b_w, v_conv_b_b, v_lru_wa, v_lru_ba, v_lru_wx, v_lru_bx, v_lru_lambda, v_w_out_a, v_w_out_b, v_gate_bias, v_w_o, v_ln2_g, v_w_ffn_gate, v_w_ffn_up, v_w_ffn_down, v_final_g):
    w = dict(ln1_g=ln1_g, w_in=w_in, conv_a_w=conv_a_w, conv_b_w=conv_b_w, conv_b_b=conv_b_b, lru_wa=lru_wa, lru_ba=lru_ba,
             lru_wx=lru_wx, lru_bx=lru_bx, lru_lambda=lru_lambda, w_out_a=w_out_a, w_out_b=w_out_b, gate_bias=gate_bias, w_o=w_o,
             ln2_g=ln2_g, w_ffn_gate=w_ffn_gate, w_ffn_up=w_ffn_up, w_ffn_down=w_ffn_down, final_g=final_g)
    m = dict(ln1_g=m_ln1_g, w_in=m_w_in, conv_a_w=m_conv_a_w, conv_b_w=m_conv_b_w, conv_b_b=m_conv_b_b, lru_wa=m_lru_wa,
             lru_ba=m_lru_ba, lru_wx=m_lru_wx, lru_bx=m_lru_bx, lru_lambda=m_lru_lambda, w_out_a=m_w_out_a, w_out_b=m_w_out_b,
             gate_bias=m_gate_bias, w_o=m_w_o, ln2_g=m_ln2_g, w_ffn_gate=m_w_ffn_gate, w_ffn_up=m_w_ffn_up,
             w_ffn_down=m_w_ffn_down, final_g=m_final_g)
    v = dict(ln1_g=v_ln1_g, w_in=v_w_in, conv_a_w=v_conv_a_w, conv_b_w=v_conv_b_w, conv_b_b=v_conv_b_b, lru_wa=v_lru_wa,
             lru_ba=v_lru_ba, lru_wx=v_lru_wx, lru_bx=v_lru_bx, lru_lambda=v_lru_lambda, w_out_a=v_w_out_a, w_out_b=v_w_out_b,
             gate_bias=v_gate_bias, w_o=v_w_o, ln2_g=v_ln2_g, w_ffn_gate=v_w_ffn_gate, w_ffn_up=v_w_ffn_up,
             w_ffn_down=v_w_ffn_down, final_g=v_final_g)
    loss, grad_x, grads, delta, new_m, new_v = _step(w, m, v, x[0], loss_target[0])
    return (loss, grad_x[None], *[grads[n] for n in WEIGHTS], *[delta[n] for n in WEIGHTS],
            *[new_m[n] for n in WEIGHTS], *[new_v[n] for n in WEIGHTS])
```

```python
import jax
import jax.numpy as jnp
from jax import lax
from jax.experimental import pallas as pl
from jax.experimental.pallas import tpu as pltpu

F32 = jnp.float32
BF16 = jnp.bfloat16
MESH = pl.DeviceIdType.MESH

N_CHIP = 4
RMS_EPS = 1e-6
LRU_C = 8.0
LRU_HEAD_DIM = 64
LRU_BLOCK = 256
CONV_A_K = 3
CONV_B_K = 4
ADAM_LR = 0.001
ADAM_B1 = 0.9
ADAM_B2 = 0.999
ADAM_EPS = 1e-08
ADAM_WD = 0.01
ADAM_STEP = 10
SUBLANES = 8
VMEM_LIMIT = 56 * 1024 * 1024


def _params(*sem):
    return pltpu.CompilerParams(dimension_semantics=sem, vmem_limit_bytes=VMEM_LIMIT)


def _sigmoid(v):
    return 1.0 / (1.0 + jnp.exp(-v))


def _one_minus_sq(la, a):
    return jnp.tanh(-la) * (1.0 + a * a)


def _gelu_parts(v):
    k = 0.7978845608028654
    v2 = v * v
    t = jnp.tanh(k * (v + 0.044715 * v * v2))
    gelu = 0.5 * v * (1.0 + t)
    dgelu = 0.5 * (1.0 + t) + 0.5 * v * (1.0 - t * t) * k * (1.0 + 3 * 0.044715 * v2)
    return gelu, dgelu


def _shift_down(v, k, prev8):
    rolled = pltpu.roll(v, k, 0)
    r8 = lax.broadcasted_iota(jnp.int32, prev8.shape, 0)
    head = jnp.where(r8 < k, pltpu.roll(prev8, k, 0), rolled[0:SUBLANES])
    return jnp.concatenate([head, rolled[SUBLANES:]], axis=0)


def _shift_up(v, k, next8):
    tm = v.shape[0]
    rolled = pltpu.roll(v, tm - k, 0)
    r8 = lax.broadcasted_iota(jnp.int32, next8.shape, 0)
    tail = jnp.where(r8 >= SUBLANES - k, pltpu.roll(next8, SUBLANES - k, 0), rolled[tm - SUBLANES:])
    return jnp.concatenate([rolled[:tm - SUBLANES], tail], axis=0)


def _group_scan(a, b, reverse):
    tm, c = a.shape
    a = a.reshape(tm // SUBLANES, SUBLANES, c)
    b = b.reshape(tm // SUBLANES, SUBLANES, c)
    q = lax.broadcasted_iota(jnp.int32, a.shape, 1)
    for s in (1, 2, 4):
        msk = q < SUBLANES - s if reverse else q >= s
        shift = SUBLANES - s if reverse else s
        b = jnp.where(msk, a * pltpu.roll(b, shift, 1) + b, b)
        a = jnp.where(msk, a * pltpu.roll(a, shift, 1), a)
    return a.reshape(tm, c), b.reshape(tm, c)


def _colsum8(v):
    tm, c = v.shape
    return jnp.sum(v.reshape(tm // SUBLANES, SUBLANES, c), axis=0)


def _rms_stats(xv):
    var = jnp.mean(xv * xv, axis=-1, keepdims=True)
    return lax.rsqrt(var + RMS_EPS)


def _rms_bwd(dh, xv, g):
    rstd = _rms_stats(xv)
    xhat = xv * rstd
    dxhat = dh * g
    dx = rstd * (dxhat - xhat * jnp.mean(dxhat * xhat, axis=-1, keepdims=True))
    return dx, dh * xhat


ANY = pl.BlockSpec(memory_space=pl.ANY)


def _place():
    x, y, c = lax.axis_index("x"), lax.axis_index("y"), lax.axis_index("c")
    other_chips = [(1 - x, y), (x, 1 - y), (1 - x, 1 - y)]
    return x, y, c, other_chips


def _chip_id(x, y):
    return 2 * x + y


def _half(c, hk):
    return pl.ds(pl.multiple_of(c * hk, 16), hk)


def _remote(src, dst, to, sems):
    return pltpu.make_async_remote_copy(src_ref=src, dst_ref=dst, device_id=to, device_id_type=MESH, **sems)


class _Carried:
    def __init__(self):
        self.ro, self.io, self.nw, self.parts, self.n = [], [], [], [], 0

    def add(self, maker, n, ro=(), io=(), nw=()):
        def index(items, item, same):
            for k, other in enumerate(items):
                if same(other, item):
                    return k
            items.append(item)
            return len(items) - 1

        r = [index(self.ro, a, lambda p, q: p is q) for a in ro]
        i = [index(self.io, a, lambda p, q: p[0] is q[0] and p[1] == q[1]) for a in io]
        w = [index(self.nw, a, lambda p, q: False) for a in nw]
        self.parts.append((maker, r, i, w, self.n))
        self.n += n
        return self

    def pairs(self, ro, io, nw, ssem, rsem):
        out = []
        for maker, r, i, w, base in self.parts:
            sems = lambda k, base=base: dict(send_sem=ssem.at[base + k], recv_sem=rsem.at[base + k])
            out += maker([ro[k] for k in r], [io[k] for k in i], [nw[k] for k in w], sems)
        return out

    def start(self, *refs):
        for send, _ in self.pairs(*refs):
            send.start()

    def finish(self, *refs):
        pairs = self.pairs(*refs)
        for _, recv in pairs:
            recv.wait_recv()
        for send, _ in pairs:
            send.wait_send()

    def operands(self):
        return list(self.ro) + [store[key] for store, key in self.io]

    def out_shapes(self):
        return [jax.ShapeDtypeStruct(store[key].shape, store[key].dtype) for store, key in self.io] + [s for _, _, s in self.nw]

    def keep(self, results):
        for (store, key), arr in zip(self.io, results[:len(self.io)]):
            store[key] = arr
        for (store, key, _), arr in zip(self.nw, results[len(self.io):]):
            store[key] = arr


def _call(body, comm, *, name, grid, in_specs, out_specs, out_shape, compiler_params, scratch_shapes=(), aliases=None):
    aliases = dict(aliases or {})
    if comm is None or not comm.parts:
        return pl.pallas_call(body, name=name, grid=grid, in_specs=in_specs, out_specs=out_specs, out_shape=out_shape,
                              scratch_shapes=list(scratch_shapes), input_output_aliases=aliases, compiler_params=compiler_params)
    n_in, n_out, n_scr = len(in_specs), len(out_shape), len(scratch_shapes)
    n_ro, n_io, n_nw = len(comm.ro), len(comm.io), len(comm.nw)

    def carried(*refs):
        base_in = refs[:n_in]
        ro = refs[n_in:n_in + n_ro]
        pos = n_in + n_ro + n_io
        base_out = refs[pos:pos + n_out]
        io = refs[pos + n_out:pos + n_out + n_io]
        nw = refs[pos + n_out + n_io:pos + n_out + n_io + n_nw]
        pos += n_out + n_io + n_nw
        scr = refs[pos:pos + n_scr]
        ssem, rsem = refs[pos + n_scr], refs[pos + n_scr + 1]
        first = pl.program_id(0) == 0
        last = pl.program_id(0) == grid[0] - 1
        for axis in range(1, len(grid)):
            first = first & (pl.program_id(axis) == 0)
            last = last & (pl.program_id(axis) == grid[axis] - 1)

        @pl.when(first)
        def _():
            comm.start(ro, io, nw, ssem, rsem)

        body(*base_in, *base_out, *scr)

        @pl.when(last)
        def _():
            comm.finish(ro, io, nw, ssem, rsem)

    aliases.update({n_in + n_ro + k: n_out + k for k in range(n_io)})
    dma = pltpu.SemaphoreType.DMA
    call = pl.pallas_call(
        carried, name=name, grid=grid,
        in_specs=list(in_specs) + [ANY] * (n_ro + n_io), out_specs=list(out_specs) + [ANY] * (n_io + n_nw),
        out_shape=list(out_shape) + comm.out_shapes(), input_output_aliases=aliases,
        scratch_shapes=list(scratch_shapes) + [dma((comm.n,)), dma((comm.n,))], compiler_params=compiler_params)

    def run(*operands):
        res = call(*operands, *comm.operands())
        comm.keep(res[n_out:])
        return res[:n_out]

    return run


def _run_comm(rounds, name):
    ro, io, nw, uses = [], [], [], []
    for r in rounds:
        def index(items, item, same):
            for k, other in enumerate(items):
                if same(other, item):
                    return k
            items.append(item)
            return len(items) - 1
        uses.append(([index(ro, a, lambda p, q: p is q) for a in r.ro],
                     [index(io, a, lambda p, q: p[0] is q[0] and p[1] == q[1]) for a in r.io],
                     [index(nw, a, lambda p, q: False) for a in r.nw]))
    n_ro, n_io, n_nw = len(ro), len(io), len(nw)

    def body(*refs):
        ro_refs = refs[:n_ro]
        io_refs = refs[n_ro + n_io:n_ro + 2 * n_io]
        nw_refs = refs[n_ro + 2 * n_io:n_ro + 2 * n_io + n_nw]
        sems = refs[n_ro + 2 * n_io + n_nw:]
        for k, (r, (a, b, c)) in enumerate(zip(rounds, uses)):
            args = ([ro_refs[i] for i in a], [io_refs[i] for i in b], [nw_refs[i] for i in c], sems[2 * k], sems[2 * k + 1])
            r.start(*args)
            r.finish(*args)

    operands = ro + [store[key] for store, key in io]
    out_shape = [jax.ShapeDtypeStruct(store[key].shape, store[key].dtype) for store, key in io] + [s for _, _, s in nw]
    dma = pltpu.SemaphoreType.DMA
    res = pl.pallas_call(
        body, name=name, out_shape=out_shape,
        in_specs=[ANY] * (n_ro + n_io), out_specs=[ANY] * (n_io + n_nw),
        input_output_aliases={n_ro + k: k for k in range(n_io)},
        scratch_shapes=[dma((r.n,)) for r in rounds for _ in range(2)],
    )(*operands)
    for (store, key), arr in zip(io, res[:n_io]):
        store[key] = arr
    for (store, key, _), arr in zip(nw, res[n_io:]):
        store[key] = arr


HBM = pl.BlockSpec(memory_space=pltpu.HBM)
SEM = pl.BlockSpec(memory_space=pltpu.SEMAPHORE)
SPLIT_COPY = pltpu.CompilerParams(has_side_effects=pltpu.SideEffectType.DATAFLOW_SIDE_EFFECTING)


def _first_gather_copies(slot, send_sems, recv_sems):
    x, y, c, chips = _place()
    hk = slot.shape[1] // 2
    mine = slot.at[_chip_id(x, y), _half(c, hk)]
    pairs = []
    for j, (cx, cy) in enumerate(chips):
        theirs = slot.at[_chip_id(cx, cy), _half(c, hk)]
        sems = dict(send_sem=send_sems.at[j], recv_sem=recv_sems.at[j])
        pairs.append((_remote(mine, mine, (cx, cy, c), sems), _remote(theirs, theirs, (cx, cy, c), sems)))
    return pairs


def _gather_first_start(slot):
    def body(slot_ref, send_sems, recv_sems, slot_thru, token):
        for send, _ in _first_gather_copies(slot_ref, send_sems, recv_sems):
            send.start()
        token[...] = jnp.zeros_like(token)

    dma = pltpu.SemaphoreType.DMA
    return pl.pallas_call(
        body, name="gather_first_start",
        out_shape=(dma((3,)), dma((3,)), pltpu.HBM(slot.shape, slot.dtype), jax.ShapeDtypeStruct((SUBLANES, 128), jnp.int32)),
        in_specs=(HBM,), out_specs=(SEM, SEM, HBM, pl.BlockSpec(memory_space=pltpu.VMEM)), input_output_aliases={0: 2},
        compiler_params=SPLIT_COPY,
    )(pltpu.with_memory_space_constraint(slot, pltpu.HBM))


def _gather_first_wait(send_sems, recv_sems, slot, after):
    def body(slot_ref, send_sems, recv_sems, *rest):
        for send, recv in _first_gather_copies(slot_ref, send_sems, recv_sems):
            send.wait_send()
            recv.wait_recv()

    return pl.pallas_call(
        body, name="gather_first_wait", out_shape=(pltpu.HBM(slot.shape, slot.dtype),),
        in_specs=(HBM, SEM, SEM) + (ANY,) * len(after), out_specs=(HBM,), input_output_aliases={0: 0},
        compiler_params=SPLIT_COPY,
    )(slot, send_sems, recv_sems, *after)[0]


def _scatter_copies(src, land, send_sems, recv_sems):
    x, y, c, chips = _place()
    return [_remote(src.at[_chip_id(cx, cy)], land.at[j], (cx, cy, c), dict(send_sem=send_sems.at[j], recv_sem=recv_sems.at[j]))
            for j, (cx, cy) in enumerate(chips)]


def _scatter_start(src):
    land = pltpu.with_memory_space_constraint(lax.empty((3,) + src.shape[1:], src.dtype), pltpu.HBM)

    def body(src_ref, land_ref, send_sems, recv_sems, src_thru, land_thru, token):
        for cp in _scatter_copies(src_ref, land_ref, send_sems, recv_sems):
            cp.start()
        token[...] = jnp.zeros_like(token)

    dma = pltpu.SemaphoreType.DMA
    return pl.pallas_call(
        body, name="reduce_chips_start",
        out_shape=(dma((3,)), dma((3,)), pltpu.HBM(src.shape, src.dtype), pltpu.HBM(land.shape, land.dtype),
                   jax.ShapeDtypeStruct((SUBLANES, 128), F32)),
        in_specs=(HBM, HBM), out_specs=(SEM, SEM, HBM, HBM, pl.BlockSpec(memory_space=pltpu.VMEM)),
        input_output_aliases={0: 2, 1: 3}, compiler_params=SPLIT_COPY,
    )(pltpu.with_memory_space_constraint(src, pltpu.HBM), land)


def _scatter_wait(send_sems, recv_sems, src, land, after):
    def body(src_ref, land_ref, send_sems, recv_sems, after_ref, src_done, land_done):
        for cp in _scatter_copies(src_ref, land_ref, send_sems, recv_sems):
            cp.wait_send()
            cp.wait_recv()

    return pl.pallas_call(
        body, name="reduce_chips_wait", out_shape=(pltpu.HBM(src.shape, src.dtype), pltpu.HBM(land.shape, land.dtype)),
        in_specs=(HBM, HBM, SEM, SEM, ANY), out_specs=(HBM, HBM), input_output_aliases={0: 0, 1: 1}, compiler_params=SPLIT_COPY,
    )(src, land, send_sems, recv_sems, after)[1]


def _gather_ici(ro, io, nw, sems):
    s = io[0]
    x, y, c, chips = _place()
    hk = s.shape[1] // 2
    mine = s.at[_chip_id(x, y), _half(c, hk)]
    pairs = []
    for j, (cx, cy) in enumerate(chips):
        theirs = s.at[_chip_id(cx, cy), _half(c, hk)]
        pairs.append((_remote(mine, mine, (cx, cy, c), sems(j)), _remote(theirs, theirs, (cx, cy, c), sems(j))))
    return pairs


def _gather_d2d(ro, io, nw, sems):
    s = io[0]
    x, y, c, chips = _place()
    hk = s.shape[1] // 2
    sib = (x, y, 1 - c)
    pairs = []
    for j, (cx, cy) in enumerate(chips):
        here = s.at[_chip_id(cx, cy), _half(c, hk)]
        there = s.at[_chip_id(cx, cy), _half(1 - c, hk)]
        pairs.append((_remote(here, here, sib, sems(j)), _remote(there, there, sib, sems(j))))
    return pairs


def _gather_small(ro, io, nw, sems):
    x, y, c, chips = _place()
    return [(_remote(ro[0], nw[0].at[j], (cx, cy, c), sems(j)),) * 2 for j, (cx, cy) in enumerate(chips)]


def _reduce_halves(ro, io, nw, sems):
    x, y, c, _ = _place()
    g = ro[0]
    hk = g.shape[1] // 2
    sib = (x, y, 1 - c)
    return [(_remote(g.at[:, _half(1 - c, hk)], nw[0], sib, sems(0)), _remote(g.at[:, _half(c, hk)], nw[0], sib, sems(0)))]


def _reduce_chips(ro, io, nw, sems):
    x, y, c, chips = _place()
    return [(_remote(ro[0].at[_chip_id(cx, cy)], nw[0].at[j], (cx, cy, c), sems(j)),) * 2 for j, (cx, cy) in enumerate(chips)]


def _reduce_share(layer):
    def maker(ro, io, nw, sems):
        g = io[0]
        x, y, c, _ = _place()
        hk = g.shape[1] // 2
        sib = (x, y, 1 - c)
        mine, theirs = g.at[layer, _half(c, hk)], g.at[layer, _half(1 - c, hk)]
        return [(_remote(mine, mine, sib, sems(0)), _remote(theirs, theirs, sib, sems(0)))]
    return maker


def _own_chip():
    return _chip_id(lax.axis_index("x"), lax.axis_index("y"))


def _other_chip(j):
    x, y = lax.axis_index("x"), lax.axis_index("y")
    return _chip_id(jnp.where(j == 1, x, 1 - x), jnp.where(j == 0, y, 1 - y))


def _rms_inproj_own(x, g_row, win, layer, tm):
    T, D = x.shape
    ns = win.shape[-1]

    def body(x_ref, g_ref, w_ref, p_ref, h_ref):
        xv = x_ref[...]
        h = (xv * _rms_stats(xv) * g_ref[...]).astype(BF16)
        h_ref[...] = h
        p_ref[...] = jnp.dot(h, w_ref[...], preferred_element_type=F32).astype(BF16)

    return pl.pallas_call(
        body, name=f"rms_inproj_own_{layer}", grid=(T // tm,),
        in_specs=[pl.BlockSpec((tm, D), lambda i: (i, 0)),
                  pl.BlockSpec((1, D), lambda i: (0, 0)),
                  pl.BlockSpec((None, D, ns), lambda i: (_own_chip(), 0, 0))],
        out_specs=[pl.BlockSpec((tm, ns), lambda i: (i, _own_chip())),
                   pl.BlockSpec((tm, D), lambda i: (i, 0))],
        out_shape=[jax.ShapeDtypeStruct((T, N_CHIP * ns), BF16), jax.ShapeDtypeStruct((T, D), BF16)],
        compiler_params=_params("parallel"),
    )(x, g_row, win)


def _inproj_rest(h, win, p, layer, tm, comm=None):
    T, D = h.shape
    ns = win.shape[-1]

    def body(h_ref, w_ref, p_in, p_ref):
        p_ref[...] = jnp.dot(h_ref[...], w_ref[...], preferred_element_type=F32).astype(BF16)

    return _call(
        body, comm, name=f"rms_inproj_{layer}", grid=(T // tm, N_CHIP - 1),
        in_specs=[pl.BlockSpec((tm, D), lambda i, j: (i, 0)),
                  pl.BlockSpec((None, D, ns), lambda i, j: (_other_chip(j), 0, 0)), ANY],
        out_specs=[pl.BlockSpec((tm, ns), lambda i, j: (i, _other_chip(j)))],
        out_shape=[jax.ShapeDtypeStruct(p.shape, p.dtype)], aliases={2: 0},
        compiler_params=_params("parallel", "arbitrary"),
    )(h, win, p)[0]


def _mixer_recompute(ca, xa, xb, zprev8, xbprev8, cw_ref, wab_ref, sp):
    row = lambda k: cw_ref[pl.ds(k, 1), :]
    z = ca * xa
    z1 = _shift_down(z, 1, zprev8)
    z2 = _shift_down(z, 2, zprev8)
    cz = row(2) * z + row(1) * z1 + row(0) * z2
    x1 = _shift_down(xb, 1, xbprev8)
    x2 = _shift_down(xb, 2, xbprev8)
    x3 = _shift_down(xb, 3, xbprev8)
    u = row(6) * xb + row(5) * x1 + row(4) * x2 + row(3) * x3 + row(7)
    ub = u.astype(BF16)
    nb = wab_ref.shape[0]
    ras, ixs = [], []
    for b in range(nb):
        ri = jnp.dot(ub[:, b * LRU_BLOCK:(b + 1) * LRU_BLOCK], wab_ref[b], preferred_element_type=F32)
        ras.append(ri[:, :LRU_BLOCK])
        ixs.append(ri[:, LRU_BLOCK:])
    r = _sigmoid(jnp.concatenate(ras, axis=1) + row(8))
    gi = _sigmoid(jnp.concatenate(ixs, axis=1) + row(9))
    la = (-LRU_C) * r * sp
    a = jnp.exp(la)
    m = jnp.sqrt(_one_minus_sq(la, a))
    return dict(z=z, z1=z1, z2=z2, cz=cz, x1=x1, x2=x2, x3=x3, u=u, ub=ub, r=r, gi=gi, a=a, m=m)


def _softplus_neg(lam):
    v = -lam
    return jnp.maximum(v, 0.0) + jnp.log1p(jnp.exp(-jnp.abs(v)))


def _mixer_fwd(p, cw, wab, lam_row, layer, tm, comm=None):
    T = p.shape[0]
    D = p.shape[1] // 7
    ngroups = tm // SUBLANES

    def body(ba_ref, ca_ref, xa_ref, xb_ref, gb_ref, cw_ref, wab_ref, lam_ref, ya_ref, yb_ref, h_ref,
             zprev, xbprev, hcarry, a_s, h_s):
        @pl.when(pl.program_id(0) == 0)
        def _():
            zprev[...] = jnp.zeros_like(zprev)
            xbprev[...] = jnp.zeros_like(xbprev)
            hcarry[...] = jnp.zeros_like(hcarry)

        ca = ca_ref[...].astype(F32)
        xa = xa_ref[...].astype(F32)
        xb = xb_ref[...].astype(F32)
        sp = _softplus_neg(lam_ref[...])
        c = _mixer_recompute(ca, xa, xb, zprev[...], xbprev[...], cw_ref, wab_ref, sp)
        zprev[...] = c["z"][tm - SUBLANES:]
        xbprev[...] = xb[tm - SUBLANES:]
        ya_ref[...] = (ba_ref[...].astype(F32) * c["cz"]).astype(BF16)

        a_s[...], h_s[...] = _group_scan(c["a"], c["m"] * c["gi"] * c["u"], reverse=False)

        def step(g, carry):
            off = pl.multiple_of(g * SUBLANES, SUBLANES)
            hg = h_s[pl.ds(off, SUBLANES), :] + a_s[pl.ds(off, SUBLANES), :] * carry
            h_s[pl.ds(off, SUBLANES), :] = hg
            return jnp.broadcast_to(hg[SUBLANES - 1:SUBLANES, :], hg.shape)

        hcarry[...] = lax.fori_loop(0, ngroups, step, hcarry[...], unroll=4)
        h = h_s[...]
        h_ref[...] = h
        gelu, _ = _gelu_parts(gb_ref[...].astype(F32))
        yb_ref[...] = (h * gelu).astype(BF16)

    col = lambda k: pl.BlockSpec((tm, D), lambda i: (i, k))
    full = lambda a: pl.BlockSpec(a.shape, lambda i: (0,) * a.ndim)
    tok = pl.BlockSpec((tm, D), lambda i: (i, 0))
    return _call(
        body, comm, name=f"mixer_fwd_{layer}", grid=(T // tm,),
        in_specs=[col(0), col(1), col(2), col(3), col(4), full(cw), full(wab), full(lam_row)],
        out_specs=[tok, tok, tok],
        out_shape=[jax.ShapeDtypeStruct((T, D), BF16), jax.ShapeDtypeStruct((T, D), BF16), jax.ShapeDtypeStruct((T, D), F32)],
        scratch_shapes=[pltpu.VMEM((SUBLANES, D), F32), pltpu.VMEM((SUBLANES, D), F32), pltpu.VMEM((SUBLANES, D), F32),
                        pltpu.VMEM((tm, D), F32), pltpu.VMEM((tm, D), F32)],
        compiler_params=_params("arbitrary"),
    )(p, p, p, p, p, cw, wab, lam_row)


def _merge_fwd(x, p, ya, yb, woa, wob, wo, gbias, layer, tm, comm=None):
    T, D = x.shape

    def body(x_ref, ga_ref, gb_ref, ya_ref, yb_ref, woa_ref, wob_ref, wo_ref, bias_ref, oa_ref, ob_ref, mg_ref, x1_ref):
        oa = jnp.dot(ya_ref[...], woa_ref[...], preferred_element_type=F32)
        ob = jnp.dot(yb_ref[...], wob_ref[...], preferred_element_type=F32)
        sa = _sigmoid(ga_ref[...].astype(F32) + bias_ref[pl.ds(0, 1), :])
        sb = _sigmoid(gb_ref[...].astype(F32) + bias_ref[pl.ds(1, 1), :])
        mg = (sa * oa + sb * ob).astype(BF16)
        oa_ref[...] = oa.astype(BF16)
        ob_ref[...] = ob.astype(BF16)
        mg_ref[...] = mg
        x1_ref[...] = x_ref[...] + jnp.dot(mg, wo_ref[...], preferred_element_type=F32)

    tok = pl.BlockSpec((tm, D), lambda i: (i, 0))
    wsp = pl.BlockSpec((D, D), lambda i: (0, 0))
    bf = jax.ShapeDtypeStruct((T, D), BF16)
    return _call(
        body, comm, name=f"merge_fwd_{layer}", grid=(T // tm,),
        in_specs=[tok, pl.BlockSpec((tm, D), lambda i: (i, 5)), pl.BlockSpec((tm, D), lambda i: (i, 6)), tok, tok,
                  wsp, wsp, wsp, pl.BlockSpec(gbias.shape, lambda i: (0, 0))],
        out_specs=[tok, tok, tok, tok],
        out_shape=[bf, bf, bf, jax.ShapeDtypeStruct((T, D), F32)],
        compiler_params=_params("parallel"),
    )(x, p, p, ya, yb, woa, wob, wo, gbias)


def _loss_tile(xv, g, tgt):
    d = xv.shape[-1]
    rstd = _rms_stats(xv)
    xhat = xv * rstd
    err = xhat * g - tgt
    dy = err * (1.0 / d)
    dxhat = dy * g
    dx = rstd * (dxhat - xhat * jnp.mean(dxhat * xhat, axis=-1, keepdims=True))
    return dx, _colsum8(err * err), _colsum8(dy * xhat)


def _ffn_fwd(x1, g_row, wg, wu, wd, layer, tm, comm=None):
    T, D = x1.shape
    fs = wg.shape[-2]
    n = T // tm
    nt = (((1,), (1,)), ((), ()))

    def body(x_ref, g_ref, wg_ref, wu_ref, wd_ref, h_ref, gg_ref, uu_ref, x2_ref, acc):
        j = pl.program_id(1)

        @pl.when(j == 0)
        def _():
            xv = x_ref[...]
            h_ref[...] = (xv * _rms_stats(xv) * g_ref[...]).astype(BF16)
            acc[...] = xv

        h = h_ref[...]
        gg = lax.dot_general(h, wg_ref[...], nt, preferred_element_type=F32)
        uu = lax.dot_general(h, wu_ref[...], nt, preferred_element_type=F32)
        gg_ref[...] = gg.astype(BF16)
        uu_ref[...] = uu.astype(BF16)
        act = (gg * _sigmoid(gg) * uu).astype(BF16)
        acc[...] += jnp.dot(act, wd_ref[...], preferred_element_type=F32)

        @pl.when(j == N_CHIP - 1)
        def _():
            x2_ref[...] = acc[...]

    tok = pl.BlockSpec((tm, D), lambda i, j: (i, 0))
    cm = pl.BlockSpec((None, tm, fs), lambda i, j: (j, i, 0))
    wsp = pl.BlockSpec((None, fs, D), lambda i, j: (j, 0, 0))
    return _call(
        body, comm, name=f"ffn_fwd_{layer}", grid=(n, N_CHIP),
        in_specs=[tok, pl.BlockSpec((1, D), lambda i, j: (0, 0)), wsp, wsp, wsp], out_specs=[tok, cm, cm, tok],
        out_shape=[jax.ShapeDtypeStruct((T, D), BF16), jax.ShapeDtypeStruct((N_CHIP, T, fs), BF16),
                   jax.ShapeDtypeStruct((N_CHIP, T, fs), BF16), jax.ShapeDtypeStruct((T, D), F32)],
        scratch_shapes=[pltpu.VMEM((tm, D), F32)], compiler_params=_params("parallel", "arbitrary"),
    )(x1, g_row, wg, wu, wd)


def _final_loss(x, g_row, target, tm):
    T, D = x.shape
    n = T // tm

    def body(x_ref, g_ref, t_ref, dx_ref, red_ref, racc):
        i = pl.program_id(0)

        @pl.when(i == 0)
        def _():
            racc[...] = jnp.zeros_like(racc)

        dx_ref[...], sq, dg = _loss_tile(x_ref[...], g_ref[...], t_ref[...])
        racc[0] += sq
        racc[1] += dg

        @pl.when(i == n - 1)
        def _():
            red_ref[pl.ds(0, 1), :] = jnp.sum(racc[0], axis=0, keepdims=True) * (0.5 / D)
            red_ref[pl.ds(1, 1), :] = jnp.sum(racc[1], axis=0, keepdims=True)

    tok = pl.BlockSpec((tm, D), lambda i: (i, 0))
    return pl.pallas_call(
        body, name="final_loss", grid=(n,),
        in_specs=[tok, pl.BlockSpec((1, D), lambda i: (0, 0)), tok],
        out_specs=[tok, pl.BlockSpec((2, D), lambda i: (0, 0))],
        out_shape=[jax.ShapeDtypeStruct((T, D), F32), jax.ShapeDtypeStruct((2, D), F32)],
        scratch_shapes=[pltpu.VMEM((2, SUBLANES, D), F32)],
        compiler_params=_params("arbitrary"),
    )(x, g_row, target)


def _ffn_bwd_gates(dx2, gg, uu, wd, layer, tm):
    T, D = dx2.shape
    fs = wd.shape[-2]
    nt = (((1,), (1,)), ((), ()))

    def body(dx_ref, gg_ref, uu_ref, wd_ref, dg_ref, du_ref, act_ref, dxb_ref):
        @pl.when(pl.program_id(1) == 0)
        def _():
            dxb_ref[...] = dx_ref[...].astype(BF16)

        dact = lax.dot_general(dxb_ref[...], wd_ref[...], nt, preferred_element_type=F32)
        g = gg_ref[...].astype(F32)
        u = uu_ref[...].astype(F32)
        s = _sigmoid(g)
        silu = g * s
        dg_ref[...] = (dact * u * (s * (1.0 + g * (1.0 - s)))).astype(BF16)
        du_ref[...] = (dact * silu).astype(BF16)
        act_ref[...] = (silu * u).astype(BF16)

    tok = pl.BlockSpec((tm, D), lambda i, j: (i, 0))
    cm = pl.BlockSpec((None, tm, fs), lambda i, j: (j, i, 0))
    cms = jax.ShapeDtypeStruct((N_CHIP, T, fs), BF16)
    return pl.pallas_call(
        body, name=f"ffn_bwd_gates_{layer}", grid=(T // tm, N_CHIP),
        in_specs=[tok, cm, cm, pl.BlockSpec((None, fs, D), lambda i, j: (j, 0, 0))], out_specs=[cm, cm, cm, tok],
        out_shape=[cms, cms, cms, jax.ShapeDtypeStruct((T, D), BF16)],
        compiler_params=_params("parallel", "arbitrary"),
    )(dx2, gg, uu, wd)


def _ffn_bwd(dgg, duu, wg, wu, dx2, x1, g_row, layer, tm, comm=None):
    T, D = dx2.shape
    fs = wg.shape[-2]
    n = T // tm

    def body(dg_ref, du_ref, wg_ref, wu_ref, dx_ref, x_ref, g_ref, dx1_ref, red_ref, acc, racc):
        i = pl.program_id(0)
        j = pl.program_id(1)

        @pl.when((i == 0) & (j == 0))
        def _():
            racc[...] = jnp.zeros_like(racc)

        @pl.when(j == 0)
        def _():
            acc[...] = jnp.zeros_like(acc)

        acc[...] += (jnp.dot(dg_ref[...], wg_ref[...], preferred_element_type=F32)
                     + jnp.dot(du_ref[...], wu_ref[...], preferred_element_type=F32))

        @pl.when(j == N_CHIP - 1)
        def _():
            dx, dgain = _rms_bwd(acc[...], x_ref[...], g_ref[...])
            dx1_ref[...] = dx_ref[...] + dx
            racc[...] += _colsum8(dgain)

        @pl.when((i == n - 1) & (j == N_CHIP - 1))
        def _():
            red_ref[...] = jnp.sum(racc[...], axis=0, keepdims=True)

    tok = pl.BlockSpec((tm, D), lambda i, j: (i, 0))
    cm = pl.BlockSpec((None, tm, fs), lambda i, j: (j, i, 0))
    wsp = pl.BlockSpec((None, fs, D), lambda i, j: (j, 0, 0))
    row = pl.BlockSpec((1, D), lambda i, j: (0, 0))
    return _call(
        body, comm, name=f"ffn_bwd_{layer}", grid=(n, N_CHIP),
        in_specs=[cm, cm, wsp, wsp, tok, tok, row], out_specs=[tok, row],
        out_shape=[jax.ShapeDtypeStruct((T, D), F32), jax.ShapeDtypeStruct((1, D), F32)],
        scratch_shapes=[pltpu.VMEM((tm, D), F32), pltpu.VMEM((SUBLANES, D), F32)],
        compiler_params=_params("arbitrary", "arbitrary"),
    )(dgg, duu, wg, wu, dx2, x1, g_row)


def _merge_bwd(dx1, p, oa, ob, woa, wob, wo, gbias, layer, tm, comm=None):
    T, D = dx1.shape
    n = T // tm
    nt = (((1,), (1,)), ((), ()))

    def body(dx_ref, ga_ref, gb_ref, oa_ref, ob_ref, woa_ref, wob_ref, wo_ref, bias_ref,
             dya_ref, dyb_ref, doa_ref, dob_ref, dgl_ref, dxb_ref, red_ref, racc):
        i = pl.program_id(0)

        @pl.when(i == 0)
        def _():
            racc[...] = jnp.zeros_like(racc)

        dxb = dx_ref[...].astype(BF16)
        dxb_ref[...] = dxb
        dm = lax.dot_general(dxb, wo_ref[...], nt, preferred_element_type=F32)
        sa = _sigmoid(ga_ref[...].astype(F32) + bias_ref[pl.ds(0, 1), :])
        sb = _sigmoid(gb_ref[...].astype(F32) + bias_ref[pl.ds(1, 1), :])
        doa = (dm * sa).astype(BF16)
        dob = (dm * sb).astype(BF16)
        dga = dm * oa_ref[...].astype(F32) * (sa * (1.0 - sa))
        dgb = dm * ob_ref[...].astype(F32) * (sb * (1.0 - sb))
        doa_ref[...] = doa
        dob_ref[...] = dob
        dgl_ref[:, 0:D] = dga.astype(BF16)
        dgl_ref[:, D:2 * D] = dgb.astype(BF16)
        racc[0] += _colsum8(dga)
        racc[1] += _colsum8(dgb)
        dya_ref[...] = lax.dot_general(doa, woa_ref[...], nt, preferred_element_type=F32).astype(BF16)
        dyb_ref[...] = lax.dot_general(dob, wob_ref[...], nt, preferred_element_type=F32).astype(BF16)

        @pl.when(i == n - 1)
        def _():
            red_ref[pl.ds(0, 1), :] = jnp.sum(racc[0], axis=0, keepdims=True)
            red_ref[pl.ds(1, 1), :] = jnp.sum(racc[1], axis=0, keepdims=True)

    tok = pl.BlockSpec((tm, D), lambda i: (i, 0))
    wsp = pl.BlockSpec((D, D), lambda i: (0, 0))
    bf = jax.ShapeDtypeStruct((T, D), BF16)
    return _call(
        body, comm, name=f"merge_bwd_{layer}", grid=(n,),
        in_specs=[tok, pl.BlockSpec((tm, D), lambda i: (i, 5)), pl.BlockSpec((tm, D), lambda i: (i, 6)), tok, tok,
                  wsp, wsp, wsp, pl.BlockSpec(gbias.shape, lambda i: (0, 0))],
        out_specs=[tok, tok, tok, tok, pl.BlockSpec((tm, 2 * D), lambda i: (i, 0)), tok, pl.BlockSpec((2, D), lambda i: (0, 0))],
        out_shape=[bf, bf, bf, bf, jax.ShapeDtypeStruct((T, 2 * D), BF16), bf, jax.ShapeDtypeStruct((2, D), F32)],
        scratch_shapes=[pltpu.VMEM((2, SUBLANES, D), F32)],
        compiler_params=_params("arbitrary"),
    )(dx1, p, p, oa, ob, woa, wob, wo, gbias)


N_MIXER_RED = 16


def _mixer_bwd(p, hseq, dya, dyb, dgl, cw, wab, wabt, lam_row, layer, tm, comm=None):
    T = p.shape[0]
    D = p.shape[1] // 7
    n = T // tm
    ngroups = tm // SUBLANES
    nb = wab.shape[0]
    hb = 16
    tn = (((0,), (0,)), ((), ()))

    def body(ba_ref, ca_ref, xa_ref, xb_ref, gb_ref, h_ref, dya_ref, dyb_ref, dgl_ref,
             cap_ref, xap_ref, xbp_ref, hp_ref, ban_ref, dyan_ref,
             cw_ref, wab_ref, wabt_ref, lam_ref,
             dp_ref, red_ref, dwab_ref,
             racc, wacc, anext, gnext, dunext, c_s, g_s):
        i = pl.program_id(0)
        first_tile = i == n - 1
        last_tile = i == 0

        @pl.when(i == 0)
        def _():
            racc[...] = jnp.zeros_like(racc)
            wacc[...] = jnp.zeros_like(wacc)
            anext[...] = jnp.zeros_like(anext)
            gnext[...] = jnp.zeros_like(gnext)
            dunext[...] = jnp.zeros_like(dunext)

        keep_prev = jnp.where(first_tile, 0.0, 1.0)
        keep_next = jnp.where(last_tile, 0.0, 1.0)
        ba = ba_ref[...].astype(F32)
        ca = ca_ref[...].astype(F32)
        xa = xa_ref[...].astype(F32)
        xb = xb_ref[...].astype(F32)
        h = h_ref[...]
        dya = dya_ref[...].astype(F32)
        dyb = dyb_ref[...].astype(F32)
        zprev8 = (cap_ref[...].astype(F32) * xap_ref[...].astype(F32))[hb - SUBLANES:] * keep_prev
        xbprev8 = xbp_ref[...].astype(F32)[hb - SUBLANES:] * keep_prev
        hprev8 = hp_ref[...] * keep_prev
        dcznext8 = (dyan_ref[...].astype(F32) * ban_ref[...].astype(F32))[:SUBLANES] * keep_next

        lam = lam_ref[...]
        sp = _softplus_neg(lam)
        c = _mixer_recompute(ca, xa, xb, zprev8, xbprev8, cw_ref, wab_ref, sp)
        row = lambda k: cw_ref[pl.ds(k, 1), :]
        a, m, r, gi, u = c["a"], c["m"], c["r"], c["gi"], c["u"]

        gelu, dgelu = _gelu_parts(gb_ref[...].astype(F32))
        dgb = dyb * h * dgelu
        c_s[...], g_s[...] = _group_scan(_shift_up(a, 1, anext[...]), dyb * gelu, reverse=True)

        def step(k, carry):
            off = pl.multiple_of((ngroups - 1 - k) * SUBLANES, SUBLANES)
            gg = g_s[pl.ds(off, SUBLANES), :] + c_s[pl.ds(off, SUBLANES), :] * carry
            g_s[pl.ds(off, SUBLANES), :] = gg
            return jnp.broadcast_to(gg[0:1, :], gg.shape)

        gnext[...] = lax.fori_loop(0, ngroups, step, gnext[...], unroll=4)
        anext[...] = a[0:SUBLANES]
        g = g_s[...]

        hprev = _shift_down(h, 1, hprev8)
        da = g * hprev
        gm = g * m
        dgi = gm * u
        du = gm * gi
        dmv = g * gi * u
        dla = a * (da - dmv * a / m)
        dra = dla * ((-LRU_C) * sp) * (r * (1.0 - r))
        dix = dgi * (gi * (1.0 - gi))
        racc[10] += _colsum8(dla * r)
        racc[8] += _colsum8(dra)
        racc[9] += _colsum8(dix)
        drab = dra.astype(BF16)
        dixb = dix.astype(BF16)
        ub = c["ub"]
        dus = []
        for b in range(nb):
            sl = slice(b * LRU_BLOCK, (b + 1) * LRU_BLOCK)
            dri = jnp.concatenate([drab[:, sl], dixb[:, sl]], axis=1)
            dus.append(jnp.dot(dri, wabt_ref[b], preferred_element_type=F32))
            wacc[b] += lax.dot_general(ub[:, sl], dri, tn, preferred_element_type=F32)
        du = du + jnp.concatenate(dus, axis=1)

        dun = dunext[...]
        du1 = _shift_up(du, 1, dun)
        du2 = _shift_up(du, 2, dun)
        du3 = _shift_up(du, 3, dun)
        dxb = row(6) * du + row(5) * du1 + row(4) * du2 + row(3) * du3
        dunext[...] = du[0:SUBLANES]
        racc[6] += _colsum8(du * xb)
        racc[5] += _colsum8(du * c["x1"])
        racc[4] += _colsum8(du * c["x2"])
        racc[3] += _colsum8(du * c["x3"])
        racc[7] += _colsum8(du)

        dba = dya * c["cz"]
        dcz = dya * ba
        dcz1 = _shift_up(dcz, 1, dcznext8)
        dcz2 = _shift_up(dcz, 2, dcznext8)
        dz = row(2) * dcz + row(1) * dcz1 + row(0) * dcz2
        racc[2] += _colsum8(dcz * c["z"])
        racc[1] += _colsum8(dcz * c["z1"])
        racc[0] += _colsum8(dcz * c["z2"])

        dp_ref[:, 0:D] = dba.astype(BF16)
        dp_ref[:, D:2 * D] = (dz * xa).astype(BF16)
        dp_ref[:, 2 * D:3 * D] = (dz * ca).astype(BF16)
        dp_ref[:, 3 * D:4 * D] = dxb.astype(BF16)
        dp_ref[:, 4 * D:5 * D] = dgb.astype(BF16)
        dp_ref[:, 5 * D:7 * D] = dgl_ref[...]

        @pl.when(i == n - 1)
        def _():
            dlam_scale = LRU_C * _sigmoid(-lam)
            for k in range(N_MIXER_RED):
                tot = jnp.sum(racc[k], axis=0, keepdims=True)
                red_ref[pl.ds(k, 1), :] = tot * dlam_scale if k == 10 else tot
            dwab_ref[...] = wacc[...]

    rt = lambda i: n - 1 - i
    col = lambda k: pl.BlockSpec((tm, D), lambda i: (rt(i), k))
    tok = pl.BlockSpec((tm, D), lambda i: (rt(i), 0))
    full = lambda a: pl.BlockSpec(a.shape, lambda i: (0,) * a.ndim)
    prev16 = lambda k: pl.BlockSpec((hb, D), lambda i: (jnp.maximum(rt(i) * (tm // hb) - 1, 0), k))
    next16 = lambda k: pl.BlockSpec((hb, D), lambda i: (jnp.minimum((rt(i) + 1) * (tm // hb), T // hb - 1), k))
    hprev = pl.BlockSpec((SUBLANES, D), lambda i: (jnp.maximum(rt(i) * ngroups - 1, 0), 0))
    return _call(
        body, comm, name=f"mixer_bwd_{layer}", grid=(n,),
        in_specs=[col(0), col(1), col(2), col(3), col(4), tok, tok, tok, pl.BlockSpec((tm, 2 * D), lambda i: (rt(i), 0)),
                  prev16(1), prev16(2), prev16(3), hprev, next16(0), next16(0),
                  full(cw), full(wab), full(wabt), full(lam_row)],
        out_specs=[pl.BlockSpec((tm, 7 * D), lambda i: (rt(i), 0)),
                   pl.BlockSpec((N_MIXER_RED, D), lambda i: (0, 0)),
                   pl.BlockSpec((nb, LRU_BLOCK, 2 * LRU_BLOCK), lambda i: (0, 0, 0))],
        out_shape=[jax.ShapeDtypeStruct((T, 7 * D), BF16), jax.ShapeDtypeStruct((N_MIXER_RED, D), F32),
                   jax.ShapeDtypeStruct((nb, LRU_BLOCK, 2 * LRU_BLOCK), F32)],
        scratch_shapes=[pltpu.VMEM((N_MIXER_RED, SUBLANES, D), F32), pltpu.VMEM((nb, LRU_BLOCK, 2 * LRU_BLOCK), F32),
                        pltpu.VMEM((SUBLANES, D), F32), pltpu.VMEM((SUBLANES, D), F32), pltpu.VMEM((SUBLANES, D), F32),
                        pltpu.VMEM((tm, D), F32), pltpu.VMEM((tm, D), F32)],
        compiler_params=_params("arbitrary"),
    )(p, p, p, p, p, hseq, dya, dyb, dgl, p, p, p, hseq, p, dya, cw, wab, wabt, lam_row)


def _inproj_bwd(dp, dx1, x, g_row, win, layer, tm, comm=None):
    T, D = x.shape
    ns = win.shape[-1]
    n = T // tm
    nt = (((1,), (1,)), ((), ()))

    def body(dp_ref, dx_ref, x_ref, g_ref, w_ref, dx0_ref, red_ref, acc, racc):
        i = pl.program_id(0)
        j = pl.program_id(1)

        @pl.when((i == 0) & (j == 0))
        def _():
            racc[...] = jnp.zeros_like(racc)

        @pl.when(j == 0)
        def _():
            acc[...] = jnp.zeros_like(acc)

        acc[...] += lax.dot_general(dp_ref[...], w_ref[...], nt, preferred_element_type=F32)

        @pl.when(j == N_CHIP - 1)
        def _():
            dx, dgain = _rms_bwd(acc[...], x_ref[...], g_ref[...])
            dx0_ref[...] = dx_ref[...] + dx
            racc[...] += _colsum8(dgain)

        @pl.when((i == n - 1) & (j == N_CHIP - 1))
        def _():
            red_ref[...] = jnp.sum(racc[...], axis=0, keepdims=True)

    tok = pl.BlockSpec((tm, D), lambda i, j: (i, 0))
    return _call(
        body, comm, name=f"inproj_bwd_{layer}", grid=(n, N_CHIP),
        in_specs=[pl.BlockSpec((tm, ns), lambda i, j: (i, j)), tok, tok, pl.BlockSpec((1, D), lambda i, j: (0, 0)),
                  pl.BlockSpec((None, D, ns), lambda i, j: (j, 0, 0))],
        out_specs=[tok, pl.BlockSpec((1, D), lambda i, j: (0, 0))],
        out_shape=[jax.ShapeDtypeStruct((T, D), F32), jax.ShapeDtypeStruct((1, D), F32)],
        scratch_shapes=[pltpu.VMEM((tm, D), F32), pltpu.VMEM((SUBLANES, D), F32)],
        compiler_params=_params("arbitrary", "arbitrary"),
    )(dp, dx1, x, g_row, win)


def _wgrad(a, b, name, tk, a_kind="whole", b_kind="whole", nj=1, comm=None):
    T = a.shape[-2]
    width = lambda v, kind: v.shape[-1] // nj if kind == "cols" else v.shape[-1]
    ka, kb = width(a, a_kind), width(b, b_kind)
    nt = T // tk
    tn = (((0,), (0,)), ((), ()))

    def spec(k, kind):
        if kind == "cm":
            return pl.BlockSpec((None, tk, k), lambda j, t: (j, t, 0))
        if kind == "cols":
            return pl.BlockSpec((tk, k), lambda j, t: (t, j))
        return pl.BlockSpec((tk, k), lambda j, t: (t, 0))

    def body(a_ref, b_ref, o_ref, ob_ref):
        t = pl.program_id(1)

        @pl.when(t == 0)
        def _():
            o_ref[...] = jnp.zeros_like(o_ref)

        o_ref[...] += lax.dot_general(a_ref[...], b_ref[...], tn, preferred_element_type=F32)

        @pl.when(t == nt - 1)
        def _():
            ob_ref[...] = o_ref[...].astype(BF16)

    o_spec = pl.BlockSpec((None, ka, kb), lambda j, t: (j, 0, 0))
    return _call(
        body, comm, name=name, grid=(nj, nt),
        in_specs=[spec(ka, a_kind), spec(kb, b_kind)], out_specs=[o_spec, o_spec],
        out_shape=[jax.ShapeDtypeStruct((nj, ka, kb), F32), jax.ShapeDtypeStruct((nj, ka, kb), BF16)],
        compiler_params=_params("parallel", "arbitrary"),
    )(a, b)


def _wgrad_pair(a, b1, b2, name, tk, comm=None):
    T, ka = a.shape
    nj, _, kb = b1.shape
    nt = T // tk
    tn = (((0,), (0,)), ((), ()))

    def body(a_ref, b1_ref, b2_ref, o1_ref, o1b_ref, o2_ref, o2b_ref):
        t = pl.program_id(1)

        @pl.when(t == 0)
        def _():
            o1_ref[...] = jnp.zeros_like(o1_ref)
            o2_ref[...] = jnp.zeros_like(o2_ref)

        av = a_ref[...]
        o1_ref[...] += lax.dot_general(b1_ref[...], av, tn, preferred_element_type=F32)
        o2_ref[...] += lax.dot_general(b2_ref[...], av, tn, preferred_element_type=F32)

        @pl.when(t == nt - 1)
        def _():
            o1b_ref[...] = o1_ref[...].astype(BF16)
            o2b_ref[...] = o2_ref[...].astype(BF16)

    b_spec = pl.BlockSpec((None, tk, kb), lambda j, t: (j, t, 0))
    o_spec = pl.BlockSpec((None, kb, ka), lambda j, t: (j, 0, 0))
    f32 = jax.ShapeDtypeStruct((nj, kb, ka), F32)
    b16 = jax.ShapeDtypeStruct((nj, kb, ka), BF16)
    return _call(
        body, comm, name=name, grid=(nj, nt),
        in_specs=[pl.BlockSpec((tk, ka), lambda j, t: (t, 0)), b_spec, b_spec], out_specs=[o_spec] * 4,
        out_shape=[f32, b16, f32, b16], compiler_params=_params("parallel", "arbitrary"),
    )(a, b1, b2)


def _block_diag(w):
    hb = LRU_BLOCK // LRU_HEAD_DIM
    nb = w.shape[0] // hb
    w4 = w.reshape(nb, hb, LRU_HEAD_DIM, LRU_HEAD_DIM)
    eye = jnp.eye(hb, dtype=w.dtype)
    return jnp.einsum("bide,ij->bidje", w4, eye).reshape(nb, LRU_BLOCK, LRU_BLOCK)


def _diag_heads(m):
    hb = LRU_BLOCK // LRU_HEAD_DIM
    nb = m.shape[0]
    m5 = m.reshape(nb, hb, LRU_HEAD_DIM, hb, LRU_HEAD_DIM)
    eye = jnp.eye(hb, dtype=m.dtype)
    return jnp.einsum("bidje,ij->bide", m5, eye).reshape(nb * hb, LRU_HEAD_DIM, LRU_HEAD_DIM)


def _tiles(T):
    cap = lambda n: min(n, T)
    return dict(inproj=cap(1024), mixer=cap(256), merge=cap(512), ffn=cap(1024), ffn_bwd=cap(1024), loss=cap(512), inproj_bwd=cap(1024),
                wgrad_in=cap(2048), wgrad=cap(2048))


class _NoSchedule:
    def carry(self, name):
        return None

    def tie(self, name, operand):
        return operand

    def between_inproj(self, layer, p, small):
        pass

    def after(self, name, result=None):
        pass

    def grad(self, key, layer, f32, b16):
        pass


def _local_step(x, target, W, small, tiles, sched):
    L = small["ln1_g"].shape[0]
    D = x.shape[1]
    square = lambda a: a.reshape(D, D)
    saved = []
    h = x
    for l in range(L):
        ln1_row = small["ln1_g"][l][None]
        ln2_row = small["ln2_g"][l][None]
        p, h1 = _rms_inproj_own(h, sched.tie(f"rms_inproj_own_{l}", ln1_row), W["win", l], l, tiles["inproj"])
        sched.between_inproj(l, p, small)
        p = _inproj_rest(h1, W["win", l], p, l, tiles["inproj"], sched.carry(f"rms_inproj_{l}"))
        cw = jnp.concatenate([small["conv_a_w"][l], small["conv_b_w"][l], small["conv_b_b"][l][None],
                              small["lru_ba"][l][None], small["lru_bx"][l][None]], axis=0)
        wab = jnp.concatenate([_block_diag(small["lru_wa"][l]), _block_diag(small["lru_wx"][l])], axis=2).astype(BF16)
        wabt = jnp.swapaxes(wab, 1, 2)
        lam_row = small["lru_lambda"][l][None]
        ya, yb, hseq = _mixer_fwd(p, cw, wab, lam_row, l, tiles["mixer"], sched.carry(f"mixer_fwd_{l}"))
        oa, ob, mg, x1 = _merge_fwd(h, p, ya, yb, square(W["woa", l]), square(W["wob", l]), square(W["wo", l]),
                                    small["gate_bias"][l], l, tiles["merge"], sched.carry(f"merge_fwd_{l}"))
        h2, gg, uu, x2 = _ffn_fwd(x1, ln2_row, W["wg", l], W["wu", l], W["wd", l], l, tiles["ffn"], sched.carry(f"ffn_fwd_{l}"))
        saved.append(dict(x0=h, p=p, h1=h1, ya=ya, yb=yb, hseq=hseq, oa=oa, ob=ob, mg=mg, x1=x1, h2=h2, gg=gg, uu=uu,
                          cw=cw, wab=wab, wabt=wabt, lam_row=lam_row, ln1_row=ln1_row, ln2_row=ln2_row))
        h = x2

    dx, red = _final_loss(h, small["final_g"][None], target, tiles["loss"])
    loss_row, d_final_g = red[0], red[1]

    gsmall = {k: [None] * L for k in ("ln1_g", "ln2_g", "conv_a_w", "conv_b_w", "conv_b_b", "lru_wa", "lru_ba", "lru_wx",
                                      "lru_bx", "lru_lambda", "gate_bias")}
    tk = tiles["wgrad"]
    for l in reversed(range(L)):
        s = saved[l]
        dgg, duu, act, dx2b = _ffn_bwd_gates(dx, s["gg"], s["uu"], W["wd", l], l, tiles["ffn_bwd"])
        dx1, dln2 = _ffn_bwd(dgg, duu, W["wg", l], W["wu", l], dx, s["x1"], s["ln2_row"], l, tiles["ffn_bwd"],
                             sched.carry(f"ffn_bwd_{l}"))
        sched.after(f"ffn_bwd_{l}")
        gate_up = _wgrad_pair(s["h2"], dgg, duu, f"wgrad_ffn_gate_up_{l}", tk, sched.carry(f"wgrad_ffn_gate_up_{l}"))
        sched.grad("wg", l, *gate_up[0:2])
        sched.grad("wu", l, *gate_up[2:4])
        sched.after(f"wgrad_ffn_gate_up_{l}")
        sched.grad("wd", l, *_wgrad(act, dx2b, f"wgrad_ffn_down_{l}", tk, "cm", "whole", N_CHIP, sched.carry(f"wgrad_ffn_down_{l}")))
        dya, dyb, doa, dob, dgl, dx1b, dgbias = _merge_bwd(dx1, s["p"], s["oa"], s["ob"], square(W["woa", l]), square(W["wob", l]),
                                                         square(W["wo", l]), small["gate_bias"][l], l, tiles["merge"],
                                                         sched.carry(f"merge_bwd_{l}"))
        sched.after(f"merge_bwd_{l}")
        sched.grad("wo", l, *_wgrad(s["mg"], dx1b, f"wgrad_w_o_{l}", tk))
        sched.grad("woa", l, *_wgrad(s["ya"], doa, f"wgrad_w_out_a_{l}", tk))
        sched.grad("wob", l, *_wgrad(s["yb"], dob, f"wgrad_w_out_b_{l}", tk))
        dp, mred, dwab = _mixer_bwd(s["p"], s["hseq"], dya, dyb, dgl, s["cw"], s["wab"], s["wabt"], s["lam_row"], l,
                                    tiles["mixer"], sched.carry(f"mixer_bwd_{l}"))
        sched.after(f"mixer_bwd_{l}")
        sched.grad("win", l, *_wgrad(s["h1"], dp, f"wgrad_w_in_{l}", tiles["wgrad_in"], "whole", "cols", N_CHIP,
                                     sched.carry(f"wgrad_w_in_{l}")))
        sched.after(f"wgrad_w_in_{l}")
        dx, dln1 = _inproj_bwd(dp, dx1, s["x0"], sched.tie(f"inproj_bwd_{l}", s["ln1_row"]), W["win", l], l, tiles["inproj_bwd"],
                               sched.carry(f"inproj_bwd_{l}"))
        sched.after(f"inproj_bwd_{l}", dx)
        gsmall["ln1_g"][l] = dln1[0]
        gsmall["ln2_g"][l] = dln2[0]
        gsmall["conv_a_w"][l] = mred[0:CONV_A_K]
        gsmall["conv_b_w"][l] = mred[CONV_A_K:CONV_A_K + CONV_B_K]
        gsmall["conv_b_b"][l] = mred[7]
        gsmall["lru_ba"][l] = mred[8]
        gsmall["lru_bx"][l] = mred[9]
        gsmall["lru_lambda"][l] = mred[10]
        gsmall["lru_wa"][l] = _diag_heads(dwab[:, :, :LRU_BLOCK])
        gsmall["lru_wx"][l] = _diag_heads(dwab[:, :, LRU_BLOCK:])
        gsmall["gate_bias"][l] = dgbias
    gsmall = {k: jnp.stack(v) for k, v in gsmall.items()}
    gsmall["final_g"] = d_final_g
    return loss_row, dx, gsmall


def _small_allreduce(buf):
    R, C = buf.shape
    n_dev = 8
    rp = R // n_dev
    rel = [(k >> 2 & 1, k >> 1 & 1, k & 1) for k in range(1, n_dev)]

    def body(in_ref, out_ref, recv, s1, r1, s2, r2):
        x, y, c, _ = _place()
        flip = lambda v, bit: 1 - v if bit else v
        peers = [(flip(x, kx), flip(y, ky), flip(c, kc)) for kx, ky, kc in rel]
        dev = lambda p: 4 * p[0] + 2 * p[1] + p[2]
        part = lambda ref, d: ref.at[pl.ds(pl.multiple_of(d * rp, SUBLANES), rp), :]
        me = dev((x, y, c))

        def scatter(k, src_dev, to):
            return pltpu.make_async_remote_copy(src_ref=part(in_ref, dev(to)), dst_ref=recv.at[src_dev], send_sem=s1.at[k],
                                                recv_sem=r1.at[k], device_id=to, device_id_type=MESH)

        def gather(k, src_dev, to):
            return pltpu.make_async_remote_copy(src_ref=part(out_ref, src_dev), dst_ref=part(out_ref, src_dev), send_sem=s2.at[k],
                                                recv_sem=r2.at[k], device_id=to, device_id_type=MESH)

        first = [scatter(k, me, p) for k, p in enumerate(peers)]
        for cp in first:
            cp.start()
        recv[me] = part(in_ref, me)[...]
        for k, p in enumerate(peers):
            scatter(k, dev(p), (x, y, c)).wait_recv()
        total = recv[0]
        for d in range(1, n_dev):
            total = total + recv[d]
        part(out_ref, me)[...] = total
        second = [gather(k, me, p) for k, p in enumerate(peers)]
        for cp in second:
            cp.start()
        for k, p in enumerate(peers):
            gather(k, dev(p), (x, y, c)).wait_recv()
        for cp in first + second:
            cp.wait_send()

    dma = pltpu.SemaphoreType.DMA
    vm = pl.BlockSpec(memory_space=pltpu.VMEM)
    return pl.pallas_call(
        body, name="small_allreduce", out_shape=jax.ShapeDtypeStruct((R, C), buf.dtype),
        in_specs=[vm], out_specs=vm,
        scratch_shapes=[pltpu.VMEM((n_dev, rp, C), buf.dtype), dma((n_dev - 1,)), dma((n_dev - 1,)), dma((n_dev - 1,)), dma((n_dev - 1,))],
    )(buf)


ELEMENTWISE_BLOCK_BYTES = 2 * 1024 * 1024


def _row_block(k, n):
    best = None
    for b in range(16, k + 1, 16):
        if k % b == 0 and b * n * 4 <= ELEMENTWISE_BLOCK_BYTES:
            best = b
    return best or k


def _add_halves(g, recv, place_arr, name):
    nj, hk, N = recv.shape
    bk = _row_block(hk, N)
    nb = hk // bk

    def body(k_ref, g_ref, r_ref, o_ref, ob_ref):
        s = g_ref[...] + r_ref[...].astype(F32)
        ob_ref[...] = s.astype(BF16)

        @pl.when(pl.program_id(1) == k_ref[0])
        def _():
            o_ref[...] = s

    blk = pl.BlockSpec((None, bk, N), lambda i, j, k_ref: (j, i, 0))
    grid_spec = pltpu.PrefetchScalarGridSpec(
        num_scalar_prefetch=1, grid=(nb, nj),
        in_specs=[pl.BlockSpec((None, bk, N), lambda i, j, k_ref: (j, k_ref[1] * nb + i, 0)), blk],
        out_specs=[pl.BlockSpec((bk, N), lambda i, j, k_ref: (i, 0)), blk])
    return pl.pallas_call(
        body, name=name, grid_spec=grid_spec,
        out_shape=[jax.ShapeDtypeStruct((hk, N), F32), jax.ShapeDtypeStruct((nj, hk, N), BF16)],
        compiler_params=_params("parallel", "arbitrary"),
    )(place_arr, g, recv)


def _add_chips(pc, recv, place_arr, layer, n_layers, prev, name):
    hk, N = pc.shape
    bk = _row_block(hk, N)
    nb = hk // bk

    def body(k_ref, p_ref, r0_ref, r1_ref, r2_ref, *rest):
        o_ref = rest[-1]
        o_ref[...] = ((p_ref[...] + r0_ref[...].astype(F32)) + r1_ref[...].astype(F32)) + r2_ref[...].astype(F32)

    rspec = lambda j: pl.BlockSpec((None, bk, N), lambda i, k_ref: (j, i, 0))
    in_specs = [pl.BlockSpec((bk, N), lambda i, k_ref: (i, 0)), rspec(0), rspec(1), rspec(2)]
    operands = [pc, recv, recv, recv]
    aliases = {}
    if prev is not None:
        in_specs.append(ANY)
        operands.append(prev)
        aliases = {5: 0}
    grid_spec = pltpu.PrefetchScalarGridSpec(
        num_scalar_prefetch=1, grid=(nb,), in_specs=in_specs,
        out_specs=pl.BlockSpec((None, bk, N), lambda i, k_ref: (layer, k_ref[1] * nb + i, 0)))
    return pl.pallas_call(
        body, name=name, grid_spec=grid_spec, out_shape=jax.ShapeDtypeStruct((n_layers, 2 * hk, N), F32),
        input_output_aliases=aliases, compiler_params=_params("parallel"),
    )(place_arr, *operands)


def _adamw_math(w, g, m, v):
    m = ADAM_B1 * m + (1.0 - ADAM_B1) * g
    v = ADAM_B2 * v + (1.0 - ADAM_B2) * (g * g)
    m_hat = m / (1.0 - ADAM_B1 ** ADAM_STEP)
    v_hat = v / (1.0 - ADAM_B2 ** ADAM_STEP)
    delta = -ADAM_LR * (m_hat / (jnp.sqrt(v_hat) + ADAM_EPS) + ADAM_WD * w)
    return delta, m, v


def _adamw(w, g, m, v, name):
    L, K, N = w.shape
    bk = _row_block(K, N)

    def body(w_ref, g_ref, m_ref, v_ref, d_ref, nm_ref, nv_ref, go_ref):
        g = g_ref[...]
        d_ref[...], nm_ref[...], nv_ref[...] = _adamw_math(w_ref[...], g, m_ref[...], v_ref[...])
        go_ref[...] = g

    blk = pl.BlockSpec((None, bk, N), lambda l, i: (l, i, 0))
    sds = jax.ShapeDtypeStruct((L, K, N), F32)
    return pl.pallas_call(
        body, name=name, grid=(L, K // bk), in_specs=[blk] * 4, out_specs=[blk] * 4, out_shape=[sds] * 4,
        compiler_params=_params("parallel", "parallel"),
    )(w, g, m, v)


def _adamw_small(ws, gs, ms, vs):
    n = len(ws)

    def body(*refs):
        w, g, m, v, d, nm, nv = (refs[k * n:(k + 1) * n] for k in range(7))
        for k in range(n):
            d[k][...], nm[k][...], nv[k][...] = _adamw_math(w[k][...], g[k][...], m[k][...], v[k][...])

    sds = [jax.ShapeDtypeStruct(a.shape, F32) for a in ws]
    out = pl.pallas_call(body, name="adamw_small", out_shape=sds * 3)(*ws, *gs, *ms, *vs)
    return out[:n], out[n:2 * n], out[2 * n:]


def _cast_bf16(w, layer, place_arr, name):
    _, K, N = w.shape
    bk = _row_block(K, N)

    def body(k_ref, w_ref, o_ref):
        o_ref[...] = w_ref[...].astype(BF16)

    grid_spec = pltpu.PrefetchScalarGridSpec(
        num_scalar_prefetch=1, grid=(K // bk,),
        in_specs=[pl.BlockSpec((None, bk, N), lambda i, k_ref: (layer, i, 0))],
        out_specs=pl.BlockSpec((None, bk, N), lambda i, k_ref: (k_ref[0], i, 0)))
    return pl.pallas_call(
        body, name=name, grid_spec=grid_spec, out_shape=jax.ShapeDtypeStruct((N_CHIP, K, N), BF16),
        compiler_params=_params("parallel"),
    )(place_arr, w)


BIG = ("w_in", "w_out_a", "w_out_b", "w_o", "w_ffn_gate", "w_ffn_up", "w_ffn_down")
BIG_KEY = dict(w_in="win", w_out_a="woa", w_out_b="wob", w_o="wo", w_ffn_gate="wg", w_ffn_up="wu", w_ffn_down="wd")
SHARDED_SMALL = ("conv_a_w", "conv_b_w", "gate_bias")
REPLICATED = ("ln1_g", "conv_b_b", "lru_wa", "lru_ba", "lru_wx", "lru_bx", "lru_lambda", "ln2_g", "final_g")
WEIGHTS = ("ln1_g", "w_in", "conv_a_w", "conv_b_w", "conv_b_b", "lru_wa", "lru_ba", "lru_wx", "lru_bx", "lru_lambda",
           "w_out_a", "w_out_b", "gate_bias", "w_o", "ln2_g", "w_ffn_gate", "w_ffn_up", "w_ffn_down", "final_g")
LANES = 1024


def _pack_rows(arrays, row_multiple):
    flat = jnp.concatenate([a.reshape(-1) for a in arrays])
    rows = -(-flat.shape[0] // LANES)
    rows = -(-rows // row_multiple) * row_multiple
    flat = jnp.pad(flat, (0, rows * LANES - flat.shape[0]))
    return flat.reshape(rows, LANES)


def _unpack_rows(buf, shapes):
    flat = buf.reshape(-1)
    out, off = [], 0
    for s in shapes:
        n = 1
        for d in s:
            n *= d
        out.append(flat[off:off + n].reshape(s))
        off += n
    return out


OUT_KEYS = ("wo", "woa", "wob")
FFN_KEYS = ("wg", "wu", "wd")


def _items(keys, layer):
    return [(k, layer) for k in keys]


CARRY = {
    "rms_inproj_0": [("gather_ici", _items(OUT_KEYS + ("wg", "wu"), 0))],
    "mixer_fwd_0": [("gather_d2d", _items(OUT_KEYS + ("wg", "wu"), 0)), ("gather_ici", [("wd", 0), ("win", 1)])],
    "merge_fwd_0": [("gather_d2d", [("wd", 0)])],
    "ffn_fwd_0": [("gather_d2d", [("win", 1)]), ("gather_ici", _items(OUT_KEYS + FFN_KEYS, 1))],
    "rms_inproj_1": [("gather_d2d", _items(OUT_KEYS + FFN_KEYS, 1))],
    "merge_bwd_1": [("halves", _items(FFN_KEYS, 1))],
    "mixer_bwd_1": [("chips", _items(FFN_KEYS, 1)), ("halves", _items(OUT_KEYS, 1))],
    "inproj_bwd_1": [("chips", _items(OUT_KEYS, 1)), ("share", _items(FFN_KEYS, 1))],
    "ffn_bwd_0": [("halves", [("win", 1)]), ("share", _items(OUT_KEYS, 1))],
    "wgrad_ffn_gate_up_0": [("chips", [("win", 1)])],
    "wgrad_ffn_down_0": [("share", [("win", 1)])],
    "merge_bwd_0": [("halves", _items(FFN_KEYS, 0))],
    "mixer_bwd_0": [("chips", _items(FFN_KEYS, 0)), ("halves", _items(OUT_KEYS, 0))],
    "wgrad_w_in_0": [("chips", _items(OUT_KEYS, 0)), ("share", _items(FFN_KEYS, 0))],
}
AFTER = {
    "merge_bwd_1": [("add_halves", _items(FFN_KEYS, 1))],
    "mixer_bwd_1": [("add_chips", _items(FFN_KEYS, 1)), ("add_halves", _items(OUT_KEYS, 1))],
    "inproj_bwd_1": [("add_chips", _items(OUT_KEYS, 1))],
    "ffn_bwd_0": [("add_halves", [("win", 1)])],
    "wgrad_ffn_gate_up_0": [("add_chips", [("win", 1)])],
    "merge_bwd_0": [("add_halves", _items(FFN_KEYS, 0))],
    "mixer_bwd_0": [("add_chips", _items(FFN_KEYS, 0)), ("add_halves", _items(OUT_KEYS, 0))],
    "wgrad_w_in_0": [("add_chips", _items(OUT_KEYS, 0)),
                     ("run", ("reduce_halves_w_in_0", [("halves", [("win", 0)]), ("share", _items(OUT_KEYS, 0))])),
                     ("add_halves", [("win", 0)]),
                     ("split_start", ("win", 0))],
    "inproj_bwd_0": [("split_wait", ("win", 0)), ("add_chips", [("win", 0)]),
                     ("run", ("reduce_share_w_in_0", [("share", [("win", 0)])]))],
}


SPLIT_BESIDE = {"wgrad_w_in_0": "inproj_bwd_0"}


class _Schedule:
    def __init__(self, slots, place_arr, n_layers):
        self.W = slots
        self.place, self.L = place_arr, n_layers
        self.g32, self.g16 = {}, {}
        self.from_sibling, self.chip_sum, self.chip_sum16, self.from_chips = {}, {}, {}, {}
        self.reduced = {}
        self.split, self.tokens = {}, {}

    def stage(self, comm, kind, items):
        bf = lambda shape: jax.ShapeDtypeStruct(shape, BF16)
        for it in items:
            if kind == "gather_ici":
                comm.add(_gather_ici, 3, io=[(self.W, it)])
            elif kind == "gather_d2d":
                comm.add(_gather_d2d, 3, io=[(self.W, it)])
            elif kind == "halves":
                nj, K, N = self.g16[it].shape
                comm.add(_reduce_halves, 1, ro=[self.g16[it]], nw=[(self.from_sibling, it, bf((nj, K // 2, N)))])
            elif kind == "chips":
                _, hk, N = self.chip_sum16[it].shape
                comm.add(_reduce_chips, 3, ro=[self.chip_sum16[it]], nw=[(self.from_chips, it, bf((3, hk, N)))])
            elif kind == "share":
                comm.add(_reduce_share(it[1]), 1, io=[(self.reduced, it[0])])
        return comm

    def carry(self, name):
        comm = _Carried()
        for kind, items in CARRY.get(name, ()):
            self.stage(comm, kind, items)
        return comm

    def run(self, name, rounds):
        _run_comm([self.stage(_Carried(), kind, items) for kind, items in rounds], name)

    def grad(self, key, layer, f32, b16):
        by_chip = lambda g: g.reshape(N_CHIP, -1, g.shape[-1])
        self.g32[key, layer], self.g16[key, layer] = by_chip(f32), by_chip(b16)

    def add(self, kind, items):
        for key, layer in items:
            it = (key, layer)
            if kind == "add_halves":
                self.chip_sum[it], self.chip_sum16[it] = _add_halves(self.g32[it], self.from_sibling[it], self.place,
                                                                    f"add_halves_{key}_{layer}")
            else:
                self.reduced[key] = _add_chips(self.chip_sum[it], self.from_chips[it], self.place, layer, self.L,
                                               self.reduced.get(key), f"add_chips_{key}_{layer}")

    def between_inproj(self, layer, p, small):
        if layer != 0:
            return
        send_sems, recv_sems, before = self.first_gather
        self.W["win", 0] = _gather_first_wait(send_sems, recv_sems, self.W["win", 0], before + [p])
        w = self.sharded_small
        small_shard = jnp.concatenate([w[n] for n in SHARDED_SMALL], axis=1)
        got = {}
        second = self.stage(_Carried(), "gather_d2d", [("win", 0)])
        second.add(_gather_small, 3, ro=[small_shard], nw=[(got, "small", jax.ShapeDtypeStruct((3,) + small_shard.shape, F32))])
        _run_comm([second], "gather_first_d2d")
        xi, yi = lax.axis_index("x"), lax.axis_index("y")
        small_g = jnp.zeros((N_CHIP,) + small_shard.shape, F32)
        small_g = lax.dynamic_update_index_in_dim(small_g, small_shard, _chip_id(xi, yi), 0)
        for j, (cx, cy) in enumerate([(1 - xi, yi), (xi, 1 - yi), (1 - xi, 1 - yi)]):
            small_g = lax.dynamic_update_index_in_dim(small_g, got["small"][j], _chip_id(cx, cy), 0)
        n_layers, rows, dc = small_shard.shape
        small_full = jnp.transpose(small_g, (1, 2, 0, 3)).reshape(n_layers, rows, N_CHIP * dc)
        off = 0
        for n in SHARDED_SMALL:
            k = w[n].shape[1]
            small[n] = small_full[:, off:off + k]
            off += k

    def tie(self, name, operand):
        token = self.tokens.pop(name, None)
        return operand if token is None else operand + token[0, 0]

    def after(self, name, result=None):
        for kind, items in AFTER.get(name, ()):
            if kind == "run":
                self.run(*items)
            elif kind == "split_start":
                send_sems, recv_sems, src, land, token = _scatter_start(self.chip_sum16[items])
                self.split[items] = (send_sems, recv_sems, src, land)
                self.tokens[SPLIT_BESIDE[name]] = token
            elif kind == "split_wait":
                self.from_chips[items] = _scatter_wait(*self.split.pop(items), result)
            else:
                self.add(kind, items)


def _step(w, m, v, x, target):
    xi, yi, ci = lax.axis_index("x"), lax.axis_index("y"), lax.axis_index("c")
    chip = _chip_id(xi, yi)
    place_arr = jnp.stack([chip, ci]).astype(jnp.int32)
    L = w["ln1_g"].shape[0]
    assert L == 2
    D = x.shape[1]
    dc = D // N_CHIP

    stored = lambda n, a: jnp.swapaxes(a, 1, 2) if n in ("w_ffn_gate", "w_ffn_up") else a
    slots = {("win", 0): _cast_bf16(w["w_in"], 0, place_arr, "cast_w_in_0")}
    send_sems, recv_sems, slots["win", 0], token = _gather_first_start(slots["win", 0])
    place_after = place_arr + token[0, 0]
    for n in BIG:
        for l in range(L):
            if (BIG_KEY[n], l) not in slots:
                slots[BIG_KEY[n], l] = _cast_bf16(stored(n, w[n]), l, place_after, f"cast_{n}_{l}")
    sched = _Schedule(slots, place_arr, L)
    sched.tokens["rms_inproj_own_0"] = token
    sched.first_gather = (send_sems, recv_sems, [v for k, v in slots.items() if k != ("win", 0)])
    sched.sharded_small = {n: w[n] for n in SHARDED_SMALL}
    small = {n: w[n] for n in REPLICATED}

    loss_row, grad_x, gsmall = _local_step(x, target, sched.W, small, _tiles(x.shape[0]), sched)

    grads = {}

    order = [n for n in WEIGHTS if n not in BIG]
    packed = _pack_rows([gsmall[n] for n in order] + [loss_row], 8 * SUBLANES)
    summed = _small_allreduce(packed)
    parts = _unpack_rows(summed, [gsmall[n].shape for n in order] + [loss_row.shape])
    loss = jnp.sum(parts[-1])
    for n, g in zip(order, parts[:-1]):
        grads[n] = lax.dynamic_slice_in_dim(g, chip * dc, dc, axis=2) if n in SHARDED_SMALL else g

    delta, new_m, new_v = {}, {}, {}
    for n in BIG:
        d, nm, nv, g = _adamw(stored(n, w[n]), sched.reduced[BIG_KEY[n]], stored(n, m[n]), stored(n, v[n]), f"adamw_{n}")
        delta[n], new_m[n], new_v[n], grads[n] = stored(n, d), stored(n, nm), stored(n, nv), stored(n, g)
    for d, arrays in zip((delta, new_m, new_v), _adamw_small(*([d[n] for n in order] for d in (w, grads, m, v)))):
        d.update(zip(order, arrays))
    return loss, grad_x, grads, delta, new_m, new_v


def kernel(x, ln1_g, w_in, conv_a_w, conv_b_w, conv_b_b, lru_wa, lru_ba, lru_wx, lru_bx, lru_lambda, w_out_a, w_out_b, gate_bias, w_o, ln2_g, w_ffn_gate, w_ffn_up, w_ffn_down, final_g, loss_target, m_ln1_g, m_w_in, m_conv_a_w, m_conv_b_w, m_conv_b_b, m_lru_wa, m_lru_ba, m_lru_wx, m_lru_bx, m_lru_lambda, m_w_out_a, m_w_out_b, m_gate_bias, m_w_o, m_ln2_g, m_w_ffn_gate, m_w_ffn_up, m_w_ffn_down, m_final_g, v_ln1_g, v_w_in, v_conv_a_w, v_conv_b_w, v_conv_b_b, v_lru_wa, v_lru_ba, v_lru_wx, v_lru_bx, v_lru_lambda, v_w_out_a, v_w_out_b, v_gate_bias, v_w_o, v_ln2_g, v_w_ffn_gate, v_w_ffn_up, v_w_ffn_down, v_final_g):
    w = dict(ln1_g=ln1_g, w_in=w_in, conv_a_w=conv_a_w, conv_b_w=conv_b_w, conv_b_b=conv_b_b, lru_wa=lru_wa, lru_ba=lru_ba,
             lru_wx=lru_wx, lru_bx=lru_bx, lru_lambda=lru_lambda, w_out_a=w_out_a, w_out_b=w_out_b, gate_bias=gate_bias, w_o=w_o,
             ln2_g=ln2_g, w_ffn_gate=w_ffn_gate, w_ffn_up=w_ffn_up, w_ffn_down=w_ffn_down, final_g=final_g)
    m = dict(ln1_g=m_ln1_g, w_in=m_w_in, conv_a_w=m_conv_a_w, conv_b_w=m_conv_b_w, conv_b_b=m_conv_b_b, lru_wa=m_lru_wa,
             lru_ba=m_lru_ba, lru_wx=m_lru_wx, lru_bx=m_lru_bx, lru_lambda=m_lru_lambda, w_out_a=m_w_out_a, w_out_b=m_w_out_b,
             gate_bias=m_gate_bias, w_o=m_w_o, ln2_g=m_ln2_g, w_ffn_gate=m_w_ffn_gate, w_ffn_up=m_w_ffn_up,
             w_ffn_down=m_w_ffn_down, final_g=m_final_g)
    v = dict(ln1_g=v_ln1_g, w_in=v_w_in, conv_a_w=v_conv_a_w, conv_b_w=v_conv_b_w, conv_b_b=v_conv_b_b, lru_wa=v_lru_wa,
             lru_ba=v_lru_ba, lru_wx=v_lru_wx, lru_bx=v_lru_bx, lru_lambda=v_lru_lambda, w_out_a=v_w_out_a, w_out_b=v_w_out_b,
             gate_bias=v_gate_bias, w_o=v_w_o, ln2_g=v_ln2_g, w_ffn_gate=v_w_ffn_gate, w_ffn_up=v_w_ffn_up,
             w_ffn_down=v_w_ffn_down, final_g=v_final_g)
    loss, grad_x, grads, delta, new_m, new_v = _step(w, m, v, x[0], loss_target[0])
    return (loss, grad_x[None], *[grads[n] for n in WEIGHTS], *[delta[n] for n in WEIGHTS],
            *[new_m[n] for n in WEIGHTS], *[new_v[n] for n in WEIGHTS])
```

```python
import jax
import jax.numpy as jnp
from jax import lax
from jax.experimental import pallas as pl
from jax.experimental.pallas import tpu as pltpu

F32 = jnp.float32
BF16 = jnp.bfloat16
MESH = pl.DeviceIdType.MESH

N_CHIP = 4
RMS_EPS = 1e-6
LRU_C = 8.0
LRU_HEAD_DIM = 64
LRU_BLOCK = 256
CONV_A_K = 3
CONV_B_K = 4
ADAM_LR = 0.001
ADAM_B1 = 0.9
ADAM_B2 = 0.999
ADAM_EPS = 1e-08
ADAM_WD = 0.01
ADAM_STEP = 10
SUBLANES = 8
VMEM_LIMIT = 56 * 1024 * 1024


def _params(*sem):
    return pltpu.CompilerParams(dimension_semantics=sem, vmem_limit_bytes=VMEM_LIMIT)


def _sigmoid(v):
    return 1.0 / (1.0 + jnp.exp(-v))


def _one_minus_sq(la, a):
    return jnp.tanh(-la) * (1.0 + a * a)


def _gelu_parts(v):
    k = 0.7978845608028654
    v2 = v * v
    t = jnp.tanh(k * (v + 0.044715 * v * v2))
    gelu = 0.5 * v * (1.0 + t)
    dgelu = 0.5 * (1.0 + t) + 0.5 * v * (1.0 - t * t) * k * (1.0 + 3 * 0.044715 * v2)
    return gelu, dgelu


def _shift_down(v, k, prev8):
    rolled = pltpu.roll(v, k, 0)
    r8 = lax.broadcasted_iota(jnp.int32, prev8.shape, 0)
    head = jnp.where(r8 < k, pltpu.roll(prev8, k, 0), rolled[0:SUBLANES])
    return jnp.concatenate([head, rolled[SUBLANES:]], axis=0)


def _shift_up(v, k, next8):
    tm = v.shape[0]
    rolled = pltpu.roll(v, tm - k, 0)
    r8 = lax.broadcasted_iota(jnp.int32, next8.shape, 0)
    tail = jnp.where(r8 >= SUBLANES - k, pltpu.roll(next8, SUBLANES - k, 0), rolled[tm - SUBLANES:])
    return jnp.concatenate([rolled[:tm - SUBLANES], tail], axis=0)


def _group_scan(a, b, reverse):
    tm, c = a.shape
    a = a.reshape(tm // SUBLANES, SUBLANES, c)
    b = b.reshape(tm // SUBLANES, SUBLANES, c)
    q = lax.broadcasted_iota(jnp.int32, a.shape, 1)
    for s in (1, 2, 4):
        msk = q < SUBLANES - s if reverse else q >= s
        shift = SUBLANES - s if reverse else s
        b = jnp.where(msk, a * pltpu.roll(b, shift, 1) + b, b)
        a = jnp.where(msk, a * pltpu.roll(a, shift, 1), a)
    return a.reshape(tm, c), b.reshape(tm, c)


def _colsum8(v):
    tm, c = v.shape
    return jnp.sum(v.reshape(tm // SUBLANES, SUBLANES, c), axis=0)


def _rms_stats(xv):
    var = jnp.mean(xv * xv, axis=-1, keepdims=True)
    return lax.rsqrt(var + RMS_EPS)


def _rms_bwd(dh, xv, g):
    rstd = _rms_stats(xv)
    xhat = xv * rstd
    dxhat = dh * g
    dx = rstd * (dxhat - xhat * jnp.mean(dxhat * xhat, axis=-1, keepdims=True))
    return dx, dh * xhat


ANY = pl.BlockSpec(memory_space=pl.ANY)


def _place():
    x, y, c = lax.axis_index("x"), lax.axis_index("y"), lax.axis_index("c")
    other_chips = [(1 - x, y), (x, 1 - y), (1 - x, 1 - y)]
    return x, y, c, other_chips


def _chip_id(x, y):
    return 2 * x + y


def _half(c, hk):
    return pl.ds(pl.multiple_of(c * hk, 16), hk)


def _remote(src, dst, to, sems):
    return pltpu.make_async_remote_copy(src_ref=src, dst_ref=dst, device_id=to, device_id_type=MESH, **sems)


class _Carried:
    def __init__(self):
        self.ro, self.io, self.nw, self.parts, self.n = [], [], [], [], 0

    def add(self, maker, n, ro=(), io=(), nw=()):
        def index(items, item, same):
            for k, other in enumerate(items):
                if same(other, item):
                    return k
            items.append(item)
            return len(items) - 1

        r = [index(self.ro, a, lambda p, q: p is q) for a in ro]
        i = [index(self.io, a, lambda p, q: p[0] is q[0] and p[1] == q[1]) for a in io]
        w = [index(self.nw, a, lambda p, q: False) for a in nw]
        self.parts.append((maker, r, i, w, self.n))
        self.n += n
        return self

    def pairs(self, ro, io, nw, ssem, rsem):
        out = []
        for maker, r, i, w, base in self.parts:
            sems = lambda k, base=base: dict(send_sem=ssem.at[base + k], recv_sem=rsem.at[base + k])
            out += maker([ro[k] for k in r], [io[k] for k in i], [nw[k] for k in w], sems)
        return out

    def start(self, *refs):
        for send, _ in self.pairs(*refs):
            send.start()

    def finish(self, *refs):
        pairs = self.pairs(*refs)
        for _, recv in pairs:
            recv.wait_recv()
        for send, _ in pairs:
            send.wait_send()

    def operands(self):
        return list(self.ro) + [store[key] for store, key in self.io]

    def out_shapes(self):
        return [jax.ShapeDtypeStruct(store[key].shape, store[key].dtype) for store, key in self.io] + [s for _, _, s in self.nw]

    def keep(self, results):
        for (store, key), arr in zip(self.io, results[:len(self.io)]):
            store[key] = arr
        for (store, key, _), arr in zip(self.nw, results[len(self.io):]):
            store[key] = arr


def _call(body, comm, *, name, grid, in_specs, out_specs, out_shape, compiler_params, scratch_shapes=(), aliases=None):
    aliases = dict(aliases or {})
    if comm is None or not comm.parts:
        return pl.pallas_call(body, name=name, grid=grid, in_specs=in_specs, out_specs=out_specs, out_shape=out_shape,
                              scratch_shapes=list(scratch_shapes), input_output_aliases=aliases, compiler_params=compiler_params)
    n_in, n_out, n_scr = len(in_specs), len(out_shape), len(scratch_shapes)
    n_ro, n_io, n_nw = len(comm.ro), len(comm.io), len(comm.nw)

    def carried(*refs):
        base_in = refs[:n_in]
        ro = refs[n_in:n_in + n_ro]
        pos = n_in + n_ro + n_io
        base_out = refs[pos:pos + n_out]
        io = refs[pos + n_out:pos + n_out + n_io]
        nw = refs[pos + n_out + n_io:pos + n_out + n_io + n_nw]
        pos += n_out + n_io + n_nw
        scr = refs[pos:pos + n_scr]
        ssem, rsem = refs[pos + n_scr], refs[pos + n_scr + 1]
        first = pl.program_id(0) == 0
        last = pl.program_id(0) == grid[0] - 1
        for axis in range(1, len(grid)):
            first = first & (pl.program_id(axis) == 0)
            last = last & (pl.program_id(axis) == grid[axis] - 1)

        @pl.when(first)
        def _():
            comm.start(ro, io, nw, ssem, rsem)

        body(*base_in, *base_out, *scr)

        @pl.when(last)
        def _():
            comm.finish(ro, io, nw, ssem, rsem)

    aliases.update({n_in + n_ro + k: n_out + k for k in range(n_io)})
    dma = pltpu.SemaphoreType.DMA
    call = pl.pallas_call(
        carried, name=name, grid=grid,
        in_specs=list(in_specs) + [ANY] * (n_ro + n_io), out_specs=list(out_specs) + [ANY] * (n_io + n_nw),
        out_shape=list(out_shape) + comm.out_shapes(), input_output_aliases=aliases,
        scratch_shapes=list(scratch_shapes) + [dma((comm.n,)), dma((comm.n,))], compiler_params=compiler_params)

    def run(*operands):
        res = call(*operands, *comm.operands())
        comm.keep(res[n_out:])
        return res[:n_out]

    return run


def _run_comm(rounds, name):
    ro, io, nw, uses = [], [], [], []
    for r in rounds:
        def index(items, item, same):
            for k, other in enumerate(items):
                if same(other, item):
                    return k
            items.append(item)
            return len(items) - 1
        uses.append(([index(ro, a, lambda p, q: p is q) for a in r.ro],
                     [index(io, a, lambda p, q: p[0] is q[0] and p[1] == q[1]) for a in r.io],
                     [index(nw, a, lambda p, q: False) for a in r.nw]))
    n_ro, n_io, n_nw = len(ro), len(io), len(nw)

    def body(*refs):
        ro_refs = refs[:n_ro]
        io_refs = refs[n_ro + n_io:n_ro + 2 * n_io]
        nw_refs = refs[n_ro + 2 * n_io:n_ro + 2 * n_io + n_nw]
        sems = refs[n_ro + 2 * n_io + n_nw:]
        for k, (r, (a, b, c)) in enumerate(zip(rounds, uses)):
            args = ([ro_refs[i] for i in a], [io_refs[i] for i in b], [nw_refs[i] for i in c], sems[2 * k], sems[2 * k + 1])
            r.start(*args)
            r.finish(*args)

    operands = ro + [store[key] for store, key in io]
    out_shape = [jax.ShapeDtypeStruct(store[key].shape, store[key].dtype) for store, key in io] + [s for _, _, s in nw]
    dma = pltpu.SemaphoreType.DMA
    res = pl.pallas_call(
        body, name=name, out_shape=out_shape,
        in_specs=[ANY] * (n_ro + n_io), out_specs=[ANY] * (n_io + n_nw),
        input_output_aliases={n_ro + k: k for k in range(n_io)},
        scratch_shapes=[dma((r.n,)) for r in rounds for _ in range(2)],
    )(*operands)
    for (store, key), arr in zip(io, res[:n_io]):
        store[key] = arr
    for (store, key, _), arr in zip(nw, res[n_io:]):
        store[key] = arr


HBM = pl.BlockSpec(memory_space=pltpu.HBM)
SEM = pl.BlockSpec(memory_space=pltpu.SEMAPHORE)
SPLIT_COPY = pltpu.CompilerParams(has_side_effects=pltpu.SideEffectType.DATAFLOW_SIDE_EFFECTING)


def _first_gather_copies(slot, send_sems, recv_sems):
    x, y, c, chips = _place()
    hk = slot.shape[1] // 2
    mine = slot.at[_chip_id(x, y), _half(c, hk)]
    pairs = []
    for j, (cx, cy) in enumerate(chips):
        theirs = slot.at[_chip_id(cx, cy), _half(c, hk)]
        sems = dict(send_sem=send_sems.at[j], recv_sem=recv_sems.at[j])
        pairs.append((_remote(mine, mine, (cx, cy, c), sems), _remote(theirs, theirs, (cx, cy, c), sems)))
    return pairs


def _gather_first_start(slot):
    def body(slot_ref, send_sems, recv_sems, slot_thru, token):
        for send, _ in _first_gather_copies(slot_ref, send_sems, recv_sems):
            send.start()
        token[...] = jnp.zeros_like(token)

    dma = pltpu.SemaphoreType.DMA
    return pl.pallas_call(
        body, name="gather_first_start",
        out_shape=(dma((3,)), dma((3,)), pltpu.HBM(slot.shape, slot.dtype), jax.ShapeDtypeStruct((SUBLANES, 128), jnp.int32)),
        in_specs=(HBM,), out_specs=(SEM, SEM, HBM, pl.BlockSpec(memory_space=pltpu.VMEM)), input_output_aliases={0: 2},
        compiler_params=SPLIT_COPY,
    )(pltpu.with_memory_space_constraint(slot, pltpu.HBM))


def _gather_first_wait(send_sems, recv_sems, slot, after):
    def body(slot_ref, send_sems, recv_sems, *rest):
        for send, recv in _first_gather_copies(slot_ref, send_sems, recv_sems):
            send.wait_send()
            recv.wait_recv()

    return pl.pallas_call(
        body, name="gather_first_wait", out_shape=(pltpu.HBM(slot.shape, slot.dtype),),
        in_specs=(HBM, SEM, SEM) + (ANY,) * len(after), out_specs=(HBM,), input_output_aliases={0: 0},
        compiler_params=SPLIT_COPY,
    )(slot, send_sems, recv_sems, *after)[0]


def _scatter_copies(src, land, send_sems, recv_sems):
    x, y, c, chips = _place()
    return [_remote(src.at[_chip_id(cx, cy)], land.at[j], (cx, cy, c), dict(send_sem=send_sems.at[j], recv_sem=recv_sems.at[j]))
            for j, (cx, cy) in enumerate(chips)]


def _scatter_start(src):
    land = pltpu.with_memory_space_constraint(lax.empty((3,) + src.shape[1:], src.dtype), pltpu.HBM)

    def body(src_ref, land_ref, send_sems, recv_sems, src_thru, land_thru, token):
        for cp in _scatter_copies(src_ref, land_ref, send_sems, recv_sems):
            cp.start()
        token[...] = jnp.zeros_like(token)

    dma = pltpu.SemaphoreType.DMA
    return pl.pallas_call(
        body, name="reduce_chips_start",
        out_shape=(dma((3,)), dma((3,)), pltpu.HBM(src.shape, src.dtype), pltpu.HBM(land.shape, land.dtype),
                   jax.ShapeDtypeStruct((SUBLANES, 128), F32)),
        in_specs=(HBM, HBM), out_specs=(SEM, SEM, HBM, HBM, pl.BlockSpec(memory_space=pltpu.VMEM)),
        input_output_aliases={0: 2, 1: 3}, compiler_params=SPLIT_COPY,
    )(pltpu.with_memory_space_constraint(src, pltpu.HBM), land)


def _scatter_wait(send_sems, recv_sems, src, land, after):
    def body(src_ref, land_ref, send_sems, recv_sems, after_ref, src_done, land_done):
        for cp in _scatter_copies(src_ref, land_ref, send_sems, recv_sems):
            cp.wait_send()
            cp.wait_recv()

    return pl.pallas_call(
        body, name="reduce_chips_wait", out_shape=(pltpu.HBM(src.shape, src.dtype), pltpu.HBM(land.shape, land.dtype)),
        in_specs=(HBM, HBM, SEM, SEM, ANY), out_specs=(HBM, HBM), input_output_aliases={0: 0, 1: 1}, compiler_params=SPLIT_COPY,
    )(src, land, send_sems, recv_sems, after)[1]


def _gather_ici(ro, io, nw, sems):
    s = io[0]
    x, y, c, chips = _place()
    hk = s.shape[1] // 2
    mine = s.at[_chip_id(x, y), _half(c, hk)]
    pairs = []
    for j, (cx, cy) in enumerate(chips):
        theirs = s.at[_chip_id(cx, cy), _half(c, hk)]
        pairs.append((_remote(mine, mine, (cx, cy, c), sems(j)), _remote(theirs, theirs, (cx, cy, c), sems(j))))
    return pairs


def _gather_d2d(ro, io, nw, sems):
    s = io[0]
    x, y, c, chips = _place()
    hk = s.shape[1] // 2
    sib = (x, y, 1 - c)
    pairs = []
    for j, (cx, cy) in enumerate(chips):
        here = s.at[_chip_id(cx, cy), _half(c, hk)]
        there = s.at[_chip_id(cx, cy), _half(1 - c, hk)]
        pairs.append((_remote(here, here, sib, sems(j)), _remote(there, there, sib, sems(j))))
    return pairs


def _gather_small(ro, io, nw, sems):
    x, y, c, chips = _place()
    return [(_remote(ro[0], nw[0].at[j], (cx, cy, c), sems(j)),) * 2 for j, (cx, cy) in enumerate(chips)]


def _reduce_halves(ro, io, nw, sems):
    x, y, c, _ = _place()
    g = ro[0]
    hk = g.shape[1] // 2
    sib = (x, y, 1 - c)
    return [(_remote(g.at[:, _half(1 - c, hk)], nw[0], sib, sems(0)), _remote(g.at[:, _half(c, hk)], nw[0], sib, sems(0)))]


def _reduce_chips(ro, io, nw, sems):
    x, y, c, chips = _place()
    return [(_remote(ro[0].at[_chip_id(cx, cy)], nw[0].at[j], (cx, cy, c), sems(j)),) * 2 for j, (cx, cy) in enumerate(chips)]


def _reduce_share(layer):
    def maker(ro, io, nw, sems):
        g = io[0]
        x, y, c, _ = _place()
        hk = g.shape[1] // 2
        sib = (x, y, 1 - c)
        mine, theirs = g.at[layer, _half(c, hk)], g.at[layer, _half(1 - c, hk)]
        return [(_remote(mine, mine, sib, sems(0)), _remote(theirs, theirs, sib, sems(0)))]
    return maker


def _own_chip():
    return _chip_id(lax.axis_index("x"), lax.axis_index("y"))


def _other_chip(j):
    x, y = lax.axis_index("x"), lax.axis_index("y")
    return _chip_id(jnp.where(j == 1, x, 1 - x), jnp.where(j == 0, y, 1 - y))


def _rms_inproj_own(x, g_row, win, layer, tm):
    T, D = x.shape
    ns = win.shape[-1]

    def body(x_ref, g_ref, w_ref, p_ref, h_ref):
        xv = x_ref[...]
        h = (xv * _rms_stats(xv) * g_ref[...]).astype(BF16)
        h_ref[...] = h
        p_ref[...] = jnp.dot(h, w_ref[...], preferred_element_type=F32).astype(BF16)

    return pl.pallas_call(
        body, name=f"rms_inproj_own_{layer}", grid=(T // tm,),
        in_specs=[pl.BlockSpec((tm, D), lambda i: (i, 0)),
                  pl.BlockSpec((1, D), lambda i: (0, 0)),
                  pl.BlockSpec((None, D, ns), lambda i: (_own_chip(), 0, 0))],
        out_specs=[pl.BlockSpec((tm, ns), lambda i: (i, _own_chip())),
                   pl.BlockSpec((tm, D), lambda i: (i, 0))],
        out_shape=[jax.ShapeDtypeStruct((T, N_CHIP * ns), BF16), jax.ShapeDtypeStruct((T, D), BF16)],
        compiler_params=_params("parallel"),
    )(x, g_row, win)


def _inproj_rest(h, win, p, layer, tm, comm=None):
    T, D = h.shape
    ns = win.shape[-1]

    def body(h_ref, w_ref, p_in, p_ref):
        p_ref[...] = jnp.dot(h_ref[...], w_ref[...], preferred_element_type=F32).astype(BF16)

    return _call(
        body, comm, name=f"rms_inproj_{layer}", grid=(T // tm, N_CHIP - 1),
        in_specs=[pl.BlockSpec((tm, D), lambda i, j: (i, 0)),
                  pl.BlockSpec((None, D, ns), lambda i, j: (_other_chip(j), 0, 0)), ANY],
        out_specs=[pl.BlockSpec((tm, ns), lambda i, j: (i, _other_chip(j)))],
        out_shape=[jax.ShapeDtypeStruct(p.shape, p.dtype)], aliases={2: 0},
        compiler_params=_params("parallel", "arbitrary"),
    )(h, win, p)[0]


def _mixer_recompute(ca, xa, xb, zprev8, xbprev8, cw_ref, wab_ref, sp):
    row = lambda k: cw_ref[pl.ds(k, 1), :]
    z = ca * xa
    z1 = _shift_down(z, 1, zprev8)
    z2 = _shift_down(z, 2, zprev8)
    cz = row(2) * z + row(1) * z1 + row(0) * z2
    x1 = _shift_down(xb, 1, xbprev8)
    x2 = _shift_down(xb, 2, xbprev8)
    x3 = _shift_down(xb, 3, xbprev8)
    u = row(6) * xb + row(5) * x1 + row(4) * x2 + row(3) * x3 + row(7)
    ub = u.astype(BF16)
    nb = wab_ref.shape[0]
    ras, ixs = [], []
    for b in range(nb):
        ri = jnp.dot(ub[:, b * LRU_BLOCK:(b + 1) * LRU_BLOCK], wab_ref[b], preferred_element_type=F32)
        ras.append(ri[:, :LRU_BLOCK])
        ixs.append(ri[:, LRU_BLOCK:])
    r = _sigmoid(jnp.concatenate(ras, axis=1) + row(8))
    gi = _sigmoid(jnp.concatenate(ixs, axis=1) + row(9))
    la = (-LRU_C) * r * sp
    a = jnp.exp(la)
    m = jnp.sqrt(_one_minus_sq(la, a))
    return dict(z=z, z1=z1, z2=z2, cz=cz, x1=x1, x2=x2, x3=x3, u=u, ub=ub, r=r, gi=gi, a=a, m=m)


def _softplus_neg(lam):
    v = -lam
    return jnp.maximum(v, 0.0) + jnp.log1p(jnp.exp(-jnp.abs(v)))


def _mixer_fwd(p, cw, wab, lam_row, layer, tm, comm=None):
    T = p.shape[0]
    D = p.shape[1] // 7
    ngroups = tm // SUBLANES

    def body(ba_ref, ca_ref, xa_ref, xb_ref, gb_ref, cw_ref, wab_ref, lam_ref, ya_ref, yb_ref, h_ref,
             zprev, xbprev, hcarry, a_s, h_s):
        @pl.when(pl.program_id(0) == 0)
        def _():
            zprev[...] = jnp.zeros_like(zprev)
            xbprev[...] = jnp.zeros_like(xbprev)
            hcarry[...] = jnp.zeros_like(hcarry)

        ca = ca_ref[...].astype(F32)
        xa = xa_ref[...].astype(F32)
        xb = xb_ref[...].astype(F32)
        sp = _softplus_neg(lam_ref[...])
        c = _mixer_recompute(ca, xa, xb, zprev[...], xbprev[...], cw_ref, wab_ref, sp)
        zprev[...] = c["z"][tm - SUBLANES:]
        xbprev[...] = xb[tm - SUBLANES:]
        ya_ref[...] = (ba_ref[...].astype(F32) * c["cz"]).astype(BF16)

        a_s[...], h_s[...] = _group_scan(c["a"], c["m"] * c["gi"] * c["u"], reverse=False)

        def step(g, carry):
            off = pl.multiple_of(g * SUBLANES, SUBLANES)
            hg = h_s[pl.ds(off, SUBLANES), :] + a_s[pl.ds(off, SUBLANES), :] * carry
            h_s[pl.ds(off, SUBLANES), :] = hg
            return jnp.broadcast_to(hg[SUBLANES - 1:SUBLANES, :], hg.shape)

        hcarry[...] = lax.fori_loop(0, ngroups, step, hcarry[...], unroll=4)
        h = h_s[...]
        h_ref[...] = h
        gelu, _ = _gelu_parts(gb_ref[...].astype(F32))
        yb_ref[...] = (h * gelu).astype(BF16)

    col = lambda k: pl.BlockSpec((tm, D), lambda i: (i, k))
    full = lambda a: pl.BlockSpec(a.shape, lambda i: (0,) * a.ndim)
    tok = pl.BlockSpec((tm, D), lambda i: (i, 0))
    return _call(
        body, comm, name=f"mixer_fwd_{layer}", grid=(T // tm,),
        in_specs=[col(0), col(1), col(2), col(3), col(4), full(cw), full(wab), full(lam_row)],
        out_specs=[tok, tok, tok],
        out_shape=[jax.ShapeDtypeStruct((T, D), BF16), jax.ShapeDtypeStruct((T, D), BF16), jax.ShapeDtypeStruct((T, D), F32)],
        scratch_shapes=[pltpu.VMEM((SUBLANES, D), F32), pltpu.VMEM((SUBLANES, D), F32), pltpu.VMEM((SUBLANES, D), F32),
                        pltpu.VMEM((tm, D), F32), pltpu.VMEM((tm, D), F32)],
        compiler_params=_params("arbitrary"),
    )(p, p, p, p, p, cw, wab, lam_row)


def _merge_fwd(x, p, ya, yb, woa, wob, wo, gbias, layer, tm, comm=None):
    T, D = x.shape

    def body(x_ref, ga_ref, gb_ref, ya_ref, yb_ref, woa_ref, wob_ref, wo_ref, bias_ref, oa_ref, ob_ref, mg_ref, x1_ref):
        oa = jnp.dot(ya_ref[...], woa_ref[...], preferred_element_type=F32)
        ob = jnp.dot(yb_ref[...], wob_ref[...], preferred_element_type=F32)
        sa = _sigmoid(ga_ref[...].astype(F32) + bias_ref[pl.ds(0, 1), :])
        sb = _sigmoid(gb_ref[...].astype(F32) + bias_ref[pl.ds(1, 1), :])
        mg = (sa * oa + sb * ob).astype(BF16)
        oa_ref[...] = oa.astype(BF16)
        ob_ref[...] = ob.astype(BF16)
        mg_ref[...] = mg
        x1_ref[...] = x_ref[...] + jnp.dot(mg, wo_ref[...], preferred_element_type=F32)

    tok = pl.BlockSpec((tm, D), lambda i: (i, 0))
    wsp = pl.BlockSpec((D, D), lambda i: (0, 0))
    bf = jax.ShapeDtypeStruct((T, D), BF16)
    return _call(
        body, comm, name=f"merge_fwd_{layer}", grid=(T // tm,),
        in_specs=[tok, pl.BlockSpec((tm, D), lambda i: (i, 5)), pl.BlockSpec((tm, D), lambda i: (i, 6)), tok, tok,
                  wsp, wsp, wsp, pl.BlockSpec(gbias.shape, lambda i: (0, 0))],
        out_specs=[tok, tok, tok, tok],
        out_shape=[bf, bf, bf, jax.ShapeDtypeStruct((T, D), F32)],
        compiler_params=_params("parallel"),
    )(x, p, p, ya, yb, woa, wob, wo, gbias)


def _loss_tile(xv, g, tgt):
    d = xv.shape[-1]
    rstd = _rms_stats(xv)
    xhat = xv * rstd
    err = xhat * g - tgt
    dy = err * (1.0 / d)
    dxhat = dy * g
    dx = rstd * (dxhat - xhat * jnp.mean(dxhat * xhat, axis=-1, keepdims=True))
    return dx, _colsum8(err * err), _colsum8(dy * xhat)


def _ffn_fwd(x1, g_row, wg, wu, wd, layer, tm, comm=None):
    T, D = x1.shape
    fs = wg.shape[-2]
    n = T // tm
    nt = (((1,), (1,)), ((), ()))

    def body(x_ref, g_ref, wg_ref, wu_ref, wd_ref, h_ref, gg_ref, uu_ref, x2_ref, acc):
        j = pl.program_id(1)

        @pl.when(j == 0)
        def _():
            xv = x_ref[...]
            h_ref[...] = (xv * _rms_stats(xv) * g_ref[...]).astype(BF16)
            acc[...] = xv

        h = h_ref[...]
        gg = lax.dot_general(h, wg_ref[...], nt, preferred_element_type=F32)
        uu = lax.dot_general(h, wu_ref[...], nt, preferred_element_type=F32)
        gg_ref[...] = gg.astype(BF16)
        uu_ref[...] = uu.astype(BF16)
        act = (gg * _sigmoid(gg) * uu).astype(BF16)
        acc[...] += jnp.dot(act, wd_ref[...], preferred_element_type=F32)

        @pl.when(j == N_CHIP - 1)
        def _():
            x2_ref[...] = acc[...]

    tok = pl.BlockSpec((tm, D), lambda i, j: (i, 0))
    cm = pl.BlockSpec((None, tm, fs), lambda i, j: (j, i, 0))
    wsp = pl.BlockSpec((None, fs, D), lambda i, j: (j, 0, 0))
    return _call(
        body, comm, name=f"ffn_fwd_{layer}", grid=(n, N_CHIP),
        in_specs=[tok, pl.BlockSpec((1, D), lambda i, j: (0, 0)), wsp, wsp, wsp], out_specs=[tok, cm, cm, tok],
        out_shape=[jax.ShapeDtypeStruct((T, D), BF16), jax.ShapeDtypeStruct((N_CHIP, T, fs), BF16),
                   jax.ShapeDtypeStruct((N_CHIP, T, fs), BF16), jax.ShapeDtypeStruct((T, D), F32)],
        scratch_shapes=[pltpu.VMEM((tm, D), F32)], compiler_params=_params("parallel", "arbitrary"),
    )(x1, g_row, wg, wu, wd)


def _final_loss(x, g_row, target, tm):
    T, D = x.shape
    n = T // tm

    def body(x_ref, g_ref, t_ref, dx_ref, red_ref, racc):
        i = pl.program_id(0)

        @pl.when(i == 0)
        def _():
            racc[...] = jnp.zeros_like(racc)

        dx_ref[...], sq, dg = _loss_tile(x_ref[...], g_ref[...], t_ref[...])
        racc[0] += sq
        racc[1] += dg

        @pl.when(i == n - 1)
        def _():
            red_ref[pl.ds(0, 1), :] = jnp.sum(racc[0], axis=0, keepdims=True) * (0.5 / D)
            red_ref[pl.ds(1, 1), :] = jnp.sum(racc[1], axis=0, keepdims=True)

    tok = pl.BlockSpec((tm, D), lambda i: (i, 0))
    return pl.pallas_call(
        body, name="final_loss", grid=(n,),
        in_specs=[tok, pl.BlockSpec((1, D), lambda i: (0, 0)), tok],
        out_specs=[tok, pl.BlockSpec((2, D), lambda i: (0, 0))],
        out_shape=[jax.ShapeDtypeStruct((T, D), F32), jax.ShapeDtypeStruct((2, D), F32)],
        scratch_shapes=[pltpu.VMEM((2, SUBLANES, D), F32)],
        compiler_params=_params("arbitrary"),
    )(x, g_row, target)


def _ffn_bwd_gates(dx2, gg, uu, wd, layer, tm):
    T, D = dx2.shape
    fs = wd.shape[-2]
    nt = (((1,), (1,)), ((), ()))

    def body(dx_ref, gg_ref, uu_ref, wd_ref, dg_ref, du_ref, act_ref, dxb_ref):
        @pl.when(pl.program_id(1) == 0)
        def _():
            dxb_ref[...] = dx_ref[...].astype(BF16)

        dact = lax.dot_general(dxb_ref[...], wd_ref[...], nt, preferred_element_type=F32)
        g = gg_ref[...].astype(F32)
        u = uu_ref[...].astype(F32)
        s = _sigmoid(g)
        silu = g * s
        dg_ref[...] = (dact * u * (s * (1.0 + g * (1.0 - s)))).astype(BF16)
        du_ref[...] = (dact * silu).astype(BF16)
        act_ref[...] = (silu * u).astype(BF16)

    tok = pl.BlockSpec((tm, D), lambda i, j: (i, 0))
    cm = pl.BlockSpec((None, tm, fs), lambda i, j: (j, i, 0))
    cms = jax.ShapeDtypeStruct((N_CHIP, T, fs), BF16)
    return pl.pallas_call(
        body, name=f"ffn_bwd_gates_{layer}", grid=(T // tm, N_CHIP),
        in_specs=[tok, cm, cm, pl.BlockSpec((None, fs, D), lambda i, j: (j, 0, 0))], out_specs=[cm, cm, cm, tok],
        out_shape=[cms, cms, cms, jax.ShapeDtypeStruct((T, D), BF16)],
        compiler_params=_params("parallel", "arbitrary"),
    )(dx2, gg, uu, wd)


def _ffn_bwd(dgg, duu, wg, wu, dx2, x1, g_row, layer, tm, comm=None):
    T, D = dx2.shape
    fs = wg.shape[-2]
    n = T // tm

    def body(dg_ref, du_ref, wg_ref, wu_ref, dx_ref, x_ref, g_ref, dx1_ref, red_ref, acc, racc):
        i = pl.program_id(0)
        j = pl.program_id(1)

        @pl.when((i == 0) & (j == 0))
        def _():
            racc[...] = jnp.zeros_like(racc)

        @pl.when(j == 0)
        def _():
            acc[...] = jnp.zeros_like(acc)

        acc[...] += (jnp.dot(dg_ref[...], wg_ref[...], preferred_element_type=F32)
                     + jnp.dot(du_ref[...], wu_ref[...], preferred_element_type=F32))

        @pl.when(j == N_CHIP - 1)
        def _():
            dx, dgain = _rms_bwd(acc[...], x_ref[...], g_ref[...])
            dx1_ref[...] = dx_ref[...] + dx
            racc[...] += _colsum8(dgain)

        @pl.when((i == n - 1) & (j == N_CHIP - 1))
        def _():
            red_ref[...] = jnp.sum(racc[...], axis=0, keepdims=True)

    tok = pl.BlockSpec((tm, D), lambda i, j: (i, 0))
    cm = pl.BlockSpec((None, tm, fs), lambda i, j: (j, i, 0))
    wsp = pl.BlockSpec((None, fs, D), lambda i, j: (j, 0, 0))
    row = pl.BlockSpec((1, D), lambda i, j: (0, 0))
    return _call(
        body, comm, name=f"ffn_bwd_{layer}", grid=(n, N_CHIP),
        in_specs=[cm, cm, wsp, wsp, tok, tok, row], out_specs=[tok, row],
        out_shape=[jax.ShapeDtypeStruct((T, D), F32), jax.ShapeDtypeStruct((1, D), F32)],
        scratch_shapes=[pltpu.VMEM((tm, D), F32), pltpu.VMEM((SUBLANES, D), F32)],
        compiler_params=_params("arbitrary", "arbitrary"),
    )(dgg, duu, wg, wu, dx2, x1, g_row)


def _merge_bwd(dx1, p, oa, ob, woa, wob, wo, gbias, layer, tm, comm=None):
    T, D = dx1.shape
    n = T // tm
    nt = (((1,), (1,)), ((), ()))

    def body(dx_ref, ga_ref, gb_ref, oa_ref, ob_ref, woa_ref, wob_ref, wo_ref, bias_ref,
             dya_ref, dyb_ref, doa_ref, dob_ref, dgl_ref, dxb_ref, red_ref, racc):
        i = pl.program_id(0)

        @pl.when(i == 0)
        def _():
            racc[...] = jnp.zeros_like(racc)

        dxb = dx_ref[...].astype(BF16)
        dxb_ref[...] = dxb
        dm = lax.dot_general(dxb, wo_ref[...], nt, preferred_element_type=F32)
        sa = _sigmoid(ga_ref[...].astype(F32) + bias_ref[pl.ds(0, 1), :])
        sb = _sigmoid(gb_ref[...].astype(F32) + bias_ref[pl.ds(1, 1), :])
        doa = (dm * sa).astype(BF16)
        dob = (dm * sb).astype(BF16)
        dga = dm * oa_ref[...].astype(F32) * (sa * (1.0 - sa))
        dgb = dm * ob_ref[...].astype(F32) * (sb * (1.0 - sb))
        doa_ref[...] = doa
        dob_ref[...] = dob
        dgl_ref[:, 0:D] = dga.astype(BF16)
        dgl_ref[:, D:2 * D] = dgb.astype(BF16)
        racc[0] += _colsum8(dga)
        racc[1] += _colsum8(dgb)
        dya_ref[...] = lax.dot_general(doa, woa_ref[...], nt, preferred_element_type=F32).astype(BF16)
        dyb_ref[...] = lax.dot_general(dob, wob_ref[...], nt, preferred_element_type=F32).astype(BF16)

        @pl.when(i == n - 1)
        def _():
            red_ref[pl.ds(0, 1), :] = jnp.sum(racc[0], axis=0, keepdims=True)
            red_ref[pl.ds(1, 1), :] = jnp.sum(racc[1], axis=0, keepdims=True)

    tok = pl.BlockSpec((tm, D), lambda i: (i, 0))
    wsp = pl.BlockSpec((D, D), lambda i: (0, 0))
    bf = jax.ShapeDtypeStruct((T, D), BF16)
    return _call(
        body, comm, name=f"merge_bwd_{layer}", grid=(n,),
        in_specs=[tok, pl.BlockSpec((tm, D), lambda i: (i, 5)), pl.BlockSpec((tm, D), lambda i: (i, 6)), tok, tok,
                  wsp, wsp, wsp, pl.BlockSpec(gbias.shape, lambda i: (0, 0))],
        out_specs=[tok, tok, tok, tok, pl.BlockSpec((tm, 2 * D), lambda i: (i, 0)), tok, pl.BlockSpec((2, D), lambda i: (0, 0))],
        out_shape=[bf, bf, bf, bf, jax.ShapeDtypeStruct((T, 2 * D), BF16), bf, jax.ShapeDtypeStruct((2, D), F32)],
        scratch_shapes=[pltpu.VMEM((2, SUBLANES, D), F32)],
        compiler_params=_params("arbitrary"),
    )(dx1, p, p, oa, ob, woa, wob, wo, gbias)


N_MIXER_RED = 16


def _mixer_bwd(p, hseq, dya, dyb, dgl, cw, wab, wabt, lam_row, layer, tm, comm=None):
    T = p.shape[0]
    D = p.shape[1] // 7
    n = T // tm
    ngroups = tm // SUBLANES
    nb = wab.shape[0]
    hb = 16
    tn = (((0,), (0,)), ((), ()))

    def body(ba_ref, ca_ref, xa_ref, xb_ref, gb_ref, h_ref, dya_ref, dyb_ref, dgl_ref,
             cap_ref, xap_ref, xbp_ref, hp_ref, ban_ref, dyan_ref,
             cw_ref, wab_ref, wabt_ref, lam_ref,
             dp_ref, red_ref, dwab_ref,
             racc, wacc, anext, gnext, dunext, c_s, g_s):
        i = pl.program_id(0)
        first_tile = i == n - 1
        last_tile = i == 0

        @pl.when(i == 0)
        def _():
            racc[...] = jnp.zeros_like(racc)
            wacc[...] = jnp.zeros_like(wacc)
            anext[...] = jnp.zeros_like(anext)
            gnext[...] = jnp.zeros_like(gnext)
            dunext[...] = jnp.zeros_like(dunext)

        keep_prev = jnp.where(first_tile, 0.0, 1.0)
        keep_next = jnp.where(last_tile, 0.0, 1.0)
        ba = ba_ref[...].astype(F32)
        ca = ca_ref[...].astype(F32)
        xa = xa_ref[...].astype(F32)
        xb = xb_ref[...].astype(F32)
        h = h_ref[...]
        dya = dya_ref[...].astype(F32)
        dyb = dyb_ref[...].astype(F32)
        zprev8 = (cap_ref[...].astype(F32) * xap_ref[...].astype(F32))[hb - SUBLANES:] * keep_prev
        xbprev8 = xbp_ref[...].astype(F32)[hb - SUBLANES:] * keep_prev
        hprev8 = hp_ref[...] * keep_prev
        dcznext8 = (dyan_ref[...].astype(F32) * ban_ref[...].astype(F32))[:SUBLANES] * keep_next

        lam = lam_ref[...]
        sp = _softplus_neg(lam)
        c = _mixer_recompute(ca, xa, xb, zprev8, xbprev8, cw_ref, wab_ref, sp)
        row = lambda k: cw_ref[pl.ds(k, 1), :]
        a, m, r, gi, u = c["a"], c["m"], c["r"], c["gi"], c["u"]

        gelu, dgelu = _gelu_parts(gb_ref[...].astype(F32))
        dgb = dyb * h * dgelu
        c_s[...], g_s[...] = _group_scan(_shift_up(a, 1, anext[...]), dyb * gelu, reverse=True)

        def step(k, carry):
            off = pl.multiple_of((ngroups - 1 - k) * SUBLANES, SUBLANES)
            gg = g_s[pl.ds(off, SUBLANES), :] + c_s[pl.ds(off, SUBLANES), :] * carry
            g_s[pl.ds(off, SUBLANES), :] = gg
            return jnp.broadcast_to(gg[0:1, :], gg.shape)

        gnext[...] = lax.fori_loop(0, ngroups, step, gnext[...], unroll=4)
        anext[...] = a[0:SUBLANES]
        g = g_s[...]

        hprev = _shift_down(h, 1, hprev8)
        da = g * hprev
        gm = g * m
        dgi = gm * u
        du = gm * gi
        dmv = g * gi * u
        dla = a * (da - dmv * a / m)
        dra = dla * ((-LRU_C) * sp) * (r * (1.0 - r))
        dix = dgi * (gi * (1.0 - gi))
        racc[10] += _colsum8(dla * r)
        racc[8] += _colsum8(dra)
        racc[9] += _colsum8(dix)
        drab = dra.astype(BF16)
        dixb = dix.astype(BF16)
        ub = c["ub"]
        dus = []
        for b in range(nb):
            sl = slice(b * LRU_BLOCK, (b + 1) * LRU_BLOCK)
            dri = jnp.concatenate([drab[:, sl], dixb[:, sl]], axis=1)
            dus.append(jnp.dot(dri, wabt_ref[b], preferred_element_type=F32))
            wacc[b] += lax.dot_general(ub[:, sl], dri, tn, preferred_element_type=F32)
        du = du + jnp.concatenate(dus, axis=1)

        dun = dunext[...]
        du1 = _shift_up(du, 1, dun)
        du2 = _shift_up(du, 2, dun)
        du3 = _shift_up(du, 3, dun)
        dxb = row(6) * du + row(5) * du1 + row(4) * du2 + row(3) * du3
        dunext[...] = du[0:SUBLANES]
        racc[6] += _colsum8(du * xb)
        racc[5] += _colsum8(du * c["x1"])
        racc[4] += _colsum8(du * c["x2"])
        racc[3] += _colsum8(du * c["x3"])
        racc[7] += _colsum8(du)

        dba = dya * c["cz"]
        dcz = dya * ba
        dcz1 = _shift_up(dcz, 1, dcznext8)
        dcz2 = _shift_up(dcz, 2, dcznext8)
        dz = row(2) * dcz + row(1) * dcz1 + row(0) * dcz2
        racc[2] += _colsum8(dcz * c["z"])
        racc[1] += _colsum8(dcz * c["z1"])
        racc[0] += _colsum8(dcz * c["z2"])

        dp_ref[:, 0:D] = dba.astype(BF16)
        dp_ref[:, D:2 * D] = (dz * xa).astype(BF16)
        dp_ref[:, 2 * D:3 * D] = (dz * ca).astype(BF16)
        dp_ref[:, 3 * D:4 * D] = dxb.astype(BF16)
        dp_ref[:, 4 * D:5 * D] = dgb.astype(BF16)
        dp_ref[:, 5 * D:7 * D] = dgl_ref[...]

        @pl.when(i == n - 1)
        def _():
            dlam_scale = LRU_C * _sigmoid(-lam)
            for k in range(N_MIXER_RED):
                tot = jnp.sum(racc[k], axis=0, keepdims=True)
                red_ref[pl.ds(k, 1), :] = tot * dlam_scale if k == 10 else tot
            dwab_ref[...] = wacc[...]

    rt = lambda i: n - 1 - i
    col = lambda k: pl.BlockSpec((tm, D), lambda i: (rt(i), k))
    tok = pl.BlockSpec((tm, D), lambda i: (rt(i), 0))
    full = lambda a: pl.BlockSpec(a.shape, lambda i: (0,) * a.ndim)
    prev16 = lambda k: pl.BlockSpec((hb, D), lambda i: (jnp.maximum(rt(i) * (tm // hb) - 1, 0), k))
    next16 = lambda k: pl.BlockSpec((hb, D), lambda i: (jnp.minimum((rt(i) + 1) * (tm // hb), T // hb - 1), k))
    hprev = pl.BlockSpec((SUBLANES, D), lambda i: (jnp.maximum(rt(i) * ngroups - 1, 0), 0))
    return _call(
        body, comm, name=f"mixer_bwd_{layer}", grid=(n,),
        in_specs=[col(0), col(1), col(2), col(3), col(4), tok, tok, tok, pl.BlockSpec((tm, 2 * D), lambda i: (rt(i), 0)),
                  prev16(1), prev16(2), prev16(3), hprev, next16(0), next16(0),
                  full(cw), full(wab), full(wabt), full(lam_row)],
        out_specs=[pl.BlockSpec((tm, 7 * D), lambda i: (rt(i), 0)),
                   pl.BlockSpec((N_MIXER_RED, D), lambda i: (0, 0)),
                   pl.BlockSpec((nb, LRU_BLOCK, 2 * LRU_BLOCK), lambda i: (0, 0, 0))],
        out_shape=[jax.ShapeDtypeStruct((T, 7 * D), BF16), jax.ShapeDtypeStruct((N_MIXER_RED, D), F32),
                   jax.ShapeDtypeStruct((nb, LRU_BLOCK, 2 * LRU_BLOCK), F32)],
        scratch_shapes=[pltpu.VMEM((N_MIXER_RED, SUBLANES, D), F32), pltpu.VMEM((nb, LRU_BLOCK, 2 * LRU_BLOCK), F32),
                        pltpu.VMEM((SUBLANES, D), F32), pltpu.VMEM((SUBLANES, D), F32), pltpu.VMEM((SUBLANES, D), F32),
                        pltpu.VMEM((tm, D), F32), pltpu.VMEM((tm, D), F32)],
        compiler_params=_params("arbitrary"),
    )(p, p, p, p, p, hseq, dya, dyb, dgl, p, p, p, hseq, p, dya, cw, wab, wabt, lam_row)


def _inproj_bwd(dp, dx1, x, g_row, win, layer, tm, comm=None):
    T, D = x.shape
    ns = win.shape[-1]
    n = T // tm
    nt = (((1,), (1,)), ((), ()))

    def body(dp_ref, dx_ref, x_ref, g_ref, w_ref, dx0_ref, red_ref, acc, racc):
        i = pl.program_id(0)
        j = pl.program_id(1)

        @pl.when((i == 0) & (j == 0))
        def _():
            racc[...] = jnp.zeros_like(racc)

        @pl.when(j == 0)
        def _():
            acc[...] = jnp.zeros_like(acc)

        acc[...] += lax.dot_general(dp_ref[...], w_ref[...], nt, preferred_element_type=F32)

        @pl.when(j == N_CHIP - 1)
        def _():
            dx, dgain = _rms_bwd(acc[...], x_ref[...], g_ref[...])
            dx0_ref[...] = dx_ref[...] + dx
            racc[...] += _colsum8(dgain)

        @pl.when((i == n - 1) & (j == N_CHIP - 1))
        def _():
            red_ref[...] = jnp.sum(racc[...], axis=0, keepdims=True)

    tok = pl.BlockSpec((tm, D), lambda i, j: (i, 0))
    return _call(
        body, comm, name=f"inproj_bwd_{layer}", grid=(n, N_CHIP),
        in_specs=[pl.BlockSpec((tm, ns), lambda i, j: (i, j)), tok, tok, pl.BlockSpec((1, D), lambda i, j: (0, 0)),
                  pl.BlockSpec((None, D, ns), lambda i, j: (j, 0, 0))],
        out_specs=[tok, pl.BlockSpec((1, D), lambda i, j: (0, 0))],
        out_shape=[jax.ShapeDtypeStruct((T, D), F32), jax.ShapeDtypeStruct((1, D), F32)],
        scratch_shapes=[pltpu.VMEM((tm, D), F32), pltpu.VMEM((SUBLANES, D), F32)],
        compiler_params=_params("arbitrary", "arbitrary"),
    )(dp, dx1, x, g_row, win)


def _wgrad(a, b, name, tk, a_kind="whole", b_kind="whole", nj=1, comm=None):
    T = a.shape[-2]
    width = lambda v, kind: v.shape[-1] // nj if kind == "cols" else v.shape[-1]
    ka, kb = width(a, a_kind), width(b, b_kind)
    nt = T // tk
    tn = (((0,), (0,)), ((), ()))

    def spec(k, kind):
        if kind == "cm":
            return pl.BlockSpec((None, tk, k), lambda j, t: (j, t, 0))
        if kind == "cols":
            return pl.BlockSpec((tk, k), lambda j, t: (t, j))
        return pl.BlockSpec((tk, k), lambda j, t: (t, 0))

    def body(a_ref, b_ref, o_ref, ob_ref):
        t = pl.program_id(1)

        @pl.when(t == 0)
        def _():
            o_ref[...] = jnp.zeros_like(o_ref)

        o_ref[...] += lax.dot_general(a_ref[...], b_ref[...], tn, preferred_element_type=F32)

        @pl.when(t == nt - 1)
        def _():
            ob_ref[...] = o_ref[...].astype(BF16)

    o_spec = pl.BlockSpec((None, ka, kb), lambda j, t: (j, 0, 0))
    return _call(
        body, comm, name=name, grid=(nj, nt),
        in_specs=[spec(ka, a_kind), spec(kb, b_kind)], out_specs=[o_spec, o_spec],
        out_shape=[jax.ShapeDtypeStruct((nj, ka, kb), F32), jax.ShapeDtypeStruct((nj, ka, kb), BF16)],
        compiler_params=_params("parallel", "arbitrary"),
    )(a, b)


def _wgrad_pair(a, b1, b2, name, tk, comm=None):
    T, ka = a.shape
    nj, _, kb = b1.shape
    nt = T // tk
    tn = (((0,), (0,)), ((), ()))

    def body(a_ref, b1_ref, b2_ref, o1_ref, o1b_ref, o2_ref, o2b_ref):
        t = pl.program_id(1)

        @pl.when(t == 0)
        def _():
            o1_ref[...] = jnp.zeros_like(o1_ref)
            o2_ref[...] = jnp.zeros_like(o2_ref)

        av = a_ref[...]
        o1_ref[...] += lax.dot_general(b1_ref[...], av, tn, preferred_element_type=F32)
        o2_ref[...] += lax.dot_general(b2_ref[...], av, tn, preferred_element_type=F32)

        @pl.when(t == nt - 1)
        def _():
            o1b_ref[...] = o1_ref[...].astype(BF16)
            o2b_ref[...] = o2_ref[...].astype(BF16)

    b_spec = pl.BlockSpec((None, tk, kb), lambda j, t: (j, t, 0))
    o_spec = pl.BlockSpec((None, kb, ka), lambda j, t: (j, 0, 0))
    f32 = jax.ShapeDtypeStruct((nj, kb, ka), F32)
    b16 = jax.ShapeDtypeStruct((nj, kb, ka), BF16)
    return _call(
        body, comm, name=name, grid=(nj, nt),
        in_specs=[pl.BlockSpec((tk, ka), lambda j, t: (t, 0)), b_spec, b_spec], out_specs=[o_spec] * 4,
        out_shape=[f32, b16, f32, b16], compiler_params=_params("parallel", "arbitrary"),
    )(a, b1, b2)


def _block_diag(w):
    hb = LRU_BLOCK // LRU_HEAD_DIM
    nb = w.shape[0] // hb
    w4 = w.reshape(nb, hb, LRU_HEAD_DIM, LRU_HEAD_DIM)
    eye = jnp.eye(hb, dtype=w.dtype)
    return jnp.einsum("bide,ij->bidje", w4, eye).reshape(nb, LRU_BLOCK, LRU_BLOCK)


def _diag_heads(m):
    hb = LRU_BLOCK // LRU_HEAD_DIM
    nb = m.shape[0]
    m5 = m.reshape(nb, hb, LRU_HEAD_DIM, hb, LRU_HEAD_DIM)
    eye = jnp.eye(hb, dtype=m.dtype)
    return jnp.einsum("bidje,ij->bide", m5, eye).reshape(nb * hb, LRU_HEAD_DIM, LRU_HEAD_DIM)


def _tiles(T):
    cap = lambda n: min(n, T)
    return dict(inproj=cap(1024), mixer=cap(256), merge=cap(512), ffn=cap(1024), ffn_bwd=cap(1024), loss=cap(512), inproj_bwd=cap(1024),
                wgrad_in=cap(2048), wgrad=cap(2048))


class _NoSchedule:
    def carry(self, name):
        return None

    def tie(self, name, operand):
        return operand

    def between_inproj(self, layer, p, small):
        pass

    def after(self, name, result=None):
        pass

    def grad(self, key, layer, f32, b16):
        pass


def _local_step(x, target, W, small, tiles, sched):
    L = small["ln1_g"].shape[0]
    D = x.shape[1]
    square = lambda a: a.reshape(D, D)
    saved = []
    h = x
    for l in range(L):
        ln1_row = small["ln1_g"][l][None]
        ln2_row = small["ln2_g"][l][None]
        p, h1 = _rms_inproj_own(h, sched.tie(f"rms_inproj_own_{l}", ln1_row), W["win", l], l, tiles["inproj"])
        sched.between_inproj(l, p, small)
        p = _inproj_rest(h1, W["win", l], p, l, tiles["inproj"], sched.carry(f"rms_inproj_{l}"))
        cw = jnp.concatenate([small["conv_a_w"][l], small["conv_b_w"][l], small["conv_b_b"][l][None],
                              small["lru_ba"][l][None], small["lru_bx"][l][None]], axis=0)
        wab = jnp.concatenate([_block_diag(small["lru_wa"][l]), _block_diag(small["lru_wx"][l])], axis=2).astype(BF16)
        wabt = jnp.swapaxes(wab, 1, 2)
        lam_row = small["lru_lambda"][l][None]
        ya, yb, hseq = _mixer_fwd(p, cw, wab, lam_row, l, tiles["mixer"], sched.carry(f"mixer_fwd_{l}"))
        oa, ob, mg, x1 = _merge_fwd(h, p, ya, yb, square(W["woa", l]), square(W["wob", l]), square(W["wo", l]),
                                    small["gate_bias"][l], l, tiles["merge"], sched.carry(f"merge_fwd_{l}"))
        h2, gg, uu, x2 = _ffn_fwd(x1, ln2_row, W["wg", l], W["wu", l], W["wd", l], l, tiles["ffn"], sched.carry(f"ffn_fwd_{l}"))
        saved.append(dict(x0=h, p=p, h1=h1, ya=ya, yb=yb, hseq=hseq, oa=oa, ob=ob, mg=mg, x1=x1, h2=h2, gg=gg, uu=uu,
                          cw=cw, wab=wab, wabt=wabt, lam_row=lam_row, ln1_row=ln1_row, ln2_row=ln2_row))
        h = x2

    dx, red = _final_loss(h, small["final_g"][None], target, tiles["loss"])
    loss_row, d_final_g = red[0], red[1]

    gsmall = {k: [None] * L for k in ("ln1_g", "ln2_g", "conv_a_w", "conv_b_w", "conv_b_b", "lru_wa", "lru_ba", "lru_wx",
                                      "lru_bx", "lru_lambda", "gate_bias")}
    tk = tiles["wgrad"]
    for l in reversed(range(L)):
        s = saved[l]
        dgg, duu, act, dx2b = _ffn_bwd_gates(dx, s["gg"], s["uu"], W["wd", l], l, tiles["ffn_bwd"])
        dx1, dln2 = _ffn_bwd(dgg, duu, W["wg", l], W["wu", l], dx, s["x1"], s["ln2_row"], l, tiles["ffn_bwd"],
                             sched.carry(f"ffn_bwd_{l}"))
        sched.after(f"ffn_bwd_{l}")
        gate_up = _wgrad_pair(s["h2"], dgg, duu, f"wgrad_ffn_gate_up_{l}", tk, sched.carry(f"wgrad_ffn_gate_up_{l}"))
        sched.grad("wg", l, *gate_up[0:2])
        sched.grad("wu", l, *gate_up[2:4])
        sched.after(f"wgrad_ffn_gate_up_{l}")
        sched.grad("wd", l, *_wgrad(act, dx2b, f"wgrad_ffn_down_{l}", tk, "cm", "whole", N_CHIP, sched.carry(f"wgrad_ffn_down_{l}")))
        dya, dyb, doa, dob, dgl, dx1b, dgbias = _merge_bwd(dx1, s["p"], s["oa"], s["ob"], square(W["woa", l]), square(W["wob", l]),
                                                         square(W["wo", l]), small["gate_bias"][l], l, tiles["merge"],
                                                         sched.carry(f"merge_bwd_{l}"))
        sched.after(f"merge_bwd_{l}")
        sched.grad("wo", l, *_wgrad(s["mg"], dx1b, f"wgrad_w_o_{l}", tk))
        sched.grad("woa", l, *_wgrad(s["ya"], doa, f"wgrad_w_out_a_{l}", tk))
        sched.grad("wob", l, *_wgrad(s["yb"], dob, f"wgrad_w_out_b_{l}", tk))
        dp, mred, dwab = _mixer_bwd(s["p"], s["hseq"], dya, dyb, dgl, s["cw"], s["wab"], s["wabt"], s["lam_row"], l,
                                    tiles["mixer"], sched.carry(f"mixer_bwd_{l}"))
        sched.after(f"mixer_bwd_{l}")
        sched.grad("win", l, *_wgrad(s["h1"], dp, f"wgrad_w_in_{l}", tiles["wgrad_in"], "whole", "cols", N_CHIP,
                                     sched.carry(f"wgrad_w_in_{l}")))
        sched.after(f"wgrad_w_in_{l}")
        dx, dln1 = _inproj_bwd(dp, dx1, s["x0"], sched.tie(f"inproj_bwd_{l}", s["ln1_row"]), W["win", l], l, tiles["inproj_bwd"],
                               sched.carry(f"inproj_bwd_{l}"))
        sched.after(f"inproj_bwd_{l}", dx)
        gsmall["ln1_g"][l] = dln1[0]
        gsmall["ln2_g"][l] = dln2[0]
        gsmall["conv_a_w"][l] = mred[0:CONV_A_K]
        gsmall["conv_b_w"][l] = mred[CONV_A_K:CONV_A_K + CONV_B_K]
        gsmall["conv_b_b"][l] = mred[7]
        gsmall["lru_ba"][l] = mred[8]
        gsmall["lru_bx"][l] = mred[9]
        gsmall["lru_lambda"][l] = mred[10]
        gsmall["lru_wa"][l] = _diag_heads(dwab[:, :, :LRU_BLOCK])
        gsmall["lru_wx"][l] = _diag_heads(dwab[:, :, LRU_BLOCK:])
        gsmall["gate_bias"][l] = dgbias
    gsmall = {k: jnp.stack(v) for k, v in gsmall.items()}
    gsmall["final_g"] = d_final_g
    return loss_row, dx, gsmall


def _small_allreduce(buf):
    R, C = buf.shape
    n_dev = 8
    rp = R // n_dev
    rel = [(k >> 2 & 1, k >> 1 & 1, k & 1) for k in range(1, n_dev)]

    def body(in_ref, out_ref, recv, s1, r1, s2, r2):
        x, y, c, _ = _place()
        flip = lambda v, bit: 1 - v if bit else v
        peers = [(flip(x, kx), flip(y, ky), flip(c, kc)) for kx, ky, kc in rel]
        dev = lambda p: 4 * p[0] + 2 * p[1] + p[2]
        part = lambda ref, d: ref.at[pl.ds(pl.multiple_of(d * rp, SUBLANES), rp), :]
        me = dev((x, y, c))

        def scatter(k, src_dev, to):
            return pltpu.make_async_remote_copy(src_ref=part(in_ref, dev(to)), dst_ref=recv.at[src_dev], send_sem=s1.at[k],
                                                recv_sem=r1.at[k], device_id=to, device_id_type=MESH)

        def gather(k, src_dev, to):
            return pltpu.make_async_remote_copy(src_ref=part(out_ref, src_dev), dst_ref=part(out_ref, src_dev), send_sem=s2.at[k],
                                                recv_sem=r2.at[k], device_id=to, device_id_type=MESH)

        first = [scatter(k, me, p) for k, p in enumerate(peers)]
        for cp in first:
            cp.start()
        recv[me] = part(in_ref, me)[...]
        for k, p in enumerate(peers):
            scatter(k, dev(p), (x, y, c)).wait_recv()
        total = recv[0]
        for d in range(1, n_dev):
            total = total + recv[d]
        part(out_ref, me)[...] = total
        second = [gather(k, me, p) for k, p in enumerate(peers)]
        for cp in second:
            cp.start()
        for k, p in enumerate(peers):
            gather(k, dev(p), (x, y, c)).wait_recv()
        for cp in first + second:
            cp.wait_send()

    dma = pltpu.SemaphoreType.DMA
    vm = pl.BlockSpec(memory_space=pltpu.VMEM)
    return pl.pallas_call(
        body, name="small_allreduce", out_shape=jax.ShapeDtypeStruct((R, C), buf.dtype),
        in_specs=[vm], out_specs=vm,
        scratch_shapes=[pltpu.VMEM((n_dev, rp, C), buf.dtype), dma((n_dev - 1,)), dma((n_dev - 1,)), dma((n_dev - 1,)), dma((n_dev - 1,))],
    )(buf)


ELEMENTWISE_BLOCK_BYTES = 2 * 1024 * 1024


def _row_block(k, n):
    best = None
    for b in range(16, k + 1, 16):
        if k % b == 0 and b * n * 4 <= ELEMENTWISE_BLOCK_BYTES:
            best = b
    return best or k


def _add_halves(g, recv, place_arr, name):
    nj, hk, N = recv.shape
    bk = _row_block(hk, N)
    nb = hk // bk

    def body(k_ref, g_ref, r_ref, o_ref, ob_ref):
        s = g_ref[...] + r_ref[...].astype(F32)
        ob_ref[...] = s.astype(BF16)

        @pl.when(pl.program_id(1) == k_ref[0])
        def _():
            o_ref[...] = s

    blk = pl.BlockSpec((None, bk, N), lambda i, j, k_ref: (j, i, 0))
    grid_spec = pltpu.PrefetchScalarGridSpec(
        num_scalar_prefetch=1, grid=(nb, nj),
        in_specs=[pl.BlockSpec((None, bk, N), lambda i, j, k_ref: (j, k_ref[1] * nb + i, 0)), blk],
        out_specs=[pl.BlockSpec((bk, N), lambda i, j, k_ref: (i, 0)), blk])
    return pl.pallas_call(
        body, name=name, grid_spec=grid_spec,
        out_shape=[jax.ShapeDtypeStruct((hk, N), F32), jax.ShapeDtypeStruct((nj, hk, N), BF16)],
        compiler_params=_params("parallel", "arbitrary"),
    )(place_arr, g, recv)


def _add_chips(pc, recv, place_arr, layer, n_layers, prev, name):
    hk, N = pc.shape
    bk = _row_block(hk, N)
    nb = hk // bk

    def body(k_ref, p_ref, r0_ref, r1_ref, r2_ref, *rest):
        o_ref = rest[-1]
        o_ref[...] = ((p_ref[...] + r0_ref[...].astype(F32)) + r1_ref[...].astype(F32)) + r2_ref[...].astype(F32)

    rspec = lambda j: pl.BlockSpec((None, bk, N), lambda i, k_ref: (j, i, 0))
    in_specs = [pl.BlockSpec((bk, N), lambda i, k_ref: (i, 0)), rspec(0), rspec(1), rspec(2)]
    operands = [pc, recv, recv, recv]
    aliases = {}
    if prev is not None:
        in_specs.append(ANY)
        operands.append(prev)
        aliases = {5: 0}
    grid_spec = pltpu.PrefetchScalarGridSpec(
        num_scalar_prefetch=1, grid=(nb,), in_specs=in_specs,
        out_specs=pl.BlockSpec((None, bk, N), lambda i, k_ref: (layer, k_ref[1] * nb + i, 0)))
    return pl.pallas_call(
        body, name=name, grid_spec=grid_spec, out_shape=jax.ShapeDtypeStruct((n_layers, 2 * hk, N), F32),
        input_output_aliases=aliases, compiler_params=_params("parallel"),
    )(place_arr, *operands)


def _adamw_math(w, g, m, v):
    m = ADAM_B1 * m + (1.0 - ADAM_B1) * g
    v = ADAM_B2 * v + (1.0 - ADAM_B2) * (g * g)
    m_hat = m / (1.0 - ADAM_B1 ** ADAM_STEP)
    v_hat = v / (1.0 - ADAM_B2 ** ADAM_STEP)
    delta = -ADAM_LR * (m_hat / (jnp.sqrt(v_hat) + ADAM_EPS) + ADAM_WD * w)
    return delta, m, v


def _adamw(w, g, m, v, name):
    L, K, N = w.shape
    bk = _row_block(K, N)

    def body(w_ref, g_ref, m_ref, v_ref, d_ref, nm_ref, nv_ref, go_ref):
        g = g_ref[...]
        d_ref[...], nm_ref[...], nv_ref[...] = _adamw_math(w_ref[...], g, m_ref[...], v_ref[...])
        go_ref[...] = g

    blk = pl.BlockSpec((None, bk, N), lambda l, i: (l, i, 0))
    sds = jax.ShapeDtypeStruct((L, K, N), F32)
    return pl.pallas_call(
        body, name=name, grid=(L, K // bk), in_specs=[blk] * 4, out_specs=[blk] * 4, out_shape=[sds] * 4,
        compiler_params=_params("parallel", "parallel"),
    )(w, g, m, v)


def _adamw_small(ws, gs, ms, vs):
    n = len(ws)

    def body(*refs):
        w, g, m, v, d, nm, nv = (refs[k * n:(k + 1) * n] for k in range(7))
        for k in range(n):
            d[k][...], nm[k][...], nv[k][...] = _adamw_math(w[k][...], g[k][...], m[k][...], v[k][...])

    sds = [jax.ShapeDtypeStruct(a.shape, F32) for a in ws]
    out = pl.pallas_call(body, name="adamw_small", out_shape=sds * 3)(*ws, *gs, *ms, *vs)
    return out[:n], out[n:2 * n], out[2 * n:]


def _cast_bf16(w, layer, place_arr, name):
    _, K, N = w.shape
    bk = _row_block(K, N)

    def body(k_ref, w_ref, o_ref):
        o_ref[...] = w_ref[...].astype(BF16)

    grid_spec = pltpu.PrefetchScalarGridSpec(
        num_scalar_prefetch=1, grid=(K // bk,),
        in_specs=[pl.BlockSpec((None, bk, N), lambda i, k_ref: (layer, i, 0))],
        out_specs=pl.BlockSpec((None, bk, N), lambda i, k_ref: (k_ref[0], i, 0)))
    return pl.pallas_call(
        body, name=name, grid_spec=grid_spec, out_shape=jax.ShapeDtypeStruct((N_CHIP, K, N), BF16),
        compiler_params=_params("parallel"),
    )(place_arr, w)


BIG = ("w_in", "w_out_a", "w_out_b", "w_o", "w_ffn_gate", "w_ffn_up", "w_ffn_down")
BIG_KEY = dict(w_in="win", w_out_a="woa", w_out_b="wob", w_o="wo", w_ffn_gate="wg", w_ffn_up="wu", w_ffn_down="wd")
SHARDED_SMALL = ("conv_a_w", "conv_b_w", "gate_bias")
REPLICATED = ("ln1_g", "conv_b_b", "lru_wa", "lru_ba", "lru_wx", "lru_bx", "lru_lambda", "ln2_g", "final_g")
WEIGHTS = ("ln1_g", "w_in", "conv_a_w", "conv_b_w", "conv_b_b", "lru_wa", "lru_ba", "lru_wx", "lru_bx", "lru_lambda",
           "w_out_a", "w_out_b", "gate_bias", "w_o", "ln2_g", "w_ffn_gate", "w_ffn_up", "w_ffn_down", "final_g")
LANES = 1024


def _pack_rows(arrays, row_multiple):
    flat = jnp.concatenate([a.reshape(-1) for a in arrays])
    rows = -(-flat.shape[0] // LANES)
    rows = -(-rows // row_multiple) * row_multiple
    flat = jnp.pad(flat, (0, rows * LANES - flat.shape[0]))
    return flat.reshape(rows, LANES)


def _unpack_rows(buf, shapes):
    flat = buf.reshape(-1)
    out, off = [], 0
    for s in shapes:
        n = 1
        for d in s:
            n *= d
        out.append(flat[off:off + n].reshape(s))
        off += n
    return out


OUT_KEYS = ("wo", "woa", "wob")
FFN_KEYS = ("wg", "wu", "wd")


def _items(keys, layer):
    return [(k, layer) for k in keys]


CARRY = {
    "rms_inproj_0": [("gather_ici", _items(OUT_KEYS + ("wg", "wu"), 0))],
    "mixer_fwd_0": [("gather_d2d", _items(OUT_KEYS + ("wg", "wu"), 0)), ("gather_ici", [("wd", 0), ("win", 1)])],
    "merge_fwd_0": [("gather_d2d", [("wd", 0)])],
    "ffn_fwd_0": [("gather_d2d", [("win", 1)]), ("gather_ici", _items(OUT_KEYS + FFN_KEYS, 1))],
    "rms_inproj_1": [("gather_d2d", _items(OUT_KEYS + FFN_KEYS, 1))],
    "merge_bwd_1": [("halves", _items(FFN_KEYS, 1))],
    "mixer_bwd_1": [("chips", _items(FFN_KEYS, 1)), ("halves", _items(OUT_KEYS, 1))],
    "inproj_bwd_1": [("chips", _items(OUT_KEYS, 1)), ("share", _items(FFN_KEYS, 1))],
    "ffn_bwd_0": [("halves", [("win", 1)]), ("share", _items(OUT_KEYS, 1))],
    "wgrad_ffn_gate_up_0": [("chips", [("win", 1)])],
    "merge_bwd_0": [("halves", _items(FFN_KEYS, 0)), ("share", [("win", 1)])],
    "mixer_bwd_0": [("chips", _items(FFN_KEYS, 0)), ("halves", _items(OUT_KEYS, 0))],
    "wgrad_w_in_0": [("chips", _items(OUT_KEYS, 0)), ("share", _items(FFN_KEYS, 0))],
}
AFTER = {
    "merge_bwd_1": [("add_halves", _items(FFN_KEYS, 1))],
    "mixer_bwd_1": [("add_chips", _items(FFN_KEYS, 1)), ("add_halves", _items(OUT_KEYS, 1))],
    "inproj_bwd_1": [("add_chips", _items(OUT_KEYS, 1))],
    "ffn_bwd_0": [("add_halves", [("win", 1)])],
    "wgrad_ffn_gate_up_0": [("add_chips", [("win", 1)])],
    "merge_bwd_0": [("add_halves", _items(FFN_KEYS, 0))],
    "mixer_bwd_0": [("add_chips", _items(FFN_KEYS, 0)), ("add_halves", _items(OUT_KEYS, 0))],
    "wgrad_w_in_0": [("add_chips", _items(OUT_KEYS, 0)),
                     ("run", ("reduce_halves_w_in_0", [("halves", [("win", 0)]), ("share", _items(OUT_KEYS, 0))])),
                     ("add_halves", [("win", 0)]),
                     ("split_start", ("win", 0))],
    "inproj_bwd_0": [("split_wait", ("win", 0)), ("add_chips", [("win", 0)]),
                     ("run", ("reduce_share_w_in_0", [("share", [("win", 0)])]))],
}


SPLIT_BESIDE = {"wgrad_w_in_0": "inproj_bwd_0"}


class _Schedule:
    def __init__(self, slots, place_arr, n_layers):
        self.W = slots
        self.place, self.L = place_arr, n_layers
        self.g32, self.g16 = {}, {}
        self.from_sibling, self.chip_sum, self.chip_sum16, self.from_chips = {}, {}, {}, {}
        self.reduced = {}
        self.split, self.tokens = {}, {}

    def stage(self, comm, kind, items):
        bf = lambda shape: jax.ShapeDtypeStruct(shape, BF16)
        for it in items:
            if kind == "gather_ici":
                comm.add(_gather_ici, 3, io=[(self.W, it)])
            elif kind == "gather_d2d":
                comm.add(_gather_d2d, 3, io=[(self.W, it)])
            elif kind == "halves":
                nj, K, N = self.g16[it].shape
                comm.add(_reduce_halves, 1, ro=[self.g16[it]], nw=[(self.from_sibling, it, bf((nj, K // 2, N)))])
            elif kind == "chips":
                _, hk, N = self.chip_sum16[it].shape
                comm.add(_reduce_chips, 3, ro=[self.chip_sum16[it]], nw=[(self.from_chips, it, bf((3, hk, N)))])
            elif kind == "share":
                comm.add(_reduce_share(it[1]), 1, io=[(self.reduced, it[0])])
        return comm

    def carry(self, name):
        comm = _Carried()
        for kind, items in CARRY.get(name, ()):
            self.stage(comm, kind, items)
        return comm

    def run(self, name, rounds):
        _run_comm([self.stage(_Carried(), kind, items) for kind, items in rounds], name)

    def grad(self, key, layer, f32, b16):
        by_chip = lambda g: g.reshape(N_CHIP, -1, g.shape[-1])
        self.g32[key, layer], self.g16[key, layer] = by_chip(f32), by_chip(b16)

    def add(self, kind, items):
        for key, layer in items:
            it = (key, layer)
            if kind == "add_halves":
                self.chip_sum[it], self.chip_sum16[it] = _add_halves(self.g32[it], self.from_sibling[it], self.place,
                                                                    f"add_halves_{key}_{layer}")
            else:
                self.reduced[key] = _add_chips(self.chip_sum[it], self.from_chips[it], self.place, layer, self.L,
                                               self.reduced.get(key), f"add_chips_{key}_{layer}")

    def between_inproj(self, layer, p, small):
        if layer != 0:
            return
        send_sems, recv_sems, before = self.first_gather
        self.W["win", 0] = _gather_first_wait(send_sems, recv_sems, self.W["win", 0], before + [p])
        w = self.sharded_small
        small_shard = jnp.concatenate([w[n] for n in SHARDED_SMALL], axis=1)
        got = {}
        second = self.stage(_Carried(), "gather_d2d", [("win", 0)])
        second.add(_gather_small, 3, ro=[small_shard], nw=[(got, "small", jax.ShapeDtypeStruct((3,) + small_shard.shape, F32))])
        _run_comm([second], "gather_first_d2d")
        xi, yi = lax.axis_index("x"), lax.axis_index("y")
        small_g = jnp.zeros((N_CHIP,) + small_shard.shape, F32)
        small_g = lax.dynamic_update_index_in_dim(small_g, small_shard, _chip_id(xi, yi), 0)
        for j, (cx, cy) in enumerate([(1 - xi, yi), (xi, 1 - yi), (1 - xi, 1 - yi)]):
            small_g = lax.dynamic_update_index_in_dim(small_g, got["small"][j], _chip_id(cx, cy), 0)
        n_layers, rows, dc = small_shard.shape
        small_full = jnp.transpose(small_g, (1, 2, 0, 3)).reshape(n_layers, rows, N_CHIP * dc)
        off = 0
        for n in SHARDED_SMALL:
            k = w[n].shape[1]
            small[n] = small_full[:, off:off + k]
            off += k

    def tie(self, name, operand):
        token = self.tokens.pop(name, None)
        return operand if token is None else operand + token[0, 0]

    def after(self, name, result=None):
        for kind, items in AFTER.get(name, ()):
            if kind == "run":
                self.run(*items)
            elif kind == "split_start":
                send_sems, recv_sems, src, land, token = _scatter_start(self.chip_sum16[items])
                self.split[items] = (send_sems, recv_sems, src, land)
                self.tokens[SPLIT_BESIDE[name]] = token
            elif kind == "split_wait":
                self.from_chips[items] = _scatter_wait(*self.split.pop(items), result)
            else:
                self.add(kind, items)


def _step(w, m, v, x, target):
    xi, yi, ci = lax.axis_index("x"), lax.axis_index("y"), lax.axis_index("c")
    chip = _chip_id(xi, yi)
    place_arr = jnp.stack([chip, ci]).astype(jnp.int32)
    L = w["ln1_g"].shape[0]
    assert L == 2
    D = x.shape[1]
    dc = D // N_CHIP

    stored = lambda n, a: jnp.swapaxes(a, 1, 2) if n in ("w_ffn_gate", "w_ffn_up") else a
    slots = {("win", 0): _cast_bf16(w["w_in"], 0, place_arr, "cast_w_in_0")}
    send_sems, recv_sems, slots["win", 0], token = _gather_first_start(slots["win", 0])
    place_after = place_arr + token[0, 0]
    for n in BIG:
        for l in range(L):
            if (BIG_KEY[n], l) not in slots:
                slots[BIG_KEY[n], l] = _cast_bf16(stored(n, w[n]), l, place_after, f"cast_{n}_{l}")
    sched = _Schedule(slots, place_arr, L)
    sched.tokens["rms_inproj_own_0"] = token
    sched.first_gather = (send_sems, recv_sems, [v for k, v in slots.items() if k != ("win", 0)])
    sched.sharded_small = {n: w[n] for n in SHARDED_SMALL}
    small = {n: w[n] for n in REPLICATED}

    loss_row, grad_x, gsmall = _local_step(x, target, sched.W, small, _tiles(x.shape[0]), sched)

    grads = {}

    order = [n for n in WEIGHTS if n not in BIG]
    packed = _pack_rows([gsmall[n] for n in order] + [loss_row], 8 * SUBLANES)
    summed = _small_allreduce(packed)
    parts = _unpack_rows(summed, [gsmall[n].shape for n in order] + [loss_row.shape])
    loss = jnp.sum(parts[-1])
    for n, g in zip(order, parts[:-1]):
        grads[n] = lax.dynamic_slice_in_dim(g, chip * dc, dc, axis=2) if n in SHARDED_SMALL else g

    delta, new_m, new_v = {}, {}, {}
    for n in BIG:
        d, nm, nv, g = _adamw(stored(n, w[n]), sched.reduced[BIG_KEY[n]], stored(n, m[n]), stored(n, v[n]), f"adamw_{n}")
        delta[n], new_m[n], new_v[n], grads[n] = stored(n, d), stored(n, nm), stored(n, nv), stored(n, g)
    for d, arrays in zip((delta, new_m, new_v), _adamw_small(*([d[n] for n in order] for d in (w, grads, m, v)))):
        d.update(zip(order, arrays))
    return loss, grad_x, grads, delta, new_m, new_v


def kernel(x, ln1_g, w_in, conv_a_w, conv_b_w, conv_b_b, lru_wa, lru_ba, lru_wx, lru_bx, lru_lambda, w_out_a, w_out_b, gate_bias, w_o, ln2_g, w_ffn_gate, w_ffn_up, w_ffn_down, final_g, loss_target, m_ln1_g, m_w_in, m_conv_a_w, m_conv_b_w, m_conv_b_b, m_lru_wa, m_lru_ba, m_lru_wx, m_lru_bx, m_lru_lambda, m_w_out_a, m_w_out_b, m_gate_bias, m_w_o, m_ln2_g, m_w_ffn_gate, m_w_ffn_up, m_w_ffn_down, m_final_g, v_ln1_g, v_w_in, v_conv_a_w, v_conv_b_w, v_conv_b_b, v_lru_wa, v_lru_ba, v_lru_wx, v_lru_bx, v_lru_lambda, v_w_out_a, v_w_out_b, v_gate_bias, v_w_o, v_ln2_g, v_w_ffn_gate, v_w_ffn_up, v_w_ffn_down, v_final_g):
    w = dict(ln1_g=ln1_g, w_in=w_in, conv_a_w=conv_a_w, conv_b_w=conv_b_w, conv_b_b=conv_b_b, lru_wa=lru_wa, lru_ba=lru_ba,
             lru_wx=lru_wx, lru_bx=lru_bx, lru_lambda=lru_lambda, w_out_a=w_out_a, w_out_b=w_out_b, gate_bias=gate_bias, w_o=w_o,
             ln2_g=ln2_g, w_ffn_gate=w_ffn_gate, w_ffn_up=w_ffn_up, w_ffn_down=w_ffn_down, final_g=final_g)
    m = dict(ln1_g=m_ln1_g, w_in=m_w_in, conv_a_w=m_conv_a_w, conv_b_w=m_conv_b_w, conv_b_b=m_conv_b_b, lru_wa=m_lru_wa,
             lru_ba=m_lru_ba, lru_wx=m_lru_wx, lru_bx=m_lru_bx, lru_lambda=m_lru_lambda, w_out_a=m_w_out_a, w_out_b=m_w_out_b,
             gate_bias=m_gate_bias, w_o=m_w_o, ln2_g=m_ln2_g, w_ffn_gate=m_w_ffn_gate, w_ffn_up=m_w_ffn_up,
             w_ffn_down=m_w_ffn_down, final_g=m_final_g)
    v = dict(ln1_g=v_ln1_g, w_in=v_w_in, conv_a_w=v_conv_a_w, conv_b_w=v_conv_b_w, conv_b_b=v_conv_b_b, lru_wa=v_lru_wa,
             lru_ba=v_lru_ba, lru_wx=v_lru_wx, lru_bx=v_lru_bx, lru_lambda=v_lru_lambda, w_out_a=v_w_out_a, w_out_b=v_w_out_b,
             gate_bias=v_gate_bias, w_o=v_w_o, ln2_g=v_ln2_g, w_ffn_gate=v_w_ffn_gate, w_ffn_up=v_w_ffn_up,
             w_ffn_down=v_w_ffn_down, final_g=v_final_g)
    loss, grad_x, grads, delta, new_m, new_v = _step(w, m, v, x[0], loss_target[0])
    return (loss, grad_x[None], *[grads[n] for n in WEIGHTS], *[delta[n] for n in WEIGHTS],
            *[new_m[n] for n in WEIGHTS], *[new_v[n] for n in WEIGHTS])
```

```python
import jax
import jax.numpy as jnp
from jax import lax
from jax.experimental import pallas as pl
from jax.experimental.pallas import tpu as pltpu

F32 = jnp.float32
BF16 = jnp.bfloat16
MESH = pl.DeviceIdType.MESH

N_CHIP = 4
RMS_EPS = 1e-6
LRU_C = 8.0
LRU_HEAD_DIM = 64
LRU_BLOCK = 256
CONV_A_K = 3
CONV_B_K = 4
ADAM_LR = 0.001
ADAM_B1 = 0.9
ADAM_B2 = 0.999
ADAM_EPS = 1e-08
ADAM_WD = 0.01
ADAM_STEP = 10
SUBLANES = 8
VMEM_LIMIT = 56 * 1024 * 1024


def _params(*sem):
    return pltpu.CompilerParams(dimension_semantics=sem, vmem_limit_bytes=VMEM_LIMIT)


def _sigmoid(v):
    return 1.0 / (1.0 + jnp.exp(-v))


def _one_minus_sq(la, a):
    return jnp.tanh(-la) * (1.0 + a * a)


def _gelu_parts(v):
    k = 0.7978845608028654
    v2 = v * v
    t = jnp.tanh(k * (v + 0.044715 * v * v2))
    gelu = 0.5 * v * (1.0 + t)
    dgelu = 0.5 * (1.0 + t) + 0.5 * v * (1.0 - t * t) * k * (1.0 + 3 * 0.044715 * v2)
    return gelu, dgelu


def _shift_down(v, k, prev8):
    rolled = pltpu.roll(v, k, 0)
    r8 = lax.broadcasted_iota(jnp.int32, prev8.shape, 0)
    head = jnp.where(r8 < k, pltpu.roll(prev8, k, 0), rolled[0:SUBLANES])
    return jnp.concatenate([head, rolled[SUBLANES:]], axis=0)


def _shift_up(v, k, next8):
    tm = v.shape[0]
    rolled = pltpu.roll(v, tm - k, 0)
    r8 = lax.broadcasted_iota(jnp.int32, next8.shape, 0)
    tail = jnp.where(r8 >= SUBLANES - k, pltpu.roll(next8, SUBLANES - k, 0), rolled[tm - SUBLANES:])
    return jnp.concatenate([rolled[:tm - SUBLANES], tail], axis=0)


def _group_scan(a, b, reverse):
    tm, c = a.shape
    a = a.reshape(tm // SUBLANES, SUBLANES, c)
    b = b.reshape(tm // SUBLANES, SUBLANES, c)
    q = lax.broadcasted_iota(jnp.int32, a.shape, 1)
    for s in (1, 2, 4):
        msk = q < SUBLANES - s if reverse else q >= s
        shift = SUBLANES - s if reverse else s
        b = jnp.where(msk, a * pltpu.roll(b, shift, 1) + b, b)
        a = jnp.where(msk, a * pltpu.roll(a, shift, 1), a)
    return a.reshape(tm, c), b.reshape(tm, c)


def _colsum8(v):
    tm, c = v.shape
    return jnp.sum(v.reshape(tm // SUBLANES, SUBLANES, c), axis=0)


def _rms_stats(xv):
    var = jnp.mean(xv * xv, axis=-1, keepdims=True)
    return lax.rsqrt(var + RMS_EPS)


def _rms_bwd(dh, xv, g):
    rstd = _rms_stats(xv)
    xhat = xv * rstd
    dxhat = dh * g
    dx = rstd * (dxhat - xhat * jnp.mean(dxhat * xhat, axis=-1, keepdims=True))
    return dx, dh * xhat


ANY = pl.BlockSpec(memory_space=pl.ANY)


def _place():
    x, y, c = lax.axis_index("x"), lax.axis_index("y"), lax.axis_index("c")
    other_chips = [(1 - x, y), (x, 1 - y), (1 - x, 1 - y)]
    return x, y, c, other_chips


def _chip_id(x, y):
    return 2 * x + y


def _half(c, hk):
    return pl.ds(pl.multiple_of(c * hk, 16), hk)


def _remote(src, dst, to, sems):
    return pltpu.make_async_remote_copy(src_ref=src, dst_ref=dst, device_id=to, device_id_type=MESH, **sems)


class _Carried:
    def __init__(self):
        self.ro, self.io, self.nw, self.parts, self.n = [], [], [], [], 0

    def add(self, maker, n, ro=(), io=(), nw=()):
        def index(items, item, same):
            for k, other in enumerate(items):
                if same(other, item):
                    return k
            items.append(item)
            return len(items) - 1

        r = [index(self.ro, a, lambda p, q: p is q) for a in ro]
        i = [index(self.io, a, lambda p, q: p[0] is q[0] and p[1] == q[1]) for a in io]
        w = [index(self.nw, a, lambda p, q: False) for a in nw]
        self.parts.append((maker, r, i, w, self.n))
        self.n += n
        return self

    def pairs(self, ro, io, nw, ssem, rsem):
        out = []
        for maker, r, i, w, base in self.parts:
            sems = lambda k, base=base: dict(send_sem=ssem.at[base + k], recv_sem=rsem.at[base + k])
            out += maker([ro[k] for k in r], [io[k] for k in i], [nw[k] for k in w], sems)
        return out

    def start(self, *refs):
        for send, _ in self.pairs(*refs):
            send.start()

    def finish(self, *refs):
        pairs = self.pairs(*refs)
        for _, recv in pairs:
            recv.wait_recv()
        for send, _ in pairs:
            send.wait_send()

    def operands(self):
        return list(self.ro) + [store[key] for store, key in self.io]

    def out_shapes(self):
        return [jax.ShapeDtypeStruct(store[key].shape, store[key].dtype) for store, key in self.io] + [s for _, _, s in self.nw]

    def keep(self, results):
        for (store, key), arr in zip(self.io, results[:len(self.io)]):
            store[key] = arr
        for (store, key, _), arr in zip(self.nw, results[len(self.io):]):
            store[key] = arr


def _call(body, comm, *, name, grid, in_specs, out_specs, out_shape, compiler_params, scratch_shapes=(), aliases=None):
    aliases = dict(aliases or {})
    if comm is None or not comm.parts:
        return pl.pallas_call(body, name=name, grid=grid, in_specs=in_specs, out_specs=out_specs, out_shape=out_shape,
                              scratch_shapes=list(scratch_shapes), input_output_aliases=aliases, compiler_params=compiler_params)
    n_in, n_out, n_scr = len(in_specs), len(out_shape), len(scratch_shapes)
    n_ro, n_io, n_nw = len(comm.ro), len(comm.io), len(comm.nw)

    def carried(*refs):
        base_in = refs[:n_in]
        ro = refs[n_in:n_in + n_ro]
        pos = n_in + n_ro + n_io
        base_out = refs[pos:pos + n_out]
        io = refs[pos + n_out:pos + n_out + n_io]
        nw = refs[pos + n_out + n_io:pos + n_out + n_io + n_nw]
        pos += n_out + n_io + n_nw
        scr = refs[pos:pos + n_scr]
        ssem, rsem = refs[pos + n_scr], refs[pos + n_scr + 1]
        first = pl.program_id(0) == 0
        last = pl.program_id(0) == grid[0] - 1
        for axis in range(1, len(grid)):
            first = first & (pl.program_id(axis) == 0)
            last = last & (pl.program_id(axis) == grid[axis] - 1)

        @pl.when(first)
        def _():
            comm.start(ro, io, nw, ssem, rsem)

        body(*base_in, *base_out, *scr)

        @pl.when(last)
        def _():
            comm.finish(ro, io, nw, ssem, rsem)

    aliases.update({n_in + n_ro + k: n_out + k for k in range(n_io)})
    dma = pltpu.SemaphoreType.DMA
    call = pl.pallas_call(
        carried, name=name, grid=grid,
        in_specs=list(in_specs) + [ANY] * (n_ro + n_io), out_specs=list(out_specs) + [ANY] * (n_io + n_nw),
        out_shape=list(out_shape) + comm.out_shapes(), input_output_aliases=aliases,
        scratch_shapes=list(scratch_shapes) + [dma((comm.n,)), dma((comm.n,))], compiler_params=compiler_params)

    def run(*operands):
        res = call(*operands, *comm.operands())
        comm.keep(res[n_out:])
        return res[:n_out]

    return run


def _run_comm(rounds, name):
    ro, io, nw, uses = [], [], [], []
    for r in rounds:
        def index(items, item, same):
            for k, other in enumerate(items):
                if same(other, item):
                    return k
            items.append(item)
            return len(items) - 1
        uses.append(([index(ro, a, lambda p, q: p is q) for a in r.ro],
                     [index(io, a, lambda p, q: p[0] is q[0] and p[1] == q[1]) for a in r.io],
                     [index(nw, a, lambda p, q: False) for a in r.nw]))
    n_ro, n_io, n_nw = len(ro), len(io), len(nw)

    def body(*refs):
        ro_refs = refs[:n_ro]
        io_refs = refs[n_ro + n_io:n_ro + 2 * n_io]
        nw_refs = refs[n_ro + 2 * n_io:n_ro + 2 * n_io + n_nw]
        sems = refs[n_ro + 2 * n_io + n_nw:]
        for k, (r, (a, b, c)) in enumerate(zip(rounds, uses)):
            args = ([ro_refs[i] for i in a], [io_refs[i] for i in b], [nw_refs[i] for i in c], sems[2 * k], sems[2 * k + 1])
            r.start(*args)
            r.finish(*args)

    operands = ro + [store[key] for store, key in io]
    out_shape = [jax.ShapeDtypeStruct(store[key].shape, store[key].dtype) for store, key in io] + [s for _, _, s in nw]
    dma = pltpu.SemaphoreType.DMA
    res = pl.pallas_call(
        body, name=name, out_shape=out_shape,
        in_specs=[ANY] * (n_ro + n_io), out_specs=[ANY] * (n_io + n_nw),
        input_output_aliases={n_ro + k: k for k in range(n_io)},
        scratch_shapes=[dma((r.n,)) for r in rounds for _ in range(2)],
    )(*operands)
    for (store, key), arr in zip(io, res[:n_io]):
        store[key] = arr
    for (store, key, _), arr in zip(nw, res[n_io:]):
        store[key] = arr


HBM = pl.BlockSpec(memory_space=pltpu.HBM)
SEM = pl.BlockSpec(memory_space=pltpu.SEMAPHORE)
SPLIT_COPY = pltpu.CompilerParams(has_side_effects=pltpu.SideEffectType.DATAFLOW_SIDE_EFFECTING)


def _first_gather_copies(slot, send_sems, recv_sems):
    x, y, c, chips = _place()
    hk = slot.shape[1] // 2
    mine = slot.at[_chip_id(x, y), _half(c, hk)]
    pairs = []
    for j, (cx, cy) in enumerate(chips):
        theirs = slot.at[_chip_id(cx, cy), _half(c, hk)]
        sems = dict(send_sem=send_sems.at[j], recv_sem=recv_sems.at[j])
        pairs.append((_remote(mine, mine, (cx, cy, c), sems), _remote(theirs, theirs, (cx, cy, c), sems)))
    return pairs


def _gather_first_start(slot):
    def body(slot_ref, send_sems, recv_sems, slot_thru, token):
        for send, _ in _first_gather_copies(slot_ref, send_sems, recv_sems):
            send.start()
        token[...] = jnp.zeros_like(token)

    dma = pltpu.SemaphoreType.DMA
    return pl.pallas_call(
        body, name="gather_first_start",
        out_shape=(dma((3,)), dma((3,)), pltpu.HBM(slot.shape, slot.dtype), jax.ShapeDtypeStruct((SUBLANES, 128), jnp.int32)),
        in_specs=(HBM,), out_specs=(SEM, SEM, HBM, pl.BlockSpec(memory_space=pltpu.VMEM)), input_output_aliases={0: 2},
        compiler_params=SPLIT_COPY,
    )(pltpu.with_memory_space_constraint(slot, pltpu.HBM))


def _gather_first_wait(send_sems, recv_sems, slot, after):
    def body(slot_ref, send_sems, recv_sems, *rest):
        for send, recv in _first_gather_copies(slot_ref, send_sems, recv_sems):
            send.wait_send()
            recv.wait_recv()

    return pl.pallas_call(
        body, name="gather_first_wait", out_shape=(pltpu.HBM(slot.shape, slot.dtype),),
        in_specs=(HBM, SEM, SEM) + (ANY,) * len(after), out_specs=(HBM,), input_output_aliases={0: 0},
        compiler_params=SPLIT_COPY,
    )(slot, send_sems, recv_sems, *after)[0]


def _scatter_copies(src, land, send_sems, recv_sems):
    x, y, c, chips = _place()
    return [_remote(src.at[_chip_id(cx, cy)], land.at[j], (cx, cy, c), dict(send_sem=send_sems.at[j], recv_sem=recv_sems.at[j]))
            for j, (cx, cy) in enumerate(chips)]


def _scatter_start(src):
    land = pltpu.with_memory_space_constraint(lax.empty((3,) + src.shape[1:], src.dtype), pltpu.HBM)

    def body(src_ref, land_ref, send_sems, recv_sems, src_thru, land_thru, token):
        for cp in _scatter_copies(src_ref, land_ref, send_sems, recv_sems):
            cp.start()
        token[...] = jnp.zeros_like(token)

    dma = pltpu.SemaphoreType.DMA
    return pl.pallas_call(
        body, name="reduce_chips_start",
        out_shape=(dma((3,)), dma((3,)), pltpu.HBM(src.shape, src.dtype), pltpu.HBM(land.shape, land.dtype),
                   jax.ShapeDtypeStruct((SUBLANES, 128), F32)),
        in_specs=(HBM, HBM), out_specs=(SEM, SEM, HBM, HBM, pl.BlockSpec(memory_space=pltpu.VMEM)),
        input_output_aliases={0: 2, 1: 3}, compiler_params=SPLIT_COPY,
    )(pltpu.with_memory_space_constraint(src, pltpu.HBM), land)


def _scatter_wait(send_sems, recv_sems, src, land, after):
    def body(src_ref, land_ref, send_sems, recv_sems, after_ref, src_done, land_done):
        for cp in _scatter_copies(src_ref, land_ref, send_sems, recv_sems):
            cp.wait_send()
            cp.wait_recv()

    return pl.pallas_call(
        body, name="reduce_chips_wait", out_shape=(pltpu.HBM(src.shape, src.dtype), pltpu.HBM(land.shape, land.dtype)),
        in_specs=(HBM, HBM, SEM, SEM, ANY), out_specs=(HBM, HBM), input_output_aliases={0: 0, 1: 1}, compiler_params=SPLIT_COPY,
    )(src, land, send_sems, recv_sems, after)[1]


def _gather_ici(ro, io, nw, sems):
    s = io[0]
    x, y, c, chips = _place()
    hk = s.shape[1] // 2
    mine = s.at[_chip_id(x, y), _half(c, hk)]
    pairs = []
    for j, (cx, cy) in enumerate(chips):
        theirs = s.at[_chip_id(cx, cy), _half(c, hk)]
        pairs.append((_remote(mine, mine, (cx, cy, c), sems(j)), _remote(theirs, theirs, (cx, cy, c), sems(j))))
    return pairs


def _gather_d2d(ro, io, nw, sems):
    s = io[0]
    x, y, c, chips = _place()
    hk = s.shape[1] // 2
    sib = (x, y, 1 - c)
    pairs = []
    for j, (cx, cy) in enumerate(chips):
        here = s.at[_chip_id(cx, cy), _half(c, hk)]
        there = s.at[_chip_id(cx, cy), _half(1 - c, hk)]
        pairs.append((_remote(here, here, sib, sems(j)), _remote(there, there, sib, sems(j))))
    return pairs


def _gather_small(ro, io, nw, sems):
    x, y, c, chips = _place()
    return [(_remote(ro[0], nw[0].at[j], (cx, cy, c), sems(j)),) * 2 for j, (cx, cy) in enumerate(chips)]


def _reduce_halves(ro, io, nw, sems):
    x, y, c, _ = _place()
    g = ro[0]
    hk = g.shape[1] // 2
    sib = (x, y, 1 - c)
    return [(_remote(g.at[:, _half(1 - c, hk)], nw[0], sib, sems(0)), _remote(g.at[:, _half(c, hk)], nw[0], sib, sems(0)))]


def _reduce_chips(ro, io, nw, sems):
    x, y, c, chips = _place()
    return [(_remote(ro[0].at[_chip_id(cx, cy)], nw[0].at[j], (cx, cy, c), sems(j)),) * 2 for j, (cx, cy) in enumerate(chips)]


def _reduce_share(layer):
    def maker(ro, io, nw, sems):
        g = io[0]
        x, y, c, _ = _place()
        hk = g.shape[1] // 2
        sib = (x, y, 1 - c)
        mine, theirs = g.at[layer, _half(c, hk)], g.at[layer, _half(1 - c, hk)]
        return [(_remote(mine, mine, sib, sems(0)), _remote(theirs, theirs, sib, sems(0)))]
    return maker


def _own_chip():
    return _chip_id(lax.axis_index("x"), lax.axis_index("y"))


def _other_chip(j):
    x, y = lax.axis_index("x"), lax.axis_index("y")
    return _chip_id(jnp.where(j == 1, x, 1 - x), jnp.where(j == 0, y, 1 - y))


def _rms_inproj_own(x, g_row, win, layer, tm):
    T, D = x.shape
    ns = win.shape[-1]

    def body(x_ref, g_ref, w_ref, p_ref, h_ref):
        xv = x_ref[...]
        h = (xv * _rms_stats(xv) * g_ref[...]).astype(BF16)
        h_ref[...] = h
        p_ref[...] = jnp.dot(h, w_ref[...], preferred_element_type=F32).astype(BF16)

    return pl.pallas_call(
        body, name=f"rms_inproj_own_{layer}", grid=(T // tm,),
        in_specs=[pl.BlockSpec((tm, D), lambda i: (i, 0)),
                  pl.BlockSpec((1, D), lambda i: (0, 0)),
                  pl.BlockSpec((None, D, ns), lambda i: (_own_chip(), 0, 0))],
        out_specs=[pl.BlockSpec((tm, ns), lambda i: (i, _own_chip())),
                   pl.BlockSpec((tm, D), lambda i: (i, 0))],
        out_shape=[jax.ShapeDtypeStruct((T, N_CHIP * ns), BF16), jax.ShapeDtypeStruct((T, D), BF16)],
        compiler_params=_params("parallel"),
    )(x, g_row, win)


def _inproj_rest(h, win, p, layer, tm, comm=None):
    T, D = h.shape
    ns = win.shape[-1]

    def body(h_ref, w_ref, p_in, p_ref):
        p_ref[...] = jnp.dot(h_ref[...], w_ref[...], preferred_element_type=F32).astype(BF16)

    return _call(
        body, comm, name=f"rms_inproj_{layer}", grid=(T // tm, N_CHIP - 1),
        in_specs=[pl.BlockSpec((tm, D), lambda i, j: (i, 0)),
                  pl.BlockSpec((None, D, ns), lambda i, j: (_other_chip(j), 0, 0)), ANY],
        out_specs=[pl.BlockSpec((tm, ns), lambda i, j: (i, _other_chip(j)))],
        out_shape=[jax.ShapeDtypeStruct(p.shape, p.dtype)], aliases={2: 0},
        compiler_params=_params("parallel", "arbitrary"),
    )(h, win, p)[0]


def _mixer_recompute(ca, xa, xb, zprev8, xbprev8, cw_ref, wab_ref, sp, cs=slice(None), b0=0):
    row = lambda k: cw_ref[pl.ds(k, 1), cs]
    z = ca * xa
    z1 = _shift_down(z, 1, zprev8)
    z2 = _shift_down(z, 2, zprev8)
    cz = row(2) * z + row(1) * z1 + row(0) * z2
    x1 = _shift_down(xb, 1, xbprev8)
    x2 = _shift_down(xb, 2, xbprev8)
    x3 = _shift_down(xb, 3, xbprev8)
    u = row(6) * xb + row(5) * x1 + row(4) * x2 + row(3) * x3 + row(7)
    ub = u.astype(BF16)
    nb = u.shape[1] // LRU_BLOCK
    ras, ixs = [], []
    for b in range(nb):
        ri = jnp.dot(ub[:, b * LRU_BLOCK:(b + 1) * LRU_BLOCK], wab_ref[b0 + b], preferred_element_type=F32)
        ras.append(ri[:, :LRU_BLOCK])
        ixs.append(ri[:, LRU_BLOCK:])
    r = _sigmoid(jnp.concatenate(ras, axis=1) + row(8))
    gi = _sigmoid(jnp.concatenate(ixs, axis=1) + row(9))
    la = (-LRU_C) * r * sp
    a = jnp.exp(la)
    m = jnp.sqrt(_one_minus_sq(la, a))
    return dict(z=z, z1=z1, z2=z2, cz=cz, x1=x1, x2=x2, x3=x3, u=u, ub=ub, r=r, gi=gi, a=a, m=m)


def _softplus_neg(lam):
    v = -lam
    return jnp.maximum(v, 0.0) + jnp.log1p(jnp.exp(-jnp.abs(v)))


def _mixer_fwd(p, cw, wab, lam_row, layer, tm, comm=None):
    T = p.shape[0]
    D = p.shape[1] // 7
    ngroups = tm // SUBLANES

    def body(ba_ref, ca_ref, xa_ref, xb_ref, gb_ref, cw_ref, wab_ref, lam_ref, ya_ref, yb_ref, h_ref,
             zprev, xbprev, hcarry, a_s, h_s):
        @pl.when(pl.program_id(0) == 0)
        def _():
            zprev[...] = jnp.zeros_like(zprev)
            xbprev[...] = jnp.zeros_like(xbprev)
            hcarry[...] = jnp.zeros_like(hcarry)

        half = D // 2
        for part in range(2):
            cs = slice(part * half, (part + 1) * half)
            ca = ca_ref[:, cs].astype(F32)
            xa = xa_ref[:, cs].astype(F32)
            xb = xb_ref[:, cs].astype(F32)
            sp = _softplus_neg(lam_ref[:, cs])
            c = _mixer_recompute(ca, xa, xb, zprev[:, cs], xbprev[:, cs], cw_ref, wab_ref, sp, cs, part * (half // LRU_BLOCK))
            zprev[:, cs] = c["z"][tm - SUBLANES:]
            xbprev[:, cs] = xb[tm - SUBLANES:]
            ya_ref[:, cs] = (ba_ref[:, cs].astype(F32) * c["cz"]).astype(BF16)
            a_s[:, cs], h_s[:, cs] = _group_scan(c["a"], c["m"] * c["gi"] * c["u"], reverse=False)

        def step(g, carry):
            off = pl.multiple_of(g * SUBLANES, SUBLANES)
            hg = h_s[pl.ds(off, SUBLANES), :] + a_s[pl.ds(off, SUBLANES), :] * carry
            h_s[pl.ds(off, SUBLANES), :] = hg
            return jnp.broadcast_to(hg[SUBLANES - 1:SUBLANES, :], hg.shape)

        hcarry[...] = lax.fori_loop(0, ngroups, step, hcarry[...], unroll=4)
        for part in range(2):
            cs = slice(part * half, (part + 1) * half)
            h = h_s[:, cs]
            h_ref[:, cs] = h
            gelu, _ = _gelu_parts(gb_ref[:, cs].astype(F32))
            yb_ref[:, cs] = (h * gelu).astype(BF16)

    col = lambda k: pl.BlockSpec((tm, D), lambda i: (i, k))
    full = lambda a: pl.BlockSpec(a.shape, lambda i: (0,) * a.ndim)
    tok = pl.BlockSpec((tm, D), lambda i: (i, 0))
    return _call(
        body, comm, name=f"mixer_fwd_{layer}", grid=(T // tm,),
        in_specs=[col(0), col(1), col(2), col(3), col(4), full(cw), full(wab), full(lam_row)],
        out_specs=[tok, tok, tok],
        out_shape=[jax.ShapeDtypeStruct((T, D), BF16), jax.ShapeDtypeStruct((T, D), BF16), jax.ShapeDtypeStruct((T, D), F32)],
        scratch_shapes=[pltpu.VMEM((SUBLANES, D), F32), pltpu.VMEM((SUBLANES, D), F32), pltpu.VMEM((SUBLANES, D), F32),
                        pltpu.VMEM((tm, D), F32), pltpu.VMEM((tm, D), F32)],
        compiler_params=_params("arbitrary"),
    )(p, p, p, p, p, cw, wab, lam_row)


def _merge_fwd(x, p, ya, yb, woa, wob, wo, gbias, layer, tm, comm=None):
    T, D = x.shape

    def body(x_ref, ga_ref, gb_ref, ya_ref, yb_ref, woa_ref, wob_ref, wo_ref, bias_ref, oa_ref, ob_ref, mg_ref, x1_ref):
        oa = jnp.dot(ya_ref[...], woa_ref[...], preferred_element_type=F32)
        ob = jnp.dot(yb_ref[...], wob_ref[...], preferred_element_type=F32)
        sa = _sigmoid(ga_ref[...].astype(F32) + bias_ref[pl.ds(0, 1), :])
        sb = _sigmoid(gb_ref[...].astype(F32) + bias_ref[pl.ds(1, 1), :])
        mg = (sa * oa + sb * ob).astype(BF16)
        oa_ref[...] = oa.astype(BF16)
        ob_ref[...] = ob.astype(BF16)
        mg_ref[...] = mg
        x1_ref[...] = x_ref[...] + jnp.dot(mg, wo_ref[...], preferred_element_type=F32)

    tok = pl.BlockSpec((tm, D), lambda i: (i, 0))
    wsp = pl.BlockSpec((D, D), lambda i: (0, 0))
    bf = jax.ShapeDtypeStruct((T, D), BF16)
    return _call(
        body, comm, name=f"merge_fwd_{layer}", grid=(T // tm,),
        in_specs=[tok, pl.BlockSpec((tm, D), lambda i: (i, 5)), pl.BlockSpec((tm, D), lambda i: (i, 6)), tok, tok,
                  wsp, wsp, wsp, pl.BlockSpec(gbias.shape, lambda i: (0, 0))],
        out_specs=[tok, tok, tok, tok],
        out_shape=[bf, bf, bf, jax.ShapeDtypeStruct((T, D), F32)],
        compiler_params=_params("parallel"),
    )(x, p, p, ya, yb, woa, wob, wo, gbias)


def _loss_tile(xv, g, tgt):
    d = xv.shape[-1]
    rstd = _rms_stats(xv)
    xhat = xv * rstd
    err = xhat * g - tgt
    dy = err * (1.0 / d)
    dxhat = dy * g
    dx = rstd * (dxhat - xhat * jnp.mean(dxhat * xhat, axis=-1, keepdims=True))
    return dx, _colsum8(err * err), _colsum8(dy * xhat)


def _ffn_fwd(x1, g_row, wg, wu, wd, layer, tm, comm=None):
    T, D = x1.shape
    fs = wg.shape[-2]
    n = T // tm
    nt = (((1,), (1,)), ((), ()))

    def body(x_ref, g_ref, wg_ref, wu_ref, wd_ref, h_ref, gg_ref, uu_ref, x2_ref, acc):
        j = pl.program_id(1)

        @pl.when(j == 0)
        def _():
            xv = x_ref[...]
            h_ref[...] = (xv * _rms_stats(xv) * g_ref[...]).astype(BF16)
            acc[...] = xv

        h = h_ref[...]
        gg = lax.dot_general(h, wg_ref[...], nt, preferred_element_type=F32)
        uu = lax.dot_general(h, wu_ref[...], nt, preferred_element_type=F32)
        gg_ref[...] = gg.astype(BF16)
        uu_ref[...] = uu.astype(BF16)
        act = (gg * _sigmoid(gg) * uu).astype(BF16)
        acc[...] += jnp.dot(act, wd_ref[...], preferred_element_type=F32)

        @pl.when(j == N_CHIP - 1)
        def _():
            x2_ref[...] = acc[...]

    tok = pl.BlockSpec((tm, D), lambda i, j: (i, 0))
    cm = pl.BlockSpec((None, tm, fs), lambda i, j: (j, i, 0))
    wsp = pl.BlockSpec((None, fs, D), lambda i, j: (j, 0, 0))
    return _call(
        body, comm, name=f"ffn_fwd_{layer}", grid=(n, N_CHIP),
        in_specs=[tok, pl.BlockSpec((1, D), lambda i, j: (0, 0)), wsp, wsp, wsp], out_specs=[tok, cm, cm, tok],
        out_shape=[jax.ShapeDtypeStruct((T, D), BF16), jax.ShapeDtypeStruct((N_CHIP, T, fs), BF16),
                   jax.ShapeDtypeStruct((N_CHIP, T, fs), BF16), jax.ShapeDtypeStruct((T, D), F32)],
        scratch_shapes=[pltpu.VMEM((tm, D), F32)], compiler_params=_params("parallel", "arbitrary"),
    )(x1, g_row, wg, wu, wd)


def _final_loss(x, g_row, target, tm):
    T, D = x.shape
    n = T // tm

    def body(x_ref, g_ref, t_ref, dx_ref, red_ref, racc):
        i = pl.program_id(0)

        @pl.when(i == 0)
        def _():
            racc[...] = jnp.zeros_like(racc)

        dx_ref[...], sq, dg = _loss_tile(x_ref[...], g_ref[...], t_ref[...])
        racc[0] += sq
        racc[1] += dg

        @pl.when(i == n - 1)
        def _():
            red_ref[pl.ds(0, 1), :] = jnp.sum(racc[0], axis=0, keepdims=True) * (0.5 / D)
            red_ref[pl.ds(1, 1), :] = jnp.sum(racc[1], axis=0, keepdims=True)

    tok = pl.BlockSpec((tm, D), lambda i: (i, 0))
    return pl.pallas_call(
        body, name="final_loss", grid=(n,),
        in_specs=[tok, pl.BlockSpec((1, D), lambda i: (0, 0)), tok],
        out_specs=[tok, pl.BlockSpec((2, D), lambda i: (0, 0))],
        out_shape=[jax.ShapeDtypeStruct((T, D), F32), jax.ShapeDtypeStruct((2, D), F32)],
        scratch_shapes=[pltpu.VMEM((2, SUBLANES, D), F32)],
        compiler_params=_params("arbitrary"),
    )(x, g_row, target)


def _ffn_bwd_gates(dx2, gg, uu, wd, layer, tm):
    T, D = dx2.shape
    fs = wd.shape[-2]
    nt = (((1,), (1,)), ((), ()))

    def body(dx_ref, gg_ref, uu_ref, wd_ref, dg_ref, du_ref, act_ref, dxb_ref):
        @pl.when(pl.program_id(1) == 0)
        def _():
            dxb_ref[...] = dx_ref[...].astype(BF16)

        dact = lax.dot_general(dxb_ref[...], wd_ref[...], nt, preferred_element_type=F32)
        g = gg_ref[...].astype(F32)
        u = uu_ref[...].astype(F32)
        s = _sigmoid(g)
        silu = g * s
        dg_ref[...] = (dact * u * (s * (1.0 + g * (1.0 - s)))).astype(BF16)
        du_ref[...] = (dact * silu).astype(BF16)
        act_ref[...] = (silu * u).astype(BF16)

    tok = pl.BlockSpec((tm, D), lambda i, j: (i, 0))
    cm = pl.BlockSpec((None, tm, fs), lambda i, j: (j, i, 0))
    cms = jax.ShapeDtypeStruct((N_CHIP, T, fs), BF16)
    return pl.pallas_call(
        body, name=f"ffn_bwd_gates_{layer}", grid=(T // tm, N_CHIP),
        in_specs=[tok, cm, cm, pl.BlockSpec((None, fs, D), lambda i, j: (j, 0, 0))], out_specs=[cm, cm, cm, tok],
        out_shape=[cms, cms, cms, jax.ShapeDtypeStruct((T, D), BF16)],
        compiler_params=_params("parallel", "arbitrary"),
    )(dx2, gg, uu, wd)


def _ffn_bwd(dgg, duu, wg, wu, dx2, x1, g_row, layer, tm, comm=None):
    T, D = dx2.shape
    fs = wg.shape[-2]
    n = T // tm

    def body(dg_ref, du_ref, wg_ref, wu_ref, dx_ref, x_ref, g_ref, dx1_ref, red_ref, acc, racc):
        i = pl.program_id(0)
        j = pl.program_id(1)

        @pl.when((i == 0) & (j == 0))
        def _():
            racc[...] = jnp.zeros_like(racc)

        @pl.when(j == 0)
        def _():
            acc[...] = jnp.zeros_like(acc)

        acc[...] += (jnp.dot(dg_ref[...], wg_ref[...], preferred_element_type=F32)
                     + jnp.dot(du_ref[...], wu_ref[...], preferred_element_type=F32))

        @pl.when(j == N_CHIP - 1)
        def _():
            dx, dgain = _rms_bwd(acc[...], x_ref[...], g_ref[...])
            dx1_ref[...] = dx_ref[...] + dx
            racc[...] += _colsum8(dgain)

        @pl.when((i == n - 1) & (j == N_CHIP - 1))
        def _():
            red_ref[...] = jnp.sum(racc[...], axis=0, keepdims=True)

    tok = pl.BlockSpec((tm, D), lambda i, j: (i, 0))
    cm = pl.BlockSpec((None, tm, fs), lambda i, j: (j, i, 0))
    wsp = pl.BlockSpec((None, fs, D), lambda i, j: (j, 0, 0))
    row = pl.BlockSpec((1, D), lambda i, j: (0, 0))
    return _call(
        body, comm, name=f"ffn_bwd_{layer}", grid=(n, N_CHIP),
        in_specs=[cm, cm, wsp, wsp, tok, tok, row], out_specs=[tok, row],
        out_shape=[jax.ShapeDtypeStruct((T, D), F32), jax.ShapeDtypeStruct((1, D), F32)],
        scratch_shapes=[pltpu.VMEM((tm, D), F32), pltpu.VMEM((SUBLANES, D), F32)],
        compiler_params=_params("arbitrary", "arbitrary"),
    )(dgg, duu, wg, wu, dx2, x1, g_row)


def _merge_bwd(dx1, p, oa, ob, woa, wob, wo, gbias, layer, tm, comm=None):
    T, D = dx1.shape
    n = T // tm
    nt = (((1,), (1,)), ((), ()))

    def body(dx_ref, ga_ref, gb_ref, oa_ref, ob_ref, woa_ref, wob_ref, wo_ref, bias_ref,
             dya_ref, dyb_ref, doa_ref, dob_ref, dgl_ref, dxb_ref, red_ref, racc):
        i = pl.program_id(0)

        @pl.when(i == 0)
        def _():
            racc[...] = jnp.zeros_like(racc)

        dxb = dx_ref[...].astype(BF16)
        dxb_ref[...] = dxb
        dm = lax.dot_general(dxb, wo_ref[...], nt, preferred_element_type=F32)
        sa = _sigmoid(ga_ref[...].astype(F32) + bias_ref[pl.ds(0, 1), :])
        sb = _sigmoid(gb_ref[...].astype(F32) + bias_ref[pl.ds(1, 1), :])
        doa = (dm * sa).astype(BF16)
        dob = (dm * sb).astype(BF16)
        dga = dm * oa_ref[...].astype(F32) * (sa * (1.0 - sa))
        dgb = dm * ob_ref[...].astype(F32) * (sb * (1.0 - sb))
        doa_ref[...] = doa
        dob_ref[...] = dob
        dgl_ref[:, 0:D] = dga.astype(BF16)
        dgl_ref[:, D:2 * D] = dgb.astype(BF16)
        racc[0] += _colsum8(dga)
        racc[1] += _colsum8(dgb)
        dya_ref[...] = lax.dot_general(doa, woa_ref[...], nt, preferred_element_type=F32).astype(BF16)
        dyb_ref[...] = lax.dot_general(dob, wob_ref[...], nt, preferred_element_type=F32).astype(BF16)

        @pl.when(i == n - 1)
        def _():
            red_ref[pl.ds(0, 1), :] = jnp.sum(racc[0], axis=0, keepdims=True)
            red_ref[pl.ds(1, 1), :] = jnp.sum(racc[1], axis=0, keepdims=True)

    tok = pl.BlockSpec((tm, D), lambda i: (i, 0))
    wsp = pl.BlockSpec((D, D), lambda i: (0, 0))
    bf = jax.ShapeDtypeStruct((T, D), BF16)
    return _call(
        body, comm, name=f"merge_bwd_{layer}", grid=(n,),
        in_specs=[tok, pl.BlockSpec((tm, D), lambda i: (i, 5)), pl.BlockSpec((tm, D), lambda i: (i, 6)), tok, tok,
                  wsp, wsp, wsp, pl.BlockSpec(gbias.shape, lambda i: (0, 0))],
        out_specs=[tok, tok, tok, tok, pl.BlockSpec((tm, 2 * D), lambda i: (i, 0)), tok, pl.BlockSpec((2, D), lambda i: (0, 0))],
        out_shape=[bf, bf, bf, bf, jax.ShapeDtypeStruct((T, 2 * D), BF16), bf, jax.ShapeDtypeStruct((2, D), F32)],
        scratch_shapes=[pltpu.VMEM((2, SUBLANES, D), F32)],
        compiler_params=_params("arbitrary"),
    )(dx1, p, p, oa, ob, woa, wob, wo, gbias)


N_MIXER_RED = 16


def _mixer_bwd(p, hseq, dya, dyb, dgl, cw, wab, wabt, lam_row, layer, tm, comm=None):
    T = p.shape[0]
    D = p.shape[1] // 7
    n = T // tm
    ngroups = tm // SUBLANES
    nb = wab.shape[0]
    hb = 16
    tn = (((0,), (0,)), ((), ()))

    def body(ba_ref, ca_ref, xa_ref, xb_ref, gb_ref, h_ref, dya_ref, dyb_ref, dgl_ref,
             cap_ref, xap_ref, xbp_ref, hp_ref, ban_ref, dyan_ref,
             cw_ref, wab_ref, wabt_ref, lam_ref,
             dp_ref, red_ref, dwab_ref,
             racc, wacc, anext, gnext, dunext, c_s, g_s):
        i = pl.program_id(0)
        first_tile = i == n - 1
        last_tile = i == 0

        @pl.when(i == 0)
        def _():
            racc[...] = jnp.zeros_like(racc)
            wacc[...] = jnp.zeros_like(wacc)
            anext[...] = jnp.zeros_like(anext)
            gnext[...] = jnp.zeros_like(gnext)
            dunext[...] = jnp.zeros_like(dunext)

        keep_prev = jnp.where(first_tile, 0.0, 1.0)
        keep_next = jnp.where(last_tile, 0.0, 1.0)
        ba = ba_ref[...].astype(F32)
        ca = ca_ref[...].astype(F32)
        xa = xa_ref[...].astype(F32)
        xb = xb_ref[...].astype(F32)
        h = h_ref[...]
        dya = dya_ref[...].astype(F32)
        dyb = dyb_ref[...].astype(F32)
        zprev8 = (cap_ref[...].astype(F32) * xap_ref[...].astype(F32))[hb - SUBLANES:] * keep_prev
        xbprev8 = xbp_ref[...].astype(F32)[hb - SUBLANES:] * keep_prev
        hprev8 = hp_ref[...] * keep_prev
        dcznext8 = (dyan_ref[...].astype(F32) * ban_ref[...].astype(F32))[:SUBLANES] * keep_next

        lam = lam_ref[...]
        sp = _softplus_neg(lam)
        c = _mixer_recompute(ca, xa, xb, zprev8, xbprev8, cw_ref, wab_ref, sp)
        row = lambda k: cw_ref[pl.ds(k, 1), :]
        a, m, r, gi, u = c["a"], c["m"], c["r"], c["gi"], c["u"]

        gelu, dgelu = _gelu_parts(gb_ref[...].astype(F32))
        dgb = dyb * h * dgelu
        c_s[...], g_s[...] = _group_scan(_shift_up(a, 1, anext[...]), dyb * gelu, reverse=True)

        def step(k, carry):
            off = pl.multiple_of((ngroups - 1 - k) * SUBLANES, SUBLANES)
            gg = g_s[pl.ds(off, SUBLANES), :] + c_s[pl.ds(off, SUBLANES), :] * carry
            g_s[pl.ds(off, SUBLANES), :] = gg
            return jnp.broadcast_to(gg[0:1, :], gg.shape)

        gnext[...] = lax.fori_loop(0, ngroups, step, gnext[...], unroll=4)
        anext[...] = a[0:SUBLANES]
        g = g_s[...]

        hprev = _shift_down(h, 1, hprev8)
        da = g * hprev
        gm = g * m
        dgi = gm * u
        du = gm * gi
        dmv = g * gi * u
        dla = a * (da - dmv * a / m)
        dra = dla * ((-LRU_C) * sp) * (r * (1.0 - r))
        dix = dgi * (gi * (1.0 - gi))
        racc[10] += _colsum8(dla * r)
        racc[8] += _colsum8(dra)
        racc[9] += _colsum8(dix)
        drab = dra.astype(BF16)
        dixb = dix.astype(BF16)
        ub = c["ub"]
        dus = []
        for b in range(nb):
            sl = slice(b * LRU_BLOCK, (b + 1) * LRU_BLOCK)
            dri = jnp.concatenate([drab[:, sl], dixb[:, sl]], axis=1)
            dus.append(jnp.dot(dri, wabt_ref[b], preferred_element_type=F32))
            wacc[b] += lax.dot_general(ub[:, sl], dri, tn, preferred_element_type=F32)
        du = du + jnp.concatenate(dus, axis=1)

        dun = dunext[...]
        du1 = _shift_up(du, 1, dun)
        du2 = _shift_up(du, 2, dun)
        du3 = _shift_up(du, 3, dun)
        dxb = row(6) * du + row(5) * du1 + row(4) * du2 + row(3) * du3
        dunext[...] = du[0:SUBLANES]
        racc[6] += _colsum8(du * xb)
        racc[5] += _colsum8(du * c["x1"])
        racc[4] += _colsum8(du * c["x2"])
        racc[3] += _colsum8(du * c["x3"])
        racc[7] += _colsum8(du)

        dba = dya * c["cz"]
        dcz = dya * ba
        dcz1 = _shift_up(dcz, 1, dcznext8)
        dcz2 = _shift_up(dcz, 2, dcznext8)
        dz = row(2) * dcz + row(1) * dcz1 + row(0) * dcz2
        racc[2] += _colsum8(dcz * c["z"])
        racc[1] += _colsum8(dcz * c["z1"])
        racc[0] += _colsum8(dcz * c["z2"])

        dp_ref[:, 0:D] = dba.astype(BF16)
        dp_ref[:, D:2 * D] = (dz * xa).astype(BF16)
        dp_ref[:, 2 * D:3 * D] = (dz * ca).astype(BF16)
        dp_ref[:, 3 * D:4 * D] = dxb.astype(BF16)
        dp_ref[:, 4 * D:5 * D] = dgb.astype(BF16)
        dp_ref[:, 5 * D:7 * D] = dgl_ref[...]

        @pl.when(i == n - 1)
        def _():
            dlam_scale = LRU_C * _sigmoid(-lam)
            for k in range(N_MIXER_RED):
                tot = jnp.sum(racc[k], axis=0, keepdims=True)
                red_ref[pl.ds(k, 1), :] = tot * dlam_scale if k == 10 else tot
            dwab_ref[...] = wacc[...]

    rt = lambda i: n - 1 - i
    col = lambda k: pl.BlockSpec((tm, D), lambda i: (rt(i), k))
    tok = pl.BlockSpec((tm, D), lambda i: (rt(i), 0))
    full = lambda a: pl.BlockSpec(a.shape, lambda i: (0,) * a.ndim)
    prev16 = lambda k: pl.BlockSpec((hb, D), lambda i: (jnp.maximum(rt(i) * (tm // hb) - 1, 0), k))
    next16 = lambda k: pl.BlockSpec((hb, D), lambda i: (jnp.minimum((rt(i) + 1) * (tm // hb), T // hb - 1), k))
    hprev = pl.BlockSpec((SUBLANES, D), lambda i: (jnp.maximum(rt(i) * ngroups - 1, 0), 0))
    return _call(
        body, comm, name=f"mixer_bwd_{layer}", grid=(n,),
        in_specs=[col(0), col(1), col(2), col(3), col(4), tok, tok, tok, pl.BlockSpec((tm, 2 * D), lambda i: (rt(i), 0)),
                  prev16(1), prev16(2), prev16(3), hprev, next16(0), next16(0),
                  full(cw), full(wab), full(wabt), full(lam_row)],
        out_specs=[pl.BlockSpec((tm, 7 * D), lambda i: (rt(i), 0)),
                   pl.BlockSpec((N_MIXER_RED, D), lambda i: (0, 0)),
                   pl.BlockSpec((nb, LRU_BLOCK, 2 * LRU_BLOCK), lambda i: (0, 0, 0))],
        out_shape=[jax.ShapeDtypeStruct((T, 7 * D), BF16), jax.ShapeDtypeStruct((N_MIXER_RED, D), F32),
                   jax.ShapeDtypeStruct((nb, LRU_BLOCK, 2 * LRU_BLOCK), F32)],
        scratch_shapes=[pltpu.VMEM((N_MIXER_RED, SUBLANES, D), F32), pltpu.VMEM((nb, LRU_BLOCK, 2 * LRU_BLOCK), F32),
                        pltpu.VMEM((SUBLANES, D), F32), pltpu.VMEM((SUBLANES, D), F32), pltpu.VMEM((SUBLANES, D), F32),
                        pltpu.VMEM((tm, D), F32), pltpu.VMEM((tm, D), F32)],
        compiler_params=_params("arbitrary"),
    )(p, p, p, p, p, hseq, dya, dyb, dgl, p, p, p, hseq, p, dya, cw, wab, wabt, lam_row)


def _inproj_bwd(dp, dx1, x, g_row, win, layer, tm, comm=None):
    T, D = x.shape
    ns = win.shape[-1]
    n = T // tm
    nt = (((1,), (1,)), ((), ()))

    def body(dp_ref, dx_ref, x_ref, g_ref, w_ref, dx0_ref, red_ref, acc, racc):
        i = pl.program_id(0)
        j = pl.program_id(1)

        @pl.when((i == 0) & (j == 0))
        def _():
            racc[...] = jnp.zeros_like(racc)

        @pl.when(j == 0)
        def _():
            acc[...] = jnp.zeros_like(acc)

        acc[...] += lax.dot_general(dp_ref[...], w_ref[...], nt, preferred_element_type=F32)

        @pl.when(j == N_CHIP - 1)
        def _():
            dx, dgain = _rms_bwd(acc[...], x_ref[...], g_ref[...])
            dx0_ref[...] = dx_ref[...] + dx
            racc[...] += _colsum8(dgain)

        @pl.when((i == n - 1) & (j == N_CHIP - 1))
        def _():
            red_ref[...] = jnp.sum(racc[...], axis=0, keepdims=True)

    tok = pl.BlockSpec((tm, D), lambda i, j: (i, 0))
    return _call(
        body, comm, name=f"inproj_bwd_{layer}", grid=(n, N_CHIP),
        in_specs=[pl.BlockSpec((tm, ns), lambda i, j: (i, j)), tok, tok, pl.BlockSpec((1, D), lambda i, j: (0, 0)),
                  pl.BlockSpec((None, D, ns), lambda i, j: (j, 0, 0))],
        out_specs=[tok, pl.BlockSpec((1, D), lambda i, j: (0, 0))],
        out_shape=[jax.ShapeDtypeStruct((T, D), F32), jax.ShapeDtypeStruct((1, D), F32)],
        scratch_shapes=[pltpu.VMEM((tm, D), F32), pltpu.VMEM((SUBLANES, D), F32)],
        compiler_params=_params("arbitrary", "arbitrary"),
    )(dp, dx1, x, g_row, win)


def _wgrad(a, b, name, tk, a_kind="whole", b_kind="whole", nj=1, comm=None):
    T = a.shape[-2]
    width = lambda v, kind: v.shape[-1] // nj if kind == "cols" else v.shape[-1]
    ka, kb = width(a, a_kind), width(b, b_kind)
    nt = T // tk
    tn = (((0,), (0,)), ((), ()))

    def spec(k, kind):
        if kind == "cm":
            return pl.BlockSpec((None, tk, k), lambda j, t: (j, t, 0))
        if kind == "cols":
            return pl.BlockSpec((tk, k), lambda j, t: (t, j))
        return pl.BlockSpec((tk, k), lambda j, t: (t, 0))

    def body(a_ref, b_ref, o_ref, ob_ref):
        t = pl.program_id(1)

        @pl.when(t == 0)
        def _():
            o_ref[...] = jnp.zeros_like(o_ref)

        o_ref[...] += lax.dot_general(a_ref[...], b_ref[...], tn, preferred_element_type=F32)

        @pl.when(t == nt - 1)
        def _():
            ob_ref[...] = o_ref[...].astype(BF16)

    o_spec = pl.BlockSpec((None, ka, kb), lambda j, t: (j, 0, 0))
    return _call(
        body, comm, name=name, grid=(nj, nt),
        in_specs=[spec(ka, a_kind), spec(kb, b_kind)], out_specs=[o_spec, o_spec],
        out_shape=[jax.ShapeDtypeStruct((nj, ka, kb), F32), jax.ShapeDtypeStruct((nj, ka, kb), BF16)],
        compiler_params=_params("parallel", "arbitrary"),
    )(a, b)


def _wgrad_pair(a, b1, b2, name, tk, comm=None):
    T, ka = a.shape
    nj, _, kb = b1.shape
    nt = T // tk
    tn = (((0,), (0,)), ((), ()))

    def body(a_ref, b1_ref, b2_ref, o1_ref, o1b_ref, o2_ref, o2b_ref):
        t = pl.program_id(1)

        @pl.when(t == 0)
        def _():
            o1_ref[...] = jnp.zeros_like(o1_ref)
            o2_ref[...] = jnp.zeros_like(o2_ref)

        av = a_ref[...]
        o1_ref[...] += lax.dot_general(b1_ref[...], av, tn, preferred_element_type=F32)
        o2_ref[...] += lax.dot_general(b2_ref[...], av, tn, preferred_element_type=F32)

        @pl.when(t == nt - 1)
        def _():
            o1b_ref[...] = o1_ref[...].astype(BF16)
            o2b_ref[...] = o2_ref[...].astype(BF16)

    b_spec = pl.BlockSpec((None, tk, kb), lambda j, t: (j, t, 0))
    o_spec = pl.BlockSpec((None, kb, ka), lambda j, t: (j, 0, 0))
    f32 = jax.ShapeDtypeStruct((nj, kb, ka), F32)
    b16 = jax.ShapeDtypeStruct((nj, kb, ka), BF16)
    return _call(
        body, comm, name=name, grid=(nj, nt),
        in_specs=[pl.BlockSpec((tk, ka), lambda j, t: (t, 0)), b_spec, b_spec], out_specs=[o_spec] * 4,
        out_shape=[f32, b16, f32, b16], compiler_params=_params("parallel", "arbitrary"),
    )(a, b1, b2)


def _block_diag(w):
    hb = LRU_BLOCK // LRU_HEAD_DIM
    nb = w.shape[0] // hb
    w4 = w.reshape(nb, hb, LRU_HEAD_DIM, LRU_HEAD_DIM)
    eye = jnp.eye(hb, dtype=w.dtype)
    return jnp.einsum("bide,ij->bidje", w4, eye).reshape(nb, LRU_BLOCK, LRU_BLOCK)


def _diag_heads(m):
    hb = LRU_BLOCK // LRU_HEAD_DIM
    nb = m.shape[0]
    m5 = m.reshape(nb, hb, LRU_HEAD_DIM, hb, LRU_HEAD_DIM)
    eye = jnp.eye(hb, dtype=m.dtype)
    return jnp.einsum("bidje,ij->bide", m5, eye).reshape(nb * hb, LRU_HEAD_DIM, LRU_HEAD_DIM)


def _tiles(T):
    cap = lambda n: min(n, T)
    return dict(inproj=cap(1024), mixer=cap(256), merge=cap(512), ffn=cap(1024), ffn_bwd=cap(1024), loss=cap(512), inproj_bwd=cap(1024),
                wgrad_in=cap(2048), wgrad=cap(2048))


class _NoSchedule:
    def carry(self, name):
        return None

    def tie(self, name, operand):
        return operand

    def between_inproj(self, layer, p, small):
        pass

    def after(self, name, result=None):
        pass

    def grad(self, key, layer, f32, b16):
        pass


def _local_step(x, target, W, small, tiles, sched):
    L = small["ln1_g"].shape[0]
    D = x.shape[1]
    square = lambda a: a.reshape(D, D)
    saved = []
    h = x
    for l in range(L):
        ln1_row = small["ln1_g"][l][None]
        ln2_row = small["ln2_g"][l][None]
        p, h1 = _rms_inproj_own(h, sched.tie(f"rms_inproj_own_{l}", ln1_row), W["win", l], l, tiles["inproj"])
        sched.between_inproj(l, p, small)
        p = _inproj_rest(h1, W["win", l], p, l, tiles["inproj"], sched.carry(f"rms_inproj_{l}"))
        cw = jnp.concatenate([small["conv_a_w"][l], small["conv_b_w"][l], small["conv_b_b"][l][None],
                              small["lru_ba"][l][None], small["lru_bx"][l][None]], axis=0)
        wab = jnp.concatenate([_block_diag(small["lru_wa"][l]), _block_diag(small["lru_wx"][l])], axis=2).astype(BF16)
        wabt = jnp.swapaxes(wab, 1, 2)
        lam_row = small["lru_lambda"][l][None]
        ya, yb, hseq = _mixer_fwd(p, cw, wab, lam_row, l, tiles["mixer"], sched.carry(f"mixer_fwd_{l}"))
        oa, ob, mg, x1 = _merge_fwd(h, p, ya, yb, square(W["woa", l]), square(W["wob", l]), square(W["wo", l]),
                                    small["gate_bias"][l], l, tiles["merge"], sched.carry(f"merge_fwd_{l}"))
        h2, gg, uu, x2 = _ffn_fwd(x1, ln2_row, W["wg", l], W["wu", l], W["wd", l], l, tiles["ffn"], sched.carry(f"ffn_fwd_{l}"))
        saved.append(dict(x0=h, p=p, h1=h1, ya=ya, yb=yb, hseq=hseq, oa=oa, ob=ob, mg=mg, x1=x1, h2=h2, gg=gg, uu=uu,
                          cw=cw, wab=wab, wabt=wabt, lam_row=lam_row, ln1_row=ln1_row, ln2_row=ln2_row))
        h = x2

    dx, red = _final_loss(h, small["final_g"][None], target, tiles["loss"])
    loss_row, d_final_g = red[0], red[1]

    gsmall = {k: [None] * L for k in ("ln1_g", "ln2_g", "conv_a_w", "conv_b_w", "conv_b_b", "lru_wa", "lru_ba", "lru_wx",
                                      "lru_bx", "lru_lambda", "gate_bias")}
    tk = tiles["wgrad"]
    for l in reversed(range(L)):
        s = saved[l]
        dgg, duu, act, dx2b = _ffn_bwd_gates(dx, s["gg"], s["uu"], W["wd", l], l, tiles["ffn_bwd"])
        dx1, dln2 = _ffn_bwd(dgg, duu, W["wg", l], W["wu", l], dx, s["x1"], s["ln2_row"], l, tiles["ffn_bwd"],
                             sched.carry(f"ffn_bwd_{l}"))
        sched.after(f"ffn_bwd_{l}")
        gate_up = _wgrad_pair(s["h2"], dgg, duu, f"wgrad_ffn_gate_up_{l}", tk, sched.carry(f"wgrad_ffn_gate_up_{l}"))
        sched.grad("wg", l, *gate_up[0:2])
        sched.grad("wu", l, *gate_up[2:4])
        sched.after(f"wgrad_ffn_gate_up_{l}")
        sched.grad("wd", l, *_wgrad(act, dx2b, f"wgrad_ffn_down_{l}", tk, "cm", "whole", N_CHIP, sched.carry(f"wgrad_ffn_down_{l}")))
        dya, dyb, doa, dob, dgl, dx1b, dgbias = _merge_bwd(dx1, s["p"], s["oa"], s["ob"], square(W["woa", l]), square(W["wob", l]),
                                                         square(W["wo", l]), small["gate_bias"][l], l, tiles["merge"],
                                                         sched.carry(f"merge_bwd_{l}"))
        sched.after(f"merge_bwd_{l}")
        sched.grad("wo", l, *_wgrad(s["mg"], dx1b, f"wgrad_w_o_{l}", tk))
        sched.grad("woa", l, *_wgrad(s["ya"], doa, f"wgrad_w_out_a_{l}", tk))
        sched.grad("wob", l, *_wgrad(s["yb"], dob, f"wgrad_w_out_b_{l}", tk))
        dp, mred, dwab = _mixer_bwd(s["p"], s["hseq"], dya, dyb, dgl, s["cw"], s["wab"], s["wabt"], s["lam_row"], l,
                                    tiles["mixer"], sched.carry(f"mixer_bwd_{l}"))
        sched.after(f"mixer_bwd_{l}")
        sched.grad("win", l, *_wgrad(s["h1"], dp, f"wgrad_w_in_{l}", tiles["wgrad_in"], "whole", "cols", N_CHIP,
                                     sched.carry(f"wgrad_w_in_{l}")))
        sched.after(f"wgrad_w_in_{l}")
        dx, dln1 = _inproj_bwd(dp, dx1, s["x0"], sched.tie(f"inproj_bwd_{l}", s["ln1_row"]), W["win", l], l, tiles["inproj_bwd"],
                               sched.carry(f"inproj_bwd_{l}"))
        sched.after(f"inproj_bwd_{l}", dx)
        gsmall["ln1_g"][l] = dln1[0]
        gsmall["ln2_g"][l] = dln2[0]
        gsmall["conv_a_w"][l] = mred[0:CONV_A_K]
        gsmall["conv_b_w"][l] = mred[CONV_A_K:CONV_A_K + CONV_B_K]
        gsmall["conv_b_b"][l] = mred[7]
        gsmall["lru_ba"][l] = mred[8]
        gsmall["lru_bx"][l] = mred[9]
        gsmall["lru_lambda"][l] = mred[10]
        gsmall["lru_wa"][l] = _diag_heads(dwab[:, :, :LRU_BLOCK])
        gsmall["lru_wx"][l] = _diag_heads(dwab[:, :, LRU_BLOCK:])
        gsmall["gate_bias"][l] = dgbias
    gsmall = {k: jnp.stack(v) for k, v in gsmall.items()}
    gsmall["final_g"] = d_final_g
    return loss_row, dx, gsmall


def _small_allreduce(buf):
    R, C = buf.shape
    n_dev = 8
    rp = R // n_dev
    rel = [(k >> 2 & 1, k >> 1 & 1, k & 1) for k in range(1, n_dev)]

    def body(in_ref, out_ref, recv, s1, r1, s2, r2):
        x, y, c, _ = _place()
        flip = lambda v, bit: 1 - v if bit else v
        peers = [(flip(x, kx), flip(y, ky), flip(c, kc)) for kx, ky, kc in rel]
        dev = lambda p: 4 * p[0] + 2 * p[1] + p[2]
        part = lambda ref, d: ref.at[pl.ds(pl.multiple_of(d * rp, SUBLANES), rp), :]
        me = dev((x, y, c))

        def scatter(k, src_dev, to):
            return pltpu.make_async_remote_copy(src_ref=part(in_ref, dev(to)), dst_ref=recv.at[src_dev], send_sem=s1.at[k],
                                                recv_sem=r1.at[k], device_id=to, device_id_type=MESH)

        def gather(k, src_dev, to):
            return pltpu.make_async_remote_copy(src_ref=part(out_ref, src_dev), dst_ref=part(out_ref, src_dev), send_sem=s2.at[k],
                                                recv_sem=r2.at[k], device_id=to, device_id_type=MESH)

        first = [scatter(k, me, p) for k, p in enumerate(peers)]
        for cp in first:
            cp.start()
        recv[me] = part(in_ref, me)[...]
        for k, p in enumerate(peers):
            scatter(k, dev(p), (x, y, c)).wait_recv()
        total = recv[0]
        for d in range(1, n_dev):
            total = total + recv[d]
        part(out_ref, me)[...] = total
        second = [gather(k, me, p) for k, p in enumerate(peers)]
        for cp in second:
            cp.start()
        for k, p in enumerate(peers):
            gather(k, dev(p), (x, y, c)).wait_recv()
        for cp in first + second:
            cp.wait_send()

    dma = pltpu.SemaphoreType.DMA
    vm = pl.BlockSpec(memory_space=pltpu.VMEM)
    return pl.pallas_call(
        body, name="small_allreduce", out_shape=jax.ShapeDtypeStruct((R, C), buf.dtype),
        in_specs=[vm], out_specs=vm,
        scratch_shapes=[pltpu.VMEM((n_dev, rp, C), buf.dtype), dma((n_dev - 1,)), dma((n_dev - 1,)), dma((n_dev - 1,)), dma((n_dev - 1,))],
    )(buf)


ELEMENTWISE_BLOCK_BYTES = 2 * 1024 * 1024


def _row_block(k, n):
    best = None
    for b in range(16, k + 1, 16):
        if k % b == 0 and b * n * 4 <= ELEMENTWISE_BLOCK_BYTES:
            best = b
    return best or k


def _add_halves(g, recv, place_arr, name):
    nj, hk, N = recv.shape
    bk = _row_block(hk, N)
    nb = hk // bk

    def body(k_ref, g_ref, r_ref, o_ref, ob_ref):
        s = g_ref[...] + r_ref[...].astype(F32)
        ob_ref[...] = s.astype(BF16)

        @pl.when(pl.program_id(1) == k_ref[0])
        def _():
            o_ref[...] = s

    blk = pl.BlockSpec((None, bk, N), lambda i, j, k_ref: (j, i, 0))
    grid_spec = pltpu.PrefetchScalarGridSpec(
        num_scalar_prefetch=1, grid=(nb, nj),
        in_specs=[pl.BlockSpec((None, bk, N), lambda i, j, k_ref: (j, k_ref[1] * nb + i, 0)), blk],
        out_specs=[pl.BlockSpec((bk, N), lambda i, j, k_ref: (i, 0)), blk])
    return pl.pallas_call(
        body, name=name, grid_spec=grid_spec,
        out_shape=[jax.ShapeDtypeStruct((hk, N), F32), jax.ShapeDtypeStruct((nj, hk, N), BF16)],
        compiler_params=_params("parallel", "arbitrary"),
    )(place_arr, g, recv)


def _add_chips(pc, recv, place_arr, layer, n_layers, prev, name):
    hk, N = pc.shape
    bk = _row_block(hk, N)
    nb = hk // bk

    def body(k_ref, p_ref, r0_ref, r1_ref, r2_ref, *rest):
        o_ref = rest[-1]
        o_ref[...] = ((p_ref[...] + r0_ref[...].astype(F32)) + r1_ref[...].astype(F32)) + r2_ref[...].astype(F32)

    rspec = lambda j: pl.BlockSpec((None, bk, N), lambda i, k_ref: (j, i, 0))
    in_specs = [pl.BlockSpec((bk, N), lambda i, k_ref: (i, 0)), rspec(0), rspec(1), rspec(2)]
    operands = [pc, recv, recv, recv]
    aliases = {}
    if prev is not None:
        in_specs.append(ANY)
        operands.append(prev)
        aliases = {5: 0}
    grid_spec = pltpu.PrefetchScalarGridSpec(
        num_scalar_prefetch=1, grid=(nb,), in_specs=in_specs,
        out_specs=pl.BlockSpec((None, bk, N), lambda i, k_ref: (layer, k_ref[1] * nb + i, 0)))
    return pl.pallas_call(
        body, name=name, grid_spec=grid_spec, out_shape=jax.ShapeDtypeStruct((n_layers, 2 * hk, N), F32),
        input_output_aliases=aliases, compiler_params=_params("parallel"),
    )(place_arr, *operands)


def _adamw_math(w, g, m, v):
    m = ADAM_B1 * m + (1.0 - ADAM_B1) * g
    v = ADAM_B2 * v + (1.0 - ADAM_B2) * (g * g)
    m_hat = m / (1.0 - ADAM_B1 ** ADAM_STEP)
    v_hat = v / (1.0 - ADAM_B2 ** ADAM_STEP)
    delta = -ADAM_LR * (m_hat / (jnp.sqrt(v_hat) + ADAM_EPS) + ADAM_WD * w)
    return delta, m, v


def _adamw(w, g, m, v, name):
    L, K, N = w.shape
    bk = _row_block(K, N)

    def body(w_ref, g_ref, m_ref, v_ref, d_ref, nm_ref, nv_ref, go_ref):
        g = g_ref[...]
        d_ref[...], nm_ref[...], nv_ref[...] = _adamw_math(w_ref[...], g, m_ref[...], v_ref[...])
        go_ref[...] = g

    blk = pl.BlockSpec((None, bk, N), lambda l, i: (l, i, 0))
    sds = jax.ShapeDtypeStruct((L, K, N), F32)
    return pl.pallas_call(
        body, name=name, grid=(L, K // bk), in_specs=[blk] * 4, out_specs=[blk] * 4, out_shape=[sds] * 4,
        compiler_params=_params("parallel", "parallel"),
    )(w, g, m, v)


def _adamw_small(ws, gs, ms, vs):
    n = len(ws)

    def body(*refs):
        w, g, m, v, d, nm, nv = (refs[k * n:(k + 1) * n] for k in range(7))
        for k in range(n):
            d[k][...], nm[k][...], nv[k][...] = _adamw_math(w[k][...], g[k][...], m[k][...], v[k][...])

    sds = [jax.ShapeDtypeStruct(a.shape, F32) for a in ws]
    out = pl.pallas_call(body, name="adamw_small", out_shape=sds * 3)(*ws, *gs, *ms, *vs)
    return out[:n], out[n:2 * n], out[2 * n:]


def _cast_bf16(w, layer, place_arr, name):
    _, K, N = w.shape
    bk = _row_block(K, N)

    def body(k_ref, w_ref, o_ref):
        o_ref[...] = w_ref[...].astype(BF16)

    grid_spec = pltpu.PrefetchScalarGridSpec(
        num_scalar_prefetch=1, grid=(K // bk,),
        in_specs=[pl.BlockSpec((None, bk, N), lambda i, k_ref: (layer, i, 0))],
        out_specs=pl.BlockSpec((None, bk, N), lambda i, k_ref: (k_ref[0], i, 0)))
    return pl.pallas_call(
        body, name=name, grid_spec=grid_spec, out_shape=jax.ShapeDtypeStruct((N_CHIP, K, N), BF16),
        compiler_params=_params("parallel"),
    )(place_arr, w)


BIG = ("w_in", "w_out_a", "w_out_b", "w_o", "w_ffn_gate", "w_ffn_up", "w_ffn_down")
BIG_KEY = dict(w_in="win", w_out_a="woa", w_out_b="wob", w_o="wo", w_ffn_gate="wg", w_ffn_up="wu", w_ffn_down="wd")
SHARDED_SMALL = ("conv_a_w", "conv_b_w", "gate_bias")
REPLICATED = ("ln1_g", "conv_b_b", "lru_wa", "lru_ba", "lru_wx", "lru_bx", "lru_lambda", "ln2_g", "final_g")
WEIGHTS = ("ln1_g", "w_in", "conv_a_w", "conv_b_w", "conv_b_b", "lru_wa", "lru_ba", "lru_wx", "lru_bx", "lru_lambda",
           "w_out_a", "w_out_b", "gate_bias", "w_o", "ln2_g", "w_ffn_gate", "w_ffn_up", "w_ffn_down", "final_g")
LANES = 1024


def _pack_rows(arrays, row_multiple):
    flat = jnp.concatenate([a.reshape(-1) for a in arrays])
    rows = -(-flat.shape[0] // LANES)
    rows = -(-rows // row_multiple) * row_multiple
    flat = jnp.pad(flat, (0, rows * LANES - flat.shape[0]))
    return flat.reshape(rows, LANES)


def _unpack_rows(buf, shapes):
    flat = buf.reshape(-1)
    out, off = [], 0
    for s in shapes:
        n = 1
        for d in s:
            n *= d
        out.append(flat[off:off + n].reshape(s))
        off += n
    return out


OUT_KEYS = ("wo", "woa", "wob")
FFN_KEYS = ("wg", "wu", "wd")


def _items(keys, layer):
    return [(k, layer) for k in keys]


CARRY = {
    "rms_inproj_0": [("gather_ici", _items(OUT_KEYS + ("wg", "wu"), 0))],
    "mixer_fwd_0": [("gather_d2d", _items(OUT_KEYS + ("wg", "wu"), 0)), ("gather_ici", [("wd", 0), ("win", 1)])],
    "merge_fwd_0": [("gather_d2d", [("wd", 0)])],
    "ffn_fwd_0": [("gather_d2d", [("win", 1)]), ("gather_ici", _items(OUT_KEYS + FFN_KEYS, 1))],
    "rms_inproj_1": [("gather_d2d", _items(OUT_KEYS + FFN_KEYS, 1))],
    "merge_bwd_1": [("halves", _items(FFN_KEYS, 1))],
    "mixer_bwd_1": [("chips", _items(FFN_KEYS, 1)), ("halves", _items(OUT_KEYS, 1))],
    "inproj_bwd_1": [("chips", _items(OUT_KEYS, 1)), ("share", _items(FFN_KEYS, 1))],
    "ffn_bwd_0": [("halves", [("win", 1)]), ("share", _items(OUT_KEYS, 1))],
    "wgrad_ffn_gate_up_0": [("chips", [("win", 1)])],
    "merge_bwd_0": [("halves", _items(FFN_KEYS, 0)), ("share", [("win", 1)])],
    "mixer_bwd_0": [("chips", _items(FFN_KEYS, 0)), ("halves", _items(OUT_KEYS, 0))],
    "wgrad_w_in_0": [("chips", _items(OUT_KEYS, 0)), ("share", _items(FFN_KEYS, 0))],
}
AFTER = {
    "merge_bwd_1": [("add_halves", _items(FFN_KEYS, 1))],
    "mixer_bwd_1": [("add_chips", _items(FFN_KEYS, 1)), ("add_halves", _items(OUT_KEYS, 1))],
    "inproj_bwd_1": [("add_chips", _items(OUT_KEYS, 1))],
    "ffn_bwd_0": [("add_halves", [("win", 1)])],
    "wgrad_ffn_gate_up_0": [("add_chips", [("win", 1)])],
    "merge_bwd_0": [("add_halves", _items(FFN_KEYS, 0))],
    "mixer_bwd_0": [("add_chips", _items(FFN_KEYS, 0)), ("add_halves", _items(OUT_KEYS, 0))],
    "wgrad_w_in_0": [("add_chips", _items(OUT_KEYS, 0)),
                     ("run", ("reduce_halves_w_in_0", [("halves", [("win", 0)]), ("share", _items(OUT_KEYS, 0))])),
                     ("add_halves", [("win", 0)]),
                     ("split_start", ("win", 0))],
    "inproj_bwd_0": [("split_wait", ("win", 0)), ("add_chips", [("win", 0)]),
                     ("run", ("reduce_share_w_in_0", [("share", [("win", 0)])]))],
}


SPLIT_BESIDE = {"wgrad_w_in_0": "inproj_bwd_0"}


class _Schedule:
    def __init__(self, slots, place_arr, n_layers):
        self.W = slots
        self.place, self.L = place_arr, n_layers
        self.g32, self.g16 = {}, {}
        self.from_sibling, self.chip_sum, self.chip_sum16, self.from_chips = {}, {}, {}, {}
        self.reduced = {}
        self.split, self.tokens = {}, {}

    def stage(self, comm, kind, items):
        bf = lambda shape: jax.ShapeDtypeStruct(shape, BF16)
        for it in items:
            if kind == "gather_ici":
                comm.add(_gather_ici, 3, io=[(self.W, it)])
            elif kind == "gather_d2d":
                comm.add(_gather_d2d, 3, io=[(self.W, it)])
            elif kind == "halves":
                nj, K, N = self.g16[it].shape
                comm.add(_reduce_halves, 1, ro=[self.g16[it]], nw=[(self.from_sibling, it, bf((nj, K // 2, N)))])
            elif kind == "chips":
                _, hk, N = self.chip_sum16[it].shape
                comm.add(_reduce_chips, 3, ro=[self.chip_sum16[it]], nw=[(self.from_chips, it, bf((3, hk, N)))])
            elif kind == "share":
                comm.add(_reduce_share(it[1]), 1, io=[(self.reduced, it[0])])
        return comm

    def carry(self, name):
        comm = _Carried()
        for kind, items in CARRY.get(name, ()):
            self.stage(comm, kind, items)
        return comm

    def run(self, name, rounds):
        _run_comm([self.stage(_Carried(), kind, items) for kind, items in rounds], name)

    def grad(self, key, layer, f32, b16):
        by_chip = lambda g: g.reshape(N_CHIP, -1, g.shape[-1])
        self.g32[key, layer], self.g16[key, layer] = by_chip(f32), by_chip(b16)

    def add(self, kind, items):
        for key, layer in items:
            it = (key, layer)
            if kind == "add_halves":
                self.chip_sum[it], self.chip_sum16[it] = _add_halves(self.g32[it], self.from_sibling[it], self.place,
                                                                    f"add_halves_{key}_{layer}")
            else:
                self.reduced[key] = _add_chips(self.chip_sum[it], self.from_chips[it], self.place, layer, self.L,
                                               self.reduced.get(key), f"add_chips_{key}_{layer}")

    def between_inproj(self, layer, p, small):
        if layer != 0:
            return
        send_sems, recv_sems, before = self.first_gather
        self.W["win", 0] = _gather_first_wait(send_sems, recv_sems, self.W["win", 0], before + [p])
        w = self.sharded_small
        small_shard = jnp.concatenate([w[n] for n in SHARDED_SMALL], axis=1)
        got = {}
        second = self.stage(_Carried(), "gather_d2d", [("win", 0)])
        second.add(_gather_small, 3, ro=[small_shard], nw=[(got, "small", jax.ShapeDtypeStruct((3,) + small_shard.shape, F32))])
        _run_comm([second], "gather_first_d2d")
        xi, yi = lax.axis_index("x"), lax.axis_index("y")
        small_g = jnp.zeros((N_CHIP,) + small_shard.shape, F32)
        small_g = lax.dynamic_update_index_in_dim(small_g, small_shard, _chip_id(xi, yi), 0)
        for j, (cx, cy) in enumerate([(1 - xi, yi), (xi, 1 - yi), (1 - xi, 1 - yi)]):
            small_g = lax.dynamic_update_index_in_dim(small_g, got["small"][j], _chip_id(cx, cy), 0)
        n_layers, rows, dc = small_shard.shape
        small_full = jnp.transpose(small_g, (1, 2, 0, 3)).reshape(n_layers, rows, N_CHIP * dc)
        off = 0
        for n in SHARDED_SMALL:
            k = w[n].shape[1]
            small[n] = small_full[:, off:off + k]
            off += k

    def tie(self, name, operand):
        token = self.tokens.pop(name, None)
        return operand if token is None else operand + token[0, 0]

    def after(self, name, result=None):
        for kind, items in AFTER.get(name, ()):
            if kind == "run":
                self.run(*items)
            elif kind == "split_start":
                send_sems, recv_sems, src, land, token = _scatter_start(self.chip_sum16[items])
                self.split[items] = (send_sems, recv_sems, src, land)
                self.tokens[SPLIT_BESIDE[name]] = token
            elif kind == "split_wait":
                self.from_chips[items] = _scatter_wait(*self.split.pop(items), result)
            else:
                self.add(kind, items)


def _step(w, m, v, x, target):
    xi, yi, ci = lax.axis_index("x"), lax.axis_index("y"), lax.axis_index("c")
    chip = _chip_id(xi, yi)
    place_arr = jnp.stack([chip, ci]).astype(jnp.int32)
    L = w["ln1_g"].shape[0]
    assert L == 2
    D = x.shape[1]
    dc = D // N_CHIP

    stored = lambda n, a: jnp.swapaxes(a, 1, 2) if n in ("w_ffn_gate", "w_ffn_up") else a
    slots = {("win", 0): _cast_bf16(w["w_in"], 0, place_arr, "cast_w_in_0")}
    send_sems, recv_sems, slots["win", 0], token = _gather_first_start(slots["win", 0])
    place_after = place_arr + token[0, 0]
    for n in BIG:
        for l in range(L):
            if (BIG_KEY[n], l) not in slots:
                slots[BIG_KEY[n], l] = _cast_bf16(stored(n, w[n]), l, place_after, f"cast_{n}_{l}")
    sched = _Schedule(slots, place_arr, L)
    sched.tokens["rms_inproj_own_0"] = token
    sched.first_gather = (send_sems, recv_sems, [v for k, v in slots.items() if k != ("win", 0)])
    sched.sharded_small = {n: w[n] for n in SHARDED_SMALL}
    small = {n: w[n] for n in REPLICATED}

    loss_row, grad_x, gsmall = _local_step(x, target, sched.W, small, _tiles(x.shape[0]), sched)

    grads = {}

    order = [n for n in WEIGHTS if n not in BIG]
    packed = _pack_rows([gsmall[n] for n in order] + [loss_row], 8 * SUBLANES)
    summed = _small_allreduce(packed)
    parts = _unpack_rows(summed, [gsmall[n].shape for n in order] + [loss_row.shape])
    loss = jnp.sum(parts[-1])
    for n, g in zip(order, parts[:-1]):
        grads[n] = lax.dynamic_slice_in_dim(g, chip * dc, dc, axis=2) if n in SHARDED_SMALL else g

    delta, new_m, new_v = {}, {}, {}
    for n in BIG:
        d, nm, nv, g = _adamw(stored(n, w[n]), sched.reduced[BIG_KEY[n]], stored(n, m[n]), stored(n, v[n]), f"adamw_{n}")
        delta[n], new_m[n], new_v[n], grads[n] = stored(n, d), stored(n, nm), stored(n, nv), stored(n, g)
    for d, arrays in zip((delta, new_m, new_v), _adamw_small(*([d[n] for n in order] for d in (w, grads, m, v)))):
        d.update(zip(order, arrays))
    return loss, grad_x, grads, delta, new_m, new_v


def kernel(x, ln1_g, w_in, conv_a_w, conv_b_w, conv_b_b, lru_wa, lru_ba, lru_wx, lru_bx, lru_lambda, w_out_a, w_out_b, gate_bias, w_o, ln2_g, w_ffn_gate, w_ffn_up, w_ffn_down, final_g, loss_target, m_ln1_g, m_w_in, m_conv_a_w, m_conv_b_w, m_conv_b_b, m_lru_wa, m_lru_ba, m_lru_wx, m_lru_bx, m_lru_lambda, m_w_out_a, m_w_out_b, m_gate_bias, m_w_o, m_ln2_g, m_w_ffn_gate, m_w_ffn_up, m_w_ffn_down, m_final_g, v_ln1_g, v_w_in, v_conv_a_w, v_conv_b_w, v_conv_b_b, v_lru_wa, v_lru_ba, v_lru_wx, v_lru_bx, v_lru_lambda, v_w_out_a, v_w_out_b, v_gate_bias, v_w_o, v_ln2_g, v_w_ffn_gate, v_w_ffn_up, v_w_ffn_down, v_final_g):
    w = dict(ln1_g=ln1_g, w_in=w_in, conv_a_w=conv_a_w, conv_b_w=conv_b_w, conv_b_b=conv_b_b, lru_wa=lru_wa, lru_ba=lru_ba,
             lru_wx=lru_wx, lru_bx=lru_bx, lru_lambda=lru_lambda, w_out_a=w_out_a, w_out_b=w_out_b, gate_bias=gate_bias, w_o=w_o,
             ln2_g=ln2_g, w_ffn_gate=w_ffn_gate, w_ffn_up=w_ffn_up, w_ffn_down=w_ffn_down, final_g=final_g)
    m = dict(ln1_g=m_ln1_g, w_in=m_w_in, conv_a_w=m_conv_a_w, conv_b_w=m_conv_b_w, conv_b_b=m_conv_b_b, lru_wa=m_lru_wa,
             lru_ba=m_lru_ba, lru_wx=m_lru_wx, lru_bx=m_lru_bx, lru_lambda=m_lru_lambda, w_out_a=m_w_out_a, w_out_b=m_w_out_b,
             gate_bias=m_gate_bias, w_o=m_w_o, ln2_g=m_ln2_g, w_ffn_gate=m_w_ffn_gate, w_ffn_up=m_w_ffn_up,
             w_ffn_down=m_w_ffn_down, final_g=m_final_g)
    v = dict(ln1_g=v_ln1_g, w_in=v_w_in, conv_a_w=v_conv_a_w, conv_b_w=v_conv_b_w, conv_b_b=v_conv_b_b, lru_wa=v_lru_wa,
             lru_ba=v_lru_ba, lru_wx=v_lru_wx, lru_bx=v_lru_bx, lru_lambda=v_lru_lambda, w_out_a=v_w_out_a, w_out_b=v_w_out_b,
             gate_bias=v_gate_bias, w_o=v_w_o, ln2_g=v_ln2_g, w_ffn_gate=v_w_ffn_gate, w_ffn_up=v_w_ffn_up,
             w_ffn_down=v_w_ffn_down, final_g=v_final_g)
    loss, grad_x, grads, delta, new_m, new_v = _step(w, m, v, x[0], loss_target[0])
    return (loss, grad_x[None], *[grads[n] for n in WEIGHTS], *[delta[n] for n in WEIGHTS],
            *[new_m[n] for n in WEIGHTS], *[new_v[n] for n in WEIGHTS])
```
